```python
import math
import jax
import jax.numpy as jnp
from jax import lax
import numpy as np

D_MODEL = 1024
BATCH = 4
SEQ = 4096
DEPTH = 2
DEC_BATCH = 128
DEC_SEQ = 4
PAST_LEN = 8192
PAGE_SIZE = 128

N_EVEN = (DEPTH + 1) // 2
N_ODD = DEPTH // 2
FFN_HIDDEN = 2816
MIX_HALF = D_MODEL // 2
GLA_HEADS = 4
GLA_DK = 64
GLA_DV = MIX_HALF // GLA_HEADS
GLA_RANK = 16
GLA_TAU = 16.0
RET_HEADS = 4
RET_DK = 64
RET_DV = MIX_HALF // RET_HEADS
RET_THETA = 10000.0
LA_CHUNK = 64
S5_WIDTH = MIX_HALF
S5_GROUP = 16
S5_GROUPS = S5_WIDTH // S5_GROUP
S5_STATE = 64
S5_STEP_MIN = 0.001
S5_STEP_MAX = 0.1
SWA_HD = 64
SWA_HEADS = MIX_HALF // SWA_HD
SWA_KV_HEADS = 2
SWA_WINDOW = 128
ROPE_THETA = 500000.0
N_MEM = 256
MEM_HEADS = 4
MEM_HD = D_MODEL // MEM_HEADS
NORM_EPS = 1e-6
EVEN_SIZES = (GLA_HEADS * GLA_DK, GLA_HEADS * GLA_DK, GLA_HEADS * GLA_DV, GLA_HEADS * GLA_DV, GLA_RANK,
              RET_HEADS * RET_DK, RET_HEADS * RET_DK, RET_HEADS * RET_DV, RET_HEADS * RET_DV)
ODD_SIZES = (S5_WIDTH, SWA_HEADS * SWA_HD, SWA_KV_HEADS * SWA_HD, SWA_KV_HEADS * SWA_HD)
EVEN_IN = sum(EVEN_SIZES)
ODD_IN = sum(ODD_SIZES)

kernel_name = 'hybrid_gla_retnet_s5_swa_macaron_step'


def rms_norm(x, gain=None):
    xf = x.astype(jnp.float32)
    y = xf * lax.rsqrt(jnp.mean(xf * xf, axis=-1, keepdims=True) + NORM_EPS)
    if gain is not None:
        y = y * gain.astype(jnp.float32)
    return y.astype(x.dtype)


def split_cols(t, sizes):
    return jnp.split(t, np.cumsum(sizes)[:-1].tolist(), axis=-1)


def swiglu(h, w_gate, w_up, w_down):
    return (jax.nn.silu(h @ w_gate) * (h @ w_up)) @ w_down


def partial_rotary(x, pos):
    rd = x.shape[-1] // 4
    half = rd // 2
    inv = 1.0 / (ROPE_THETA ** (jnp.arange(half, dtype=jnp.float32) * 2.0 / rd))
    ang = pos[:, None] * inv[None, :]
    cos, sin = jnp.cos(ang)[:, None, :], jnp.sin(ang)[:, None, :]
    xf = x.astype(jnp.float32)
    x1, x2, rest = xf[..., :half], xf[..., half:rd], xf[..., rd:]
    return jnp.concatenate([x1 * cos - x2 * sin, x2 * cos + x1 * sin, rest], axis=-1).astype(x.dtype)


def retention_rotate(x, pos):
    dk = x.shape[-1]
    inv = 1.0 / (RET_THETA ** jnp.linspace(0.0, 1.0, dk // 2, dtype=jnp.float32))
    ang = pos[:, None] * inv[None, :]
    cos, sin = jnp.cos(ang)[:, None, :], jnp.sin(ang)[:, None, :]
    xf = x.astype(jnp.float32)
    xe, xo = xf[..., 0::2], xf[..., 1::2]
    return jnp.stack([xe * cos - xo * sin, xo * cos + xe * sin], axis=-1).reshape(x.shape).astype(x.dtype)


def chunked_gated_linear_attention(q, k, v, log_a, s0):
    f32 = jnp.float32
    bsz, seq, n_heads, _ = q.shape
    dv = v.shape[-1]
    chunk = math.gcd(seq, LA_CHUNK)
    n_chunks = seq // chunk

    def blocks(t):
        return t.astype(f32).reshape(bsz, n_chunks, chunk, n_heads, t.shape[-1]).transpose(1, 0, 3, 2, 4)

    q, k, v, log_a = blocks(q), blocks(k), blocks(v), blocks(log_a)
    cum = jnp.cumsum(log_a, axis=3)
    cum_last = cum[:, :, :, -1:, :]
    q_dec = q * jnp.exp(cum)
    k_inv = k * jnp.exp(-cum)
    k_dec = k * jnp.exp(cum_last - cum)
    causal = jnp.tril(jnp.ones((chunk, chunk), dtype=bool))
    scores = jnp.where(causal, jnp.einsum('nbhtd,nbhsd->nbhts', q_dec, k_inv), 0.0)
    o_intra = jnp.einsum('nbhts,nbhsv->nbhtv', scores, v)

    def step(state, inp):
        q_c, k_c, v_c, last_c = inp
        o_c = jnp.einsum('bhtd,bhdv->bhtv', q_c, state)
        state = state * jnp.exp(last_c[:, :, 0, :, None]) + jnp.einsum('bhtd,bhtv->bhdv', k_c, v_c)
        return state, o_c

    s_final, o_inter = lax.scan(step, s0.astype(f32), (q_dec, k_dec, v, cum_last))
    out = (o_intra + o_inter).transpose(1, 0, 3, 2, 4).reshape(bsz, seq, n_heads, dv)
    return out, s_final


def gla_retention_mixer(h, pos, gla_s0, ret_s0, w, i):
    f32 = jnp.float32
    bsz, seq, _ = h.shape
    gq, gk, gv, gg, ga, rq, rk, rv, rg = split_cols(h @ w['even_w_in'][i], EVEN_SIZES)

    def heads(t, n):
        return t.reshape(bsz, seq, n, -1)

    log_a = jax.nn.log_sigmoid((ga @ w['gla_w_gate'][i] + w['gla_b_gate'][i]).astype(f32)) / GLA_TAU
    o_g, s_g = chunked_gated_linear_attention(heads(gq, GLA_HEADS) * GLA_DK ** -0.5, heads(gk, GLA_HEADS),
                                              heads(gv, GLA_HEADS), heads(log_a, GLA_HEADS), gla_s0)
    o_g = rms_norm(o_g, w['gla_out_norm'][i]) * jax.nn.silu(heads(gg, GLA_HEADS).astype(f32))
    log_decay = jnp.log(1.0 - 2.0 ** (-5.0 - jnp.arange(RET_HEADS, dtype=f32)))
    q_r = retention_rotate(heads(rq, RET_HEADS), pos)
    k_r = retention_rotate(heads(rk, RET_HEADS), pos) * RET_DK ** -0.5
    log_g = jnp.broadcast_to(log_decay[:, None], (bsz, seq, RET_HEADS, RET_DK))
    o_r, s_r = chunked_gated_linear_attention(q_r, k_r, heads(rv, RET_HEADS), log_g, ret_s0)
    o_r = rms_norm(o_r) * jax.nn.silu(heads(rg, RET_HEADS).astype(f32))
    mixed = jnp.concatenate([o_g.reshape(bsz, seq, -1), o_r.reshape(bsz, seq, -1)], axis=-1)
    return mixed.astype(h.dtype) @ w['even_w_out'][i], s_g, s_r


def complex_affine_combine(e1, e2):
    a1r, a1i, b1r, b1i = e1
    a2r, a2i, b2r, b2i = e2
    return (a2r * a1r - a2i * a1i, a2r * a1i + a2i * a1r,
            a2r * b1r - a2i * b1i + b2r, a2r * b1i + a2i * b1r + b2i)


def s5_scan(u, s_re0, s_im0, w, i):
    f32 = jnp.float32
    bsz, seq, _ = u.shape
    a_re, a_im = w['s5_a_re'][i].astype(f32), w['s5_a_im'][i].astype(f32)
    step = jnp.exp(w['s5_log_step'][i].astype(f32))[:, None]
    mag = jnp.exp(a_re * step)
    ab_re, ab_im = mag * jnp.cos(a_im * step), mag * jnp.sin(a_im * step)
    den = a_re * a_re + a_im * a_im
    coef_re = ((ab_re - 1.0) * a_re + ab_im * a_im) / den
    coef_im = (ab_im * a_re - (ab_re - 1.0) * a_im) / den
    b_re, b_im = w['s5_b_re'][i].astype(f32), w['s5_b_im'][i].astype(f32)
    bb_re = coef_re[..., None] * b_re - coef_im[..., None] * b_im
    bb_im = coef_re[..., None] * b_im + coef_im[..., None] * b_re
    uf = u.astype(f32).reshape(bsz, seq, S5_GROUPS, S5_GROUP)
    bu_re = jnp.einsum('blgi,gni->blgn', uf, bb_re)
    bu_im = jnp.einsum('blgi,gni->blgn', uf, bb_im)
    x_re0, x_im0 = s_re0.astype(f32), s_im0.astype(f32)
    bu_re = bu_re.at[:, 0].add(ab_re * x_re0 - ab_im * x_im0)
    bu_im = bu_im.at[:, 0].add(ab_re * x_im0 + ab_im * x_re0)
    a_re_t = jnp.broadcast_to(ab_re, bu_re.shape)
    a_im_t = jnp.broadcast_to(ab_im, bu_im.shape)
    _, _, h_re, h_im = lax.associative_scan(complex_affine_combine, (a_re_t, a_im_t, bu_re, bu_im), axis=1)
    y = (jnp.einsum('blgn,gin->blgi', h_re, w['s5_c_re'][i].astype(f32))
         - jnp.einsum('blgn,gin->blgi', h_im, w['s5_c_im'][i].astype(f32)))
    y = y.reshape(bsz, seq, S5_WIDTH) + w['s5_d'][i].astype(f32) * uf.reshape(bsz, seq, S5_WIDTH)
    z = jax.nn.gelu(y)
    out = z * jax.nn.sigmoid(z @ w['s5_w_glu'][i].astype(f32) + w['s5_b_glu'][i].astype(f32))
    return out, h_re[:, -1], h_im[:, -1]


def sliding_window_sink_attention(q, k, v, prev_k, prev_v, sinks, pos0):
    f32 = jnp.float32
    bsz, seq, n_q, hd = q.shape
    n_kv = k.shape[2]
    grp = n_q // n_kv
    past = prev_k.shape[1]
    win = SWA_WINDOW
    pad = jnp.zeros((bsz, win - past, n_kv, hd), k.dtype)
    k_ext = jnp.concatenate([pad, prev_k.astype(k.dtype), k], axis=1)
    v_ext = jnp.concatenate([pad, prev_v.astype(v.dtype), v], axis=1)
    qb = math.gcd(seq, win)
    nb = seq // qb
    idx = np.arange(nb)[:, None] * qb + np.arange(qb + win)[None, :]
    kb = jnp.take(k_ext, idx, axis=1)
    vb = jnp.take(v_ext, idx, axis=1)
    qr = q.reshape(bsz, nb, qb, n_kv, grp, hd)
    s = jnp.einsum('bnqkgd,bnskd->bnkgqs', qr, kb).astype(f32) * hd ** -0.5
    q_pos = pos0 + np.arange(seq).reshape(nb, qb)
    k_pos = pos0 - win + idx
    kp, qp = k_pos[:, None, :], q_pos[:, :, None]
    mask = (kp <= qp) & (kp > qp - win) & (kp >= pos0 - past)
    s = jnp.where(mask[None, :, None, None], s, -jnp.inf)
    sink = sinks.astype(f32).reshape(n_kv, grp)[None, None, :, :, None, None]
    m = jnp.maximum(jnp.max(s, axis=-1, keepdims=True), sink)
    e = jnp.exp(s - m)
    p = e / (jnp.sum(e, axis=-1, keepdims=True) + jnp.exp(sink - m))
    o = jnp.einsum('bnkgqs,bnskd->bnqkgd', p.astype(v.dtype), vb).reshape(bsz, seq, n_q * hd)
    keep = min(win, past + seq)
    return o, k_ext[:, -keep:], v_ext[:, -keep:]


def s5_swa_mixer(h, pos, pos0, s_re0, s_im0, prev_k, prev_v, w, i):
    bsz, seq, _ = h.shape
    u, q, k, v = split_cols(h @ w['odd_w_in'][i], ODD_SIZES)
    c_out, s_re, s_im = s5_scan(u, s_re0, s_im0, w, i)
    q = partial_rotary(rms_norm(q.reshape(bsz, seq, SWA_HEADS, SWA_HD), w['swa_q_norm'][i]), pos)
    k = partial_rotary(rms_norm(k.reshape(bsz, seq, SWA_KV_HEADS, SWA_HD), w['swa_k_norm'][i]), pos)
    v = v.reshape(bsz, seq, SWA_KV_HEADS, SWA_HD)
    d_out, buf_k, buf_v = sliding_window_sink_attention(q, k, v, prev_k, prev_v, w['swa_sinks'][i], pos0)
    mixed = jnp.concatenate([c_out.astype(h.dtype), d_out.astype(h.dtype)], axis=-1)
    return mixed @ w['odd_w_out'][i], s_re, s_im, buf_k, buf_v


def memory_kv(mem, w, layer):
    bsz, n_mem, _ = mem.shape
    mn = rms_norm(mem, w['mem_m_norm'][layer])
    k = rms_norm((mn @ w['mem_w_k'][layer]).reshape(bsz, n_mem, MEM_HEADS, MEM_HD), w['mem_k_norm'][layer])
    v = (mn @ w['mem_w_v'][layer]).reshape(bsz, n_mem, MEM_HEADS, MEM_HD)
    return k, v


def memory_attend(h, mem_k, mem_v, w, layer):
    bsz, seq, _ = h.shape
    q = rms_norm((h @ w['mem_w_q'][layer]).reshape(bsz, seq, MEM_HEADS, MEM_HD), w['mem_q_norm'][layer])
    s = jnp.einsum('blhd,bmhd->bhlm', q, mem_k.astype(q.dtype)).astype(jnp.float32) * MEM_HD ** -0.5
    p = jax.nn.softmax(s, axis=-1).astype(mem_v.dtype)
    o = jnp.einsum('bhlm,bmhd->blhd', p, mem_v).reshape(bsz, seq, D_MODEL)
    return o.astype(h.dtype) @ w['mem_w_o'][layer]


def run_trunk(x, pos0, gla_s, ret_s, s5_re, s5_im, swa_k, swa_v, mem_k, mem_v, w):
    pos = pos0 + jnp.arange(x.shape[1], dtype=jnp.float32)
    new_gla, new_ret, new_s5_re, new_s5_im, new_swa_k, new_swa_v = [], [], [], [], [], []
    for layer in range(DEPTH):
        x = x + 0.5 * swiglu(rms_norm(x, w['ffn1_norm'][layer]), w['ffn1_w_gate'][layer],
                             w['ffn1_w_up'][layer], w['ffn1_w_down'][layer])
        h = rms_norm(x, w['mix_norm'][layer])
        i = layer // 2
        if layer % 2 == 0:
            mix, g_s, r_s = gla_retention_mixer(h, pos, gla_s[i], ret_s[i], w, i)
            new_gla.append(g_s)
            new_ret.append(r_s)
        else:
            mix, sr, si, kb, vb = s5_swa_mixer(h, pos, pos0, s5_re[i], s5_im[i], swa_k[i], swa_v[i], w, i)
            new_s5_re.append(sr)
            new_s5_im.append(si)
            new_swa_k.append(kb)
            new_swa_v.append(vb)
        x = x + mix
        x = x + memory_attend(rms_norm(x, w['mem_x_norm'][layer]), mem_k[layer], mem_v[layer], w, layer)
        x = x + 0.5 * swiglu(rms_norm(x, w['ffn2_norm'][layer]), w['ffn2_w_gate'][layer],
                             w['ffn2_w_up'][layer], w['ffn2_w_down'][layer])
    return (x, jnp.stack(new_gla), jnp.stack(new_ret), jnp.stack(new_s5_re), jnp.stack(new_s5_im),
            jnp.stack(new_swa_k), jnp.stack(new_swa_v))


def setup_inputs(seed: int = 0) -> dict:
    key = jax.random.key(seed)
    keys = iter(jax.random.split(key, 64))
    f32 = jnp.float32

    def normal(shape, scale=1.0):
        return scale * jax.random.normal(next(keys), shape, f32)

    def gain(shape):
        return 1.0 + 0.05 * jax.random.normal(next(keys), shape, f32)

    win = min(SWA_WINDOW, PAST_LEN)
    log_lo, log_hi = math.log(S5_STEP_MIN), math.log(S5_STEP_MAX)
    n_idx = jnp.arange(S5_STATE, dtype=f32)
    mix_w = 2 * MIX_HALF
    return {
        'x_prompt': normal((BATCH, SEQ, D_MODEL)),
        'x_sample': normal((DEC_BATCH, DEC_SEQ, D_MODEL)),
        'mem_prompt': normal((BATCH, N_MEM, D_MODEL)),
        'state_gla': normal((N_EVEN, DEC_BATCH, GLA_HEADS, GLA_DK, GLA_DV), 0.5),
        'state_ret': normal((N_EVEN, DEC_BATCH, RET_HEADS, RET_DK, RET_DV), 0.5),
        'state_s5_re': normal((N_ODD, DEC_BATCH, S5_GROUPS, S5_STATE), 0.1),
        'state_s5_im': normal((N_ODD, DEC_BATCH, S5_GROUPS, S5_STATE), 0.1),
        'cache_swa_k': normal((N_ODD, DEC_BATCH, win, SWA_KV_HEADS, SWA_HD)),
        'cache_swa_v': normal((N_ODD, DEC_BATCH, win, SWA_KV_HEADS, SWA_HD)),
        'cache_mem_k': normal((DEPTH, DEC_BATCH, N_MEM, MEM_HEADS, MEM_HD)),
        'cache_mem_v': normal((DEPTH, DEC_BATCH, N_MEM, MEM_HEADS, MEM_HD)),
        'ffn1_norm': gain((DEPTH, D_MODEL)),
        'ffn1_w_gate': normal((DEPTH, D_MODEL, FFN_HIDDEN), D_MODEL ** -0.5),
        'ffn1_w_up': normal((DEPTH, D_MODEL, FFN_HIDDEN), D_MODEL ** -0.5),
        'ffn1_w_down': normal((DEPTH, FFN_HIDDEN, D_MODEL), FFN_HIDDEN ** -0.5),
        'ffn2_norm': gain((DEPTH, D_MODEL)),
        'ffn2_w_gate': normal((DEPTH, D_MODEL, FFN_HIDDEN), D_MODEL ** -0.5),
        'ffn2_w_up': normal((DEPTH, D_MODEL, FFN_HIDDEN), D_MODEL ** -0.5),
        'ffn2_w_down': normal((DEPTH, FFN_HIDDEN, D_MODEL), FFN_HIDDEN ** -0.5),
        'mix_norm': gain((DEPTH, D_MODEL)),
        'even_w_in': normal((N_EVEN, D_MODEL, EVEN_IN), D_MODEL ** -0.5),
        'gla_w_gate': normal((N_EVEN, GLA_RANK, GLA_HEADS * GLA_DK), GLA_RANK ** -0.5),
        'gla_b_gate': normal((N_EVEN, GLA_HEADS * GLA_DK), 0.01),
        'gla_out_norm': gain((N_EVEN, GLA_DV)),
        'even_w_out': normal((N_EVEN, mix_w, D_MODEL), mix_w ** -0.5),
        'odd_w_in': normal((N_ODD, D_MODEL, ODD_IN), D_MODEL ** -0.5),
        's5_a_re': -0.5 + normal((N_ODD, S5_GROUPS, S5_STATE), 0.01),
        's5_a_im': math.pi * n_idx + normal((N_ODD, S5_GROUPS, S5_STATE), 0.01),
        's5_log_step': log_lo + (log_hi - log_lo) * jax.random.uniform(next(keys), (N_ODD, S5_GROUPS), f32),
        's5_b_re': normal((N_ODD, S5_GROUPS, S5_STATE, S5_GROUP), (2.0 * S5_GROUP) ** -0.5),
        's5_b_im': normal((N_ODD, S5_GROUPS, S5_STATE, S5_GROUP), (2.0 * S5_GROUP) ** -0.5),
        's5_c_re': normal((N_ODD, S5_GROUPS, S5_GROUP, S5_STATE), (2.0 * S5_STATE) ** -0.5),
        's5_c_im': normal((N_ODD, S5_GROUPS, S5_GROUP, S5_STATE), (2.0 * S5_STATE) ** -0.5),
        's5_d': normal((N_ODD, S5_WIDTH)),
        's5_w_glu': normal((N_ODD, S5_WIDTH, S5_WIDTH), S5_WIDTH ** -0.5),
        's5_b_glu': normal((N_ODD, S5_WIDTH), 0.01),
        'swa_q_norm': gain((N_ODD, SWA_HD)),
        'swa_k_norm': gain((N_ODD, SWA_HD)),
        'swa_sinks': normal((N_ODD, SWA_HEADS), 0.5),
        'odd_w_out': normal((N_ODD, mix_w, D_MODEL), mix_w ** -0.5),
        'mem_x_norm': gain((DEPTH, D_MODEL)),
        'mem_m_norm': gain((DEPTH, D_MODEL)),
        'mem_w_q': normal((DEPTH, D_MODEL, D_MODEL), D_MODEL ** -0.5),
        'mem_w_k': normal((DEPTH, D_MODEL, D_MODEL), D_MODEL ** -0.5),
        'mem_w_v': normal((DEPTH, D_MODEL, D_MODEL), D_MODEL ** -0.5),
        'mem_w_o': normal((DEPTH, D_MODEL, D_MODEL), D_MODEL ** -0.5),
        'mem_q_norm': gain((DEPTH, MEM_HD)),
        'mem_k_norm': gain((DEPTH, MEM_HD)),
    }


def reference(x_prompt, x_sample, mem_prompt, state_gla, state_ret, state_s5_re, state_s5_im,
              cache_swa_k, cache_swa_v, cache_mem_k, cache_mem_v,
              ffn1_norm, ffn1_w_gate, ffn1_w_up, ffn1_w_down, ffn2_norm, ffn2_w_gate, ffn2_w_up, ffn2_w_down,
              mix_norm, even_w_in, gla_w_gate, gla_b_gate, gla_out_norm, even_w_out,
              odd_w_in, s5_a_re, s5_a_im, s5_log_step, s5_b_re, s5_b_im, s5_c_re, s5_c_im, s5_d, s5_w_glu,
              s5_b_glu, swa_q_norm, swa_k_norm, swa_sinks, odd_w_out,
              mem_x_norm, mem_m_norm, mem_w_q, mem_w_k, mem_w_v, mem_w_o, mem_q_norm, mem_k_norm):
    w = dict(ffn1_norm=ffn1_norm, ffn1_w_gate=ffn1_w_gate, ffn1_w_up=ffn1_w_up, ffn1_w_down=ffn1_w_down,
             ffn2_norm=ffn2_norm, ffn2_w_gate=ffn2_w_gate, ffn2_w_up=ffn2_w_up, ffn2_w_down=ffn2_w_down,
             mix_norm=mix_norm, even_w_in=even_w_in, gla_w_gate=gla_w_gate, gla_b_gate=gla_b_gate,
             gla_out_norm=gla_out_norm, even_w_out=even_w_out, odd_w_in=odd_w_in,
             s5_a_re=s5_a_re, s5_a_im=s5_a_im, s5_log_step=s5_log_step, s5_b_re=s5_b_re, s5_b_im=s5_b_im,
             s5_c_re=s5_c_re, s5_c_im=s5_c_im, s5_d=s5_d, s5_w_glu=s5_w_glu, s5_b_glu=s5_b_glu,
             swa_q_norm=swa_q_norm, swa_k_norm=swa_k_norm, swa_sinks=swa_sinks, odd_w_out=odd_w_out,
             mem_x_norm=mem_x_norm, mem_m_norm=mem_m_norm, mem_w_q=mem_w_q, mem_w_k=mem_w_k,
             mem_w_v=mem_w_v, mem_w_o=mem_w_o, mem_q_norm=mem_q_norm, mem_k_norm=mem_k_norm)
    dt = x_prompt.dtype
    mem_kv = [memory_kv(mem_prompt, w, layer) for layer in range(DEPTH)]
    p_mem_k = jnp.stack([kv[0] for kv in mem_kv])
    p_mem_v = jnp.stack([kv[1] for kv in mem_kv])
    z_gla = jnp.zeros((N_EVEN, BATCH, GLA_HEADS, GLA_DK, GLA_DV), dt)
    z_ret = jnp.zeros((N_EVEN, BATCH, RET_HEADS, RET_DK, RET_DV), dt)
    z_s5 = jnp.zeros((N_ODD, BATCH, S5_GROUPS, S5_STATE), dt)
    z_buf = jnp.zeros((N_ODD, BATCH, 0, SWA_KV_HEADS, SWA_HD), dt)
    y_prompt, p_gla, p_ret, p_s5_re, p_s5_im, p_swa_k, p_swa_v = run_trunk(
        x_prompt, 0, z_gla, z_ret, z_s5, z_s5, z_buf, z_buf, p_mem_k, p_mem_v, w)
    y_sample, s_gla, s_ret, s_s5_re, s_s5_im, s_swa_k, s_swa_v = run_trunk(
        x_sample, PAST_LEN, state_gla, state_ret, state_s5_re, state_s5_im, cache_swa_k, cache_swa_v,
        cache_mem_k, cache_mem_v, w)
    return (y_prompt, y_sample, p_gla, p_ret, p_s5_re, p_s5_im, p_swa_k, p_swa_v, p_mem_k, p_mem_v,
            s_gla, s_ret, s_s5_re, s_s5_im, s_swa_k, s_swa_v)
```

```python
import functools
import math

import jax
import jax.numpy as jnp
import numpy as np
from jax import lax
from jax.experimental import pallas as pl
from jax.experimental.pallas import tpu as pltpu

F32 = jnp.float32
BF16 = jnp.bfloat16
NORM_EPS = 1e-6
HIGHEST = lax.Precision.HIGHEST

D_MODEL = 1024
GLA_HEADS = 4
GLA_DK = 64
GLA_DV = 128
GLA_RANK = 16
GLA_TAU = 16.0
RET_HEADS = 4
RET_DK = 64
RET_THETA = 10000.0
LA_CHUNK = 64
S5_WIDTH = 512
S5_GROUP = 16
S5_GROUPS = 32
S5_STATE = 64
SWA_HD = 64
SWA_HEADS = 8
SWA_KV_HEADS = 2
SWA_WINDOW = 128
ROPE_THETA = 500000.0
MEM_HEADS = 4
MEM_HD = 256
PAST_LEN = 8192

VMEM_LIMIT_BYTES = 52 * 1024 * 1024
LANES = 128


def _params(*sem):
    return pltpu.CompilerParams(dimension_semantics=sem, vmem_limit_bytes=VMEM_LIMIT_BYTES)


def _rms(x, gain=None):
    y = x * lax.rsqrt(jnp.mean(x * x, axis=-1, keepdims=True) + NORM_EPS)
    return y if gain is None else y * gain


def _dot(a, b):
    return jnp.dot(a, b, preferred_element_type=F32)


def _dot_nt(a, b):
    return lax.dot_general(a, b, (((1,), (1,)), ((), ())), preferred_element_type=F32)


def _dot_tn(a, b):
    return lax.dot_general(a, b, (((0,), (0,)), ((), ())), preferred_element_type=F32)


def _dot_f32(a, b):
    return jnp.dot(a, b, precision=HIGHEST, preferred_element_type=F32)


def _dot_nt_f32(a, b):
    return lax.dot_general(a, b, (((1,), (1,)), ((), ())), precision=HIGHEST, preferred_element_type=F32)


def _dot_tn_f32(a, b):
    return lax.dot_general(a, b, (((0,), (0,)), ((), ())), precision=HIGHEST, preferred_element_type=F32)


def _bf(x):
    return x.astype(BF16)


def _log_sigmoid(x):
    return jnp.minimum(x, 0.0) - jnp.log1p(jnp.exp(-jnp.abs(x)))


def _iota(shape, dim):
    return lax.broadcasted_iota(jnp.int32, shape, dim)


def _swap_pairs(x):
    n = x.shape[-1]
    even = (_iota(x.shape, 1) & 1) == 0
    return jnp.where(even, pltpu.roll(x, n - 1, 1), pltpu.roll(x, 1, 1))


def _rope_partner(x, head_dim, half):
    n = x.shape[-1]
    first = (_iota(x.shape, 1) & (head_dim - 1)) < half
    return jnp.where(first, pltpu.roll(x, n - half, 1), pltpu.roll(x, half, 1))


def _ffn_kernel(x_ref, g_ref, wg_ref, wu_ref, wd_ref, o_ref, xn_ref, acc_ref):
    j = pl.program_id(1)

    @pl.when(j == 0)
    def _():
        xn_ref[...] = _bf(_rms(x_ref[...], g_ref[...]))
        acc_ref[...] = jnp.zeros_like(acc_ref)

    xn = xn_ref[...]
    gate = _dot(xn, wg_ref[...])
    up = _dot(xn, wu_ref[...])
    acc_ref[...] += _dot(_bf(jax.nn.silu(gate) * up), wd_ref[...])

    @pl.when(j == pl.num_programs(1) - 1)
    def _():
        o_ref[...] = x_ref[...] + 0.5 * acc_ref[...]


def _ffn(x, gain, wg, wu, wd, *, tm, th):
    t, d = x.shape
    h = wg.shape[1]
    return pl.pallas_call(
        _ffn_kernel,
        grid=(t // tm, h // th),
        in_specs=[
            pl.BlockSpec((tm, d), lambda i, j: (i, 0)),
            pl.BlockSpec((1, d), lambda i, j: (0, 0)),
            pl.BlockSpec((d, th), lambda i, j: (0, j)),
            pl.BlockSpec((d, th), lambda i, j: (0, j)),
            pl.BlockSpec((th, d), lambda i, j: (j, 0)),
        ],
        out_specs=pl.BlockSpec((tm, d), lambda i, j: (i, 0)),
        out_shape=jax.ShapeDtypeStruct((t, d), F32),
        scratch_shapes=[pltpu.VMEM((tm, d), BF16), pltpu.VMEM((tm, d), F32)],
        compiler_params=_params("parallel", "arbitrary"),
        name="ffn",
    )(x, gain, wg, wu, wd)


def _nmm_kernel(x_ref, g_ref, w_ref, hg_ref, o_ref, xn_ref, *, n_norm_tiles):
    j = pl.program_id(1)

    @pl.when(j == 0)
    def _():
        xn_ref[...] = _bf(_rms(x_ref[...], g_ref[...]))

    y = _dot(xn_ref[...], w_ref[...])
    if n_norm_tiles == 0:
        o_ref[...] = y
    else:
        @pl.when(j < n_norm_tiles)
        def _():
            o_ref[...] = _rms(y, hg_ref[...])

        @pl.when(j >= n_norm_tiles)
        def _():
            o_ref[...] = y


def _norm_matmul(x, gain, w, *, tm, tn, head_gain=None, n_norm_tiles=0):
    t, d = x.shape
    n = w.shape[1]
    if head_gain is None:
        head_gain = jnp.ones((1, tn), F32)
    return pl.pallas_call(
        functools.partial(_nmm_kernel, n_norm_tiles=n_norm_tiles),
        grid=(t // tm, n // tn),
        in_specs=[
            pl.BlockSpec((tm, d), lambda i, j: (i, 0)),
            pl.BlockSpec((1, d), lambda i, j: (0, 0)),
            pl.BlockSpec((d, tn), lambda i, j: (0, j)),
            pl.BlockSpec((1, tn), lambda i, j: (0, 0)),
        ],
        out_specs=pl.BlockSpec((tm, tn), lambda i, j: (i, j)),
        out_shape=jax.ShapeDtypeStruct((t, n), F32),
        scratch_shapes=[pltpu.VMEM((tm, d), BF16)],
        compiler_params=_params("parallel", "arbitrary"),
        name="norm_matmul",
    )(x, gain, w, head_gain)


def _mmr_kernel(*refs, n_terms):
    x_ref = refs[0]
    a_refs = refs[1:1 + n_terms]
    w_refs = refs[1 + n_terms:1 + 2 * n_terms]
    o_ref = refs[1 + 2 * n_terms]
    acc = x_ref[...]
    for a_ref, w_ref in zip(a_refs, w_refs):
        acc = acc + _dot(_bf(a_ref[...]), w_ref[...])
    o_ref[...] = acc


def _matmul_residual(x, terms, *, tm):
    t, d = x.shape
    acts = [a for a, _ in terms]
    ws = [w for _, w in terms]
    in_specs = [pl.BlockSpec((tm, d), lambda i: (i, 0))]
    in_specs += [pl.BlockSpec((tm, a.shape[1]), lambda i: (i, 0)) for a in acts]
    in_specs += [pl.BlockSpec(w.shape, lambda i: (0, 0)) for w in ws]
    return pl.pallas_call(
        functools.partial(_mmr_kernel, n_terms=len(terms)),
        grid=(t // tm,),
        in_specs=in_specs,
        out_specs=pl.BlockSpec((tm, d), lambda i: (i, 0)),
        out_shape=jax.ShapeDtypeStruct((t, d), F32),
        compiler_params=_params("parallel"),
        name="matmul_residual",
    )(x, *acts, *ws)


EV_GQ, EV_GK, EV_GV, EV_GG = 0, 256, 512, 1024
EV_RQ, EV_RK, EV_RV, EV_RG = 1536, 1792, 2048, 2560
EV_GA = 3072
EV_COLS = 3200


def _gate_and_norm(o, gate, gain=None):
    return _rms(o, gain) * jax.nn.silu(gate)


def _even_prompt_kernel(proj_ref, cos_ref, sin_ref, ld_ref, wgate_ref, bgate_ref, gnorm_ref, s0g_ref, s0r_ref,
                        mix_ref, sg_ref, sr_ref, *, chunk, n_chunks):
    @pl.when(pl.program_id(1) == 0)
    def _():
        sg_ref[...] = s0g_ref[...]
        sr_ref[...] = s0r_ref[...]

    c = chunk
    causal = _iota((c, c), 1) <= _iota((c, c), 0)
    tril = causal.astype(F32)
    ones_cv = jnp.ones((c, GLA_DV), F32)
    tpos = (_iota((c, 1), 0) + 1).astype(F32)
    ld = ld_ref[...]
    cum_r = tpos * ld
    tot_r = float(c) * ld
    r_qdec, r_kinv, r_kdec = jnp.exp(cum_r), jnp.exp(-cum_r), jnp.exp(tot_r - cum_r)
    r_dec = jnp.exp(tot_r)
    wgate = wgate_ref[...]
    bgate = bgate_ref[...]
    gnorm = gnorm_ref[...]

    def body(ci, carry):
        r0 = pl.multiple_of(ci * c, c)

        def cols(a, b):
            return proj_ref[0, pl.ds(r0, c), a:b]

        log_a = _log_sigmoid(_dot(_bf(cols(EV_GA, EV_COLS)), wgate) + bgate) * (1.0 / GLA_TAU)
        cum = _dot_f32(tril, log_a)
        tot = cum[c - 1:c, :]
        q_dec = cols(EV_GQ, EV_GQ + 256) * (GLA_DK ** -0.5) * jnp.exp(cum)
        k = cols(EV_GK, EV_GK + 256)
        k_inv = k * jnp.exp(-cum)
        k_dec = k * jnp.exp(tot - cum)
        for h in range(GLA_HEADS):
            sl = slice(h * GLA_DK, (h + 1) * GLA_DK)
            v = _bf(cols(EV_GV + h * GLA_DV, EV_GV + (h + 1) * GLA_DV))
            qd = _bf(q_dec[:, sl])
            scores = jnp.where(causal, _dot_nt(qd, _bf(k_inv[:, sl])), 0.0)
            state = sg_ref[0, h]
            o = _dot(_bf(scores), v) + _dot(qd, _bf(state))
            decay = jnp.exp(_dot_tn_f32(log_a[:, sl], ones_cv))
            sg_ref[0, h] = state * decay + _dot_tn(_bf(k_dec[:, sl]), v)
            gate = cols(EV_GG + h * GLA_DV, EV_GG + (h + 1) * GLA_DV)
            mix_ref[0, pl.ds(r0, c), h * GLA_DV:(h + 1) * GLA_DV] = _gate_and_norm(o, gate, gnorm)

        cos = cos_ref[pl.ds(r0, c), :]
        sin = sin_ref[pl.ds(r0, c), :]
        rq = cols(EV_RQ, EV_RQ + 256)
        rk = cols(EV_RK, EV_RK + 256)
        q_rot = rq * cos + _swap_pairs(rq) * sin
        k_rot = (rk * cos + _swap_pairs(rk) * sin) * (RET_DK ** -0.5)
        q_dec = q_rot * r_qdec
        k_inv = k_rot * r_kinv
        k_dec = k_rot * r_kdec
        for h in range(RET_HEADS):
            sl = slice(h * RET_DK, (h + 1) * RET_DK)
            v = _bf(cols(EV_RV + h * GLA_DV, EV_RV + (h + 1) * GLA_DV))
            qd = _bf(q_dec[:, sl])
            scores = jnp.where(causal, _dot_nt(qd, _bf(k_inv[:, sl])), 0.0)
            state = sr_ref[0, h]
            o = _dot(_bf(scores), v) + _dot(qd, _bf(state))
            sr_ref[0, h] = state * r_dec[:, h * RET_DK:h * RET_DK + 1] + _dot_tn(_bf(k_dec[:, sl]), v)
            gate = cols(EV_RG + h * GLA_DV, EV_RG + (h + 1) * GLA_DV)
            mix_ref[0, pl.ds(r0, c), 512 + h * GLA_DV:512 + (h + 1) * GLA_DV] = _gate_and_norm(o, gate)
        return carry

    lax.fori_loop(0, n_chunks, body, 0)


def _even_prompt(proj, cos, sin, ld_row, wgate, bgate, gnorm, s0g, s0r, *, lt):
    b, l, _ = proj.shape
    chunk = math.gcd(l, LA_CHUNK)
    st_spec = pl.BlockSpec((1, GLA_HEADS, GLA_DK, GLA_DV), lambda i, j: (i, 0, 0, 0))
    full = lambda a: pl.BlockSpec(a.shape, lambda i, j: (0,) * a.ndim)
    return pl.pallas_call(
        functools.partial(_even_prompt_kernel, chunk=chunk, n_chunks=lt // chunk),
        grid=(b, l // lt),
        in_specs=[
            pl.BlockSpec((1, lt, EV_COLS), lambda i, j: (i, j, 0)),
            pl.BlockSpec((lt, 256), lambda i, j: (j, 0)),
            pl.BlockSpec((lt, 256), lambda i, j: (j, 0)),
            full(ld_row), full(wgate), full(bgate), full(gnorm), st_spec, st_spec,
        ],
        out_specs=[pl.BlockSpec((1, lt, D_MODEL), lambda i, j: (i, j, 0)), st_spec, st_spec],
        out_shape=[jax.ShapeDtypeStruct((b, l, D_MODEL), F32),
                   jax.ShapeDtypeStruct(s0g.shape, F32), jax.ShapeDtypeStruct(s0r.shape, F32)],
        compiler_params=_params("parallel", "arbitrary"),
        name="even_prompt",
    )(proj, cos, sin, ld_row, wgate, bgate, gnorm, s0g, s0r)


def _even_sample_kernel(proj_ref, cos_ref, sin_ref, ld_ref, wgate_ref, bgate_ref, gnorm_ref, s0g_ref, s0r_ref,
                        mix_ref, sg_ref, sr_ref, *, bg, ls):
    r = bg * ls
    ls_shift = ls.bit_length() - 1
    dk_shift = GLA_DK.bit_length() - 1
    n_exp = bg * GLA_DK
    row_seq = _iota((r, r), 0) >> ls_shift
    col_seq = _iota((r, r), 1) >> ls_shift
    same = row_seq == col_seq
    seg = same & (_iota((r, r), 1) <= _iota((r, r), 0))
    seg_f = seg.astype(F32)
    same_f = same.astype(F32)
    tile_b = _bf(((_iota((GLA_DK, n_exp), 1) & (GLA_DK - 1)) == _iota((GLA_DK, n_exp), 0)).astype(F32))
    tile_t_f = ((_iota((n_exp, GLA_DK), 0) & (GLA_DK - 1)) == _iota((n_exp, GLA_DK), 1)).astype(F32)
    tile_t_b = _bf(tile_t_f)
    q_mask = (_iota((r, n_exp), 0) >> ls_shift) == (_iota((r, n_exp), 1) >> dk_shift)
    k_mask = (_iota((n_exp, r), 0) >> dk_shift) == (_iota((n_exp, r), 1) >> ls_shift)
    tpos = ((_iota((r, 1), 0) & (ls - 1)) + 1).astype(F32)
    ld = ld_ref[...]
    cum_r = tpos * ld
    tot_r = float(ls) * ld
    r_dec = jnp.exp(tot_r)
    gnorm = gnorm_ref[...]

    def mixer(q_dec, k_inv, k_dec, v_col, g_col, s0_ref, s_ref, out_col, decay_of, gain):
        for h in range(GLA_HEADS):
            sl = slice(h * GLA_DK, (h + 1) * GLA_DK)
            v = _bf(proj_ref[:, v_col + h * GLA_DV:v_col + (h + 1) * GLA_DV])
            qd = _bf(q_dec[:, sl])
            scores = jnp.where(seg, _dot_nt(qd, _bf(k_inv[:, sl])), 0.0)
            state = s0_ref[:, h].reshape(n_exp, GLA_DV)
            q_exp = _bf(jnp.where(q_mask, _dot(qd, tile_b), 0.0))
            o = _dot(_bf(scores), v) + _dot(q_exp, _bf(state))
            k_exp = _bf(jnp.where(k_mask, _dot_nt(tile_t_b, _bf(k_dec[:, sl])), 0.0))
            new_state = state * decay_of(h, sl) + _dot(k_exp, v)
            s_ref[:, h] = new_state.reshape(bg, GLA_DK, GLA_DV)
            gate = proj_ref[:, g_col + h * GLA_DV:g_col + (h + 1) * GLA_DV]
            mix_ref[:, out_col + h * GLA_DV:out_col + (h + 1) * GLA_DV] = _gate_and_norm(o, gate, gain)

    log_a = _log_sigmoid(_dot(_bf(proj_ref[:, EV_GA:EV_COLS]), wgate_ref[...]) + bgate_ref[...]) * (1.0 / GLA_TAU)
    cum = _dot_f32(seg_f, log_a)
    tot = _dot_f32(same_f, log_a)
    k = proj_ref[:, EV_GK:EV_GK + 256]

    def gla_decay(h, sl):
        la_exp = jnp.where(k_mask, _dot_nt_f32(tile_t_f, log_a[:, sl]), 0.0)
        return jnp.exp(jnp.sum(la_exp, axis=-1, keepdims=True))

    mixer(proj_ref[:, EV_GQ:EV_GQ + 256] * (GLA_DK ** -0.5) * jnp.exp(cum), k * jnp.exp(-cum), k * jnp.exp(tot - cum),
          EV_GV, EV_GG, s0g_ref, sg_ref, 0, gla_decay, gnorm)

    cos = cos_ref[...]
    sin = sin_ref[...]
    rq = proj_ref[:, EV_RQ:EV_RQ + 256]
    rk = proj_ref[:, EV_RK:EV_RK + 256]
    q_rot = rq * cos + _swap_pairs(rq) * sin
    k_rot = (rk * cos + _swap_pairs(rk) * sin) * (RET_DK ** -0.5)

    def ret_decay(h, sl):
        return r_dec[:, h * RET_DK:h * RET_DK + 1]

    mixer(q_rot * jnp.exp(cum_r), k_rot * jnp.exp(-cum_r), k_rot * jnp.exp(tot_r - cum_r),
          EV_RV, EV_RG, s0r_ref, sr_ref, 512, ret_decay, None)


def _even_sample(proj, cos, sin, ld_row, wgate, bgate, gnorm, s0g, s0r, *, bg, ls):
    t = proj.shape[0]
    n_b = t // ls
    r = bg * ls
    st_spec = pl.BlockSpec((bg, GLA_HEADS, GLA_DK, GLA_DV), lambda i: (i, 0, 0, 0))
    full = lambda a: pl.BlockSpec(a.shape, lambda i: (0,) * a.ndim)
    return pl.pallas_call(
        functools.partial(_even_sample_kernel, bg=bg, ls=ls),
        grid=(n_b // bg,),
        in_specs=[
            pl.BlockSpec((r, EV_COLS), lambda i: (i, 0)),
            full(cos), full(sin), full(ld_row), full(wgate), full(bgate), full(gnorm), st_spec, st_spec,
        ],
        out_specs=[pl.BlockSpec((r, D_MODEL), lambda i: (i, 0)), st_spec, st_spec],
        out_shape=[jax.ShapeDtypeStruct((t, D_MODEL), F32),
                   jax.ShapeDtypeStruct(s0g.shape, F32), jax.ShapeDtypeStruct(s0r.shape, F32)],
        compiler_params=_params("parallel"),
        name="even_sample",
    )(proj, cos, sin, ld_row, wgate, bgate, gnorm, s0g, s0r)


def _s5_prep_kernel(are_ref, aim_ref, lstep_ref, bre_ref, bim_ref, abre_ref, abim_ref, bbre_ref, bbim_ref):
    a_re, a_im = are_ref[...], aim_ref[...]
    step = jnp.exp(lstep_ref[...])
    mag = jnp.exp(a_re * step)
    ab_re = mag * jnp.cos(a_im * step)
    ab_im = mag * jnp.sin(a_im * step)
    den = a_re * a_re + a_im * a_im
    coef_re = ((ab_re - 1.0) * a_re + ab_im * a_im) / den
    coef_im = (ab_im * a_re - (ab_re - 1.0) * a_im) / den
    b_re, b_im = bre_ref[...], bim_ref[...]
    abre_ref[...] = ab_re
    abim_ref[...] = ab_im
    bbre_ref[...] = coef_re * b_re - coef_im * b_im
    bbim_ref[...] = coef_re * b_im + coef_im * b_re


def _s5_prep(a_re, a_im, log_step, b_re, b_im):
    g, n = a_re.shape
    shp3 = jax.ShapeDtypeStruct((g, 1, n), F32)
    shpb = jax.ShapeDtypeStruct((g, S5_GROUP, n), F32)
    return pl.pallas_call(_s5_prep_kernel, out_shape=[shp3, shp3, shpb, shpb], name="s5_prep")(
        a_re.reshape(g, 1, n), a_im.reshape(g, 1, n), log_step.reshape(g, 1, 1),
        jnp.swapaxes(b_re, 1, 2), jnp.swapaxes(b_im, 1, 2))


def _s5_kernel(u_ref, s0re_ref, s0im_ref, abre_ref, abim_ref, bbre_ref, bbim_ref, ccre_ref, ccim_ref,
               d_ref, wglu_ref, bglu_ref, out_ref, sre_ref, sim_ref, xr_ref, xi_ref, *, n_seq, lt, chunks_per_pass):
    @pl.when(pl.program_id(0) == 0)
    def _():
        sre_ref[...] = s0re_ref[...]
        sim_ref[...] = s0im_ref[...]

    n_lc = S5_GROUPS * S5_STATE // LANES
    rows = n_seq * lt
    u = u_ref[...].reshape(rows, S5_WIDTH)
    ub = _bf(u)
    x_re = _dot(ub, bbre_ref[...])
    x_im = _dot(ub, bbim_ref[...])
    for c in range(n_lc):
        xr_ref[c] = x_re[:, c * LANES:(c + 1) * LANES]
        xi_ref[c] = x_im[:, c * LANES:(c + 1) * LANES]

    for c0 in range(0, n_lc, chunks_per_pass):
        cs = list(range(c0, c0 + chunks_per_pass))
        init = (tuple(sre_ref[:, c * LANES:(c + 1) * LANES] for c in cs)
                + tuple(sim_ref[:, c * LANES:(c + 1) * LANES] for c in cs))

        def step(t, carry, cs=cs):
            rws = pl.ds(t, n_seq, stride=lt) if lt > 1 else pl.ds(t, n_seq)
            new_re, new_im = [], []
            for k, c in enumerate(cs):
                a_re, a_im = abre_ref[c], abim_ref[c]
                h_re, h_im = carry[k], carry[len(cs) + k]
                n_re = a_re * h_re - a_im * h_im + xr_ref[c, rws, :]
                n_im = a_re * h_im + a_im * h_re + xi_ref[c, rws, :]
                xr_ref[c, rws, :] = n_re
                xi_ref[c, rws, :] = n_im
                new_re.append(n_re)
                new_im.append(n_im)
            return tuple(new_re + new_im)

        fin = lax.fori_loop(0, lt, step, init)
        for k, c in enumerate(cs):
            sre_ref[:, c * LANES:(c + 1) * LANES] = fin[k]
            sim_ref[:, c * LANES:(c + 1) * LANES] = fin[len(cs) + k]

    h_re = jnp.concatenate([xr_ref[c] for c in range(n_lc)], axis=1)
    h_im = jnp.concatenate([xi_ref[c] for c in range(n_lc)], axis=1)
    y = _dot(_bf(h_re), ccre_ref[...]) - _dot(_bf(h_im), ccim_ref[...]) + d_ref[...] * u
    z = jax.nn.gelu(y, approximate=True)
    out = z * jax.nn.sigmoid(_dot(_bf(z), wglu_ref[...]) + bglu_ref[...])
    out_ref[...] = out.reshape(out_ref.shape)


def _s5(proj3, s0_re, s0_im, ab_re, ab_im, bb_re, bb_im, cc_re, cc_im, d_row, wglu, bglu, *, n_seq, lt,
        chunks_per_pass):
    nb, rows_b, _ = proj3.shape
    blk_rows = n_seq * lt // nb
    n_lc = S5_GROUPS * S5_STATE // LANES
    full = lambda a: pl.BlockSpec(a.shape, lambda j: (0,) * a.ndim)
    io_spec = pl.BlockSpec((nb, blk_rows, S5_WIDTH), lambda j: (0, j, 0))
    return pl.pallas_call(
        functools.partial(_s5_kernel, n_seq=n_seq, lt=lt, chunks_per_pass=chunks_per_pass),
        grid=(rows_b // blk_rows,),
        in_specs=[io_spec, full(s0_re), full(s0_im), full(ab_re), full(ab_im), full(bb_re), full(bb_im),
                  full(cc_re), full(cc_im), full(d_row), full(wglu), full(bglu)],
        out_specs=[io_spec, full(s0_re), full(s0_im)],
        out_shape=[jax.ShapeDtypeStruct((nb, rows_b, S5_WIDTH), F32),
                   jax.ShapeDtypeStruct(s0_re.shape, F32), jax.ShapeDtypeStruct(s0_im.shape, F32)],
        scratch_shapes=[pltpu.VMEM((n_lc, n_seq * lt, LANES), F32), pltpu.VMEM((n_lc, n_seq * lt, LANES), F32)],
        compiler_params=_params("arbitrary"),
        name="s5",
    )(proj3, s0_re, s0_im, ab_re, ab_im, bb_re, bb_im, cc_re, cc_im, d_row, wglu, bglu)


OD_Q_BLOCK = 1
OD_KV_BLOCK = 4
OD_COLS = 1280
ROPE_HALF = SWA_HD // 8
SWA_GROUP = SWA_HEADS // SWA_KV_HEADS


def _swa_qk(xq, xk, qg, kg, cq, sq):
    xq_g = xq * qg
    q_rot = xq_g * cq + _rope_partner(xq_g, SWA_HD, ROPE_HALF) * sq
    xk_g = xk * kg
    k_rot = xk_g * cq[:, :LANES] + _rope_partner(xk_g, SWA_HD, ROPE_HALF) * sq[:, :LANES]
    q_heads = []
    for h in range(SWA_HEADS):
        sl = slice(h * SWA_HD, (h + 1) * SWA_HD)
        xs = xq[:, sl]
        q_heads.append(q_rot[:, sl] * lax.rsqrt(jnp.mean(xs * xs, axis=-1, keepdims=True) + NORM_EPS))
    k_scale = []
    for h in range(SWA_KV_HEADS):
        xs = xk[:, h * SWA_HD:(h + 1) * SWA_HD]
        k_scale.append(lax.rsqrt(jnp.mean(xs * xs, axis=-1, keepdims=True) + NORM_EPS))
    k_n = k_rot * jnp.where(_iota(k_rot.shape, 1) < SWA_HD, k_scale[0], k_scale[1])
    return q_heads, k_n


def _swa_prompt_kernel(q_ref, kv_ref, cq_ref, sq_ref, qg_ref, kg_ref, sink_ref,
                       o_ref, ck_ref, cv_ref, kprev_ref, vprev_ref, *, n_blocks):
    w = SWA_WINDOW
    lstep = pl.program_id(1)

    @pl.when(lstep == 0)
    def _():
        kprev_ref[...] = jnp.zeros_like(kprev_ref)
        vprev_ref[...] = jnp.zeros_like(vprev_ref)

    rows_g = SWA_GROUP * w
    t_idx = _iota((rows_g, 2 * w), 0) & (w - 1)
    s_idx = _iota((rows_g, 2 * w), 1)
    band = (s_idx > t_idx) & (s_idx <= t_idx + w)
    k_prev = kprev_ref[...]
    v_prev = vprev_ref[...]
    for jb in range(n_blocks):
        rs = slice(jb * w, (jb + 1) * w)
        xk = kv_ref[0, rs, 0:LANES]
        v_cur = kv_ref[0, rs, LANES:2 * LANES]
        q_heads, k_cur = _swa_qk(q_ref[0, rs, :], xk, qg_ref[...], kg_ref[...], cq_ref[rs, :], sq_ref[rs, :])
        mask = band & (s_idx >= jnp.where(lstep == 0, w, 0)) if jb == 0 else band
        for kh in range(SWA_KV_HEADS):
            sl = slice(kh * SWA_HD, (kh + 1) * SWA_HD)
            q_stack = jnp.concatenate(q_heads[kh * SWA_GROUP:(kh + 1) * SWA_GROUP], axis=0)
            k_ext = jnp.concatenate([k_prev[:, sl], k_cur[:, sl]], axis=0)
            v_ext = jnp.concatenate([v_prev[:, sl], v_cur[:, sl]], axis=0)
            s = jnp.where(mask, _dot_nt(_bf(q_stack), _bf(k_ext)) * (SWA_HD ** -0.5), -jnp.inf)
            sink = sink_ref[kh * rows_g:(kh + 1) * rows_g, :]
            m = jnp.maximum(jnp.max(s, axis=-1, keepdims=True), sink)
            e = jnp.exp(s - m)
            p = e / (jnp.sum(e, axis=-1, keepdims=True) + jnp.exp(sink - m))
            o = _dot(_bf(p), _bf(v_ext))
            for g in range(SWA_GROUP):
                hq = kh * SWA_GROUP + g
                o_ref[0, rs, hq * SWA_HD:(hq + 1) * SWA_HD] = o[g * w:(g + 1) * w, :]
        k_prev, v_prev = k_cur, v_cur
    kprev_ref[...] = k_prev
    vprev_ref[...] = v_prev

    @pl.when(lstep == pl.num_programs(1) - 1)
    def _():
        ck_ref[0] = k_prev
        cv_ref[0] = v_prev


def _swa_prompt(proj, cq, sq, qg, kg, sink_col, *, lt):
    b, l, _ = proj.shape
    w = SWA_WINDOW
    full = lambda a: pl.BlockSpec(a.shape, lambda i, j: (0,) * a.ndim)
    cache_spec = pl.BlockSpec((1, w, LANES), lambda i, j: (i, 0, 0))
    return pl.pallas_call(
        functools.partial(_swa_prompt_kernel, n_blocks=lt // w),
        grid=(b, l // lt),
        in_specs=[
            pl.BlockSpec((1, lt, 512), lambda i, j: (i, j, OD_Q_BLOCK)),
            pl.BlockSpec((1, lt, 256), lambda i, j: (i, j, OD_KV_BLOCK)),
            pl.BlockSpec((lt, 512), lambda i, j: (j, 0)),
            pl.BlockSpec((lt, 512), lambda i, j: (j, 0)),
            full(qg), full(kg), full(sink_col),
        ],
        out_specs=[pl.BlockSpec((1, lt, 512), lambda i, j: (i, j, 0)), cache_spec, cache_spec],
        out_shape=[jax.ShapeDtypeStruct((b, l, 512), F32),
                   jax.ShapeDtypeStruct((b, w, LANES), F32), jax.ShapeDtypeStruct((b, w, LANES), F32)],
        scratch_shapes=[pltpu.VMEM((w, LANES), F32), pltpu.VMEM((w, LANES), F32)],
        compiler_params=_params("parallel", "arbitrary"),
        name="swa_prompt",
    )(proj, proj, cq, sq, qg, kg, sink_col)


def _swa_sample_kernel(q_ref, kv_ref, ck_ref, cv_ref, cq_ref, sq_ref, qg_ref, kg_ref, sink_ref,
                       o_ref, nk_ref, nv_ref, *, bg, ls):
    w = SWA_WINDOW
    r = bg * ls
    ls_shift = ls.bit_length() - 1
    w_shift = w.bit_length() - 1
    rows_g = SWA_GROUP * r
    xk = kv_ref[:, 0:LANES]
    v_new = kv_ref[:, LANES:2 * LANES]
    q_heads, k_new = _swa_qk(q_ref[...], xk, qg_ref[...], kg_ref[...], cq_ref[...], sq_ref[...])
    k_cache = ck_ref[...].reshape(bg * w, LANES)
    v_cache = cv_ref[...].reshape(bg * w, LANES)

    row = _iota((rows_g, bg * w), 0) & (r - 1)
    col = _iota((rows_g, bg * w), 1)
    mask_c = ((row >> ls_shift) == (col >> w_shift)) & ((col & (w - 1)) > (row & (ls - 1)))
    row_n = _iota((rows_g, r), 0) & (r - 1)
    col_n = _iota((rows_g, r), 1)
    mask_n = ((row_n >> ls_shift) == (col_n >> ls_shift)) & ((col_n & (ls - 1)) <= (row_n & (ls - 1)))
    for kh in range(SWA_KV_HEADS):
        sl = slice(kh * SWA_HD, (kh + 1) * SWA_HD)
        q_stack = _bf(jnp.concatenate(q_heads[kh * SWA_GROUP:(kh + 1) * SWA_GROUP], axis=0))
        s_c = jnp.where(mask_c, _dot_nt(q_stack, _bf(k_cache[:, sl])) * (SWA_HD ** -0.5), -jnp.inf)
        s_n = jnp.where(mask_n, _dot_nt(q_stack, _bf(k_new[:, sl])) * (SWA_HD ** -0.5), -jnp.inf)
        sink = sink_ref[kh * rows_g:(kh + 1) * rows_g, :]
        m = jnp.maximum(jnp.maximum(jnp.max(s_c, axis=-1, keepdims=True), jnp.max(s_n, axis=-1, keepdims=True)), sink)
        e_c = jnp.exp(s_c - m)
        e_n = jnp.exp(s_n - m)
        den = jnp.sum(e_c, axis=-1, keepdims=True) + jnp.sum(e_n, axis=-1, keepdims=True) + jnp.exp(sink - m)
        o = _dot(_bf(e_c / den), _bf(v_cache[:, sl])) + _dot(_bf(e_n / den), _bf(v_new[:, sl]))
        for g in range(SWA_GROUP):
            hq = kh * SWA_GROUP + g
            o_ref[:, hq * SWA_HD:(hq + 1) * SWA_HD] = o[g * r:(g + 1) * r, :]

    nk_ref[:, 0:w - ls, :] = ck_ref[:, ls:w, :]
    nv_ref[:, 0:w - ls, :] = cv_ref[:, ls:w, :]
    for b in range(bg):
        nk_ref[b, w - ls:w, :] = k_new[b * ls:(b + 1) * ls, :]
        nv_ref[b, w - ls:w, :] = v_new[b * ls:(b + 1) * ls, :]


def _swa_sample(proj, cache_k, cache_v, cq, sq, qg, kg, sink_col, *, bg, ls):
    t = proj.shape[0]
    w = SWA_WINDOW
    r = bg * ls
    full = lambda a: pl.BlockSpec(a.shape, lambda i: (0,) * a.ndim)
    cache_spec = pl.BlockSpec((bg, w, LANES), lambda i: (i, 0, 0))
    return pl.pallas_call(
        functools.partial(_swa_sample_kernel, bg=bg, ls=ls),
        grid=(t // r,),
        in_specs=[
            pl.BlockSpec((r, 512), lambda i: (i, OD_Q_BLOCK)),
            pl.BlockSpec((r, 256), lambda i: (i, OD_KV_BLOCK)),
            cache_spec, cache_spec, full(cq), full(sq), full(qg), full(kg), full(sink_col),
        ],
        out_specs=[pl.BlockSpec((r, 512), lambda i: (i, 0)), cache_spec, cache_spec],
        out_shape=[jax.ShapeDtypeStruct((t, 512), F32),
                   jax.ShapeDtypeStruct(cache_k.shape, F32), jax.ShapeDtypeStruct(cache_v.shape, F32)],
        compiler_params=_params("parallel"),
        name="swa_sample",
    )(proj, proj, cache_k, cache_v, cq, sq, qg, kg, sink_col)


def _mem_prompt_kernel(q_ref, k_ref, v_ref, o_ref):
    for h in range(MEM_HEADS):
        sl = slice(h * MEM_HD, (h + 1) * MEM_HD)
        s = _dot_nt(_bf(q_ref[0, :, sl]), _bf(k_ref[0, :, sl])) * (MEM_HD ** -0.5)
        e = jnp.exp(s - jnp.max(s, axis=-1, keepdims=True))
        p = e / jnp.sum(e, axis=-1, keepdims=True)
        o_ref[0, :, sl] = _dot(_bf(p), _bf(v_ref[0, :, sl]))


def _mem_prompt(q, k, v, *, lt):
    b, l, d = q.shape
    n_mem = k.shape[1]
    kv_spec = pl.BlockSpec((1, n_mem, d), lambda i, j: (i, 0, 0))
    io_spec = pl.BlockSpec((1, lt, d), lambda i, j: (i, j, 0))
    return pl.pallas_call(
        _mem_prompt_kernel,
        grid=(b, l // lt),
        in_specs=[io_spec, kv_spec, kv_spec],
        out_specs=io_spec,
        out_shape=jax.ShapeDtypeStruct(q.shape, F32),
        compiler_params=_params("parallel", "arbitrary"),
        name="mem_prompt",
    )(q, k, v)


def _mem_sample_kernel(q_ref, k_ref, v_ref, o_ref, *, bs, ls):
    r = bs * ls
    n_rows = MEM_HEADS * ls
    hd_shift = MEM_HD.bit_length() - 1
    ls_shift = ls.bit_length() - 1
    qb = _bf(q_ref[...])
    head_mask = (_iota((n_rows, D_MODEL), 0) >> ls_shift) == (_iota((n_rows, D_MODEL), 1) >> hd_shift)
    acc = jnp.zeros((r, D_MODEL), F32)
    for b in range(bs):
        sel = _bf(((_iota((n_rows, r), 0) & (ls - 1)) + b * ls == _iota((n_rows, r), 1)).astype(F32))
        q_exp = _bf(jnp.where(head_mask, _dot(sel, qb), 0.0))
        s = _dot_nt(_bf(k_ref[b]), q_exp) * (MEM_HD ** -0.5)
        e = jnp.exp(s - jnp.max(s, axis=0, keepdims=True))
        p = e / jnp.sum(e, axis=0, keepdims=True)
        o_all = jnp.where(head_mask, _dot_tn(_bf(p), _bf(v_ref[b])), 0.0)
        acc = acc + _dot_tn(sel, _bf(o_all))
    o_ref[...] = acc


def _mem_sample(q, k, v, *, bs, ls):
    t, d = q.shape
    n_mem = k.shape[1]
    r = bs * ls
    kv_spec = pl.BlockSpec((bs, n_mem, d), lambda i: (i, 0, 0))
    io_spec = pl.BlockSpec((r, d), lambda i: (i, 0))
    return pl.pallas_call(
        functools.partial(_mem_sample_kernel, bs=bs, ls=ls),
        grid=(t // r,),
        in_specs=[io_spec, kv_spec, kv_spec],
        out_specs=io_spec,
        out_shape=jax.ShapeDtypeStruct(q.shape, F32),
        compiler_params=_params("parallel"),
        name="mem_sample",
    )(q, k, v)


def _retention_tables(pos):
    inv = 1.0 / (RET_THETA ** jnp.linspace(0.0, 1.0, RET_DK // 2, dtype=F32))
    ang = pos[:, None] * inv[None, :]
    cos = jnp.repeat(jnp.cos(ang), 2, axis=1)
    sin = jnp.stack([-jnp.sin(ang), jnp.sin(ang)], axis=-1).reshape(pos.shape[0], RET_DK)
    return jnp.tile(cos, (1, RET_HEADS)), jnp.tile(sin, (1, RET_HEADS))


def _rope_tables(pos):
    half = ROPE_HALF
    inv = 1.0 / (ROPE_THETA ** (jnp.arange(half, dtype=F32) * 2.0 / (2 * half)))
    ang = pos[:, None] * inv[None, :]
    n = pos.shape[0]
    rest = SWA_HD - 2 * half
    cos = jnp.concatenate([jnp.cos(ang), jnp.cos(ang), jnp.ones((n, rest), F32)], axis=1)
    sin = jnp.concatenate([-jnp.sin(ang), jnp.sin(ang), jnp.zeros((n, rest), F32)], axis=1)
    return jnp.tile(cos, (1, SWA_HEADS)), jnp.tile(sin, (1, SWA_HEADS))


def _block_diag(t):
    g, a, b = t.shape
    eye = jnp.eye(g, dtype=t.dtype)
    return (t[:, :, None, :] * eye[:, None, :, None]).reshape(g * a, g * b)


def _sink_column(sinks, rows):
    return jnp.repeat(sinks.astype(F32), rows)[:, None]


def _trunk(x3, pos0, states, mem_k, mem_v, w, *, sample):
    b, l, d = x3.shape
    t = b * l
    x = x3.reshape(t, d)
    pos = pos0 + jnp.arange(l, dtype=F32)
    tm = 512 if sample else 1024
    gla_s, ret_s, s5_re, s5_im, swa_k, swa_v = states
    out_states = {k: [] for k in ("gla", "ret", "s5_re", "s5_im", "swa_k", "swa_v")}
    ld_row = jnp.repeat(jnp.log(1.0 - 2.0 ** (-5.0 - jnp.arange(RET_HEADS, dtype=F32))), RET_DK)[None, :]

    for layer in range(2):
        i = layer // 2
        x = _ffn(x, w['ffn1_norm'][layer][None], w['ffn1_w_gate'][layer], w['ffn1_w_up'][layer],
                 w['ffn1_w_down'][layer], tm=tm, th=256)
        if layer % 2 == 0:
            proj = _norm_matmul(x, w['mix_norm'][layer][None], w['even_w_in'][i], tm=tm, tn=640)
            cos, sin = _retention_tables(pos)
            args = (ld_row, w['gla_w_gate'][i], w['gla_b_gate'][i], w['gla_out_norm'][i])
            if sample:
                bg = 16
                mixed, g_s, r_s = _even_sample(proj, jnp.tile(cos, (bg, 1)), jnp.tile(sin, (bg, 1)), *args,
                                               gla_s[i], ret_s[i], bg=bg, ls=l)
            else:
                mixed, g_s, r_s = _even_prompt(proj.reshape(b, l, EV_COLS), cos, sin, *args,
                                               gla_s[i], ret_s[i], lt=512)
                mixed = mixed.reshape(t, d)
            out_states["gla"].append(g_s)
            out_states["ret"].append(r_s)
            x = _matmul_residual(x, [(mixed, w['even_w_out'][i])], tm=tm)
        else:
            proj = _norm_matmul(x, w['mix_norm'][layer][None], w['odd_w_in'][i], tm=tm, tn=640)
            cq, sq = _rope_tables(pos)
            qg = jnp.tile(w['swa_q_norm'][i], SWA_HEADS)[None, :]
            kg = jnp.tile(w['swa_k_norm'][i], SWA_KV_HEADS)[None, :]
            s5_args = (w['s5_ab_re'][i], w['s5_ab_im'][i], w['s5_bb_re'][i], w['s5_bb_im'][i],
                       w['s5_cc_re'][i], w['s5_cc_im'][i], w['s5_d'][i][None], w['s5_w_glu'][i], w['s5_b_glu'][i][None])
            n_state = S5_GROUPS * S5_STATE
            if sample:
                c_out, sr, si = _s5(proj.reshape(1, t, OD_COLS), s5_re[i].reshape(b, n_state),
                                    s5_im[i].reshape(b, n_state), *s5_args, n_seq=b, lt=l, chunks_per_pass=1)
                c_out = c_out.reshape(t, S5_WIDTH)
                bg = 8
                d_out, kb, vb = _swa_sample(proj, swa_k[i].reshape(b, SWA_WINDOW, LANES),
                                            swa_v[i].reshape(b, SWA_WINDOW, LANES),
                                            jnp.tile(cq, (bg, 1)), jnp.tile(sq, (bg, 1)), qg, kg,
                                            _sink_column(w['swa_sinks'][i], bg * l), bg=bg, ls=l)
            else:
                proj3 = proj.reshape(b, l, OD_COLS)
                c_out, sr, si = _s5(proj3, s5_re[i].reshape(b, n_state), s5_im[i].reshape(b, n_state),
                                    *s5_args, n_seq=b, lt=128, chunks_per_pass=8)
                c_out = c_out.reshape(t, S5_WIDTH)
                d_out, kb, vb = _swa_prompt(proj3, cq, sq, qg, kg, _sink_column(w['swa_sinks'][i], SWA_WINDOW), lt=512)
                d_out = d_out.reshape(t, 512)
            out_states["s5_re"].append(sr.reshape(b, S5_GROUPS, S5_STATE))
            out_states["s5_im"].append(si.reshape(b, S5_GROUPS, S5_STATE))
            out_states["swa_k"].append(kb.reshape(b, -1, SWA_KV_HEADS, SWA_HD))
            out_states["swa_v"].append(vb.reshape(b, -1, SWA_KV_HEADS, SWA_HD))
            w_out = w['odd_w_out'][i]
            x = _matmul_residual(x, [(c_out, w_out[:S5_WIDTH]), (d_out, w_out[S5_WIDTH:])], tm=tm)
        q = _norm_matmul(x, w['mem_x_norm'][layer][None], w['mem_w_q'][layer], tm=tm, tn=MEM_HD,
                         head_gain=w['mem_q_norm'][layer][None], n_norm_tiles=MEM_HEADS)
        if sample:
            o = _mem_sample(q, mem_k[layer], mem_v[layer], bs=4, ls=l)
        else:
            o = _mem_prompt(q.reshape(b, l, d), mem_k[layer], mem_v[layer], lt=512).reshape(t, d)
        x = _matmul_residual(x, [(o, w['mem_w_o'][layer])], tm=tm)
        x = _ffn(x, w['ffn2_norm'][layer][None], w['ffn2_w_gate'][layer], w['ffn2_w_up'][layer],
                 w['ffn2_w_down'][layer], tm=tm, th=256)
    return x.reshape(b, l, d), {k: jnp.stack(v) for k, v in out_states.items()}


def kernel(x_prompt, x_sample, mem_prompt, state_gla, state_ret, state_s5_re, state_s5_im, cache_swa_k, cache_swa_v, cache_mem_k, cache_mem_v, ffn1_norm, ffn1_w_gate, ffn1_w_up, ffn1_w_down, ffn2_norm, ffn2_w_gate, ffn2_w_up, ffn2_w_down, mix_norm, even_w_in, gla_w_gate, gla_b_gate, gla_out_norm, even_w_out, odd_w_in, s5_a_re, s5_a_im, s5_log_step, s5_b_re, s5_b_im, s5_c_re, s5_c_im, s5_d, s5_w_glu, s5_b_glu, swa_q_norm, swa_k_norm, swa_sinks, odd_w_out, mem_x_norm, mem_m_norm, mem_w_q, mem_w_k, mem_w_v, mem_w_o, mem_q_norm, mem_k_norm):
    depth = ffn1_norm.shape[0]
    n_even, n_odd = even_w_in.shape[0], odd_w_in.shape[0]
    batch, seq, d = x_prompt.shape
    dec_batch = x_sample.shape[0]
    n_mem = mem_prompt.shape[1]

    ev = even_w_in
    ev_cols = jnp.concatenate(
        [ev[..., 0:1536], ev[..., 1552:3088], ev[..., 1536:1552],
         jnp.zeros(ev.shape[:2] + (EV_COLS - 3088,), ev.dtype)], axis=-1)
    wgate_pad = jnp.concatenate(
        [gla_w_gate, jnp.zeros((n_even, EV_COLS - EV_GA - GLA_RANK, gla_w_gate.shape[-1]), gla_w_gate.dtype)], axis=1)
    w = dict(
        ffn1_norm=ffn1_norm, ffn2_norm=ffn2_norm, mix_norm=mix_norm, mem_x_norm=mem_x_norm,
        ffn1_w_gate=_bf(ffn1_w_gate), ffn1_w_up=_bf(ffn1_w_up), ffn1_w_down=_bf(ffn1_w_down),
        ffn2_w_gate=_bf(ffn2_w_gate), ffn2_w_up=_bf(ffn2_w_up), ffn2_w_down=_bf(ffn2_w_down),
        even_w_in=_bf(ev_cols), gla_w_gate=_bf(wgate_pad), gla_b_gate=gla_b_gate[:, None, :],
        gla_out_norm=gla_out_norm[:, None, :], even_w_out=_bf(even_w_out),
        odd_w_in=_bf(odd_w_in), odd_w_out=_bf(odd_w_out), s5_d=s5_d, s5_w_glu=_bf(s5_w_glu), s5_b_glu=s5_b_glu,
        swa_q_norm=swa_q_norm, swa_k_norm=swa_k_norm, swa_sinks=swa_sinks,
        mem_w_q=_bf(mem_w_q), mem_w_o=_bf(mem_w_o), mem_q_norm=mem_q_norm,
    )
    ab_re, ab_im, bb_re, bb_im = [], [], [], []
    for i in range(n_odd):
        a_r, a_i, b_r, b_i = _s5_prep(s5_a_re[i], s5_a_im[i], s5_log_step[i], s5_b_re[i], s5_b_im[i])
        ab_re.append(a_r.reshape(-1, 1, LANES))
        ab_im.append(a_i.reshape(-1, 1, LANES))
        bb_re.append(_bf(_block_diag(b_r)))
        bb_im.append(_bf(_block_diag(b_i)))
    w.update(s5_ab_re=ab_re, s5_ab_im=ab_im, s5_bb_re=bb_re, s5_bb_im=bb_im,
             s5_cc_re=[_bf(_block_diag(jnp.swapaxes(s5_c_re[i], 1, 2))) for i in range(n_odd)],
             s5_cc_im=[_bf(_block_diag(jnp.swapaxes(s5_c_im[i], 1, 2))) for i in range(n_odd)])

    mem2 = mem_prompt.reshape(batch * n_mem, d)
    p_mem_k, p_mem_v = [], []
    for layer in range(depth):
        w_kv = _bf(jnp.concatenate([mem_w_k[layer], mem_w_v[layer]], axis=1))
        kv = _norm_matmul(mem2, mem_m_norm[layer][None], w_kv, tm=batch * n_mem, tn=MEM_HD,
                          head_gain=mem_k_norm[layer][None], n_norm_tiles=MEM_HEADS)
        p_mem_k.append(kv[:, :d].reshape(batch, n_mem, d))
        p_mem_v.append(kv[:, d:].reshape(batch, n_mem, d))

    zeros = lambda *s: jnp.zeros(s, F32)
    p_states = (zeros(n_even, batch, GLA_HEADS, GLA_DK, GLA_DV), zeros(n_even, batch, RET_HEADS, RET_DK, GLA_DV),
                zeros(n_odd, batch, S5_GROUPS, S5_STATE), zeros(n_odd, batch, S5_GROUPS, S5_STATE), None, None)
    y_prompt, ps = _trunk(x_prompt, 0.0, p_states, p_mem_k, p_mem_v, w, sample=False)

    s_mem_k = [cache_mem_k[layer].reshape(dec_batch, n_mem, d) for layer in range(depth)]
    s_mem_v = [cache_mem_v[layer].reshape(dec_batch, n_mem, d) for layer in range(depth)]
    s_states = (state_gla, state_ret, state_s5_re, state_s5_im, cache_swa_k, cache_swa_v)
    y_sample, ss = _trunk(x_sample, float(PAST_LEN), s_states, s_mem_k, s_mem_v, w, sample=True)

    p_mem_k = jnp.stack(p_mem_k).reshape(depth, batch, n_mem, MEM_HEADS, MEM_HD)
    p_mem_v = jnp.stack(p_mem_v).reshape(depth, batch, n_mem, MEM_HEADS, MEM_HD)
    return (y_prompt, y_sample, ps["gla"], ps["ret"], ps["s5_re"], ps["s5_im"], ps["swa_k"], ps["swa_v"],
            p_mem_k, p_mem_v, ss["gla"], ss["ret"], ss["s5_re"], ss["s5_im"], ss["swa_k"], ss["swa_v"])
```

```python
import functools
import math

import jax
import jax.numpy as jnp
import numpy as np
from jax import lax
from jax.experimental import pallas as pl
from jax.experimental.pallas import tpu as pltpu

F32 = jnp.float32
BF16 = jnp.bfloat16
NORM_EPS = 1e-6
HIGHEST = lax.Precision.HIGHEST

D_MODEL = 1024
GLA_HEADS = 4
GLA_DK = 64
GLA_DV = 128
GLA_RANK = 16
GLA_TAU = 16.0
RET_HEADS = 4
RET_DK = 64
RET_THETA = 10000.0
LA_CHUNK = 64
S5_WIDTH = 512
S5_GROUP = 16
S5_GROUPS = 32
S5_STATE = 64
SWA_HD = 64
SWA_HEADS = 8
SWA_KV_HEADS = 2
SWA_WINDOW = 128
ROPE_THETA = 500000.0
MEM_HEADS = 4
MEM_HD = 256
PAST_LEN = 8192

VMEM_LIMIT_BYTES = 52 * 1024 * 1024
LANES = 128
SUBLANES = 8


def _params(*sem):
    return pltpu.CompilerParams(dimension_semantics=sem, vmem_limit_bytes=VMEM_LIMIT_BYTES)


def _rms(x, gain=None):
    y = x * lax.rsqrt(jnp.mean(x * x, axis=-1, keepdims=True) + NORM_EPS)
    return y if gain is None else y * gain


def _dot(a, b):
    return jnp.dot(a, b, preferred_element_type=F32)


def _dot_nt(a, b):
    return lax.dot_general(a, b, (((1,), (1,)), ((), ())), preferred_element_type=F32)


def _dot_tn(a, b):
    return lax.dot_general(a, b, (((0,), (0,)), ((), ())), preferred_element_type=F32)


def _dot_f32(a, b):
    return jnp.dot(a, b, precision=HIGHEST, preferred_element_type=F32)


def _dot_nt_f32(a, b):
    return lax.dot_general(a, b, (((1,), (1,)), ((), ())), precision=HIGHEST, preferred_element_type=F32)


def _bf(x):
    return x.astype(BF16)


def _log_sigmoid(x):
    return jnp.minimum(x, 0.0) - jnp.log1p(jnp.exp(-jnp.abs(x)))


def _iota(shape, dim):
    return lax.broadcasted_iota(jnp.int32, shape, dim)


def _swap_pairs(x):
    n = x.shape[-1]
    even = (_iota(x.shape, 1) & 1) == 0
    return jnp.where(even, pltpu.roll(x, n - 1, 1), pltpu.roll(x, 1, 1))


def _rope_partner(x, head_dim, half):
    n = x.shape[-1]
    first = (_iota(x.shape, 1) & (head_dim - 1)) < half
    return jnp.where(first, pltpu.roll(x, n - half, 1), pltpu.roll(x, half, 1))


def _ffn_kernel(x_ref, g_ref, wg_ref, wu_ref, wd_ref, o_ref, xn_ref, acc_ref):
    j = pl.program_id(1)

    @pl.when(j == 0)
    def _():
        xn_ref[...] = _bf(_rms(x_ref[...], g_ref[...]))
        acc_ref[...] = jnp.zeros_like(acc_ref)

    xn = xn_ref[...]
    gate = _dot(xn, wg_ref[...])
    up = _dot(xn, wu_ref[...])
    acc_ref[...] += _dot(_bf(jax.nn.silu(gate) * up), wd_ref[...])

    @pl.when(j == pl.num_programs(1) - 1)
    def _():
        o_ref[...] = x_ref[...] + 0.5 * acc_ref[...]


def _ffn(x, gain, wg, wu, wd, *, tm, th):
    t, d = x.shape
    h = wg.shape[1]
    return pl.pallas_call(
        _ffn_kernel,
        grid=(t // tm, h // th),
        in_specs=[
            pl.BlockSpec((tm, d), lambda i, j: (i, 0)),
            pl.BlockSpec((1, d), lambda i, j: (0, 0)),
            pl.BlockSpec((d, th), lambda i, j: (0, j)),
            pl.BlockSpec((d, th), lambda i, j: (0, j)),
            pl.BlockSpec((th, d), lambda i, j: (j, 0)),
        ],
        out_specs=pl.BlockSpec((tm, d), lambda i, j: (i, 0)),
        out_shape=jax.ShapeDtypeStruct((t, d), F32),
        scratch_shapes=[pltpu.VMEM((tm, d), BF16), pltpu.VMEM((tm, d), F32)],
        compiler_params=_params("parallel", "arbitrary"),
        name="ffn",
    )(x, gain, wg, wu, wd)


def _nmm_kernel(x_ref, g_ref, w_ref, hg_ref, o_ref, xn_ref, *, n_norm_tiles):
    j = pl.program_id(1)

    @pl.when(j == 0)
    def _():
        xn_ref[...] = _bf(_rms(x_ref[...], g_ref[...]))

    y = _dot(xn_ref[...], w_ref[...])
    if n_norm_tiles == 0:
        o_ref[...] = y
    else:
        @pl.when(j < n_norm_tiles)
        def _():
            o_ref[...] = _rms(y, hg_ref[...])

        @pl.when(j >= n_norm_tiles)
        def _():
            o_ref[...] = y


def _norm_matmul(x, gain, w, *, tm, tn, head_gain=None, n_norm_tiles=0):
    t, d = x.shape
    n = w.shape[1]
    if head_gain is None:
        head_gain = jnp.ones((1, tn), F32)
    return pl.pallas_call(
        functools.partial(_nmm_kernel, n_norm_tiles=n_norm_tiles),
        grid=(t // tm, n // tn),
        in_specs=[
            pl.BlockSpec((tm, d), lambda i, j: (i, 0)),
            pl.BlockSpec((1, d), lambda i, j: (0, 0)),
            pl.BlockSpec((d, tn), lambda i, j: (0, j)),
            pl.BlockSpec((1, tn), lambda i, j: (0, 0)),
        ],
        out_specs=pl.BlockSpec((tm, tn), lambda i, j: (i, j)),
        out_shape=jax.ShapeDtypeStruct((t, n), F32),
        scratch_shapes=[pltpu.VMEM((tm, d), BF16)],
        compiler_params=_params("parallel", "arbitrary"),
        name="norm_matmul",
    )(x, gain, w, head_gain)


def _mmr_kernel(*refs, n_terms):
    x_ref = refs[0]
    a_refs = refs[1:1 + n_terms]
    w_refs = refs[1 + n_terms:1 + 2 * n_terms]
    o_ref = refs[1 + 2 * n_terms]
    acc = x_ref[...]
    for a_ref, w_ref in zip(a_refs, w_refs):
        acc = acc + _dot(_bf(a_ref[...]), w_ref[...])
    o_ref[...] = acc


def _matmul_residual(x, terms, *, tm):
    t, d = x.shape
    acts = [a for a, _ in terms]
    ws = [w for _, w in terms]
    in_specs = [pl.BlockSpec((tm, d), lambda i: (i, 0))]
    in_specs += [pl.BlockSpec((tm, a.shape[1]), lambda i: (i, 0)) for a in acts]
    in_specs += [pl.BlockSpec(w.shape, lambda i: (0, 0)) for w in ws]
    return pl.pallas_call(
        functools.partial(_mmr_kernel, n_terms=len(terms)),
        grid=(t // tm,),
        in_specs=in_specs,
        out_specs=pl.BlockSpec((tm, d), lambda i: (i, 0)),
        out_shape=jax.ShapeDtypeStruct((t, d), F32),
        compiler_params=_params("parallel"),
        name="matmul_residual",
    )(x, *acts, *ws)


EV_GQ, EV_GK, EV_GV, EV_GG = 0, 256, 512, 1024
EV_RQ, EV_RK, EV_RV, EV_RG = 1536, 1792, 2048, 2560
EV_GA = 3072
EV_COLS = 3200
EV_BLOCK = 256
PAIR = 2


def _gate_and_norm(o, gate, gain=None):
    return _rms(o, gain) * jax.nn.silu(gate)


def _even_prompt_kernel(proj_ref, cos_ref, sin_ref, ld_ref, wgate_ref, bgate_ref, gnorm_ref, s0g_ref, s0r_ref,
                        mix_ref, sg_ref, sr_ref, o_sc, *, chunk, n_chunks):
    @pl.when(pl.program_id(1) == 0)
    def _():
        sg_ref[...] = s0g_ref[...]
        sr_ref[...] = s0r_ref[...]

    c = chunk
    lt = c * n_chunks
    blk = min(EV_BLOCK, lt)
    c_shift = c.bit_length() - 1

    def cols(a, b):
        return proj_ref[0, :, a:b]

    log_a = _log_sigmoid(_dot(_bf(cols(EV_GA, EV_COLS)), wgate_ref[...]) + bgate_ref[...]) * (1.0 / GLA_TAU)
    tril = (_iota((c, c), 1) <= _iota((c, c), 0)).astype(F32)
    cum_parts = [_dot_f32(tril, log_a[i * c:(i + 1) * c]) for i in range(n_chunks)]
    tots = [p[c - 1:c] for p in cum_parts]
    cum = jnp.concatenate(cum_parts, axis=0)
    tot_b = jnp.concatenate([jnp.broadcast_to(t, (c, 256)) for t in tots], axis=0)
    k = cols(EV_GK, EV_GK + 256)
    gla = (cols(EV_GQ, EV_GQ + 256) * (GLA_DK ** -0.5) * jnp.exp(cum), k * jnp.exp(-cum), k * jnp.exp(tot_b - cum))

    ld = ld_ref[...]
    tpos = ((_iota((lt, 1), 0) & (c - 1)) + 1).astype(F32)
    cum_r = tpos * ld
    tot_r = float(c) * ld
    cos, sin = cos_ref[...], sin_ref[...]
    rq = cols(EV_RQ, EV_RQ + 256)
    rk = cols(EV_RK, EV_RK + 256)
    q_rot = rq * cos + _swap_pairs(rq) * sin
    k_rot = (rk * cos + _swap_pairs(rk) * sin) * (RET_DK ** -0.5)
    ret = (q_rot * jnp.exp(cum_r), k_rot * jnp.exp(-cum_r), k_rot * jnp.exp(tot_r - cum_r))

    tot_rows = jnp.concatenate(tots + [tot_r, jnp.zeros((LANES - n_chunks - 1, 256), F32)], axis=0)
    decay_cols = jnp.exp(jnp.transpose(tot_rows))

    row = _iota((blk, blk), 0)
    col = _iota((blk, blk), 1)
    blk_mask = ((row >> c_shift) == (col >> c_shift)) & (col <= row)
    lo = (_iota((lt, LANES), 1) < GLA_DK)
    gnorm = gnorm_ref[...]

    mixers = ((gla, EV_GV, EV_GG, sg_ref, 0, gnorm, lambda i: i),
              (ret, EV_RV, EV_RG, sr_ref, 512, None, lambda i: n_chunks))
    for m, ((q_dec, k_inv, k_dec), v_col, g_col, s_ref, out_col, gain, decay_col_of) in enumerate(mixers):
        for p in range(GLA_HEADS // PAIR):
            lanes = slice(p * LANES, (p + 1) * LANES)
            q_pair = q_dec[:, lanes]
            q_masked = [_bf(jnp.where(lo, q_pair, 0.0)), _bf(jnp.where(lo, 0.0, q_pair))]
            ki = _bf(k_inv[:, lanes])
            kd = _bf(k_dec[:, lanes])
            v_pair = _bf(cols(v_col + p * PAIR * GLA_DV, v_col + (p + 1) * PAIR * GLA_DV))
            for e in range(PAIR):
                slot = (m * (GLA_HEADS // PAIR) + p) * PAIR + e
                for r0 in range(0, lt, blk):
                    rs = slice(r0, r0 + blk)
                    scores = jnp.where(blk_mask, _dot_nt(q_masked[e][rs], ki[rs]), 0.0)
                    o_sc[slot, rs, :] = _dot(_bf(scores), v_pair[rs, e * GLA_DV:(e + 1) * GLA_DV])
            state = s_ref[0, p * PAIR:(p + 1) * PAIR].reshape(PAIR * GLA_DK, GLA_DV)
            for i in range(n_chunks):
                rs = slice(i * c, (i + 1) * c)
                q_stack = jnp.concatenate([q_masked[0][rs], q_masked[1][rs]], axis=0)
                o_inter = _dot(q_stack, _bf(state))
                kv = _dot_tn(kd[rs], v_pair[rs])
                kv = jnp.concatenate([kv[:GLA_DK, :GLA_DV], kv[GLA_DK:, GLA_DV:]], axis=0)
                ci = decay_col_of(i)
                state = state * decay_cols[p * LANES:(p + 1) * LANES, ci:ci + 1] + kv
                for e in range(PAIR):
                    h = p * PAIR + e
                    slot = (m * (GLA_HEADS // PAIR) + p) * PAIR + e
                    o = o_sc[slot, rs, :] + o_inter[e * c:(e + 1) * c]
                    gate = proj_ref[0, rs, g_col + h * GLA_DV:g_col + (h + 1) * GLA_DV]
                    mix_ref[0, rs, out_col + h * GLA_DV:out_col + (h + 1) * GLA_DV] = _gate_and_norm(o, gate, gain)
            s_ref[0, p * PAIR:(p + 1) * PAIR] = state.reshape(PAIR, GLA_DK, GLA_DV)


def _even_prompt(proj, cos, sin, ld_row, wgate, bgate, gnorm, s0g, s0r, *, lt):
    b, l, _ = proj.shape
    chunk = math.gcd(l, LA_CHUNK)
    st_spec = pl.BlockSpec((1, GLA_HEADS, GLA_DK, GLA_DV), lambda i, j: (i, 0, 0, 0))
    full = lambda a: pl.BlockSpec(a.shape, lambda i, j: (0,) * a.ndim)
    return pl.pallas_call(
        functools.partial(_even_prompt_kernel, chunk=chunk, n_chunks=lt // chunk),
        grid=(b, l // lt),
        in_specs=[
            pl.BlockSpec((1, lt, EV_COLS), lambda i, j: (i, j, 0)),
            pl.BlockSpec((lt, 256), lambda i, j: (j, 0)),
            pl.BlockSpec((lt, 256), lambda i, j: (j, 0)),
            full(ld_row), full(wgate), full(bgate), full(gnorm), st_spec, st_spec,
        ],
        out_specs=[pl.BlockSpec((1, lt, D_MODEL), lambda i, j: (i, j, 0)), st_spec, st_spec],
        out_shape=[jax.ShapeDtypeStruct((b, l, D_MODEL), F32),
                   jax.ShapeDtypeStruct(s0g.shape, F32), jax.ShapeDtypeStruct(s0r.shape, F32)],
        scratch_shapes=[pltpu.VMEM((GLA_HEADS + RET_HEADS, lt, GLA_DV), F32)],
        compiler_params=_params("parallel", "arbitrary"),
        name="even_prompt",
    )(proj, cos, sin, ld_row, wgate, bgate, gnorm, s0g, s0r)


def _even_sample_kernel(proj_ref, cos_ref, sin_ref, ld_ref, wgate_ref, bgate_ref, gnorm_ref, s0g_ref, s0r_ref,
                        mix_ref, sg_ref, sr_ref, *, bg, ls):
    r = bg * ls
    ls_shift = ls.bit_length() - 1
    dk_shift = GLA_DK.bit_length() - 1
    n_exp = bg * GLA_DK
    row_seq = _iota((r, r), 0) >> ls_shift
    col_seq = _iota((r, r), 1) >> ls_shift
    same = row_seq == col_seq
    seg = same & (_iota((r, r), 1) <= _iota((r, r), 0))
    seg_f = seg.astype(F32)
    same_f = same.astype(F32)
    tile_b = _bf(((_iota((GLA_DK, n_exp), 1) & (GLA_DK - 1)) == _iota((GLA_DK, n_exp), 0)).astype(F32))
    tile_t_f = ((_iota((n_exp, GLA_DK), 0) & (GLA_DK - 1)) == _iota((n_exp, GLA_DK), 1)).astype(F32)
    tile_t_b = _bf(tile_t_f)
    q_mask = (_iota((r, n_exp), 0) >> ls_shift) == (_iota((r, n_exp), 1) >> dk_shift)
    k_mask = (_iota((n_exp, r), 0) >> dk_shift) == (_iota((n_exp, r), 1) >> ls_shift)
    tpos = ((_iota((r, 1), 0) & (ls - 1)) + 1).astype(F32)
    ld = ld_ref[...]
    cum_r = tpos * ld
    tot_r = float(ls) * ld
    r_dec = jnp.exp(tot_r)
    gnorm = gnorm_ref[...]

    def mixer(q_dec, k_inv, k_dec, v_col, g_col, s0_ref, s_ref, out_col, decay_of, gain):
        for h in range(GLA_HEADS):
            sl = slice(h * GLA_DK, (h + 1) * GLA_DK)
            v = _bf(proj_ref[:, v_col + h * GLA_DV:v_col + (h + 1) * GLA_DV])
            qd = _bf(q_dec[:, sl])
            scores = jnp.where(seg, _dot_nt(qd, _bf(k_inv[:, sl])), 0.0)
            state = s0_ref[:, h].reshape(n_exp, GLA_DV)
            q_exp = _bf(jnp.where(q_mask, _dot(qd, tile_b), 0.0))
            o = _dot(_bf(scores), v) + _dot(q_exp, _bf(state))
            k_exp = _bf(jnp.where(k_mask, _dot_nt(tile_t_b, _bf(k_dec[:, sl])), 0.0))
            new_state = state * decay_of(h, sl) + _dot(k_exp, v)
            s_ref[:, h] = new_state.reshape(bg, GLA_DK, GLA_DV)
            gate = proj_ref[:, g_col + h * GLA_DV:g_col + (h + 1) * GLA_DV]
            mix_ref[:, out_col + h * GLA_DV:out_col + (h + 1) * GLA_DV] = _gate_and_norm(o, gate, gain)

    log_a = _log_sigmoid(_dot(_bf(proj_ref[:, EV_GA:EV_COLS]), wgate_ref[...]) + bgate_ref[...]) * (1.0 / GLA_TAU)
    cum = _dot_f32(seg_f, log_a)
    tot = _dot_f32(same_f, log_a)
    k = proj_ref[:, EV_GK:EV_GK + 256]

    def gla_decay(h, sl):
        la_exp = jnp.where(k_mask, _dot_nt_f32(tile_t_f, log_a[:, sl]), 0.0)
        return jnp.exp(jnp.sum(la_exp, axis=-1, keepdims=True))

    mixer(proj_ref[:, EV_GQ:EV_GQ + 256] * (GLA_DK ** -0.5) * jnp.exp(cum), k * jnp.exp(-cum), k * jnp.exp(tot - cum),
          EV_GV, EV_GG, s0g_ref, sg_ref, 0, gla_decay, gnorm)

    cos = cos_ref[...]
    sin = sin_ref[...]
    rq = proj_ref[:, EV_RQ:EV_RQ + 256]
    rk = proj_ref[:, EV_RK:EV_RK + 256]
    q_rot = rq * cos + _swap_pairs(rq) * sin
    k_rot = (rk * cos + _swap_pairs(rk) * sin) * (RET_DK ** -0.5)

    def ret_decay(h, sl):
        return r_dec[:, h * RET_DK:h * RET_DK + 1]

    mixer(q_rot * jnp.exp(cum_r), k_rot * jnp.exp(-cum_r), k_rot * jnp.exp(tot_r - cum_r),
          EV_RV, EV_RG, s0r_ref, sr_ref, 512, ret_decay, None)


def _even_sample(proj, cos, sin, ld_row, wgate, bgate, gnorm, s0g, s0r, *, bg, ls):
    t = proj.shape[0]
    n_b = t // ls
    r = bg * ls
    st_spec = pl.BlockSpec((bg, GLA_HEADS, GLA_DK, GLA_DV), lambda i: (i, 0, 0, 0))
    full = lambda a: pl.BlockSpec(a.shape, lambda i: (0,) * a.ndim)
    return pl.pallas_call(
        functools.partial(_even_sample_kernel, bg=bg, ls=ls),
        grid=(n_b // bg,),
        in_specs=[
            pl.BlockSpec((r, EV_COLS), lambda i: (i, 0)),
            full(cos), full(sin), full(ld_row), full(wgate), full(bgate), full(gnorm), st_spec, st_spec,
        ],
        out_specs=[pl.BlockSpec((r, D_MODEL), lambda i: (i, 0)), st_spec, st_spec],
        out_shape=[jax.ShapeDtypeStruct((t, D_MODEL), F32),
                   jax.ShapeDtypeStruct(s0g.shape, F32), jax.ShapeDtypeStruct(s0r.shape, F32)],
        compiler_params=_params("parallel"),
        name="even_sample",
    )(proj, cos, sin, ld_row, wgate, bgate, gnorm, s0g, s0r)


def _s5_prep_kernel(are_ref, aim_ref, lstep_ref, bre_ref, bim_ref, abre_ref, abim_ref, bbre_ref, bbim_ref):
    a_re, a_im = are_ref[...], aim_ref[...]
    step = jnp.exp(lstep_ref[...])
    mag = jnp.exp(a_re * step)
    ab_re = mag * jnp.cos(a_im * step)
    ab_im = mag * jnp.sin(a_im * step)
    den = a_re * a_re + a_im * a_im
    coef_re = ((ab_re - 1.0) * a_re + ab_im * a_im) / den
    coef_im = (ab_im * a_re - (ab_re - 1.0) * a_im) / den
    b_re, b_im = bre_ref[...], bim_ref[...]
    abre_ref[...] = ab_re
    abim_ref[...] = ab_im
    bbre_ref[...] = coef_re * b_re - coef_im * b_im
    bbim_ref[...] = coef_re * b_im + coef_im * b_re


def _s5_prep(a_re, a_im, log_step, b_re, b_im):
    g, n = a_re.shape
    shp3 = jax.ShapeDtypeStruct((g, 1, n), F32)
    shpb = jax.ShapeDtypeStruct((g, S5_GROUP, n), F32)
    return pl.pallas_call(_s5_prep_kernel, out_shape=[shp3, shp3, shpb, shpb], name="s5_prep")(
        a_re.reshape(g, 1, n), a_im.reshape(g, 1, n), log_step.reshape(g, 1, 1),
        jnp.swapaxes(b_re, 1, 2), jnp.swapaxes(b_im, 1, 2))


S5_LANE_CHUNKS = S5_GROUPS * S5_STATE // LANES
S5_HALVES = 2


def _s5_layout(n_seq, lt):
    pack = max(1, SUBLANES // n_seq)
    pitch = lt + 4 if lt % SUBLANES == 0 else lt
    return pack, S5_LANE_CHUNKS // pack, pitch


def _s5_kernel(u_ref, s0re_ref, s0im_ref, abre_ref, abim_ref, bbre_ref, bbim_ref, ccre_ref, ccim_ref,
               d_ref, wglu_ref, bglu_ref, out_ref, sre_ref, sim_ref, xr_ref, xi_ref, hre_ref, him_ref,
               *, n_seq, lt, groups_per_pass, unroll):
    pack, n_groups, pitch = _s5_layout(n_seq, lt)
    rows = n_seq * lt
    per_half = S5_LANE_CHUNKS // S5_HALVES
    lanes_of = lambda c: slice(c * LANES, (c + 1) * LANES)
    slot_of = lambda c: (c % n_groups, c // n_groups)

    @pl.when(pl.program_id(0) == 0)
    def _():
        for c in range(S5_LANE_CHUNKS):
            g, j = slot_of(c)
            hre_ref[g, j * n_seq:(j + 1) * n_seq, :] = s0re_ref[:, lanes_of(c)]
            him_ref[g, j * n_seq:(j + 1) * n_seq, :] = s0im_ref[:, lanes_of(c)]

    def seq_rows(j, s):
        return slice((j * n_seq + s) * pitch, (j * n_seq + s) * pitch + lt)

    u = u_ref[...].reshape(rows, S5_WIDTH)
    ub = _bf(u)
    u_cols = S5_WIDTH // S5_HALVES
    for half in range(S5_HALVES):
        uh = ub[:, half * u_cols:(half + 1) * u_cols]
        for x_ref, bb_ref in ((xr_ref, bbre_ref), (xi_ref, bbim_ref)):
            x = _dot(uh, bb_ref[half])
            for k in range(per_half):
                g, j = slot_of(half * per_half + k)
                if pitch == lt:
                    x_ref[g, j * rows:(j + 1) * rows, :] = x[:, lanes_of(k)]
                else:
                    for s in range(n_seq):
                        x_ref[g, seq_rows(j, s), :] = x[s * lt:(s + 1) * lt, lanes_of(k)]

    for g0 in range(0, n_groups, groups_per_pass):
        gs = list(range(g0, g0 + groups_per_pass))
        init = tuple(hre_ref[g] for g in gs) + tuple(him_ref[g] for g in gs)

        def step(t, carry, gs=gs):
            rws = pl.ds(t, pack * n_seq, stride=pitch)
            new_re, new_im = [], []
            for k, g in enumerate(gs):
                a_re, a_im = abre_ref[g], abim_ref[g]
                h_re, h_im = carry[k], carry[len(gs) + k]
                n_re = a_re * h_re - a_im * h_im + xr_ref[g, rws, :]
                n_im = a_re * h_im + a_im * h_re + xi_ref[g, rws, :]
                xr_ref[g, rws, :] = n_re
                xi_ref[g, rws, :] = n_im
                new_re.append(n_re)
                new_im.append(n_im)
            return tuple(new_re + new_im)

        fin = lax.fori_loop(0, lt, step, init, unroll=unroll)
        for k, g in enumerate(gs):
            hre_ref[g] = fin[k]
            him_ref[g] = fin[len(gs) + k]

    def gather(x_ref, half):
        cols = []
        for k in range(per_half):
            g, j = slot_of(half * per_half + k)
            if pitch == lt:
                cols.append(x_ref[g, j * rows:(j + 1) * rows, :])
            else:
                cols.append(jnp.concatenate([x_ref[g, seq_rows(j, s), :] for s in range(n_seq)], axis=0))
        return _bf(jnp.concatenate(cols, axis=1))

    y = jnp.concatenate([_dot(gather(xr_ref, half), ccre_ref[half]) - _dot(gather(xi_ref, half), ccim_ref[half])
                         for half in range(S5_HALVES)], axis=1) + d_ref[...] * u
    z = jax.nn.gelu(y, approximate=True)
    out = z * jax.nn.sigmoid(_dot(_bf(z), wglu_ref[...]) + bglu_ref[...])
    out_ref[...] = out.reshape(out_ref.shape)

    @pl.when(pl.program_id(0) == pl.num_programs(0) - 1)
    def _():
        for c in range(S5_LANE_CHUNKS):
            g, j = slot_of(c)
            sre_ref[:, lanes_of(c)] = hre_ref[g, j * n_seq:(j + 1) * n_seq, :]
            sim_ref[:, lanes_of(c)] = him_ref[g, j * n_seq:(j + 1) * n_seq, :]


def _s5_tables(ab, n_seq, lt):
    pack, n_groups, _ = _s5_layout(n_seq, lt)
    tab = jnp.swapaxes(ab.reshape(pack, n_groups, 1, LANES), 0, 1)
    return jnp.broadcast_to(tab, (n_groups, pack, n_seq, LANES)).reshape(n_groups, pack * n_seq, LANES)


def _s5(proj3, s0_re, s0_im, ab_re, ab_im, bb_re, bb_im, cc_re, cc_im, d_row, wglu, bglu, *, n_seq, lt,
        groups_per_pass, unroll):
    nb, rows_b, _ = proj3.shape
    blk_rows = n_seq * lt // nb
    pack, n_groups, pitch = _s5_layout(n_seq, lt)
    ab_re, ab_im = _s5_tables(ab_re, n_seq, lt), _s5_tables(ab_im, n_seq, lt)
    full = lambda a: pl.BlockSpec(a.shape, lambda j: (0,) * a.ndim)
    io_spec = pl.BlockSpec((nb, blk_rows, S5_WIDTH), lambda j: (0, j, 0))
    x_scratch = pltpu.VMEM((n_groups, pack * n_seq * pitch, LANES), F32)
    h_scratch = pltpu.VMEM((n_groups, pack * n_seq, LANES), F32)
    return pl.pallas_call(
        functools.partial(_s5_kernel, n_seq=n_seq, lt=lt, groups_per_pass=groups_per_pass, unroll=unroll),
        grid=(rows_b // blk_rows,),
        in_specs=[io_spec, full(s0_re), full(s0_im), full(ab_re), full(ab_im), full(bb_re), full(bb_im),
                  full(cc_re), full(cc_im), full(d_row), full(wglu), full(bglu)],
        out_specs=[io_spec, full(s0_re), full(s0_im)],
        out_shape=[jax.ShapeDtypeStruct((nb, rows_b, S5_WIDTH), F32),
                   jax.ShapeDtypeStruct(s0_re.shape, F32), jax.ShapeDtypeStruct(s0_im.shape, F32)],
        scratch_shapes=[x_scratch, x_scratch, h_scratch, h_scratch],
        compiler_params=_params("arbitrary"),
        name="s5",
    )(proj3, s0_re, s0_im, ab_re, ab_im, bb_re, bb_im, cc_re, cc_im, d_row, wglu, bglu)


OD_Q_BLOCK = 1
OD_KV_BLOCK = 4
OD_COLS = 1280
ROPE_HALF = SWA_HD // 8
SWA_GROUP = SWA_HEADS // SWA_KV_HEADS


def _half_lanes(shape):
    return (_iota(shape, 1) & (LANES - 1)) < SWA_HD


def _pair_rms_scale(x):
    lo = _half_lanes(x.shape)
    x2 = x * x
    s_lo = jnp.sum(jnp.where(lo, x2, 0.0), axis=-1, keepdims=True)
    s_hi = jnp.sum(jnp.where(lo, 0.0, x2), axis=-1, keepdims=True)
    inv = 1.0 / SWA_HD
    return jnp.where(lo, lax.rsqrt(s_lo * inv + NORM_EPS), lax.rsqrt(s_hi * inv + NORM_EPS))


def _swa_qk(xq, xk, qg, kg, cq, sq):
    xq_g = xq * qg
    q_rot = xq_g * cq + _rope_partner(xq_g, SWA_HD, ROPE_HALF) * sq
    xk_g = xk * kg
    k_rot = xk_g * cq[:, :LANES] + _rope_partner(xk_g, SWA_HD, ROPE_HALF) * sq[:, :LANES]
    q_pairs = [q_rot[:, j * LANES:(j + 1) * LANES] * _pair_rms_scale(xq[:, j * LANES:(j + 1) * LANES])
               for j in range(SWA_HEADS // 2)]
    return q_pairs, k_rot * _pair_rms_scale(xk)


def _swa_query_stack(q_pairs, kh):
    lo = _half_lanes(q_pairs[0].shape)
    keep = lo if kh == 0 else jnp.logical_not(lo)
    parts = []
    for g in range(SWA_GROUP):
        hq = kh * SWA_GROUP + g
        pair = q_pairs[hq // 2]
        src = pair if hq % 2 == kh else pltpu.roll(pair, SWA_HD, 1)
        parts.append(jnp.where(keep, src, 0.0))
    return _bf(jnp.concatenate(parts, axis=0))


def _swa_merge_heads(o, kh, rows):
    lo = _half_lanes((rows, LANES))
    pairs = []
    for p in range(SWA_GROUP // 2):
        even, odd = o[2 * p * rows:(2 * p + 1) * rows], o[(2 * p + 1) * rows:(2 * p + 2) * rows]
        if kh == 0:
            pairs.append(jnp.where(lo, even, pltpu.roll(odd, SWA_HD, 1)))
        else:
            pairs.append(jnp.where(lo, pltpu.roll(even, SWA_HD, 1), odd))
    return pairs


def _swa_prompt_kernel(q_ref, kv_ref, cq_ref, sq_ref, qg_ref, kg_ref, sink_ref,
                       o_ref, ck_ref, cv_ref, kprev_ref, vprev_ref, *, n_blocks):
    w = SWA_WINDOW
    lstep = pl.program_id(1)

    @pl.when(lstep == 0)
    def _():
        kprev_ref[...] = jnp.zeros_like(kprev_ref)
        vprev_ref[...] = jnp.zeros_like(vprev_ref)

    rows_g = SWA_GROUP * w
    t_idx = _iota((rows_g, 2 * w), 0) & (w - 1)
    s_idx = _iota((rows_g, 2 * w), 1)
    band = (s_idx > t_idx) & (s_idx <= t_idx + w)
    k_prev = kprev_ref[...]
    v_prev = vprev_ref[...]
    for jb in range(n_blocks):
        rs = slice(jb * w, (jb + 1) * w)
        v_cur = kv_ref[0, rs, LANES:2 * LANES]
        q_pairs, k_cur = _swa_qk(q_ref[0, rs, :], kv_ref[0, rs, 0:LANES], qg_ref[...], kg_ref[...],
                                 cq_ref[rs, :], sq_ref[rs, :])
        mask = band & (s_idx >= jnp.where(lstep == 0, w, 0)) if jb == 0 else band
        k_ext = _bf(jnp.concatenate([k_prev, k_cur], axis=0))
        v_ext = _bf(jnp.concatenate([v_prev, v_cur], axis=0))
        for kh in range(SWA_KV_HEADS):
            s = jnp.where(mask, _dot_nt(_swa_query_stack(q_pairs, kh), k_ext) * (SWA_HD ** -0.5), -jnp.inf)
            sink = sink_ref[kh * rows_g:(kh + 1) * rows_g, :]
            m = jnp.maximum(jnp.max(s, axis=-1, keepdims=True), sink)
            e = jnp.exp(s - m)
            p = e / (jnp.sum(e, axis=-1, keepdims=True) + jnp.exp(sink - m))
            for i, pair in enumerate(_swa_merge_heads(_dot(_bf(p), v_ext), kh, w)):
                col = kh * (SWA_GROUP // 2) + i
                o_ref[0, rs, col * LANES:(col + 1) * LANES] = pair
        k_prev, v_prev = k_cur, v_cur
    kprev_ref[...] = k_prev
    vprev_ref[...] = v_prev

    @pl.when(lstep == pl.num_programs(1) - 1)
    def _():
        ck_ref[0] = k_prev
        cv_ref[0] = v_prev


def _swa_prompt(proj, cq, sq, qg, kg, sink_col, *, lt):
    b, l, _ = proj.shape
    w = SWA_WINDOW
    full = lambda a: pl.BlockSpec(a.shape, lambda i, j: (0,) * a.ndim)
    cache_spec = pl.BlockSpec((1, w, LANES), lambda i, j: (i, 0, 0))
    return pl.pallas_call(
        functools.partial(_swa_prompt_kernel, n_blocks=lt // w),
        grid=(b, l // lt),
        in_specs=[
            pl.BlockSpec((1, lt, 512), lambda i, j: (i, j, OD_Q_BLOCK)),
            pl.BlockSpec((1, lt, 256), lambda i, j: (i, j, OD_KV_BLOCK)),
            pl.BlockSpec((lt, 512), lambda i, j: (j, 0)),
            pl.BlockSpec((lt, 512), lambda i, j: (j, 0)),
            full(qg), full(kg), full(sink_col),
        ],
        out_specs=[pl.BlockSpec((1, lt, 512), lambda i, j: (i, j, 0)), cache_spec, cache_spec],
        out_shape=[jax.ShapeDtypeStruct((b, l, 512), F32),
                   jax.ShapeDtypeStruct((b, w, LANES), F32), jax.ShapeDtypeStruct((b, w, LANES), F32)],
        scratch_shapes=[pltpu.VMEM((w, LANES), F32), pltpu.VMEM((w, LANES), F32)],
        compiler_params=_params("parallel", "arbitrary"),
        name="swa_prompt",
    )(proj, proj, cq, sq, qg, kg, sink_col)


def _swa_sample_kernel(q_ref, kv_ref, ck_ref, cv_ref, cq_ref, sq_ref, qg_ref, kg_ref, sink_ref,
                       o_ref, nk_ref, nv_ref, *, bg, ls):
    w = SWA_WINDOW
    r = bg * ls
    ls_shift = ls.bit_length() - 1
    w_shift = w.bit_length() - 1
    rows_g = SWA_GROUP * r
    v_new = kv_ref[:, LANES:2 * LANES]
    q_pairs, k_new = _swa_qk(q_ref[...], kv_ref[:, 0:LANES], qg_ref[...], kg_ref[...], cq_ref[...], sq_ref[...])
    k_cache = _bf(ck_ref[...].reshape(bg * w, LANES))
    v_cache = _bf(cv_ref[...].reshape(bg * w, LANES))

    row = _iota((rows_g, bg * w), 0) & (r - 1)
    col = _iota((rows_g, bg * w), 1)
    mask_c = ((row >> ls_shift) == (col >> w_shift)) & ((col & (w - 1)) > (row & (ls - 1)))
    row_n = _iota((rows_g, r), 0) & (r - 1)
    col_n = _iota((rows_g, r), 1)
    mask_n = ((row_n >> ls_shift) == (col_n >> ls_shift)) & ((col_n & (ls - 1)) <= (row_n & (ls - 1)))
    for kh in range(SWA_KV_HEADS):
        q_stack = _swa_query_stack(q_pairs, kh)
        s_c = jnp.where(mask_c, _dot_nt(q_stack, k_cache) * (SWA_HD ** -0.5), -jnp.inf)
        s_n = jnp.where(mask_n, _dot_nt(q_stack, _bf(k_new)) * (SWA_HD ** -0.5), -jnp.inf)
        sink = sink_ref[kh * rows_g:(kh + 1) * rows_g, :]
        m = jnp.maximum(jnp.maximum(jnp.max(s_c, axis=-1, keepdims=True), jnp.max(s_n, axis=-1, keepdims=True)), sink)
        e_c = jnp.exp(s_c - m)
        e_n = jnp.exp(s_n - m)
        den = jnp.sum(e_c, axis=-1, keepdims=True) + jnp.sum(e_n, axis=-1, keepdims=True) + jnp.exp(sink - m)
        o = _dot(_bf(e_c / den), v_cache) + _dot(_bf(e_n / den), _bf(v_new))
        for i, pair in enumerate(_swa_merge_heads(o, kh, r)):
            col_i = kh * (SWA_GROUP // 2) + i
            o_ref[:, col_i * LANES:(col_i + 1) * LANES] = pair

    nk_ref[:, 0:w - ls, :] = ck_ref[:, ls:w, :]
    nv_ref[:, 0:w - ls, :] = cv_ref[:, ls:w, :]
    for b in range(bg):
        nk_ref[b, w - ls:w, :] = k_new[b * ls:(b + 1) * ls, :]
        nv_ref[b, w - ls:w, :] = v_new[b * ls:(b + 1) * ls, :]


def _swa_sample(proj, cache_k, cache_v, cq, sq, qg, kg, sink_col, *, bg, ls):
    t = proj.shape[0]
    w = SWA_WINDOW
    r = bg * ls
    full = lambda a: pl.BlockSpec(a.shape, lambda i: (0,) * a.ndim)
    cache_spec = pl.BlockSpec((bg, w, LANES), lambda i: (i, 0, 0))
    return pl.pallas_call(
        functools.partial(_swa_sample_kernel, bg=bg, ls=ls),
        grid=(t // r,),
        in_specs=[
            pl.BlockSpec((r, 512), lambda i: (i, OD_Q_BLOCK)),
            pl.BlockSpec((r, 256), lambda i: (i, OD_KV_BLOCK)),
            cache_spec, cache_spec, full(cq), full(sq), full(qg), full(kg), full(sink_col),
        ],
        out_specs=[pl.BlockSpec((r, 512), lambda i: (i, 0)), cache_spec, cache_spec],
        out_shape=[jax.ShapeDtypeStruct((t, 512), F32),
                   jax.ShapeDtypeStruct(cache_k.shape, F32), jax.ShapeDtypeStruct(cache_v.shape, F32)],
        compiler_params=_params("parallel"),
        name="swa_sample",
    )(proj, proj, cache_k, cache_v, cq, sq, qg, kg, sink_col)


def _mem_prompt_kernel(q_ref, k_ref, v_ref, o_ref):
    for h in range(MEM_HEADS):
        sl = slice(h * MEM_HD, (h + 1) * MEM_HD)
        s = _dot_nt(_bf(q_ref[0, :, sl]), _bf(k_ref[0, :, sl])) * (MEM_HD ** -0.5)
        e = jnp.exp(s - jnp.max(s, axis=-1, keepdims=True))
        p = e / jnp.sum(e, axis=-1, keepdims=True)
        o_ref[0, :, sl] = _dot(_bf(p), _bf(v_ref[0, :, sl]))


def _mem_prompt(q, k, v, *, lt):
    b, l, d = q.shape
    n_mem = k.shape[1]
    kv_spec = pl.BlockSpec((1, n_mem, d), lambda i, j: (i, 0, 0))
    io_spec = pl.BlockSpec((1, lt, d), lambda i, j: (i, j, 0))
    return pl.pallas_call(
        _mem_prompt_kernel,
        grid=(b, l // lt),
        in_specs=[io_spec, kv_spec, kv_spec],
        out_specs=io_spec,
        out_shape=jax.ShapeDtypeStruct(q.shape, F32),
        compiler_params=_params("parallel", "arbitrary"),
        name="mem_prompt",
    )(q, k, v)


def _mem_sample_kernel(q_ref, k_ref, v_ref, o_ref, *, bs, ls):
    r = bs * ls
    n_rows = MEM_HEADS * ls
    hd_shift = MEM_HD.bit_length() - 1
    ls_shift = ls.bit_length() - 1
    qb = _bf(q_ref[...])
    head_mask = (_iota((n_rows, D_MODEL), 0) >> ls_shift) == (_iota((n_rows, D_MODEL), 1) >> hd_shift)
    acc = jnp.zeros((r, D_MODEL), F32)
    for b in range(bs):
        sel = _bf(((_iota((n_rows, r), 0) & (ls - 1)) + b * ls == _iota((n_rows, r), 1)).astype(F32))
        q_exp = _bf(jnp.where(head_mask, _dot(sel, qb), 0.0))
        s = _dot_nt(_bf(k_ref[b]), q_exp) * (MEM_HD ** -0.5)
        e = jnp.exp(s - jnp.max(s, axis=0, keepdims=True))
        p = e / jnp.sum(e, axis=0, keepdims=True)
        o_all = jnp.where(head_mask, _dot_tn(_bf(p), _bf(v_ref[b])), 0.0)
        acc = acc + _dot_tn(sel, _bf(o_all))
    o_ref[...] = acc


def _mem_sample(q, k, v, *, layer, bs, ls):
    t, d = q.shape
    n_mem = k.shape[2]
    r = bs * ls
    kv_spec = pl.BlockSpec((None, bs, n_mem, d), lambda i: (layer, i, 0, 0))
    io_spec = pl.BlockSpec((r, d), lambda i: (i, 0))
    return pl.pallas_call(
        functools.partial(_mem_sample_kernel, bs=bs, ls=ls),
        grid=(t // r,),
        in_specs=[io_spec, kv_spec, kv_spec],
        out_specs=io_spec,
        out_shape=jax.ShapeDtypeStruct(q.shape, F32),
        compiler_params=_params("parallel"),
        name="mem_sample",
    )(q, k, v)


def _retention_tables(pos):
    inv = 1.0 / (RET_THETA ** jnp.linspace(0.0, 1.0, RET_DK // 2, dtype=F32))
    ang = pos[:, None] * inv[None, :]
    cos = jnp.repeat(jnp.cos(ang), 2, axis=1)
    sin = jnp.stack([-jnp.sin(ang), jnp.sin(ang)], axis=-1).reshape(pos.shape[0], RET_DK)
    return jnp.tile(cos, (1, RET_HEADS)), jnp.tile(sin, (1, RET_HEADS))


def _rope_tables(pos):
    half = ROPE_HALF
    inv = 1.0 / (ROPE_THETA ** (jnp.arange(half, dtype=F32) * 2.0 / (2 * half)))
    ang = pos[:, None] * inv[None, :]
    n = pos.shape[0]
    rest = SWA_HD - 2 * half
    cos = jnp.concatenate([jnp.cos(ang), jnp.cos(ang), jnp.ones((n, rest), F32)], axis=1)
    sin = jnp.concatenate([-jnp.sin(ang), jnp.sin(ang), jnp.zeros((n, rest), F32)], axis=1)
    return jnp.tile(cos, (1, SWA_HEADS)), jnp.tile(sin, (1, SWA_HEADS))


def _block_diag(t):
    g, a, b = t.shape
    eye = jnp.eye(g, dtype=t.dtype)
    return (t[:, :, None, :] * eye[:, None, :, None]).reshape(g * a, g * b)


def _half_block_diag(t):
    per = t.shape[0] // S5_HALVES
    return _bf(jnp.stack([_block_diag(t[h * per:(h + 1) * per]) for h in range(S5_HALVES)]))


def _sink_column(sinks, rows):
    return jnp.repeat(sinks.astype(F32), rows)[:, None]


def _trunk(x3, pos0, states, mem_k, mem_v, w, *, sample):
    b, l, d = x3.shape
    t = b * l
    x = x3.reshape(t, d)
    pos = pos0 + jnp.arange(l, dtype=F32)
    tm = 512 if sample else 1024
    gla_s, ret_s, s5_re, s5_im, swa_k, swa_v = states
    out_states = {k: [] for k in ("gla", "ret", "s5_re", "s5_im", "swa_k", "swa_v")}
    ld_row = jnp.repeat(jnp.log(1.0 - 2.0 ** (-5.0 - jnp.arange(RET_HEADS, dtype=F32))), RET_DK)[None, :]

    for layer in range(2):
        i = layer // 2
        x = _ffn(x, w['ffn1_norm'][layer][None], w['ffn1_w_gate'][layer], w['ffn1_w_up'][layer],
                 w['ffn1_w_down'][layer], tm=tm, th=256)
        if layer % 2 == 0:
            proj = _norm_matmul(x, w['mix_norm'][layer][None], w['even_w_in'][i], tm=tm, tn=640)
            cos, sin = _retention_tables(pos)
            args = (ld_row, w['gla_w_gate'][i], w['gla_b_gate'][i], w['gla_out_norm'][i])
            if sample:
                bg = 16
                mixed, g_s, r_s = _even_sample(proj, jnp.tile(cos, (bg, 1)), jnp.tile(sin, (bg, 1)), *args,
                                               gla_s[i], ret_s[i], bg=bg, ls=l)
            else:
                mixed, g_s, r_s = _even_prompt(proj.reshape(b, l, EV_COLS), cos, sin, *args,
                                               gla_s[i], ret_s[i], lt=512)
                mixed = mixed.reshape(t, d)
            out_states["gla"].append(g_s)
            out_states["ret"].append(r_s)
            x = _matmul_residual(x, [(mixed, w['even_w_out'][i])], tm=tm)
        else:
            proj = _norm_matmul(x, w['mix_norm'][layer][None], w['odd_w_in'][i], tm=tm, tn=640)
            cq, sq = _rope_tables(pos)
            qg = jnp.tile(w['swa_q_norm'][i], SWA_HEADS)[None, :]
            kg = jnp.tile(w['swa_k_norm'][i], SWA_KV_HEADS)[None, :]
            s5_args = (w['s5_ab_re'][i], w['s5_ab_im'][i], w['s5_bb_re'][i], w['s5_bb_im'][i],
                       w['s5_cc_re'][i], w['s5_cc_im'][i], w['s5_d'][i][None], w['s5_w_glu'][i], w['s5_b_glu'][i][None])
            n_state = S5_GROUPS * S5_STATE
            if sample:
                c_out, sr, si = _s5(proj.reshape(1, t, OD_COLS), s5_re[i].reshape(b, n_state),
                                    s5_im[i].reshape(b, n_state), *s5_args, n_seq=b, lt=l, groups_per_pass=1,
                                    unroll=True)
                c_out = c_out.reshape(t, S5_WIDTH)
                bg = 8
                d_out, kb, vb = _swa_sample(proj, swa_k[i].reshape(b, SWA_WINDOW, LANES),
                                            swa_v[i].reshape(b, SWA_WINDOW, LANES),
                                            jnp.tile(cq, (bg, 1)), jnp.tile(sq, (bg, 1)), qg, kg,
                                            _sink_column(w['swa_sinks'][i], bg * l), bg=bg, ls=l)
            else:
                proj3 = proj.reshape(b, l, OD_COLS)
                c_out, sr, si = _s5(proj3, s5_re[i].reshape(b, n_state), s5_im[i].reshape(b, n_state),
                                    *s5_args, n_seq=b, lt=128, groups_per_pass=8, unroll=4)
                c_out = c_out.reshape(t, S5_WIDTH)
                d_out, kb, vb = _swa_prompt(proj3, cq, sq, qg, kg, _sink_column(w['swa_sinks'][i], SWA_WINDOW), lt=512)
                d_out = d_out.reshape(t, 512)
            out_states["s5_re"].append(sr.reshape(b, S5_GROUPS, S5_STATE))
            out_states["s5_im"].append(si.reshape(b, S5_GROUPS, S5_STATE))
            out_states["swa_k"].append(kb.reshape(b, -1, SWA_KV_HEADS, SWA_HD))
            out_states["swa_v"].append(vb.reshape(b, -1, SWA_KV_HEADS, SWA_HD))
            w_out = w['odd_w_out'][i]
            x = _matmul_residual(x, [(c_out, w_out[:S5_WIDTH]), (d_out, w_out[S5_WIDTH:])], tm=tm)
        q = _norm_matmul(x, w['mem_x_norm'][layer][None], w['mem_w_q'][layer], tm=tm, tn=MEM_HD,
                         head_gain=w['mem_q_norm'][layer][None], n_norm_tiles=MEM_HEADS)
        if sample:
            o = _mem_sample(q, mem_k, mem_v, layer=layer, bs=4, ls=l)
        else:
            o = _mem_prompt(q.reshape(b, l, d), mem_k[layer], mem_v[layer], lt=512).reshape(t, d)
        x = _matmul_residual(x, [(o, w['mem_w_o'][layer])], tm=tm)
        x = _ffn(x, w['ffn2_norm'][layer][None], w['ffn2_w_gate'][layer], w['ffn2_w_up'][layer],
                 w['ffn2_w_down'][layer], tm=tm, th=256)
    return x.reshape(b, l, d), {k: jnp.stack(v) for k, v in out_states.items()}


def kernel(x_prompt, x_sample, mem_prompt, state_gla, state_ret, state_s5_re, state_s5_im, cache_swa_k, cache_swa_v, cache_mem_k, cache_mem_v, ffn1_norm, ffn1_w_gate, ffn1_w_up, ffn1_w_down, ffn2_norm, ffn2_w_gate, ffn2_w_up, ffn2_w_down, mix_norm, even_w_in, gla_w_gate, gla_b_gate, gla_out_norm, even_w_out, odd_w_in, s5_a_re, s5_a_im, s5_log_step, s5_b_re, s5_b_im, s5_c_re, s5_c_im, s5_d, s5_w_glu, s5_b_glu, swa_q_norm, swa_k_norm, swa_sinks, odd_w_out, mem_x_norm, mem_m_norm, mem_w_q, mem_w_k, mem_w_v, mem_w_o, mem_q_norm, mem_k_norm):
    depth = ffn1_norm.shape[0]
    n_even, n_odd = even_w_in.shape[0], odd_w_in.shape[0]
    batch, seq, d = x_prompt.shape
    dec_batch = x_sample.shape[0]
    n_mem = mem_prompt.shape[1]

    ev = even_w_in
    ev_cols = jnp.concatenate(
        [ev[..., 0:1536], ev[..., 1552:3088], ev[..., 1536:1552],
         jnp.zeros(ev.shape[:2] + (EV_COLS - 3088,), ev.dtype)], axis=-1)
    wgate_pad = jnp.concatenate(
        [gla_w_gate, jnp.zeros((n_even, EV_COLS - EV_GA - GLA_RANK, gla_w_gate.shape[-1]), gla_w_gate.dtype)], axis=1)
    w = dict(
        ffn1_norm=ffn1_norm, ffn2_norm=ffn2_norm, mix_norm=mix_norm, mem_x_norm=mem_x_norm,
        ffn1_w_gate=_bf(ffn1_w_gate), ffn1_w_up=_bf(ffn1_w_up), ffn1_w_down=_bf(ffn1_w_down),
        ffn2_w_gate=_bf(ffn2_w_gate), ffn2_w_up=_bf(ffn2_w_up), ffn2_w_down=_bf(ffn2_w_down),
        even_w_in=_bf(ev_cols), gla_w_gate=_bf(wgate_pad), gla_b_gate=gla_b_gate[:, None, :],
        gla_out_norm=gla_out_norm[:, None, :], even_w_out=_bf(even_w_out),
        odd_w_in=_bf(odd_w_in), odd_w_out=_bf(odd_w_out), s5_d=s5_d, s5_w_glu=_bf(s5_w_glu), s5_b_glu=s5_b_glu,
        swa_q_norm=swa_q_norm, swa_k_norm=swa_k_norm, swa_sinks=swa_sinks,
        mem_w_q=_bf(mem_w_q), mem_w_o=_bf(mem_w_o), mem_q_norm=mem_q_norm,
    )
    ab_re, ab_im, bb_re, bb_im = [], [], [], []
    for i in range(n_odd):
        a_r, a_i, b_r, b_i = _s5_prep(s5_a_re[i], s5_a_im[i], s5_log_step[i], s5_b_re[i], s5_b_im[i])
        ab_re.append(a_r.reshape(-1))
        ab_im.append(a_i.reshape(-1))
        bb_re.append(_half_block_diag(b_r))
        bb_im.append(_half_block_diag(b_i))
    w.update(s5_ab_re=ab_re, s5_ab_im=ab_im, s5_bb_re=bb_re, s5_bb_im=bb_im,
             s5_cc_re=[_half_block_diag(jnp.swapaxes(s5_c_re[i], 1, 2)) for i in range(n_odd)],
             s5_cc_im=[_half_block_diag(jnp.swapaxes(s5_c_im[i], 1, 2)) for i in range(n_odd)])

    mem2 = mem_prompt.reshape(batch * n_mem, d)
    p_mem_k, p_mem_v = [], []
    for layer in range(depth):
        w_kv = _bf(jnp.concatenate([mem_w_k[layer], mem_w_v[layer]], axis=1))
        kv = _norm_matmul(mem2, mem_m_norm[layer][None], w_kv, tm=batch * n_mem, tn=MEM_HD,
                          head_gain=mem_k_norm[layer][None], n_norm_tiles=MEM_HEADS)
        p_mem_k.append(kv[:, :d].reshape(batch, n_mem, d))
        p_mem_v.append(kv[:, d:].reshape(batch, n_mem, d))

    zeros = lambda *s: jnp.zeros(s, F32)
    p_states = (zeros(n_even, batch, GLA_HEADS, GLA_DK, GLA_DV), zeros(n_even, batch, RET_HEADS, RET_DK, GLA_DV),
                zeros(n_odd, batch, S5_GROUPS, S5_STATE), zeros(n_odd, batch, S5_GROUPS, S5_STATE), None, None)
    y_prompt, ps = _trunk(x_prompt, 0.0, p_states, p_mem_k, p_mem_v, w, sample=False)

    s_mem_k = cache_mem_k.reshape(depth, dec_batch, n_mem, d)
    s_mem_v = cache_mem_v.reshape(depth, dec_batch, n_mem, d)
    s_states = (state_gla, state_ret, state_s5_re, state_s5_im, cache_swa_k, cache_swa_v)
    y_sample, ss = _trunk(x_sample, float(PAST_LEN), s_states, s_mem_k, s_mem_v, w, sample=True)

    p_mem_k = jnp.stack(p_mem_k).reshape(depth, batch, n_mem, MEM_HEADS, MEM_HD)
    p_mem_v = jnp.stack(p_mem_v).reshape(depth, batch, n_mem, MEM_HEADS, MEM_HD)
    return (y_prompt, y_sample, ps["gla"], ps["ret"], ps["s5_re"], ps["s5_im"], ps["swa_k"], ps["swa_v"],
            p_mem_k, p_mem_v, ss["gla"], ss["ret"], ss["s5_re"], ss["s5_im"], ss["swa_k"], ss["swa_v"])
```

```python
import functools
import math

import jax
import jax.numpy as jnp
import numpy as np
from jax import lax
from jax.experimental import pallas as pl
from jax.experimental.pallas import tpu as pltpu

F32 = jnp.float32
BF16 = jnp.bfloat16
NORM_EPS = 1e-6
HIGHEST = lax.Precision.HIGHEST

D_MODEL = 1024
GLA_HEADS = 4
GLA_DK = 64
GLA_DV = 128
GLA_RANK = 16
GLA_TAU = 16.0
RET_HEADS = 4
RET_DK = 64
RET_THETA = 10000.0
LA_CHUNK = 64
S5_WIDTH = 512
S5_GROUP = 16
S5_GROUPS = 32
S5_STATE = 64
SWA_HD = 64
SWA_HEADS = 8
SWA_KV_HEADS = 2
SWA_WINDOW = 128
ROPE_THETA = 500000.0
MEM_HEADS = 4
MEM_HD = 256
PAST_LEN = 8192

VMEM_LIMIT_BYTES = 52 * 1024 * 1024
LANES = 128
SUBLANES = 8


def _params(*sem):
    return pltpu.CompilerParams(dimension_semantics=sem, vmem_limit_bytes=VMEM_LIMIT_BYTES)


def _rms(x, gain=None):
    y = x * lax.rsqrt(jnp.mean(x * x, axis=-1, keepdims=True) + NORM_EPS)
    return y if gain is None else y * gain


def _dot(a, b):
    return jnp.dot(a, b, preferred_element_type=F32)


def _dot_nt(a, b):
    return lax.dot_general(a, b, (((1,), (1,)), ((), ())), preferred_element_type=F32)


def _dot_tn(a, b):
    return lax.dot_general(a, b, (((0,), (0,)), ((), ())), preferred_element_type=F32)


def _dot_f32(a, b):
    return jnp.dot(a, b, precision=HIGHEST, preferred_element_type=F32)


def _dot_nt_f32(a, b):
    return lax.dot_general(a, b, (((1,), (1,)), ((), ())), precision=HIGHEST, preferred_element_type=F32)


def _bf(x):
    return x.astype(BF16)


def _log_sigmoid(x):
    return jnp.minimum(x, 0.0) - jnp.log1p(jnp.exp(-jnp.abs(x)))


def _iota(shape, dim):
    return lax.broadcasted_iota(jnp.int32, shape, dim)


def _swap_pairs(x):
    n = x.shape[-1]
    even = (_iota(x.shape, 1) & 1) == 0
    return jnp.where(even, pltpu.roll(x, n - 1, 1), pltpu.roll(x, 1, 1))


def _rope_partner(x, head_dim, half):
    n = x.shape[-1]
    first = (_iota(x.shape, 1) & (head_dim - 1)) < half
    return jnp.where(first, pltpu.roll(x, n - half, 1), pltpu.roll(x, half, 1))


def _ffn_kernel(x_ref, g_ref, wg_ref, wu_ref, wd_ref, o_ref, xn_ref, acc_ref):
    j = pl.program_id(1)

    @pl.when(j == 0)
    def _():
        xn_ref[...] = _bf(_rms(x_ref[...], g_ref[...]))
        acc_ref[...] = jnp.zeros_like(acc_ref)

    xn = xn_ref[...]
    gate = _dot(xn, wg_ref[...])
    up = _dot(xn, wu_ref[...])
    acc_ref[...] += _dot(_bf(jax.nn.silu(gate) * up), wd_ref[...])

    @pl.when(j == pl.num_programs(1) - 1)
    def _():
        o_ref[...] = x_ref[...] + 0.5 * acc_ref[...]


def _ffn(x, gain, wg, wu, wd, *, tm, th):
    t, d = x.shape
    h = wg.shape[1]
    return pl.pallas_call(
        _ffn_kernel,
        grid=(t // tm, h // th),
        in_specs=[
            pl.BlockSpec((tm, d), lambda i, j: (i, 0)),
            pl.BlockSpec((1, d), lambda i, j: (0, 0)),
            pl.BlockSpec((d, th), lambda i, j: (0, j)),
            pl.BlockSpec((d, th), lambda i, j: (0, j)),
            pl.BlockSpec((th, d), lambda i, j: (j, 0)),
        ],
        out_specs=pl.BlockSpec((tm, d), lambda i, j: (i, 0)),
        out_shape=jax.ShapeDtypeStruct((t, d), F32),
        scratch_shapes=[pltpu.VMEM((tm, d), BF16), pltpu.VMEM((tm, d), F32)],
        compiler_params=_params("parallel", "arbitrary"),
        name="ffn",
    )(x, gain, wg, wu, wd)


def _nmm_kernel(x_ref, g_ref, w_ref, hg_ref, o_ref, xn_ref, *, n_norm_tiles):
    j = pl.program_id(1)

    @pl.when(j == 0)
    def _():
        xn_ref[...] = _bf(_rms(x_ref[...], g_ref[...]))

    y = _dot(xn_ref[...], w_ref[...])
    if n_norm_tiles == 0:
        o_ref[...] = y
    else:
        @pl.when(j < n_norm_tiles)
        def _():
            o_ref[...] = _rms(y, hg_ref[...])

        @pl.when(j >= n_norm_tiles)
        def _():
            o_ref[...] = y


def _norm_matmul(x, gain, w, *, tm, tn, head_gain=None, n_norm_tiles=0):
    t, d = x.shape
    n = w.shape[1]
    if head_gain is None:
        head_gain = jnp.ones((1, tn), F32)
    return pl.pallas_call(
        functools.partial(_nmm_kernel, n_norm_tiles=n_norm_tiles),
        grid=(t // tm, n // tn),
        in_specs=[
            pl.BlockSpec((tm, d), lambda i, j: (i, 0)),
            pl.BlockSpec((1, d), lambda i, j: (0, 0)),
            pl.BlockSpec((d, tn), lambda i, j: (0, j)),
            pl.BlockSpec((1, tn), lambda i, j: (0, 0)),
        ],
        out_specs=pl.BlockSpec((tm, tn), lambda i, j: (i, j)),
        out_shape=jax.ShapeDtypeStruct((t, n), F32),
        scratch_shapes=[pltpu.VMEM((tm, d), BF16)],
        compiler_params=_params("parallel", "arbitrary"),
        name="norm_matmul",
    )(x, gain, w, head_gain)


def _mmr_kernel(*refs, n_terms):
    x_ref = refs[0]
    a_refs = refs[1:1 + n_terms]
    w_refs = refs[1 + n_terms:1 + 2 * n_terms]
    o_ref = refs[1 + 2 * n_terms]
    acc = x_ref[...]
    for a_ref, w_ref in zip(a_refs, w_refs):
        acc = acc + _dot(_bf(a_ref[...]), w_ref[...])
    o_ref[...] = acc


def _matmul_residual(x, terms, *, tm):
    t, d = x.shape
    acts = [a for a, _ in terms]
    ws = [w for _, w in terms]
    in_specs = [pl.BlockSpec((tm, d), lambda i: (i, 0))]
    in_specs += [pl.BlockSpec((tm, a.shape[1]), lambda i: (i, 0)) for a in acts]
    in_specs += [pl.BlockSpec(w.shape, lambda i: (0, 0)) for w in ws]
    return pl.pallas_call(
        functools.partial(_mmr_kernel, n_terms=len(terms)),
        grid=(t // tm,),
        in_specs=in_specs,
        out_specs=pl.BlockSpec((tm, d), lambda i: (i, 0)),
        out_shape=jax.ShapeDtypeStruct((t, d), F32),
        compiler_params=_params("parallel"),
        name="matmul_residual",
    )(x, *acts, *ws)


EV_GQ, EV_GK, EV_GV, EV_GG = 0, 256, 512, 1024
EV_RQ, EV_RK, EV_RV, EV_RG = 1536, 1792, 2048, 2560
EV_GA = 3072
EV_COLS = 3200
EV_BLOCK = 256
PAIR = 2


def _gate_and_norm(o, gate, gain=None):
    return _rms(o, gain) * jax.nn.silu(gate)


def _even_prompt_kernel(x_ref, gain_ref, win_ref, wout_ref, cos_ref, sin_ref, ld_ref, wgate_ref, bgate_ref, gnorm_ref,
                        s0g_ref, s0r_ref, o_ref, sg_ref, sr_ref, proj_sc, mix_sc, o_sc, *, chunk, n_chunks):
    @pl.when(pl.program_id(1) == 0)
    def _():
        sg_ref[...] = s0g_ref[...]
        sr_ref[...] = s0r_ref[...]

    c = chunk
    lt = c * n_chunks
    blk = min(EV_BLOCK, lt)
    c_shift = c.bit_length() - 1
    x = x_ref[0]
    proj_sc[...] = _dot(_bf(_rms(x, gain_ref[...])), win_ref[...])

    def cols(a, b):
        return proj_sc[:, a:b]

    log_a = _log_sigmoid(_dot(_bf(cols(EV_GA, EV_COLS)), wgate_ref[...]) + bgate_ref[...]) * (1.0 / GLA_TAU)
    tril = (_iota((c, c), 1) <= _iota((c, c), 0)).astype(F32)
    cum_parts = [_dot_f32(tril, log_a[i * c:(i + 1) * c]) for i in range(n_chunks)]
    tots = [p[c - 1:c] for p in cum_parts]
    cum = jnp.concatenate(cum_parts, axis=0)
    tot_b = jnp.concatenate([jnp.broadcast_to(t, (c, 256)) for t in tots], axis=0)
    k = cols(EV_GK, EV_GK + 256)
    gla = (cols(EV_GQ, EV_GQ + 256) * (GLA_DK ** -0.5) * jnp.exp(cum), k * jnp.exp(-cum), k * jnp.exp(tot_b - cum))

    ld = ld_ref[...]
    tpos = ((_iota((lt, 1), 0) & (c - 1)) + 1).astype(F32)
    cum_r = tpos * ld
    tot_r = float(c) * ld
    cos, sin = cos_ref[...], sin_ref[...]
    rq = cols(EV_RQ, EV_RQ + 256)
    rk = cols(EV_RK, EV_RK + 256)
    q_rot = rq * cos + _swap_pairs(rq) * sin
    k_rot = (rk * cos + _swap_pairs(rk) * sin) * (RET_DK ** -0.5)
    ret = (q_rot * jnp.exp(cum_r), k_rot * jnp.exp(-cum_r), k_rot * jnp.exp(tot_r - cum_r))

    tot_rows = jnp.concatenate(tots + [tot_r, jnp.zeros((LANES - n_chunks - 1, 256), F32)], axis=0)
    decay_cols = jnp.exp(jnp.transpose(tot_rows))

    row = _iota((blk, blk), 0)
    col = _iota((blk, blk), 1)
    blk_mask = ((row >> c_shift) == (col >> c_shift)) & (col <= row)
    lo = (_iota((lt, LANES), 1) < GLA_DK)
    gnorm = gnorm_ref[...]

    mixers = ((gla, EV_GV, EV_GG, sg_ref, 0, gnorm, lambda i: i),
              (ret, EV_RV, EV_RG, sr_ref, 512, None, lambda i: n_chunks))
    for m, ((q_dec, k_inv, k_dec), v_col, g_col, s_ref, out_col, gain, decay_col_of) in enumerate(mixers):
        for p in range(GLA_HEADS // PAIR):
            lanes = slice(p * LANES, (p + 1) * LANES)
            q_pair = q_dec[:, lanes]
            q_masked = [_bf(jnp.where(lo, q_pair, 0.0)), _bf(jnp.where(lo, 0.0, q_pair))]
            ki = _bf(k_inv[:, lanes])
            kd = _bf(k_dec[:, lanes])
            v_pair = _bf(cols(v_col + p * PAIR * GLA_DV, v_col + (p + 1) * PAIR * GLA_DV))
            for e in range(PAIR):
                slot = (m * (GLA_HEADS // PAIR) + p) * PAIR + e
                for r0 in range(0, lt, blk):
                    rs = slice(r0, r0 + blk)
                    scores = jnp.where(blk_mask, _dot_nt(q_masked[e][rs], ki[rs]), 0.0)
                    o_sc[slot, rs, :] = _dot(_bf(scores), v_pair[rs, e * GLA_DV:(e + 1) * GLA_DV])
            state = s_ref[0, p * PAIR:(p + 1) * PAIR].reshape(PAIR * GLA_DK, GLA_DV)
            for i in range(n_chunks):
                rs = slice(i * c, (i + 1) * c)
                q_stack = jnp.concatenate([q_masked[0][rs], q_masked[1][rs]], axis=0)
                o_inter = _dot(q_stack, _bf(state))
                kv = _dot_tn(kd[rs], v_pair[rs])
                kv = jnp.concatenate([kv[:GLA_DK, :GLA_DV], kv[GLA_DK:, GLA_DV:]], axis=0)
                ci = decay_col_of(i)
                state = state * decay_cols[p * LANES:(p + 1) * LANES, ci:ci + 1] + kv
                for e in range(PAIR):
                    h = p * PAIR + e
                    slot = (m * (GLA_HEADS // PAIR) + p) * PAIR + e
                    o = o_sc[slot, rs, :] + o_inter[e * c:(e + 1) * c]
                    gate = proj_sc[rs, g_col + h * GLA_DV:g_col + (h + 1) * GLA_DV]
                    mix_sc[rs, out_col + h * GLA_DV:out_col + (h + 1) * GLA_DV] = _gate_and_norm(o, gate, gain)
            s_ref[0, p * PAIR:(p + 1) * PAIR] = state.reshape(PAIR, GLA_DK, GLA_DV)
    o_ref[0] = x + _dot(_bf(mix_sc[...]), wout_ref[...])


def _even_prompt(x, gain, w_in, w_out, cos, sin, ld_row, wgate, bgate, gnorm, s0g, s0r, *, lt):
    b, l, d = x.shape
    chunk = math.gcd(l, LA_CHUNK)
    st_spec = pl.BlockSpec((1, GLA_HEADS, GLA_DK, GLA_DV), lambda i, j: (i, 0, 0, 0))
    io_spec = pl.BlockSpec((1, lt, d), lambda i, j: (i, j, 0))
    full = lambda a: pl.BlockSpec(a.shape, lambda i, j: (0,) * a.ndim)
    return pl.pallas_call(
        functools.partial(_even_prompt_kernel, chunk=chunk, n_chunks=lt // chunk),
        grid=(b, l // lt),
        in_specs=[
            io_spec, full(gain), full(w_in), full(w_out),
            pl.BlockSpec((lt, 256), lambda i, j: (j, 0)),
            pl.BlockSpec((lt, 256), lambda i, j: (j, 0)),
            full(ld_row), full(wgate), full(bgate), full(gnorm), st_spec, st_spec,
        ],
        out_specs=[io_spec, st_spec, st_spec],
        out_shape=[jax.ShapeDtypeStruct(x.shape, F32),
                   jax.ShapeDtypeStruct(s0g.shape, F32), jax.ShapeDtypeStruct(s0r.shape, F32)],
        scratch_shapes=[pltpu.VMEM((lt, EV_COLS), F32), pltpu.VMEM((lt, d), F32),
                        pltpu.VMEM((GLA_HEADS + RET_HEADS, lt, GLA_DV), F32)],
        compiler_params=_params("parallel", "arbitrary"),
        name="even_prompt",
    )(x, gain, w_in, w_out, cos, sin, ld_row, wgate, bgate, gnorm, s0g, s0r)


def _even_sample_kernel(proj_ref, cos_ref, sin_ref, ld_ref, wgate_ref, bgate_ref, gnorm_ref, s0g_ref, s0r_ref,
                        mix_ref, sg_ref, sr_ref, *, bg, ls):
    r = bg * ls
    ls_shift = ls.bit_length() - 1
    dk_shift = GLA_DK.bit_length() - 1
    n_exp = bg * GLA_DK
    row_seq = _iota((r, r), 0) >> ls_shift
    col_seq = _iota((r, r), 1) >> ls_shift
    same = row_seq == col_seq
    seg = same & (_iota((r, r), 1) <= _iota((r, r), 0))
    seg_f = seg.astype(F32)
    same_f = same.astype(F32)
    tile_b = _bf(((_iota((GLA_DK, n_exp), 1) & (GLA_DK - 1)) == _iota((GLA_DK, n_exp), 0)).astype(F32))
    tile_t_f = ((_iota((n_exp, GLA_DK), 0) & (GLA_DK - 1)) == _iota((n_exp, GLA_DK), 1)).astype(F32)
    tile_t_b = _bf(tile_t_f)
    q_mask = (_iota((r, n_exp), 0) >> ls_shift) == (_iota((r, n_exp), 1) >> dk_shift)
    k_mask = (_iota((n_exp, r), 0) >> dk_shift) == (_iota((n_exp, r), 1) >> ls_shift)
    tpos = ((_iota((r, 1), 0) & (ls - 1)) + 1).astype(F32)
    ld = ld_ref[...]
    cum_r = tpos * ld
    tot_r = float(ls) * ld
    r_dec = jnp.exp(tot_r)
    gnorm = gnorm_ref[...]

    def mixer(q_dec, k_inv, k_dec, v_col, g_col, s0_ref, s_ref, out_col, decay_of, gain):
        for h in range(GLA_HEADS):
            sl = slice(h * GLA_DK, (h + 1) * GLA_DK)
            v = _bf(proj_ref[:, v_col + h * GLA_DV:v_col + (h + 1) * GLA_DV])
            qd = _bf(q_dec[:, sl])
            scores = jnp.where(seg, _dot_nt(qd, _bf(k_inv[:, sl])), 0.0)
            state = s0_ref[:, h].reshape(n_exp, GLA_DV)
            q_exp = _bf(jnp.where(q_mask, _dot(qd, tile_b), 0.0))
            o = _dot(_bf(scores), v) + _dot(q_exp, _bf(state))
            k_exp = _bf(jnp.where(k_mask, _dot_nt(tile_t_b, _bf(k_dec[:, sl])), 0.0))
            new_state = state * decay_of(h, sl) + _dot(k_exp, v)
            s_ref[:, h] = new_state.reshape(bg, GLA_DK, GLA_DV)
            gate = proj_ref[:, g_col + h * GLA_DV:g_col + (h + 1) * GLA_DV]
            mix_ref[:, out_col + h * GLA_DV:out_col + (h + 1) * GLA_DV] = _gate_and_norm(o, gate, gain)

    log_a = _log_sigmoid(_dot(_bf(proj_ref[:, EV_GA:EV_COLS]), wgate_ref[...]) + bgate_ref[...]) * (1.0 / GLA_TAU)
    cum = _dot_f32(seg_f, log_a)
    tot = _dot_f32(same_f, log_a)
    k = proj_ref[:, EV_GK:EV_GK + 256]

    def gla_decay(h, sl):
        la_exp = jnp.where(k_mask, _dot_nt_f32(tile_t_f, log_a[:, sl]), 0.0)
        return jnp.exp(jnp.sum(la_exp, axis=-1, keepdims=True))

    mixer(proj_ref[:, EV_GQ:EV_GQ + 256] * (GLA_DK ** -0.5) * jnp.exp(cum), k * jnp.exp(-cum), k * jnp.exp(tot - cum),
          EV_GV, EV_GG, s0g_ref, sg_ref, 0, gla_decay, gnorm)

    cos = cos_ref[...]
    sin = sin_ref[...]
    rq = proj_ref[:, EV_RQ:EV_RQ + 256]
    rk = proj_ref[:, EV_RK:EV_RK + 256]
    q_rot = rq * cos + _swap_pairs(rq) * sin
    k_rot = (rk * cos + _swap_pairs(rk) * sin) * (RET_DK ** -0.5)

    def ret_decay(h, sl):
        return r_dec[:, h * RET_DK:h * RET_DK + 1]

    mixer(q_rot * jnp.exp(cum_r), k_rot * jnp.exp(-cum_r), k_rot * jnp.exp(tot_r - cum_r),
          EV_RV, EV_RG, s0r_ref, sr_ref, 512, ret_decay, None)


def _even_sample(proj, cos, sin, ld_row, wgate, bgate, gnorm, s0g, s0r, *, bg, ls):
    t = proj.shape[0]
    n_b = t // ls
    r = bg * ls
    st_spec = pl.BlockSpec((bg, GLA_HEADS, GLA_DK, GLA_DV), lambda i: (i, 0, 0, 0))
    full = lambda a: pl.BlockSpec(a.shape, lambda i: (0,) * a.ndim)
    return pl.pallas_call(
        functools.partial(_even_sample_kernel, bg=bg, ls=ls),
        grid=(n_b // bg,),
        in_specs=[
            pl.BlockSpec((r, EV_COLS), lambda i: (i, 0)),
            full(cos), full(sin), full(ld_row), full(wgate), full(bgate), full(gnorm), st_spec, st_spec,
        ],
        out_specs=[pl.BlockSpec((r, D_MODEL), lambda i: (i, 0)), st_spec, st_spec],
        out_shape=[jax.ShapeDtypeStruct((t, D_MODEL), F32),
                   jax.ShapeDtypeStruct(s0g.shape, F32), jax.ShapeDtypeStruct(s0r.shape, F32)],
        compiler_params=_params("parallel"),
        name="even_sample",
    )(proj, cos, sin, ld_row, wgate, bgate, gnorm, s0g, s0r)


def _s5_prep_kernel(are_ref, aim_ref, lstep_ref, bre_ref, bim_ref, abre_ref, abim_ref, bbre_ref, bbim_ref):
    a_re, a_im = are_ref[...], aim_ref[...]
    step = jnp.exp(lstep_ref[...])
    mag = jnp.exp(a_re * step)
    ab_re = mag * jnp.cos(a_im * step)
    ab_im = mag * jnp.sin(a_im * step)
    den = a_re * a_re + a_im * a_im
    coef_re = ((ab_re - 1.0) * a_re + ab_im * a_im) / den
    coef_im = (ab_im * a_re - (ab_re - 1.0) * a_im) / den
    b_re, b_im = bre_ref[...], bim_ref[...]
    abre_ref[...] = ab_re
    abim_ref[...] = ab_im
    bbre_ref[...] = coef_re * b_re - coef_im * b_im
    bbim_ref[...] = coef_re * b_im + coef_im * b_re


def _s5_prep(a_re, a_im, log_step, b_re, b_im):
    g, n = a_re.shape
    shp3 = jax.ShapeDtypeStruct((g, 1, n), F32)
    shpb = jax.ShapeDtypeStruct((g, S5_GROUP, n), F32)
    return pl.pallas_call(_s5_prep_kernel, out_shape=[shp3, shp3, shpb, shpb], name="s5_prep")(
        a_re.reshape(g, 1, n), a_im.reshape(g, 1, n), log_step.reshape(g, 1, 1),
        jnp.swapaxes(b_re, 1, 2), jnp.swapaxes(b_im, 1, 2))


S5_LANE_CHUNKS = S5_GROUPS * S5_STATE // LANES
S5_HALVES = 2


def _s5_layout(n_seq, lt):
    pack = max(1, SUBLANES // n_seq)
    pitch = lt + 4 if lt % SUBLANES == 0 else lt
    return pack, S5_LANE_CHUNKS // pack, pitch


def _s5_slot(c, n_groups):
    return c % n_groups, c // n_groups


def _s5_load_state(s0re_ref, s0im_ref, hre_ref, him_ref, n_seq, n_groups):
    for c in range(S5_LANE_CHUNKS):
        g, j = _s5_slot(c, n_groups)
        hre_ref[g, j * n_seq:(j + 1) * n_seq, :] = s0re_ref[:, c * LANES:(c + 1) * LANES]
        him_ref[g, j * n_seq:(j + 1) * n_seq, :] = s0im_ref[:, c * LANES:(c + 1) * LANES]


def _s5_store_state(sre_ref, sim_ref, hre_ref, him_ref, n_seq, n_groups):
    for c in range(S5_LANE_CHUNKS):
        g, j = _s5_slot(c, n_groups)
        sre_ref[:, c * LANES:(c + 1) * LANES] = hre_ref[g, j * n_seq:(j + 1) * n_seq, :]
        sim_ref[:, c * LANES:(c + 1) * LANES] = him_ref[g, j * n_seq:(j + 1) * n_seq, :]


def _s5_core(u, abre_ref, abim_ref, bbre_ref, bbim_ref, ccre_ref, ccim_ref, d_ref, wglu_ref, bglu_ref,
             xr_ref, xi_ref, hre_ref, him_ref, *, n_seq, lt, groups_per_pass, unroll):
    pack, n_groups, pitch = _s5_layout(n_seq, lt)
    rows = n_seq * lt
    per_half = S5_LANE_CHUNKS // S5_HALVES
    lanes_of = lambda c: slice(c * LANES, (c + 1) * LANES)
    slot_of = lambda c: _s5_slot(c, n_groups)

    def seq_rows(j, s):
        return slice((j * n_seq + s) * pitch, (j * n_seq + s) * pitch + lt)

    ub = _bf(u)
    u_cols = S5_WIDTH // S5_HALVES
    for half in range(S5_HALVES):
        uh = ub[:, half * u_cols:(half + 1) * u_cols]
        for x_ref, bb_ref in ((xr_ref, bbre_ref), (xi_ref, bbim_ref)):
            x = _dot(uh, bb_ref[half])
            for k in range(per_half):
                g, j = slot_of(half * per_half + k)
                if pitch == lt:
                    x_ref[g, j * rows:(j + 1) * rows, :] = x[:, lanes_of(k)]
                else:
                    for s in range(n_seq):
                        x_ref[g, seq_rows(j, s), :] = x[s * lt:(s + 1) * lt, lanes_of(k)]

    for g0 in range(0, n_groups, groups_per_pass):
        gs = list(range(g0, g0 + groups_per_pass))
        init = tuple(hre_ref[g] for g in gs) + tuple(him_ref[g] for g in gs)

        def step(t, carry, gs=gs):
            rws = pl.ds(t, pack * n_seq, stride=pitch)
            new_re, new_im = [], []
            for k, g in enumerate(gs):
                a_re, a_im = abre_ref[g], abim_ref[g]
                h_re, h_im = carry[k], carry[len(gs) + k]
                n_re = a_re * h_re - a_im * h_im + xr_ref[g, rws, :]
                n_im = a_re * h_im + a_im * h_re + xi_ref[g, rws, :]
                xr_ref[g, rws, :] = n_re
                xi_ref[g, rws, :] = n_im
                new_re.append(n_re)
                new_im.append(n_im)
            return tuple(new_re + new_im)

        fin = lax.fori_loop(0, lt, step, init, unroll=unroll)
        for k, g in enumerate(gs):
            hre_ref[g] = fin[k]
            him_ref[g] = fin[len(gs) + k]

    def gather(x_ref, half):
        cols = []
        for k in range(per_half):
            g, j = slot_of(half * per_half + k)
            if pitch == lt:
                cols.append(x_ref[g, j * rows:(j + 1) * rows, :])
            else:
                cols.append(jnp.concatenate([x_ref[g, seq_rows(j, s), :] for s in range(n_seq)], axis=0))
        return _bf(jnp.concatenate(cols, axis=1))

    y = jnp.concatenate([_dot(gather(xr_ref, half), ccre_ref[half]) - _dot(gather(xi_ref, half), ccim_ref[half])
                         for half in range(S5_HALVES)], axis=1) + d_ref[...] * u
    z = jax.nn.gelu(y, approximate=True)
    return z * jax.nn.sigmoid(_dot(_bf(z), wglu_ref[...]) + bglu_ref[...])


def _s5_kernel(u_ref, s0re_ref, s0im_ref, abre_ref, abim_ref, bbre_ref, bbim_ref, ccre_ref, ccim_ref,
               d_ref, wglu_ref, bglu_ref, out_ref, sre_ref, sim_ref, xr_ref, xi_ref, hre_ref, him_ref,
               *, n_seq, lt, groups_per_pass, unroll):
    n_groups = _s5_layout(n_seq, lt)[1]

    @pl.when(pl.program_id(0) == 0)
    def _():
        _s5_load_state(s0re_ref, s0im_ref, hre_ref, him_ref, n_seq, n_groups)

    out = _s5_core(u_ref[...].reshape(n_seq * lt, S5_WIDTH), abre_ref, abim_ref, bbre_ref, bbim_ref, ccre_ref, ccim_ref,
                   d_ref, wglu_ref, bglu_ref, xr_ref, xi_ref, hre_ref, him_ref,
                   n_seq=n_seq, lt=lt, groups_per_pass=groups_per_pass, unroll=unroll)
    out_ref[...] = out.reshape(out_ref.shape)

    @pl.when(pl.program_id(0) == pl.num_programs(0) - 1)
    def _():
        _s5_store_state(sre_ref, sim_ref, hre_ref, him_ref, n_seq, n_groups)


def _s5_tables(ab, n_seq, lt):
    pack, n_groups, _ = _s5_layout(n_seq, lt)
    tab = jnp.swapaxes(ab.reshape(pack, n_groups, 1, LANES), 0, 1)
    return jnp.broadcast_to(tab, (n_groups, pack, n_seq, LANES)).reshape(n_groups, pack * n_seq, LANES)


def _s5(proj3, s0_re, s0_im, ab_re, ab_im, bb_re, bb_im, cc_re, cc_im, d_row, wglu, bglu, *, n_seq, lt,
        groups_per_pass, unroll):
    nb, rows_b, _ = proj3.shape
    blk_rows = n_seq * lt // nb
    pack, n_groups, pitch = _s5_layout(n_seq, lt)
    ab_re, ab_im = _s5_tables(ab_re, n_seq, lt), _s5_tables(ab_im, n_seq, lt)
    full = lambda a: pl.BlockSpec(a.shape, lambda j: (0,) * a.ndim)
    io_spec = pl.BlockSpec((nb, blk_rows, S5_WIDTH), lambda j: (0, j, 0))
    x_scratch = pltpu.VMEM((n_groups, pack * n_seq * pitch, LANES), F32)
    h_scratch = pltpu.VMEM((n_groups, pack * n_seq, LANES), F32)
    return pl.pallas_call(
        functools.partial(_s5_kernel, n_seq=n_seq, lt=lt, groups_per_pass=groups_per_pass, unroll=unroll),
        grid=(rows_b // blk_rows,),
        in_specs=[io_spec, full(s0_re), full(s0_im), full(ab_re), full(ab_im), full(bb_re), full(bb_im),
                  full(cc_re), full(cc_im), full(d_row), full(wglu), full(bglu)],
        out_specs=[io_spec, full(s0_re), full(s0_im)],
        out_shape=[jax.ShapeDtypeStruct((nb, rows_b, S5_WIDTH), F32),
                   jax.ShapeDtypeStruct(s0_re.shape, F32), jax.ShapeDtypeStruct(s0_im.shape, F32)],
        scratch_shapes=[x_scratch, x_scratch, h_scratch, h_scratch],
        compiler_params=_params("arbitrary"),
        name="s5",
    )(proj3, s0_re, s0_im, ab_re, ab_im, bb_re, bb_im, cc_re, cc_im, d_row, wglu, bglu)


OD_Q_BLOCK = 1
OD_KV_BLOCK = 4
OD_COLS = 1280
ROPE_HALF = SWA_HD // 8
SWA_GROUP = SWA_HEADS // SWA_KV_HEADS


def _half_lanes(shape):
    return (_iota(shape, 1) & (LANES - 1)) < SWA_HD


def _pair_rms_scale(x):
    same_head = (_iota((LANES, LANES), 0) >= SWA_HD) == (_iota((LANES, LANES), 1) >= SWA_HD)
    sums = _dot_f32(x * x, same_head.astype(F32))
    return lax.rsqrt(sums * (1.0 / SWA_HD) + NORM_EPS)


def _swa_qk(xq, xk, qg, kg, cq, sq):
    xq_g = xq * qg
    q_rot = xq_g * cq + _rope_partner(xq_g, SWA_HD, ROPE_HALF) * sq
    xk_g = xk * kg
    k_rot = xk_g * cq[:, :LANES] + _rope_partner(xk_g, SWA_HD, ROPE_HALF) * sq[:, :LANES]
    q_pairs = [q_rot[:, j * LANES:(j + 1) * LANES] * _pair_rms_scale(xq[:, j * LANES:(j + 1) * LANES])
               for j in range(SWA_HEADS // 2)]
    return q_pairs, k_rot * _pair_rms_scale(xk)


def _swa_query_stack(q_pairs, kh):
    lo = _half_lanes(q_pairs[0].shape)
    keep = lo if kh == 0 else jnp.logical_not(lo)
    parts = []
    for g in range(SWA_GROUP):
        hq = kh * SWA_GROUP + g
        pair = q_pairs[hq // 2]
        src = pair if hq % 2 == kh else pltpu.roll(pair, SWA_HD, 1)
        parts.append(jnp.where(keep, src, 0.0))
    return _bf(jnp.concatenate(parts, axis=0))


def _swa_merge_heads(o, kh, rows):
    lo = _half_lanes((rows, LANES))
    pairs = []
    for p in range(SWA_GROUP // 2):
        even, odd = o[2 * p * rows:(2 * p + 1) * rows], o[(2 * p + 1) * rows:(2 * p + 2) * rows]
        if kh == 0:
            pairs.append(jnp.where(lo, even, pltpu.roll(odd, SWA_HD, 1)))
        else:
            pairs.append(jnp.where(lo, pltpu.roll(even, SWA_HD, 1), odd))
    return pairs


def _swa_block(xq, xk, v_cur, k_prev, v_prev, mask, qg, kg, cq, sq, sink_ref):
    w = SWA_WINDOW
    ones_col = jnp.ones((2 * w, LANES), BF16)
    q_pairs, k_cur = _swa_qk(xq, xk, qg, kg, cq, sq)
    k_ext = _bf(jnp.concatenate([k_prev, k_cur], axis=0))
    v_ext = _bf(jnp.concatenate([v_prev, v_cur], axis=0))
    out_pairs = []
    for kh in range(SWA_KV_HEADS):
        s_all = _dot_nt(_swa_query_stack(q_pairs, kh), k_ext)
        weights, sink_terms = [], []
        for g in range(SWA_GROUP):
            s = jnp.where(mask, s_all[g * w:(g + 1) * w] * (SWA_HD ** -0.5), -jnp.inf)
            sink = sink_ref[kh * SWA_GROUP + g:kh * SWA_GROUP + g + 1, :]
            m = jnp.maximum(jnp.broadcast_to(jnp.max(s, axis=-1, keepdims=True), (w, LANES)), sink)
            weights.append(_bf(jnp.exp(s - jnp.concatenate([m, m], axis=1))))
            sink_terms.append(jnp.exp(sink - m))
        weights = jnp.concatenate(weights, axis=0)
        den = _dot(weights, ones_col) + jnp.concatenate(sink_terms, axis=0)
        out_pairs += _swa_merge_heads(_dot(weights, v_ext) / den, kh, w)
    return out_pairs, k_cur


def _odd_prompt_kernel(x_ref, gain_ref, win_ref, wout_ref, s0re_ref, s0im_ref, abre_ref, abim_ref, bbre_ref, bbim_ref,
                       ccre_ref, ccim_ref, d_ref, wglu_ref, bglu_ref, cq_ref, sq_ref, qg_ref, kg_ref, sink_ref,
                       o_ref, sre_ref, sim_ref, ck_ref, cv_ref,
                       proj_sc, mix_sc, xr_ref, xi_ref, hre_ref, him_ref, kprev_sc, vprev_sc,
                       *, n_seq, groups_per_pass, unroll):
    w = SWA_WINDOW
    rows = n_seq * w
    n_groups = _s5_layout(n_seq, w)[1]
    step = pl.program_id(0)

    @pl.when(step == 0)
    def _():
        _s5_load_state(s0re_ref, s0im_ref, hre_ref, him_ref, n_seq, n_groups)
        kprev_sc[...] = jnp.zeros_like(kprev_sc)
        vprev_sc[...] = jnp.zeros_like(vprev_sc)

    x = x_ref[...].reshape(rows, D_MODEL)
    proj_sc[...] = _dot(_bf(_rms(x, gain_ref[...])), win_ref[...])
    mix_sc[:, 0:S5_WIDTH] = _s5_core(proj_sc[:, 0:S5_WIDTH], abre_ref, abim_ref, bbre_ref, bbim_ref, ccre_ref, ccim_ref,
                                     d_ref, wglu_ref, bglu_ref, xr_ref, xi_ref, hre_ref, him_ref,
                                     n_seq=n_seq, lt=w, groups_per_pass=groups_per_pass, unroll=unroll)

    t_idx = _iota((w, 2 * w), 0)
    s_idx = _iota((w, 2 * w), 1)
    mask = (s_idx > t_idx) & (s_idx <= t_idx + w) & (s_idx >= jnp.where(step == 0, w, 0))
    q0, k0, v0 = OD_Q_BLOCK * 512, OD_KV_BLOCK * 256, OD_KV_BLOCK * 256 + LANES
    for s in range(n_seq):
        rs = slice(s * w, (s + 1) * w)
        v_cur = proj_sc[rs, v0:v0 + LANES]
        pairs, k_cur = _swa_block(proj_sc[rs, q0:q0 + 512], proj_sc[rs, k0:k0 + LANES], v_cur, kprev_sc[s], vprev_sc[s],
                                  mask, qg_ref[...], kg_ref[...], cq_ref[...], sq_ref[...], sink_ref)
        for i, pair in enumerate(pairs):
            mix_sc[rs, S5_WIDTH + i * LANES:S5_WIDTH + (i + 1) * LANES] = pair
        kprev_sc[s] = k_cur
        vprev_sc[s] = v_cur
    o_ref[...] = (x + _dot(_bf(mix_sc[...]), wout_ref[...])).reshape(o_ref.shape)

    @pl.when(step == pl.num_programs(0) - 1)
    def _():
        _s5_store_state(sre_ref, sim_ref, hre_ref, him_ref, n_seq, n_groups)
        ck_ref[...] = kprev_sc[...]
        cv_ref[...] = vprev_sc[...]


def _odd_prompt(x, gain, w_in, w_out, s0_re, s0_im, ab_re, ab_im, bb_re, bb_im, cc_re, cc_im, d_row, wglu, bglu,
                cq, sq, qg, kg, sink_rows, *, groups_per_pass, unroll):
    b, l, d = x.shape
    w = SWA_WINDOW
    pack, n_groups, pitch = _s5_layout(b, w)
    ab_re, ab_im = _s5_tables(ab_re, b, w), _s5_tables(ab_im, b, w)
    full = lambda a: pl.BlockSpec(a.shape, lambda j: (0,) * a.ndim)
    io_spec = pl.BlockSpec((b, w, d), lambda j: (0, j, 0))
    tab_spec = pl.BlockSpec((w, 512), lambda j: (j, 0))
    cache_shape = jax.ShapeDtypeStruct((b, w, LANES), F32)
    x_scratch = pltpu.VMEM((n_groups, pack * b * pitch, LANES), F32)
    h_scratch = pltpu.VMEM((n_groups, pack * b, LANES), F32)
    kv_scratch = pltpu.VMEM((b, w, LANES), F32)
    return pl.pallas_call(
        functools.partial(_odd_prompt_kernel, n_seq=b, groups_per_pass=groups_per_pass, unroll=unroll),
        grid=(l // w,),
        in_specs=[io_spec, full(gain), full(w_in), full(w_out), full(s0_re), full(s0_im), full(ab_re), full(ab_im),
                  full(bb_re), full(bb_im), full(cc_re), full(cc_im), full(d_row), full(wglu), full(bglu),
                  tab_spec, tab_spec, full(qg), full(kg), full(sink_rows)],
        out_specs=[io_spec, full(s0_re), full(s0_im), pl.BlockSpec((b, w, LANES), lambda j: (0, 0, 0)),
                   pl.BlockSpec((b, w, LANES), lambda j: (0, 0, 0))],
        out_shape=[jax.ShapeDtypeStruct(x.shape, F32), jax.ShapeDtypeStruct(s0_re.shape, F32),
                   jax.ShapeDtypeStruct(s0_im.shape, F32), cache_shape, cache_shape],
        scratch_shapes=[pltpu.VMEM((b * w, OD_COLS), F32), pltpu.VMEM((b * w, d), F32),
                        x_scratch, x_scratch, h_scratch, h_scratch, kv_scratch, kv_scratch],
        compiler_params=_params("arbitrary"),
        name="odd_prompt",
    )(x, gain, w_in, w_out, s0_re, s0_im, ab_re, ab_im, bb_re, bb_im, cc_re, cc_im, d_row, wglu, bglu,
      cq, sq, qg, kg, sink_rows)


def _swa_sample_kernel(q_ref, kv_ref, ck_ref, cv_ref, cq_ref, sq_ref, qg_ref, kg_ref, sink_ref,
                       o_ref, nk_ref, nv_ref, *, bg, ls):
    w = SWA_WINDOW
    r = bg * ls
    ls_shift = ls.bit_length() - 1
    w_shift = w.bit_length() - 1
    rows_g = SWA_GROUP * r
    v_new = kv_ref[:, LANES:2 * LANES]
    q_pairs, k_new = _swa_qk(q_ref[...], kv_ref[:, 0:LANES], qg_ref[...], kg_ref[...], cq_ref[...], sq_ref[...])
    k_cache = _bf(ck_ref[...].reshape(bg * w, LANES))
    v_cache = _bf(cv_ref[...].reshape(bg * w, LANES))

    row = _iota((rows_g, bg * w), 0) & (r - 1)
    col = _iota((rows_g, bg * w), 1)
    mask_c = ((row >> ls_shift) == (col >> w_shift)) & ((col & (w - 1)) > (row & (ls - 1)))
    row_n = _iota((rows_g, r), 0) & (r - 1)
    col_n = _iota((rows_g, r), 1)
    mask_n = ((row_n >> ls_shift) == (col_n >> ls_shift)) & ((col_n & (ls - 1)) <= (row_n & (ls - 1)))
    for kh in range(SWA_KV_HEADS):
        q_stack = _swa_query_stack(q_pairs, kh)
        s_c = jnp.where(mask_c, _dot_nt(q_stack, k_cache) * (SWA_HD ** -0.5), -jnp.inf)
        s_n = jnp.where(mask_n, _dot_nt(q_stack, _bf(k_new)) * (SWA_HD ** -0.5), -jnp.inf)
        sink = jnp.concatenate([jnp.broadcast_to(sink_ref[kh * SWA_GROUP + g:kh * SWA_GROUP + g + 1, :], (r, LANES))
                                for g in range(SWA_GROUP)], axis=0)
        row_max = jnp.maximum(jnp.max(s_c, axis=-1, keepdims=True), jnp.max(s_n, axis=-1, keepdims=True))
        m = jnp.maximum(jnp.broadcast_to(row_max, (rows_g, LANES)), sink)
        e_c = _bf(jnp.exp(s_c - jnp.concatenate([m] * bg, axis=1)))
        e_n = _bf(jnp.exp(s_n - m[:, :r]))
        den = (_dot(e_c, jnp.ones((bg * w, LANES), BF16)) + _dot(e_n, jnp.ones((r, LANES), BF16))
               + jnp.exp(sink - m))
        o = (_dot(e_c, v_cache) + _dot(e_n, _bf(v_new))) / den
        for i, pair in enumerate(_swa_merge_heads(o, kh, r)):
            col_i = kh * (SWA_GROUP // 2) + i
            o_ref[:, col_i * LANES:(col_i + 1) * LANES] = pair

    nk_ref[:, 0:w - ls, :] = ck_ref[:, ls:w, :]
    nv_ref[:, 0:w - ls, :] = cv_ref[:, ls:w, :]
    for b in range(bg):
        nk_ref[b, w - ls:w, :] = k_new[b * ls:(b + 1) * ls, :]
        nv_ref[b, w - ls:w, :] = v_new[b * ls:(b + 1) * ls, :]


def _swa_sample(proj, cache_k, cache_v, cq, sq, qg, kg, sink_rows, *, bg, ls):
    t = proj.shape[0]
    w = SWA_WINDOW
    r = bg * ls
    full = lambda a: pl.BlockSpec(a.shape, lambda i: (0,) * a.ndim)
    cache_spec = pl.BlockSpec((bg, w, LANES), lambda i: (i, 0, 0))
    return pl.pallas_call(
        functools.partial(_swa_sample_kernel, bg=bg, ls=ls),
        grid=(t // r,),
        in_specs=[
            pl.BlockSpec((r, 512), lambda i: (i, OD_Q_BLOCK)),
            pl.BlockSpec((r, 256), lambda i: (i, OD_KV_BLOCK)),
            cache_spec, cache_spec, full(cq), full(sq), full(qg), full(kg), full(sink_rows),
        ],
        out_specs=[pl.BlockSpec((r, 512), lambda i: (i, 0)), cache_spec, cache_spec],
        out_shape=[jax.ShapeDtypeStruct((t, 512), F32),
                   jax.ShapeDtypeStruct(cache_k.shape, F32), jax.ShapeDtypeStruct(cache_v.shape, F32)],
        compiler_params=_params("parallel"),
        name="swa_sample",
    )(proj, proj, cache_k, cache_v, cq, sq, qg, kg, sink_rows)


def _mem_prompt_kernel(x_ref, g_ref, wq_ref, qg_ref, k_ref, v_ref, wo_ref, o_ref):
    x = x_ref[0]
    xn = _bf(_rms(x, g_ref[...]))
    acc = x
    for h in range(MEM_HEADS):
        sl = slice(h * MEM_HD, (h + 1) * MEM_HD)
        q = _bf(_rms(_dot(xn, wq_ref[:, sl]), qg_ref[...]))
        s = _dot_nt(q, _bf(k_ref[0, :, sl])) * (MEM_HD ** -0.5)
        e = jnp.exp(s - jnp.max(s, axis=-1, keepdims=True))
        p = e / jnp.sum(e, axis=-1, keepdims=True)
        o = _dot(_bf(p), _bf(v_ref[0, :, sl]))
        acc = acc + _dot(_bf(o), wo_ref[sl, :])
    o_ref[0] = acc


def _mem_prompt(x, gain, wq, q_gain, k, v, wo, *, lt):
    b, l, d = x.shape
    n_mem = k.shape[1]
    kv_spec = pl.BlockSpec((1, n_mem, d), lambda i, j: (i, 0, 0))
    io_spec = pl.BlockSpec((1, lt, d), lambda i, j: (i, j, 0))
    full = lambda a: pl.BlockSpec(a.shape, lambda i, j: (0,) * a.ndim)
    return pl.pallas_call(
        _mem_prompt_kernel,
        grid=(b, l // lt),
        in_specs=[io_spec, full(gain), full(wq), full(q_gain), kv_spec, kv_spec, full(wo)],
        out_specs=io_spec,
        out_shape=jax.ShapeDtypeStruct(x.shape, F32),
        compiler_params=_params("parallel", "arbitrary"),
        name="mem_prompt",
    )(x, gain, wq, q_gain, k, v, wo)


def _mem_sample_kernel(q_ref, k_ref, v_ref, o_ref, *, bs, ls):
    r = bs * ls
    n_rows = MEM_HEADS * ls
    hd_shift = MEM_HD.bit_length() - 1
    ls_shift = ls.bit_length() - 1
    qb = _bf(q_ref[...])
    head_mask = (_iota((n_rows, D_MODEL), 0) >> ls_shift) == (_iota((n_rows, D_MODEL), 1) >> hd_shift)
    acc = jnp.zeros((r, D_MODEL), F32)
    for b in range(bs):
        sel = _bf(((_iota((n_rows, r), 0) & (ls - 1)) + b * ls == _iota((n_rows, r), 1)).astype(F32))
        q_exp = _bf(jnp.where(head_mask, _dot(sel, qb), 0.0))
        k_b = jnp.concatenate([k_ref[b, :, h, :] for h in range(MEM_HEADS)], axis=1)
        v_b = jnp.concatenate([v_ref[b, :, h, :] for h in range(MEM_HEADS)], axis=1)
        s = _dot_nt(_bf(k_b), q_exp) * (MEM_HD ** -0.5)
        e = jnp.exp(s - jnp.max(s, axis=0, keepdims=True))
        p = e / jnp.sum(e, axis=0, keepdims=True)
        o_all = jnp.where(head_mask, _dot_tn(_bf(p), _bf(v_b)), 0.0)
        acc = acc + _dot_tn(sel, _bf(o_all))
    o_ref[...] = acc


def _mem_sample(q, k, v, *, layer, bs, ls):
    t, d = q.shape
    n_mem = k.shape[2]
    r = bs * ls
    kv_spec = pl.BlockSpec((None, bs, n_mem, MEM_HEADS, MEM_HD), lambda i: (layer, i, 0, 0, 0))
    io_spec = pl.BlockSpec((r, d), lambda i: (i, 0))
    return pl.pallas_call(
        functools.partial(_mem_sample_kernel, bs=bs, ls=ls),
        grid=(t // r,),
        in_specs=[io_spec, kv_spec, kv_spec],
        out_specs=io_spec,
        out_shape=jax.ShapeDtypeStruct(q.shape, F32),
        compiler_params=_params("parallel"),
        name="mem_sample",
    )(q, k, v)


def _retention_tables(pos):
    inv = 1.0 / (RET_THETA ** jnp.linspace(0.0, 1.0, RET_DK // 2, dtype=F32))
    ang = pos[:, None] * inv[None, :]
    cos = jnp.repeat(jnp.cos(ang), 2, axis=1)
    sin = jnp.stack([-jnp.sin(ang), jnp.sin(ang)], axis=-1).reshape(pos.shape[0], RET_DK)
    return jnp.tile(cos, (1, RET_HEADS)), jnp.tile(sin, (1, RET_HEADS))


def _rope_tables(pos):
    half = ROPE_HALF
    inv = 1.0 / (ROPE_THETA ** (jnp.arange(half, dtype=F32) * 2.0 / (2 * half)))
    ang = pos[:, None] * inv[None, :]
    n = pos.shape[0]
    rest = SWA_HD - 2 * half
    cos = jnp.concatenate([jnp.cos(ang), jnp.cos(ang), jnp.ones((n, rest), F32)], axis=1)
    sin = jnp.concatenate([-jnp.sin(ang), jnp.sin(ang), jnp.zeros((n, rest), F32)], axis=1)
    return jnp.tile(cos, (1, SWA_HEADS)), jnp.tile(sin, (1, SWA_HEADS))


def _block_diag(t):
    g, a, b = t.shape
    eye = jnp.eye(g, dtype=t.dtype)
    return (t[:, :, None, :] * eye[:, None, :, None]).reshape(g * a, g * b)


def _half_block_diag(t):
    per = t.shape[0] // S5_HALVES
    return _bf(jnp.stack([_block_diag(t[h * per:(h + 1) * per]) for h in range(S5_HALVES)]))


def _sink_rows(sinks):
    return jnp.broadcast_to(sinks.astype(F32)[:, None], (sinks.shape[0], LANES))


def _trunk(x3, pos0, states, mem_k, mem_v, w, *, sample):
    b, l, d = x3.shape
    t = b * l
    x = x3.reshape(t, d)
    pos = pos0 + jnp.arange(l, dtype=F32)
    tm = 512 if sample else 1024
    gla_s, ret_s, s5_re, s5_im, swa_k, swa_v = states
    out_states = {k: [] for k in ("gla", "ret", "s5_re", "s5_im", "swa_k", "swa_v")}
    ld_row = jnp.repeat(jnp.log(1.0 - 2.0 ** (-5.0 - jnp.arange(RET_HEADS, dtype=F32))), RET_DK)[None, :]

    for layer in range(2):
        i = layer // 2
        x = _ffn(x, w['ffn1_norm'][layer][None], w['ffn1_w_gate'][layer], w['ffn1_w_up'][layer],
                 w['ffn1_w_down'][layer], tm=tm, th=256)
        if layer % 2 == 0:
            cos, sin = _retention_tables(pos)
            args = (ld_row, w['gla_w_gate'][i], w['gla_b_gate'][i], w['gla_out_norm'][i])
            if sample:
                bg = 16
                proj = _norm_matmul(x, w['mix_norm'][layer][None], w['even_w_in'][i], tm=tm, tn=640)
                mixed, g_s, r_s = _even_sample(proj, jnp.tile(cos, (bg, 1)), jnp.tile(sin, (bg, 1)), *args,
                                               gla_s[i], ret_s[i], bg=bg, ls=l)
                x = _matmul_residual(x, [(mixed, w['even_w_out'][i])], tm=tm)
            else:
                x, g_s, r_s = _even_prompt(x.reshape(b, l, d), w['mix_norm'][layer][None], w['even_w_in'][i],
                                           w['even_w_out'][i], cos, sin, *args, gla_s[i], ret_s[i], lt=512)
                x = x.reshape(t, d)
            out_states["gla"].append(g_s)
            out_states["ret"].append(r_s)
        else:
            cq, sq = _rope_tables(pos)
            qg = jnp.tile(w['swa_q_norm'][i], SWA_HEADS)[None, :]
            kg = jnp.tile(w['swa_k_norm'][i], SWA_KV_HEADS)[None, :]
            s5_args = (w['s5_ab_re'][i], w['s5_ab_im'][i], w['s5_bb_re'][i], w['s5_bb_im'][i],
                       w['s5_cc_re'][i], w['s5_cc_im'][i], w['s5_d'][i][None], w['s5_w_glu'][i], w['s5_b_glu'][i][None])
            n_state = S5_GROUPS * S5_STATE
            s0_re, s0_im = s5_re[i].reshape(b, n_state), s5_im[i].reshape(b, n_state)
            sinks = _sink_rows(w['swa_sinks'][i])
            if sample:
                proj = _norm_matmul(x, w['mix_norm'][layer][None], w['odd_w_in'][i], tm=tm, tn=640)
                c_out, sr, si = _s5(proj.reshape(1, t, OD_COLS), s0_re, s0_im, *s5_args, n_seq=b, lt=l,
                                    groups_per_pass=1, unroll=True)
                bg = 8
                d_out, kb, vb = _swa_sample(proj, swa_k[i].reshape(b, SWA_WINDOW, LANES),
                                            swa_v[i].reshape(b, SWA_WINDOW, LANES),
                                            jnp.tile(cq, (bg, 1)), jnp.tile(sq, (bg, 1)), qg, kg, sinks, bg=bg, ls=l)
                w_out = w['odd_w_out'][i]
                x = _matmul_residual(x, [(c_out.reshape(t, S5_WIDTH), w_out[:S5_WIDTH]), (d_out, w_out[S5_WIDTH:])],
                                     tm=tm)
            else:
                x, sr, si, kb, vb = _odd_prompt(x.reshape(b, l, d), w['mix_norm'][layer][None], w['odd_w_in'][i],
                                                w['odd_w_out'][i], s0_re, s0_im, *s5_args, cq, sq, qg, kg, sinks,
                                                groups_per_pass=8, unroll=4)
                x = x.reshape(t, d)
            out_states["s5_re"].append(sr.reshape(b, S5_GROUPS, S5_STATE))
            out_states["s5_im"].append(si.reshape(b, S5_GROUPS, S5_STATE))
            out_states["swa_k"].append(kb.reshape(b, -1, SWA_KV_HEADS, SWA_HD))
            out_states["swa_v"].append(vb.reshape(b, -1, SWA_KV_HEADS, SWA_HD))
        if sample:
            q = _norm_matmul(x, w['mem_x_norm'][layer][None], w['mem_w_q'][layer], tm=tm, tn=MEM_HD,
                             head_gain=w['mem_q_norm'][layer][None], n_norm_tiles=MEM_HEADS)
            o = _mem_sample(q, mem_k, mem_v, layer=layer, bs=4, ls=l)
            x = _matmul_residual(x, [(o, w['mem_w_o'][layer])], tm=tm)
        else:
            x = _mem_prompt(x.reshape(b, l, d), w['mem_x_norm'][layer][None], w['mem_w_q'][layer],
                            w['mem_q_norm'][layer][None], mem_k[layer], mem_v[layer], w['mem_w_o'][layer],
                            lt=512).reshape(t, d)
        x = _ffn(x, w['ffn2_norm'][layer][None], w['ffn2_w_gate'][layer], w['ffn2_w_up'][layer],
                 w['ffn2_w_down'][layer], tm=tm, th=256)
    return x.reshape(b, l, d), {k: jnp.stack(v) for k, v in out_states.items()}


def kernel(x_prompt, x_sample, mem_prompt, state_gla, state_ret, state_s5_re, state_s5_im, cache_swa_k, cache_swa_v, cache_mem_k, cache_mem_v, ffn1_norm, ffn1_w_gate, ffn1_w_up, ffn1_w_down, ffn2_norm, ffn2_w_gate, ffn2_w_up, ffn2_w_down, mix_norm, even_w_in, gla_w_gate, gla_b_gate, gla_out_norm, even_w_out, odd_w_in, s5_a_re, s5_a_im, s5_log_step, s5_b_re, s5_b_im, s5_c_re, s5_c_im, s5_d, s5_w_glu, s5_b_glu, swa_q_norm, swa_k_norm, swa_sinks, odd_w_out, mem_x_norm, mem_m_norm, mem_w_q, mem_w_k, mem_w_v, mem_w_o, mem_q_norm, mem_k_norm):
    depth = ffn1_norm.shape[0]
    n_even, n_odd = even_w_in.shape[0], odd_w_in.shape[0]
    batch, seq, d = x_prompt.shape
    dec_batch = x_sample.shape[0]
    n_mem = mem_prompt.shape[1]

    ev = even_w_in
    ev_cols = jnp.concatenate(
        [ev[..., 0:1536], ev[..., 1552:3088], ev[..., 1536:1552],
         jnp.zeros(ev.shape[:2] + (EV_COLS - 3088,), ev.dtype)], axis=-1)
    wgate_pad = jnp.concatenate(
        [gla_w_gate, jnp.zeros((n_even, EV_COLS - EV_GA - GLA_RANK, gla_w_gate.shape[-1]), gla_w_gate.dtype)], axis=1)
    w = dict(
        ffn1_norm=ffn1_norm, ffn2_norm=ffn2_norm, mix_norm=mix_norm, mem_x_norm=mem_x_norm,
        ffn1_w_gate=_bf(ffn1_w_gate), ffn1_w_up=_bf(ffn1_w_up), ffn1_w_down=_bf(ffn1_w_down),
        ffn2_w_gate=_bf(ffn2_w_gate), ffn2_w_up=_bf(ffn2_w_up), ffn2_w_down=_bf(ffn2_w_down),
        even_w_in=_bf(ev_cols), gla_w_gate=_bf(wgate_pad), gla_b_gate=gla_b_gate[:, None, :],
        gla_out_norm=gla_out_norm[:, None, :], even_w_out=_bf(even_w_out),
        odd_w_in=_bf(odd_w_in), odd_w_out=_bf(odd_w_out), s5_d=s5_d, s5_w_glu=_bf(s5_w_glu), s5_b_glu=s5_b_glu,
        swa_q_norm=swa_q_norm, swa_k_norm=swa_k_norm, swa_sinks=swa_sinks,
        mem_w_q=_bf(mem_w_q), mem_w_o=_bf(mem_w_o), mem_q_norm=mem_q_norm,
    )
    ab_re, ab_im, bb_re, bb_im = [], [], [], []
    for i in range(n_odd):
        a_r, a_i, b_r, b_i = _s5_prep(s5_a_re[i], s5_a_im[i], s5_log_step[i], s5_b_re[i], s5_b_im[i])
        ab_re.append(a_r.reshape(-1))
        ab_im.append(a_i.reshape(-1))
        bb_re.append(_half_block_diag(b_r))
        bb_im.append(_half_block_diag(b_i))
    w.update(s5_ab_re=ab_re, s5_ab_im=ab_im, s5_bb_re=bb_re, s5_bb_im=bb_im,
             s5_cc_re=[_half_block_diag(jnp.swapaxes(s5_c_re[i], 1, 2)) for i in range(n_odd)],
             s5_cc_im=[_half_block_diag(jnp.swapaxes(s5_c_im[i], 1, 2)) for i in range(n_odd)])

    mem2 = mem_prompt.reshape(batch * n_mem, d)
    p_mem_k, p_mem_v = [], []
    for layer in range(depth):
        w_kv = _bf(jnp.concatenate([mem_w_k[layer], mem_w_v[layer]], axis=1))
        kv = _norm_matmul(mem2, mem_m_norm[layer][None], w_kv, tm=batch * n_mem, tn=MEM_HD,
                          head_gain=mem_k_norm[layer][None], n_norm_tiles=MEM_HEADS)
        p_mem_k.append(kv[:, :d].reshape(batch, n_mem, d))
        p_mem_v.append(kv[:, d:].reshape(batch, n_mem, d))

    zeros = lambda *s: jnp.zeros(s, F32)
    p_states = (zeros(n_even, batch, GLA_HEADS, GLA_DK, GLA_DV), zeros(n_even, batch, RET_HEADS, RET_DK, GLA_DV),
                zeros(n_odd, batch, S5_GROUPS, S5_STATE), zeros(n_odd, batch, S5_GROUPS, S5_STATE), None, None)
    y_prompt, ps = _trunk(x_prompt, 0.0, p_states, p_mem_k, p_mem_v, w, sample=False)

    s_states = (state_gla, state_ret, state_s5_re, state_s5_im, cache_swa_k, cache_swa_v)
    y_sample, ss = _trunk(x_sample, float(PAST_LEN), s_states, cache_mem_k, cache_mem_v, w, sample=True)

    p_mem_k = jnp.stack(p_mem_k).reshape(depth, batch, n_mem, MEM_HEADS, MEM_HD)
    p_mem_v = jnp.stack(p_mem_v).reshape(depth, batch, n_mem, MEM_HEADS, MEM_HD)
    return (y_prompt, y_sample, ps["gla"], ps["ret"], ps["s5_re"], ps["s5_im"], ps["swa_k"], ps["swa_v"],
            p_mem_k, p_mem_v, ss["gla"], ss["ret"], ss["s5_re"], ss["s5_im"], ss["swa_k"], ss["swa_v"])
```

```python
import functools
import math

import jax
import jax.numpy as jnp
import numpy as np
from jax import lax
from jax.experimental import pallas as pl
from jax.experimental.pallas import tpu as pltpu

F32 = jnp.float32
BF16 = jnp.bfloat16
NORM_EPS = 1e-6
HIGHEST = lax.Precision.HIGHEST

D_MODEL = 1024
GLA_HEADS = 4
GLA_DK = 64
GLA_DV = 128
GLA_RANK = 16
GLA_TAU = 16.0
RET_HEADS = 4
RET_DK = 64
RET_THETA = 10000.0
LA_CHUNK = 64
S5_WIDTH = 512
S5_GROUP = 16
S5_GROUPS = 32
S5_STATE = 64
SWA_HD = 64
SWA_HEADS = 8
SWA_KV_HEADS = 2
SWA_WINDOW = 128
ROPE_THETA = 500000.0
MEM_HEADS = 4
MEM_HD = 256
PAST_LEN = 8192

VMEM_LIMIT_BYTES = 52 * 1024 * 1024
LANES = 128
SUBLANES = 8


def _params(*sem):
    return pltpu.CompilerParams(dimension_semantics=sem, vmem_limit_bytes=VMEM_LIMIT_BYTES)


def _rms(x, gain=None):
    y = x * lax.rsqrt(jnp.mean(x * x, axis=-1, keepdims=True) + NORM_EPS)
    return y if gain is None else y * gain


def _dot(a, b):
    return jnp.dot(a, b, preferred_element_type=F32)


def _dot_nt(a, b):
    return lax.dot_general(a, b, (((1,), (1,)), ((), ())), preferred_element_type=F32)


def _dot_tn(a, b):
    return lax.dot_general(a, b, (((0,), (0,)), ((), ())), preferred_element_type=F32)


def _dot_f32(a, b):
    return jnp.dot(a, b, precision=HIGHEST, preferred_element_type=F32)


def _dot_nt_f32(a, b):
    return lax.dot_general(a, b, (((1,), (1,)), ((), ())), precision=HIGHEST, preferred_element_type=F32)


def _bf(x):
    return x.astype(BF16)


def _log_sigmoid(x):
    return jnp.minimum(x, 0.0) - jnp.log1p(jnp.exp(-jnp.abs(x)))


def _iota(shape, dim):
    return lax.broadcasted_iota(jnp.int32, shape, dim)


def _swap_pairs(x):
    n = x.shape[-1]
    even = (_iota(x.shape, 1) & 1) == 0
    return jnp.where(even, pltpu.roll(x, n - 1, 1), pltpu.roll(x, 1, 1))


def _rope_partner(x, head_dim, half):
    n = x.shape[-1]
    first = (_iota(x.shape, 1) & (head_dim - 1)) < half
    return jnp.where(first, pltpu.roll(x, n - half, 1), pltpu.roll(x, half, 1))


def _ffn_kernel(x_ref, g_ref, wg_ref, wu_ref, wd_ref, o_ref, xn_ref, acc_ref):
    j = pl.program_id(1)

    @pl.when(j == 0)
    def _():
        xn_ref[...] = _bf(_rms(x_ref[...], g_ref[...]))
        acc_ref[...] = jnp.zeros_like(acc_ref)

    xn = xn_ref[...]
    gate = _dot(xn, wg_ref[...])
    up = _dot(xn, wu_ref[...])
    acc_ref[...] += _dot(_bf(jax.nn.silu(gate) * up), wd_ref[...])

    @pl.when(j == pl.num_programs(1) - 1)
    def _():
        o_ref[...] = x_ref[...] + 0.5 * acc_ref[...]


def _ffn(x, gain, wg, wu, wd, *, tm, th):
    t, d = x.shape
    h = wg.shape[1]
    return pl.pallas_call(
        _ffn_kernel,
        grid=(t // tm, h // th),
        in_specs=[
            pl.BlockSpec((tm, d), lambda i, j: (i, 0)),
            pl.BlockSpec((1, d), lambda i, j: (0, 0)),
            pl.BlockSpec((d, th), lambda i, j: (0, j)),
            pl.BlockSpec((d, th), lambda i, j: (0, j)),
            pl.BlockSpec((th, d), lambda i, j: (j, 0)),
        ],
        out_specs=pl.BlockSpec((tm, d), lambda i, j: (i, 0)),
        out_shape=jax.ShapeDtypeStruct((t, d), F32),
        scratch_shapes=[pltpu.VMEM((tm, d), BF16), pltpu.VMEM((tm, d), F32)],
        compiler_params=_params("parallel", "arbitrary"),
        name="ffn",
    )(x, gain, wg, wu, wd)


def _nmm_kernel(x_ref, g_ref, w_ref, hg_ref, o_ref, xn_ref, *, n_norm_tiles):
    j = pl.program_id(1)

    @pl.when(j == 0)
    def _():
        xn_ref[...] = _bf(_rms(x_ref[...], g_ref[...]))

    y = _dot(xn_ref[...], w_ref[...])
    if n_norm_tiles == 0:
        o_ref[...] = y
    else:
        @pl.when(j < n_norm_tiles)
        def _():
            o_ref[...] = _rms(y, hg_ref[...])

        @pl.when(j >= n_norm_tiles)
        def _():
            o_ref[...] = y


def _norm_matmul(x, gain, w, *, tm, tn, head_gain=None, n_norm_tiles=0):
    t, d = x.shape
    n = w.shape[1]
    if head_gain is None:
        head_gain = jnp.ones((1, tn), F32)
    return pl.pallas_call(
        functools.partial(_nmm_kernel, n_norm_tiles=n_norm_tiles),
        grid=(t // tm, n // tn),
        in_specs=[
            pl.BlockSpec((tm, d), lambda i, j: (i, 0)),
            pl.BlockSpec((1, d), lambda i, j: (0, 0)),
            pl.BlockSpec((d, tn), lambda i, j: (0, j)),
            pl.BlockSpec((1, tn), lambda i, j: (0, 0)),
        ],
        out_specs=pl.BlockSpec((tm, tn), lambda i, j: (i, j)),
        out_shape=jax.ShapeDtypeStruct((t, n), F32),
        scratch_shapes=[pltpu.VMEM((tm, d), BF16)],
        compiler_params=_params("parallel", "arbitrary"),
        name="norm_matmul",
    )(x, gain, w, head_gain)


def _mmr_kernel(*refs, n_terms):
    x_ref = refs[0]
    a_refs = refs[1:1 + n_terms]
    w_refs = refs[1 + n_terms:1 + 2 * n_terms]
    o_ref = refs[1 + 2 * n_terms]
    acc = x_ref[...]
    for a_ref, w_ref in zip(a_refs, w_refs):
        acc = acc + _dot(_bf(a_ref[...]), w_ref[...])
    o_ref[...] = acc


def _matmul_residual(x, terms, *, tm):
    t, d = x.shape
    acts = [a for a, _ in terms]
    ws = [w for _, w in terms]
    in_specs = [pl.BlockSpec((tm, d), lambda i: (i, 0))]
    in_specs += [pl.BlockSpec((tm, a.shape[1]), lambda i: (i, 0)) for a in acts]
    in_specs += [pl.BlockSpec(w.shape, lambda i: (0, 0)) for w in ws]
    return pl.pallas_call(
        functools.partial(_mmr_kernel, n_terms=len(terms)),
        grid=(t // tm,),
        in_specs=in_specs,
        out_specs=pl.BlockSpec((tm, d), lambda i: (i, 0)),
        out_shape=jax.ShapeDtypeStruct((t, d), F32),
        compiler_params=_params("parallel"),
        name="matmul_residual",
    )(x, *acts, *ws)


EV_GQ, EV_GK, EV_GV, EV_GG = 0, 256, 512, 1024
EV_RQ, EV_RK, EV_RV, EV_RG = 1536, 1792, 2048, 2560
EV_GA = 3072
EV_COLS = 3200
EV_BLOCK = 256
PAIR = 2


def _gate_and_norm(o, gate, gain=None):
    return _rms(o, gain) * jax.nn.silu(gate)


def _even_prompt_kernel(x_ref, gain_ref, win_ref, wout_ref, cos_ref, sin_ref, ld_ref, wgate_ref, bgate_ref, gnorm_ref,
                        s0g_ref, s0r_ref, o_ref, sg_ref, sr_ref, proj_sc, mix_sc, o_sc, *, chunk, n_chunks):
    @pl.when(pl.program_id(1) == 0)
    def _():
        sg_ref[...] = s0g_ref[...]
        sr_ref[...] = s0r_ref[...]

    c = chunk
    lt = c * n_chunks
    blk = min(EV_BLOCK, lt)
    c_shift = c.bit_length() - 1
    x = x_ref[0]
    proj_sc[...] = _dot(_bf(_rms(x, gain_ref[...])), win_ref[...])

    def cols(a, b):
        return proj_sc[:, a:b]

    log_a = _log_sigmoid(_dot(_bf(cols(EV_GA, EV_COLS)), wgate_ref[...]) + bgate_ref[...]) * (1.0 / GLA_TAU)
    tril = (_iota((c, c), 1) <= _iota((c, c), 0)).astype(F32)
    cum_parts = [_dot_f32(tril, log_a[i * c:(i + 1) * c]) for i in range(n_chunks)]
    tots = [p[c - 1:c] for p in cum_parts]
    cum = jnp.concatenate(cum_parts, axis=0)
    tot_b = jnp.concatenate([jnp.broadcast_to(t, (c, 256)) for t in tots], axis=0)
    k = cols(EV_GK, EV_GK + 256)
    gla = (cols(EV_GQ, EV_GQ + 256) * (GLA_DK ** -0.5) * jnp.exp(cum), k * jnp.exp(-cum), k * jnp.exp(tot_b - cum))

    ld = ld_ref[...]
    tpos = ((_iota((lt, 1), 0) & (c - 1)) + 1).astype(F32)
    cum_r = tpos * ld
    tot_r = float(c) * ld
    cos, sin = cos_ref[...], sin_ref[...]
    rq = cols(EV_RQ, EV_RQ + 256)
    rk = cols(EV_RK, EV_RK + 256)
    q_rot = rq * cos + _swap_pairs(rq) * sin
    k_rot = (rk * cos + _swap_pairs(rk) * sin) * (RET_DK ** -0.5)
    ret = (q_rot * jnp.exp(cum_r), k_rot * jnp.exp(-cum_r), k_rot * jnp.exp(tot_r - cum_r))

    tot_rows = jnp.concatenate(tots + [tot_r, jnp.zeros((LANES - n_chunks - 1, 256), F32)], axis=0)
    decay_cols = jnp.exp(jnp.transpose(tot_rows))

    row = _iota((blk, blk), 0)
    col = _iota((blk, blk), 1)
    blk_mask = ((row >> c_shift) == (col >> c_shift)) & (col <= row)
    lo = (_iota((lt, LANES), 1) < GLA_DK)
    gnorm = gnorm_ref[...]

    mixers = ((gla, EV_GV, EV_GG, sg_ref, 0, gnorm, lambda i: i),
              (ret, EV_RV, EV_RG, sr_ref, 512, None, lambda i: n_chunks))
    for m, ((q_dec, k_inv, k_dec), v_col, g_col, s_ref, out_col, gain, decay_col_of) in enumerate(mixers):
        for p in range(GLA_HEADS // PAIR):
            lanes = slice(p * LANES, (p + 1) * LANES)
            q_pair = q_dec[:, lanes]
            q_masked = [_bf(jnp.where(lo, q_pair, 0.0)), _bf(jnp.where(lo, 0.0, q_pair))]
            ki = _bf(k_inv[:, lanes])
            kd = _bf(k_dec[:, lanes])
            v_pair = _bf(cols(v_col + p * PAIR * GLA_DV, v_col + (p + 1) * PAIR * GLA_DV))
            for e in range(PAIR):
                slot = (m * (GLA_HEADS // PAIR) + p) * PAIR + e
                for r0 in range(0, lt, blk):
                    rs = slice(r0, r0 + blk)
                    scores = jnp.where(blk_mask, _dot_nt(q_masked[e][rs], ki[rs]), 0.0)
                    o_sc[slot, rs, :] = _dot(_bf(scores), v_pair[rs, e * GLA_DV:(e + 1) * GLA_DV])
            state = s_ref[0, p * PAIR:(p + 1) * PAIR].reshape(PAIR * GLA_DK, GLA_DV)
            for i in range(n_chunks):
                rs = slice(i * c, (i + 1) * c)
                q_stack = jnp.concatenate([q_masked[0][rs], q_masked[1][rs]], axis=0)
                o_inter = _dot(q_stack, _bf(state))
                kv = _dot_tn(kd[rs], v_pair[rs])
                kv = jnp.concatenate([kv[:GLA_DK, :GLA_DV], kv[GLA_DK:, GLA_DV:]], axis=0)
                ci = decay_col_of(i)
                state = state * decay_cols[p * LANES:(p + 1) * LANES, ci:ci + 1] + kv
                for e in range(PAIR):
                    h = p * PAIR + e
                    slot = (m * (GLA_HEADS // PAIR) + p) * PAIR + e
                    o = o_sc[slot, rs, :] + o_inter[e * c:(e + 1) * c]
                    gate = proj_sc[rs, g_col + h * GLA_DV:g_col + (h + 1) * GLA_DV]
                    mix_sc[rs, out_col + h * GLA_DV:out_col + (h + 1) * GLA_DV] = _gate_and_norm(o, gate, gain)
            s_ref[0, p * PAIR:(p + 1) * PAIR] = state.reshape(PAIR, GLA_DK, GLA_DV)
    o_ref[0] = x + _dot(_bf(mix_sc[...]), wout_ref[...])


def _even_prompt(x, gain, w_in, w_out, cos, sin, ld_row, wgate, bgate, gnorm, s0g, s0r, *, lt):
    b, l, d = x.shape
    chunk = math.gcd(l, LA_CHUNK)
    st_spec = pl.BlockSpec((1, GLA_HEADS, GLA_DK, GLA_DV), lambda i, j: (i, 0, 0, 0))
    io_spec = pl.BlockSpec((1, lt, d), lambda i, j: (i, j, 0))
    full = lambda a: pl.BlockSpec(a.shape, lambda i, j: (0,) * a.ndim)
    return pl.pallas_call(
        functools.partial(_even_prompt_kernel, chunk=chunk, n_chunks=lt // chunk),
        grid=(b, l // lt),
        in_specs=[
            io_spec, full(gain), full(w_in), full(w_out),
            pl.BlockSpec((lt, 256), lambda i, j: (j, 0)),
            pl.BlockSpec((lt, 256), lambda i, j: (j, 0)),
            full(ld_row), full(wgate), full(bgate), full(gnorm), st_spec, st_spec,
        ],
        out_specs=[io_spec, st_spec, st_spec],
        out_shape=[jax.ShapeDtypeStruct(x.shape, F32),
                   jax.ShapeDtypeStruct(s0g.shape, F32), jax.ShapeDtypeStruct(s0r.shape, F32)],
        scratch_shapes=[pltpu.VMEM((lt, EV_COLS), F32), pltpu.VMEM((lt, d), F32),
                        pltpu.VMEM((GLA_HEADS + RET_HEADS, lt, GLA_DV), F32)],
        compiler_params=_params("parallel", "arbitrary"),
        name="even_prompt",
    )(x, gain, w_in, w_out, cos, sin, ld_row, wgate, bgate, gnorm, s0g, s0r)


def _even_sample_kernel(proj_ref, cos_ref, sin_ref, ld_ref, wgate_ref, bgate_ref, gnorm_ref, s0g_ref, s0r_ref,
                        mix_ref, sg_ref, sr_ref, *, bg, ls):
    r = bg * ls
    ls_shift = ls.bit_length() - 1
    dk_shift = GLA_DK.bit_length() - 1
    n_exp = bg * GLA_DK
    row_seq = _iota((r, r), 0) >> ls_shift
    col_seq = _iota((r, r), 1) >> ls_shift
    same = row_seq == col_seq
    seg = same & (_iota((r, r), 1) <= _iota((r, r), 0))
    seg_f = seg.astype(F32)
    same_f = same.astype(F32)
    tile_b = _bf(((_iota((GLA_DK, n_exp), 1) & (GLA_DK - 1)) == _iota((GLA_DK, n_exp), 0)).astype(F32))
    tile_t_f = ((_iota((n_exp, GLA_DK), 0) & (GLA_DK - 1)) == _iota((n_exp, GLA_DK), 1)).astype(F32)
    tile_t_b = _bf(tile_t_f)
    q_mask = (_iota((r, n_exp), 0) >> ls_shift) == (_iota((r, n_exp), 1) >> dk_shift)
    k_mask = (_iota((n_exp, r), 0) >> dk_shift) == (_iota((n_exp, r), 1) >> ls_shift)
    tpos = ((_iota((r, 1), 0) & (ls - 1)) + 1).astype(F32)
    ld = ld_ref[...]
    cum_r = tpos * ld
    tot_r = float(ls) * ld
    r_dec = jnp.exp(tot_r)
    gnorm = gnorm_ref[...]

    def mixer(q_dec, k_inv, k_dec, v_col, g_col, s0_ref, s_ref, out_col, decay_of, gain):
        for h in range(GLA_HEADS):
            sl = slice(h * GLA_DK, (h + 1) * GLA_DK)
            v = _bf(proj_ref[:, v_col + h * GLA_DV:v_col + (h + 1) * GLA_DV])
            qd = _bf(q_dec[:, sl])
            scores = jnp.where(seg, _dot_nt(qd, _bf(k_inv[:, sl])), 0.0)
            state = s0_ref[:, h].reshape(n_exp, GLA_DV)
            q_exp = _bf(jnp.where(q_mask, _dot(qd, tile_b), 0.0))
            o = _dot(_bf(scores), v) + _dot(q_exp, _bf(state))
            k_exp = _bf(jnp.where(k_mask, _dot_nt(tile_t_b, _bf(k_dec[:, sl])), 0.0))
            new_state = state * decay_of(h, sl) + _dot(k_exp, v)
            s_ref[:, h] = new_state.reshape(bg, GLA_DK, GLA_DV)
            gate = proj_ref[:, g_col + h * GLA_DV:g_col + (h + 1) * GLA_DV]
            mix_ref[:, out_col + h * GLA_DV:out_col + (h + 1) * GLA_DV] = _gate_and_norm(o, gate, gain)

    log_a = _log_sigmoid(_dot(_bf(proj_ref[:, EV_GA:EV_COLS]), wgate_ref[...]) + bgate_ref[...]) * (1.0 / GLA_TAU)
    cum = _dot_f32(seg_f, log_a)
    tot = _dot_f32(same_f, log_a)
    k = proj_ref[:, EV_GK:EV_GK + 256]

    def gla_decay(h, sl):
        la_exp = jnp.where(k_mask, _dot_nt_f32(tile_t_f, log_a[:, sl]), 0.0)
        return jnp.exp(jnp.sum(la_exp, axis=-1, keepdims=True))

    mixer(proj_ref[:, EV_GQ:EV_GQ + 256] * (GLA_DK ** -0.5) * jnp.exp(cum), k * jnp.exp(-cum), k * jnp.exp(tot - cum),
          EV_GV, EV_GG, s0g_ref, sg_ref, 0, gla_decay, gnorm)

    cos = cos_ref[...]
    sin = sin_ref[...]
    rq = proj_ref[:, EV_RQ:EV_RQ + 256]
    rk = proj_ref[:, EV_RK:EV_RK + 256]
    q_rot = rq * cos + _swap_pairs(rq) * sin
    k_rot = (rk * cos + _swap_pairs(rk) * sin) * (RET_DK ** -0.5)

    def ret_decay(h, sl):
        return r_dec[:, h * RET_DK:h * RET_DK + 1]

    mixer(q_rot * jnp.exp(cum_r), k_rot * jnp.exp(-cum_r), k_rot * jnp.exp(tot_r - cum_r),
          EV_RV, EV_RG, s0r_ref, sr_ref, 512, ret_decay, None)


def _even_sample(proj, cos, sin, ld_row, wgate, bgate, gnorm, s0g, s0r, *, bg, ls):
    t = proj.shape[0]
    n_b = t // ls
    r = bg * ls
    st_spec = pl.BlockSpec((bg, GLA_HEADS, GLA_DK, GLA_DV), lambda i: (i, 0, 0, 0))
    full = lambda a: pl.BlockSpec(a.shape, lambda i: (0,) * a.ndim)
    return pl.pallas_call(
        functools.partial(_even_sample_kernel, bg=bg, ls=ls),
        grid=(n_b // bg,),
        in_specs=[
            pl.BlockSpec((r, EV_COLS), lambda i: (i, 0)),
            full(cos), full(sin), full(ld_row), full(wgate), full(bgate), full(gnorm), st_spec, st_spec,
        ],
        out_specs=[pl.BlockSpec((r, D_MODEL), lambda i: (i, 0)), st_spec, st_spec],
        out_shape=[jax.ShapeDtypeStruct((t, D_MODEL), F32),
                   jax.ShapeDtypeStruct(s0g.shape, F32), jax.ShapeDtypeStruct(s0r.shape, F32)],
        compiler_params=_params("parallel"),
        name="even_sample",
    )(proj, cos, sin, ld_row, wgate, bgate, gnorm, s0g, s0r)


def _s5_prep_kernel(are_ref, aim_ref, lstep_ref, bre_ref, bim_ref, abre_ref, abim_ref, bbre_ref, bbim_ref):
    a_re, a_im = are_ref[...], aim_ref[...]
    step = jnp.exp(lstep_ref[...])
    mag = jnp.exp(a_re * step)
    ab_re = mag * jnp.cos(a_im * step)
    ab_im = mag * jnp.sin(a_im * step)
    den = a_re * a_re + a_im * a_im
    coef_re = ((ab_re - 1.0) * a_re + ab_im * a_im) / den
    coef_im = (ab_im * a_re - (ab_re - 1.0) * a_im) / den
    b_re, b_im = bre_ref[...], bim_ref[...]
    abre_ref[...] = ab_re
    abim_ref[...] = ab_im
    bbre_ref[...] = coef_re * b_re - coef_im * b_im
    bbim_ref[...] = coef_re * b_im + coef_im * b_re


def _s5_prep(a_re, a_im, log_step, b_re, b_im):
    g, n = a_re.shape
    shp3 = jax.ShapeDtypeStruct((g, 1, n), F32)
    shpb = jax.ShapeDtypeStruct((g, S5_GROUP, n), F32)
    return pl.pallas_call(_s5_prep_kernel, out_shape=[shp3, shp3, shpb, shpb], name="s5_prep")(
        a_re.reshape(g, 1, n), a_im.reshape(g, 1, n), log_step.reshape(g, 1, 1),
        jnp.swapaxes(b_re, 1, 2), jnp.swapaxes(b_im, 1, 2))


S5_LANE_CHUNKS = S5_GROUPS * S5_STATE // LANES
S5_HALVES = 2


def _s5_layout(n_seq, lt):
    pack = max(1, SUBLANES // n_seq)
    pitch = lt + 4 if lt % SUBLANES == 0 else lt
    return pack, S5_LANE_CHUNKS // pack, pitch


def _s5_slot(c, n_groups):
    return c % n_groups, c // n_groups


def _s5_load_state(s0re_ref, s0im_ref, hre_ref, him_ref, n_seq, n_groups):
    for c in range(S5_LANE_CHUNKS):
        g, j = _s5_slot(c, n_groups)
        hre_ref[g, j * n_seq:(j + 1) * n_seq, :] = s0re_ref[:, c * LANES:(c + 1) * LANES]
        him_ref[g, j * n_seq:(j + 1) * n_seq, :] = s0im_ref[:, c * LANES:(c + 1) * LANES]


def _s5_store_state(sre_ref, sim_ref, hre_ref, him_ref, n_seq, n_groups):
    for c in range(S5_LANE_CHUNKS):
        g, j = _s5_slot(c, n_groups)
        sre_ref[:, c * LANES:(c + 1) * LANES] = hre_ref[g, j * n_seq:(j + 1) * n_seq, :]
        sim_ref[:, c * LANES:(c + 1) * LANES] = him_ref[g, j * n_seq:(j + 1) * n_seq, :]


def _s5_core(u, abre_ref, abim_ref, bbre_ref, bbim_ref, ccre_ref, ccim_ref, d_ref, wglu_ref, bglu_ref,
             xr_ref, xi_ref, hre_ref, him_ref, *, n_seq, lt, groups_per_pass, unroll):
    pack, n_groups, pitch = _s5_layout(n_seq, lt)
    rows = n_seq * lt
    per_half = S5_LANE_CHUNKS // S5_HALVES
    lanes_of = lambda c: slice(c * LANES, (c + 1) * LANES)
    slot_of = lambda c: _s5_slot(c, n_groups)

    def seq_rows(j, s):
        return slice((j * n_seq + s) * pitch, (j * n_seq + s) * pitch + lt)

    ub = _bf(u)
    u_cols = S5_WIDTH // S5_HALVES
    for half in range(S5_HALVES):
        uh = ub[:, half * u_cols:(half + 1) * u_cols]
        for x_ref, bb_ref in ((xr_ref, bbre_ref), (xi_ref, bbim_ref)):
            x = _dot(uh, bb_ref[half])
            for k in range(per_half):
                g, j = slot_of(half * per_half + k)
                if pitch == lt:
                    x_ref[g, j * rows:(j + 1) * rows, :] = x[:, lanes_of(k)]
                else:
                    for s in range(n_seq):
                        x_ref[g, seq_rows(j, s), :] = x[s * lt:(s + 1) * lt, lanes_of(k)]

    for g0 in range(0, n_groups, groups_per_pass):
        gs = list(range(g0, g0 + groups_per_pass))
        init = tuple(hre_ref[g] for g in gs) + tuple(him_ref[g] for g in gs)

        def step(t, carry, gs=gs):
            rws = pl.ds(t, pack * n_seq, stride=pitch)
            new_re, new_im = [], []
            for k, g in enumerate(gs):
                a_re, a_im = abre_ref[g], abim_ref[g]
                h_re, h_im = carry[k], carry[len(gs) + k]
                n_re = a_re * h_re - a_im * h_im + xr_ref[g, rws, :]
                n_im = a_re * h_im + a_im * h_re + xi_ref[g, rws, :]
                xr_ref[g, rws, :] = n_re
                xi_ref[g, rws, :] = n_im
                new_re.append(n_re)
                new_im.append(n_im)
            return tuple(new_re + new_im)

        fin = lax.fori_loop(0, lt, step, init, unroll=unroll)
        for k, g in enumerate(gs):
            hre_ref[g] = fin[k]
            him_ref[g] = fin[len(gs) + k]

    def gather(x_ref, half):
        cols = []
        for k in range(per_half):
            g, j = slot_of(half * per_half + k)
            if pitch == lt:
                cols.append(x_ref[g, j * rows:(j + 1) * rows, :])
            else:
                cols.append(jnp.concatenate([x_ref[g, seq_rows(j, s), :] for s in range(n_seq)], axis=0))
        return _bf(jnp.concatenate(cols, axis=1))

    y = jnp.concatenate([_dot(gather(xr_ref, half), ccre_ref[half]) - _dot(gather(xi_ref, half), ccim_ref[half])
                         for half in range(S5_HALVES)], axis=1) + d_ref[...] * u
    z = jax.nn.gelu(y, approximate=True)
    return z * jax.nn.sigmoid(_dot(_bf(z), wglu_ref[...]) + bglu_ref[...])


def _s5_kernel(u_ref, s0re_ref, s0im_ref, abre_ref, abim_ref, bbre_ref, bbim_ref, ccre_ref, ccim_ref,
               d_ref, wglu_ref, bglu_ref, out_ref, sre_ref, sim_ref, xr_ref, xi_ref, hre_ref, him_ref,
               *, n_seq, lt, groups_per_pass, unroll):
    n_groups = _s5_layout(n_seq, lt)[1]

    @pl.when(pl.program_id(0) == 0)
    def _():
        _s5_load_state(s0re_ref, s0im_ref, hre_ref, him_ref, n_seq, n_groups)

    out = _s5_core(u_ref[...].reshape(n_seq * lt, S5_WIDTH), abre_ref, abim_ref, bbre_ref, bbim_ref, ccre_ref, ccim_ref,
                   d_ref, wglu_ref, bglu_ref, xr_ref, xi_ref, hre_ref, him_ref,
                   n_seq=n_seq, lt=lt, groups_per_pass=groups_per_pass, unroll=unroll)
    out_ref[...] = out.reshape(out_ref.shape)

    @pl.when(pl.program_id(0) == pl.num_programs(0) - 1)
    def _():
        _s5_store_state(sre_ref, sim_ref, hre_ref, him_ref, n_seq, n_groups)


def _s5_tables(ab, n_seq, lt):
    pack, n_groups, _ = _s5_layout(n_seq, lt)
    tab = jnp.swapaxes(ab.reshape(pack, n_groups, 1, LANES), 0, 1)
    return jnp.broadcast_to(tab, (n_groups, pack, n_seq, LANES)).reshape(n_groups, pack * n_seq, LANES)


def _s5(proj3, s0_re, s0_im, ab_re, ab_im, bb_re, bb_im, cc_re, cc_im, d_row, wglu, bglu, *, n_seq, lt,
        groups_per_pass, unroll):
    nb, rows_b, _ = proj3.shape
    blk_rows = n_seq * lt // nb
    pack, n_groups, pitch = _s5_layout(n_seq, lt)
    ab_re, ab_im = _s5_tables(ab_re, n_seq, lt), _s5_tables(ab_im, n_seq, lt)
    full = lambda a: pl.BlockSpec(a.shape, lambda j: (0,) * a.ndim)
    io_spec = pl.BlockSpec((nb, blk_rows, S5_WIDTH), lambda j: (0, j, 0))
    x_scratch = pltpu.VMEM((n_groups, pack * n_seq * pitch, LANES), F32)
    h_scratch = pltpu.VMEM((n_groups, pack * n_seq, LANES), F32)
    return pl.pallas_call(
        functools.partial(_s5_kernel, n_seq=n_seq, lt=lt, groups_per_pass=groups_per_pass, unroll=unroll),
        grid=(rows_b // blk_rows,),
        in_specs=[io_spec, full(s0_re), full(s0_im), full(ab_re), full(ab_im), full(bb_re), full(bb_im),
                  full(cc_re), full(cc_im), full(d_row), full(wglu), full(bglu)],
        out_specs=[io_spec, full(s0_re), full(s0_im)],
        out_shape=[jax.ShapeDtypeStruct((nb, rows_b, S5_WIDTH), F32),
                   jax.ShapeDtypeStruct(s0_re.shape, F32), jax.ShapeDtypeStruct(s0_im.shape, F32)],
        scratch_shapes=[x_scratch, x_scratch, h_scratch, h_scratch],
        compiler_params=_params("arbitrary"),
        name="s5",
    )(proj3, s0_re, s0_im, ab_re, ab_im, bb_re, bb_im, cc_re, cc_im, d_row, wglu, bglu)


OD_Q_BLOCK = 1
OD_KV_BLOCK = 4
OD_COLS = 1280
ROPE_HALF = SWA_HD // 8
SWA_GROUP = SWA_HEADS // SWA_KV_HEADS


def _half_lanes(shape):
    return (_iota(shape, 1) & (LANES - 1)) < SWA_HD


def _pair_rms_scale(x):
    same_head = (_iota((LANES, LANES), 0) >= SWA_HD) == (_iota((LANES, LANES), 1) >= SWA_HD)
    sums = _dot_f32(x * x, same_head.astype(F32))
    return lax.rsqrt(sums * (1.0 / SWA_HD) + NORM_EPS)


def _swa_qk(xq, xk, qg, kg, cq, sq):
    xq_g = xq * qg
    q_rot = xq_g * cq + _rope_partner(xq_g, SWA_HD, ROPE_HALF) * sq
    xk_g = xk * kg
    k_rot = xk_g * cq[:, :LANES] + _rope_partner(xk_g, SWA_HD, ROPE_HALF) * sq[:, :LANES]
    q_pairs = [q_rot[:, j * LANES:(j + 1) * LANES] * _pair_rms_scale(xq[:, j * LANES:(j + 1) * LANES])
               for j in range(SWA_HEADS // 2)]
    return q_pairs, k_rot * _pair_rms_scale(xk)


def _swa_query_stack(q_pairs, kh):
    lo = _half_lanes(q_pairs[0].shape)
    keep = lo if kh == 0 else jnp.logical_not(lo)
    parts = []
    for g in range(SWA_GROUP):
        hq = kh * SWA_GROUP + g
        pair = q_pairs[hq // 2]
        src = pair if hq % 2 == kh else pltpu.roll(pair, SWA_HD, 1)
        parts.append(jnp.where(keep, src, 0.0))
    return _bf(jnp.concatenate(parts, axis=0))


def _swa_merge_heads(o, kh, rows):
    lo = _half_lanes((rows, LANES))
    pairs = []
    for p in range(SWA_GROUP // 2):
        even, odd = o[2 * p * rows:(2 * p + 1) * rows], o[(2 * p + 1) * rows:(2 * p + 2) * rows]
        if kh == 0:
            pairs.append(jnp.where(lo, even, pltpu.roll(odd, SWA_HD, 1)))
        else:
            pairs.append(jnp.where(lo, pltpu.roll(even, SWA_HD, 1), odd))
    return pairs


def _swa_block(xq, xk, v_cur, k_prev, v_prev, mask, qg, kg, cq, sq, sink_ref):
    w = SWA_WINDOW
    ones_col = jnp.ones((2 * w, LANES), BF16)
    q_pairs, k_cur = _swa_qk(xq, xk, qg, kg, cq, sq)
    k_ext = _bf(jnp.concatenate([k_prev, k_cur], axis=0))
    v_ext = _bf(jnp.concatenate([v_prev, v_cur], axis=0))
    out_pairs = []
    for kh in range(SWA_KV_HEADS):
        s_all = _dot_nt(_swa_query_stack(q_pairs, kh), k_ext)
        weights, sink_terms = [], []
        for g in range(SWA_GROUP):
            s = jnp.where(mask, s_all[g * w:(g + 1) * w] * (SWA_HD ** -0.5), -jnp.inf)
            sink = sink_ref[kh * SWA_GROUP + g:kh * SWA_GROUP + g + 1, :]
            m = jnp.maximum(jnp.broadcast_to(jnp.max(s, axis=-1, keepdims=True), (w, LANES)), sink)
            weights.append(_bf(jnp.exp(s - jnp.concatenate([m, m], axis=1))))
            sink_terms.append(jnp.exp(sink - m))
        weights = jnp.concatenate(weights, axis=0)
        den = _dot(weights, ones_col) + jnp.concatenate(sink_terms, axis=0)
        out_pairs += _swa_merge_heads(_dot(weights, v_ext) / den, kh, w)
    return out_pairs, k_cur


def _odd_prompt_kernel(x_ref, gain_ref, win_ref, wout_ref, s0re_ref, s0im_ref, abre_ref, abim_ref, bbre_ref, bbim_ref,
                       ccre_ref, ccim_ref, d_ref, wglu_ref, bglu_ref, cq_ref, sq_ref, qg_ref, kg_ref, sink_ref,
                       o_ref, sre_ref, sim_ref, ck_ref, cv_ref,
                       proj_sc, mix_sc, xr_ref, xi_ref, hre_ref, him_ref, kprev_sc, vprev_sc,
                       *, n_seq, groups_per_pass, unroll):
    w = SWA_WINDOW
    rows = n_seq * w
    n_groups = _s5_layout(n_seq, w)[1]
    step = pl.program_id(0)

    @pl.when(step == 0)
    def _():
        _s5_load_state(s0re_ref, s0im_ref, hre_ref, him_ref, n_seq, n_groups)
        kprev_sc[...] = jnp.zeros_like(kprev_sc)
        vprev_sc[...] = jnp.zeros_like(vprev_sc)

    x = x_ref[...].reshape(rows, D_MODEL)
    proj_sc[...] = _dot(_bf(_rms(x, gain_ref[...])), win_ref[...])
    mix_sc[:, 0:S5_WIDTH] = _s5_core(proj_sc[:, 0:S5_WIDTH], abre_ref, abim_ref, bbre_ref, bbim_ref, ccre_ref, ccim_ref,
                                     d_ref, wglu_ref, bglu_ref, xr_ref, xi_ref, hre_ref, him_ref,
                                     n_seq=n_seq, lt=w, groups_per_pass=groups_per_pass, unroll=unroll)

    t_idx = _iota((w, 2 * w), 0)
    s_idx = _iota((w, 2 * w), 1)
    mask = (s_idx > t_idx) & (s_idx <= t_idx + w) & (s_idx >= jnp.where(step == 0, w, 0))
    q0, k0, v0 = OD_Q_BLOCK * 512, OD_KV_BLOCK * 256, OD_KV_BLOCK * 256 + LANES
    for s in range(n_seq):
        rs = slice(s * w, (s + 1) * w)
        v_cur = proj_sc[rs, v0:v0 + LANES]
        pairs, k_cur = _swa_block(proj_sc[rs, q0:q0 + 512], proj_sc[rs, k0:k0 + LANES], v_cur, kprev_sc[s], vprev_sc[s],
                                  mask, qg_ref[...], kg_ref[...], cq_ref[...], sq_ref[...], sink_ref)
        for i, pair in enumerate(pairs):
            mix_sc[rs, S5_WIDTH + i * LANES:S5_WIDTH + (i + 1) * LANES] = pair
        kprev_sc[s] = k_cur
        vprev_sc[s] = v_cur
    o_ref[...] = (x + _dot(_bf(mix_sc[...]), wout_ref[...])).reshape(o_ref.shape)

    @pl.when(step == pl.num_programs(0) - 1)
    def _():
        _s5_store_state(sre_ref, sim_ref, hre_ref, him_ref, n_seq, n_groups)
        ck_ref[...] = kprev_sc[...]
        cv_ref[...] = vprev_sc[...]


def _odd_prompt(x, gain, w_in, w_out, s0_re, s0_im, ab_re, ab_im, bb_re, bb_im, cc_re, cc_im, d_row, wglu, bglu,
                cq, sq, qg, kg, sink_rows, *, groups_per_pass, unroll):
    b, l, d = x.shape
    w = SWA_WINDOW
    pack, n_groups, pitch = _s5_layout(b, w)
    ab_re, ab_im = _s5_tables(ab_re, b, w), _s5_tables(ab_im, b, w)
    full = lambda a: pl.BlockSpec(a.shape, lambda j: (0,) * a.ndim)
    io_spec = pl.BlockSpec((b, w, d), lambda j: (0, j, 0))
    tab_spec = pl.BlockSpec((w, 512), lambda j: (j, 0))
    cache_shape = jax.ShapeDtypeStruct((b, w, LANES), F32)
    x_scratch = pltpu.VMEM((n_groups, pack * b * pitch, LANES), F32)
    h_scratch = pltpu.VMEM((n_groups, pack * b, LANES), F32)
    kv_scratch = pltpu.VMEM((b, w, LANES), F32)
    return pl.pallas_call(
        functools.partial(_odd_prompt_kernel, n_seq=b, groups_per_pass=groups_per_pass, unroll=unroll),
        grid=(l // w,),
        in_specs=[io_spec, full(gain), full(w_in), full(w_out), full(s0_re), full(s0_im), full(ab_re), full(ab_im),
                  full(bb_re), full(bb_im), full(cc_re), full(cc_im), full(d_row), full(wglu), full(bglu),
                  tab_spec, tab_spec, full(qg), full(kg), full(sink_rows)],
        out_specs=[io_spec, full(s0_re), full(s0_im), pl.BlockSpec((b, w, LANES), lambda j: (0, 0, 0)),
                   pl.BlockSpec((b, w, LANES), lambda j: (0, 0, 0))],
        out_shape=[jax.ShapeDtypeStruct(x.shape, F32), jax.ShapeDtypeStruct(s0_re.shape, F32),
                   jax.ShapeDtypeStruct(s0_im.shape, F32), cache_shape, cache_shape],
        scratch_shapes=[pltpu.VMEM((b * w, OD_COLS), F32), pltpu.VMEM((b * w, d), F32),
                        x_scratch, x_scratch, h_scratch, h_scratch, kv_scratch, kv_scratch],
        compiler_params=_params("arbitrary"),
        name="odd_prompt",
    )(x, gain, w_in, w_out, s0_re, s0_im, ab_re, ab_im, bb_re, bb_im, cc_re, cc_im, d_row, wglu, bglu,
      cq, sq, qg, kg, sink_rows)


def _swa_sample_kernel(q_ref, kv_ref, ck_ref, cv_ref, cq_ref, sq_ref, qg_ref, kg_ref, sink_ref,
                       o_ref, nk_ref, nv_ref, *, bg, ls):
    w = SWA_WINDOW
    r = bg * ls
    ls_shift = ls.bit_length() - 1
    w_shift = w.bit_length() - 1
    rows_g = SWA_GROUP * r
    v_new = kv_ref[:, LANES:2 * LANES]
    q_pairs, k_new = _swa_qk(q_ref[...], kv_ref[:, 0:LANES], qg_ref[...], kg_ref[...], cq_ref[...], sq_ref[...])
    k_cache = _bf(ck_ref[...].reshape(bg * w, LANES))
    v_cache = _bf(cv_ref[...].reshape(bg * w, LANES))

    row = _iota((rows_g, bg * w), 0) & (r - 1)
    col = _iota((rows_g, bg * w), 1)
    mask_c = ((row >> ls_shift) == (col >> w_shift)) & ((col & (w - 1)) > (row & (ls - 1)))
    row_n = _iota((rows_g, r), 0) & (r - 1)
    col_n = _iota((rows_g, r), 1)
    mask_n = ((row_n >> ls_shift) == (col_n >> ls_shift)) & ((col_n & (ls - 1)) <= (row_n & (ls - 1)))
    for kh in range(SWA_KV_HEADS):
        q_stack = _swa_query_stack(q_pairs, kh)
        s_c = jnp.where(mask_c, _dot_nt(q_stack, k_cache) * (SWA_HD ** -0.5), -jnp.inf)
        s_n = jnp.where(mask_n, _dot_nt(q_stack, _bf(k_new)) * (SWA_HD ** -0.5), -jnp.inf)
        sink = jnp.concatenate([jnp.broadcast_to(sink_ref[kh * SWA_GROUP + g:kh * SWA_GROUP + g + 1, :], (r, LANES))
                                for g in range(SWA_GROUP)], axis=0)
        row_max = jnp.maximum(jnp.max(s_c, axis=-1, keepdims=True), jnp.max(s_n, axis=-1, keepdims=True))
        m = jnp.maximum(jnp.broadcast_to(row_max, (rows_g, LANES)), sink)
        e_c = _bf(jnp.exp(s_c - jnp.concatenate([m] * bg, axis=1)))
        e_n = _bf(jnp.exp(s_n - m[:, :r]))
        den = (_dot(e_c, jnp.ones((bg * w, LANES), BF16)) + _dot(e_n, jnp.ones((r, LANES), BF16))
               + jnp.exp(sink - m))
        o = (_dot(e_c, v_cache) + _dot(e_n, _bf(v_new))) / den
        for i, pair in enumerate(_swa_merge_heads(o, kh, r)):
            col_i = kh * (SWA_GROUP // 2) + i
            o_ref[:, col_i * LANES:(col_i + 1) * LANES] = pair

    nk_ref[:, 0:w - ls, :] = ck_ref[:, ls:w, :]
    nv_ref[:, 0:w - ls, :] = cv_ref[:, ls:w, :]
    for b in range(bg):
        nk_ref[b, w - ls:w, :] = k_new[b * ls:(b + 1) * ls, :]
        nv_ref[b, w - ls:w, :] = v_new[b * ls:(b + 1) * ls, :]


def _swa_sample(proj, cache_k, cache_v, cq, sq, qg, kg, sink_rows, *, bg, ls):
    t = proj.shape[0]
    w = SWA_WINDOW
    r = bg * ls
    full = lambda a: pl.BlockSpec(a.shape, lambda i: (0,) * a.ndim)
    cache_spec = pl.BlockSpec((bg, w, LANES), lambda i: (i, 0, 0))
    return pl.pallas_call(
        functools.partial(_swa_sample_kernel, bg=bg, ls=ls),
        grid=(t // r,),
        in_specs=[
            pl.BlockSpec((r, 512), lambda i: (i, OD_Q_BLOCK)),
            pl.BlockSpec((r, 256), lambda i: (i, OD_KV_BLOCK)),
            cache_spec, cache_spec, full(cq), full(sq), full(qg), full(kg), full(sink_rows),
        ],
        out_specs=[pl.BlockSpec((r, 512), lambda i: (i, 0)), cache_spec, cache_spec],
        out_shape=[jax.ShapeDtypeStruct((t, 512), F32),
                   jax.ShapeDtypeStruct(cache_k.shape, F32), jax.ShapeDtypeStruct(cache_v.shape, F32)],
        compiler_params=_params("parallel"),
        name="swa_sample",
    )(proj, proj, cache_k, cache_v, cq, sq, qg, kg, sink_rows)


def _mem_prompt_kernel(x_ref, g_ref, wq_ref, qg_ref, k_ref, v_ref, wo_ref, o_ref):
    x = x_ref[0]
    xn = _bf(_rms(x, g_ref[...]))
    acc = x
    for h in range(MEM_HEADS):
        sl = slice(h * MEM_HD, (h + 1) * MEM_HD)
        q = _bf(_rms(_dot(xn, wq_ref[:, sl]), qg_ref[...]))
        s = _dot_nt(q, _bf(k_ref[0, :, sl])) * (MEM_HD ** -0.5)
        e = jnp.exp(s - jnp.max(s, axis=-1, keepdims=True))
        p = e / jnp.sum(e, axis=-1, keepdims=True)
        o = _dot(_bf(p), _bf(v_ref[0, :, sl]))
        acc = acc + _dot(_bf(o), wo_ref[sl, :])
    o_ref[0] = acc


def _mem_prompt(x, gain, wq, q_gain, k, v, wo, *, lt):
    b, l, d = x.shape
    n_mem = k.shape[1]
    kv_spec = pl.BlockSpec((1, n_mem, d), lambda i, j: (i, 0, 0))
    io_spec = pl.BlockSpec((1, lt, d), lambda i, j: (i, j, 0))
    full = lambda a: pl.BlockSpec(a.shape, lambda i, j: (0,) * a.ndim)
    return pl.pallas_call(
        _mem_prompt_kernel,
        grid=(b, l // lt),
        in_specs=[io_spec, full(gain), full(wq), full(q_gain), kv_spec, kv_spec, full(wo)],
        out_specs=io_spec,
        out_shape=jax.ShapeDtypeStruct(x.shape, F32),
        compiler_params=_params("parallel", "arbitrary"),
        name="mem_prompt",
    )(x, gain, wq, q_gain, k, v, wo)


def _mem_sample_kernel(q_ref, k_ref, v_ref, o_ref, *, bs, ls):
    r = bs * ls
    n_rows = MEM_HEADS * ls
    hd_shift = MEM_HD.bit_length() - 1
    ls_shift = ls.bit_length() - 1
    qb = _bf(q_ref[...])
    head_mask = (_iota((n_rows, D_MODEL), 0) >> ls_shift) == (_iota((n_rows, D_MODEL), 1) >> hd_shift)
    acc = jnp.zeros((r, D_MODEL), F32)
    for b in range(bs):
        sel = _bf(((_iota((n_rows, r), 0) & (ls - 1)) + b * ls == _iota((n_rows, r), 1)).astype(F32))
        q_exp = _bf(jnp.where(head_mask, _dot(sel, qb), 0.0))
        k_b = _mem_rows(k_ref, b)
        v_b = _mem_rows(v_ref, b)
        s = _dot_nt(_bf(k_b), q_exp) * (MEM_HD ** -0.5)
        e = jnp.exp(s - jnp.max(s, axis=0, keepdims=True))
        p = e / jnp.sum(e, axis=0, keepdims=True)
        o_all = jnp.where(head_mask, _dot_tn(_bf(p), _bf(v_b)), 0.0)
        acc = acc + _dot_tn(sel, _bf(o_all))
    o_ref[...] = acc


MEM_LANE_TILES = MEM_HD // LANES
MEM_ROW_GROUP = MEM_HEADS * MEM_LANE_TILES


def _mem_rows(ref, b):
    n_mem = ref.shape[1] // MEM_ROW_GROUP
    return jnp.concatenate([ref[b, pl.ds(lt * MEM_HEADS + h, n_mem, stride=MEM_ROW_GROUP), :]
                            for h in range(MEM_HEADS) for lt in range(MEM_LANE_TILES)], axis=1)


def _mem_flat_view(a):
    depth, nb, n_mem, heads, hd = a.shape
    a = a.reshape(depth, nb, n_mem, heads, hd // LANES, LANES).transpose(0, 1, 2, 4, 3, 5)
    return a.reshape(depth, nb, n_mem * MEM_ROW_GROUP, LANES)


def _mem_sample(q, k, v, *, layer, bs, ls):
    t, d = q.shape
    k, v = _mem_flat_view(k), _mem_flat_view(v)
    r = bs * ls
    kv_spec = pl.BlockSpec((None, bs, k.shape[2], LANES), lambda i: (layer, i, 0, 0))
    io_spec = pl.BlockSpec((r, d), lambda i: (i, 0))
    return pl.pallas_call(
        functools.partial(_mem_sample_kernel, bs=bs, ls=ls),
        grid=(t // r,),
        in_specs=[io_spec, kv_spec, kv_spec],
        out_specs=io_spec,
        out_shape=jax.ShapeDtypeStruct(q.shape, F32),
        compiler_params=_params("parallel"),
        name="mem_sample",
    )(q, k, v)


def _retention_tables(pos):
    inv = 1.0 / (RET_THETA ** jnp.linspace(0.0, 1.0, RET_DK // 2, dtype=F32))
    ang = pos[:, None] * inv[None, :]
    cos = jnp.repeat(jnp.cos(ang), 2, axis=1)
    sin = jnp.stack([-jnp.sin(ang), jnp.sin(ang)], axis=-1).reshape(pos.shape[0], RET_DK)
    return jnp.tile(cos, (1, RET_HEADS)), jnp.tile(sin, (1, RET_HEADS))


def _rope_tables(pos):
    half = ROPE_HALF
    inv = 1.0 / (ROPE_THETA ** (jnp.arange(half, dtype=F32) * 2.0 / (2 * half)))
    ang = pos[:, None] * inv[None, :]
    n = pos.shape[0]
    rest = SWA_HD - 2 * half
    cos = jnp.concatenate([jnp.cos(ang), jnp.cos(ang), jnp.ones((n, rest), F32)], axis=1)
    sin = jnp.concatenate([-jnp.sin(ang), jnp.sin(ang), jnp.zeros((n, rest), F32)], axis=1)
    return jnp.tile(cos, (1, SWA_HEADS)), jnp.tile(sin, (1, SWA_HEADS))


def _block_diag(t):
    g, a, b = t.shape
    eye = jnp.eye(g, dtype=t.dtype)
    return (t[:, :, None, :] * eye[:, None, :, None]).reshape(g * a, g * b)


def _half_block_diag(t):
    per = t.shape[0] // S5_HALVES
    return _bf(jnp.stack([_block_diag(t[h * per:(h + 1) * per]) for h in range(S5_HALVES)]))


def _sink_rows(sinks):
    return jnp.broadcast_to(sinks.astype(F32)[:, None], (sinks.shape[0], LANES))


def _trunk(x3, pos0, states, mem_k, mem_v, w, *, sample):
    b, l, d = x3.shape
    t = b * l
    x = x3.reshape(t, d)
    pos = pos0 + jnp.arange(l, dtype=F32)
    tm = 512 if sample else 1024
    gla_s, ret_s, s5_re, s5_im, swa_k, swa_v = states
    out_states = {k: [] for k in ("gla", "ret", "s5_re", "s5_im", "swa_k", "swa_v")}
    ld_row = jnp.repeat(jnp.log(1.0 - 2.0 ** (-5.0 - jnp.arange(RET_HEADS, dtype=F32))), RET_DK)[None, :]

    for layer in range(2):
        i = layer // 2
        x = _ffn(x, w['ffn1_norm'][layer][None], w['ffn1_w_gate'][layer], w['ffn1_w_up'][layer],
                 w['ffn1_w_down'][layer], tm=tm, th=256)
        if layer % 2 == 0:
            cos, sin = _retention_tables(pos)
            args = (ld_row, w['gla_w_gate'][i], w['gla_b_gate'][i], w['gla_out_norm'][i])
            if sample:
                bg = 16
                proj = _norm_matmul(x, w['mix_norm'][layer][None], w['even_w_in'][i], tm=tm, tn=640)
                mixed, g_s, r_s = _even_sample(proj, jnp.tile(cos, (bg, 1)), jnp.tile(sin, (bg, 1)), *args,
                                               gla_s[i], ret_s[i], bg=bg, ls=l)
                x = _matmul_residual(x, [(mixed, w['even_w_out'][i])], tm=tm)
            else:
                x, g_s, r_s = _even_prompt(x.reshape(b, l, d), w['mix_norm'][layer][None], w['even_w_in'][i],
                                           w['even_w_out'][i], cos, sin, *args, gla_s[i], ret_s[i], lt=512)
                x = x.reshape(t, d)
            out_states["gla"].append(g_s)
            out_states["ret"].append(r_s)
        else:
            cq, sq = _rope_tables(pos)
            qg = jnp.tile(w['swa_q_norm'][i], SWA_HEADS)[None, :]
            kg = jnp.tile(w['swa_k_norm'][i], SWA_KV_HEADS)[None, :]
            s5_args = (w['s5_ab_re'][i], w['s5_ab_im'][i], w['s5_bb_re'][i], w['s5_bb_im'][i],
                       w['s5_cc_re'][i], w['s5_cc_im'][i], w['s5_d'][i][None], w['s5_w_glu'][i], w['s5_b_glu'][i][None])
            n_state = S5_GROUPS * S5_STATE
            s0_re, s0_im = s5_re[i].reshape(b, n_state), s5_im[i].reshape(b, n_state)
            sinks = _sink_rows(w['swa_sinks'][i])
            if sample:
                proj = _norm_matmul(x, w['mix_norm'][layer][None], w['odd_w_in'][i], tm=tm, tn=640)
                c_out, sr, si = _s5(proj.reshape(1, t, OD_COLS), s0_re, s0_im, *s5_args, n_seq=b, lt=l,
                                    groups_per_pass=1, unroll=True)
                bg = 8
                d_out, kb, vb = _swa_sample(proj, swa_k[i].reshape(b, SWA_WINDOW, LANES),
                                            swa_v[i].reshape(b, SWA_WINDOW, LANES),
                                            jnp.tile(cq, (bg, 1)), jnp.tile(sq, (bg, 1)), qg, kg, sinks, bg=bg, ls=l)
                w_out = w['odd_w_out'][i]
                x = _matmul_residual(x, [(c_out.reshape(t, S5_WIDTH), w_out[:S5_WIDTH]), (d_out, w_out[S5_WIDTH:])],
                                     tm=tm)
            else:
                x, sr, si, kb, vb = _odd_prompt(x.reshape(b, l, d), w['mix_norm'][layer][None], w['odd_w_in'][i],
                                                w['odd_w_out'][i], s0_re, s0_im, *s5_args, cq, sq, qg, kg, sinks,
                                                groups_per_pass=8, unroll=4)
                x = x.reshape(t, d)
            out_states["s5_re"].append(sr.reshape(b, S5_GROUPS, S5_STATE))
            out_states["s5_im"].append(si.reshape(b, S5_GROUPS, S5_STATE))
            out_states["swa_k"].append(kb.reshape(b, -1, SWA_KV_HEADS, SWA_HD))
            out_states["swa_v"].append(vb.reshape(b, -1, SWA_KV_HEADS, SWA_HD))
        if sample:
            q = _norm_matmul(x, w['mem_x_norm'][layer][None], w['mem_w_q'][layer], tm=tm, tn=MEM_HD,
                             head_gain=w['mem_q_norm'][layer][None], n_norm_tiles=MEM_HEADS)
            o = _mem_sample(q, mem_k, mem_v, layer=layer, bs=4, ls=l)
            x = _matmul_residual(x, [(o, w['mem_w_o'][layer])], tm=tm)
        else:
            x = _mem_prompt(x.reshape(b, l, d), w['mem_x_norm'][layer][None], w['mem_w_q'][layer],
                            w['mem_q_norm'][layer][None], mem_k[layer], mem_v[layer], w['mem_w_o'][layer],
                            lt=1024).reshape(t, d)
        x = _ffn(x, w['ffn2_norm'][layer][None], w['ffn2_w_gate'][layer], w['ffn2_w_up'][layer],
                 w['ffn2_w_down'][layer], tm=tm, th=256)
    return x.reshape(b, l, d), {k: jnp.stack(v) for k, v in out_states.items()}


def kernel(x_prompt, x_sample, mem_prompt, state_gla, state_ret, state_s5_re, state_s5_im, cache_swa_k, cache_swa_v, cache_mem_k, cache_mem_v, ffn1_norm, ffn1_w_gate, ffn1_w_up, ffn1_w_down, ffn2_norm, ffn2_w_gate, ffn2_w_up, ffn2_w_down, mix_norm, even_w_in, gla_w_gate, gla_b_gate, gla_out_norm, even_w_out, odd_w_in, s5_a_re, s5_a_im, s5_log_step, s5_b_re, s5_b_im, s5_c_re, s5_c_im, s5_d, s5_w_glu, s5_b_glu, swa_q_norm, swa_k_norm, swa_sinks, odd_w_out, mem_x_norm, mem_m_norm, mem_w_q, mem_w_k, mem_w_v, mem_w_o, mem_q_norm, mem_k_norm):
    depth = ffn1_norm.shape[0]
    n_even, n_odd = even_w_in.shape[0], odd_w_in.shape[0]
    batch, seq, d = x_prompt.shape
    dec_batch = x_sample.shape[0]
    n_mem = mem_prompt.shape[1]

    ev = even_w_in
    ev_cols = jnp.concatenate(
        [ev[..., 0:1536], ev[..., 1552:3088], ev[..., 1536:1552],
         jnp.zeros(ev.shape[:2] + (EV_COLS - 3088,), ev.dtype)], axis=-1)
    wgate_pad = jnp.concatenate(
        [gla_w_gate, jnp.zeros((n_even, EV_COLS - EV_GA - GLA_RANK, gla_w_gate.shape[-1]), gla_w_gate.dtype)], axis=1)
    per_layer = lambda a: [_bf(a[layer]) for layer in range(a.shape[0])]
    w = dict(
        ffn1_norm=ffn1_norm, ffn2_norm=ffn2_norm, mix_norm=mix_norm, mem_x_norm=mem_x_norm,
        ffn1_w_gate=per_layer(ffn1_w_gate), ffn1_w_up=per_layer(ffn1_w_up), ffn1_w_down=per_layer(ffn1_w_down),
        ffn2_w_gate=per_layer(ffn2_w_gate), ffn2_w_up=per_layer(ffn2_w_up), ffn2_w_down=per_layer(ffn2_w_down),
        even_w_in=_bf(ev_cols), gla_w_gate=_bf(wgate_pad), gla_b_gate=gla_b_gate[:, None, :],
        gla_out_norm=gla_out_norm[:, None, :], even_w_out=_bf(even_w_out),
        odd_w_in=_bf(odd_w_in), odd_w_out=_bf(odd_w_out), s5_d=s5_d, s5_w_glu=_bf(s5_w_glu), s5_b_glu=s5_b_glu,
        swa_q_norm=swa_q_norm, swa_k_norm=swa_k_norm, swa_sinks=swa_sinks,
        mem_w_q=per_layer(mem_w_q), mem_w_o=per_layer(mem_w_o), mem_q_norm=mem_q_norm,
    )
    ab_re, ab_im, bb_re, bb_im = [], [], [], []
    for i in range(n_odd):
        a_r, a_i, b_r, b_i = _s5_prep(s5_a_re[i], s5_a_im[i], s5_log_step[i], s5_b_re[i], s5_b_im[i])
        ab_re.append(a_r.reshape(-1))
        ab_im.append(a_i.reshape(-1))
        bb_re.append(_half_block_diag(b_r))
        bb_im.append(_half_block_diag(b_i))
    w.update(s5_ab_re=ab_re, s5_ab_im=ab_im, s5_bb_re=bb_re, s5_bb_im=bb_im,
             s5_cc_re=[_half_block_diag(jnp.swapaxes(s5_c_re[i], 1, 2)) for i in range(n_odd)],
             s5_cc_im=[_half_block_diag(jnp.swapaxes(s5_c_im[i], 1, 2)) for i in range(n_odd)])

    mem2 = mem_prompt.reshape(batch * n_mem, d)
    p_mem_k, p_mem_v = [], []
    for layer in range(depth):
        w_kv = _bf(jnp.concatenate([mem_w_k[layer], mem_w_v[layer]], axis=1))
        kv = _norm_matmul(mem2, mem_m_norm[layer][None], w_kv, tm=batch * n_mem, tn=MEM_HD,
                          head_gain=mem_k_norm[layer][None], n_norm_tiles=MEM_HEADS)
        p_mem_k.append(kv[:, :d].reshape(batch, n_mem, d))
        p_mem_v.append(kv[:, d:].reshape(batch, n_mem, d))

    zeros = lambda *s: jnp.zeros(s, F32)
    p_states = (zeros(n_even, batch, GLA_HEADS, GLA_DK, GLA_DV), zeros(n_even, batch, RET_HEADS, RET_DK, GLA_DV),
                zeros(n_odd, batch, S5_GROUPS, S5_STATE), zeros(n_odd, batch, S5_GROUPS, S5_STATE), None, None)
    y_prompt, ps = _trunk(x_prompt, 0.0, p_states, p_mem_k, p_mem_v, w, sample=False)

    s_states = (state_gla, state_ret, state_s5_re, state_s5_im, cache_swa_k, cache_swa_v)
    y_sample, ss = _trunk(x_sample, float(PAST_LEN), s_states, cache_mem_k, cache_mem_v, w, sample=True)

    p_mem_k = jnp.stack(p_mem_k).reshape(depth, batch, n_mem, MEM_HEADS, MEM_HD)
    p_mem_v = jnp.stack(p_mem_v).reshape(depth, batch, n_mem, MEM_HEADS, MEM_HD)
    return (y_prompt, y_sample, ps["gla"], ps["ret"], ps["s5_re"], ps["s5_im"], ps["swa_k"], ps["swa_v"],
            p_mem_k, p_mem_v, ss["gla"], ss["ret"], ss["s5_re"], ss["s5_im"], ss["swa_k"], ss["swa_v"])
```

```python
import functools
import math

import jax
import jax.numpy as jnp
import numpy as np
from jax import lax
from jax.experimental import pallas as pl
from jax.experimental.pallas import tpu as pltpu

F32 = jnp.float32
BF16 = jnp.bfloat16
NORM_EPS = 1e-6
HIGHEST = lax.Precision.HIGHEST

D_MODEL = 1024
GLA_HEADS = 4
GLA_DK = 64
GLA_DV = 128
GLA_RANK = 16
GLA_TAU = 16.0
RET_HEADS = 4
RET_DK = 64
RET_THETA = 10000.0
LA_CHUNK = 64
S5_WIDTH = 512
S5_GROUP = 16
S5_GROUPS = 32
S5_STATE = 64
SWA_HD = 64
SWA_HEADS = 8
SWA_KV_HEADS = 2
SWA_WINDOW = 128
ROPE_THETA = 500000.0
MEM_HEADS = 4
MEM_HD = 256
PAST_LEN = 8192

VMEM_LIMIT_BYTES = 52 * 1024 * 1024
LANES = 128
SUBLANES = 8


def _params(*sem):
    return pltpu.CompilerParams(dimension_semantics=sem, vmem_limit_bytes=VMEM_LIMIT_BYTES)


def _rms(x, gain=None):
    y = x * lax.rsqrt(jnp.mean(x * x, axis=-1, keepdims=True) + NORM_EPS)
    return y if gain is None else y * gain


def _dot(a, b):
    return jnp.dot(a, b, preferred_element_type=F32)


def _dot_nt(a, b):
    return lax.dot_general(a, b, (((1,), (1,)), ((), ())), preferred_element_type=F32)


def _dot_tn(a, b):
    return lax.dot_general(a, b, (((0,), (0,)), ((), ())), preferred_element_type=F32)


def _dot_f32(a, b):
    return jnp.dot(a, b, precision=HIGHEST, preferred_element_type=F32)


def _dot_nt_f32(a, b):
    return lax.dot_general(a, b, (((1,), (1,)), ((), ())), precision=HIGHEST, preferred_element_type=F32)


def _bf(x):
    return x.astype(BF16)


def _log_sigmoid(x):
    return jnp.minimum(x, 0.0) - jnp.log1p(jnp.exp(-jnp.abs(x)))


def _iota(shape, dim):
    return lax.broadcasted_iota(jnp.int32, shape, dim)


def _swap_pairs(x):
    n = x.shape[-1]
    even = (_iota(x.shape, 1) & 1) == 0
    return jnp.where(even, pltpu.roll(x, n - 1, 1), pltpu.roll(x, 1, 1))


def _rope_partner(x, head_dim, half):
    n = x.shape[-1]
    first = (_iota(x.shape, 1) & (head_dim - 1)) < half
    return jnp.where(first, pltpu.roll(x, n - half, 1), pltpu.roll(x, half, 1))


def _ffn_kernel(x_ref, g_ref, wg_ref, wu_ref, wd_ref, o_ref, xn_ref):
    @pl.when(pl.program_id(1) == 0)
    def _():
        x = x_ref[...]
        xn_ref[...] = _bf(_rms(x, g_ref[...]))
        o_ref[...] = x

    xn = xn_ref[...]
    gate = _dot(xn, wg_ref[...])
    up = _dot(xn, wu_ref[...])
    o_ref[...] += _dot(_bf(jax.nn.silu(gate) * up * 0.5), wd_ref[...])


def _ffn(x, gain, wg, wu, wd, *, layer, tm, th):
    t, d = x.shape
    h = wg.shape[2]
    return pl.pallas_call(
        _ffn_kernel,
        grid=(t // tm, h // th),
        in_specs=[
            pl.BlockSpec((tm, d), lambda i, j: (i, 0)),
            pl.BlockSpec((1, d), lambda i, j: (0, 0)),
            pl.BlockSpec((None, d, th), lambda i, j: (layer, 0, j)),
            pl.BlockSpec((None, d, th), lambda i, j: (layer, 0, j)),
            pl.BlockSpec((None, th, d), lambda i, j: (layer, j, 0)),
        ],
        out_specs=pl.BlockSpec((tm, d), lambda i, j: (i, 0)),
        out_shape=jax.ShapeDtypeStruct((t, d), F32),
        scratch_shapes=[pltpu.VMEM((tm, d), BF16)],
        compiler_params=_params("parallel", "arbitrary"),
        name="ffn",
    )(x, gain, wg, wu, wd)


def _nmm_kernel(x_ref, g_ref, w_ref, hg_ref, o_ref, xn_ref, *, n_norm_tiles):
    j = pl.program_id(1)

    @pl.when(j == 0)
    def _():
        xn_ref[...] = _bf(_rms(x_ref[...], g_ref[...]))

    y = _dot(xn_ref[...], w_ref[...])
    if n_norm_tiles == 0:
        o_ref[...] = y
    else:
        @pl.when(j < n_norm_tiles)
        def _():
            o_ref[...] = _rms(y, hg_ref[...])

        @pl.when(j >= n_norm_tiles)
        def _():
            o_ref[...] = y


def _norm_matmul(x, gain, w, *, tm, tn, head_gain=None, n_norm_tiles=0):
    t, d = x.shape
    n = w.shape[1]
    if head_gain is None:
        head_gain = jnp.ones((1, tn), F32)
    return pl.pallas_call(
        functools.partial(_nmm_kernel, n_norm_tiles=n_norm_tiles),
        grid=(t // tm, n // tn),
        in_specs=[
            pl.BlockSpec((tm, d), lambda i, j: (i, 0)),
            pl.BlockSpec((1, d), lambda i, j: (0, 0)),
            pl.BlockSpec((d, tn), lambda i, j: (0, j)),
            pl.BlockSpec((1, tn), lambda i, j: (0, 0)),
        ],
        out_specs=pl.BlockSpec((tm, tn), lambda i, j: (i, j)),
        out_shape=jax.ShapeDtypeStruct((t, n), F32),
        scratch_shapes=[pltpu.VMEM((tm, d), BF16)],
        compiler_params=_params("parallel", "arbitrary"),
        name="norm_matmul",
    )(x, gain, w, head_gain)


def _mmr_kernel(*refs, n_terms):
    x_ref = refs[0]
    a_refs = refs[1:1 + n_terms]
    w_refs = refs[1 + n_terms:1 + 2 * n_terms]
    o_ref = refs[1 + 2 * n_terms]
    acc = x_ref[...]
    for a_ref, w_ref in zip(a_refs, w_refs):
        acc = acc + _dot(_bf(a_ref[...]), w_ref[...])
    o_ref[...] = acc


def _matmul_residual(x, terms, *, tm):
    t, d = x.shape
    acts = [a for a, _ in terms]
    ws = [w for _, w in terms]
    in_specs = [pl.BlockSpec((tm, d), lambda i: (i, 0))]
    in_specs += [pl.BlockSpec((tm, a.shape[1]), lambda i: (i, 0)) for a in acts]
    in_specs += [pl.BlockSpec(w.shape, lambda i: (0, 0)) for w in ws]
    return pl.pallas_call(
        functools.partial(_mmr_kernel, n_terms=len(terms)),
        grid=(t // tm,),
        in_specs=in_specs,
        out_specs=pl.BlockSpec((tm, d), lambda i: (i, 0)),
        out_shape=jax.ShapeDtypeStruct((t, d), F32),
        compiler_params=_params("parallel"),
        name="matmul_residual",
    )(x, *acts, *ws)


EV_GQ, EV_GK, EV_GV, EV_GG = 0, 256, 512, 1024
EV_RQ, EV_RK, EV_RV, EV_RG = 1536, 1792, 2048, 2560
EV_GA = 3072
EV_COLS = 3200
EV_BLOCK = 256
PAIR = 2


def _gate_and_norm(o, gate, gain=None):
    return _rms(o, gain) * jax.nn.silu(gate)


def _even_prompt_kernel(x_ref, gain_ref, win_ref, wout_ref, cos_ref, sin_ref, ld_ref, wgate_ref, bgate_ref, gnorm_ref,
                        s0g_ref, s0r_ref, o_ref, sg_ref, sr_ref, proj_sc, mix_sc, o_sc, *, chunk, n_chunks):
    @pl.when(pl.program_id(1) == 0)
    def _():
        sg_ref[...] = s0g_ref[...]
        sr_ref[...] = s0r_ref[...]

    c = chunk
    lt = c * n_chunks
    blk = min(EV_BLOCK, lt)
    c_shift = c.bit_length() - 1
    x = x_ref[0]
    proj_sc[...] = _dot(_bf(_rms(x, gain_ref[...])), win_ref[...])

    def cols(a, b):
        return proj_sc[:, a:b]

    log_a = _log_sigmoid(_dot(_bf(cols(EV_GA, EV_COLS)), wgate_ref[...]) + bgate_ref[...]) * (1.0 / GLA_TAU)
    tril = (_iota((c, c), 1) <= _iota((c, c), 0)).astype(F32)
    cum_parts = [_dot_f32(tril, log_a[i * c:(i + 1) * c]) for i in range(n_chunks)]
    tots = [p[c - 1:c] for p in cum_parts]
    cum = jnp.concatenate(cum_parts, axis=0)
    tot_b = jnp.concatenate([jnp.broadcast_to(t, (c, 256)) for t in tots], axis=0)
    k = cols(EV_GK, EV_GK + 256)
    gla = (cols(EV_GQ, EV_GQ + 256) * (GLA_DK ** -0.5) * jnp.exp(cum), k * jnp.exp(-cum), k * jnp.exp(tot_b - cum))

    ld = ld_ref[...]
    tpos = ((_iota((lt, 1), 0) & (c - 1)) + 1).astype(F32)
    cum_r = tpos * ld
    tot_r = float(c) * ld
    cos, sin = cos_ref[...], sin_ref[...]
    rq = cols(EV_RQ, EV_RQ + 256)
    rk = cols(EV_RK, EV_RK + 256)
    q_rot = rq * cos + _swap_pairs(rq) * sin
    k_rot = (rk * cos + _swap_pairs(rk) * sin) * (RET_DK ** -0.5)
    ret = (q_rot * jnp.exp(cum_r), k_rot * jnp.exp(-cum_r), k_rot * jnp.exp(tot_r - cum_r))

    tot_rows = jnp.concatenate(tots + [tot_r, jnp.zeros((LANES - n_chunks - 1, 256), F32)], axis=0)
    decay_cols = jnp.exp(jnp.transpose(tot_rows))

    row = _iota((blk, blk), 0)
    col = _iota((blk, blk), 1)
    blk_mask = ((row >> c_shift) == (col >> c_shift)) & (col <= row)
    lo = (_iota((lt, LANES), 1) < GLA_DK)
    gnorm = gnorm_ref[...]

    mixers = ((gla, EV_GV, EV_GG, sg_ref, 0, gnorm, lambda i: i),
              (ret, EV_RV, EV_RG, sr_ref, 512, None, lambda i: n_chunks))
    for m, ((q_dec, k_inv, k_dec), v_col, g_col, s_ref, out_col, gain, decay_col_of) in enumerate(mixers):
        for p in range(GLA_HEADS // PAIR):
            lanes = slice(p * LANES, (p + 1) * LANES)
            q_pair = q_dec[:, lanes]
            q_masked = [_bf(jnp.where(lo, q_pair, 0.0)), _bf(jnp.where(lo, 0.0, q_pair))]
            ki = _bf(k_inv[:, lanes])
            kd = _bf(k_dec[:, lanes])
            v_pair = _bf(cols(v_col + p * PAIR * GLA_DV, v_col + (p + 1) * PAIR * GLA_DV))
            for e in range(PAIR):
                slot = (m * (GLA_HEADS // PAIR) + p) * PAIR + e
                for r0 in range(0, lt, blk):
                    rs = slice(r0, r0 + blk)
                    scores = jnp.where(blk_mask, _dot_nt(q_masked[e][rs], ki[rs]), 0.0)
                    o_sc[slot, rs, :] = _dot(_bf(scores), v_pair[rs, e * GLA_DV:(e + 1) * GLA_DV])
            state = s_ref[0, p * PAIR:(p + 1) * PAIR].reshape(PAIR * GLA_DK, GLA_DV)
            for i in range(n_chunks):
                rs = slice(i * c, (i + 1) * c)
                q_stack = jnp.concatenate([q_masked[0][rs], q_masked[1][rs]], axis=0)
                o_inter = _dot(q_stack, _bf(state))
                kv = _dot_tn(kd[rs], v_pair[rs])
                kv = jnp.concatenate([kv[:GLA_DK, :GLA_DV], kv[GLA_DK:, GLA_DV:]], axis=0)
                ci = decay_col_of(i)
                state = state * decay_cols[p * LANES:(p + 1) * LANES, ci:ci + 1] + kv
                for e in range(PAIR):
                    slot = (m * (GLA_HEADS // PAIR) + p) * PAIR + e
                    o_sc[slot, rs, :] += o_inter[e * c:(e + 1) * c]
            s_ref[0, p * PAIR:(p + 1) * PAIR] = state.reshape(PAIR, GLA_DK, GLA_DV)
            for e in range(PAIR):
                h = p * PAIR + e
                slot = (m * (GLA_HEADS // PAIR) + p) * PAIR + e
                gate = cols(g_col + h * GLA_DV, g_col + (h + 1) * GLA_DV)
                mix_sc[:, out_col + h * GLA_DV:out_col + (h + 1) * GLA_DV] = _gate_and_norm(o_sc[slot], gate, gain)
    o_ref[0] = x + _dot(_bf(mix_sc[...]), wout_ref[...])


def _even_prompt(x, gain, w_in, w_out, cos, sin, ld_row, wgate, bgate, gnorm, s0g, s0r, *, lt):
    b, l, d = x.shape
    chunk = math.gcd(l, LA_CHUNK)
    st_spec = pl.BlockSpec((1, GLA_HEADS, GLA_DK, GLA_DV), lambda i, j: (i, 0, 0, 0))
    io_spec = pl.BlockSpec((1, lt, d), lambda i, j: (i, j, 0))
    full = lambda a: pl.BlockSpec(a.shape, lambda i, j: (0,) * a.ndim)
    return pl.pallas_call(
        functools.partial(_even_prompt_kernel, chunk=chunk, n_chunks=lt // chunk),
        grid=(b, l // lt),
        in_specs=[
            io_spec, full(gain), full(w_in), full(w_out),
            pl.BlockSpec((lt, 256), lambda i, j: (j, 0)),
            pl.BlockSpec((lt, 256), lambda i, j: (j, 0)),
            full(ld_row), full(wgate), full(bgate), full(gnorm), st_spec, st_spec,
        ],
        out_specs=[io_spec, st_spec, st_spec],
        out_shape=[jax.ShapeDtypeStruct(x.shape, F32),
                   jax.ShapeDtypeStruct(s0g.shape, F32), jax.ShapeDtypeStruct(s0r.shape, F32)],
        scratch_shapes=[pltpu.VMEM((lt, EV_COLS), F32), pltpu.VMEM((lt, d), F32),
                        pltpu.VMEM((GLA_HEADS + RET_HEADS, lt, GLA_DV), F32)],
        compiler_params=_params("parallel", "arbitrary"),
        name="even_prompt",
    )(x, gain, w_in, w_out, cos, sin, ld_row, wgate, bgate, gnorm, s0g, s0r)


def _even_sample_kernel(proj_ref, cos_ref, sin_ref, ld_ref, wgate_ref, bgate_ref, gnorm_ref, s0g_ref, s0r_ref,
                        mix_ref, sg_ref, sr_ref, *, bg, ls):
    r = bg * ls
    ls_shift = ls.bit_length() - 1
    dk_shift = GLA_DK.bit_length() - 1
    n_exp = bg * GLA_DK
    row_seq = _iota((r, r), 0) >> ls_shift
    col_seq = _iota((r, r), 1) >> ls_shift
    same = row_seq == col_seq
    seg = same & (_iota((r, r), 1) <= _iota((r, r), 0))
    seg_f = seg.astype(F32)
    same_f = same.astype(F32)
    tile_b = _bf(((_iota((GLA_DK, n_exp), 1) & (GLA_DK - 1)) == _iota((GLA_DK, n_exp), 0)).astype(F32))
    tile_t_f = ((_iota((n_exp, GLA_DK), 0) & (GLA_DK - 1)) == _iota((n_exp, GLA_DK), 1)).astype(F32)
    tile_t_b = _bf(tile_t_f)
    q_mask = (_iota((r, n_exp), 0) >> ls_shift) == (_iota((r, n_exp), 1) >> dk_shift)
    k_mask = (_iota((n_exp, r), 0) >> dk_shift) == (_iota((n_exp, r), 1) >> ls_shift)
    tpos = ((_iota((r, 1), 0) & (ls - 1)) + 1).astype(F32)
    ld = ld_ref[...]
    cum_r = tpos * ld
    tot_r = float(ls) * ld
    r_dec = jnp.exp(tot_r)
    gnorm = gnorm_ref[...]

    def mixer(q_dec, k_inv, k_dec, v_col, g_col, s0_ref, s_ref, out_col, decay_of, gain):
        for h in range(GLA_HEADS):
            sl = slice(h * GLA_DK, (h + 1) * GLA_DK)
            v = _bf(proj_ref[:, v_col + h * GLA_DV:v_col + (h + 1) * GLA_DV])
            qd = _bf(q_dec[:, sl])
            scores = jnp.where(seg, _dot_nt(qd, _bf(k_inv[:, sl])), 0.0)
            state = s0_ref[:, h].reshape(n_exp, GLA_DV)
            q_exp = _bf(jnp.where(q_mask, _dot(qd, tile_b), 0.0))
            o = _dot(_bf(scores), v) + _dot(q_exp, _bf(state))
            k_exp = _bf(jnp.where(k_mask, _dot_nt(tile_t_b, _bf(k_dec[:, sl])), 0.0))
            new_state = state * decay_of(h, sl) + _dot(k_exp, v)
            s_ref[:, h] = new_state.reshape(bg, GLA_DK, GLA_DV)
            gate = proj_ref[:, g_col + h * GLA_DV:g_col + (h + 1) * GLA_DV]
            mix_ref[:, out_col + h * GLA_DV:out_col + (h + 1) * GLA_DV] = _gate_and_norm(o, gate, gain)

    log_a = _log_sigmoid(_dot(_bf(proj_ref[:, EV_GA:EV_COLS]), wgate_ref[...]) + bgate_ref[...]) * (1.0 / GLA_TAU)
    cum = _dot_f32(seg_f, log_a)
    tot = _dot_f32(same_f, log_a)
    k = proj_ref[:, EV_GK:EV_GK + 256]

    def gla_decay(h, sl):
        la_exp = jnp.where(k_mask, _dot_nt_f32(tile_t_f, log_a[:, sl]), 0.0)
        return jnp.exp(jnp.sum(la_exp, axis=-1, keepdims=True))

    mixer(proj_ref[:, EV_GQ:EV_GQ + 256] * (GLA_DK ** -0.5) * jnp.exp(cum), k * jnp.exp(-cum), k * jnp.exp(tot - cum),
          EV_GV, EV_GG, s0g_ref, sg_ref, 0, gla_decay, gnorm)

    cos = cos_ref[...]
    sin = sin_ref[...]
    rq = proj_ref[:, EV_RQ:EV_RQ + 256]
    rk = proj_ref[:, EV_RK:EV_RK + 256]
    q_rot = rq * cos + _swap_pairs(rq) * sin
    k_rot = (rk * cos + _swap_pairs(rk) * sin) * (RET_DK ** -0.5)

    def ret_decay(h, sl):
        return r_dec[:, h * RET_DK:h * RET_DK + 1]

    mixer(q_rot * jnp.exp(cum_r), k_rot * jnp.exp(-cum_r), k_rot * jnp.exp(tot_r - cum_r),
          EV_RV, EV_RG, s0r_ref, sr_ref, 512, ret_decay, None)


def _even_sample(proj, cos, sin, ld_row, wgate, bgate, gnorm, s0g, s0r, *, bg, ls):
    t = proj.shape[0]
    n_b = t // ls
    r = bg * ls
    st_spec = pl.BlockSpec((bg, GLA_HEADS, GLA_DK, GLA_DV), lambda i: (i, 0, 0, 0))
    full = lambda a: pl.BlockSpec(a.shape, lambda i: (0,) * a.ndim)
    return pl.pallas_call(
        functools.partial(_even_sample_kernel, bg=bg, ls=ls),
        grid=(n_b // bg,),
        in_specs=[
            pl.BlockSpec((r, EV_COLS), lambda i: (i, 0)),
            full(cos), full(sin), full(ld_row), full(wgate), full(bgate), full(gnorm), st_spec, st_spec,
        ],
        out_specs=[pl.BlockSpec((r, D_MODEL), lambda i: (i, 0)), st_spec, st_spec],
        out_shape=[jax.ShapeDtypeStruct((t, D_MODEL), F32),
                   jax.ShapeDtypeStruct(s0g.shape, F32), jax.ShapeDtypeStruct(s0r.shape, F32)],
        compiler_params=_params("parallel"),
        name="even_sample",
    )(proj, cos, sin, ld_row, wgate, bgate, gnorm, s0g, s0r)


def _s5_prep_kernel(are_ref, aim_ref, lstep_ref, bre_ref, bim_ref, abre_ref, abim_ref, bbre_ref, bbim_ref):
    a_re, a_im = are_ref[...], aim_ref[...]
    step = jnp.exp(lstep_ref[...])
    mag = jnp.exp(a_re * step)
    ab_re = mag * jnp.cos(a_im * step)
    ab_im = mag * jnp.sin(a_im * step)
    den = a_re * a_re + a_im * a_im
    coef_re = ((ab_re - 1.0) * a_re + ab_im * a_im) / den
    coef_im = (ab_im * a_re - (ab_re - 1.0) * a_im) / den
    b_re, b_im = bre_ref[...], bim_ref[...]
    abre_ref[...] = ab_re
    abim_ref[...] = ab_im
    bbre_ref[...] = coef_re * b_re - coef_im * b_im
    bbim_ref[...] = coef_re * b_im + coef_im * b_re


def _s5_prep(a_re, a_im, log_step, b_re, b_im):
    g, n = a_re.shape
    shp3 = jax.ShapeDtypeStruct((g, 1, n), F32)
    shpb = jax.ShapeDtypeStruct((g, S5_GROUP, n), F32)
    return pl.pallas_call(_s5_prep_kernel, out_shape=[shp3, shp3, shpb, shpb], name="s5_prep")(
        a_re.reshape(g, 1, n), a_im.reshape(g, 1, n), log_step.reshape(g, 1, 1),
        jnp.swapaxes(b_re, 1, 2), jnp.swapaxes(b_im, 1, 2))


S5_LANE_CHUNKS = S5_GROUPS * S5_STATE // LANES
S5_HALVES = 2


def _s5_layout(n_seq, lt):
    pack = max(1, SUBLANES // n_seq)
    pitch = lt + 4 if lt % SUBLANES == 0 else lt
    return pack, S5_LANE_CHUNKS // pack, pitch


def _s5_slot(c, n_groups):
    return c % n_groups, c // n_groups


def _s5_load_state(s0re_ref, s0im_ref, hre_ref, him_ref, n_seq, n_groups):
    for c in range(S5_LANE_CHUNKS):
        g, j = _s5_slot(c, n_groups)
        hre_ref[g, j * n_seq:(j + 1) * n_seq, :] = s0re_ref[:, c * LANES:(c + 1) * LANES]
        him_ref[g, j * n_seq:(j + 1) * n_seq, :] = s0im_ref[:, c * LANES:(c + 1) * LANES]


def _s5_store_state(sre_ref, sim_ref, hre_ref, him_ref, n_seq, n_groups):
    for c in range(S5_LANE_CHUNKS):
        g, j = _s5_slot(c, n_groups)
        sre_ref[:, c * LANES:(c + 1) * LANES] = hre_ref[g, j * n_seq:(j + 1) * n_seq, :]
        sim_ref[:, c * LANES:(c + 1) * LANES] = him_ref[g, j * n_seq:(j + 1) * n_seq, :]


def _s5_core(u, abre_ref, abim_ref, bbre_ref, bbim_ref, ccre_ref, ccim_ref, d_ref, wglu_ref, bglu_ref,
             xr_ref, xi_ref, hre_ref, him_ref, *, n_seq, lt, groups_per_pass, unroll):
    pack, n_groups, pitch = _s5_layout(n_seq, lt)
    rows = n_seq * lt
    per_half = S5_LANE_CHUNKS // S5_HALVES
    lanes_of = lambda c: slice(c * LANES, (c + 1) * LANES)
    slot_of = lambda c: _s5_slot(c, n_groups)

    def seq_rows(j, s):
        return slice((j * n_seq + s) * pitch, (j * n_seq + s) * pitch + lt)

    ub = _bf(u)
    u_cols = S5_WIDTH // S5_HALVES
    for half in range(S5_HALVES):
        uh = ub[:, half * u_cols:(half + 1) * u_cols]
        for x_ref, bb_ref in ((xr_ref, bbre_ref), (xi_ref, bbim_ref)):
            x = _dot(uh, bb_ref[half])
            for k in range(per_half):
                g, j = slot_of(half * per_half + k)
                if pitch == lt:
                    x_ref[g, j * rows:(j + 1) * rows, :] = x[:, lanes_of(k)]
                else:
                    for s in range(n_seq):
                        x_ref[g, seq_rows(j, s), :] = x[s * lt:(s + 1) * lt, lanes_of(k)]

    for g0 in range(0, n_groups, groups_per_pass):
        gs = list(range(g0, g0 + groups_per_pass))
        init = tuple(hre_ref[g] for g in gs) + tuple(him_ref[g] for g in gs)

        def step(t, carry, gs=gs):
            rws = pl.ds(t, pack * n_seq, stride=pitch)
            new_re, new_im = [], []
            for k, g in enumerate(gs):
                a_re, a_im = abre_ref[g], abim_ref[g]
                h_re, h_im = carry[k], carry[len(gs) + k]
                n_re = a_re * h_re - a_im * h_im + xr_ref[g, rws, :]
                n_im = a_re * h_im + a_im * h_re + xi_ref[g, rws, :]
                xr_ref[g, rws, :] = n_re
                xi_ref[g, rws, :] = n_im
                new_re.append(n_re)
                new_im.append(n_im)
            return tuple(new_re + new_im)

        fin = lax.fori_loop(0, lt, step, init, unroll=unroll)
        for k, g in enumerate(gs):
            hre_ref[g] = fin[k]
            him_ref[g] = fin[len(gs) + k]

    def gather(x_ref, half):
        cols = []
        for k in range(per_half):
            g, j = slot_of(half * per_half + k)
            if pitch == lt:
                cols.append(x_ref[g, j * rows:(j + 1) * rows, :])
            else:
                cols.append(jnp.concatenate([x_ref[g, seq_rows(j, s), :] for s in range(n_seq)], axis=0))
        return _bf(jnp.concatenate(cols, axis=1))

    y = jnp.concatenate([_dot(gather(xr_ref, half), ccre_ref[half]) - _dot(gather(xi_ref, half), ccim_ref[half])
                         for half in range(S5_HALVES)], axis=1) + d_ref[...] * u
    z = jax.nn.gelu(y, approximate=True)
    return z * jax.nn.sigmoid(_dot(_bf(z), wglu_ref[...]) + bglu_ref[...])


def _s5_kernel(u_ref, s0re_ref, s0im_ref, abre_ref, abim_ref, bbre_ref, bbim_ref, ccre_ref, ccim_ref,
               d_ref, wglu_ref, bglu_ref, out_ref, sre_ref, sim_ref, xr_ref, xi_ref, hre_ref, him_ref,
               *, n_seq, lt, groups_per_pass, unroll):
    n_groups = _s5_layout(n_seq, lt)[1]

    @pl.when(pl.program_id(0) == 0)
    def _():
        _s5_load_state(s0re_ref, s0im_ref, hre_ref, him_ref, n_seq, n_groups)

    out = _s5_core(u_ref[...].reshape(n_seq * lt, S5_WIDTH), abre_ref, abim_ref, bbre_ref, bbim_ref, ccre_ref, ccim_ref,
                   d_ref, wglu_ref, bglu_ref, xr_ref, xi_ref, hre_ref, him_ref,
                   n_seq=n_seq, lt=lt, groups_per_pass=groups_per_pass, unroll=unroll)
    out_ref[...] = out.reshape(out_ref.shape)

    @pl.when(pl.program_id(0) == pl.num_programs(0) - 1)
    def _():
        _s5_store_state(sre_ref, sim_ref, hre_ref, him_ref, n_seq, n_groups)


def _s5_tables(ab, n_seq, lt):
    pack, n_groups, _ = _s5_layout(n_seq, lt)
    tab = jnp.swapaxes(ab.reshape(pack, n_groups, 1, LANES), 0, 1)
    return jnp.broadcast_to(tab, (n_groups, pack, n_seq, LANES)).reshape(n_groups, pack * n_seq, LANES)


def _s5(proj3, s0_re, s0_im, ab_re, ab_im, bb_re, bb_im, cc_re, cc_im, d_row, wglu, bglu, *, n_seq, lt,
        groups_per_pass, unroll):
    nb, rows_b, _ = proj3.shape
    blk_rows = n_seq * lt // nb
    pack, n_groups, pitch = _s5_layout(n_seq, lt)
    ab_re, ab_im = _s5_tables(ab_re, n_seq, lt), _s5_tables(ab_im, n_seq, lt)
    full = lambda a: pl.BlockSpec(a.shape, lambda j: (0,) * a.ndim)
    io_spec = pl.BlockSpec((nb, blk_rows, S5_WIDTH), lambda j: (0, j, 0))
    x_scratch = pltpu.VMEM((n_groups, pack * n_seq * pitch, LANES), F32)
    h_scratch = pltpu.VMEM((n_groups, pack * n_seq, LANES), F32)
    return pl.pallas_call(
        functools.partial(_s5_kernel, n_seq=n_seq, lt=lt, groups_per_pass=groups_per_pass, unroll=unroll),
        grid=(rows_b // blk_rows,),
        in_specs=[io_spec, full(s0_re), full(s0_im), full(ab_re), full(ab_im), full(bb_re), full(bb_im),
                  full(cc_re), full(cc_im), full(d_row), full(wglu), full(bglu)],
        out_specs=[io_spec, full(s0_re), full(s0_im)],
        out_shape=[jax.ShapeDtypeStruct((nb, rows_b, S5_WIDTH), F32),
                   jax.ShapeDtypeStruct(s0_re.shape, F32), jax.ShapeDtypeStruct(s0_im.shape, F32)],
        scratch_shapes=[x_scratch, x_scratch, h_scratch, h_scratch],
        compiler_params=_params("arbitrary"),
        name="s5",
    )(proj3, s0_re, s0_im, ab_re, ab_im, bb_re, bb_im, cc_re, cc_im, d_row, wglu, bglu)


OD_Q_BLOCK = 1
OD_KV_BLOCK = 4
OD_COLS = 1280
ROPE_HALF = SWA_HD // 8
SWA_GROUP = SWA_HEADS // SWA_KV_HEADS


def _half_lanes(shape):
    return (_iota(shape, 1) & (LANES - 1)) < SWA_HD


def _pair_rms_scale(x):
    same_head = (_iota((LANES, LANES), 0) >= SWA_HD) == (_iota((LANES, LANES), 1) >= SWA_HD)
    sums = _dot_f32(x * x, same_head.astype(F32))
    return lax.rsqrt(sums * (1.0 / SWA_HD) + NORM_EPS)


def _swa_qk(xq, xk, qg, kg, cq, sq):
    xq_g = xq * qg
    q_rot = xq_g * cq + _rope_partner(xq_g, SWA_HD, ROPE_HALF) * sq
    xk_g = xk * kg
    k_rot = xk_g * cq[:, :LANES] + _rope_partner(xk_g, SWA_HD, ROPE_HALF) * sq[:, :LANES]
    q_pairs = [q_rot[:, j * LANES:(j + 1) * LANES] * _pair_rms_scale(xq[:, j * LANES:(j + 1) * LANES])
               for j in range(SWA_HEADS // 2)]
    return q_pairs, k_rot * _pair_rms_scale(xk)


def _swa_query_stack(q_pairs, kh):
    lo = _half_lanes(q_pairs[0].shape)
    keep = lo if kh == 0 else jnp.logical_not(lo)
    parts = []
    for g in range(SWA_GROUP):
        hq = kh * SWA_GROUP + g
        pair = q_pairs[hq // 2]
        src = pair if hq % 2 == kh else pltpu.roll(pair, SWA_HD, 1)
        parts.append(jnp.where(keep, src, 0.0))
    return _bf(jnp.concatenate(parts, axis=0))


def _swa_merge_heads(o, kh, rows):
    lo = _half_lanes((rows, LANES))
    pairs = []
    for p in range(SWA_GROUP // 2):
        even, odd = o[2 * p * rows:(2 * p + 1) * rows], o[(2 * p + 1) * rows:(2 * p + 2) * rows]
        if kh == 0:
            pairs.append(jnp.where(lo, even, pltpu.roll(odd, SWA_HD, 1)))
        else:
            pairs.append(jnp.where(lo, pltpu.roll(even, SWA_HD, 1), odd))
    return pairs


def _swa_block(xq, xk, v_cur, k_prev, v_prev, mask, qg, kg, cq, sq, sink_ref):
    w = SWA_WINDOW
    ones_col = jnp.ones((2 * w, LANES), BF16)
    q_pairs, k_cur = _swa_qk(xq, xk, qg, kg, cq, sq)
    k_ext = _bf(jnp.concatenate([k_prev, k_cur], axis=0))
    v_ext = _bf(jnp.concatenate([v_prev, v_cur], axis=0))
    out_pairs = []
    for kh in range(SWA_KV_HEADS):
        s_all = _dot_nt(_swa_query_stack(q_pairs, kh), k_ext)
        weights, sink_terms = [], []
        for g in range(SWA_GROUP):
            s = jnp.where(mask, s_all[g * w:(g + 1) * w] * (SWA_HD ** -0.5), -jnp.inf)
            sink = sink_ref[kh * SWA_GROUP + g:kh * SWA_GROUP + g + 1, :]
            m = jnp.maximum(jnp.broadcast_to(jnp.max(s, axis=-1, keepdims=True), (w, LANES)), sink)
            weights.append(_bf(jnp.exp(s - jnp.concatenate([m, m], axis=1))))
            sink_terms.append(jnp.exp(sink - m))
        weights = jnp.concatenate(weights, axis=0)
        den = _dot(weights, ones_col) + jnp.concatenate(sink_terms, axis=0)
        out_pairs += _swa_merge_heads(_dot(weights, v_ext) / den, kh, w)
    return out_pairs, k_cur


def _odd_prompt_kernel(x_ref, gain_ref, win_ref, wout_ref, s0re_ref, s0im_ref, abre_ref, abim_ref, bbre_ref, bbim_ref,
                       ccre_ref, ccim_ref, d_ref, wglu_ref, bglu_ref, cq_ref, sq_ref, qg_ref, kg_ref, sink_ref,
                       o_ref, sre_ref, sim_ref, ck_ref, cv_ref,
                       proj_sc, mix_sc, xr_ref, xi_ref, hre_ref, him_ref, kprev_sc, vprev_sc,
                       *, n_seq, groups_per_pass, unroll):
    w = SWA_WINDOW
    rows = n_seq * w
    n_groups = _s5_layout(n_seq, w)[1]
    step = pl.program_id(0)

    @pl.when(step == 0)
    def _():
        _s5_load_state(s0re_ref, s0im_ref, hre_ref, him_ref, n_seq, n_groups)
        kprev_sc[...] = jnp.zeros_like(kprev_sc)
        vprev_sc[...] = jnp.zeros_like(vprev_sc)

    x = x_ref[...].reshape(rows, D_MODEL)
    proj_sc[...] = _dot(_bf(_rms(x, gain_ref[...])), win_ref[...])
    mix_sc[:, 0:S5_WIDTH] = _s5_core(proj_sc[:, 0:S5_WIDTH], abre_ref, abim_ref, bbre_ref, bbim_ref, ccre_ref, ccim_ref,
                                     d_ref, wglu_ref, bglu_ref, xr_ref, xi_ref, hre_ref, him_ref,
                                     n_seq=n_seq, lt=w, groups_per_pass=groups_per_pass, unroll=unroll)

    t_idx = _iota((w, 2 * w), 0)
    s_idx = _iota((w, 2 * w), 1)
    mask = (s_idx > t_idx) & (s_idx <= t_idx + w) & (s_idx >= jnp.where(step == 0, w, 0))
    q0, k0, v0 = OD_Q_BLOCK * 512, OD_KV_BLOCK * 256, OD_KV_BLOCK * 256 + LANES
    for s in range(n_seq):
        rs = slice(s * w, (s + 1) * w)
        v_cur = proj_sc[rs, v0:v0 + LANES]
        pairs, k_cur = _swa_block(proj_sc[rs, q0:q0 + 512], proj_sc[rs, k0:k0 + LANES], v_cur, kprev_sc[s], vprev_sc[s],
                                  mask, qg_ref[...], kg_ref[...], cq_ref[...], sq_ref[...], sink_ref)
        for i, pair in enumerate(pairs):
            mix_sc[rs, S5_WIDTH + i * LANES:S5_WIDTH + (i + 1) * LANES] = pair
        kprev_sc[s] = k_cur
        vprev_sc[s] = v_cur
    o_ref[...] = (x + _dot(_bf(mix_sc[...]), wout_ref[...])).reshape(o_ref.shape)

    @pl.when(step == pl.num_programs(0) - 1)
    def _():
        _s5_store_state(sre_ref, sim_ref, hre_ref, him_ref, n_seq, n_groups)
        ck_ref[...] = kprev_sc[...]
        cv_ref[...] = vprev_sc[...]


def _odd_prompt(x, gain, w_in, w_out, s0_re, s0_im, ab_re, ab_im, bb_re, bb_im, cc_re, cc_im, d_row, wglu, bglu,
                cq, sq, qg, kg, sink_rows, *, groups_per_pass, unroll):
    b, l, d = x.shape
    w = SWA_WINDOW
    pack, n_groups, pitch = _s5_layout(b, w)
    ab_re, ab_im = _s5_tables(ab_re, b, w), _s5_tables(ab_im, b, w)
    full = lambda a: pl.BlockSpec(a.shape, lambda j: (0,) * a.ndim)
    io_spec = pl.BlockSpec((b, w, d), lambda j: (0, j, 0))
    tab_spec = pl.BlockSpec((w, 512), lambda j: (j, 0))
    cache_shape = jax.ShapeDtypeStruct((b, w, LANES), F32)
    x_scratch = pltpu.VMEM((n_groups, pack * b * pitch, LANES), F32)
    h_scratch = pltpu.VMEM((n_groups, pack * b, LANES), F32)
    kv_scratch = pltpu.VMEM((b, w, LANES), F32)
    return pl.pallas_call(
        functools.partial(_odd_prompt_kernel, n_seq=b, groups_per_pass=groups_per_pass, unroll=unroll),
        grid=(l // w,),
        in_specs=[io_spec, full(gain), full(w_in), full(w_out), full(s0_re), full(s0_im), full(ab_re), full(ab_im),
                  full(bb_re), full(bb_im), full(cc_re), full(cc_im), full(d_row), full(wglu), full(bglu),
                  tab_spec, tab_spec, full(qg), full(kg), full(sink_rows)],
        out_specs=[io_spec, full(s0_re), full(s0_im), pl.BlockSpec((b, w, LANES), lambda j: (0, 0, 0)),
                   pl.BlockSpec((b, w, LANES), lambda j: (0, 0, 0))],
        out_shape=[jax.ShapeDtypeStruct(x.shape, F32), jax.ShapeDtypeStruct(s0_re.shape, F32),
                   jax.ShapeDtypeStruct(s0_im.shape, F32), cache_shape, cache_shape],
        scratch_shapes=[pltpu.VMEM((b * w, OD_COLS), F32), pltpu.VMEM((b * w, d), F32),
                        x_scratch, x_scratch, h_scratch, h_scratch, kv_scratch, kv_scratch],
        compiler_params=_params("arbitrary"),
        name="odd_prompt",
    )(x, gain, w_in, w_out, s0_re, s0_im, ab_re, ab_im, bb_re, bb_im, cc_re, cc_im, d_row, wglu, bglu,
      cq, sq, qg, kg, sink_rows)


def _swa_sample_kernel(q_ref, kv_ref, ck_ref, cv_ref, cq_ref, sq_ref, qg_ref, kg_ref, sink_ref,
                       o_ref, nk_ref, nv_ref, *, bg, ls):
    w = SWA_WINDOW
    r = bg * ls
    ls_shift = ls.bit_length() - 1
    w_shift = w.bit_length() - 1
    rows_g = SWA_GROUP * r
    v_new = kv_ref[:, LANES:2 * LANES]
    q_pairs, k_new = _swa_qk(q_ref[...], kv_ref[:, 0:LANES], qg_ref[...], kg_ref[...], cq_ref[...], sq_ref[...])
    k_cache = _bf(ck_ref[...].reshape(bg * w, LANES))
    v_cache = _bf(cv_ref[...].reshape(bg * w, LANES))

    row = _iota((rows_g, bg * w), 0) & (r - 1)
    col = _iota((rows_g, bg * w), 1)
    mask_c = ((row >> ls_shift) == (col >> w_shift)) & ((col & (w - 1)) > (row & (ls - 1)))
    row_n = _iota((rows_g, r), 0) & (r - 1)
    col_n = _iota((rows_g, r), 1)
    mask_n = ((row_n >> ls_shift) == (col_n >> ls_shift)) & ((col_n & (ls - 1)) <= (row_n & (ls - 1)))
    for kh in range(SWA_KV_HEADS):
        q_stack = _swa_query_stack(q_pairs, kh)
        s_c = jnp.where(mask_c, _dot_nt(q_stack, k_cache) * (SWA_HD ** -0.5), -jnp.inf)
        s_n = jnp.where(mask_n, _dot_nt(q_stack, _bf(k_new)) * (SWA_HD ** -0.5), -jnp.inf)
        sink = jnp.concatenate([jnp.broadcast_to(sink_ref[kh * SWA_GROUP + g:kh * SWA_GROUP + g + 1, :], (r, LANES))
                                for g in range(SWA_GROUP)], axis=0)
        row_max = jnp.maximum(jnp.max(s_c, axis=-1, keepdims=True), jnp.max(s_n, axis=-1, keepdims=True))
        m = jnp.maximum(jnp.broadcast_to(row_max, (rows_g, LANES)), sink)
        e_c = _bf(jnp.exp(s_c - jnp.concatenate([m] * bg, axis=1)))
        e_n = _bf(jnp.exp(s_n - m[:, :r]))
        den = (_dot(e_c, jnp.ones((bg * w, LANES), BF16)) + _dot(e_n, jnp.ones((r, LANES), BF16))
               + jnp.exp(sink - m))
        o = (_dot(e_c, v_cache) + _dot(e_n, _bf(v_new))) / den
        for i, pair in enumerate(_swa_merge_heads(o, kh, r)):
            col_i = kh * (SWA_GROUP // 2) + i
            o_ref[:, col_i * LANES:(col_i + 1) * LANES] = pair

    nk_ref[:, 0:w - ls, :] = ck_ref[:, ls:w, :]
    nv_ref[:, 0:w - ls, :] = cv_ref[:, ls:w, :]
    for b in range(bg):
        nk_ref[b, w - ls:w, :] = k_new[b * ls:(b + 1) * ls, :]
        nv_ref[b, w - ls:w, :] = v_new[b * ls:(b + 1) * ls, :]


def _swa_sample(proj, cache_k, cache_v, cq, sq, qg, kg, sink_rows, *, bg, ls):
    t = proj.shape[0]
    w = SWA_WINDOW
    r = bg * ls
    full = lambda a: pl.BlockSpec(a.shape, lambda i: (0,) * a.ndim)
    cache_spec = pl.BlockSpec((bg, w, LANES), lambda i: (i, 0, 0))
    return pl.pallas_call(
        functools.partial(_swa_sample_kernel, bg=bg, ls=ls),
        grid=(t // r,),
        in_specs=[
            pl.BlockSpec((r, 512), lambda i: (i, OD_Q_BLOCK)),
            pl.BlockSpec((r, 256), lambda i: (i, OD_KV_BLOCK)),
            cache_spec, cache_spec, full(cq), full(sq), full(qg), full(kg), full(sink_rows),
        ],
        out_specs=[pl.BlockSpec((r, 512), lambda i: (i, 0)), cache_spec, cache_spec],
        out_shape=[jax.ShapeDtypeStruct((t, 512), F32),
                   jax.ShapeDtypeStruct(cache_k.shape, F32), jax.ShapeDtypeStruct(cache_v.shape, F32)],
        compiler_params=_params("parallel"),
        name="swa_sample",
    )(proj, proj, cache_k, cache_v, cq, sq, qg, kg, sink_rows)


def _mem_prompt_kernel(x_ref, g_ref, wq_ref, qg_ref, k_ref, v_ref, wo_ref, o_ref):
    x = x_ref[0]
    xn = _bf(_rms(x, g_ref[...]))
    acc = x
    for h in range(MEM_HEADS):
        sl = slice(h * MEM_HD, (h + 1) * MEM_HD)
        q = _bf(_rms(_dot(xn, wq_ref[:, sl]), qg_ref[...]))
        s = _dot_nt(q, _bf(k_ref[0, :, sl])) * (MEM_HD ** -0.5)
        e = jnp.exp(s - jnp.max(s, axis=-1, keepdims=True))
        p = e / jnp.sum(e, axis=-1, keepdims=True)
        o = _dot(_bf(p), _bf(v_ref[0, :, sl]))
        acc = acc + _dot(_bf(o), wo_ref[sl, :])
    o_ref[0] = acc


def _mem_prompt(x, gain, wq, q_gain, k, v, wo, *, lt):
    b, l, d = x.shape
    n_mem = k.shape[1]
    kv_spec = pl.BlockSpec((1, n_mem, d), lambda i, j: (i, 0, 0))
    io_spec = pl.BlockSpec((1, lt, d), lambda i, j: (i, j, 0))
    full = lambda a: pl.BlockSpec(a.shape, lambda i, j: (0,) * a.ndim)
    return pl.pallas_call(
        _mem_prompt_kernel,
        grid=(b, l // lt),
        in_specs=[io_spec, full(gain), full(wq), full(q_gain), kv_spec, kv_spec, full(wo)],
        out_specs=io_spec,
        out_shape=jax.ShapeDtypeStruct(x.shape, F32),
        compiler_params=_params("parallel", "arbitrary"),
        name="mem_prompt",
    )(x, gain, wq, q_gain, k, v, wo)


def _mem_sample_kernel(q_ref, k_ref, v_ref, o_ref, *, bs, ls):
    r = bs * ls
    n_rows = MEM_HEADS * ls
    hd_shift = MEM_HD.bit_length() - 1
    ls_shift = ls.bit_length() - 1
    qb = _bf(q_ref[...])
    head_mask = (_iota((n_rows, D_MODEL), 0) >> ls_shift) == (_iota((n_rows, D_MODEL), 1) >> hd_shift)
    acc = jnp.zeros((r, D_MODEL), F32)
    for b in range(bs):
        sel = _bf(((_iota((n_rows, r), 0) & (ls - 1)) + b * ls == _iota((n_rows, r), 1)).astype(F32))
        q_exp = _bf(jnp.where(head_mask, _dot(sel, qb), 0.0))
        k_b = _mem_rows(k_ref, b)
        v_b = _mem_rows(v_ref, b)
        s = _dot_nt(_bf(k_b), q_exp) * (MEM_HD ** -0.5)
        e = jnp.exp(s - jnp.max(s, axis=0, keepdims=True))
        p = e / jnp.sum(e, axis=0, keepdims=True)
        o_all = jnp.where(head_mask, _dot_tn(_bf(p), _bf(v_b)), 0.0)
        acc = acc + _dot_tn(sel, _bf(o_all))
    o_ref[...] = acc


MEM_LANE_TILES = MEM_HD // LANES
MEM_ROW_GROUP = MEM_HEADS * MEM_LANE_TILES


def _mem_rows(ref, b):
    n_mem = ref.shape[1] // MEM_ROW_GROUP
    return jnp.concatenate([ref[b, pl.ds(lt * MEM_HEADS + h, n_mem, stride=MEM_ROW_GROUP), :]
                            for h in range(MEM_HEADS) for lt in range(MEM_LANE_TILES)], axis=1)


def _mem_flat_view(a):
    depth, nb, n_mem, heads, hd = a.shape
    a = a.reshape(depth, nb, n_mem, heads, hd // LANES, LANES).transpose(0, 1, 2, 4, 3, 5)
    return a.reshape(depth, nb, n_mem * MEM_ROW_GROUP, LANES)


def _mem_sample(q, k, v, *, layer, bs, ls):
    t, d = q.shape
    k, v = _mem_flat_view(k), _mem_flat_view(v)
    r = bs * ls
    kv_spec = pl.BlockSpec((None, bs, k.shape[2], LANES), lambda i: (layer, i, 0, 0))
    io_spec = pl.BlockSpec((r, d), lambda i: (i, 0))
    return pl.pallas_call(
        functools.partial(_mem_sample_kernel, bs=bs, ls=ls),
        grid=(t // r,),
        in_specs=[io_spec, kv_spec, kv_spec],
        out_specs=io_spec,
        out_shape=jax.ShapeDtypeStruct(q.shape, F32),
        compiler_params=_params("parallel"),
        name="mem_sample",
    )(q, k, v)


def _retention_tables(pos):
    inv = 1.0 / (RET_THETA ** jnp.linspace(0.0, 1.0, RET_DK // 2, dtype=F32))
    ang = pos[:, None] * inv[None, :]
    cos = jnp.repeat(jnp.cos(ang), 2, axis=1)
    sin = jnp.stack([-jnp.sin(ang), jnp.sin(ang)], axis=-1).reshape(pos.shape[0], RET_DK)
    return jnp.tile(cos, (1, RET_HEADS)), jnp.tile(sin, (1, RET_HEADS))


def _rope_tables(pos):
    half = ROPE_HALF
    inv = 1.0 / (ROPE_THETA ** (jnp.arange(half, dtype=F32) * 2.0 / (2 * half)))
    ang = pos[:, None] * inv[None, :]
    n = pos.shape[0]
    rest = SWA_HD - 2 * half
    cos = jnp.concatenate([jnp.cos(ang), jnp.cos(ang), jnp.ones((n, rest), F32)], axis=1)
    sin = jnp.concatenate([-jnp.sin(ang), jnp.sin(ang), jnp.zeros((n, rest), F32)], axis=1)
    return jnp.tile(cos, (1, SWA_HEADS)), jnp.tile(sin, (1, SWA_HEADS))


def _block_diag(t):
    g, a, b = t.shape
    eye = jnp.eye(g, dtype=t.dtype)
    return (t[:, :, None, :] * eye[:, None, :, None]).reshape(g * a, g * b)


def _half_block_diag(t):
    per = t.shape[0] // S5_HALVES
    return _bf(jnp.stack([_block_diag(t[h * per:(h + 1) * per]) for h in range(S5_HALVES)]))


def _sink_rows(sinks):
    return jnp.broadcast_to(sinks.astype(F32)[:, None], (sinks.shape[0], LANES))


def _trunk(x3, pos0, states, mem_k, mem_v, w, *, sample):
    b, l, d = x3.shape
    t = b * l
    x = x3.reshape(t, d)
    pos = pos0 + jnp.arange(l, dtype=F32)
    tm = 512 if sample else 1024
    gla_s, ret_s, s5_re, s5_im, swa_k, swa_v = states
    out_states = {k: [] for k in ("gla", "ret", "s5_re", "s5_im", "swa_k", "swa_v")}
    ld_row = jnp.repeat(jnp.log(1.0 - 2.0 ** (-5.0 - jnp.arange(RET_HEADS, dtype=F32))), RET_DK)[None, :]

    for layer in range(2):
        i = layer // 2
        x = _ffn(x, w['ffn1_norm'][layer][None], w['ffn1_w_gate'], w['ffn1_w_up'], w['ffn1_w_down'],
                 layer=layer, tm=tm, th=256)
        if layer % 2 == 0:
            cos, sin = _retention_tables(pos)
            args = (ld_row, w['gla_w_gate'][i], w['gla_b_gate'][i], w['gla_out_norm'][i])
            if sample:
                bg = 16
                proj = _norm_matmul(x, w['mix_norm'][layer][None], w['even_w_in'][i], tm=tm, tn=640)
                mixed, g_s, r_s = _even_sample(proj, jnp.tile(cos, (bg, 1)), jnp.tile(sin, (bg, 1)), *args,
                                               gla_s[i], ret_s[i], bg=bg, ls=l)
                x = _matmul_residual(x, [(mixed, w['even_w_out'][i])], tm=tm)
            else:
                x, g_s, r_s = _even_prompt(x.reshape(b, l, d), w['mix_norm'][layer][None], w['even_w_in'][i],
                                           w['even_w_out'][i], cos, sin, *args, gla_s[i], ret_s[i], lt=512)
                x = x.reshape(t, d)
            out_states["gla"].append(g_s)
            out_states["ret"].append(r_s)
        else:
            cq, sq = _rope_tables(pos)
            qg = jnp.tile(w['swa_q_norm'][i], SWA_HEADS)[None, :]
            kg = jnp.tile(w['swa_k_norm'][i], SWA_KV_HEADS)[None, :]
            s5_args = (w['s5_ab_re'][i], w['s5_ab_im'][i], w['s5_bb_re'][i], w['s5_bb_im'][i],
                       w['s5_cc_re'][i], w['s5_cc_im'][i], w['s5_d'][i][None], w['s5_w_glu'][i], w['s5_b_glu'][i][None])
            n_state = S5_GROUPS * S5_STATE
            s0_re, s0_im = s5_re[i].reshape(b, n_state), s5_im[i].reshape(b, n_state)
            sinks = _sink_rows(w['swa_sinks'][i])
            if sample:
                proj = _norm_matmul(x, w['mix_norm'][layer][None], w['odd_w_in'][i], tm=tm, tn=640)
                c_out, sr, si = _s5(proj.reshape(1, t, OD_COLS), s0_re, s0_im, *s5_args, n_seq=b, lt=l,
                                    groups_per_pass=1, unroll=True)
                bg = 8
                d_out, kb, vb = _swa_sample(proj, swa_k[i].reshape(b, SWA_WINDOW, LANES),
                                            swa_v[i].reshape(b, SWA_WINDOW, LANES),
                                            jnp.tile(cq, (bg, 1)), jnp.tile(sq, (bg, 1)), qg, kg, sinks, bg=bg, ls=l)
                w_out = w['odd_w_out'][i]
                x = _matmul_residual(x, [(c_out.reshape(t, S5_WIDTH), w_out[:S5_WIDTH]), (d_out, w_out[S5_WIDTH:])],
                                     tm=tm)
            else:
                x, sr, si, kb, vb = _odd_prompt(x.reshape(b, l, d), w['mix_norm'][layer][None], w['odd_w_in'][i],
                                                w['odd_w_out'][i], s0_re, s0_im, *s5_args, cq, sq, qg, kg, sinks,
                                                groups_per_pass=8, unroll=4)
                x = x.reshape(t, d)
            out_states["s5_re"].append(sr.reshape(b, S5_GROUPS, S5_STATE))
            out_states["s5_im"].append(si.reshape(b, S5_GROUPS, S5_STATE))
            out_states["swa_k"].append(kb.reshape(b, -1, SWA_KV_HEADS, SWA_HD))
            out_states["swa_v"].append(vb.reshape(b, -1, SWA_KV_HEADS, SWA_HD))
        if sample:
            q = _norm_matmul(x, w['mem_x_norm'][layer][None], w['mem_w_q'][layer], tm=tm, tn=MEM_HD,
                             head_gain=w['mem_q_norm'][layer][None], n_norm_tiles=MEM_HEADS)
            o = _mem_sample(q, mem_k, mem_v, layer=layer, bs=4, ls=l)
            x = _matmul_residual(x, [(o, w['mem_w_o'][layer])], tm=tm)
        else:
            x = _mem_prompt(x.reshape(b, l, d), w['mem_x_norm'][layer][None], w['mem_w_q'][layer],
                            w['mem_q_norm'][layer][None], mem_k[layer], mem_v[layer], w['mem_w_o'][layer],
                            lt=1024).reshape(t, d)
        x = _ffn(x, w['ffn2_norm'][layer][None], w['ffn2_w_gate'], w['ffn2_w_up'], w['ffn2_w_down'],
                 layer=layer, tm=tm, th=256)
    return x.reshape(b, l, d), {k: jnp.stack(v) for k, v in out_states.items()}


def kernel(x_prompt, x_sample, mem_prompt, state_gla, state_ret, state_s5_re, state_s5_im, cache_swa_k, cache_swa_v, cache_mem_k, cache_mem_v, ffn1_norm, ffn1_w_gate, ffn1_w_up, ffn1_w_down, ffn2_norm, ffn2_w_gate, ffn2_w_up, ffn2_w_down, mix_norm, even_w_in, gla_w_gate, gla_b_gate, gla_out_norm, even_w_out, odd_w_in, s5_a_re, s5_a_im, s5_log_step, s5_b_re, s5_b_im, s5_c_re, s5_c_im, s5_d, s5_w_glu, s5_b_glu, swa_q_norm, swa_k_norm, swa_sinks, odd_w_out, mem_x_norm, mem_m_norm, mem_w_q, mem_w_k, mem_w_v, mem_w_o, mem_q_norm, mem_k_norm):
    depth = ffn1_norm.shape[0]
    n_even, n_odd = even_w_in.shape[0], odd_w_in.shape[0]
    batch, seq, d = x_prompt.shape
    dec_batch = x_sample.shape[0]
    n_mem = mem_prompt.shape[1]

    ev = even_w_in
    ev_cols = jnp.concatenate(
        [ev[..., 0:1536], ev[..., 1552:3088], ev[..., 1536:1552],
         jnp.zeros(ev.shape[:2] + (EV_COLS - 3088,), ev.dtype)], axis=-1)
    wgate_pad = jnp.concatenate(
        [gla_w_gate, jnp.zeros((n_even, EV_COLS - EV_GA - GLA_RANK, gla_w_gate.shape[-1]), gla_w_gate.dtype)], axis=1)
    w = dict(
        ffn1_norm=ffn1_norm, ffn2_norm=ffn2_norm, mix_norm=mix_norm, mem_x_norm=mem_x_norm,
        ffn1_w_gate=_bf(ffn1_w_gate), ffn1_w_up=_bf(ffn1_w_up), ffn1_w_down=_bf(ffn1_w_down),
        ffn2_w_gate=_bf(ffn2_w_gate), ffn2_w_up=_bf(ffn2_w_up), ffn2_w_down=_bf(ffn2_w_down),
        even_w_in=_bf(ev_cols), gla_w_gate=_bf(wgate_pad), gla_b_gate=gla_b_gate[:, None, :],
        gla_out_norm=gla_out_norm[:, None, :], even_w_out=_bf(even_w_out),
        odd_w_in=_bf(odd_w_in), odd_w_out=_bf(odd_w_out), s5_d=s5_d, s5_w_glu=_bf(s5_w_glu), s5_b_glu=s5_b_glu,
        swa_q_norm=swa_q_norm, swa_k_norm=swa_k_norm, swa_sinks=swa_sinks,
        mem_w_q=_bf(mem_w_q), mem_w_o=_bf(mem_w_o), mem_q_norm=mem_q_norm,
    )
    ab_re, ab_im, bb_re, bb_im = [], [], [], []
    for i in range(n_odd):
        a_r, a_i, b_r, b_i = _s5_prep(s5_a_re[i], s5_a_im[i], s5_log_step[i], s5_b_re[i], s5_b_im[i])
        ab_re.append(a_r.reshape(-1))
        ab_im.append(a_i.reshape(-1))
        bb_re.append(_half_block_diag(b_r))
        bb_im.append(_half_block_diag(b_i))
    w.update(s5_ab_re=ab_re, s5_ab_im=ab_im, s5_bb_re=bb_re, s5_bb_im=bb_im,
             s5_cc_re=[_half_block_diag(jnp.swapaxes(s5_c_re[i], 1, 2)) for i in range(n_odd)],
             s5_cc_im=[_half_block_diag(jnp.swapaxes(s5_c_im[i], 1, 2)) for i in range(n_odd)])

    mem2 = mem_prompt.reshape(batch * n_mem, d)
    p_mem_k, p_mem_v = [], []
    for layer in range(depth):
        w_kv = _bf(jnp.concatenate([mem_w_k[layer], mem_w_v[layer]], axis=1))
        kv = _norm_matmul(mem2, mem_m_norm[layer][None], w_kv, tm=batch * n_mem, tn=MEM_HD,
                          head_gain=mem_k_norm[layer][None], n_norm_tiles=MEM_HEADS)
        p_mem_k.append(kv[:, :d].reshape(batch, n_mem, d))
        p_mem_v.append(kv[:, d:].reshape(batch, n_mem, d))

    zeros = lambda *s: jnp.zeros(s, F32)
    p_states = (zeros(n_even, batch, GLA_HEADS, GLA_DK, GLA_DV), zeros(n_even, batch, RET_HEADS, RET_DK, GLA_DV),
                zeros(n_odd, batch, S5_GROUPS, S5_STATE), zeros(n_odd, batch, S5_GROUPS, S5_STATE), None, None)
    y_prompt, ps = _trunk(x_prompt, 0.0, p_states, p_mem_k, p_mem_v, w, sample=False)

    s_states = (state_gla, state_ret, state_s5_re, state_s5_im, cache_swa_k, cache_swa_v)
    y_sample, ss = _trunk(x_sample, float(PAST_LEN), s_states, cache_mem_k, cache_mem_v, w, sample=True)

    p_mem_k = jnp.stack(p_mem_k).reshape(depth, batch, n_mem, MEM_HEADS, MEM_HD)
    p_mem_v = jnp.stack(p_mem_v).reshape(depth, batch, n_mem, MEM_HEADS, MEM_HD)
    return (y_prompt, y_sample, ps["gla"], ps["ret"], ps["s5_re"], ps["s5_im"], ps["swa_k"], ps["swa_v"],
            p_mem_k, p_mem_v, ss["gla"], ss["ret"], ss["s5_re"], ss["s5_im"], ss["swa_k"], ss["swa_v"])
```

```python
import functools
import math

import jax
import jax.numpy as jnp
import numpy as np
from jax import lax
from jax.experimental import pallas as pl
from jax.experimental.pallas import tpu as pltpu

F32 = jnp.float32
BF16 = jnp.bfloat16
NORM_EPS = 1e-6
HIGHEST = lax.Precision.HIGHEST

D_MODEL = 1024
GLA_HEADS = 4
GLA_DK = 64
GLA_DV = 128
GLA_RANK = 16
GLA_TAU = 16.0
RET_HEADS = 4
RET_DK = 64
RET_THETA = 10000.0
LA_CHUNK = 64
S5_WIDTH = 512
S5_GROUP = 16
S5_GROUPS = 32
S5_STATE = 64
SWA_HD = 64
SWA_HEADS = 8
SWA_KV_HEADS = 2
SWA_WINDOW = 128
ROPE_THETA = 500000.0
MEM_HEADS = 4
MEM_HD = 256
PAST_LEN = 8192

VMEM_LIMIT_BYTES = 52 * 1024 * 1024
LANES = 128
SUBLANES = 8


def _params(*sem):
    return pltpu.CompilerParams(dimension_semantics=sem, vmem_limit_bytes=VMEM_LIMIT_BYTES)


def _rms(x, gain=None):
    y = x * lax.rsqrt(jnp.mean(x * x, axis=-1, keepdims=True) + NORM_EPS)
    return y if gain is None else y * gain


def _dot(a, b):
    return jnp.dot(a, b, preferred_element_type=F32)


def _dot_nt(a, b):
    return lax.dot_general(a, b, (((1,), (1,)), ((), ())), preferred_element_type=F32)


def _dot_tn(a, b):
    return lax.dot_general(a, b, (((0,), (0,)), ((), ())), preferred_element_type=F32)


def _dot_f32(a, b):
    return jnp.dot(a, b, precision=HIGHEST, preferred_element_type=F32)


def _split_bf16(x, terms):
    pieces = []
    for _ in range(terms):
        piece = _bf(x)
        pieces.append(piece)
        x = x - piece.astype(F32)
    return pieces


def _dot_exact_lhs(a, x, terms=3):
    return sum(_dot(a, piece) for piece in _split_bf16(x, terms))


def _dot_exact_rhs(x, b, terms=3):
    return sum(_dot(piece, b) for piece in _split_bf16(x, terms))


def _dot_nt_f32(a, b):
    return lax.dot_general(a, b, (((1,), (1,)), ((), ())), precision=HIGHEST, preferred_element_type=F32)


def _bf(x):
    return x.astype(BF16)


def _log_sigmoid(x):
    return jnp.minimum(x, 0.0) - jnp.log1p(jnp.exp(-jnp.abs(x)))


def _iota(shape, dim):
    return lax.broadcasted_iota(jnp.int32, shape, dim)


def _lane_tile(x, n):
    return jnp.concatenate([x] * n, axis=1)


def _swap_pairs(x):
    n = x.shape[-1]
    even = (_iota(x.shape, 1) & 1) == 0
    return jnp.where(even, pltpu.roll(x, n - 1, 1), pltpu.roll(x, 1, 1))


def _rope_partner(x, head_dim, half):
    n = x.shape[-1]
    first = (_iota(x.shape, 1) & (head_dim - 1)) < half
    return jnp.where(first, pltpu.roll(x, n - half, 1), pltpu.roll(x, half, 1))


def _ffn_kernel(x_ref, g_ref, wg_ref, wu_ref, wd_ref, o_ref, xn_ref):
    @pl.when(pl.program_id(1) == 0)
    def _():
        x = x_ref[...]
        xn_ref[...] = _bf(_rms(x, g_ref[...]))
        o_ref[...] = x

    xn = xn_ref[...]
    gate = _dot(xn, wg_ref[...])
    up = _dot(xn, wu_ref[...])
    o_ref[...] += _dot(_bf(jax.nn.silu(gate) * up * 0.5), wd_ref[...])


def _ffn(x, gain, wg, wu, wd, *, layer, tm, th):
    t, d = x.shape
    h = wg.shape[2]
    return pl.pallas_call(
        _ffn_kernel,
        grid=(t // tm, h // th),
        in_specs=[
            pl.BlockSpec((tm, d), lambda i, j: (i, 0)),
            pl.BlockSpec((1, d), lambda i, j: (0, 0)),
            pl.BlockSpec((None, d, th), lambda i, j: (layer, 0, j)),
            pl.BlockSpec((None, d, th), lambda i, j: (layer, 0, j)),
            pl.BlockSpec((None, th, d), lambda i, j: (layer, j, 0)),
        ],
        out_specs=pl.BlockSpec((tm, d), lambda i, j: (i, 0)),
        out_shape=jax.ShapeDtypeStruct((t, d), F32),
        scratch_shapes=[pltpu.VMEM((tm, d), BF16)],
        compiler_params=_params("parallel", "arbitrary"),
        name="ffn",
    )(x, gain, wg, wu, wd)


def _nmm_kernel(x_ref, g_ref, w_ref, hg_ref, o_ref, xn_ref, *, n_norm_tiles):
    j = pl.program_id(1)

    @pl.when(j == 0)
    def _():
        xn_ref[...] = _bf(_rms(x_ref[...], g_ref[...]))

    y = _dot(xn_ref[...], w_ref[...])
    if n_norm_tiles == 0:
        o_ref[...] = y
    else:
        @pl.when(j < n_norm_tiles)
        def _():
            o_ref[...] = _rms(y, hg_ref[...])

        @pl.when(j >= n_norm_tiles)
        def _():
            o_ref[...] = y


def _norm_matmul(x, gain, w, *, tm, tn, head_gain=None, n_norm_tiles=0):
    t, d = x.shape
    n = w.shape[1]
    if head_gain is None:
        head_gain = jnp.ones((1, tn), F32)
    return pl.pallas_call(
        functools.partial(_nmm_kernel, n_norm_tiles=n_norm_tiles),
        grid=(t // tm, n // tn),
        in_specs=[
            pl.BlockSpec((tm, d), lambda i, j: (i, 0)),
            pl.BlockSpec((1, d), lambda i, j: (0, 0)),
            pl.BlockSpec((d, tn), lambda i, j: (0, j)),
            pl.BlockSpec((1, tn), lambda i, j: (0, 0)),
        ],
        out_specs=pl.BlockSpec((tm, tn), lambda i, j: (i, j)),
        out_shape=jax.ShapeDtypeStruct((t, n), F32),
        scratch_shapes=[pltpu.VMEM((tm, d), BF16)],
        compiler_params=_params("parallel", "arbitrary"),
        name="norm_matmul",
    )(x, gain, w, head_gain)


def _mmr_kernel(*refs, n_terms):
    x_ref = refs[0]
    a_refs = refs[1:1 + n_terms]
    w_refs = refs[1 + n_terms:1 + 2 * n_terms]
    o_ref = refs[1 + 2 * n_terms]
    acc = x_ref[...]
    for a_ref, w_ref in zip(a_refs, w_refs):
        acc = acc + _dot(_bf(a_ref[...]), w_ref[...])
    o_ref[...] = acc


def _matmul_residual(x, terms, *, tm):
    t, d = x.shape
    acts = [a for a, _ in terms]
    ws = [w for _, w in terms]
    in_specs = [pl.BlockSpec((tm, d), lambda i: (i, 0))]
    in_specs += [pl.BlockSpec((tm, a.shape[1]), lambda i: (i, 0)) for a in acts]
    in_specs += [pl.BlockSpec(w.shape, lambda i: (0, 0)) for w in ws]
    return pl.pallas_call(
        functools.partial(_mmr_kernel, n_terms=len(terms)),
        grid=(t // tm,),
        in_specs=in_specs,
        out_specs=pl.BlockSpec((tm, d), lambda i: (i, 0)),
        out_shape=jax.ShapeDtypeStruct((t, d), F32),
        compiler_params=_params("parallel"),
        name="matmul_residual",
    )(x, *acts, *ws)


EV_GQ, EV_GK, EV_GV, EV_GG = 0, 256, 512, 1024
EV_RQ, EV_RK, EV_RV, EV_RG = 1536, 1792, 2048, 2560
EV_GA = 3072
EV_COLS = 3200
EV_BLOCK = 256
PAIR = 2


def _gate_and_norm(o, gate, gain=None):
    return _rms(o, gain) * jax.nn.silu(gate)


def _even_prompt_kernel(x_ref, gain_ref, win_ref, wout_ref, cos_ref, sin_ref, ld_ref, wgate_ref, bgate_ref, gnorm_ref,
                        s0g_ref, s0r_ref, o_ref, sg_ref, sr_ref, proj_sc, mix_sc, o_sc, *, chunk, n_chunks):
    @pl.when(pl.program_id(1) == 0)
    def _():
        sg_ref[...] = s0g_ref[...]
        sr_ref[...] = s0r_ref[...]

    c = chunk
    lt = c * n_chunks
    blk = min(EV_BLOCK, lt)
    c_shift = c.bit_length() - 1
    x = x_ref[0]
    proj_sc[...] = _dot(_bf(_rms(x, gain_ref[...])), win_ref[...])

    def cols(a, b):
        return proj_sc[:, a:b]

    log_a = _log_sigmoid(_dot(_bf(cols(EV_GA, EV_COLS)), wgate_ref[...]) + bgate_ref[...]) * (1.0 / GLA_TAU)
    tril = _bf((_iota((c, c), 1) <= _iota((c, c), 0)).astype(F32))
    cum_parts = [_dot_exact_lhs(tril, log_a[i * c:(i + 1) * c]) for i in range(n_chunks)]
    tots = [p[c - 1:c] for p in cum_parts]
    cum = jnp.concatenate(cum_parts, axis=0)
    tot_b = jnp.concatenate([jnp.broadcast_to(t, (c, 256)) for t in tots], axis=0)
    k = cols(EV_GK, EV_GK + 256)
    gla = (cols(EV_GQ, EV_GQ + 256) * (GLA_DK ** -0.5) * jnp.exp(cum), k * jnp.exp(-cum), k * jnp.exp(tot_b - cum))

    ld = ld_ref[...]
    tpos = ((_iota((lt, 1), 0) & (c - 1)) + 1).astype(F32)
    cum_r = tpos * ld
    tot_r = float(c) * ld
    cos, sin = _lane_tile(cos_ref[...], PAIR), _lane_tile(sin_ref[...], PAIR)
    rq = cols(EV_RQ, EV_RQ + 256)
    rk = cols(EV_RK, EV_RK + 256)
    q_rot = rq * cos + _swap_pairs(rq) * sin
    k_rot = (rk * cos + _swap_pairs(rk) * sin) * (RET_DK ** -0.5)
    ret = (q_rot * jnp.exp(cum_r), k_rot * jnp.exp(-cum_r), k_rot * jnp.exp(tot_r - cum_r))

    tot_rows = jnp.concatenate(tots + [tot_r, jnp.zeros((LANES - n_chunks - 1, 256), F32)], axis=0)
    decay_cols = jnp.exp(jnp.transpose(tot_rows))

    row = _iota((blk, blk), 0)
    col = _iota((blk, blk), 1)
    blk_mask = ((row >> c_shift) == (col >> c_shift)) & (col <= row)
    lo = (_iota((lt, LANES), 1) < GLA_DK)
    gnorm = gnorm_ref[...]

    mixers = ((gla, EV_GV, EV_GG, sg_ref, 0, gnorm, lambda i: i),
              (ret, EV_RV, EV_RG, sr_ref, 512, None, lambda i: n_chunks))
    for m, ((q_dec, k_inv, k_dec), v_col, g_col, s_ref, out_col, gain, decay_col_of) in enumerate(mixers):
        for p in range(GLA_HEADS // PAIR):
            lanes = slice(p * LANES, (p + 1) * LANES)
            q_pair = q_dec[:, lanes]
            q_masked = [_bf(jnp.where(lo, q_pair, 0.0)), _bf(jnp.where(lo, 0.0, q_pair))]
            ki = _bf(k_inv[:, lanes])
            kd = _bf(k_dec[:, lanes])
            v_pair = _bf(cols(v_col + p * PAIR * GLA_DV, v_col + (p + 1) * PAIR * GLA_DV))
            for e in range(PAIR):
                slot = (m * (GLA_HEADS // PAIR) + p) * PAIR + e
                for r0 in range(0, lt, blk):
                    rs = slice(r0, r0 + blk)
                    scores = jnp.where(blk_mask, _dot_nt(q_masked[e][rs], ki[rs]), 0.0)
                    o_sc[slot, rs, :] = _dot(_bf(scores), v_pair[rs, e * GLA_DV:(e + 1) * GLA_DV])
            state = s_ref[0, p * PAIR:(p + 1) * PAIR].reshape(PAIR * GLA_DK, GLA_DV)
            for i in range(n_chunks):
                rs = slice(i * c, (i + 1) * c)
                q_stack = jnp.concatenate([q_masked[0][rs], q_masked[1][rs]], axis=0)
                o_inter = _dot(q_stack, _bf(state))
                kv = _dot_tn(kd[rs], v_pair[rs])
                kv = jnp.concatenate([kv[:GLA_DK, :GLA_DV], kv[GLA_DK:, GLA_DV:]], axis=0)
                ci = decay_col_of(i)
                state = state * decay_cols[p * LANES:(p + 1) * LANES, ci:ci + 1] + kv
                for e in range(PAIR):
                    slot = (m * (GLA_HEADS // PAIR) + p) * PAIR + e
                    o_sc[slot, rs, :] += o_inter[e * c:(e + 1) * c]
            s_ref[0, p * PAIR:(p + 1) * PAIR] = state.reshape(PAIR, GLA_DK, GLA_DV)
            for e in range(PAIR):
                h = p * PAIR + e
                slot = (m * (GLA_HEADS // PAIR) + p) * PAIR + e
                gate = cols(g_col + h * GLA_DV, g_col + (h + 1) * GLA_DV)
                mix_sc[:, out_col + h * GLA_DV:out_col + (h + 1) * GLA_DV] = _gate_and_norm(o_sc[slot], gate, gain)
    o_ref[0] = x + _dot(_bf(mix_sc[...]), wout_ref[...])


def _even_prompt(x, gain, w_in, w_out, cos, sin, ld_row, wgate, bgate, gnorm, s0g, s0r, *, lt):
    b, l, d = x.shape
    chunk = math.gcd(l, LA_CHUNK)
    st_spec = pl.BlockSpec((1, GLA_HEADS, GLA_DK, GLA_DV), lambda i, j: (i, 0, 0, 0))
    io_spec = pl.BlockSpec((1, lt, d), lambda i, j: (i, j, 0))
    full = lambda a: pl.BlockSpec(a.shape, lambda i, j: (0,) * a.ndim)
    return pl.pallas_call(
        functools.partial(_even_prompt_kernel, chunk=chunk, n_chunks=lt // chunk),
        grid=(b, l // lt),
        in_specs=[
            io_spec, full(gain), full(w_in), full(w_out),
            pl.BlockSpec((lt, LANES), lambda i, j: (j, 0)),
            pl.BlockSpec((lt, LANES), lambda i, j: (j, 0)),
            full(ld_row), full(wgate), full(bgate), full(gnorm), st_spec, st_spec,
        ],
        out_specs=[io_spec, st_spec, st_spec],
        out_shape=[jax.ShapeDtypeStruct(x.shape, F32),
                   jax.ShapeDtypeStruct(s0g.shape, F32), jax.ShapeDtypeStruct(s0r.shape, F32)],
        scratch_shapes=[pltpu.VMEM((lt, EV_COLS), F32), pltpu.VMEM((lt, d), F32),
                        pltpu.VMEM((GLA_HEADS + RET_HEADS, lt, GLA_DV), F32)],
        compiler_params=_params("parallel", "arbitrary"),
        name="even_prompt",
    )(x, gain, w_in, w_out, cos, sin, ld_row, wgate, bgate, gnorm, s0g, s0r)


def _even_sample_kernel(proj_ref, cos_ref, sin_ref, ld_ref, wgate_ref, bgate_ref, gnorm_ref, s0g_ref, s0r_ref,
                        mix_ref, sg_ref, sr_ref, *, bg, ls):
    r = bg * ls
    ls_shift = ls.bit_length() - 1
    dk_shift = GLA_DK.bit_length() - 1
    n_exp = bg * GLA_DK
    row_seq = _iota((r, r), 0) >> ls_shift
    col_seq = _iota((r, r), 1) >> ls_shift
    same = row_seq == col_seq
    seg = same & (_iota((r, r), 1) <= _iota((r, r), 0))
    seg_f = seg.astype(F32)
    same_f = same.astype(F32)
    tile_b = _bf(((_iota((GLA_DK, n_exp), 1) & (GLA_DK - 1)) == _iota((GLA_DK, n_exp), 0)).astype(F32))
    tile_t_f = ((_iota((n_exp, GLA_DK), 0) & (GLA_DK - 1)) == _iota((n_exp, GLA_DK), 1)).astype(F32)
    tile_t_b = _bf(tile_t_f)
    q_mask = (_iota((r, n_exp), 0) >> ls_shift) == (_iota((r, n_exp), 1) >> dk_shift)
    k_mask = (_iota((n_exp, r), 0) >> dk_shift) == (_iota((n_exp, r), 1) >> ls_shift)
    tpos = ((_iota((r, 1), 0) & (ls - 1)) + 1).astype(F32)
    ld = ld_ref[...]
    cum_r = tpos * ld
    tot_r = float(ls) * ld
    r_dec = jnp.exp(tot_r)
    gnorm = gnorm_ref[...]

    def mixer(q_dec, k_inv, k_dec, v_col, g_col, s0_ref, s_ref, out_col, decay_of, gain):
        for h in range(GLA_HEADS):
            sl = slice(h * GLA_DK, (h + 1) * GLA_DK)
            v = _bf(proj_ref[:, v_col + h * GLA_DV:v_col + (h + 1) * GLA_DV])
            qd = _bf(q_dec[:, sl])
            scores = jnp.where(seg, _dot_nt(qd, _bf(k_inv[:, sl])), 0.0)
            state = s0_ref[:, h].reshape(n_exp, GLA_DV)
            q_exp = _bf(jnp.where(q_mask, _dot(qd, tile_b), 0.0))
            o = _dot(_bf(scores), v) + _dot(q_exp, _bf(state))
            k_exp = _bf(jnp.where(k_mask, _dot_nt(tile_t_b, _bf(k_dec[:, sl])), 0.0))
            new_state = state * decay_of(h, sl) + _dot(k_exp, v)
            s_ref[:, h] = new_state.reshape(bg, GLA_DK, GLA_DV)
            gate = proj_ref[:, g_col + h * GLA_DV:g_col + (h + 1) * GLA_DV]
            mix_ref[:, out_col + h * GLA_DV:out_col + (h + 1) * GLA_DV] = _gate_and_norm(o, gate, gain)

    log_a = _log_sigmoid(_dot(_bf(proj_ref[:, EV_GA:EV_COLS]), wgate_ref[...]) + bgate_ref[...]) * (1.0 / GLA_TAU)
    cum = _dot_f32(seg_f, log_a)
    tot = _dot_f32(same_f, log_a)
    k = proj_ref[:, EV_GK:EV_GK + 256]

    def gla_decay(h, sl):
        la_exp = jnp.where(k_mask, _dot_nt_f32(tile_t_f, log_a[:, sl]), 0.0)
        return jnp.exp(jnp.sum(la_exp, axis=-1, keepdims=True))

    mixer(proj_ref[:, EV_GQ:EV_GQ + 256] * (GLA_DK ** -0.5) * jnp.exp(cum), k * jnp.exp(-cum), k * jnp.exp(tot - cum),
          EV_GV, EV_GG, s0g_ref, sg_ref, 0, gla_decay, gnorm)

    cos = _lane_tile(cos_ref[...], PAIR)
    sin = _lane_tile(sin_ref[...], PAIR)
    rq = proj_ref[:, EV_RQ:EV_RQ + 256]
    rk = proj_ref[:, EV_RK:EV_RK + 256]
    q_rot = rq * cos + _swap_pairs(rq) * sin
    k_rot = (rk * cos + _swap_pairs(rk) * sin) * (RET_DK ** -0.5)

    def ret_decay(h, sl):
        return r_dec[:, h * RET_DK:h * RET_DK + 1]

    mixer(q_rot * jnp.exp(cum_r), k_rot * jnp.exp(-cum_r), k_rot * jnp.exp(tot_r - cum_r),
          EV_RV, EV_RG, s0r_ref, sr_ref, 512, ret_decay, None)


def _even_sample(proj, cos, sin, ld_row, wgate, bgate, gnorm, s0g, s0r, *, bg, ls):
    t = proj.shape[0]
    n_b = t // ls
    r = bg * ls
    st_spec = pl.BlockSpec((bg, GLA_HEADS, GLA_DK, GLA_DV), lambda i: (i, 0, 0, 0))
    full = lambda a: pl.BlockSpec(a.shape, lambda i: (0,) * a.ndim)
    return pl.pallas_call(
        functools.partial(_even_sample_kernel, bg=bg, ls=ls),
        grid=(n_b // bg,),
        in_specs=[
            pl.BlockSpec((r, EV_COLS), lambda i: (i, 0)),
            full(cos), full(sin), full(ld_row), full(wgate), full(bgate), full(gnorm), st_spec, st_spec,
        ],
        out_specs=[pl.BlockSpec((r, D_MODEL), lambda i: (i, 0)), st_spec, st_spec],
        out_shape=[jax.ShapeDtypeStruct((t, D_MODEL), F32),
                   jax.ShapeDtypeStruct(s0g.shape, F32), jax.ShapeDtypeStruct(s0r.shape, F32)],
        compiler_params=_params("parallel"),
        name="even_sample",
    )(proj, cos, sin, ld_row, wgate, bgate, gnorm, s0g, s0r)


def _s5_prep_kernel(are_ref, aim_ref, lstep_ref, bre_ref, bim_ref, abre_ref, abim_ref, bbre_ref, bbim_ref):
    a_re, a_im = are_ref[...], aim_ref[...]
    step = jnp.exp(lstep_ref[...])
    mag = jnp.exp(a_re * step)
    ab_re = mag * jnp.cos(a_im * step)
    ab_im = mag * jnp.sin(a_im * step)
    den = a_re * a_re + a_im * a_im
    coef_re = ((ab_re - 1.0) * a_re + ab_im * a_im) / den
    coef_im = (ab_im * a_re - (ab_re - 1.0) * a_im) / den
    b_re, b_im = bre_ref[...], bim_ref[...]
    abre_ref[...] = ab_re
    abim_ref[...] = ab_im
    bbre_ref[...] = coef_re * b_re - coef_im * b_im
    bbim_ref[...] = coef_re * b_im + coef_im * b_re


def _s5_prep(a_re, a_im, log_step, b_re, b_im):
    g, n = a_re.shape
    shp3 = jax.ShapeDtypeStruct((g, 1, n), F32)
    shpb = jax.ShapeDtypeStruct((g, S5_GROUP, n), F32)
    return pl.pallas_call(_s5_prep_kernel, out_shape=[shp3, shp3, shpb, shpb], name="s5_prep")(
        a_re.reshape(g, 1, n), a_im.reshape(g, 1, n), log_step.reshape(g, 1, 1),
        jnp.swapaxes(b_re, 1, 2), jnp.swapaxes(b_im, 1, 2))


S5_LANE_CHUNKS = S5_GROUPS * S5_STATE // LANES
S5_HALVES = 2


def _s5_layout(n_seq, lt):
    pack = max(1, SUBLANES // n_seq)
    pitch = lt + 4 if lt % SUBLANES == 0 else lt
    return pack, S5_LANE_CHUNKS // pack, pitch


def _s5_slot(c, n_groups):
    return c % n_groups, c // n_groups


def _s5_load_state(s0re_ref, s0im_ref, hre_ref, him_ref, n_seq, n_groups):
    for c in range(S5_LANE_CHUNKS):
        g, j = _s5_slot(c, n_groups)
        hre_ref[g, j * n_seq:(j + 1) * n_seq, :] = s0re_ref[:, c * LANES:(c + 1) * LANES]
        him_ref[g, j * n_seq:(j + 1) * n_seq, :] = s0im_ref[:, c * LANES:(c + 1) * LANES]


def _s5_store_state(sre_ref, sim_ref, hre_ref, him_ref, n_seq, n_groups):
    for c in range(S5_LANE_CHUNKS):
        g, j = _s5_slot(c, n_groups)
        sre_ref[:, c * LANES:(c + 1) * LANES] = hre_ref[g, j * n_seq:(j + 1) * n_seq, :]
        sim_ref[:, c * LANES:(c + 1) * LANES] = him_ref[g, j * n_seq:(j + 1) * n_seq, :]


def _s5_core(u, abre_ref, abim_ref, bbre_ref, bbim_ref, ccre_ref, ccim_ref, d_ref, wglu_ref, bglu_ref,
             xr_ref, xi_ref, hre_ref, him_ref, *, n_seq, lt, groups_per_pass, unroll):
    pack, n_groups, pitch = _s5_layout(n_seq, lt)
    rows = n_seq * lt
    per_half = S5_LANE_CHUNKS // S5_HALVES
    lanes_of = lambda c: slice(c * LANES, (c + 1) * LANES)
    slot_of = lambda c: _s5_slot(c, n_groups)

    def seq_rows(j, s):
        return slice((j * n_seq + s) * pitch, (j * n_seq + s) * pitch + lt)

    ub = _bf(u)
    u_cols = S5_WIDTH // S5_HALVES
    for half in range(S5_HALVES):
        uh = ub[:, half * u_cols:(half + 1) * u_cols]
        for x_ref, bb_ref in ((xr_ref, bbre_ref), (xi_ref, bbim_ref)):
            x = _dot(uh, bb_ref[half])
            for k in range(per_half):
                g, j = slot_of(half * per_half + k)
                if pitch == lt:
                    x_ref[g, j * rows:(j + 1) * rows, :] = x[:, lanes_of(k)]
                else:
                    for s in range(n_seq):
                        x_ref[g, seq_rows(j, s), :] = x[s * lt:(s + 1) * lt, lanes_of(k)]

    for g0 in range(0, n_groups, groups_per_pass):
        gs = list(range(g0, g0 + groups_per_pass))
        init = tuple(hre_ref[g] for g in gs) + tuple(him_ref[g] for g in gs)

        def step(t, carry, gs=gs):
            rws = pl.ds(t, pack * n_seq, stride=pitch)
            new_re, new_im = [], []
            for k, g in enumerate(gs):
                a_re, a_im = abre_ref[g], abim_ref[g]
                h_re, h_im = carry[k], carry[len(gs) + k]
                n_re = a_re * h_re - a_im * h_im + xr_ref[g, rws, :]
                n_im = a_re * h_im + a_im * h_re + xi_ref[g, rws, :]
                xr_ref[g, rws, :] = n_re
                xi_ref[g, rws, :] = n_im
                new_re.append(n_re)
                new_im.append(n_im)
            return tuple(new_re + new_im)

        fin = lax.fori_loop(0, lt, step, init, unroll=unroll)
        for k, g in enumerate(gs):
            hre_ref[g] = fin[k]
            him_ref[g] = fin[len(gs) + k]

    def gather(x_ref, half):
        cols = []
        for k in range(per_half):
            g, j = slot_of(half * per_half + k)
            if pitch == lt:
                cols.append(x_ref[g, j * rows:(j + 1) * rows, :])
            else:
                cols.append(jnp.concatenate([x_ref[g, seq_rows(j, s), :] for s in range(n_seq)], axis=0))
        return _bf(jnp.concatenate(cols, axis=1))

    y = jnp.concatenate([_dot(gather(xr_ref, half), ccre_ref[half]) - _dot(gather(xi_ref, half), ccim_ref[half])
                         for half in range(S5_HALVES)], axis=1) + d_ref[...] * u
    z = jax.nn.gelu(y, approximate=True)
    return z * jax.nn.sigmoid(_dot(_bf(z), wglu_ref[...]) + bglu_ref[...])


def _s5_kernel(u_ref, s0re_ref, s0im_ref, abre_ref, abim_ref, bbre_ref, bbim_ref, ccre_ref, ccim_ref,
               d_ref, wglu_ref, bglu_ref, out_ref, sre_ref, sim_ref, xr_ref, xi_ref, hre_ref, him_ref,
               *, n_seq, lt, groups_per_pass, unroll):
    n_groups = _s5_layout(n_seq, lt)[1]

    @pl.when(pl.program_id(0) == 0)
    def _():
        _s5_load_state(s0re_ref, s0im_ref, hre_ref, him_ref, n_seq, n_groups)

    out = _s5_core(u_ref[...].reshape(n_seq * lt, S5_WIDTH), abre_ref, abim_ref, bbre_ref, bbim_ref, ccre_ref, ccim_ref,
                   d_ref, wglu_ref, bglu_ref, xr_ref, xi_ref, hre_ref, him_ref,
                   n_seq=n_seq, lt=lt, groups_per_pass=groups_per_pass, unroll=unroll)
    out_ref[...] = out.reshape(out_ref.shape)

    @pl.when(pl.program_id(0) == pl.num_programs(0) - 1)
    def _():
        _s5_store_state(sre_ref, sim_ref, hre_ref, him_ref, n_seq, n_groups)


def _s5_tables(ab, n_seq, lt):
    pack, n_groups, _ = _s5_layout(n_seq, lt)
    tab = jnp.swapaxes(ab.reshape(pack, n_groups, 1, LANES), 0, 1)
    return jnp.broadcast_to(tab, (n_groups, pack, n_seq, LANES)).reshape(n_groups, pack * n_seq, LANES)


def _s5(proj3, s0_re, s0_im, ab_re, ab_im, bb_re, bb_im, cc_re, cc_im, d_row, wglu, bglu, *, n_seq, lt,
        groups_per_pass, unroll):
    nb, rows_b, _ = proj3.shape
    blk_rows = n_seq * lt // nb
    pack, n_groups, pitch = _s5_layout(n_seq, lt)
    ab_re, ab_im = _s5_tables(ab_re, n_seq, lt), _s5_tables(ab_im, n_seq, lt)
    full = lambda a: pl.BlockSpec(a.shape, lambda j: (0,) * a.ndim)
    io_spec = pl.BlockSpec((nb, blk_rows, S5_WIDTH), lambda j: (0, j, 0))
    x_scratch = pltpu.VMEM((n_groups, pack * n_seq * pitch, LANES), F32)
    h_scratch = pltpu.VMEM((n_groups, pack * n_seq, LANES), F32)
    return pl.pallas_call(
        functools.partial(_s5_kernel, n_seq=n_seq, lt=lt, groups_per_pass=groups_per_pass, unroll=unroll),
        grid=(rows_b // blk_rows,),
        in_specs=[io_spec, full(s0_re), full(s0_im), full(ab_re), full(ab_im), full(bb_re), full(bb_im),
                  full(cc_re), full(cc_im), full(d_row), full(wglu), full(bglu)],
        out_specs=[io_spec, full(s0_re), full(s0_im)],
        out_shape=[jax.ShapeDtypeStruct((nb, rows_b, S5_WIDTH), F32),
                   jax.ShapeDtypeStruct(s0_re.shape, F32), jax.ShapeDtypeStruct(s0_im.shape, F32)],
        scratch_shapes=[x_scratch, x_scratch, h_scratch, h_scratch],
        compiler_params=_params("arbitrary"),
        name="s5",
    )(proj3, s0_re, s0_im, ab_re, ab_im, bb_re, bb_im, cc_re, cc_im, d_row, wglu, bglu)


OD_Q_BLOCK = 1
OD_KV_BLOCK = 4
OD_COLS = 1280
ROPE_HALF = SWA_HD // 8
SWA_GROUP = SWA_HEADS // SWA_KV_HEADS


def _half_lanes(shape):
    return (_iota(shape, 1) & (LANES - 1)) < SWA_HD


def _pair_rms_scale(x):
    same_head = (_iota((LANES, LANES), 0) >= SWA_HD) == (_iota((LANES, LANES), 1) >= SWA_HD)
    sums = _dot_exact_rhs(x * x, _bf(same_head.astype(F32)), terms=2)
    return lax.rsqrt(sums * (1.0 / SWA_HD) + NORM_EPS)


def _swa_qk(xq, xk, qg, kg, cq, sq):
    xq_g = xq * qg
    n_pairs = SWA_HEADS // 2
    q_rot = xq_g * _lane_tile(cq, n_pairs) + _rope_partner(xq_g, SWA_HD, ROPE_HALF) * _lane_tile(sq, n_pairs)
    xk_g = xk * kg
    k_rot = xk_g * cq + _rope_partner(xk_g, SWA_HD, ROPE_HALF) * sq
    q_pairs = [q_rot[:, j * LANES:(j + 1) * LANES] * _pair_rms_scale(xq[:, j * LANES:(j + 1) * LANES])
               for j in range(SWA_HEADS // 2)]
    return q_pairs, k_rot * _pair_rms_scale(xk)


def _swa_query_stack(q_pairs, kh):
    lo = _half_lanes(q_pairs[0].shape)
    keep = lo if kh == 0 else jnp.logical_not(lo)
    parts = []
    for g in range(SWA_GROUP):
        hq = kh * SWA_GROUP + g
        pair = q_pairs[hq // 2]
        src = pair if hq % 2 == kh else pltpu.roll(pair, SWA_HD, 1)
        parts.append(jnp.where(keep, src, 0.0))
    return _bf(jnp.concatenate(parts, axis=0))


def _swa_merge_heads(o, kh, rows):
    lo = _half_lanes((rows, LANES))
    pairs = []
    for p in range(SWA_GROUP // 2):
        even, odd = o[2 * p * rows:(2 * p + 1) * rows], o[(2 * p + 1) * rows:(2 * p + 2) * rows]
        if kh == 0:
            pairs.append(jnp.where(lo, even, pltpu.roll(odd, SWA_HD, 1)))
        else:
            pairs.append(jnp.where(lo, pltpu.roll(even, SWA_HD, 1), odd))
    return pairs


def _swa_block(xq, xk, v_cur, k_prev, v_prev, mask, qg, kg, cq, sq, sink_ref):
    w = SWA_WINDOW
    ones_col = jnp.ones((2 * w, LANES), BF16)
    q_pairs, k_cur = _swa_qk(xq, xk, qg, kg, cq, sq)
    k_ext = _bf(jnp.concatenate([k_prev, k_cur], axis=0))
    v_ext = _bf(jnp.concatenate([v_prev, v_cur], axis=0))
    out_pairs = []
    for kh in range(SWA_KV_HEADS):
        s_all = _dot_nt(_swa_query_stack(q_pairs, kh), k_ext)
        weights, sink_terms = [], []
        for g in range(SWA_GROUP):
            s = jnp.where(mask, s_all[g * w:(g + 1) * w] * (SWA_HD ** -0.5), -jnp.inf)
            sink = sink_ref[kh * SWA_GROUP + g:kh * SWA_GROUP + g + 1, :]
            m = jnp.maximum(jnp.broadcast_to(jnp.max(s, axis=-1, keepdims=True), (w, LANES)), sink)
            weights.append(_bf(jnp.exp(s - jnp.concatenate([m, m], axis=1))))
            sink_terms.append(jnp.exp(sink - m))
        weights = jnp.concatenate(weights, axis=0)
        den = _dot(weights, ones_col) + jnp.concatenate(sink_terms, axis=0)
        out_pairs += _swa_merge_heads(_dot(weights, v_ext) / den, kh, w)
    return out_pairs, k_cur


def _odd_prompt_kernel(x_ref, gain_ref, win_ref, wout_ref, s0re_ref, s0im_ref, abre_ref, abim_ref, bbre_ref, bbim_ref,
                       ccre_ref, ccim_ref, d_ref, wglu_ref, bglu_ref, cq_ref, sq_ref, qg_ref, kg_ref, sink_ref,
                       o_ref, sre_ref, sim_ref, ck_ref, cv_ref,
                       proj_sc, mix_sc, xr_ref, xi_ref, hre_ref, him_ref, kprev_sc, vprev_sc,
                       *, n_seq, groups_per_pass, unroll):
    w = SWA_WINDOW
    rows = n_seq * w
    n_groups = _s5_layout(n_seq, w)[1]
    step = pl.program_id(0)

    @pl.when(step == 0)
    def _():
        _s5_load_state(s0re_ref, s0im_ref, hre_ref, him_ref, n_seq, n_groups)
        kprev_sc[...] = jnp.zeros_like(kprev_sc)
        vprev_sc[...] = jnp.zeros_like(vprev_sc)

    x = x_ref[...].reshape(rows, D_MODEL)
    proj_sc[...] = _dot(_bf(_rms(x, gain_ref[...])), win_ref[...])
    mix_sc[:, 0:S5_WIDTH] = _s5_core(proj_sc[:, 0:S5_WIDTH], abre_ref, abim_ref, bbre_ref, bbim_ref, ccre_ref, ccim_ref,
                                     d_ref, wglu_ref, bglu_ref, xr_ref, xi_ref, hre_ref, him_ref,
                                     n_seq=n_seq, lt=w, groups_per_pass=groups_per_pass, unroll=unroll)

    t_idx = _iota((w, 2 * w), 0)
    s_idx = _iota((w, 2 * w), 1)
    mask = (s_idx > t_idx) & (s_idx <= t_idx + w) & (s_idx >= jnp.where(step == 0, w, 0))
    q0, k0, v0 = OD_Q_BLOCK * 512, OD_KV_BLOCK * 256, OD_KV_BLOCK * 256 + LANES
    for s in range(n_seq):
        rs = slice(s * w, (s + 1) * w)
        v_cur = proj_sc[rs, v0:v0 + LANES]
        pairs, k_cur = _swa_block(proj_sc[rs, q0:q0 + 512], proj_sc[rs, k0:k0 + LANES], v_cur, kprev_sc[s], vprev_sc[s],
                                  mask, qg_ref[...], kg_ref[...], cq_ref[...], sq_ref[...], sink_ref)
        for i, pair in enumerate(pairs):
            mix_sc[rs, S5_WIDTH + i * LANES:S5_WIDTH + (i + 1) * LANES] = pair
        kprev_sc[s] = k_cur
        vprev_sc[s] = v_cur
    o_ref[...] = (x + _dot(_bf(mix_sc[...]), wout_ref[...])).reshape(o_ref.shape)

    @pl.when(step == pl.num_programs(0) - 1)
    def _():
        _s5_store_state(sre_ref, sim_ref, hre_ref, him_ref, n_seq, n_groups)
        ck_ref[...] = kprev_sc[...]
        cv_ref[...] = vprev_sc[...]


def _odd_prompt(x, gain, w_in, w_out, s0_re, s0_im, ab_re, ab_im, bb_re, bb_im, cc_re, cc_im, d_row, wglu, bglu,
                cq, sq, qg, kg, sink_rows, *, groups_per_pass, unroll):
    b, l, d = x.shape
    w = SWA_WINDOW
    pack, n_groups, pitch = _s5_layout(b, w)
    ab_re, ab_im = _s5_tables(ab_re, b, w), _s5_tables(ab_im, b, w)
    full = lambda a: pl.BlockSpec(a.shape, lambda j: (0,) * a.ndim)
    io_spec = pl.BlockSpec((b, w, d), lambda j: (0, j, 0))
    tab_spec = pl.BlockSpec((w, LANES), lambda j: (j, 0))
    cache_shape = jax.ShapeDtypeStruct((b, w, LANES), F32)
    x_scratch = pltpu.VMEM((n_groups, pack * b * pitch, LANES), F32)
    h_scratch = pltpu.VMEM((n_groups, pack * b, LANES), F32)
    kv_scratch = pltpu.VMEM((b, w, LANES), F32)
    return pl.pallas_call(
        functools.partial(_odd_prompt_kernel, n_seq=b, groups_per_pass=groups_per_pass, unroll=unroll),
        grid=(l // w,),
        in_specs=[io_spec, full(gain), full(w_in), full(w_out), full(s0_re), full(s0_im), full(ab_re), full(ab_im),
                  full(bb_re), full(bb_im), full(cc_re), full(cc_im), full(d_row), full(wglu), full(bglu),
                  tab_spec, tab_spec, full(qg), full(kg), full(sink_rows)],
        out_specs=[io_spec, full(s0_re), full(s0_im), pl.BlockSpec((b, w, LANES), lambda j: (0, 0, 0)),
                   pl.BlockSpec((b, w, LANES), lambda j: (0, 0, 0))],
        out_shape=[jax.ShapeDtypeStruct(x.shape, F32), jax.ShapeDtypeStruct(s0_re.shape, F32),
                   jax.ShapeDtypeStruct(s0_im.shape, F32), cache_shape, cache_shape],
        scratch_shapes=[pltpu.VMEM((b * w, OD_COLS), F32), pltpu.VMEM((b * w, d), F32),
                        x_scratch, x_scratch, h_scratch, h_scratch, kv_scratch, kv_scratch],
        compiler_params=_params("arbitrary"),
        name="odd_prompt",
    )(x, gain, w_in, w_out, s0_re, s0_im, ab_re, ab_im, bb_re, bb_im, cc_re, cc_im, d_row, wglu, bglu,
      cq, sq, qg, kg, sink_rows)


def _swa_sample_kernel(q_ref, kv_ref, ck_ref, cv_ref, cq_ref, sq_ref, qg_ref, kg_ref, sink_ref,
                       o_ref, nk_ref, nv_ref, *, bg, ls):
    w = SWA_WINDOW
    r = bg * ls
    ls_shift = ls.bit_length() - 1
    w_shift = w.bit_length() - 1
    rows_g = SWA_GROUP * r
    v_new = kv_ref[:, LANES:2 * LANES]
    q_pairs, k_new = _swa_qk(q_ref[...], kv_ref[:, 0:LANES], qg_ref[...], kg_ref[...], cq_ref[...], sq_ref[...])
    k_cache = _bf(ck_ref[...].reshape(bg * w, LANES))
    v_cache = _bf(cv_ref[...].reshape(bg * w, LANES))

    row = _iota((rows_g, bg * w), 0) & (r - 1)
    col = _iota((rows_g, bg * w), 1)
    mask_c = ((row >> ls_shift) == (col >> w_shift)) & ((col & (w - 1)) > (row & (ls - 1)))
    row_n = _iota((rows_g, r), 0) & (r - 1)
    col_n = _iota((rows_g, r), 1)
    mask_n = ((row_n >> ls_shift) == (col_n >> ls_shift)) & ((col_n & (ls - 1)) <= (row_n & (ls - 1)))
    for kh in range(SWA_KV_HEADS):
        q_stack = _swa_query_stack(q_pairs, kh)
        s_c = jnp.where(mask_c, _dot_nt(q_stack, k_cache) * (SWA_HD ** -0.5), -jnp.inf)
        s_n = jnp.where(mask_n, _dot_nt(q_stack, _bf(k_new)) * (SWA_HD ** -0.5), -jnp.inf)
        sink = jnp.concatenate([jnp.broadcast_to(sink_ref[kh * SWA_GROUP + g:kh * SWA_GROUP + g + 1, :], (r, LANES))
                                for g in range(SWA_GROUP)], axis=0)
        row_max = jnp.maximum(jnp.max(s_c, axis=-1, keepdims=True), jnp.max(s_n, axis=-1, keepdims=True))
        m = jnp.maximum(jnp.broadcast_to(row_max, (rows_g, LANES)), sink)
        e_c = _bf(jnp.exp(s_c - jnp.concatenate([m] * bg, axis=1)))
        e_n = _bf(jnp.exp(s_n - m[:, :r]))
        den = (_dot(e_c, jnp.ones((bg * w, LANES), BF16)) + _dot(e_n, jnp.ones((r, LANES), BF16))
               + jnp.exp(sink - m))
        o = (_dot(e_c, v_cache) + _dot(e_n, _bf(v_new))) / den
        for i, pair in enumerate(_swa_merge_heads(o, kh, r)):
            col_i = kh * (SWA_GROUP // 2) + i
            o_ref[:, col_i * LANES:(col_i + 1) * LANES] = pair

    nk_ref[:, 0:w - ls, :] = ck_ref[:, ls:w, :]
    nv_ref[:, 0:w - ls, :] = cv_ref[:, ls:w, :]
    for b in range(bg):
        nk_ref[b, w - ls:w, :] = k_new[b * ls:(b + 1) * ls, :]
        nv_ref[b, w - ls:w, :] = v_new[b * ls:(b + 1) * ls, :]


def _swa_sample(proj, cache_k, cache_v, cq, sq, qg, kg, sink_rows, *, bg, ls):
    t = proj.shape[0]
    w = SWA_WINDOW
    r = bg * ls
    full = lambda a: pl.BlockSpec(a.shape, lambda i: (0,) * a.ndim)
    cache_spec = pl.BlockSpec((bg, w, LANES), lambda i: (i, 0, 0))
    return pl.pallas_call(
        functools.partial(_swa_sample_kernel, bg=bg, ls=ls),
        grid=(t // r,),
        in_specs=[
            pl.BlockSpec((r, 512), lambda i: (i, OD_Q_BLOCK)),
            pl.BlockSpec((r, 256), lambda i: (i, OD_KV_BLOCK)),
            cache_spec, cache_spec, full(cq), full(sq), full(qg), full(kg), full(sink_rows),
        ],
        out_specs=[pl.BlockSpec((r, 512), lambda i: (i, 0)), cache_spec, cache_spec],
        out_shape=[jax.ShapeDtypeStruct((t, 512), F32),
                   jax.ShapeDtypeStruct(cache_k.shape, F32), jax.ShapeDtypeStruct(cache_v.shape, F32)],
        compiler_params=_params("parallel"),
        name="swa_sample",
    )(proj, proj, cache_k, cache_v, cq, sq, qg, kg, sink_rows)


def _mem_prompt_kernel(x_ref, g_ref, wq_ref, qg_ref, k_ref, v_ref, wo_ref, o_ref):
    x = x_ref[0]
    xn = _bf(_rms(x, g_ref[...]))
    acc = x
    for h in range(MEM_HEADS):
        sl = slice(h * MEM_HD, (h + 1) * MEM_HD)
        q = _bf(_rms(_dot(xn, wq_ref[:, sl]), qg_ref[...]))
        s = _dot_nt(q, _bf(k_ref[0, :, sl])) * (MEM_HD ** -0.5)
        e = jnp.exp(s - jnp.max(s, axis=-1, keepdims=True))
        p = e / jnp.sum(e, axis=-1, keepdims=True)
        o = _dot(_bf(p), _bf(v_ref[0, :, sl]))
        acc = acc + _dot(_bf(o), wo_ref[sl, :])
    o_ref[0] = acc


def _mem_prompt(x, gain, wq, q_gain, k, v, wo, *, lt):
    b, l, d = x.shape
    n_mem = k.shape[1]
    kv_spec = pl.BlockSpec((1, n_mem, d), lambda i, j: (i, 0, 0))
    io_spec = pl.BlockSpec((1, lt, d), lambda i, j: (i, j, 0))
    full = lambda a: pl.BlockSpec(a.shape, lambda i, j: (0,) * a.ndim)
    return pl.pallas_call(
        _mem_prompt_kernel,
        grid=(b, l // lt),
        in_specs=[io_spec, full(gain), full(wq), full(q_gain), kv_spec, kv_spec, full(wo)],
        out_specs=io_spec,
        out_shape=jax.ShapeDtypeStruct(x.shape, F32),
        compiler_params=_params("parallel", "arbitrary"),
        name="mem_prompt",
    )(x, gain, wq, q_gain, k, v, wo)


def _mem_sample_kernel(q_ref, k_ref, v_ref, o_ref, *, bs, ls):
    r = bs * ls
    per_seq = MEM_HEADS * ls
    n_exp = bs * per_seq
    n_mem = k_ref.shape[1] // MEM_ROW_GROUP
    hd_shift = MEM_HD.bit_length() - 1
    ls_shift = ls.bit_length() - 1
    seq_shift = per_seq.bit_length() - 1
    mem_shift = n_mem.bit_length() - 1
    qb = _bf(q_ref[...])
    e_row = _iota((n_exp, r), 0)
    sel = _bf((((e_row >> seq_shift) << ls_shift) + (e_row & (ls - 1)) == _iota((n_exp, r), 1)).astype(F32))
    head_mask = (((_iota((n_exp, D_MODEL), 0) >> ls_shift) & (MEM_HEADS - 1))
                 == (_iota((n_exp, D_MODEL), 1) >> hd_shift))
    q_exp = _bf(jnp.where(head_mask, _dot(sel, qb), 0.0))
    k_all = _bf(jnp.concatenate([_mem_rows(k_ref, b) for b in range(bs)], axis=0))
    v_all = _bf(jnp.concatenate([_mem_rows(v_ref, b) for b in range(bs)], axis=0))
    own = ((_iota((bs * n_mem, n_exp), 0) >> mem_shift) == (_iota((bs * n_mem, n_exp), 1) >> seq_shift))
    own = own.reshape(bs, n_mem, n_exp)
    s = (_dot_nt(k_all, q_exp) * (MEM_HD ** -0.5)).reshape(bs, n_mem, n_exp)
    s = jnp.where(own, s, -1e30)
    e = jnp.where(own, jnp.exp(s - jnp.max(s, axis=1, keepdims=True)), 0.0)
    den = jnp.sum(e, axis=1, keepdims=True) + jnp.where(jnp.any(own, axis=1, keepdims=True), 0.0, 1.0)
    p = _bf((e / den).reshape(bs * n_mem, n_exp))
    o_all = jnp.where(head_mask, _dot_tn(p, v_all), 0.0)
    o_ref[...] = _dot_tn(sel, _bf(o_all))


MEM_LANE_TILES = MEM_HD // LANES
MEM_ROW_GROUP = MEM_HEADS * MEM_LANE_TILES


def _mem_rows(ref, b):
    n_mem = ref.shape[1] // MEM_ROW_GROUP
    return jnp.concatenate([ref[b, pl.ds(lt * MEM_HEADS + h, n_mem, stride=MEM_ROW_GROUP), :]
                            for h in range(MEM_HEADS) for lt in range(MEM_LANE_TILES)], axis=1)


def _mem_flat_view(a):
    depth, nb, n_mem, heads, hd = a.shape
    a = a.reshape(depth, nb, n_mem, heads, hd // LANES, LANES).transpose(0, 1, 2, 4, 3, 5)
    return a.reshape(depth, nb, n_mem * MEM_ROW_GROUP, LANES)


def _mem_sample(q, k, v, *, layer, bs, ls):
    t, d = q.shape
    k, v = _mem_flat_view(k), _mem_flat_view(v)
    r = bs * ls
    kv_spec = pl.BlockSpec((None, bs, k.shape[2], LANES), lambda i: (layer, i, 0, 0))
    io_spec = pl.BlockSpec((r, d), lambda i: (i, 0))
    return pl.pallas_call(
        functools.partial(_mem_sample_kernel, bs=bs, ls=ls),
        grid=(t // r,),
        in_specs=[io_spec, kv_spec, kv_spec],
        out_specs=io_spec,
        out_shape=jax.ShapeDtypeStruct(q.shape, F32),
        compiler_params=_params("parallel"),
        name="mem_sample",
    )(q, k, v)


def _retention_tables(pos):
    inv = 1.0 / (RET_THETA ** jnp.linspace(0.0, 1.0, RET_DK // 2, dtype=F32))
    ang = pos[:, None] * inv[None, :]
    cos = jnp.repeat(jnp.cos(ang), 2, axis=1)
    sin = jnp.stack([-jnp.sin(ang), jnp.sin(ang)], axis=-1).reshape(pos.shape[0], RET_DK)
    return jnp.tile(cos, (1, PAIR)), jnp.tile(sin, (1, PAIR))


def _rope_tables(pos):
    half = ROPE_HALF
    inv = 1.0 / (ROPE_THETA ** (jnp.arange(half, dtype=F32) * 2.0 / (2 * half)))
    ang = pos[:, None] * inv[None, :]
    n = pos.shape[0]
    rest = SWA_HD - 2 * half
    cos = jnp.concatenate([jnp.cos(ang), jnp.cos(ang), jnp.ones((n, rest), F32)], axis=1)
    sin = jnp.concatenate([-jnp.sin(ang), jnp.sin(ang), jnp.zeros((n, rest), F32)], axis=1)
    return jnp.tile(cos, (1, PAIR)), jnp.tile(sin, (1, PAIR))


def _block_diag(t):
    g, a, b = t.shape
    eye = jnp.eye(g, dtype=t.dtype)
    return (t[:, :, None, :] * eye[:, None, :, None]).reshape(g * a, g * b)


def _half_block_diag(t):
    per = t.shape[0] // S5_HALVES
    return _bf(jnp.stack([_block_diag(t[h * per:(h + 1) * per]) for h in range(S5_HALVES)]))


def _sink_rows(sinks):
    return jnp.broadcast_to(sinks.astype(F32)[:, None], (sinks.shape[0], LANES))


def _trunk(x3, pos0, states, mem_k, mem_v, w, *, sample):
    b, l, d = x3.shape
    t = b * l
    x = x3.reshape(t, d)
    pos = pos0 + jnp.arange(l, dtype=F32)
    tm = 512 if sample else 1024
    gla_s, ret_s, s5_re, s5_im, swa_k, swa_v = states
    out_states = {k: [] for k in ("gla", "ret", "s5_re", "s5_im", "swa_k", "swa_v")}
    ld_row = jnp.repeat(jnp.log(1.0 - 2.0 ** (-5.0 - jnp.arange(RET_HEADS, dtype=F32))), RET_DK)[None, :]

    for layer in range(2):
        i = layer // 2
        x = _ffn(x, w['ffn1_norm'][layer][None], w['ffn1_w_gate'], w['ffn1_w_up'], w['ffn1_w_down'],
                 layer=layer, tm=tm, th=256)
        if layer % 2 == 0:
            cos, sin = _retention_tables(pos)
            args = (ld_row, w['gla_w_gate'][i], w['gla_b_gate'][i], w['gla_out_norm'][i])
            if sample:
                bg = 16
                proj = _norm_matmul(x, w['mix_norm'][layer][None], w['even_w_in'][i], tm=tm, tn=640)
                mixed, g_s, r_s = _even_sample(proj, jnp.tile(cos, (bg, 1)), jnp.tile(sin, (bg, 1)), *args,
                                               gla_s[i], ret_s[i], bg=bg, ls=l)
                x = _matmul_residual(x, [(mixed, w['even_w_out'][i])], tm=tm)
            else:
                x, g_s, r_s = _even_prompt(x.reshape(b, l, d), w['mix_norm'][layer][None], w['even_w_in'][i],
                                           w['even_w_out'][i], cos, sin, *args, gla_s[i], ret_s[i], lt=512)
                x = x.reshape(t, d)
            out_states["gla"].append(g_s)
            out_states["ret"].append(r_s)
        else:
            cq, sq = _rope_tables(pos)
            qg = jnp.tile(w['swa_q_norm'][i], SWA_HEADS)[None, :]
            kg = jnp.tile(w['swa_k_norm'][i], SWA_KV_HEADS)[None, :]
            s5_args = (w['s5_ab_re'][i], w['s5_ab_im'][i], w['s5_bb_re'][i], w['s5_bb_im'][i],
                       w['s5_cc_re'][i], w['s5_cc_im'][i], w['s5_d'][i][None], w['s5_w_glu'][i], w['s5_b_glu'][i][None])
            n_state = S5_GROUPS * S5_STATE
            s0_re, s0_im = s5_re[i].reshape(b, n_state), s5_im[i].reshape(b, n_state)
            sinks = _sink_rows(w['swa_sinks'][i])
            if sample:
                proj = _norm_matmul(x, w['mix_norm'][layer][None], w['odd_w_in'][i], tm=tm, tn=640)
                c_out, sr, si = _s5(proj.reshape(1, t, OD_COLS), s0_re, s0_im, *s5_args, n_seq=b, lt=l,
                                    groups_per_pass=1, unroll=True)
                bg = 8
                d_out, kb, vb = _swa_sample(proj, swa_k[i].reshape(b, SWA_WINDOW, LANES),
                                            swa_v[i].reshape(b, SWA_WINDOW, LANES),
                                            jnp.tile(cq, (bg, 1)), jnp.tile(sq, (bg, 1)), qg, kg, sinks, bg=bg, ls=l)
                w_out = w['odd_w_out'][i]
                x = _matmul_residual(x, [(c_out.reshape(t, S5_WIDTH), w_out[:S5_WIDTH]), (d_out, w_out[S5_WIDTH:])],
                                     tm=tm)
            else:
                x, sr, si, kb, vb = _odd_prompt(x.reshape(b, l, d), w['mix_norm'][layer][None], w['odd_w_in'][i],
                                                w['odd_w_out'][i], s0_re, s0_im, *s5_args, cq, sq, qg, kg, sinks,
                                                groups_per_pass=8, unroll=4)
                x = x.reshape(t, d)
            out_states["s5_re"].append(sr.reshape(b, S5_GROUPS, S5_STATE))
            out_states["s5_im"].append(si.reshape(b, S5_GROUPS, S5_STATE))
            out_states["swa_k"].append(kb.reshape(b, -1, SWA_KV_HEADS, SWA_HD))
            out_states["swa_v"].append(vb.reshape(b, -1, SWA_KV_HEADS, SWA_HD))
        if sample:
            q = _norm_matmul(x, w['mem_x_norm'][layer][None], w['mem_w_q'][layer], tm=tm, tn=MEM_HD,
                             head_gain=w['mem_q_norm'][layer][None], n_norm_tiles=MEM_HEADS)
            o = _mem_sample(q, mem_k, mem_v, layer=layer, bs=4, ls=l)
            x = _matmul_residual(x, [(o, w['mem_w_o'][layer])], tm=tm)
        else:
            x = _mem_prompt(x.reshape(b, l, d), w['mem_x_norm'][layer][None], w['mem_w_q'][layer],
                            w['mem_q_norm'][layer][None], mem_k[layer], mem_v[layer], w['mem_w_o'][layer],
                            lt=1024).reshape(t, d)
        x = _ffn(x, w['ffn2_norm'][layer][None], w['ffn2_w_gate'], w['ffn2_w_up'], w['ffn2_w_down'],
                 layer=layer, tm=tm, th=256)
    return x.reshape(b, l, d), {k: jnp.stack(v) for k, v in out_states.items()}


def kernel(x_prompt, x_sample, mem_prompt, state_gla, state_ret, state_s5_re, state_s5_im, cache_swa_k, cache_swa_v, cache_mem_k, cache_mem_v, ffn1_norm, ffn1_w_gate, ffn1_w_up, ffn1_w_down, ffn2_norm, ffn2_w_gate, ffn2_w_up, ffn2_w_down, mix_norm, even_w_in, gla_w_gate, gla_b_gate, gla_out_norm, even_w_out, odd_w_in, s5_a_re, s5_a_im, s5_log_step, s5_b_re, s5_b_im, s5_c_re, s5_c_im, s5_d, s5_w_glu, s5_b_glu, swa_q_norm, swa_k_norm, swa_sinks, odd_w_out, mem_x_norm, mem_m_norm, mem_w_q, mem_w_k, mem_w_v, mem_w_o, mem_q_norm, mem_k_norm):
    depth = ffn1_norm.shape[0]
    n_even, n_odd = even_w_in.shape[0], odd_w_in.shape[0]
    batch, seq, d = x_prompt.shape
    dec_batch = x_sample.shape[0]
    n_mem = mem_prompt.shape[1]

    ev = even_w_in
    ev_cols = jnp.concatenate(
        [ev[..., 0:1536], ev[..., 1552:3088], ev[..., 1536:1552],
         jnp.zeros(ev.shape[:2] + (EV_COLS - 3088,), ev.dtype)], axis=-1)
    wgate_pad = jnp.concatenate(
        [gla_w_gate, jnp.zeros((n_even, EV_COLS - EV_GA - GLA_RANK, gla_w_gate.shape[-1]), gla_w_gate.dtype)], axis=1)
    w = dict(
        ffn1_norm=ffn1_norm, ffn2_norm=ffn2_norm, mix_norm=mix_norm, mem_x_norm=mem_x_norm,
        ffn1_w_gate=_bf(ffn1_w_gate), ffn1_w_up=_bf(ffn1_w_up), ffn1_w_down=_bf(ffn1_w_down),
        ffn2_w_gate=_bf(ffn2_w_gate), ffn2_w_up=_bf(ffn2_w_up), ffn2_w_down=_bf(ffn2_w_down),
        even_w_in=_bf(ev_cols), gla_w_gate=_bf(wgate_pad), gla_b_gate=gla_b_gate[:, None, :],
        gla_out_norm=gla_out_norm[:, None, :], even_w_out=_bf(even_w_out),
        odd_w_in=_bf(odd_w_in), odd_w_out=_bf(odd_w_out), s5_d=s5_d, s5_w_glu=_bf(s5_w_glu), s5_b_glu=s5_b_glu,
        swa_q_norm=swa_q_norm, swa_k_norm=swa_k_norm, swa_sinks=swa_sinks,
        mem_w_q=_bf(mem_w_q), mem_w_o=_bf(mem_w_o), mem_q_norm=mem_q_norm,
    )
    ab_re, ab_im, bb_re, bb_im = [], [], [], []
    for i in range(n_odd):
        a_r, a_i, b_r, b_i = _s5_prep(s5_a_re[i], s5_a_im[i], s5_log_step[i], s5_b_re[i], s5_b_im[i])
        ab_re.append(a_r.reshape(-1))
        ab_im.append(a_i.reshape(-1))
        bb_re.append(_half_block_diag(b_r))
        bb_im.append(_half_block_diag(b_i))
    w.update(s5_ab_re=ab_re, s5_ab_im=ab_im, s5_bb_re=bb_re, s5_bb_im=bb_im,
             s5_cc_re=[_half_block_diag(jnp.swapaxes(s5_c_re[i], 1, 2)) for i in range(n_odd)],
             s5_cc_im=[_half_block_diag(jnp.swapaxes(s5_c_im[i], 1, 2)) for i in range(n_odd)])

    mem2 = mem_prompt.reshape(batch * n_mem, d)
    p_mem_k, p_mem_v = [], []
    for layer in range(depth):
        w_kv = _bf(jnp.concatenate([mem_w_k[layer], mem_w_v[layer]], axis=1))
        kv = _norm_matmul(mem2, mem_m_norm[layer][None], w_kv, tm=batch * n_mem, tn=MEM_HD,
                          head_gain=mem_k_norm[layer][None], n_norm_tiles=MEM_HEADS)
        p_mem_k.append(kv[:, :d].reshape(batch, n_mem, d))
        p_mem_v.append(kv[:, d:].reshape(batch, n_mem, d))

    zeros = lambda *s: jnp.zeros(s, F32)
    p_states = (zeros(n_even, batch, GLA_HEADS, GLA_DK, GLA_DV), zeros(n_even, batch, RET_HEADS, RET_DK, GLA_DV),
                zeros(n_odd, batch, S5_GROUPS, S5_STATE), zeros(n_odd, batch, S5_GROUPS, S5_STATE), None, None)
    y_prompt, ps = _trunk(x_prompt, 0.0, p_states, p_mem_k, p_mem_v, w, sample=False)

    s_states = (state_gla, state_ret, state_s5_re, state_s5_im, cache_swa_k, cache_swa_v)
    y_sample, ss = _trunk(x_sample, float(PAST_LEN), s_states, cache_mem_k, cache_mem_v, w, sample=True)

    p_mem_k = jnp.stack(p_mem_k).reshape(depth, batch, n_mem, MEM_HEADS, MEM_HD)
    p_mem_v = jnp.stack(p_mem_v).reshape(depth, batch, n_mem, MEM_HEADS, MEM_HD)
    return (y_prompt, y_sample, ps["gla"], ps["ret"], ps["s5_re"], ps["s5_im"], ps["swa_k"], ps["swa_v"],
            p_mem_k, p_mem_v, ss["gla"], ss["ret"], ss["s5_re"], ss["s5_im"], ss["swa_k"], ss["swa_v"])
```

```python
import functools
import math

import jax
import jax.numpy as jnp
import numpy as np
from jax import lax
from jax.experimental import pallas as pl
from jax.experimental.pallas import tpu as pltpu

F32 = jnp.float32
BF16 = jnp.bfloat16
NORM_EPS = 1e-6
HIGHEST = lax.Precision.HIGHEST

D_MODEL = 1024
GLA_HEADS = 4
GLA_DK = 64
GLA_DV = 128
GLA_RANK = 16
GLA_TAU = 16.0
RET_HEADS = 4
RET_DK = 64
RET_THETA = 10000.0
LA_CHUNK = 64
S5_WIDTH = 512
S5_GROUP = 16
S5_GROUPS = 32
S5_STATE = 64
SWA_HD = 64
SWA_HEADS = 8
SWA_KV_HEADS = 2
SWA_WINDOW = 128
ROPE_THETA = 500000.0
MEM_HEADS = 4
MEM_HD = 256
PAST_LEN = 8192

VMEM_LIMIT_BYTES = 52 * 1024 * 1024
LANES = 128
SUBLANES = 8


def _params(*sem):
    return pltpu.CompilerParams(dimension_semantics=sem, vmem_limit_bytes=VMEM_LIMIT_BYTES)


def _rms(x, gain=None):
    y = x * lax.rsqrt(jnp.mean(x * x, axis=-1, keepdims=True) + NORM_EPS)
    return y if gain is None else y * gain


def _dot(a, b):
    return jnp.dot(a, b, preferred_element_type=F32)


def _dot_nt(a, b):
    return lax.dot_general(a, b, (((1,), (1,)), ((), ())), preferred_element_type=F32)


def _dot_tn(a, b):
    return lax.dot_general(a, b, (((0,), (0,)), ((), ())), preferred_element_type=F32)


def _dot_f32(a, b):
    return jnp.dot(a, b, precision=HIGHEST, preferred_element_type=F32)


def _split_bf16(x, terms):
    pieces = []
    for _ in range(terms):
        piece = _bf(x)
        pieces.append(piece)
        x = x - piece.astype(F32)
    return pieces


def _dot_exact_lhs(a, x, terms=3):
    return sum(_dot(a, piece) for piece in _split_bf16(x, terms))


def _dot_exact_rhs(x, b, terms=3):
    return sum(_dot(piece, b) for piece in _split_bf16(x, terms))


def _dot_nt_f32(a, b):
    return lax.dot_general(a, b, (((1,), (1,)), ((), ())), precision=HIGHEST, preferred_element_type=F32)


def _bf(x):
    return x.astype(BF16)


def _log_sigmoid(x):
    return jnp.minimum(x, 0.0) - jnp.log1p(jnp.exp(-jnp.abs(x)))


def _iota(shape, dim):
    return lax.broadcasted_iota(jnp.int32, shape, dim)


def _lane_tile(x, n):
    return jnp.concatenate([x] * n, axis=1)


def _swap_pairs(x):
    n = x.shape[-1]
    even = (_iota(x.shape, 1) & 1) == 0
    return jnp.where(even, pltpu.roll(x, n - 1, 1), pltpu.roll(x, 1, 1))


def _rope_partner(x, head_dim, half):
    n = x.shape[-1]
    first = (_iota(x.shape, 1) & (head_dim - 1)) < half
    return jnp.where(first, pltpu.roll(x, n - half, 1), pltpu.roll(x, half, 1))


def _ffn_kernel(x_ref, g_ref, wg_ref, wu_ref, wd_ref, o_ref, xn_ref):
    @pl.when(pl.program_id(1) == 0)
    def _():
        x = x_ref[...]
        xn_ref[...] = _bf(_rms(x, g_ref[...]))
        o_ref[...] = x

    xn = xn_ref[...]
    gate = _dot(xn, wg_ref[...])
    up = _dot(xn, wu_ref[...])
    o_ref[...] += _dot(_bf(jax.nn.silu(gate) * up * 0.5), wd_ref[...])


def _ffn(x, gain, wg, wu, wd, *, layer, tm, th):
    t, d = x.shape
    h = wg.shape[2]
    return pl.pallas_call(
        _ffn_kernel,
        grid=(t // tm, h // th),
        in_specs=[
            pl.BlockSpec((tm, d), lambda i, j: (i, 0)),
            pl.BlockSpec((1, d), lambda i, j: (0, 0)),
            pl.BlockSpec((None, d, th), lambda i, j: (layer, 0, j)),
            pl.BlockSpec((None, d, th), lambda i, j: (layer, 0, j)),
            pl.BlockSpec((None, th, d), lambda i, j: (layer, j, 0)),
        ],
        out_specs=pl.BlockSpec((tm, d), lambda i, j: (i, 0)),
        out_shape=jax.ShapeDtypeStruct((t, d), F32),
        scratch_shapes=[pltpu.VMEM((tm, d), BF16)],
        compiler_params=_params("parallel", "arbitrary"),
        name="ffn",
    )(x, gain, wg, wu, wd)


def _nmm_kernel(x_ref, g_ref, w_ref, hg_ref, o_ref, xn_ref, *, n_norm_tiles):
    j = pl.program_id(1)

    @pl.when(j == 0)
    def _():
        xn_ref[...] = _bf(_rms(x_ref[...], g_ref[...]))

    y = _dot(xn_ref[...], w_ref[...])
    if n_norm_tiles == 0:
        o_ref[...] = y
    else:
        @pl.when(j < n_norm_tiles)
        def _():
            o_ref[...] = _rms(y, hg_ref[...])

        @pl.when(j >= n_norm_tiles)
        def _():
            o_ref[...] = y


def _norm_matmul(x, gain, w, *, tm, tn, head_gain=None, n_norm_tiles=0):
    t, d = x.shape
    n = w.shape[1]
    if head_gain is None:
        head_gain = jnp.ones((1, tn), F32)
    return pl.pallas_call(
        functools.partial(_nmm_kernel, n_norm_tiles=n_norm_tiles),
        grid=(t // tm, n // tn),
        in_specs=[
            pl.BlockSpec((tm, d), lambda i, j: (i, 0)),
            pl.BlockSpec((1, d), lambda i, j: (0, 0)),
            pl.BlockSpec((d, tn), lambda i, j: (0, j)),
            pl.BlockSpec((1, tn), lambda i, j: (0, 0)),
        ],
        out_specs=pl.BlockSpec((tm, tn), lambda i, j: (i, j)),
        out_shape=jax.ShapeDtypeStruct((t, n), F32),
        scratch_shapes=[pltpu.VMEM((tm, d), BF16)],
        compiler_params=_params("parallel", "arbitrary"),
        name="norm_matmul",
    )(x, gain, w, head_gain)


def _mmr_kernel(*refs, n_terms):
    x_ref = refs[0]
    a_refs = refs[1:1 + n_terms]
    w_refs = refs[1 + n_terms:1 + 2 * n_terms]
    o_ref = refs[1 + 2 * n_terms]
    acc = x_ref[...]
    for a_ref, w_ref in zip(a_refs, w_refs):
        acc = acc + _dot(_bf(a_ref[...]), w_ref[...])
    o_ref[...] = acc


def _matmul_residual(x, terms, *, tm):
    t, d = x.shape
    acts = [a for a, _ in terms]
    ws = [w for _, w in terms]
    in_specs = [pl.BlockSpec((tm, d), lambda i: (i, 0))]
    in_specs += [pl.BlockSpec((tm, a.shape[1]), lambda i: (i, 0)) for a in acts]
    in_specs += [pl.BlockSpec(w.shape, lambda i: (0, 0)) for w in ws]
    return pl.pallas_call(
        functools.partial(_mmr_kernel, n_terms=len(terms)),
        grid=(t // tm,),
        in_specs=in_specs,
        out_specs=pl.BlockSpec((tm, d), lambda i: (i, 0)),
        out_shape=jax.ShapeDtypeStruct((t, d), F32),
        compiler_params=_params("parallel"),
        name="matmul_residual",
    )(x, *acts, *ws)


EV_GQ, EV_GK, EV_GV, EV_GG = 0, 256, 512, 1024
EV_RQ, EV_RK, EV_RV, EV_RG = 1536, 1792, 2048, 2560
EV_GA = 3072
EV_COLS = 3200
EV_BLOCK = 256
PAIR = 2


def _gate_and_norm(o, gate, gain=None):
    return _rms(o, gain) * jax.nn.silu(gate)


def _even_prompt_kernel(x_ref, gain_ref, win_ref, wout_ref, cos_ref, sin_ref, ld_ref, wgate_ref, bgate_ref, gnorm_ref,
                        s0g_ref, s0r_ref, o_ref, sg_ref, sr_ref, proj_sc, mix_sc, o_sc, *, chunk, n_chunks):
    @pl.when(pl.program_id(1) == 0)
    def _():
        sg_ref[...] = s0g_ref[...]
        sr_ref[...] = s0r_ref[...]

    c = chunk
    lt = c * n_chunks
    blk = min(EV_BLOCK, lt)
    c_shift = c.bit_length() - 1
    x = x_ref[0]
    proj_sc[...] = _dot(_bf(_rms(x, gain_ref[...])), win_ref[...])

    def cols(a, b):
        return proj_sc[:, a:b]

    log_a = _log_sigmoid(_dot(_bf(cols(EV_GA, EV_COLS)), wgate_ref[...]) + bgate_ref[...]) * (1.0 / GLA_TAU)
    tril = _bf((_iota((c, c), 1) <= _iota((c, c), 0)).astype(F32))
    cum_parts = [_dot_exact_lhs(tril, log_a[i * c:(i + 1) * c]) for i in range(n_chunks)]
    tots = [p[c - 1:c] for p in cum_parts]
    cum = jnp.concatenate(cum_parts, axis=0)
    tot_b = jnp.concatenate([jnp.broadcast_to(t, (c, 256)) for t in tots], axis=0)
    k = cols(EV_GK, EV_GK + 256)
    gla = (cols(EV_GQ, EV_GQ + 256) * (GLA_DK ** -0.5) * jnp.exp(cum), k * jnp.exp(-cum), k * jnp.exp(tot_b - cum))

    ld = ld_ref[...]
    tpos = ((_iota((lt, 1), 0) & (c - 1)) + 1).astype(F32)
    cum_r = tpos * ld
    tot_r = float(c) * ld
    cos, sin = _lane_tile(cos_ref[...], PAIR), _lane_tile(sin_ref[...], PAIR)
    rq = cols(EV_RQ, EV_RQ + 256)
    rk = cols(EV_RK, EV_RK + 256)
    q_rot = rq * cos + _swap_pairs(rq) * sin
    k_rot = (rk * cos + _swap_pairs(rk) * sin) * (RET_DK ** -0.5)
    ret = (q_rot * jnp.exp(cum_r), k_rot * jnp.exp(-cum_r), k_rot * jnp.exp(tot_r - cum_r))

    tot_rows = jnp.concatenate(tots + [tot_r, jnp.zeros((LANES - n_chunks - 1, 256), F32)], axis=0)
    decay_cols = jnp.exp(jnp.transpose(tot_rows))

    row = _iota((blk, blk), 0)
    col = _iota((blk, blk), 1)
    blk_mask = ((row >> c_shift) == (col >> c_shift)) & (col <= row)
    lo = (_iota((lt, LANES), 1) < GLA_DK)
    gnorm = gnorm_ref[...]

    mixers = ((gla, EV_GV, EV_GG, sg_ref, 0, gnorm, lambda i: i),
              (ret, EV_RV, EV_RG, sr_ref, 512, None, lambda i: n_chunks))
    for m, ((q_dec, k_inv, k_dec), v_col, g_col, s_ref, out_col, gain, decay_col_of) in enumerate(mixers):
        for p in range(GLA_HEADS // PAIR):
            lanes = slice(p * LANES, (p + 1) * LANES)
            q_pair = q_dec[:, lanes]
            q_masked = [_bf(jnp.where(lo, q_pair, 0.0)), _bf(jnp.where(lo, 0.0, q_pair))]
            ki = _bf(k_inv[:, lanes])
            kd = _bf(k_dec[:, lanes])
            v_pair = _bf(cols(v_col + p * PAIR * GLA_DV, v_col + (p + 1) * PAIR * GLA_DV))
            for e in range(PAIR):
                slot = (m * (GLA_HEADS // PAIR) + p) * PAIR + e
                for r0 in range(0, lt, blk):
                    rs = slice(r0, r0 + blk)
                    scores = jnp.where(blk_mask, _dot_nt(q_masked[e][rs], ki[rs]), 0.0)
                    o_sc[slot, rs, :] = _dot(_bf(scores), v_pair[rs, e * GLA_DV:(e + 1) * GLA_DV])
            state = s_ref[0, p * PAIR:(p + 1) * PAIR].reshape(PAIR * GLA_DK, GLA_DV)
            for i in range(n_chunks):
                rs = slice(i * c, (i + 1) * c)
                q_stack = jnp.concatenate([q_masked[0][rs], q_masked[1][rs]], axis=0)
                o_inter = _dot(q_stack, _bf(state))
                kv = _dot_tn(kd[rs], v_pair[rs])
                kv = jnp.concatenate([kv[:GLA_DK, :GLA_DV], kv[GLA_DK:, GLA_DV:]], axis=0)
                ci = decay_col_of(i)
                state = state * decay_cols[p * LANES:(p + 1) * LANES, ci:ci + 1] + kv
                for e in range(PAIR):
                    slot = (m * (GLA_HEADS // PAIR) + p) * PAIR + e
                    o_sc[slot, rs, :] += o_inter[e * c:(e + 1) * c]
            s_ref[0, p * PAIR:(p + 1) * PAIR] = state.reshape(PAIR, GLA_DK, GLA_DV)
            for e in range(PAIR):
                h = p * PAIR + e
                slot = (m * (GLA_HEADS // PAIR) + p) * PAIR + e
                gate = cols(g_col + h * GLA_DV, g_col + (h + 1) * GLA_DV)
                mix_sc[:, out_col + h * GLA_DV:out_col + (h + 1) * GLA_DV] = _gate_and_norm(o_sc[slot], gate, gain)
    o_ref[0] = x + _dot(_bf(mix_sc[...]), wout_ref[...])


def _even_prompt(x, gain, w_in, w_out, cos, sin, ld_row, wgate, bgate, gnorm, s0g, s0r, *, lt):
    b, l, d = x.shape
    chunk = math.gcd(l, LA_CHUNK)
    st_spec = pl.BlockSpec((1, GLA_HEADS, GLA_DK, GLA_DV), lambda i, j: (i, 0, 0, 0))
    io_spec = pl.BlockSpec((1, lt, d), lambda i, j: (i, j, 0))
    full = lambda a: pl.BlockSpec(a.shape, lambda i, j: (0,) * a.ndim)
    return pl.pallas_call(
        functools.partial(_even_prompt_kernel, chunk=chunk, n_chunks=lt // chunk),
        grid=(b, l // lt),
        in_specs=[
            io_spec, full(gain), full(w_in), full(w_out),
            pl.BlockSpec((lt, LANES), lambda i, j: (j, 0)),
            pl.BlockSpec((lt, LANES), lambda i, j: (j, 0)),
            full(ld_row), full(wgate), full(bgate), full(gnorm), st_spec, st_spec,
        ],
        out_specs=[io_spec, st_spec, st_spec],
        out_shape=[jax.ShapeDtypeStruct(x.shape, F32),
                   jax.ShapeDtypeStruct(s0g.shape, F32), jax.ShapeDtypeStruct(s0r.shape, F32)],
        scratch_shapes=[pltpu.VMEM((lt, EV_COLS), F32), pltpu.VMEM((lt, d), F32),
                        pltpu.VMEM((GLA_HEADS + RET_HEADS, lt, GLA_DV), F32)],
        compiler_params=_params("parallel", "arbitrary"),
        name="even_prompt",
    )(x, gain, w_in, w_out, cos, sin, ld_row, wgate, bgate, gnorm, s0g, s0r)


def _even_sample_kernel(proj_ref, cos_ref, sin_ref, ld_ref, wgate_ref, bgate_ref, gnorm_ref, s0g_ref, s0r_ref,
                        mix_ref, sg_ref, sr_ref, *, bg, ls):
    r = bg * ls
    ls_shift = ls.bit_length() - 1
    dk_shift = GLA_DK.bit_length() - 1
    n_exp = bg * GLA_DK
    row_seq = _iota((r, r), 0) >> ls_shift
    col_seq = _iota((r, r), 1) >> ls_shift
    same = row_seq == col_seq
    seg = same & (_iota((r, r), 1) <= _iota((r, r), 0))
    seg_f = seg.astype(F32)
    same_f = same.astype(F32)
    tile_b = _bf(((_iota((GLA_DK, n_exp), 1) & (GLA_DK - 1)) == _iota((GLA_DK, n_exp), 0)).astype(F32))
    tile_t_f = ((_iota((n_exp, GLA_DK), 0) & (GLA_DK - 1)) == _iota((n_exp, GLA_DK), 1)).astype(F32)
    tile_t_b = _bf(tile_t_f)
    q_mask = (_iota((r, n_exp), 0) >> ls_shift) == (_iota((r, n_exp), 1) >> dk_shift)
    k_mask = (_iota((n_exp, r), 0) >> dk_shift) == (_iota((n_exp, r), 1) >> ls_shift)
    tpos = ((_iota((r, 1), 0) & (ls - 1)) + 1).astype(F32)
    ld = ld_ref[...]
    cum_r = tpos * ld
    tot_r = float(ls) * ld
    r_dec = jnp.exp(tot_r)
    gnorm = gnorm_ref[...]

    def mixer(q_dec, k_inv, k_dec, v_col, g_col, s0_ref, s_ref, out_col, decay_of, gain):
        for h in range(GLA_HEADS):
            sl = slice(h * GLA_DK, (h + 1) * GLA_DK)
            v = _bf(proj_ref[:, v_col + h * GLA_DV:v_col + (h + 1) * GLA_DV])
            qd = _bf(q_dec[:, sl])
            scores = jnp.where(seg, _dot_nt(qd, _bf(k_inv[:, sl])), 0.0)
            state = s0_ref[:, h].reshape(n_exp, GLA_DV)
            q_exp = _bf(jnp.where(q_mask, _dot(qd, tile_b), 0.0))
            o = _dot(_bf(scores), v) + _dot(q_exp, _bf(state))
            k_exp = _bf(jnp.where(k_mask, _dot_nt(tile_t_b, _bf(k_dec[:, sl])), 0.0))
            new_state = state * decay_of(h, sl) + _dot(k_exp, v)
            s_ref[:, h] = new_state.reshape(bg, GLA_DK, GLA_DV)
            gate = proj_ref[:, g_col + h * GLA_DV:g_col + (h + 1) * GLA_DV]
            mix_ref[:, out_col + h * GLA_DV:out_col + (h + 1) * GLA_DV] = _gate_and_norm(o, gate, gain)

    log_a = _log_sigmoid(_dot(_bf(proj_ref[:, EV_GA:EV_COLS]), wgate_ref[...]) + bgate_ref[...]) * (1.0 / GLA_TAU)
    cum = _dot_f32(seg_f, log_a)
    tot = _dot_f32(same_f, log_a)
    k = proj_ref[:, EV_GK:EV_GK + 256]

    def gla_decay(h, sl):
        la_exp = jnp.where(k_mask, _dot_nt_f32(tile_t_f, log_a[:, sl]), 0.0)
        return jnp.exp(jnp.sum(la_exp, axis=-1, keepdims=True))

    mixer(proj_ref[:, EV_GQ:EV_GQ + 256] * (GLA_DK ** -0.5) * jnp.exp(cum), k * jnp.exp(-cum), k * jnp.exp(tot - cum),
          EV_GV, EV_GG, s0g_ref, sg_ref, 0, gla_decay, gnorm)

    cos = _lane_tile(cos_ref[...], PAIR)
    sin = _lane_tile(sin_ref[...], PAIR)
    rq = proj_ref[:, EV_RQ:EV_RQ + 256]
    rk = proj_ref[:, EV_RK:EV_RK + 256]
    q_rot = rq * cos + _swap_pairs(rq) * sin
    k_rot = (rk * cos + _swap_pairs(rk) * sin) * (RET_DK ** -0.5)

    def ret_decay(h, sl):
        return r_dec[:, h * RET_DK:h * RET_DK + 1]

    mixer(q_rot * jnp.exp(cum_r), k_rot * jnp.exp(-cum_r), k_rot * jnp.exp(tot_r - cum_r),
          EV_RV, EV_RG, s0r_ref, sr_ref, 512, ret_decay, None)


def _even_sample(proj, cos, sin, ld_row, wgate, bgate, gnorm, s0g, s0r, *, bg, ls):
    t = proj.shape[0]
    n_b = t // ls
    r = bg * ls
    st_spec = pl.BlockSpec((bg, GLA_HEADS, GLA_DK, GLA_DV), lambda i: (i, 0, 0, 0))
    full = lambda a: pl.BlockSpec(a.shape, lambda i: (0,) * a.ndim)
    return pl.pallas_call(
        functools.partial(_even_sample_kernel, bg=bg, ls=ls),
        grid=(n_b // bg,),
        in_specs=[
            pl.BlockSpec((r, EV_COLS), lambda i: (i, 0)),
            full(cos), full(sin), full(ld_row), full(wgate), full(bgate), full(gnorm), st_spec, st_spec,
        ],
        out_specs=[pl.BlockSpec((r, D_MODEL), lambda i: (i, 0)), st_spec, st_spec],
        out_shape=[jax.ShapeDtypeStruct((t, D_MODEL), F32),
                   jax.ShapeDtypeStruct(s0g.shape, F32), jax.ShapeDtypeStruct(s0r.shape, F32)],
        compiler_params=_params("parallel"),
        name="even_sample",
    )(proj, cos, sin, ld_row, wgate, bgate, gnorm, s0g, s0r)


def _s5_prep_kernel(are_ref, aim_ref, lstep_ref, bre_ref, bim_ref, abre_ref, abim_ref, bbre_ref, bbim_ref):
    a_re, a_im = are_ref[...], aim_ref[...]
    step = jnp.exp(lstep_ref[...])
    mag = jnp.exp(a_re * step)
    ab_re = mag * jnp.cos(a_im * step)
    ab_im = mag * jnp.sin(a_im * step)
    den = a_re * a_re + a_im * a_im
    coef_re = ((ab_re - 1.0) * a_re + ab_im * a_im) / den
    coef_im = (ab_im * a_re - (ab_re - 1.0) * a_im) / den
    b_re, b_im = bre_ref[...], bim_ref[...]
    abre_ref[...] = ab_re
    abim_ref[...] = ab_im
    bbre_ref[...] = coef_re * b_re - coef_im * b_im
    bbim_ref[...] = coef_re * b_im + coef_im * b_re


def _s5_prep(a_re, a_im, log_step, b_re, b_im):
    g, n = a_re.shape
    shp3 = jax.ShapeDtypeStruct((g, 1, n), F32)
    shpb = jax.ShapeDtypeStruct((g, S5_GROUP, n), F32)
    return pl.pallas_call(_s5_prep_kernel, out_shape=[shp3, shp3, shpb, shpb], name="s5_prep")(
        a_re.reshape(g, 1, n), a_im.reshape(g, 1, n), log_step.reshape(g, 1, 1),
        jnp.swapaxes(b_re, 1, 2), jnp.swapaxes(b_im, 1, 2))


S5_LANE_CHUNKS = S5_GROUPS * S5_STATE // LANES
S5_HALVES = 2


def _s5_layout(n_seq, lt):
    pack = max(1, SUBLANES // n_seq)
    pitch = lt + 4 if lt % SUBLANES == 0 else lt
    return pack, S5_LANE_CHUNKS // pack, pitch


def _s5_slot(c, n_groups):
    return c % n_groups, c // n_groups


def _s5_load_state(s0re_ref, s0im_ref, hre_ref, him_ref, n_seq, n_groups):
    for c in range(S5_LANE_CHUNKS):
        g, j = _s5_slot(c, n_groups)
        hre_ref[g, j * n_seq:(j + 1) * n_seq, :] = s0re_ref[:, c * LANES:(c + 1) * LANES]
        him_ref[g, j * n_seq:(j + 1) * n_seq, :] = s0im_ref[:, c * LANES:(c + 1) * LANES]


def _s5_store_state(sre_ref, sim_ref, hre_ref, him_ref, n_seq, n_groups):
    for c in range(S5_LANE_CHUNKS):
        g, j = _s5_slot(c, n_groups)
        sre_ref[:, c * LANES:(c + 1) * LANES] = hre_ref[g, j * n_seq:(j + 1) * n_seq, :]
        sim_ref[:, c * LANES:(c + 1) * LANES] = him_ref[g, j * n_seq:(j + 1) * n_seq, :]


def _s5_core(u, abre_ref, abim_ref, bbre_ref, bbim_ref, ccre_ref, ccim_ref, d_ref, wglu_ref, bglu_ref,
             xr_ref, xi_ref, hre_ref, him_ref, *, n_seq, lt, groups_per_pass, unroll):
    pack, n_groups, pitch = _s5_layout(n_seq, lt)
    rows = n_seq * lt
    per_half = S5_LANE_CHUNKS // S5_HALVES
    lanes_of = lambda c: slice(c * LANES, (c + 1) * LANES)
    slot_of = lambda c: _s5_slot(c, n_groups)

    def seq_rows(j, s):
        return slice((j * n_seq + s) * pitch, (j * n_seq + s) * pitch + lt)

    ub = _bf(u)
    u_cols = S5_WIDTH // S5_HALVES
    for half in range(S5_HALVES):
        uh = ub[:, half * u_cols:(half + 1) * u_cols]
        for x_ref, bb_ref in ((xr_ref, bbre_ref), (xi_ref, bbim_ref)):
            x = _dot(uh, bb_ref[half])
            for k in range(per_half):
                g, j = slot_of(half * per_half + k)
                if pitch == lt:
                    x_ref[g, j * rows:(j + 1) * rows, :] = x[:, lanes_of(k)]
                else:
                    for s in range(n_seq):
                        x_ref[g, seq_rows(j, s), :] = x[s * lt:(s + 1) * lt, lanes_of(k)]

    for g0 in range(0, n_groups, groups_per_pass):
        gs = list(range(g0, g0 + groups_per_pass))
        init = tuple(hre_ref[g] for g in gs) + tuple(him_ref[g] for g in gs)

        def step(t, carry, gs=gs):
            rws = pl.ds(t, pack * n_seq, stride=pitch)
            new_re, new_im = [], []
            for k, g in enumerate(gs):
                a_re, a_im = abre_ref[g], abim_ref[g]
                h_re, h_im = carry[k], carry[len(gs) + k]
                n_re = a_re * h_re - a_im * h_im + xr_ref[g, rws, :]
                n_im = a_re * h_im + a_im * h_re + xi_ref[g, rws, :]
                xr_ref[g, rws, :] = n_re
                xi_ref[g, rws, :] = n_im
                new_re.append(n_re)
                new_im.append(n_im)
            return tuple(new_re + new_im)

        fin = lax.fori_loop(0, lt, step, init, unroll=unroll)
        for k, g in enumerate(gs):
            hre_ref[g] = fin[k]
            him_ref[g] = fin[len(gs) + k]

    def gather(x_ref, half):
        cols = []
        for k in range(per_half):
            g, j = slot_of(half * per_half + k)
            if pitch == lt:
                cols.append(x_ref[g, j * rows:(j + 1) * rows, :])
            else:
                cols.append(jnp.concatenate([x_ref[g, seq_rows(j, s), :] for s in range(n_seq)], axis=0))
        return _bf(jnp.concatenate(cols, axis=1))

    y = jnp.concatenate([_dot(gather(xr_ref, half), ccre_ref[half]) - _dot(gather(xi_ref, half), ccim_ref[half])
                         for half in range(S5_HALVES)], axis=1) + d_ref[...] * u
    z = jax.nn.gelu(y, approximate=True)
    return z * jax.nn.sigmoid(_dot(_bf(z), wglu_ref[...]) + bglu_ref[...])


def _s5_kernel(u_ref, s0re_ref, s0im_ref, abre_ref, abim_ref, bbre_ref, bbim_ref, ccre_ref, ccim_ref,
               d_ref, wglu_ref, bglu_ref, out_ref, sre_ref, sim_ref, xr_ref, xi_ref, hre_ref, him_ref,
               *, n_seq, lt, groups_per_pass, unroll):
    n_groups = _s5_layout(n_seq, lt)[1]

    @pl.when(pl.program_id(0) == 0)
    def _():
        _s5_load_state(s0re_ref, s0im_ref, hre_ref, him_ref, n_seq, n_groups)

    out = _s5_core(u_ref[...].reshape(n_seq * lt, S5_WIDTH), abre_ref, abim_ref, bbre_ref, bbim_ref, ccre_ref, ccim_ref,
                   d_ref, wglu_ref, bglu_ref, xr_ref, xi_ref, hre_ref, him_ref,
                   n_seq=n_seq, lt=lt, groups_per_pass=groups_per_pass, unroll=unroll)
    out_ref[...] = out.reshape(out_ref.shape)

    @pl.when(pl.program_id(0) == pl.num_programs(0) - 1)
    def _():
        _s5_store_state(sre_ref, sim_ref, hre_ref, him_ref, n_seq, n_groups)


def _s5_tables(ab, n_seq, lt):
    pack, n_groups, _ = _s5_layout(n_seq, lt)
    tab = jnp.swapaxes(ab.reshape(pack, n_groups, 1, LANES), 0, 1)
    return jnp.broadcast_to(tab, (n_groups, pack, n_seq, LANES)).reshape(n_groups, pack * n_seq, LANES)


def _s5(proj3, s0_re, s0_im, ab_re, ab_im, bb_re, bb_im, cc_re, cc_im, d_row, wglu, bglu, *, n_seq, lt,
        groups_per_pass, unroll):
    nb, rows_b, _ = proj3.shape
    blk_rows = n_seq * lt // nb
    pack, n_groups, pitch = _s5_layout(n_seq, lt)
    ab_re, ab_im = _s5_tables(ab_re, n_seq, lt), _s5_tables(ab_im, n_seq, lt)
    full = lambda a: pl.BlockSpec(a.shape, lambda j: (0,) * a.ndim)
    io_spec = pl.BlockSpec((nb, blk_rows, S5_WIDTH), lambda j: (0, j, 0))
    x_scratch = pltpu.VMEM((n_groups, pack * n_seq * pitch, LANES), F32)
    h_scratch = pltpu.VMEM((n_groups, pack * n_seq, LANES), F32)
    return pl.pallas_call(
        functools.partial(_s5_kernel, n_seq=n_seq, lt=lt, groups_per_pass=groups_per_pass, unroll=unroll),
        grid=(rows_b // blk_rows,),
        in_specs=[io_spec, full(s0_re), full(s0_im), full(ab_re), full(ab_im), full(bb_re), full(bb_im),
                  full(cc_re), full(cc_im), full(d_row), full(wglu), full(bglu)],
        out_specs=[io_spec, full(s0_re), full(s0_im)],
        out_shape=[jax.ShapeDtypeStruct((nb, rows_b, S5_WIDTH), F32),
                   jax.ShapeDtypeStruct(s0_re.shape, F32), jax.ShapeDtypeStruct(s0_im.shape, F32)],
        scratch_shapes=[x_scratch, x_scratch, h_scratch, h_scratch],
        compiler_params=_params("arbitrary"),
        name="s5",
    )(proj3, s0_re, s0_im, ab_re, ab_im, bb_re, bb_im, cc_re, cc_im, d_row, wglu, bglu)


OD_Q_BLOCK = 1
OD_KV_BLOCK = 4
OD_COLS = 1280
ROPE_HALF = SWA_HD // 8
SWA_GROUP = SWA_HEADS // SWA_KV_HEADS


def _half_lanes(shape):
    return (_iota(shape, 1) & (LANES - 1)) < SWA_HD


def _pair_rms_scale(x):
    same_head = (_iota((LANES, LANES), 0) >= SWA_HD) == (_iota((LANES, LANES), 1) >= SWA_HD)
    sums = _dot_exact_rhs(x * x, _bf(same_head.astype(F32)), terms=2)
    return lax.rsqrt(sums * (1.0 / SWA_HD) + NORM_EPS)


def _swa_qk(xq, xk, qg, kg, cq, sq):
    xq_g = xq * qg
    n_pairs = SWA_HEADS // 2
    q_rot = xq_g * _lane_tile(cq, n_pairs) + _rope_partner(xq_g, SWA_HD, ROPE_HALF) * _lane_tile(sq, n_pairs)
    xk_g = xk * kg
    k_rot = xk_g * cq + _rope_partner(xk_g, SWA_HD, ROPE_HALF) * sq
    q_pairs = [q_rot[:, j * LANES:(j + 1) * LANES] * _pair_rms_scale(xq[:, j * LANES:(j + 1) * LANES])
               for j in range(SWA_HEADS // 2)]
    return q_pairs, k_rot * _pair_rms_scale(xk)


def _swa_query_stack(q_pairs, kh):
    lo = _half_lanes(q_pairs[0].shape)
    keep = lo if kh == 0 else jnp.logical_not(lo)
    parts = []
    for g in range(SWA_GROUP):
        hq = kh * SWA_GROUP + g
        pair = q_pairs[hq // 2]
        src = pair if hq % 2 == kh else pltpu.roll(pair, SWA_HD, 1)
        parts.append(jnp.where(keep, src, 0.0))
    return _bf(jnp.concatenate(parts, axis=0))


def _swa_merge_heads(o, kh, rows):
    lo = _half_lanes((rows, LANES))
    pairs = []
    for p in range(SWA_GROUP // 2):
        even, odd = o[2 * p * rows:(2 * p + 1) * rows], o[(2 * p + 1) * rows:(2 * p + 2) * rows]
        if kh == 0:
            pairs.append(jnp.where(lo, even, pltpu.roll(odd, SWA_HD, 1)))
        else:
            pairs.append(jnp.where(lo, pltpu.roll(even, SWA_HD, 1), odd))
    return pairs


def _swa_block(xq, xk, v_cur, k_prev, v_prev, mask, qg, kg, cq, sq, sink_ref):
    w = SWA_WINDOW
    ones_col = jnp.ones((2 * w, LANES), BF16)
    q_pairs, k_cur = _swa_qk(xq, xk, qg, kg, cq, sq)
    k_ext = _bf(jnp.concatenate([k_prev, k_cur], axis=0))
    v_ext = _bf(jnp.concatenate([v_prev, v_cur], axis=0))
    out_pairs = []
    for kh in range(SWA_KV_HEADS):
        s_all = _dot_nt(_swa_query_stack(q_pairs, kh), k_ext)
        weights, sink_terms = [], []
        for g in range(SWA_GROUP):
            s = jnp.where(mask, s_all[g * w:(g + 1) * w] * (SWA_HD ** -0.5), -jnp.inf)
            sink = sink_ref[kh * SWA_GROUP + g:kh * SWA_GROUP + g + 1, :]
            m = jnp.maximum(jnp.broadcast_to(jnp.max(s, axis=-1, keepdims=True), (w, LANES)), sink)
            weights.append(_bf(jnp.exp(s - jnp.concatenate([m, m], axis=1))))
            sink_terms.append(jnp.exp(sink - m))
        weights = jnp.concatenate(weights, axis=0)
        den = _dot(weights, ones_col) + jnp.concatenate(sink_terms, axis=0)
        out_pairs += _swa_merge_heads(_dot(weights, v_ext) / den, kh, w)
    return out_pairs, k_cur


def _odd_prompt_kernel(x_ref, gain_ref, win_ref, wout_ref, s0re_ref, s0im_ref, abre_ref, abim_ref, bbre_ref, bbim_ref,
                       ccre_ref, ccim_ref, d_ref, wglu_ref, bglu_ref, cq_ref, sq_ref, qg_ref, kg_ref, sink_ref,
                       o_ref, sre_ref, sim_ref, ck_ref, cv_ref,
                       proj_sc, mix_sc, xr_ref, xi_ref, hre_ref, him_ref, kprev_sc, vprev_sc,
                       *, n_seq, groups_per_pass, unroll):
    w = SWA_WINDOW
    rows = n_seq * w
    n_groups = _s5_layout(n_seq, w)[1]
    step = pl.program_id(0)

    @pl.when(step == 0)
    def _():
        _s5_load_state(s0re_ref, s0im_ref, hre_ref, him_ref, n_seq, n_groups)
        kprev_sc[...] = jnp.zeros_like(kprev_sc)
        vprev_sc[...] = jnp.zeros_like(vprev_sc)

    x = x_ref[...].reshape(rows, D_MODEL)
    proj_sc[...] = _dot(_bf(_rms(x, gain_ref[...])), win_ref[...])
    mix_sc[:, 0:S5_WIDTH] = _s5_core(proj_sc[:, 0:S5_WIDTH], abre_ref, abim_ref, bbre_ref, bbim_ref, ccre_ref, ccim_ref,
                                     d_ref, wglu_ref, bglu_ref, xr_ref, xi_ref, hre_ref, him_ref,
                                     n_seq=n_seq, lt=w, groups_per_pass=groups_per_pass, unroll=unroll)

    t_idx = _iota((w, 2 * w), 0)
    s_idx = _iota((w, 2 * w), 1)
    mask = (s_idx > t_idx) & (s_idx <= t_idx + w) & (s_idx >= jnp.where(step == 0, w, 0))
    q0, k0, v0 = OD_Q_BLOCK * 512, OD_KV_BLOCK * 256, OD_KV_BLOCK * 256 + LANES
    for s in range(n_seq):
        rs = slice(s * w, (s + 1) * w)
        v_cur = proj_sc[rs, v0:v0 + LANES]
        pairs, k_cur = _swa_block(proj_sc[rs, q0:q0 + 512], proj_sc[rs, k0:k0 + LANES], v_cur, kprev_sc[s], vprev_sc[s],
                                  mask, qg_ref[...], kg_ref[...], cq_ref[...], sq_ref[...], sink_ref)
        for i, pair in enumerate(pairs):
            mix_sc[rs, S5_WIDTH + i * LANES:S5_WIDTH + (i + 1) * LANES] = pair
        kprev_sc[s] = k_cur
        vprev_sc[s] = v_cur
    o_ref[...] = (x + _dot(_bf(mix_sc[...]), wout_ref[...])).reshape(o_ref.shape)

    @pl.when(step == pl.num_programs(0) - 1)
    def _():
        _s5_store_state(sre_ref, sim_ref, hre_ref, him_ref, n_seq, n_groups)
        ck_ref[...] = kprev_sc[...]
        cv_ref[...] = vprev_sc[...]


def _odd_prompt(x, gain, w_in, w_out, s0_re, s0_im, ab_re, ab_im, bb_re, bb_im, cc_re, cc_im, d_row, wglu, bglu,
                cq, sq, qg, kg, sink_rows, *, groups_per_pass, unroll):
    b, l, d = x.shape
    w = SWA_WINDOW
    pack, n_groups, pitch = _s5_layout(b, w)
    ab_re, ab_im = _s5_tables(ab_re, b, w), _s5_tables(ab_im, b, w)
    full = lambda a: pl.BlockSpec(a.shape, lambda j: (0,) * a.ndim)
    io_spec = pl.BlockSpec((b, w, d), lambda j: (0, j, 0))
    tab_spec = pl.BlockSpec((w, LANES), lambda j: (j, 0))
    cache_shape = jax.ShapeDtypeStruct((b, w, LANES), F32)
    x_scratch = pltpu.VMEM((n_groups, pack * b * pitch, LANES), F32)
    h_scratch = pltpu.VMEM((n_groups, pack * b, LANES), F32)
    kv_scratch = pltpu.VMEM((b, w, LANES), F32)
    return pl.pallas_call(
        functools.partial(_odd_prompt_kernel, n_seq=b, groups_per_pass=groups_per_pass, unroll=unroll),
        grid=(l // w,),
        in_specs=[io_spec, full(gain), full(w_in), full(w_out), full(s0_re), full(s0_im), full(ab_re), full(ab_im),
                  full(bb_re), full(bb_im), full(cc_re), full(cc_im), full(d_row), full(wglu), full(bglu),
                  tab_spec, tab_spec, full(qg), full(kg), full(sink_rows)],
        out_specs=[io_spec, full(s0_re), full(s0_im), pl.BlockSpec((b, w, LANES), lambda j: (0, 0, 0)),
                   pl.BlockSpec((b, w, LANES), lambda j: (0, 0, 0))],
        out_shape=[jax.ShapeDtypeStruct(x.shape, F32), jax.ShapeDtypeStruct(s0_re.shape, F32),
                   jax.ShapeDtypeStruct(s0_im.shape, F32), cache_shape, cache_shape],
        scratch_shapes=[pltpu.VMEM((b * w, OD_COLS), F32), pltpu.VMEM((b * w, d), F32),
                        x_scratch, x_scratch, h_scratch, h_scratch, kv_scratch, kv_scratch],
        compiler_params=_params("arbitrary"),
        name="odd_prompt",
    )(x, gain, w_in, w_out, s0_re, s0_im, ab_re, ab_im, bb_re, bb_im, cc_re, cc_im, d_row, wglu, bglu,
      cq, sq, qg, kg, sink_rows)


def _swa_sample_kernel(q_ref, kv_ref, ck_ref, cv_ref, cq_ref, sq_ref, qg_ref, kg_ref, sink_ref,
                       o_ref, nk_ref, nv_ref, *, bg, ls):
    w = SWA_WINDOW
    r = bg * ls
    ls_shift = ls.bit_length() - 1
    w_shift = w.bit_length() - 1
    rows_g = SWA_GROUP * r
    v_new = kv_ref[:, LANES:2 * LANES]
    q_pairs, k_new = _swa_qk(q_ref[...], kv_ref[:, 0:LANES], qg_ref[...], kg_ref[...], cq_ref[...], sq_ref[...])
    k_cache = _bf(ck_ref[...].reshape(bg * w, LANES))
    v_cache = _bf(cv_ref[...].reshape(bg * w, LANES))

    row = _iota((rows_g, bg * w), 0) & (r - 1)
    col = _iota((rows_g, bg * w), 1)
    mask_c = ((row >> ls_shift) == (col >> w_shift)) & ((col & (w - 1)) > (row & (ls - 1)))
    row_n = _iota((rows_g, r), 0) & (r - 1)
    col_n = _iota((rows_g, r), 1)
    mask_n = ((row_n >> ls_shift) == (col_n >> ls_shift)) & ((col_n & (ls - 1)) <= (row_n & (ls - 1)))
    for kh in range(SWA_KV_HEADS):
        q_stack = _swa_query_stack(q_pairs, kh)
        s_c = jnp.where(mask_c, _dot_nt(q_stack, k_cache) * (SWA_HD ** -0.5), -jnp.inf)
        s_n = jnp.where(mask_n, _dot_nt(q_stack, _bf(k_new)) * (SWA_HD ** -0.5), -jnp.inf)
        sink = jnp.concatenate([jnp.broadcast_to(sink_ref[kh * SWA_GROUP + g:kh * SWA_GROUP + g + 1, :], (r, LANES))
                                for g in range(SWA_GROUP)], axis=0)
        row_max = jnp.maximum(jnp.max(s_c, axis=-1, keepdims=True), jnp.max(s_n, axis=-1, keepdims=True))
        m = jnp.maximum(jnp.broadcast_to(row_max, (rows_g, LANES)), sink)
        e_c = _bf(jnp.exp(s_c - jnp.concatenate([m] * bg, axis=1)))
        e_n = _bf(jnp.exp(s_n - m[:, :r]))
        den = (_dot(e_c, jnp.ones((bg * w, LANES), BF16)) + _dot(e_n, jnp.ones((r, LANES), BF16))
               + jnp.exp(sink - m))
        o = (_dot(e_c, v_cache) + _dot(e_n, _bf(v_new))) / den
        for i, pair in enumerate(_swa_merge_heads(o, kh, r)):
            col_i = kh * (SWA_GROUP // 2) + i
            o_ref[:, col_i * LANES:(col_i + 1) * LANES] = pair

    nk_ref[:, 0:w - ls, :] = ck_ref[:, ls:w, :]
    nv_ref[:, 0:w - ls, :] = cv_ref[:, ls:w, :]
    for b in range(bg):
        nk_ref[b, w - ls:w, :] = k_new[b * ls:(b + 1) * ls, :]
        nv_ref[b, w - ls:w, :] = v_new[b * ls:(b + 1) * ls, :]


def _swa_sample(proj, cache_k, cache_v, cq, sq, qg, kg, sink_rows, *, bg, ls):
    t = proj.shape[0]
    w = SWA_WINDOW
    r = bg * ls
    full = lambda a: pl.BlockSpec(a.shape, lambda i: (0,) * a.ndim)
    cache_spec = pl.BlockSpec((bg, w, LANES), lambda i: (i, 0, 0))
    return pl.pallas_call(
        functools.partial(_swa_sample_kernel, bg=bg, ls=ls),
        grid=(t // r,),
        in_specs=[
            pl.BlockSpec((r, 512), lambda i: (i, OD_Q_BLOCK)),
            pl.BlockSpec((r, 256), lambda i: (i, OD_KV_BLOCK)),
            cache_spec, cache_spec, full(cq), full(sq), full(qg), full(kg), full(sink_rows),
        ],
        out_specs=[pl.BlockSpec((r, 512), lambda i: (i, 0)), cache_spec, cache_spec],
        out_shape=[jax.ShapeDtypeStruct((t, 512), F32),
                   jax.ShapeDtypeStruct(cache_k.shape, F32), jax.ShapeDtypeStruct(cache_v.shape, F32)],
        compiler_params=_params("parallel"),
        name="swa_sample",
    )(proj, proj, cache_k, cache_v, cq, sq, qg, kg, sink_rows)


def _mem_prompt_kernel(x_ref, g_ref, wq_ref, qg_ref, k_ref, v_ref, wo_ref, o_ref):
    x = x_ref[0]
    xn = _bf(_rms(x, g_ref[...]))
    k_all = _bf(_mem_rows(k_ref, 0))
    v_all = _bf(_mem_rows(v_ref, 0))
    acc = x
    for h in range(MEM_HEADS):
        sl = slice(h * MEM_HD, (h + 1) * MEM_HD)
        q = _bf(_rms(_dot(xn, wq_ref[:, sl]), qg_ref[...]))
        s = _dot_nt(q, k_all[:, sl]) * (MEM_HD ** -0.5)
        e = jnp.exp(s - jnp.max(s, axis=-1, keepdims=True))
        p = e / jnp.sum(e, axis=-1, keepdims=True)
        o = _dot(_bf(p), v_all[:, sl])
        acc = acc + _dot(_bf(o), wo_ref[sl, :])
    o_ref[0] = acc


def _mem_prompt(x, gain, wq, q_gain, k, v, wo, *, layer, lt):
    b, l, d = x.shape
    k, v = _mem_flat_view(k), _mem_flat_view(v)
    kv_spec = pl.BlockSpec((None, 1, k.shape[2], LANES), lambda i, j: (layer, i, 0, 0))
    io_spec = pl.BlockSpec((1, lt, d), lambda i, j: (i, j, 0))
    full = lambda a: pl.BlockSpec(a.shape, lambda i, j: (0,) * a.ndim)
    return pl.pallas_call(
        _mem_prompt_kernel,
        grid=(b, l // lt),
        in_specs=[io_spec, full(gain), full(wq), full(q_gain), kv_spec, kv_spec, full(wo)],
        out_specs=io_spec,
        out_shape=jax.ShapeDtypeStruct(x.shape, F32),
        compiler_params=_params("parallel", "arbitrary"),
        name="mem_prompt",
    )(x, gain, wq, q_gain, k, v, wo)


def _memkv_kernel(mem_ref, g_ref, wk_ref, wv_ref, kg_ref, k_out, v_out):
    xn = _bf(_rms(mem_ref[...], g_ref[...]))
    nb, n_mem, heads, hd = k_out.shape
    for h in range(heads):
        sl = slice(h * hd, (h + 1) * hd)
        k_out[:, :, h, :] = _rms(_dot(xn, wk_ref[:, sl]), kg_ref[...]).reshape(nb, n_mem, hd)
        v_out[:, :, h, :] = _dot(xn, wv_ref[:, sl]).reshape(nb, n_mem, hd)


def _memkv(mem, m_gain, wk, wv, k_gain):
    nb, n_mem, d = mem.shape
    depth = wk.shape[0]
    out_shape = jax.ShapeDtypeStruct((depth, nb, n_mem, MEM_HEADS, MEM_HD), F32)
    per_layer = lambda s: pl.BlockSpec((None,) + s, lambda l: (l,) + (0,) * len(s))
    return pl.pallas_call(
        _memkv_kernel,
        grid=(depth,),
        in_specs=[pl.BlockSpec((nb * n_mem, d), lambda l: (0, 0)), per_layer((1, d)), per_layer((d, d)),
                  per_layer((d, d)), per_layer((1, MEM_HD))],
        out_specs=[per_layer((nb, n_mem, MEM_HEADS, MEM_HD)), per_layer((nb, n_mem, MEM_HEADS, MEM_HD))],
        out_shape=[out_shape, out_shape],
        compiler_params=_params("arbitrary"),
        name="memkv",
    )(mem.reshape(nb * n_mem, d), m_gain, wk, wv, k_gain)


def _mem_sample_kernel(q_ref, k_ref, v_ref, o_ref, *, bs, ls):
    r = bs * ls
    per_seq = MEM_HEADS * ls
    n_exp = bs * per_seq
    n_mem = k_ref.shape[1] // MEM_ROW_GROUP
    hd_shift = MEM_HD.bit_length() - 1
    ls_shift = ls.bit_length() - 1
    seq_shift = per_seq.bit_length() - 1
    mem_shift = n_mem.bit_length() - 1
    qb = _bf(q_ref[...])
    e_row = _iota((n_exp, r), 0)
    sel = _bf((((e_row >> seq_shift) << ls_shift) + (e_row & (ls - 1)) == _iota((n_exp, r), 1)).astype(F32))
    head_mask = (((_iota((n_exp, D_MODEL), 0) >> ls_shift) & (MEM_HEADS - 1))
                 == (_iota((n_exp, D_MODEL), 1) >> hd_shift))
    q_exp = _bf(jnp.where(head_mask, _dot(sel, qb), 0.0))
    k_all = _bf(jnp.concatenate([_mem_rows(k_ref, b) for b in range(bs)], axis=0))
    v_all = _bf(jnp.concatenate([_mem_rows(v_ref, b) for b in range(bs)], axis=0))
    own = ((_iota((bs * n_mem, n_exp), 0) >> mem_shift) == (_iota((bs * n_mem, n_exp), 1) >> seq_shift))
    own = own.reshape(bs, n_mem, n_exp)
    s = (_dot_nt(k_all, q_exp) * (MEM_HD ** -0.5)).reshape(bs, n_mem, n_exp)
    s = jnp.where(own, s, -1e30)
    e = jnp.where(own, jnp.exp(s - jnp.max(s, axis=1, keepdims=True)), 0.0)
    den = jnp.sum(e, axis=1, keepdims=True) + jnp.where(jnp.any(own, axis=1, keepdims=True), 0.0, 1.0)
    p = _bf((e / den).reshape(bs * n_mem, n_exp))
    o_all = jnp.where(head_mask, _dot_tn(p, v_all), 0.0)
    o_ref[...] = _dot_tn(sel, _bf(o_all))


MEM_LANE_TILES = MEM_HD // LANES
MEM_ROW_GROUP = MEM_HEADS * MEM_LANE_TILES


def _mem_rows(ref, b):
    n_mem = ref.shape[1] // MEM_ROW_GROUP
    return jnp.concatenate([ref[b, pl.ds(lt * MEM_HEADS + h, n_mem, stride=MEM_ROW_GROUP), :]
                            for h in range(MEM_HEADS) for lt in range(MEM_LANE_TILES)], axis=1)


def _mem_flat_view(a):
    depth, nb, n_mem, heads, hd = a.shape
    a = a.reshape(depth, nb, n_mem, heads, hd // LANES, LANES).transpose(0, 1, 2, 4, 3, 5)
    return a.reshape(depth, nb, n_mem * MEM_ROW_GROUP, LANES)


def _mem_sample(q, k, v, *, layer, bs, ls):
    t, d = q.shape
    k, v = _mem_flat_view(k), _mem_flat_view(v)
    r = bs * ls
    kv_spec = pl.BlockSpec((None, bs, k.shape[2], LANES), lambda i: (layer, i, 0, 0))
    io_spec = pl.BlockSpec((r, d), lambda i: (i, 0))
    return pl.pallas_call(
        functools.partial(_mem_sample_kernel, bs=bs, ls=ls),
        grid=(t // r,),
        in_specs=[io_spec, kv_spec, kv_spec],
        out_specs=io_spec,
        out_shape=jax.ShapeDtypeStruct(q.shape, F32),
        compiler_params=_params("parallel"),
        name="mem_sample",
    )(q, k, v)


def _retention_tables(pos):
    inv = 1.0 / (RET_THETA ** jnp.linspace(0.0, 1.0, RET_DK // 2, dtype=F32))
    ang = pos[:, None] * inv[None, :]
    cos = jnp.repeat(jnp.cos(ang), 2, axis=1)
    sin = jnp.stack([-jnp.sin(ang), jnp.sin(ang)], axis=-1).reshape(pos.shape[0], RET_DK)
    return jnp.tile(cos, (1, PAIR)), jnp.tile(sin, (1, PAIR))


def _rope_tables(pos):
    half = ROPE_HALF
    inv = 1.0 / (ROPE_THETA ** (jnp.arange(half, dtype=F32) * 2.0 / (2 * half)))
    ang = pos[:, None] * inv[None, :]
    n = pos.shape[0]
    rest = SWA_HD - 2 * half
    cos = jnp.concatenate([jnp.cos(ang), jnp.cos(ang), jnp.ones((n, rest), F32)], axis=1)
    sin = jnp.concatenate([-jnp.sin(ang), jnp.sin(ang), jnp.zeros((n, rest), F32)], axis=1)
    return jnp.tile(cos, (1, PAIR)), jnp.tile(sin, (1, PAIR))


def _block_diag(t):
    g, a, b = t.shape
    eye = jnp.eye(g, dtype=t.dtype)
    return (t[:, :, None, :] * eye[:, None, :, None]).reshape(g * a, g * b)


def _half_block_diag(t):
    per = t.shape[0] // S5_HALVES
    return _bf(jnp.stack([_block_diag(t[h * per:(h + 1) * per]) for h in range(S5_HALVES)]))


def _sink_rows(sinks):
    return jnp.broadcast_to(sinks.astype(F32)[:, None], (sinks.shape[0], LANES))


def _trunk(x3, pos0, states, mem_k, mem_v, w, *, sample):
    b, l, d = x3.shape
    t = b * l
    x = x3.reshape(t, d)
    pos = pos0 + jnp.arange(l, dtype=F32)
    tm = 512 if sample else 1024
    gla_s, ret_s, s5_re, s5_im, swa_k, swa_v = states
    out_states = {k: [] for k in ("gla", "ret", "s5_re", "s5_im", "swa_k", "swa_v")}
    ld_row = jnp.repeat(jnp.log(1.0 - 2.0 ** (-5.0 - jnp.arange(RET_HEADS, dtype=F32))), RET_DK)[None, :]

    for layer in range(2):
        i = layer // 2
        x = _ffn(x, w['ffn1_norm'][layer][None], w['ffn1_w_gate'], w['ffn1_w_up'], w['ffn1_w_down'],
                 layer=layer, tm=tm, th=256)
        if layer % 2 == 0:
            cos, sin = _retention_tables(pos)
            args = (ld_row, w['gla_w_gate'][i], w['gla_b_gate'][i], w['gla_out_norm'][i])
            if sample:
                bg = 16
                proj = _norm_matmul(x, w['mix_norm'][layer][None], w['even_w_in'][i], tm=tm, tn=640)
                mixed, g_s, r_s = _even_sample(proj, jnp.tile(cos, (bg, 1)), jnp.tile(sin, (bg, 1)), *args,
                                               gla_s[i], ret_s[i], bg=bg, ls=l)
                x = _matmul_residual(x, [(mixed, w['even_w_out'][i])], tm=tm)
            else:
                x, g_s, r_s = _even_prompt(x.reshape(b, l, d), w['mix_norm'][layer][None], w['even_w_in'][i],
                                           w['even_w_out'][i], cos, sin, *args, gla_s[i], ret_s[i], lt=512)
                x = x.reshape(t, d)
            out_states["gla"].append(g_s)
            out_states["ret"].append(r_s)
        else:
            cq, sq = _rope_tables(pos)
            qg = jnp.tile(w['swa_q_norm'][i], SWA_HEADS)[None, :]
            kg = jnp.tile(w['swa_k_norm'][i], SWA_KV_HEADS)[None, :]
            s5_args = (w['s5_ab_re'][i], w['s5_ab_im'][i], w['s5_bb_re'][i], w['s5_bb_im'][i],
                       w['s5_cc_re'][i], w['s5_cc_im'][i], w['s5_d'][i][None], w['s5_w_glu'][i], w['s5_b_glu'][i][None])
            n_state = S5_GROUPS * S5_STATE
            s0_re, s0_im = s5_re[i].reshape(b, n_state), s5_im[i].reshape(b, n_state)
            sinks = _sink_rows(w['swa_sinks'][i])
            if sample:
                proj = _norm_matmul(x, w['mix_norm'][layer][None], w['odd_w_in'][i], tm=tm, tn=640)
                c_out, sr, si = _s5(proj.reshape(1, t, OD_COLS), s0_re, s0_im, *s5_args, n_seq=b, lt=l,
                                    groups_per_pass=1, unroll=True)
                bg = 8
                d_out, kb, vb = _swa_sample(proj, swa_k[i].reshape(b, SWA_WINDOW, LANES),
                                            swa_v[i].reshape(b, SWA_WINDOW, LANES),
                                            jnp.tile(cq, (bg, 1)), jnp.tile(sq, (bg, 1)), qg, kg, sinks, bg=bg, ls=l)
                w_out = w['odd_w_out'][i]
                x = _matmul_residual(x, [(c_out.reshape(t, S5_WIDTH), w_out[:S5_WIDTH]), (d_out, w_out[S5_WIDTH:])],
                                     tm=tm)
            else:
                x, sr, si, kb, vb = _odd_prompt(x.reshape(b, l, d), w['mix_norm'][layer][None], w['odd_w_in'][i],
                                                w['odd_w_out'][i], s0_re, s0_im, *s5_args, cq, sq, qg, kg, sinks,
                                                groups_per_pass=8, unroll=4)
                x = x.reshape(t, d)
            out_states["s5_re"].append(sr.reshape(b, S5_GROUPS, S5_STATE))
            out_states["s5_im"].append(si.reshape(b, S5_GROUPS, S5_STATE))
            out_states["swa_k"].append(kb.reshape(b, -1, SWA_KV_HEADS, SWA_HD))
            out_states["swa_v"].append(vb.reshape(b, -1, SWA_KV_HEADS, SWA_HD))
        if sample:
            q = _norm_matmul(x, w['mem_x_norm'][layer][None], w['mem_w_q'][layer], tm=tm, tn=MEM_HD,
                             head_gain=w['mem_q_norm'][layer][None], n_norm_tiles=MEM_HEADS)
            o = _mem_sample(q, mem_k, mem_v, layer=layer, bs=4, ls=l)
            x = _matmul_residual(x, [(o, w['mem_w_o'][layer])], tm=tm)
        else:
            x = _mem_prompt(x.reshape(b, l, d), w['mem_x_norm'][layer][None], w['mem_w_q'][layer],
                            w['mem_q_norm'][layer][None], mem_k, mem_v, w['mem_w_o'][layer],
                            layer=layer, lt=1024).reshape(t, d)
        x = _ffn(x, w['ffn2_norm'][layer][None], w['ffn2_w_gate'], w['ffn2_w_up'], w['ffn2_w_down'],
                 layer=layer, tm=tm, th=256)
    return x.reshape(b, l, d), {k: jnp.stack(v) for k, v in out_states.items()}


def kernel(x_prompt, x_sample, mem_prompt, state_gla, state_ret, state_s5_re, state_s5_im, cache_swa_k, cache_swa_v, cache_mem_k, cache_mem_v, ffn1_norm, ffn1_w_gate, ffn1_w_up, ffn1_w_down, ffn2_norm, ffn2_w_gate, ffn2_w_up, ffn2_w_down, mix_norm, even_w_in, gla_w_gate, gla_b_gate, gla_out_norm, even_w_out, odd_w_in, s5_a_re, s5_a_im, s5_log_step, s5_b_re, s5_b_im, s5_c_re, s5_c_im, s5_d, s5_w_glu, s5_b_glu, swa_q_norm, swa_k_norm, swa_sinks, odd_w_out, mem_x_norm, mem_m_norm, mem_w_q, mem_w_k, mem_w_v, mem_w_o, mem_q_norm, mem_k_norm):
    depth = ffn1_norm.shape[0]
    n_even, n_odd = even_w_in.shape[0], odd_w_in.shape[0]
    batch, seq, d = x_prompt.shape
    dec_batch = x_sample.shape[0]
    n_mem = mem_prompt.shape[1]

    ev = even_w_in
    ev_cols = jnp.concatenate(
        [ev[..., 0:1536], ev[..., 1552:3088], ev[..., 1536:1552],
         jnp.zeros(ev.shape[:2] + (EV_COLS - 3088,), ev.dtype)], axis=-1)
    wgate_pad = jnp.concatenate(
        [gla_w_gate, jnp.zeros((n_even, EV_COLS - EV_GA - GLA_RANK, gla_w_gate.shape[-1]), gla_w_gate.dtype)], axis=1)
    w = dict(
        ffn1_norm=ffn1_norm, ffn2_norm=ffn2_norm, mix_norm=mix_norm, mem_x_norm=mem_x_norm,
        ffn1_w_gate=_bf(ffn1_w_gate), ffn1_w_up=_bf(ffn1_w_up), ffn1_w_down=_bf(ffn1_w_down),
        ffn2_w_gate=_bf(ffn2_w_gate), ffn2_w_up=_bf(ffn2_w_up), ffn2_w_down=_bf(ffn2_w_down),
        even_w_in=_bf(ev_cols), gla_w_gate=_bf(wgate_pad), gla_b_gate=gla_b_gate[:, None, :],
        gla_out_norm=gla_out_norm[:, None, :], even_w_out=_bf(even_w_out),
        odd_w_in=_bf(odd_w_in), odd_w_out=_bf(odd_w_out), s5_d=s5_d, s5_w_glu=_bf(s5_w_glu), s5_b_glu=s5_b_glu,
        swa_q_norm=swa_q_norm, swa_k_norm=swa_k_norm, swa_sinks=swa_sinks,
        mem_w_q=_bf(mem_w_q), mem_w_o=_bf(mem_w_o), mem_q_norm=mem_q_norm,
    )
    ab_re, ab_im, bb_re, bb_im = [], [], [], []
    for i in range(n_odd):
        a_r, a_i, b_r, b_i = _s5_prep(s5_a_re[i], s5_a_im[i], s5_log_step[i], s5_b_re[i], s5_b_im[i])
        ab_re.append(a_r.reshape(-1))
        ab_im.append(a_i.reshape(-1))
        bb_re.append(_half_block_diag(b_r))
        bb_im.append(_half_block_diag(b_i))
    w.update(s5_ab_re=ab_re, s5_ab_im=ab_im, s5_bb_re=bb_re, s5_bb_im=bb_im,
             s5_cc_re=[_half_block_diag(jnp.swapaxes(s5_c_re[i], 1, 2)) for i in range(n_odd)],
             s5_cc_im=[_half_block_diag(jnp.swapaxes(s5_c_im[i], 1, 2)) for i in range(n_odd)])

    p_mem_k, p_mem_v = _memkv(mem_prompt, mem_m_norm[:, None, :], _bf(mem_w_k), _bf(mem_w_v), mem_k_norm[:, None, :])

    zeros = lambda *s: jnp.zeros(s, F32)
    p_states = (zeros(n_even, batch, GLA_HEADS, GLA_DK, GLA_DV), zeros(n_even, batch, RET_HEADS, RET_DK, GLA_DV),
                zeros(n_odd, batch, S5_GROUPS, S5_STATE), zeros(n_odd, batch, S5_GROUPS, S5_STATE), None, None)
    y_prompt, ps = _trunk(x_prompt, 0.0, p_states, p_mem_k, p_mem_v, w, sample=False)

    s_states = (state_gla, state_ret, state_s5_re, state_s5_im, cache_swa_k, cache_swa_v)
    y_sample, ss = _trunk(x_sample, float(PAST_LEN), s_states, cache_mem_k, cache_mem_v, w, sample=True)

    return (y_prompt, y_sample, ps["gla"], ps["ret"], ps["s5_re"], ps["s5_im"], ps["swa_k"], ps["swa_v"],
            p_mem_k, p_mem_v, ss["gla"], ss["ret"], ss["s5_re"], ss["s5_im"], ss["swa_k"], ss["swa_v"])
```

```python
import functools
import math

import jax
import jax.numpy as jnp
import numpy as np
from jax import lax
from jax.experimental import pallas as pl
from jax.experimental.pallas import tpu as pltpu

F32 = jnp.float32
BF16 = jnp.bfloat16
NORM_EPS = 1e-6
HIGHEST = lax.Precision.HIGHEST

D_MODEL = 1024
GLA_HEADS = 4
GLA_DK = 64
GLA_DV = 128
GLA_RANK = 16
GLA_TAU = 16.0
RET_HEADS = 4
RET_DK = 64
RET_THETA = 10000.0
LA_CHUNK = 64
S5_WIDTH = 512
S5_GROUP = 16
S5_GROUPS = 32
S5_STATE = 64
SWA_HD = 64
SWA_HEADS = 8
SWA_KV_HEADS = 2
SWA_WINDOW = 128
ROPE_THETA = 500000.0
MEM_HEADS = 4
MEM_HD = 256
PAST_LEN = 8192

VMEM_LIMIT_BYTES = 52 * 1024 * 1024
LANES = 128
SUBLANES = 8


def _params(*sem):
    return pltpu.CompilerParams(dimension_semantics=sem, vmem_limit_bytes=VMEM_LIMIT_BYTES)


def _rms(x, gain=None):
    y = x * lax.rsqrt(jnp.mean(x * x, axis=-1, keepdims=True) + NORM_EPS)
    return y if gain is None else y * gain


def _dot(a, b):
    return jnp.dot(a, b, preferred_element_type=F32)


def _dot_nt(a, b):
    return lax.dot_general(a, b, (((1,), (1,)), ((), ())), preferred_element_type=F32)


def _dot_tn(a, b):
    return lax.dot_general(a, b, (((0,), (0,)), ((), ())), preferred_element_type=F32)


def _dot_f32(a, b):
    return jnp.dot(a, b, precision=HIGHEST, preferred_element_type=F32)


def _split_bf16(x, terms):
    pieces = []
    for _ in range(terms):
        piece = _bf(x)
        pieces.append(piece)
        x = x - piece.astype(F32)
    return pieces


def _dot_exact_lhs(a, x, terms=3):
    return sum(_dot(a, piece) for piece in _split_bf16(x, terms))


def _dot_exact_rhs(x, b, terms=3):
    return sum(_dot(piece, b) for piece in _split_bf16(x, terms))


def _dot_nt_f32(a, b):
    return lax.dot_general(a, b, (((1,), (1,)), ((), ())), precision=HIGHEST, preferred_element_type=F32)


def _bf(x):
    return x.astype(BF16)


def _log_sigmoid(x):
    return jnp.minimum(x, 0.0) - jnp.log1p(jnp.exp(-jnp.abs(x)))


def _iota(shape, dim):
    return lax.broadcasted_iota(jnp.int32, shape, dim)


def _lane_tile(x, n):
    return jnp.concatenate([x] * n, axis=1)


def _swap_pairs(x):
    n = x.shape[-1]
    even = (_iota(x.shape, 1) & 1) == 0
    return jnp.where(even, pltpu.roll(x, n - 1, 1), pltpu.roll(x, 1, 1))


def _rope_partner(x, head_dim, half):
    n = x.shape[-1]
    first = (_iota(x.shape, 1) & (head_dim - 1)) < half
    return jnp.where(first, pltpu.roll(x, n - half, 1), pltpu.roll(x, half, 1))


FFN_HIDDEN_TILE = 1408
def _ffn_kernel(x_ref, g_ref, wg_ref, wu_ref, wd_ref, o_ref, xn_ref):
    @pl.when(pl.program_id(1) == 0)
    def _():
        x = x_ref[...]
        xn_ref[...] = _bf(_rms(x, g_ref[...]))
        o_ref[...] = x

    xn = xn_ref[...]
    gate = _dot(xn, wg_ref[...])
    up = _dot(xn, wu_ref[...])
    o_ref[...] += _dot(_bf(jax.nn.silu(gate) * up * 0.5), wd_ref[...])


def _ffn(x, gain, wg, wu, wd, *, layer, tm, th):
    t, d = x.shape
    h = wg.shape[2]
    return pl.pallas_call(
        _ffn_kernel,
        grid=(t // tm, h // th),
        in_specs=[
            pl.BlockSpec((tm, d), lambda i, j: (i, 0)),
            pl.BlockSpec((1, d), lambda i, j: (0, 0)),
            pl.BlockSpec((None, d, th), lambda i, j: (layer, 0, j)),
            pl.BlockSpec((None, d, th), lambda i, j: (layer, 0, j)),
            pl.BlockSpec((None, th, d), lambda i, j: (layer, j, 0)),
        ],
        out_specs=pl.BlockSpec((tm, d), lambda i, j: (i, 0)),
        out_shape=jax.ShapeDtypeStruct((t, d), F32),
        scratch_shapes=[pltpu.VMEM((tm, d), BF16)],
        compiler_params=_params("parallel", "arbitrary"),
        name="ffn",
    )(x, gain, wg, wu, wd)


def _nmm_kernel(x_ref, g_ref, w_ref, hg_ref, o_ref, xn_ref, *, n_norm_tiles):
    j = pl.program_id(1)

    @pl.when(j == 0)
    def _():
        xn_ref[...] = _bf(_rms(x_ref[...], g_ref[...]))

    y = _dot(xn_ref[...], w_ref[...])
    if n_norm_tiles == 0:
        o_ref[...] = y
    else:
        @pl.when(j < n_norm_tiles)
        def _():
            o_ref[...] = _rms(y, hg_ref[...])

        @pl.when(j >= n_norm_tiles)
        def _():
            o_ref[...] = y


def _norm_matmul(x, gain, w, *, tm, tn, head_gain=None, n_norm_tiles=0):
    t, d = x.shape
    n = w.shape[1]
    if head_gain is None:
        head_gain = jnp.ones((1, tn), F32)
    return pl.pallas_call(
        functools.partial(_nmm_kernel, n_norm_tiles=n_norm_tiles),
        grid=(t // tm, n // tn),
        in_specs=[
            pl.BlockSpec((tm, d), lambda i, j: (i, 0)),
            pl.BlockSpec((1, d), lambda i, j: (0, 0)),
            pl.BlockSpec((d, tn), lambda i, j: (0, j)),
            pl.BlockSpec((1, tn), lambda i, j: (0, 0)),
        ],
        out_specs=pl.BlockSpec((tm, tn), lambda i, j: (i, j)),
        out_shape=jax.ShapeDtypeStruct((t, n), F32),
        scratch_shapes=[pltpu.VMEM((tm, d), BF16)],
        compiler_params=_params("parallel", "arbitrary"),
        name="norm_matmul",
    )(x, gain, w, head_gain)


def _mmr_kernel(*refs, n_terms):
    x_ref = refs[0]
    a_refs = refs[1:1 + n_terms]
    w_refs = refs[1 + n_terms:1 + 2 * n_terms]
    o_ref = refs[1 + 2 * n_terms]
    acc = x_ref[...]
    for a_ref, w_ref in zip(a_refs, w_refs):
        acc = acc + _dot(_bf(a_ref[...]), w_ref[...])
    o_ref[...] = acc


def _matmul_residual(x, terms, *, tm):
    t, d = x.shape
    acts = [a for a, _ in terms]
    ws = [w for _, w in terms]
    in_specs = [pl.BlockSpec((tm, d), lambda i: (i, 0))]
    in_specs += [pl.BlockSpec((tm, a.shape[1]), lambda i: (i, 0)) for a in acts]
    in_specs += [pl.BlockSpec(w.shape, lambda i: (0, 0)) for w in ws]
    return pl.pallas_call(
        functools.partial(_mmr_kernel, n_terms=len(terms)),
        grid=(t // tm,),
        in_specs=in_specs,
        out_specs=pl.BlockSpec((tm, d), lambda i: (i, 0)),
        out_shape=jax.ShapeDtypeStruct((t, d), F32),
        compiler_params=_params("parallel"),
        name="matmul_residual",
    )(x, *acts, *ws)


EV_GQ, EV_GK, EV_GV, EV_GG = 0, 256, 512, 1024
EV_RQ, EV_RK, EV_RV, EV_RG = 1536, 1792, 2048, 2560
EV_GA = 3072
EV_COLS = 3200
EV_BLOCK = 256
PAIR = 2


def _gate_and_norm(o, gate, gain=None):
    return _rms(o, gain) * jax.nn.silu(gate)


def _even_prompt_kernel(x_ref, gain_ref, win_ref, wout_ref, cos_ref, sin_ref, ld_ref, wgate_ref, bgate_ref, gnorm_ref,
                        s0g_ref, s0r_ref, o_ref, sg_ref, sr_ref, proj_sc, mix_sc, o_sc, *, chunk, n_chunks):
    @pl.when(pl.program_id(1) == 0)
    def _():
        sg_ref[...] = s0g_ref[...]
        sr_ref[...] = s0r_ref[...]

    c = chunk
    lt = c * n_chunks
    blk = min(EV_BLOCK, lt)
    c_shift = c.bit_length() - 1
    x = x_ref[0]
    proj_sc[...] = _dot(_bf(_rms(x, gain_ref[...])), win_ref[...])

    def cols(a, b):
        return proj_sc[:, a:b]

    log_a = _log_sigmoid(_dot(_bf(cols(EV_GA, EV_COLS)), wgate_ref[...]) + bgate_ref[...]) * (1.0 / GLA_TAU)
    tril = _bf((_iota((c, c), 1) <= _iota((c, c), 0)).astype(F32))
    cum_parts = [_dot_exact_lhs(tril, log_a[i * c:(i + 1) * c]) for i in range(n_chunks)]
    tots = [p[c - 1:c] for p in cum_parts]
    cum = jnp.concatenate(cum_parts, axis=0)
    tot_b = jnp.concatenate([jnp.broadcast_to(t, (c, 256)) for t in tots], axis=0)
    k = cols(EV_GK, EV_GK + 256)
    gla = (cols(EV_GQ, EV_GQ + 256) * (GLA_DK ** -0.5) * jnp.exp(cum), k * jnp.exp(-cum), k * jnp.exp(tot_b - cum))

    ld = ld_ref[...]
    tpos = ((_iota((lt, 1), 0) & (c - 1)) + 1).astype(F32)
    cum_r = tpos * ld
    tot_r = float(c) * ld
    cos, sin = _lane_tile(cos_ref[...], PAIR), _lane_tile(sin_ref[...], PAIR)
    rq = cols(EV_RQ, EV_RQ + 256)
    rk = cols(EV_RK, EV_RK + 256)
    q_rot = rq * cos + _swap_pairs(rq) * sin
    k_rot = (rk * cos + _swap_pairs(rk) * sin) * (RET_DK ** -0.5)
    ret = (q_rot * jnp.exp(cum_r), k_rot * jnp.exp(-cum_r), k_rot * jnp.exp(tot_r - cum_r))

    tot_rows = jnp.concatenate(tots + [tot_r, jnp.zeros((LANES - n_chunks - 1, 256), F32)], axis=0)
    decay_cols = jnp.exp(jnp.transpose(tot_rows))

    row = _iota((blk, blk), 0)
    col = _iota((blk, blk), 1)
    blk_mask = ((row >> c_shift) == (col >> c_shift)) & (col <= row)
    lo = (_iota((lt, LANES), 1) < GLA_DK)
    gnorm = gnorm_ref[...]

    mixers = ((gla, EV_GV, EV_GG, sg_ref, 0, gnorm, lambda i: i),
              (ret, EV_RV, EV_RG, sr_ref, 512, None, lambda i: n_chunks))
    for m, ((q_dec, k_inv, k_dec), v_col, g_col, s_ref, out_col, gain, decay_col_of) in enumerate(mixers):
        for p in range(GLA_HEADS // PAIR):
            lanes = slice(p * LANES, (p + 1) * LANES)
            q_pair = q_dec[:, lanes]
            q_masked = [_bf(jnp.where(lo, q_pair, 0.0)), _bf(jnp.where(lo, 0.0, q_pair))]
            ki = _bf(k_inv[:, lanes])
            kd = _bf(k_dec[:, lanes])
            v_pair = _bf(cols(v_col + p * PAIR * GLA_DV, v_col + (p + 1) * PAIR * GLA_DV))
            for e in range(PAIR):
                slot = (m * (GLA_HEADS // PAIR) + p) * PAIR + e
                for r0 in range(0, lt, blk):
                    rs = slice(r0, r0 + blk)
                    scores = jnp.where(blk_mask, _dot_nt(q_masked[e][rs], ki[rs]), 0.0)
                    o_sc[slot, rs, :] = _dot(_bf(scores), v_pair[rs, e * GLA_DV:(e + 1) * GLA_DV])
            state = s_ref[0, p * PAIR:(p + 1) * PAIR].reshape(PAIR * GLA_DK, GLA_DV)
            for i in range(n_chunks):
                rs = slice(i * c, (i + 1) * c)
                q_stack = jnp.concatenate([q_masked[0][rs], q_masked[1][rs]], axis=0)
                o_inter = _dot(q_stack, _bf(state))
                kv = _dot_tn(kd[rs], v_pair[rs])
                kv = jnp.concatenate([kv[:GLA_DK, :GLA_DV], kv[GLA_DK:, GLA_DV:]], axis=0)
                ci = decay_col_of(i)
                state = state * decay_cols[p * LANES:(p + 1) * LANES, ci:ci + 1] + kv
                for e in range(PAIR):
                    slot = (m * (GLA_HEADS // PAIR) + p) * PAIR + e
                    o_sc[slot, rs, :] += o_inter[e * c:(e + 1) * c]
            s_ref[0, p * PAIR:(p + 1) * PAIR] = state.reshape(PAIR, GLA_DK, GLA_DV)
            for e in range(PAIR):
                h = p * PAIR + e
                slot = (m * (GLA_HEADS // PAIR) + p) * PAIR + e
                gate = cols(g_col + h * GLA_DV, g_col + (h + 1) * GLA_DV)
                mix_sc[:, out_col + h * GLA_DV:out_col + (h + 1) * GLA_DV] = _gate_and_norm(o_sc[slot], gate, gain)
    o_ref[0] = x + _dot(_bf(mix_sc[...]), wout_ref[...])


def _even_prompt(x, gain, w_in, w_out, cos, sin, ld_row, wgate, bgate, gnorm, s0g, s0r, *, lt):
    b, l, d = x.shape
    chunk = math.gcd(l, LA_CHUNK)
    st_spec = pl.BlockSpec((1, GLA_HEADS, GLA_DK, GLA_DV), lambda i, j: (i, 0, 0, 0))
    io_spec = pl.BlockSpec((1, lt, d), lambda i, j: (i, j, 0))
    full = lambda a: pl.BlockSpec(a.shape, lambda i, j: (0,) * a.ndim)
    return pl.pallas_call(
        functools.partial(_even_prompt_kernel, chunk=chunk, n_chunks=lt // chunk),
        grid=(b, l // lt),
        in_specs=[
            io_spec, full(gain), full(w_in), full(w_out),
            pl.BlockSpec((lt, LANES), lambda i, j: (j, 0)),
            pl.BlockSpec((lt, LANES), lambda i, j: (j, 0)),
            full(ld_row), full(wgate), full(bgate), full(gnorm), st_spec, st_spec,
        ],
        out_specs=[io_spec, st_spec, st_spec],
        out_shape=[jax.ShapeDtypeStruct(x.shape, F32),
                   jax.ShapeDtypeStruct(s0g.shape, F32), jax.ShapeDtypeStruct(s0r.shape, F32)],
        scratch_shapes=[pltpu.VMEM((lt, EV_COLS), F32), pltpu.VMEM((lt, d), F32),
                        pltpu.VMEM((GLA_HEADS + RET_HEADS, lt, GLA_DV), F32)],
        compiler_params=_params("parallel", "arbitrary"),
        name="even_prompt",
    )(x, gain, w_in, w_out, cos, sin, ld_row, wgate, bgate, gnorm, s0g, s0r)


def _even_sample_kernel(proj_ref, cos_ref, sin_ref, ld_ref, wgate_ref, bgate_ref, gnorm_ref, s0g_ref, s0r_ref,
                        mix_ref, sg_ref, sr_ref, *, bg, ls):
    r = bg * ls
    ls_shift = ls.bit_length() - 1
    dk_shift = GLA_DK.bit_length() - 1
    n_exp = bg * GLA_DK
    row_seq = _iota((r, r), 0) >> ls_shift
    col_seq = _iota((r, r), 1) >> ls_shift
    same = row_seq == col_seq
    seg = same & (_iota((r, r), 1) <= _iota((r, r), 0))
    seg_f = seg.astype(F32)
    same_f = same.astype(F32)
    tile_b = _bf(((_iota((GLA_DK, n_exp), 1) & (GLA_DK - 1)) == _iota((GLA_DK, n_exp), 0)).astype(F32))
    tile_t_f = ((_iota((n_exp, GLA_DK), 0) & (GLA_DK - 1)) == _iota((n_exp, GLA_DK), 1)).astype(F32)
    tile_t_b = _bf(tile_t_f)
    q_mask = (_iota((r, n_exp), 0) >> ls_shift) == (_iota((r, n_exp), 1) >> dk_shift)
    k_mask = (_iota((n_exp, r), 0) >> dk_shift) == (_iota((n_exp, r), 1) >> ls_shift)
    tpos = ((_iota((r, 1), 0) & (ls - 1)) + 1).astype(F32)
    ld = ld_ref[...]
    cum_r = tpos * ld
    tot_r = float(ls) * ld
    r_dec = jnp.exp(tot_r)
    gnorm = gnorm_ref[...]

    def mixer(q_dec, k_inv, k_dec, v_col, g_col, s0_ref, s_ref, out_col, decay_of, gain):
        for h in range(GLA_HEADS):
            sl = slice(h * GLA_DK, (h + 1) * GLA_DK)
            v = _bf(proj_ref[:, v_col + h * GLA_DV:v_col + (h + 1) * GLA_DV])
            qd = _bf(q_dec[:, sl])
            scores = jnp.where(seg, _dot_nt(qd, _bf(k_inv[:, sl])), 0.0)
            state = s0_ref[:, h].reshape(n_exp, GLA_DV)
            q_exp = _bf(jnp.where(q_mask, _dot(qd, tile_b), 0.0))
            o = _dot(_bf(scores), v) + _dot(q_exp, _bf(state))
            k_exp = _bf(jnp.where(k_mask, _dot_nt(tile_t_b, _bf(k_dec[:, sl])), 0.0))
            new_state = state * decay_of(h, sl) + _dot(k_exp, v)
            s_ref[:, h] = new_state.reshape(bg, GLA_DK, GLA_DV)
            gate = proj_ref[:, g_col + h * GLA_DV:g_col + (h + 1) * GLA_DV]
            mix_ref[:, out_col + h * GLA_DV:out_col + (h + 1) * GLA_DV] = _gate_and_norm(o, gate, gain)

    log_a = _log_sigmoid(_dot(_bf(proj_ref[:, EV_GA:EV_COLS]), wgate_ref[...]) + bgate_ref[...]) * (1.0 / GLA_TAU)
    cum = _dot_f32(seg_f, log_a)
    tot = _dot_f32(same_f, log_a)
    k = proj_ref[:, EV_GK:EV_GK + 256]

    def gla_decay(h, sl):
        la_exp = jnp.where(k_mask, _dot_nt_f32(tile_t_f, log_a[:, sl]), 0.0)
        return jnp.exp(jnp.sum(la_exp, axis=-1, keepdims=True))

    mixer(proj_ref[:, EV_GQ:EV_GQ + 256] * (GLA_DK ** -0.5) * jnp.exp(cum), k * jnp.exp(-cum), k * jnp.exp(tot - cum),
          EV_GV, EV_GG, s0g_ref, sg_ref, 0, gla_decay, gnorm)

    cos = _lane_tile(cos_ref[...], PAIR)
    sin = _lane_tile(sin_ref[...], PAIR)
    rq = proj_ref[:, EV_RQ:EV_RQ + 256]
    rk = proj_ref[:, EV_RK:EV_RK + 256]
    q_rot = rq * cos + _swap_pairs(rq) * sin
    k_rot = (rk * cos + _swap_pairs(rk) * sin) * (RET_DK ** -0.5)

    def ret_decay(h, sl):
        return r_dec[:, h * RET_DK:h * RET_DK + 1]

    mixer(q_rot * jnp.exp(cum_r), k_rot * jnp.exp(-cum_r), k_rot * jnp.exp(tot_r - cum_r),
          EV_RV, EV_RG, s0r_ref, sr_ref, 512, ret_decay, None)


def _even_sample(proj, cos, sin, ld_row, wgate, bgate, gnorm, s0g, s0r, *, bg, ls):
    t = proj.shape[0]
    n_b = t // ls
    r = bg * ls
    st_spec = pl.BlockSpec((bg, GLA_HEADS, GLA_DK, GLA_DV), lambda i: (i, 0, 0, 0))
    full = lambda a: pl.BlockSpec(a.shape, lambda i: (0,) * a.ndim)
    return pl.pallas_call(
        functools.partial(_even_sample_kernel, bg=bg, ls=ls),
        grid=(n_b // bg,),
        in_specs=[
            pl.BlockSpec((r, EV_COLS), lambda i: (i, 0)),
            full(cos), full(sin), full(ld_row), full(wgate), full(bgate), full(gnorm), st_spec, st_spec,
        ],
        out_specs=[pl.BlockSpec((r, D_MODEL), lambda i: (i, 0)), st_spec, st_spec],
        out_shape=[jax.ShapeDtypeStruct((t, D_MODEL), F32),
                   jax.ShapeDtypeStruct(s0g.shape, F32), jax.ShapeDtypeStruct(s0r.shape, F32)],
        compiler_params=_params("parallel"),
        name="even_sample",
    )(proj, cos, sin, ld_row, wgate, bgate, gnorm, s0g, s0r)


def _s5_prep_kernel(are_ref, aim_ref, lstep_ref, bre_ref, bim_ref, abre_ref, abim_ref, bbre_ref, bbim_ref):
    a_re, a_im = are_ref[...], aim_ref[...]
    step = jnp.exp(lstep_ref[...])
    mag = jnp.exp(a_re * step)
    ab_re = mag * jnp.cos(a_im * step)
    ab_im = mag * jnp.sin(a_im * step)
    den = a_re * a_re + a_im * a_im
    coef_re = ((ab_re - 1.0) * a_re + ab_im * a_im) / den
    coef_im = (ab_im * a_re - (ab_re - 1.0) * a_im) / den
    b_re, b_im = bre_ref[...], bim_ref[...]
    abre_ref[...] = ab_re
    abim_ref[...] = ab_im
    bbre_ref[...] = coef_re * b_re - coef_im * b_im
    bbim_ref[...] = coef_re * b_im + coef_im * b_re


def _s5_prep(a_re, a_im, log_step, b_re, b_im):
    g, n = a_re.shape
    shp3 = jax.ShapeDtypeStruct((g, 1, n), F32)
    shpb = jax.ShapeDtypeStruct((g, S5_GROUP, n), F32)
    return pl.pallas_call(_s5_prep_kernel, out_shape=[shp3, shp3, shpb, shpb], name="s5_prep")(
        a_re.reshape(g, 1, n), a_im.reshape(g, 1, n), log_step.reshape(g, 1, 1),
        jnp.swapaxes(b_re, 1, 2), jnp.swapaxes(b_im, 1, 2))


S5_LANE_CHUNKS = S5_GROUPS * S5_STATE // LANES
S5_HALVES = 2


def _s5_layout(n_seq, lt):
    pack = max(1, SUBLANES // n_seq)
    pitch = lt + 4 if lt % SUBLANES == 0 else lt
    return pack, S5_LANE_CHUNKS // pack, pitch


def _s5_slot(c, n_groups):
    return c % n_groups, c // n_groups


def _s5_load_state(s0re_ref, s0im_ref, hre_ref, him_ref, n_seq, n_groups):
    for c in range(S5_LANE_CHUNKS):
        g, j = _s5_slot(c, n_groups)
        hre_ref[g, j * n_seq:(j + 1) * n_seq, :] = s0re_ref[:, c * LANES:(c + 1) * LANES]
        him_ref[g, j * n_seq:(j + 1) * n_seq, :] = s0im_ref[:, c * LANES:(c + 1) * LANES]


def _s5_store_state(sre_ref, sim_ref, hre_ref, him_ref, n_seq, n_groups):
    for c in range(S5_LANE_CHUNKS):
        g, j = _s5_slot(c, n_groups)
        sre_ref[:, c * LANES:(c + 1) * LANES] = hre_ref[g, j * n_seq:(j + 1) * n_seq, :]
        sim_ref[:, c * LANES:(c + 1) * LANES] = him_ref[g, j * n_seq:(j + 1) * n_seq, :]


def _s5_core(u, abre_ref, abim_ref, bbre_ref, bbim_ref, ccre_ref, ccim_ref, d_ref, wglu_ref, bglu_ref,
             xr_ref, xi_ref, hre_ref, him_ref, *, n_seq, lt, groups_per_pass, unroll):
    pack, n_groups, pitch = _s5_layout(n_seq, lt)
    rows = n_seq * lt
    per_half = S5_LANE_CHUNKS // S5_HALVES
    lanes_of = lambda c: slice(c * LANES, (c + 1) * LANES)
    slot_of = lambda c: _s5_slot(c, n_groups)

    def seq_rows(j, s):
        return slice((j * n_seq + s) * pitch, (j * n_seq + s) * pitch + lt)

    ub = _bf(u)
    u_cols = S5_WIDTH // S5_HALVES
    for half in range(S5_HALVES):
        uh = ub[:, half * u_cols:(half + 1) * u_cols]
        for x_ref, bb_ref in ((xr_ref, bbre_ref), (xi_ref, bbim_ref)):
            x = _dot(uh, bb_ref[half])
            for k in range(per_half):
                g, j = slot_of(half * per_half + k)
                if pitch == lt:
                    x_ref[g, j * rows:(j + 1) * rows, :] = x[:, lanes_of(k)]
                else:
                    for s in range(n_seq):
                        x_ref[g, seq_rows(j, s), :] = x[s * lt:(s + 1) * lt, lanes_of(k)]

    for g0 in range(0, n_groups, groups_per_pass):
        gs = list(range(g0, g0 + groups_per_pass))
        init = tuple(hre_ref[g] for g in gs) + tuple(him_ref[g] for g in gs)

        def step(t, carry, gs=gs):
            rws = pl.ds(t, pack * n_seq, stride=pitch)
            new_re, new_im = [], []
            for k, g in enumerate(gs):
                a_re, a_im = abre_ref[g], abim_ref[g]
                h_re, h_im = carry[k], carry[len(gs) + k]
                n_re = a_re * h_re - a_im * h_im + xr_ref[g, rws, :]
                n_im = a_re * h_im + a_im * h_re + xi_ref[g, rws, :]
                xr_ref[g, rws, :] = n_re
                xi_ref[g, rws, :] = n_im
                new_re.append(n_re)
                new_im.append(n_im)
            return tuple(new_re + new_im)

        fin = lax.fori_loop(0, lt, step, init, unroll=unroll)
        for k, g in enumerate(gs):
            hre_ref[g] = fin[k]
            him_ref[g] = fin[len(gs) + k]

    def gather(x_ref, half):
        cols = []
        for k in range(per_half):
            g, j = slot_of(half * per_half + k)
            if pitch == lt:
                cols.append(x_ref[g, j * rows:(j + 1) * rows, :])
            else:
                cols.append(jnp.concatenate([x_ref[g, seq_rows(j, s), :] for s in range(n_seq)], axis=0))
        return _bf(jnp.concatenate(cols, axis=1))

    y = jnp.concatenate([_dot(gather(xr_ref, half), ccre_ref[half]) - _dot(gather(xi_ref, half), ccim_ref[half])
                         for half in range(S5_HALVES)], axis=1) + d_ref[...] * u
    z = jax.nn.gelu(y, approximate=True)
    return z * jax.nn.sigmoid(_dot(_bf(z), wglu_ref[...]) + bglu_ref[...])


def _s5_kernel(u_ref, s0re_ref, s0im_ref, abre_ref, abim_ref, bbre_ref, bbim_ref, ccre_ref, ccim_ref,
               d_ref, wglu_ref, bglu_ref, out_ref, sre_ref, sim_ref, xr_ref, xi_ref, hre_ref, him_ref,
               *, n_seq, lt, groups_per_pass, unroll):
    n_groups = _s5_layout(n_seq, lt)[1]

    @pl.when(pl.program_id(0) == 0)
    def _():
        _s5_load_state(s0re_ref, s0im_ref, hre_ref, him_ref, n_seq, n_groups)

    out = _s5_core(u_ref[...].reshape(n_seq * lt, S5_WIDTH), abre_ref, abim_ref, bbre_ref, bbim_ref, ccre_ref, ccim_ref,
                   d_ref, wglu_ref, bglu_ref, xr_ref, xi_ref, hre_ref, him_ref,
                   n_seq=n_seq, lt=lt, groups_per_pass=groups_per_pass, unroll=unroll)
    out_ref[...] = out.reshape(out_ref.shape)

    @pl.when(pl.program_id(0) == pl.num_programs(0) - 1)
    def _():
        _s5_store_state(sre_ref, sim_ref, hre_ref, him_ref, n_seq, n_groups)


def _s5_tables(ab, n_seq, lt):
    pack, n_groups, _ = _s5_layout(n_seq, lt)
    tab = jnp.swapaxes(ab.reshape(pack, n_groups, 1, LANES), 0, 1)
    return jnp.broadcast_to(tab, (n_groups, pack, n_seq, LANES)).reshape(n_groups, pack * n_seq, LANES)


def _s5(proj3, s0_re, s0_im, ab_re, ab_im, bb_re, bb_im, cc_re, cc_im, d_row, wglu, bglu, *, n_seq, lt,
        groups_per_pass, unroll):
    nb, rows_b, _ = proj3.shape
    blk_rows = n_seq * lt // nb
    pack, n_groups, pitch = _s5_layout(n_seq, lt)
    ab_re, ab_im = _s5_tables(ab_re, n_seq, lt), _s5_tables(ab_im, n_seq, lt)
    full = lambda a: pl.BlockSpec(a.shape, lambda j: (0,) * a.ndim)
    io_spec = pl.BlockSpec((nb, blk_rows, S5_WIDTH), lambda j: (0, j, 0))
    x_scratch = pltpu.VMEM((n_groups, pack * n_seq * pitch, LANES), F32)
    h_scratch = pltpu.VMEM((n_groups, pack * n_seq, LANES), F32)
    return pl.pallas_call(
        functools.partial(_s5_kernel, n_seq=n_seq, lt=lt, groups_per_pass=groups_per_pass, unroll=unroll),
        grid=(rows_b // blk_rows,),
        in_specs=[io_spec, full(s0_re), full(s0_im), full(ab_re), full(ab_im), full(bb_re), full(bb_im),
                  full(cc_re), full(cc_im), full(d_row), full(wglu), full(bglu)],
        out_specs=[io_spec, full(s0_re), full(s0_im)],
        out_shape=[jax.ShapeDtypeStruct((nb, rows_b, S5_WIDTH), F32),
                   jax.ShapeDtypeStruct(s0_re.shape, F32), jax.ShapeDtypeStruct(s0_im.shape, F32)],
        scratch_shapes=[x_scratch, x_scratch, h_scratch, h_scratch],
        compiler_params=_params("arbitrary"),
        name="s5",
    )(proj3, s0_re, s0_im, ab_re, ab_im, bb_re, bb_im, cc_re, cc_im, d_row, wglu, bglu)


OD_Q_BLOCK = 1
OD_KV_BLOCK = 4
OD_COLS = 1280
ROPE_HALF = SWA_HD // 8
SWA_GROUP = SWA_HEADS // SWA_KV_HEADS


def _half_lanes(shape):
    return (_iota(shape, 1) & (LANES - 1)) < SWA_HD


def _pair_rms_scale(x):
    same_head = (_iota((LANES, LANES), 0) >= SWA_HD) == (_iota((LANES, LANES), 1) >= SWA_HD)
    sums = _dot_exact_rhs(x * x, _bf(same_head.astype(F32)), terms=2)
    return lax.rsqrt(sums * (1.0 / SWA_HD) + NORM_EPS)


def _swa_qk(xq, xk, qg, kg, cq, sq):
    xq_g = xq * qg
    n_pairs = SWA_HEADS // 2
    q_rot = xq_g * _lane_tile(cq, n_pairs) + _rope_partner(xq_g, SWA_HD, ROPE_HALF) * _lane_tile(sq, n_pairs)
    xk_g = xk * kg
    k_rot = xk_g * cq + _rope_partner(xk_g, SWA_HD, ROPE_HALF) * sq
    q_pairs = [q_rot[:, j * LANES:(j + 1) * LANES] * _pair_rms_scale(xq[:, j * LANES:(j + 1) * LANES])
               for j in range(SWA_HEADS // 2)]
    return q_pairs, k_rot * _pair_rms_scale(xk)


def _swa_query_stack(q_pairs, kh):
    lo = _half_lanes(q_pairs[0].shape)
    keep = lo if kh == 0 else jnp.logical_not(lo)
    parts = []
    for g in range(SWA_GROUP):
        hq = kh * SWA_GROUP + g
        pair = q_pairs[hq // 2]
        src = pair if hq % 2 == kh else pltpu.roll(pair, SWA_HD, 1)
        parts.append(jnp.where(keep, src, 0.0))
    return _bf(jnp.concatenate(parts, axis=0))


def _swa_merge_heads(o, kh, rows):
    lo = _half_lanes((rows, LANES))
    pairs = []
    for p in range(SWA_GROUP // 2):
        even, odd = o[2 * p * rows:(2 * p + 1) * rows], o[(2 * p + 1) * rows:(2 * p + 2) * rows]
        if kh == 0:
            pairs.append(jnp.where(lo, even, pltpu.roll(odd, SWA_HD, 1)))
        else:
            pairs.append(jnp.where(lo, pltpu.roll(even, SWA_HD, 1), odd))
    return pairs


def _swa_block(xq, xk, v_cur, k_prev, v_prev, mask, qg, kg, cq, sq, sink_ref):
    w = SWA_WINDOW
    ones_col = jnp.ones((2 * w, LANES), BF16)
    q_pairs, k_cur = _swa_qk(xq, xk, qg, kg, cq, sq)
    k_ext = _bf(jnp.concatenate([k_prev, k_cur], axis=0))
    v_ext = _bf(jnp.concatenate([v_prev, v_cur], axis=0))
    out_pairs = []
    for kh in range(SWA_KV_HEADS):
        s_all = _dot_nt(_swa_query_stack(q_pairs, kh), k_ext)
        weights, sink_terms = [], []
        for g in range(SWA_GROUP):
            s = jnp.where(mask, s_all[g * w:(g + 1) * w] * (SWA_HD ** -0.5), -jnp.inf)
            sink = sink_ref[kh * SWA_GROUP + g:kh * SWA_GROUP + g + 1, :]
            m = jnp.maximum(jnp.broadcast_to(jnp.max(s, axis=-1, keepdims=True), (w, LANES)), sink)
            weights.append(_bf(jnp.exp(s - jnp.concatenate([m, m], axis=1))))
            sink_terms.append(jnp.exp(sink - m))
        weights = jnp.concatenate(weights, axis=0)
        den = _dot(weights, ones_col) + jnp.concatenate(sink_terms, axis=0)
        out_pairs += _swa_merge_heads(_dot(weights, v_ext) / den, kh, w)
    return out_pairs, k_cur


def _odd_prompt_kernel(x_ref, gain_ref, win_ref, wout_ref, s0re_ref, s0im_ref, abre_ref, abim_ref, bbre_ref, bbim_ref,
                       ccre_ref, ccim_ref, d_ref, wglu_ref, bglu_ref, cq_ref, sq_ref, qg_ref, kg_ref, sink_ref,
                       o_ref, sre_ref, sim_ref, ck_ref, cv_ref,
                       proj_sc, mix_sc, xr_ref, xi_ref, hre_ref, him_ref, kprev_sc, vprev_sc,
                       *, n_seq, groups_per_pass, unroll):
    w = SWA_WINDOW
    rows = n_seq * w
    n_groups = _s5_layout(n_seq, w)[1]
    step = pl.program_id(0)

    @pl.when(step == 0)
    def _():
        _s5_load_state(s0re_ref, s0im_ref, hre_ref, him_ref, n_seq, n_groups)
        kprev_sc[...] = jnp.zeros_like(kprev_sc)
        vprev_sc[...] = jnp.zeros_like(vprev_sc)

    x = x_ref[...].reshape(rows, D_MODEL)
    proj_sc[...] = _dot(_bf(_rms(x, gain_ref[...])), win_ref[...])
    mix_sc[:, 0:S5_WIDTH] = _s5_core(proj_sc[:, 0:S5_WIDTH], abre_ref, abim_ref, bbre_ref, bbim_ref, ccre_ref, ccim_ref,
                                     d_ref, wglu_ref, bglu_ref, xr_ref, xi_ref, hre_ref, him_ref,
                                     n_seq=n_seq, lt=w, groups_per_pass=groups_per_pass, unroll=unroll)

    t_idx = _iota((w, 2 * w), 0)
    s_idx = _iota((w, 2 * w), 1)
    mask = (s_idx > t_idx) & (s_idx <= t_idx + w) & (s_idx >= jnp.where(step == 0, w, 0))
    q0, k0, v0 = OD_Q_BLOCK * 512, OD_KV_BLOCK * 256, OD_KV_BLOCK * 256 + LANES
    for s in range(n_seq):
        rs = slice(s * w, (s + 1) * w)
        v_cur = proj_sc[rs, v0:v0 + LANES]
        pairs, k_cur = _swa_block(proj_sc[rs, q0:q0 + 512], proj_sc[rs, k0:k0 + LANES], v_cur, kprev_sc[s], vprev_sc[s],
                                  mask, qg_ref[...], kg_ref[...], cq_ref[...], sq_ref[...], sink_ref)
        for i, pair in enumerate(pairs):
            mix_sc[rs, S5_WIDTH + i * LANES:S5_WIDTH + (i + 1) * LANES] = pair
        kprev_sc[s] = k_cur
        vprev_sc[s] = v_cur
    o_ref[...] = (x + _dot(_bf(mix_sc[...]), wout_ref[...])).reshape(o_ref.shape)

    @pl.when(step == pl.num_programs(0) - 1)
    def _():
        _s5_store_state(sre_ref, sim_ref, hre_ref, him_ref, n_seq, n_groups)
        ck_ref[...] = kprev_sc[...]
        cv_ref[...] = vprev_sc[...]


def _odd_prompt(x, gain, w_in, w_out, s0_re, s0_im, ab_re, ab_im, bb_re, bb_im, cc_re, cc_im, d_row, wglu, bglu,
                cq, sq, qg, kg, sink_rows, *, groups_per_pass, unroll):
    b, l, d = x.shape
    w = SWA_WINDOW
    pack, n_groups, pitch = _s5_layout(b, w)
    ab_re, ab_im = _s5_tables(ab_re, b, w), _s5_tables(ab_im, b, w)
    full = lambda a: pl.BlockSpec(a.shape, lambda j: (0,) * a.ndim)
    io_spec = pl.BlockSpec((b, w, d), lambda j: (0, j, 0))
    tab_spec = pl.BlockSpec((w, LANES), lambda j: (j, 0))
    cache_shape = jax.ShapeDtypeStruct((b, w, LANES), F32)
    x_scratch = pltpu.VMEM((n_groups, pack * b * pitch, LANES), F32)
    h_scratch = pltpu.VMEM((n_groups, pack * b, LANES), F32)
    kv_scratch = pltpu.VMEM((b, w, LANES), F32)
    return pl.pallas_call(
        functools.partial(_odd_prompt_kernel, n_seq=b, groups_per_pass=groups_per_pass, unroll=unroll),
        grid=(l // w,),
        in_specs=[io_spec, full(gain), full(w_in), full(w_out), full(s0_re), full(s0_im), full(ab_re), full(ab_im),
                  full(bb_re), full(bb_im), full(cc_re), full(cc_im), full(d_row), full(wglu), full(bglu),
                  tab_spec, tab_spec, full(qg), full(kg), full(sink_rows)],
        out_specs=[io_spec, full(s0_re), full(s0_im), pl.BlockSpec((b, w, LANES), lambda j: (0, 0, 0)),
                   pl.BlockSpec((b, w, LANES), lambda j: (0, 0, 0))],
        out_shape=[jax.ShapeDtypeStruct(x.shape, F32), jax.ShapeDtypeStruct(s0_re.shape, F32),
                   jax.ShapeDtypeStruct(s0_im.shape, F32), cache_shape, cache_shape],
        scratch_shapes=[pltpu.VMEM((b * w, OD_COLS), F32), pltpu.VMEM((b * w, d), F32),
                        x_scratch, x_scratch, h_scratch, h_scratch, kv_scratch, kv_scratch],
        compiler_params=_params("arbitrary"),
        name="odd_prompt",
    )(x, gain, w_in, w_out, s0_re, s0_im, ab_re, ab_im, bb_re, bb_im, cc_re, cc_im, d_row, wglu, bglu,
      cq, sq, qg, kg, sink_rows)


def _swa_sample_kernel(q_ref, kv_ref, ck_ref, cv_ref, cq_ref, sq_ref, qg_ref, kg_ref, sink_ref,
                       o_ref, nk_ref, nv_ref, *, bg, ls):
    w = SWA_WINDOW
    r = bg * ls
    ls_shift = ls.bit_length() - 1
    w_shift = w.bit_length() - 1
    rows_g = SWA_GROUP * r
    v_new = kv_ref[:, LANES:2 * LANES]
    q_pairs, k_new = _swa_qk(q_ref[...], kv_ref[:, 0:LANES], qg_ref[...], kg_ref[...], cq_ref[...], sq_ref[...])
    k_cache = _bf(ck_ref[...].reshape(bg * w, LANES))
    v_cache = _bf(cv_ref[...].reshape(bg * w, LANES))

    row = _iota((rows_g, bg * w), 0) & (r - 1)
    col = _iota((rows_g, bg * w), 1)
    mask_c = ((row >> ls_shift) == (col >> w_shift)) & ((col & (w - 1)) > (row & (ls - 1)))
    row_n = _iota((rows_g, r), 0) & (r - 1)
    col_n = _iota((rows_g, r), 1)
    mask_n = ((row_n >> ls_shift) == (col_n >> ls_shift)) & ((col_n & (ls - 1)) <= (row_n & (ls - 1)))
    for kh in range(SWA_KV_HEADS):
        q_stack = _swa_query_stack(q_pairs, kh)
        s_c = jnp.where(mask_c, _dot_nt(q_stack, k_cache) * (SWA_HD ** -0.5), -jnp.inf)
        s_n = jnp.where(mask_n, _dot_nt(q_stack, _bf(k_new)) * (SWA_HD ** -0.5), -jnp.inf)
        sink = jnp.concatenate([jnp.broadcast_to(sink_ref[kh * SWA_GROUP + g:kh * SWA_GROUP + g + 1, :], (r, LANES))
                                for g in range(SWA_GROUP)], axis=0)
        row_max = jnp.maximum(jnp.max(s_c, axis=-1, keepdims=True), jnp.max(s_n, axis=-1, keepdims=True))
        m = jnp.maximum(jnp.broadcast_to(row_max, (rows_g, LANES)), sink)
        e_c = _bf(jnp.exp(s_c - jnp.concatenate([m] * bg, axis=1)))
        e_n = _bf(jnp.exp(s_n - m[:, :r]))
        den = (_dot(e_c, jnp.ones((bg * w, LANES), BF16)) + _dot(e_n, jnp.ones((r, LANES), BF16))
               + jnp.exp(sink - m))
        o = (_dot(e_c, v_cache) + _dot(e_n, _bf(v_new))) / den
        for i, pair in enumerate(_swa_merge_heads(o, kh, r)):
            col_i = kh * (SWA_GROUP // 2) + i
            o_ref[:, col_i * LANES:(col_i + 1) * LANES] = pair

    nk_ref[:, 0:w - ls, :] = ck_ref[:, ls:w, :]
    nv_ref[:, 0:w - ls, :] = cv_ref[:, ls:w, :]
    for b in range(bg):
        nk_ref[b, w - ls:w, :] = k_new[b * ls:(b + 1) * ls, :]
        nv_ref[b, w - ls:w, :] = v_new[b * ls:(b + 1) * ls, :]


def _swa_sample(proj, cache_k, cache_v, cq, sq, qg, kg, sink_rows, *, bg, ls):
    t = proj.shape[0]
    w = SWA_WINDOW
    r = bg * ls
    full = lambda a: pl.BlockSpec(a.shape, lambda i: (0,) * a.ndim)
    cache_spec = pl.BlockSpec((bg, w, LANES), lambda i: (i, 0, 0))
    return pl.pallas_call(
        functools.partial(_swa_sample_kernel, bg=bg, ls=ls),
        grid=(t // r,),
        in_specs=[
            pl.BlockSpec((r, 512), lambda i: (i, OD_Q_BLOCK)),
            pl.BlockSpec((r, 256), lambda i: (i, OD_KV_BLOCK)),
            cache_spec, cache_spec, full(cq), full(sq), full(qg), full(kg), full(sink_rows),
        ],
        out_specs=[pl.BlockSpec((r, 512), lambda i: (i, 0)), cache_spec, cache_spec],
        out_shape=[jax.ShapeDtypeStruct((t, 512), F32),
                   jax.ShapeDtypeStruct(cache_k.shape, F32), jax.ShapeDtypeStruct(cache_v.shape, F32)],
        compiler_params=_params("parallel"),
        name="swa_sample",
    )(proj, proj, cache_k, cache_v, cq, sq, qg, kg, sink_rows)


def _mem_prompt_kernel(x_ref, g_ref, wq_ref, qg_ref, k_ref, v_ref, wo_ref, o_ref):
    x = x_ref[0]
    xn = _bf(_rms(x, g_ref[...]))
    k_all = _bf(_mem_rows(k_ref, 0))
    v_all = _bf(_mem_rows(v_ref, 0))
    acc = x
    for h in range(MEM_HEADS):
        sl = slice(h * MEM_HD, (h + 1) * MEM_HD)
        q = _bf(_rms(_dot(xn, wq_ref[:, sl]), qg_ref[...]))
        s = _dot_nt(q, k_all[:, sl]) * (MEM_HD ** -0.5)
        e = jnp.exp(s - jnp.max(s, axis=-1, keepdims=True))
        p = e / jnp.sum(e, axis=-1, keepdims=True)
        o = _dot(_bf(p), v_all[:, sl])
        acc = acc + _dot(_bf(o), wo_ref[sl, :])
    o_ref[0] = acc


def _mem_prompt(x, gain, wq, q_gain, k, v, wo, *, layer, lt):
    b, l, d = x.shape
    k, v = _mem_flat_view(k), _mem_flat_view(v)
    kv_spec = pl.BlockSpec((None, 1, k.shape[2], LANES), lambda i, j: (layer, i, 0, 0))
    io_spec = pl.BlockSpec((1, lt, d), lambda i, j: (i, j, 0))
    full = lambda a: pl.BlockSpec(a.shape, lambda i, j: (0,) * a.ndim)
    return pl.pallas_call(
        _mem_prompt_kernel,
        grid=(b, l // lt),
        in_specs=[io_spec, full(gain), full(wq), full(q_gain), kv_spec, kv_spec, full(wo)],
        out_specs=io_spec,
        out_shape=jax.ShapeDtypeStruct(x.shape, F32),
        compiler_params=_params("parallel", "arbitrary"),
        name="mem_prompt",
    )(x, gain, wq, q_gain, k, v, wo)


def _memkv_kernel(mem_ref, g_ref, wk_ref, wv_ref, kg_ref, k_out, v_out):
    xn = _bf(_rms(mem_ref[...], g_ref[...]))
    nb, n_mem, heads, hd = k_out.shape
    for h in range(heads):
        sl = slice(h * hd, (h + 1) * hd)
        k_out[:, :, h, :] = _rms(_dot(xn, wk_ref[:, sl]), kg_ref[...]).reshape(nb, n_mem, hd)
        v_out[:, :, h, :] = _dot(xn, wv_ref[:, sl]).reshape(nb, n_mem, hd)


def _memkv(mem, m_gain, wk, wv, k_gain):
    nb, n_mem, d = mem.shape
    depth = wk.shape[0]
    out_shape = jax.ShapeDtypeStruct((depth, nb, n_mem, MEM_HEADS, MEM_HD), F32)
    per_layer = lambda s: pl.BlockSpec((None,) + s, lambda l: (l,) + (0,) * len(s))
    return pl.pallas_call(
        _memkv_kernel,
        grid=(depth,),
        in_specs=[pl.BlockSpec((nb * n_mem, d), lambda l: (0, 0)), per_layer((1, d)), per_layer((d, d)),
                  per_layer((d, d)), per_layer((1, MEM_HD))],
        out_specs=[per_layer((nb, n_mem, MEM_HEADS, MEM_HD)), per_layer((nb, n_mem, MEM_HEADS, MEM_HD))],
        out_shape=[out_shape, out_shape],
        compiler_params=_params("arbitrary"),
        name="memkv",
    )(mem.reshape(nb * n_mem, d), m_gain, wk, wv, k_gain)


def _mem_sample_kernel(q_ref, k_ref, v_ref, o_ref, *, bs, ls):
    r = bs * ls
    per_seq = MEM_HEADS * ls
    n_exp = bs * per_seq
    n_mem = k_ref.shape[1] // MEM_ROW_GROUP
    hd_shift = MEM_HD.bit_length() - 1
    ls_shift = ls.bit_length() - 1
    seq_shift = per_seq.bit_length() - 1
    mem_shift = n_mem.bit_length() - 1
    qb = _bf(q_ref[...])
    e_row = _iota((n_exp, r), 0)
    sel = _bf((((e_row >> seq_shift) << ls_shift) + (e_row & (ls - 1)) == _iota((n_exp, r), 1)).astype(F32))
    head_mask = (((_iota((n_exp, D_MODEL), 0) >> ls_shift) & (MEM_HEADS - 1))
                 == (_iota((n_exp, D_MODEL), 1) >> hd_shift))
    q_exp = _bf(jnp.where(head_mask, _dot(sel, qb), 0.0))
    k_all = _bf(jnp.concatenate([_mem_rows(k_ref, b) for b in range(bs)], axis=0))
    v_all = _bf(jnp.concatenate([_mem_rows(v_ref, b) for b in range(bs)], axis=0))
    own = ((_iota((bs * n_mem, n_exp), 0) >> mem_shift) == (_iota((bs * n_mem, n_exp), 1) >> seq_shift))
    own = own.reshape(bs, n_mem, n_exp)
    s = (_dot_nt(k_all, q_exp) * (MEM_HD ** -0.5)).reshape(bs, n_mem, n_exp)
    s = jnp.where(own, s, -1e30)
    e = jnp.where(own, jnp.exp(s - jnp.max(s, axis=1, keepdims=True)), 0.0)
    den = jnp.sum(e, axis=1, keepdims=True) + jnp.where(jnp.any(own, axis=1, keepdims=True), 0.0, 1.0)
    p = _bf((e / den).reshape(bs * n_mem, n_exp))
    o_all = jnp.where(head_mask, _dot_tn(p, v_all), 0.0)
    o_ref[...] = _dot_tn(sel, _bf(o_all))


MEM_LANE_TILES = MEM_HD // LANES
MEM_ROW_GROUP = MEM_HEADS * MEM_LANE_TILES


def _mem_rows(ref, b):
    n_mem = ref.shape[1] // MEM_ROW_GROUP
    return jnp.concatenate([ref[b, pl.ds(lt * MEM_HEADS + h, n_mem, stride=MEM_ROW_GROUP), :]
                            for h in range(MEM_HEADS) for lt in range(MEM_LANE_TILES)], axis=1)


def _mem_flat_view(a):
    depth, nb, n_mem, heads, hd = a.shape
    a = a.reshape(depth, nb, n_mem, heads, hd // LANES, LANES).transpose(0, 1, 2, 4, 3, 5)
    return a.reshape(depth, nb, n_mem * MEM_ROW_GROUP, LANES)


def _mem_sample(q, k, v, *, layer, bs, ls):
    t, d = q.shape
    k, v = _mem_flat_view(k), _mem_flat_view(v)
    r = bs * ls
    kv_spec = pl.BlockSpec((None, bs, k.shape[2], LANES), lambda i: (layer, i, 0, 0))
    io_spec = pl.BlockSpec((r, d), lambda i: (i, 0))
    return pl.pallas_call(
        functools.partial(_mem_sample_kernel, bs=bs, ls=ls),
        grid=(t // r,),
        in_specs=[io_spec, kv_spec, kv_spec],
        out_specs=io_spec,
        out_shape=jax.ShapeDtypeStruct(q.shape, F32),
        compiler_params=_params("parallel"),
        name="mem_sample",
    )(q, k, v)


def _retention_tables(pos):
    inv = 1.0 / (RET_THETA ** jnp.linspace(0.0, 1.0, RET_DK // 2, dtype=F32))
    ang = pos[:, None] * inv[None, :]
    cos = jnp.repeat(jnp.cos(ang), 2, axis=1)
    sin = jnp.stack([-jnp.sin(ang), jnp.sin(ang)], axis=-1).reshape(pos.shape[0], RET_DK)
    return jnp.tile(cos, (1, PAIR)), jnp.tile(sin, (1, PAIR))


def _rope_tables(pos):
    half = ROPE_HALF
    inv = 1.0 / (ROPE_THETA ** (jnp.arange(half, dtype=F32) * 2.0 / (2 * half)))
    ang = pos[:, None] * inv[None, :]
    n = pos.shape[0]
    rest = SWA_HD - 2 * half
    cos = jnp.concatenate([jnp.cos(ang), jnp.cos(ang), jnp.ones((n, rest), F32)], axis=1)
    sin = jnp.concatenate([-jnp.sin(ang), jnp.sin(ang), jnp.zeros((n, rest), F32)], axis=1)
    return jnp.tile(cos, (1, PAIR)), jnp.tile(sin, (1, PAIR))


def _block_diag(t):
    g, a, b = t.shape
    eye = jnp.eye(g, dtype=t.dtype)
    return (t[:, :, None, :] * eye[:, None, :, None]).reshape(g * a, g * b)


def _half_block_diag(t):
    per = t.shape[0] // S5_HALVES
    return _bf(jnp.stack([_block_diag(t[h * per:(h + 1) * per]) for h in range(S5_HALVES)]))


def _sink_rows(sinks):
    return jnp.broadcast_to(sinks.astype(F32)[:, None], (sinks.shape[0], LANES))


def _trunk(x3, pos0, states, mem_k, mem_v, w, *, sample):
    b, l, d = x3.shape
    t = b * l
    x = x3.reshape(t, d)
    pos = pos0 + jnp.arange(l, dtype=F32)
    tm = 512 if sample else 1024
    gla_s, ret_s, s5_re, s5_im, swa_k, swa_v = states
    out_states = {k: [] for k in ("gla", "ret", "s5_re", "s5_im", "swa_k", "swa_v")}
    ld_row = jnp.repeat(jnp.log(1.0 - 2.0 ** (-5.0 - jnp.arange(RET_HEADS, dtype=F32))), RET_DK)[None, :]

    for layer in range(2):
        i = layer // 2
        x = _ffn(x, w['ffn1_norm'][layer][None], w['ffn1_w_gate'], w['ffn1_w_up'], w['ffn1_w_down'],
                 layer=layer, tm=tm, th=FFN_HIDDEN_TILE)
        if layer % 2 == 0:
            cos, sin = _retention_tables(pos)
            args = (ld_row, w['gla_w_gate'][i], w['gla_b_gate'][i], w['gla_out_norm'][i])
            if sample:
                bg = 16
                proj = _norm_matmul(x, w['mix_norm'][layer][None], w['even_w_in'][i], tm=tm, tn=640)
                mixed, g_s, r_s = _even_sample(proj, jnp.tile(cos, (bg, 1)), jnp.tile(sin, (bg, 1)), *args,
                                               gla_s[i], ret_s[i], bg=bg, ls=l)
                x = _matmul_residual(x, [(mixed, w['even_w_out'][i])], tm=tm)
            else:
                x, g_s, r_s = _even_prompt(x.reshape(b, l, d), w['mix_norm'][layer][None], w['even_w_in'][i],
                                           w['even_w_out'][i], cos, sin, *args, gla_s[i], ret_s[i], lt=512)
                x = x.reshape(t, d)
            out_states["gla"].append(g_s)
            out_states["ret"].append(r_s)
        else:
            cq, sq = _rope_tables(pos)
            qg = jnp.tile(w['swa_q_norm'][i], SWA_HEADS)[None, :]
            kg = jnp.tile(w['swa_k_norm'][i], SWA_KV_HEADS)[None, :]
            s5_args = (w['s5_ab_re'][i], w['s5_ab_im'][i], w['s5_bb_re'][i], w['s5_bb_im'][i],
                       w['s5_cc_re'][i], w['s5_cc_im'][i], w['s5_d'][i][None], w['s5_w_glu'][i], w['s5_b_glu'][i][None])
            n_state = S5_GROUPS * S5_STATE
            s0_re, s0_im = s5_re[i].reshape(b, n_state), s5_im[i].reshape(b, n_state)
            sinks = _sink_rows(w['swa_sinks'][i])
            if sample:
                proj = _norm_matmul(x, w['mix_norm'][layer][None], w['odd_w_in'][i], tm=tm, tn=640)
                c_out, sr, si = _s5(proj.reshape(1, t, OD_COLS), s0_re, s0_im, *s5_args, n_seq=b, lt=l,
                                    groups_per_pass=1, unroll=True)
                bg = 8
                d_out, kb, vb = _swa_sample(proj, swa_k[i].reshape(b, SWA_WINDOW, LANES),
                                            swa_v[i].reshape(b, SWA_WINDOW, LANES),
                                            jnp.tile(cq, (bg, 1)), jnp.tile(sq, (bg, 1)), qg, kg, sinks, bg=bg, ls=l)
                w_out = w['odd_w_out'][i]
                x = _matmul_residual(x, [(c_out.reshape(t, S5_WIDTH), w_out[:S5_WIDTH]), (d_out, w_out[S5_WIDTH:])],
                                     tm=tm)
            else:
                x, sr, si, kb, vb = _odd_prompt(x.reshape(b, l, d), w['mix_norm'][layer][None], w['odd_w_in'][i],
                                                w['odd_w_out'][i], s0_re, s0_im, *s5_args, cq, sq, qg, kg, sinks,
                                                groups_per_pass=8, unroll=4)
                x = x.reshape(t, d)
            out_states["s5_re"].append(sr.reshape(b, S5_GROUPS, S5_STATE))
            out_states["s5_im"].append(si.reshape(b, S5_GROUPS, S5_STATE))
            out_states["swa_k"].append(kb.reshape(b, -1, SWA_KV_HEADS, SWA_HD))
            out_states["swa_v"].append(vb.reshape(b, -1, SWA_KV_HEADS, SWA_HD))
        if sample:
            q = _norm_matmul(x, w['mem_x_norm'][layer][None], w['mem_w_q'][layer], tm=tm, tn=MEM_HD,
                             head_gain=w['mem_q_norm'][layer][None], n_norm_tiles=MEM_HEADS)
            o = _mem_sample(q, mem_k, mem_v, layer=layer, bs=4, ls=l)
            x = _matmul_residual(x, [(o, w['mem_w_o'][layer])], tm=tm)
        else:
            x = _mem_prompt(x.reshape(b, l, d), w['mem_x_norm'][layer][None], w['mem_w_q'][layer],
                            w['mem_q_norm'][layer][None], mem_k, mem_v, w['mem_w_o'][layer],
                            layer=layer, lt=1024).reshape(t, d)
        x = _ffn(x, w['ffn2_norm'][layer][None], w['ffn2_w_gate'], w['ffn2_w_up'], w['ffn2_w_down'],
                 layer=layer, tm=tm, th=FFN_HIDDEN_TILE)
    return x.reshape(b, l, d), {k: jnp.stack(v) for k, v in out_states.items()}


def kernel(x_prompt, x_sample, mem_prompt, state_gla, state_ret, state_s5_re, state_s5_im, cache_swa_k, cache_swa_v, cache_mem_k, cache_mem_v, ffn1_norm, ffn1_w_gate, ffn1_w_up, ffn1_w_down, ffn2_norm, ffn2_w_gate, ffn2_w_up, ffn2_w_down, mix_norm, even_w_in, gla_w_gate, gla_b_gate, gla_out_norm, even_w_out, odd_w_in, s5_a_re, s5_a_im, s5_log_step, s5_b_re, s5_b_im, s5_c_re, s5_c_im, s5_d, s5_w_glu, s5_b_glu, swa_q_norm, swa_k_norm, swa_sinks, odd_w_out, mem_x_norm, mem_m_norm, mem_w_q, mem_w_k, mem_w_v, mem_w_o, mem_q_norm, mem_k_norm):
    depth = ffn1_norm.shape[0]
    n_even, n_odd = even_w_in.shape[0], odd_w_in.shape[0]
    batch, seq, d = x_prompt.shape
    dec_batch = x_sample.shape[0]
    n_mem = mem_prompt.shape[1]

    ev = even_w_in
    ev_cols = jnp.concatenate(
        [ev[..., 0:1536], ev[..., 1552:3088], ev[..., 1536:1552],
         jnp.zeros(ev.shape[:2] + (EV_COLS - 3088,), ev.dtype)], axis=-1)
    wgate_pad = jnp.concatenate(
        [gla_w_gate, jnp.zeros((n_even, EV_COLS - EV_GA - GLA_RANK, gla_w_gate.shape[-1]), gla_w_gate.dtype)], axis=1)
    w = dict(
        ffn1_norm=ffn1_norm, ffn2_norm=ffn2_norm, mix_norm=mix_norm, mem_x_norm=mem_x_norm,
        ffn1_w_gate=_bf(ffn1_w_gate), ffn1_w_up=_bf(ffn1_w_up), ffn1_w_down=_bf(ffn1_w_down),
        ffn2_w_gate=_bf(ffn2_w_gate), ffn2_w_up=_bf(ffn2_w_up), ffn2_w_down=_bf(ffn2_w_down),
        even_w_in=_bf(ev_cols), gla_w_gate=_bf(wgate_pad), gla_b_gate=gla_b_gate[:, None, :],
        gla_out_norm=gla_out_norm[:, None, :], even_w_out=_bf(even_w_out),
        odd_w_in=_bf(odd_w_in), odd_w_out=_bf(odd_w_out), s5_d=s5_d, s5_w_glu=_bf(s5_w_glu), s5_b_glu=s5_b_glu,
        swa_q_norm=swa_q_norm, swa_k_norm=swa_k_norm, swa_sinks=swa_sinks,
        mem_w_q=_bf(mem_w_q), mem_w_o=_bf(mem_w_o), mem_q_norm=mem_q_norm,
    )
    ab_re, ab_im, bb_re, bb_im = [], [], [], []
    for i in range(n_odd):
        a_r, a_i, b_r, b_i = _s5_prep(s5_a_re[i], s5_a_im[i], s5_log_step[i], s5_b_re[i], s5_b_im[i])
        ab_re.append(a_r.reshape(-1))
        ab_im.append(a_i.reshape(-1))
        bb_re.append(_half_block_diag(b_r))
        bb_im.append(_half_block_diag(b_i))
    w.update(s5_ab_re=ab_re, s5_ab_im=ab_im, s5_bb_re=bb_re, s5_bb_im=bb_im,
             s5_cc_re=[_half_block_diag(jnp.swapaxes(s5_c_re[i], 1, 2)) for i in range(n_odd)],
             s5_cc_im=[_half_block_diag(jnp.swapaxes(s5_c_im[i], 1, 2)) for i in range(n_odd)])

    p_mem_k, p_mem_v = _memkv(mem_prompt, mem_m_norm[:, None, :], _bf(mem_w_k), _bf(mem_w_v), mem_k_norm[:, None, :])

    zeros = lambda *s: jnp.zeros(s, F32)
    p_states = (zeros(n_even, batch, GLA_HEADS, GLA_DK, GLA_DV), zeros(n_even, batch, RET_HEADS, RET_DK, GLA_DV),
                zeros(n_odd, batch, S5_GROUPS, S5_STATE), zeros(n_odd, batch, S5_GROUPS, S5_STATE), None, None)
    y_prompt, ps = _trunk(x_prompt, 0.0, p_states, p_mem_k, p_mem_v, w, sample=False)

    s_states = (state_gla, state_ret, state_s5_re, state_s5_im, cache_swa_k, cache_swa_v)
    y_sample, ss = _trunk(x_sample, float(PAST_LEN), s_states, cache_mem_k, cache_mem_v, w, sample=True)

    return (y_prompt, y_sample, ps["gla"], ps["ret"], ps["s5_re"], ps["s5_im"], ps["swa_k"], ps["swa_v"],
            p_mem_k, p_mem_v, ss["gla"], ss["ret"], ss["s5_re"], ss["s5_im"], ss["swa_k"], ss["swa_v"])
```

```python
import functools
import math

import jax
import jax.numpy as jnp
import numpy as np
from jax import lax
from jax.experimental import pallas as pl
from jax.experimental.pallas import tpu as pltpu

F32 = jnp.float32
BF16 = jnp.bfloat16
NORM_EPS = 1e-6
HIGHEST = lax.Precision.HIGHEST

D_MODEL = 1024
GLA_HEADS = 4
GLA_DK = 64
GLA_DV = 128
GLA_RANK = 16
GLA_TAU = 16.0
RET_HEADS = 4
RET_DK = 64
RET_THETA = 10000.0
LA_CHUNK = 64
S5_WIDTH = 512
S5_GROUP = 16
S5_GROUPS = 32
S5_STATE = 64
SWA_HD = 64
SWA_HEADS = 8
SWA_KV_HEADS = 2
SWA_WINDOW = 128
ROPE_THETA = 500000.0
MEM_HEADS = 4
MEM_HD = 256
PAST_LEN = 8192

VMEM_LIMIT_BYTES = 52 * 1024 * 1024
LANES = 128
SUBLANES = 8


def _params(*sem):
    return pltpu.CompilerParams(dimension_semantics=sem, vmem_limit_bytes=VMEM_LIMIT_BYTES)


def _rms(x, gain=None):
    y = x * lax.rsqrt(jnp.mean(x * x, axis=-1, keepdims=True) + NORM_EPS)
    return y if gain is None else y * gain


def _dot(a, b):
    return jnp.dot(a, b, preferred_element_type=F32)


def _dot_nt(a, b):
    return lax.dot_general(a, b, (((1,), (1,)), ((), ())), preferred_element_type=F32)


def _dot_tn(a, b):
    return lax.dot_general(a, b, (((0,), (0,)), ((), ())), preferred_element_type=F32)


def _dot_f32(a, b):
    return jnp.dot(a, b, precision=HIGHEST, preferred_element_type=F32)


def _split_bf16(x, terms):
    pieces = []
    for _ in range(terms):
        piece = _bf(x)
        pieces.append(piece)
        x = x - piece.astype(F32)
    return pieces


def _dot_exact_lhs(a, x, terms=3):
    return sum(_dot(a, piece) for piece in _split_bf16(x, terms))


def _dot_exact_rhs(x, b, terms=3):
    return sum(_dot(piece, b) for piece in _split_bf16(x, terms))


def _dot_nt_f32(a, b):
    return lax.dot_general(a, b, (((1,), (1,)), ((), ())), precision=HIGHEST, preferred_element_type=F32)


def _bf(x):
    return x.astype(BF16)


def _log_sigmoid(x):
    return jnp.minimum(x, 0.0) - jnp.log1p(jnp.exp(-jnp.abs(x)))


def _iota(shape, dim):
    return lax.broadcasted_iota(jnp.int32, shape, dim)


def _lane_tile(x, n):
    return jnp.concatenate([x] * n, axis=1)


def _swap_pairs(x):
    n = x.shape[-1]
    even = (_iota(x.shape, 1) & 1) == 0
    return jnp.where(even, pltpu.roll(x, n - 1, 1), pltpu.roll(x, 1, 1))


def _rope_partner(x, head_dim, half):
    n = x.shape[-1]
    first = (_iota(x.shape, 1) & (head_dim - 1)) < half
    return jnp.where(first, pltpu.roll(x, n - half, 1), pltpu.roll(x, half, 1))


FFN_ROW_TILE = 512


def _ffn_kernel(x_ref, g_ref, wg_ref, wu_ref, wd_ref, o_ref):
    x = x_ref[...]
    xn = _bf(_rms(x, g_ref[...]))
    gate = _dot(xn, wg_ref[...])
    up = _dot(xn, wu_ref[...])
    o_ref[...] = x + _dot(_bf(jax.nn.silu(gate) * up * 0.5), wd_ref[...])


def _ffn(x, gain, wg, wu, wd, *, layer):
    t, d = x.shape
    h = wg.shape[2]
    tm = FFN_ROW_TILE
    resident = dict(pipeline_mode=pl.Buffered(1))
    return pl.pallas_call(
        _ffn_kernel,
        grid=(t // tm,),
        in_specs=[
            pl.BlockSpec((tm, d), lambda i: (i, 0)),
            pl.BlockSpec((1, d), lambda i: (0, 0)),
            pl.BlockSpec((None, d, h), lambda i: (layer, 0, 0), **resident),
            pl.BlockSpec((None, d, h), lambda i: (layer, 0, 0), **resident),
            pl.BlockSpec((None, h, d), lambda i: (layer, 0, 0), **resident),
        ],
        out_specs=pl.BlockSpec((tm, d), lambda i: (i, 0)),
        out_shape=jax.ShapeDtypeStruct((t, d), F32),
        compiler_params=_params("parallel"),
        name="ffn",
    )(x, gain, wg, wu, wd)


def _nmm_kernel(x_ref, g_ref, w_ref, hg_ref, o_ref, xn_ref, *, n_norm_tiles):
    j = pl.program_id(1)

    @pl.when(j == 0)
    def _():
        xn_ref[...] = _bf(_rms(x_ref[...], g_ref[...]))

    y = _dot(xn_ref[...], w_ref[...])
    if n_norm_tiles == 0:
        o_ref[...] = y
    else:
        @pl.when(j < n_norm_tiles)
        def _():
            o_ref[...] = _rms(y, hg_ref[...])

        @pl.when(j >= n_norm_tiles)
        def _():
            o_ref[...] = y


def _norm_matmul(x, gain, w, *, tm, tn, head_gain=None, n_norm_tiles=0):
    t, d = x.shape
    n = w.shape[1]
    if head_gain is None:
        head_gain = jnp.ones((1, tn), F32)
    return pl.pallas_call(
        functools.partial(_nmm_kernel, n_norm_tiles=n_norm_tiles),
        grid=(t // tm, n // tn),
        in_specs=[
            pl.BlockSpec((tm, d), lambda i, j: (i, 0)),
            pl.BlockSpec((1, d), lambda i, j: (0, 0)),
            pl.BlockSpec((d, tn), lambda i, j: (0, j)),
            pl.BlockSpec((1, tn), lambda i, j: (0, 0)),
        ],
        out_specs=pl.BlockSpec((tm, tn), lambda i, j: (i, j)),
        out_shape=jax.ShapeDtypeStruct((t, n), F32),
        scratch_shapes=[pltpu.VMEM((tm, d), BF16)],
        compiler_params=_params("parallel", "arbitrary"),
        name="norm_matmul",
    )(x, gain, w, head_gain)


def _mmr_kernel(*refs, n_terms):
    x_ref = refs[0]
    a_refs = refs[1:1 + n_terms]
    w_refs = refs[1 + n_terms:1 + 2 * n_terms]
    o_ref = refs[1 + 2 * n_terms]
    acc = x_ref[...]
    for a_ref, w_ref in zip(a_refs, w_refs):
        acc = acc + _dot(_bf(a_ref[...]), w_ref[...])
    o_ref[...] = acc


def _matmul_residual(x, terms, *, tm):
    t, d = x.shape
    acts = [a for a, _ in terms]
    ws = [w for _, w in terms]
    in_specs = [pl.BlockSpec((tm, d), lambda i: (i, 0))]
    in_specs += [pl.BlockSpec((tm, a.shape[1]), lambda i: (i, 0)) for a in acts]
    in_specs += [pl.BlockSpec(w.shape, lambda i: (0, 0)) for w in ws]
    return pl.pallas_call(
        functools.partial(_mmr_kernel, n_terms=len(terms)),
        grid=(t // tm,),
        in_specs=in_specs,
        out_specs=pl.BlockSpec((tm, d), lambda i: (i, 0)),
        out_shape=jax.ShapeDtypeStruct((t, d), F32),
        compiler_params=_params("parallel"),
        name="matmul_residual",
    )(x, *acts, *ws)


EV_GQ, EV_GK, EV_GV, EV_GG = 0, 256, 512, 1024
EV_RQ, EV_RK, EV_RV, EV_RG = 1536, 1792, 2048, 2560
EV_GA = 3072
EV_COLS = 3200
EV_BLOCK = 256
PAIR = 2


def _gate_and_norm(o, gate, gain=None):
    return _rms(o, gain) * jax.nn.silu(gate)


def _even_prompt_kernel(x_ref, gain_ref, win_ref, wout_ref, cos_ref, sin_ref, ld_ref, wgate_ref, bgate_ref, gnorm_ref,
                        s0g_ref, s0r_ref, o_ref, sg_ref, sr_ref, proj_sc, mix_sc, o_sc, *, chunk, n_chunks):
    @pl.when(pl.program_id(1) == 0)
    def _():
        sg_ref[...] = s0g_ref[...]
        sr_ref[...] = s0r_ref[...]

    c = chunk
    lt = c * n_chunks
    blk = min(EV_BLOCK, lt)
    c_shift = c.bit_length() - 1
    x = x_ref[0]
    proj_sc[...] = _dot(_bf(_rms(x, gain_ref[...])), win_ref[...])

    def cols(a, b):
        return proj_sc[:, a:b]

    log_a = _log_sigmoid(_dot(_bf(cols(EV_GA, EV_COLS)), wgate_ref[...]) + bgate_ref[...]) * (1.0 / GLA_TAU)
    tril = _bf((_iota((c, c), 1) <= _iota((c, c), 0)).astype(F32))
    cum_parts = [_dot_exact_lhs(tril, log_a[i * c:(i + 1) * c]) for i in range(n_chunks)]
    tots = [p[c - 1:c] for p in cum_parts]
    cum = jnp.concatenate(cum_parts, axis=0)
    tot_b = jnp.concatenate([jnp.broadcast_to(t, (c, 256)) for t in tots], axis=0)
    k = cols(EV_GK, EV_GK + 256)
    gla = (cols(EV_GQ, EV_GQ + 256) * (GLA_DK ** -0.5) * jnp.exp(cum), k * jnp.exp(-cum), k * jnp.exp(tot_b - cum))

    ld = ld_ref[...]
    tpos = ((_iota((lt, 1), 0) & (c - 1)) + 1).astype(F32)
    cum_r = tpos * ld
    tot_r = float(c) * ld
    cos, sin = _lane_tile(cos_ref[...], PAIR), _lane_tile(sin_ref[...], PAIR)
    rq = cols(EV_RQ, EV_RQ + 256)
    rk = cols(EV_RK, EV_RK + 256)
    q_rot = rq * cos + _swap_pairs(rq) * sin
    k_rot = (rk * cos + _swap_pairs(rk) * sin) * (RET_DK ** -0.5)
    ret = (q_rot * jnp.exp(cum_r), k_rot * jnp.exp(-cum_r), k_rot * jnp.exp(tot_r - cum_r))

    tot_rows = jnp.concatenate(tots + [tot_r, jnp.zeros((LANES - n_chunks - 1, 256), F32)], axis=0)
    decay_cols = jnp.exp(jnp.transpose(tot_rows))

    row = _iota((blk, blk), 0)
    col = _iota((blk, blk), 1)
    blk_mask = ((row >> c_shift) == (col >> c_shift)) & (col <= row)
    lo = (_iota((lt, LANES), 1) < GLA_DK)
    gnorm = gnorm_ref[...]

    mixers = ((gla, EV_GV, EV_GG, sg_ref, 0, gnorm, lambda i: i),
              (ret, EV_RV, EV_RG, sr_ref, 512, None, lambda i: n_chunks))
    for m, ((q_dec, k_inv, k_dec), v_col, g_col, s_ref, out_col, gain, decay_col_of) in enumerate(mixers):
        for p in range(GLA_HEADS // PAIR):
            lanes = slice(p * LANES, (p + 1) * LANES)
            q_pair = q_dec[:, lanes]
            q_masked = [_bf(jnp.where(lo, q_pair, 0.0)), _bf(jnp.where(lo, 0.0, q_pair))]
            ki = _bf(k_inv[:, lanes])
            kd = _bf(k_dec[:, lanes])
            v_pair = _bf(cols(v_col + p * PAIR * GLA_DV, v_col + (p + 1) * PAIR * GLA_DV))
            for e in range(PAIR):
                slot = (m * (GLA_HEADS // PAIR) + p) * PAIR + e
                for r0 in range(0, lt, blk):
                    rs = slice(r0, r0 + blk)
                    scores = jnp.where(blk_mask, _dot_nt(q_masked[e][rs], ki[rs]), 0.0)
                    o_sc[slot, rs, :] = _dot(_bf(scores), v_pair[rs, e * GLA_DV:(e + 1) * GLA_DV])
            state = s_ref[0, p * PAIR:(p + 1) * PAIR].reshape(PAIR * GLA_DK, GLA_DV)
            for i in range(n_chunks):
                rs = slice(i * c, (i + 1) * c)
                q_stack = jnp.concatenate([q_masked[0][rs], q_masked[1][rs]], axis=0)
                o_inter = _dot(q_stack, _bf(state))
                kv = _dot_tn(kd[rs], v_pair[rs])
                kv = jnp.concatenate([kv[:GLA_DK, :GLA_DV], kv[GLA_DK:, GLA_DV:]], axis=0)
                ci = decay_col_of(i)
                state = state * decay_cols[p * LANES:(p + 1) * LANES, ci:ci + 1] + kv
                for e in range(PAIR):
                    slot = (m * (GLA_HEADS // PAIR) + p) * PAIR + e
                    o_sc[slot, rs, :] += o_inter[e * c:(e + 1) * c]
            s_ref[0, p * PAIR:(p + 1) * PAIR] = state.reshape(PAIR, GLA_DK, GLA_DV)
            for e in range(PAIR):
                h = p * PAIR + e
                slot = (m * (GLA_HEADS // PAIR) + p) * PAIR + e
                gate = cols(g_col + h * GLA_DV, g_col + (h + 1) * GLA_DV)
                mix_sc[:, out_col + h * GLA_DV:out_col + (h + 1) * GLA_DV] = _gate_and_norm(o_sc[slot], gate, gain)
    o_ref[0] = x + _dot(_bf(mix_sc[...]), wout_ref[...])


def _even_prompt(x, gain, w_in, w_out, cos, sin, ld_row, wgate, bgate, gnorm, s0g, s0r, *, lt):
    b, l, d = x.shape
    chunk = math.gcd(l, LA_CHUNK)
    st_spec = pl.BlockSpec((1, GLA_HEADS, GLA_DK, GLA_DV), lambda i, j: (i, 0, 0, 0))
    io_spec = pl.BlockSpec((1, lt, d), lambda i, j: (i, j, 0))
    full = lambda a: pl.BlockSpec(a.shape, lambda i, j: (0,) * a.ndim)
    return pl.pallas_call(
        functools.partial(_even_prompt_kernel, chunk=chunk, n_chunks=lt // chunk),
        grid=(b, l // lt),
        in_specs=[
            io_spec, full(gain), full(w_in), full(w_out),
            pl.BlockSpec((lt, LANES), lambda i, j: (j, 0)),
            pl.BlockSpec((lt, LANES), lambda i, j: (j, 0)),
            full(ld_row), full(wgate), full(bgate), full(gnorm), st_spec, st_spec,
        ],
        out_specs=[io_spec, st_spec, st_spec],
        out_shape=[jax.ShapeDtypeStruct(x.shape, F32),
                   jax.ShapeDtypeStruct(s0g.shape, F32), jax.ShapeDtypeStruct(s0r.shape, F32)],
        scratch_shapes=[pltpu.VMEM((lt, EV_COLS), F32), pltpu.VMEM((lt, d), F32),
                        pltpu.VMEM((GLA_HEADS + RET_HEADS, lt, GLA_DV), F32)],
        compiler_params=_params("parallel", "arbitrary"),
        name="even_prompt",
    )(x, gain, w_in, w_out, cos, sin, ld_row, wgate, bgate, gnorm, s0g, s0r)


def _even_sample_kernel(proj_ref, cos_ref, sin_ref, ld_ref, wgate_ref, bgate_ref, gnorm_ref, s0g_ref, s0r_ref,
                        mix_ref, sg_ref, sr_ref, *, bg, ls):
    r = bg * ls
    ls_shift = ls.bit_length() - 1
    dk_shift = GLA_DK.bit_length() - 1
    n_exp = bg * GLA_DK
    row_seq = _iota((r, r), 0) >> ls_shift
    col_seq = _iota((r, r), 1) >> ls_shift
    same = row_seq == col_seq
    seg = same & (_iota((r, r), 1) <= _iota((r, r), 0))
    seg_f = seg.astype(F32)
    same_f = same.astype(F32)
    tile_b = _bf(((_iota((GLA_DK, n_exp), 1) & (GLA_DK - 1)) == _iota((GLA_DK, n_exp), 0)).astype(F32))
    tile_t_f = ((_iota((n_exp, GLA_DK), 0) & (GLA_DK - 1)) == _iota((n_exp, GLA_DK), 1)).astype(F32)
    tile_t_b = _bf(tile_t_f)
    q_mask = (_iota((r, n_exp), 0) >> ls_shift) == (_iota((r, n_exp), 1) >> dk_shift)
    k_mask = (_iota((n_exp, r), 0) >> dk_shift) == (_iota((n_exp, r), 1) >> ls_shift)
    tpos = ((_iota((r, 1), 0) & (ls - 1)) + 1).astype(F32)
    ld = ld_ref[...]
    cum_r = tpos * ld
    tot_r = float(ls) * ld
    r_dec = jnp.exp(tot_r)
    gnorm = gnorm_ref[...]

    def mixer(q_dec, k_inv, k_dec, v_col, g_col, s0_ref, s_ref, out_col, decay_of, gain):
        for h in range(GLA_HEADS):
            sl = slice(h * GLA_DK, (h + 1) * GLA_DK)
            v = _bf(proj_ref[:, v_col + h * GLA_DV:v_col + (h + 1) * GLA_DV])
            qd = _bf(q_dec[:, sl])
            scores = jnp.where(seg, _dot_nt(qd, _bf(k_inv[:, sl])), 0.0)
            state = s0_ref[:, h].reshape(n_exp, GLA_DV)
            q_exp = _bf(jnp.where(q_mask, _dot(qd, tile_b), 0.0))
            o = _dot(_bf(scores), v) + _dot(q_exp, _bf(state))
            k_exp = _bf(jnp.where(k_mask, _dot_nt(tile_t_b, _bf(k_dec[:, sl])), 0.0))
            new_state = state * decay_of(h, sl) + _dot(k_exp, v)
            s_ref[:, h] = new_state.reshape(bg, GLA_DK, GLA_DV)
            gate = proj_ref[:, g_col + h * GLA_DV:g_col + (h + 1) * GLA_DV]
            mix_ref[:, out_col + h * GLA_DV:out_col + (h + 1) * GLA_DV] = _gate_and_norm(o, gate, gain)

    log_a = _log_sigmoid(_dot(_bf(proj_ref[:, EV_GA:EV_COLS]), wgate_ref[...]) + bgate_ref[...]) * (1.0 / GLA_TAU)
    cum = _dot_f32(seg_f, log_a)
    tot = _dot_f32(same_f, log_a)
    k = proj_ref[:, EV_GK:EV_GK + 256]

    def gla_decay(h, sl):
        la_exp = jnp.where(k_mask, _dot_nt_f32(tile_t_f, log_a[:, sl]), 0.0)
        return jnp.exp(jnp.sum(la_exp, axis=-1, keepdims=True))

    mixer(proj_ref[:, EV_GQ:EV_GQ + 256] * (GLA_DK ** -0.5) * jnp.exp(cum), k * jnp.exp(-cum), k * jnp.exp(tot - cum),
          EV_GV, EV_GG, s0g_ref, sg_ref, 0, gla_decay, gnorm)

    cos = _lane_tile(cos_ref[...], PAIR)
    sin = _lane_tile(sin_ref[...], PAIR)
    rq = proj_ref[:, EV_RQ:EV_RQ + 256]
    rk = proj_ref[:, EV_RK:EV_RK + 256]
    q_rot = rq * cos + _swap_pairs(rq) * sin
    k_rot = (rk * cos + _swap_pairs(rk) * sin) * (RET_DK ** -0.5)

    def ret_decay(h, sl):
        return r_dec[:, h * RET_DK:h * RET_DK + 1]

    mixer(q_rot * jnp.exp(cum_r), k_rot * jnp.exp(-cum_r), k_rot * jnp.exp(tot_r - cum_r),
          EV_RV, EV_RG, s0r_ref, sr_ref, 512, ret_decay, None)


def _even_sample(proj, cos, sin, ld_row, wgate, bgate, gnorm, s0g, s0r, *, bg, ls):
    t = proj.shape[0]
    n_b = t // ls
    r = bg * ls
    st_spec = pl.BlockSpec((bg, GLA_HEADS, GLA_DK, GLA_DV), lambda i: (i, 0, 0, 0))
    full = lambda a: pl.BlockSpec(a.shape, lambda i: (0,) * a.ndim)
    return pl.pallas_call(
        functools.partial(_even_sample_kernel, bg=bg, ls=ls),
        grid=(n_b // bg,),
        in_specs=[
            pl.BlockSpec((r, EV_COLS), lambda i: (i, 0)),
            full(cos), full(sin), full(ld_row), full(wgate), full(bgate), full(gnorm), st_spec, st_spec,
        ],
        out_specs=[pl.BlockSpec((r, D_MODEL), lambda i: (i, 0)), st_spec, st_spec],
        out_shape=[jax.ShapeDtypeStruct((t, D_MODEL), F32),
                   jax.ShapeDtypeStruct(s0g.shape, F32), jax.ShapeDtypeStruct(s0r.shape, F32)],
        compiler_params=_params("parallel"),
        name="even_sample",
    )(proj, cos, sin, ld_row, wgate, bgate, gnorm, s0g, s0r)


def _s5_prep_kernel(are_ref, aim_ref, lstep_ref, bre_ref, bim_ref, abre_ref, abim_ref, bbre_ref, bbim_ref):
    a_re, a_im = are_ref[...], aim_ref[...]
    step = jnp.exp(lstep_ref[...])
    mag = jnp.exp(a_re * step)
    ab_re = mag * jnp.cos(a_im * step)
    ab_im = mag * jnp.sin(a_im * step)
    den = a_re * a_re + a_im * a_im
    coef_re = ((ab_re - 1.0) * a_re + ab_im * a_im) / den
    coef_im = (ab_im * a_re - (ab_re - 1.0) * a_im) / den
    b_re, b_im = bre_ref[...], bim_ref[...]
    abre_ref[...] = ab_re
    abim_ref[...] = ab_im
    bbre_ref[...] = coef_re * b_re - coef_im * b_im
    bbim_ref[...] = coef_re * b_im + coef_im * b_re


def _s5_prep(a_re, a_im, log_step, b_re, b_im):
    g, n = a_re.shape
    shp3 = jax.ShapeDtypeStruct((g, 1, n), F32)
    shpb = jax.ShapeDtypeStruct((g, S5_GROUP, n), F32)
    return pl.pallas_call(_s5_prep_kernel, out_shape=[shp3, shp3, shpb, shpb], name="s5_prep")(
        a_re.reshape(g, 1, n), a_im.reshape(g, 1, n), log_step.reshape(g, 1, 1),
        jnp.swapaxes(b_re, 1, 2), jnp.swapaxes(b_im, 1, 2))


S5_LANE_CHUNKS = S5_GROUPS * S5_STATE // LANES
S5_HALVES = 2


def _s5_layout(n_seq, lt):
    pack = max(1, SUBLANES // n_seq)
    pitch = lt + 4 if lt % SUBLANES == 0 else lt
    return pack, S5_LANE_CHUNKS // pack, pitch


def _s5_slot(c, n_groups):
    return c % n_groups, c // n_groups


def _s5_load_state(s0re_ref, s0im_ref, hre_ref, him_ref, n_seq, n_groups):
    for c in range(S5_LANE_CHUNKS):
        g, j = _s5_slot(c, n_groups)
        hre_ref[g, j * n_seq:(j + 1) * n_seq, :] = s0re_ref[:, c * LANES:(c + 1) * LANES]
        him_ref[g, j * n_seq:(j + 1) * n_seq, :] = s0im_ref[:, c * LANES:(c + 1) * LANES]


def _s5_store_state(sre_ref, sim_ref, hre_ref, him_ref, n_seq, n_groups):
    for c in range(S5_LANE_CHUNKS):
        g, j = _s5_slot(c, n_groups)
        sre_ref[:, c * LANES:(c + 1) * LANES] = hre_ref[g, j * n_seq:(j + 1) * n_seq, :]
        sim_ref[:, c * LANES:(c + 1) * LANES] = him_ref[g, j * n_seq:(j + 1) * n_seq, :]


def _s5_core(u, abre_ref, abim_ref, bbre_ref, bbim_ref, ccre_ref, ccim_ref, d_ref, wglu_ref, bglu_ref,
             xr_ref, xi_ref, hre_ref, him_ref, *, n_seq, lt, groups_per_pass, unroll):
    pack, n_groups, pitch = _s5_layout(n_seq, lt)
    rows = n_seq * lt
    per_half = S5_LANE_CHUNKS // S5_HALVES
    lanes_of = lambda c: slice(c * LANES, (c + 1) * LANES)
    slot_of = lambda c: _s5_slot(c, n_groups)

    def seq_rows(j, s):
        return slice((j * n_seq + s) * pitch, (j * n_seq + s) * pitch + lt)

    ub = _bf(u)
    u_cols = S5_WIDTH // S5_HALVES
    for half in range(S5_HALVES):
        uh = ub[:, half * u_cols:(half + 1) * u_cols]
        for x_ref, bb_ref in ((xr_ref, bbre_ref), (xi_ref, bbim_ref)):
            x = _dot(uh, bb_ref[half])
            for k in range(per_half):
                g, j = slot_of(half * per_half + k)
                if pitch == lt:
                    x_ref[g, j * rows:(j + 1) * rows, :] = x[:, lanes_of(k)]
                else:
                    for s in range(n_seq):
                        x_ref[g, seq_rows(j, s), :] = x[s * lt:(s + 1) * lt, lanes_of(k)]

    for g0 in range(0, n_groups, groups_per_pass):
        gs = list(range(g0, g0 + groups_per_pass))
        init = tuple(hre_ref[g] for g in gs) + tuple(him_ref[g] for g in gs)

        def step(t, carry, gs=gs):
            rws = pl.ds(t, pack * n_seq, stride=pitch)
            new_re, new_im = [], []
            for k, g in enumerate(gs):
                a_re, a_im = abre_ref[g], abim_ref[g]
                h_re, h_im = carry[k], carry[len(gs) + k]
                n_re = a_re * h_re - a_im * h_im + xr_ref[g, rws, :]
                n_im = a_re * h_im + a_im * h_re + xi_ref[g, rws, :]
                xr_ref[g, rws, :] = n_re
                xi_ref[g, rws, :] = n_im
                new_re.append(n_re)
                new_im.append(n_im)
            return tuple(new_re + new_im)

        fin = lax.fori_loop(0, lt, step, init, unroll=unroll)
        for k, g in enumerate(gs):
            hre_ref[g] = fin[k]
            him_ref[g] = fin[len(gs) + k]

    def gather(x_ref, half):
        cols = []
        for k in range(per_half):
            g, j = slot_of(half * per_half + k)
            if pitch == lt:
                cols.append(x_ref[g, j * rows:(j + 1) * rows, :])
            else:
                cols.append(jnp.concatenate([x_ref[g, seq_rows(j, s), :] for s in range(n_seq)], axis=0))
        return _bf(jnp.concatenate(cols, axis=1))

    y = jnp.concatenate([_dot(gather(xr_ref, half), ccre_ref[half]) - _dot(gather(xi_ref, half), ccim_ref[half])
                         for half in range(S5_HALVES)], axis=1) + d_ref[...] * u
    z = jax.nn.gelu(y, approximate=True)
    return z * jax.nn.sigmoid(_dot(_bf(z), wglu_ref[...]) + bglu_ref[...])


def _s5_kernel(u_ref, s0re_ref, s0im_ref, abre_ref, abim_ref, bbre_ref, bbim_ref, ccre_ref, ccim_ref,
               d_ref, wglu_ref, bglu_ref, out_ref, sre_ref, sim_ref, xr_ref, xi_ref, hre_ref, him_ref,
               *, n_seq, lt, groups_per_pass, unroll):
    n_groups = _s5_layout(n_seq, lt)[1]

    @pl.when(pl.program_id(0) == 0)
    def _():
        _s5_load_state(s0re_ref, s0im_ref, hre_ref, him_ref, n_seq, n_groups)

    out = _s5_core(u_ref[...].reshape(n_seq * lt, S5_WIDTH), abre_ref, abim_ref, bbre_ref, bbim_ref, ccre_ref, ccim_ref,
                   d_ref, wglu_ref, bglu_ref, xr_ref, xi_ref, hre_ref, him_ref,
                   n_seq=n_seq, lt=lt, groups_per_pass=groups_per_pass, unroll=unroll)
    out_ref[...] = out.reshape(out_ref.shape)

    @pl.when(pl.program_id(0) == pl.num_programs(0) - 1)
    def _():
        _s5_store_state(sre_ref, sim_ref, hre_ref, him_ref, n_seq, n_groups)


def _s5_tables(ab, n_seq, lt):
    pack, n_groups, _ = _s5_layout(n_seq, lt)
    tab = jnp.swapaxes(ab.reshape(pack, n_groups, 1, LANES), 0, 1)
    return jnp.broadcast_to(tab, (n_groups, pack, n_seq, LANES)).reshape(n_groups, pack * n_seq, LANES)


def _s5(proj3, s0_re, s0_im, ab_re, ab_im, bb_re, bb_im, cc_re, cc_im, d_row, wglu, bglu, *, n_seq, lt,
        groups_per_pass, unroll):
    nb, rows_b, _ = proj3.shape
    blk_rows = n_seq * lt // nb
    pack, n_groups, pitch = _s5_layout(n_seq, lt)
    ab_re, ab_im = _s5_tables(ab_re, n_seq, lt), _s5_tables(ab_im, n_seq, lt)
    full = lambda a: pl.BlockSpec(a.shape, lambda j: (0,) * a.ndim)
    io_spec = pl.BlockSpec((nb, blk_rows, S5_WIDTH), lambda j: (0, j, 0))
    x_scratch = pltpu.VMEM((n_groups, pack * n_seq * pitch, LANES), F32)
    h_scratch = pltpu.VMEM((n_groups, pack * n_seq, LANES), F32)
    return pl.pallas_call(
        functools.partial(_s5_kernel, n_seq=n_seq, lt=lt, groups_per_pass=groups_per_pass, unroll=unroll),
        grid=(rows_b // blk_rows,),
        in_specs=[io_spec, full(s0_re), full(s0_im), full(ab_re), full(ab_im), full(bb_re), full(bb_im),
                  full(cc_re), full(cc_im), full(d_row), full(wglu), full(bglu)],
        out_specs=[io_spec, full(s0_re), full(s0_im)],
        out_shape=[jax.ShapeDtypeStruct((nb, rows_b, S5_WIDTH), F32),
                   jax.ShapeDtypeStruct(s0_re.shape, F32), jax.ShapeDtypeStruct(s0_im.shape, F32)],
        scratch_shapes=[x_scratch, x_scratch, h_scratch, h_scratch],
        compiler_params=_params("arbitrary"),
        name="s5",
    )(proj3, s0_re, s0_im, ab_re, ab_im, bb_re, bb_im, cc_re, cc_im, d_row, wglu, bglu)


OD_Q_BLOCK = 1
OD_KV_BLOCK = 4
OD_COLS = 1280
ROPE_HALF = SWA_HD // 8
SWA_GROUP = SWA_HEADS // SWA_KV_HEADS


def _half_lanes(shape):
    return (_iota(shape, 1) & (LANES - 1)) < SWA_HD


def _pair_rms_scale(x):
    same_head = (_iota((LANES, LANES), 0) >= SWA_HD) == (_iota((LANES, LANES), 1) >= SWA_HD)
    sums = _dot_exact_rhs(x * x, _bf(same_head.astype(F32)), terms=2)
    return lax.rsqrt(sums * (1.0 / SWA_HD) + NORM_EPS)


def _swa_qk(xq, xk, qg, kg, cq, sq):
    xq_g = xq * qg
    n_pairs = SWA_HEADS // 2
    q_rot = xq_g * _lane_tile(cq, n_pairs) + _rope_partner(xq_g, SWA_HD, ROPE_HALF) * _lane_tile(sq, n_pairs)
    xk_g = xk * kg
    k_rot = xk_g * cq + _rope_partner(xk_g, SWA_HD, ROPE_HALF) * sq
    q_pairs = [q_rot[:, j * LANES:(j + 1) * LANES] * _pair_rms_scale(xq[:, j * LANES:(j + 1) * LANES])
               for j in range(SWA_HEADS // 2)]
    return q_pairs, k_rot * _pair_rms_scale(xk)


def _swa_query_stack(q_pairs, kh):
    lo = _half_lanes(q_pairs[0].shape)
    keep = lo if kh == 0 else jnp.logical_not(lo)
    parts = []
    for g in range(SWA_GROUP):
        hq = kh * SWA_GROUP + g
        pair = q_pairs[hq // 2]
        src = pair if hq % 2 == kh else pltpu.roll(pair, SWA_HD, 1)
        parts.append(jnp.where(keep, src, 0.0))
    return _bf(jnp.concatenate(parts, axis=0))


def _swa_merge_heads(o, kh, rows):
    lo = _half_lanes((rows, LANES))
    pairs = []
    for p in range(SWA_GROUP // 2):
        even, odd = o[2 * p * rows:(2 * p + 1) * rows], o[(2 * p + 1) * rows:(2 * p + 2) * rows]
        if kh == 0:
            pairs.append(jnp.where(lo, even, pltpu.roll(odd, SWA_HD, 1)))
        else:
            pairs.append(jnp.where(lo, pltpu.roll(even, SWA_HD, 1), odd))
    return pairs


def _swa_block(xq, xk, v_cur, k_prev, v_prev, mask, qg, kg, cq, sq, sink_ref):
    w = SWA_WINDOW
    ones_col = jnp.ones((2 * w, LANES), BF16)
    q_pairs, k_cur = _swa_qk(xq, xk, qg, kg, cq, sq)
    k_ext = _bf(jnp.concatenate([k_prev, k_cur], axis=0))
    v_ext = _bf(jnp.concatenate([v_prev, v_cur], axis=0))
    out_pairs = []
    for kh in range(SWA_KV_HEADS):
        s_all = _dot_nt(_swa_query_stack(q_pairs, kh), k_ext)
        weights, sink_terms = [], []
        for g in range(SWA_GROUP):
            s = jnp.where(mask, s_all[g * w:(g + 1) * w] * (SWA_HD ** -0.5), -jnp.inf)
            sink = sink_ref[kh * SWA_GROUP + g:kh * SWA_GROUP + g + 1, :]
            m = jnp.maximum(jnp.broadcast_to(jnp.max(s, axis=-1, keepdims=True), (w, LANES)), sink)
            weights.append(_bf(jnp.exp(s - jnp.concatenate([m, m], axis=1))))
            sink_terms.append(jnp.exp(sink - m))
        weights = jnp.concatenate(weights, axis=0)
        den = _dot(weights, ones_col) + jnp.concatenate(sink_terms, axis=0)
        out_pairs += _swa_merge_heads(_dot(weights, v_ext) / den, kh, w)
    return out_pairs, k_cur


def _odd_prompt_kernel(x_ref, gain_ref, win_ref, wout_ref, s0re_ref, s0im_ref, abre_ref, abim_ref, bbre_ref, bbim_ref,
                       ccre_ref, ccim_ref, d_ref, wglu_ref, bglu_ref, cq_ref, sq_ref, qg_ref, kg_ref, sink_ref,
                       o_ref, sre_ref, sim_ref, ck_ref, cv_ref,
                       proj_sc, mix_sc, xr_ref, xi_ref, hre_ref, him_ref, kprev_sc, vprev_sc,
                       *, n_seq, groups_per_pass, unroll):
    w = SWA_WINDOW
    rows = n_seq * w
    n_groups = _s5_layout(n_seq, w)[1]
    step = pl.program_id(0)

    @pl.when(step == 0)
    def _():
        _s5_load_state(s0re_ref, s0im_ref, hre_ref, him_ref, n_seq, n_groups)
        kprev_sc[...] = jnp.zeros_like(kprev_sc)
        vprev_sc[...] = jnp.zeros_like(vprev_sc)

    x = x_ref[...].reshape(rows, D_MODEL)
    proj_sc[...] = _dot(_bf(_rms(x, gain_ref[...])), win_ref[...])
    mix_sc[:, 0:S5_WIDTH] = _s5_core(proj_sc[:, 0:S5_WIDTH], abre_ref, abim_ref, bbre_ref, bbim_ref, ccre_ref, ccim_ref,
                                     d_ref, wglu_ref, bglu_ref, xr_ref, xi_ref, hre_ref, him_ref,
                                     n_seq=n_seq, lt=w, groups_per_pass=groups_per_pass, unroll=unroll)

    t_idx = _iota((w, 2 * w), 0)
    s_idx = _iota((w, 2 * w), 1)
    mask = (s_idx > t_idx) & (s_idx <= t_idx + w) & (s_idx >= jnp.where(step == 0, w, 0))
    q0, k0, v0 = OD_Q_BLOCK * 512, OD_KV_BLOCK * 256, OD_KV_BLOCK * 256 + LANES
    for s in range(n_seq):
        rs = slice(s * w, (s + 1) * w)
        v_cur = proj_sc[rs, v0:v0 + LANES]
        pairs, k_cur = _swa_block(proj_sc[rs, q0:q0 + 512], proj_sc[rs, k0:k0 + LANES], v_cur, kprev_sc[s], vprev_sc[s],
                                  mask, qg_ref[...], kg_ref[...], cq_ref[...], sq_ref[...], sink_ref)
        for i, pair in enumerate(pairs):
            mix_sc[rs, S5_WIDTH + i * LANES:S5_WIDTH + (i + 1) * LANES] = pair
        kprev_sc[s] = k_cur
        vprev_sc[s] = v_cur
    o_ref[...] = (x + _dot(_bf(mix_sc[...]), wout_ref[...])).reshape(o_ref.shape)

    @pl.when(step == pl.num_programs(0) - 1)
    def _():
        _s5_store_state(sre_ref, sim_ref, hre_ref, him_ref, n_seq, n_groups)
        ck_ref[...] = kprev_sc[...]
        cv_ref[...] = vprev_sc[...]


def _odd_prompt(x, gain, w_in, w_out, s0_re, s0_im, ab_re, ab_im, bb_re, bb_im, cc_re, cc_im, d_row, wglu, bglu,
                cq, sq, qg, kg, sink_rows, *, groups_per_pass, unroll):
    b, l, d = x.shape
    w = SWA_WINDOW
    pack, n_groups, pitch = _s5_layout(b, w)
    ab_re, ab_im = _s5_tables(ab_re, b, w), _s5_tables(ab_im, b, w)
    full = lambda a: pl.BlockSpec(a.shape, lambda j: (0,) * a.ndim)
    io_spec = pl.BlockSpec((b, w, d), lambda j: (0, j, 0))
    tab_spec = pl.BlockSpec((w, LANES), lambda j: (j, 0))
    cache_shape = jax.ShapeDtypeStruct((b, w, LANES), F32)
    x_scratch = pltpu.VMEM((n_groups, pack * b * pitch, LANES), F32)
    h_scratch = pltpu.VMEM((n_groups, pack * b, LANES), F32)
    kv_scratch = pltpu.VMEM((b, w, LANES), F32)
    return pl.pallas_call(
        functools.partial(_odd_prompt_kernel, n_seq=b, groups_per_pass=groups_per_pass, unroll=unroll),
        grid=(l // w,),
        in_specs=[io_spec, full(gain), full(w_in), full(w_out), full(s0_re), full(s0_im), full(ab_re), full(ab_im),
                  full(bb_re), full(bb_im), full(cc_re), full(cc_im), full(d_row), full(wglu), full(bglu),
                  tab_spec, tab_spec, full(qg), full(kg), full(sink_rows)],
        out_specs=[io_spec, full(s0_re), full(s0_im), pl.BlockSpec((b, w, LANES), lambda j: (0, 0, 0)),
                   pl.BlockSpec((b, w, LANES), lambda j: (0, 0, 0))],
        out_shape=[jax.ShapeDtypeStruct(x.shape, F32), jax.ShapeDtypeStruct(s0_re.shape, F32),
                   jax.ShapeDtypeStruct(s0_im.shape, F32), cache_shape, cache_shape],
        scratch_shapes=[pltpu.VMEM((b * w, OD_COLS), F32), pltpu.VMEM((b * w, d), F32),
                        x_scratch, x_scratch, h_scratch, h_scratch, kv_scratch, kv_scratch],
        compiler_params=_params("arbitrary"),
        name="odd_prompt",
    )(x, gain, w_in, w_out, s0_re, s0_im, ab_re, ab_im, bb_re, bb_im, cc_re, cc_im, d_row, wglu, bglu,
      cq, sq, qg, kg, sink_rows)


def _swa_sample_kernel(q_ref, kv_ref, ck_ref, cv_ref, cq_ref, sq_ref, qg_ref, kg_ref, sink_ref,
                       o_ref, nk_ref, nv_ref, *, bg, ls):
    w = SWA_WINDOW
    r = bg * ls
    ls_shift = ls.bit_length() - 1
    w_shift = w.bit_length() - 1
    rows_g = SWA_GROUP * r
    v_new = kv_ref[:, LANES:2 * LANES]
    q_pairs, k_new = _swa_qk(q_ref[...], kv_ref[:, 0:LANES], qg_ref[...], kg_ref[...], cq_ref[...], sq_ref[...])
    k_cache = _bf(ck_ref[...].reshape(bg * w, LANES))
    v_cache = _bf(cv_ref[...].reshape(bg * w, LANES))

    row = _iota((rows_g, bg * w), 0) & (r - 1)
    col = _iota((rows_g, bg * w), 1)
    mask_c = ((row >> ls_shift) == (col >> w_shift)) & ((col & (w - 1)) > (row & (ls - 1)))
    row_n = _iota((rows_g, r), 0) & (r - 1)
    col_n = _iota((rows_g, r), 1)
    mask_n = ((row_n >> ls_shift) == (col_n >> ls_shift)) & ((col_n & (ls - 1)) <= (row_n & (ls - 1)))
    for kh in range(SWA_KV_HEADS):
        q_stack = _swa_query_stack(q_pairs, kh)
        s_c = jnp.where(mask_c, _dot_nt(q_stack, k_cache) * (SWA_HD ** -0.5), -jnp.inf)
        s_n = jnp.where(mask_n, _dot_nt(q_stack, _bf(k_new)) * (SWA_HD ** -0.5), -jnp.inf)
        sink = jnp.concatenate([jnp.broadcast_to(sink_ref[kh * SWA_GROUP + g:kh * SWA_GROUP + g + 1, :], (r, LANES))
                                for g in range(SWA_GROUP)], axis=0)
        row_max = jnp.maximum(jnp.max(s_c, axis=-1, keepdims=True), jnp.max(s_n, axis=-1, keepdims=True))
        m = jnp.maximum(jnp.broadcast_to(row_max, (rows_g, LANES)), sink)
        e_c = _bf(jnp.exp(s_c - jnp.concatenate([m] * bg, axis=1)))
        e_n = _bf(jnp.exp(s_n - m[:, :r]))
        den = (_dot(e_c, jnp.ones((bg * w, LANES), BF16)) + _dot(e_n, jnp.ones((r, LANES), BF16))
               + jnp.exp(sink - m))
        o = (_dot(e_c, v_cache) + _dot(e_n, _bf(v_new))) / den
        for i, pair in enumerate(_swa_merge_heads(o, kh, r)):
            col_i = kh * (SWA_GROUP // 2) + i
            o_ref[:, col_i * LANES:(col_i + 1) * LANES] = pair

    nk_ref[:, 0:w - ls, :] = ck_ref[:, ls:w, :]
    nv_ref[:, 0:w - ls, :] = cv_ref[:, ls:w, :]
    for b in range(bg):
        nk_ref[b, w - ls:w, :] = k_new[b * ls:(b + 1) * ls, :]
        nv_ref[b, w - ls:w, :] = v_new[b * ls:(b + 1) * ls, :]


def _swa_sample(proj, cache_k, cache_v, cq, sq, qg, kg, sink_rows, *, bg, ls):
    t = proj.shape[0]
    w = SWA_WINDOW
    r = bg * ls
    full = lambda a: pl.BlockSpec(a.shape, lambda i: (0,) * a.ndim)
    cache_spec = pl.BlockSpec((bg, w, LANES), lambda i: (i, 0, 0))
    return pl.pallas_call(
        functools.partial(_swa_sample_kernel, bg=bg, ls=ls),
        grid=(t // r,),
        in_specs=[
            pl.BlockSpec((r, 512), lambda i: (i, OD_Q_BLOCK)),
            pl.BlockSpec((r, 256), lambda i: (i, OD_KV_BLOCK)),
            cache_spec, cache_spec, full(cq), full(sq), full(qg), full(kg), full(sink_rows),
        ],
        out_specs=[pl.BlockSpec((r, 512), lambda i: (i, 0)), cache_spec, cache_spec],
        out_shape=[jax.ShapeDtypeStruct((t, 512), F32),
                   jax.ShapeDtypeStruct(cache_k.shape, F32), jax.ShapeDtypeStruct(cache_v.shape, F32)],
        compiler_params=_params("parallel"),
        name="swa_sample",
    )(proj, proj, cache_k, cache_v, cq, sq, qg, kg, sink_rows)


def _mem_prompt_kernel(x_ref, g_ref, wq_ref, qg_ref, k_ref, v_ref, wo_ref, o_ref):
    x = x_ref[0]
    xn = _bf(_rms(x, g_ref[...]))
    k_all = _bf(_mem_rows(k_ref, 0))
    v_all = _bf(_mem_rows(v_ref, 0))
    acc = x
    for h in range(MEM_HEADS):
        sl = slice(h * MEM_HD, (h + 1) * MEM_HD)
        q = _bf(_rms(_dot(xn, wq_ref[:, sl]), qg_ref[...]))
        s = _dot_nt(q, k_all[:, sl]) * (MEM_HD ** -0.5)
        e = jnp.exp(s - jnp.max(s, axis=-1, keepdims=True))
        p = e / jnp.sum(e, axis=-1, keepdims=True)
        o = _dot(_bf(p), v_all[:, sl])
        acc = acc + _dot(_bf(o), wo_ref[sl, :])
    o_ref[0] = acc


def _mem_prompt(x, gain, wq, q_gain, k, v, wo, *, layer, lt):
    b, l, d = x.shape
    k, v = _mem_flat_view(k), _mem_flat_view(v)
    kv_spec = pl.BlockSpec((None, 1, k.shape[2], LANES), lambda i, j: (layer, i, 0, 0))
    io_spec = pl.BlockSpec((1, lt, d), lambda i, j: (i, j, 0))
    full = lambda a: pl.BlockSpec(a.shape, lambda i, j: (0,) * a.ndim)
    resident = lambda a: pl.BlockSpec(a.shape, lambda i, j: (0,) * a.ndim, pipeline_mode=pl.Buffered(1))
    return pl.pallas_call(
        _mem_prompt_kernel,
        grid=(b, l // lt),
        in_specs=[io_spec, full(gain), resident(wq), full(q_gain), kv_spec, kv_spec, resident(wo)],
        out_specs=io_spec,
        out_shape=jax.ShapeDtypeStruct(x.shape, F32),
        compiler_params=_params("parallel", "arbitrary"),
        name="mem_prompt",
    )(x, gain, wq, q_gain, k, v, wo)


def _memkv_kernel(mem_ref, g_ref, wk_ref, wv_ref, kg_ref, k_out, v_out):
    xn = _bf(_rms(mem_ref[...], g_ref[...]))
    nb, n_mem, heads, hd = k_out.shape
    for h in range(heads):
        sl = slice(h * hd, (h + 1) * hd)
        k_out[:, :, h, :] = _rms(_dot(xn, wk_ref[:, sl]), kg_ref[...]).reshape(nb, n_mem, hd)
        v_out[:, :, h, :] = _dot(xn, wv_ref[:, sl]).reshape(nb, n_mem, hd)


def _memkv(mem, m_gain, wk, wv, k_gain):
    nb, n_mem, d = mem.shape
    depth = wk.shape[0]
    out_shape = jax.ShapeDtypeStruct((depth, nb, n_mem, MEM_HEADS, MEM_HD), F32)
    per_layer = lambda s: pl.BlockSpec((None,) + s, lambda l: (l,) + (0,) * len(s))
    return pl.pallas_call(
        _memkv_kernel,
        grid=(depth,),
        in_specs=[pl.BlockSpec((nb * n_mem, d), lambda l: (0, 0)), per_layer((1, d)), per_layer((d, d)),
                  per_layer((d, d)), per_layer((1, MEM_HD))],
        out_specs=[per_layer((nb, n_mem, MEM_HEADS, MEM_HD)), per_layer((nb, n_mem, MEM_HEADS, MEM_HD))],
        out_shape=[out_shape, out_shape],
        compiler_params=_params("arbitrary"),
        name="memkv",
    )(mem.reshape(nb * n_mem, d), m_gain, wk, wv, k_gain)


def _mem_sample_kernel(q_ref, k_ref, v_ref, o_ref, *, bs, ls):
    r = bs * ls
    per_seq = MEM_HEADS * ls
    n_exp = bs * per_seq
    n_mem = k_ref.shape[1] // MEM_ROW_GROUP
    hd_shift = MEM_HD.bit_length() - 1
    ls_shift = ls.bit_length() - 1
    seq_shift = per_seq.bit_length() - 1
    mem_shift = n_mem.bit_length() - 1
    qb = _bf(q_ref[...])
    e_row = _iota((n_exp, r), 0)
    sel = _bf((((e_row >> seq_shift) << ls_shift) + (e_row & (ls - 1)) == _iota((n_exp, r), 1)).astype(F32))
    head_mask = (((_iota((n_exp, D_MODEL), 0) >> ls_shift) & (MEM_HEADS - 1))
                 == (_iota((n_exp, D_MODEL), 1) >> hd_shift))
    q_exp = _bf(jnp.where(head_mask, _dot(sel, qb), 0.0))
    k_all = _bf(jnp.concatenate([_mem_rows(k_ref, b) for b in range(bs)], axis=0))
    v_all = _bf(jnp.concatenate([_mem_rows(v_ref, b) for b in range(bs)], axis=0))
    own = ((_iota((bs * n_mem, n_exp), 0) >> mem_shift) == (_iota((bs * n_mem, n_exp), 1) >> seq_shift))
    own = own.reshape(bs, n_mem, n_exp)
    s = (_dot_nt(k_all, q_exp) * (MEM_HD ** -0.5)).reshape(bs, n_mem, n_exp)
    s = jnp.where(own, s, -1e30)
    e = jnp.where(own, jnp.exp(s - jnp.max(s, axis=1, keepdims=True)), 0.0)
    den = jnp.sum(e, axis=1, keepdims=True) + jnp.where(jnp.any(own, axis=1, keepdims=True), 0.0, 1.0)
    p = _bf((e / den).reshape(bs * n_mem, n_exp))
    o_all = jnp.where(head_mask, _dot_tn(p, v_all), 0.0)
    o_ref[...] = _dot_tn(sel, _bf(o_all))


MEM_LANE_TILES = MEM_HD // LANES
MEM_ROW_GROUP = MEM_HEADS * MEM_LANE_TILES


def _mem_rows(ref, b):
    n_mem = ref.shape[1] // MEM_ROW_GROUP
    return jnp.concatenate([ref[b, pl.ds(lt * MEM_HEADS + h, n_mem, stride=MEM_ROW_GROUP), :]
                            for h in range(MEM_HEADS) for lt in range(MEM_LANE_TILES)], axis=1)


def _mem_flat_view(a):
    depth, nb, n_mem, heads, hd = a.shape
    a = a.reshape(depth, nb, n_mem, heads, hd // LANES, LANES).transpose(0, 1, 2, 4, 3, 5)
    return a.reshape(depth, nb, n_mem * MEM_ROW_GROUP, LANES)


def _mem_sample(q, k, v, *, layer, bs, ls):
    t, d = q.shape
    k, v = _mem_flat_view(k), _mem_flat_view(v)
    r = bs * ls
    kv_spec = pl.BlockSpec((None, bs, k.shape[2], LANES), lambda i: (layer, i, 0, 0))
    io_spec = pl.BlockSpec((r, d), lambda i: (i, 0))
    return pl.pallas_call(
        functools.partial(_mem_sample_kernel, bs=bs, ls=ls),
        grid=(t // r,),
        in_specs=[io_spec, kv_spec, kv_spec],
        out_specs=io_spec,
        out_shape=jax.ShapeDtypeStruct(q.shape, F32),
        compiler_params=_params("parallel"),
        name="mem_sample",
    )(q, k, v)


def _retention_tables(pos):
    inv = 1.0 / (RET_THETA ** jnp.linspace(0.0, 1.0, RET_DK // 2, dtype=F32))
    ang = pos[:, None] * inv[None, :]
    cos = jnp.repeat(jnp.cos(ang), 2, axis=1)
    sin = jnp.stack([-jnp.sin(ang), jnp.sin(ang)], axis=-1).reshape(pos.shape[0], RET_DK)
    return jnp.tile(cos, (1, PAIR)), jnp.tile(sin, (1, PAIR))


def _rope_tables(pos):
    half = ROPE_HALF
    inv = 1.0 / (ROPE_THETA ** (jnp.arange(half, dtype=F32) * 2.0 / (2 * half)))
    ang = pos[:, None] * inv[None, :]
    n = pos.shape[0]
    rest = SWA_HD - 2 * half
    cos = jnp.concatenate([jnp.cos(ang), jnp.cos(ang), jnp.ones((n, rest), F32)], axis=1)
    sin = jnp.concatenate([-jnp.sin(ang), jnp.sin(ang), jnp.zeros((n, rest), F32)], axis=1)
    return jnp.tile(cos, (1, PAIR)), jnp.tile(sin, (1, PAIR))


def _block_diag(t):
    g, a, b = t.shape
    eye = jnp.eye(g, dtype=t.dtype)
    return (t[:, :, None, :] * eye[:, None, :, None]).reshape(g * a, g * b)


def _half_block_diag(t):
    per = t.shape[0] // S5_HALVES
    return _bf(jnp.stack([_block_diag(t[h * per:(h + 1) * per]) for h in range(S5_HALVES)]))


def _sink_rows(sinks):
    return jnp.broadcast_to(sinks.astype(F32)[:, None], (sinks.shape[0], LANES))


def _trunk(x3, pos0, states, mem_k, mem_v, w, *, sample):
    b, l, d = x3.shape
    t = b * l
    x = x3.reshape(t, d)
    pos = pos0 + jnp.arange(l, dtype=F32)
    tm = 512 if sample else 1024
    gla_s, ret_s, s5_re, s5_im, swa_k, swa_v = states
    out_states = {k: [] for k in ("gla", "ret", "s5_re", "s5_im", "swa_k", "swa_v")}
    ld_row = jnp.repeat(jnp.log(1.0 - 2.0 ** (-5.0 - jnp.arange(RET_HEADS, dtype=F32))), RET_DK)[None, :]

    for layer in range(2):
        i = layer // 2
        x = _ffn(x, w['ffn1_norm'][layer][None], w['ffn1_w_gate'], w['ffn1_w_up'], w['ffn1_w_down'],
                 layer=layer)
        if layer % 2 == 0:
            cos, sin = _retention_tables(pos)
            args = (ld_row, w['gla_w_gate'][i], w['gla_b_gate'][i], w['gla_out_norm'][i])
            if sample:
                bg = 16
                proj = _norm_matmul(x, w['mix_norm'][layer][None], w['even_w_in'][i], tm=tm, tn=640)
                mixed, g_s, r_s = _even_sample(proj, jnp.tile(cos, (bg, 1)), jnp.tile(sin, (bg, 1)), *args,
                                               gla_s[i], ret_s[i], bg=bg, ls=l)
                x = _matmul_residual(x, [(mixed, w['even_w_out'][i])], tm=tm)
            else:
                x, g_s, r_s = _even_prompt(x.reshape(b, l, d), w['mix_norm'][layer][None], w['even_w_in'][i],
                                           w['even_w_out'][i], cos, sin, *args, gla_s[i], ret_s[i], lt=512)
                x = x.reshape(t, d)
            out_states["gla"].append(g_s)
            out_states["ret"].append(r_s)
        else:
            cq, sq = _rope_tables(pos)
            qg = jnp.tile(w['swa_q_norm'][i], SWA_HEADS)[None, :]
            kg = jnp.tile(w['swa_k_norm'][i], SWA_KV_HEADS)[None, :]
            s5_args = (w['s5_ab_re'][i], w['s5_ab_im'][i], w['s5_bb_re'][i], w['s5_bb_im'][i],
                       w['s5_cc_re'][i], w['s5_cc_im'][i], w['s5_d'][i][None], w['s5_w_glu'][i], w['s5_b_glu'][i][None])
            n_state = S5_GROUPS * S5_STATE
            s0_re, s0_im = s5_re[i].reshape(b, n_state), s5_im[i].reshape(b, n_state)
            sinks = _sink_rows(w['swa_sinks'][i])
            if sample:
                proj = _norm_matmul(x, w['mix_norm'][layer][None], w['odd_w_in'][i], tm=tm, tn=640)
                c_out, sr, si = _s5(proj.reshape(1, t, OD_COLS), s0_re, s0_im, *s5_args, n_seq=b, lt=l,
                                    groups_per_pass=1, unroll=True)
                bg = 8
                d_out, kb, vb = _swa_sample(proj, swa_k[i].reshape(b, SWA_WINDOW, LANES),
                                            swa_v[i].reshape(b, SWA_WINDOW, LANES),
                                            jnp.tile(cq, (bg, 1)), jnp.tile(sq, (bg, 1)), qg, kg, sinks, bg=bg, ls=l)
                w_out = w['odd_w_out'][i]
                x = _matmul_residual(x, [(c_out.reshape(t, S5_WIDTH), w_out[:S5_WIDTH]), (d_out, w_out[S5_WIDTH:])],
                                     tm=tm)
            else:
                x, sr, si, kb, vb = _odd_prompt(x.reshape(b, l, d), w['mix_norm'][layer][None], w['odd_w_in'][i],
                                                w['odd_w_out'][i], s0_re, s0_im, *s5_args, cq, sq, qg, kg, sinks,
                                                groups_per_pass=8, unroll=4)
                x = x.reshape(t, d)
            out_states["s5_re"].append(sr.reshape(b, S5_GROUPS, S5_STATE))
            out_states["s5_im"].append(si.reshape(b, S5_GROUPS, S5_STATE))
            out_states["swa_k"].append(kb.reshape(b, -1, SWA_KV_HEADS, SWA_HD))
            out_states["swa_v"].append(vb.reshape(b, -1, SWA_KV_HEADS, SWA_HD))
        if sample:
            q = _norm_matmul(x, w['mem_x_norm'][layer][None], w['mem_w_q'][layer], tm=tm, tn=MEM_HD,
                             head_gain=w['mem_q_norm'][layer][None], n_norm_tiles=MEM_HEADS)
            o = _mem_sample(q, mem_k, mem_v, layer=layer, bs=4, ls=l)
            x = _matmul_residual(x, [(o, w['mem_w_o'][layer])], tm=tm)
        else:
            x = _mem_prompt(x.reshape(b, l, d), w['mem_x_norm'][layer][None], w['mem_w_q'][layer],
                            w['mem_q_norm'][layer][None], mem_k, mem_v, w['mem_w_o'][layer],
                            layer=layer, lt=1024).reshape(t, d)
        x = _ffn(x, w['ffn2_norm'][layer][None], w['ffn2_w_gate'], w['ffn2_w_up'], w['ffn2_w_down'],
                 layer=layer)
    return x.reshape(b, l, d), {k: jnp.stack(v) for k, v in out_states.items()}


def kernel(x_prompt, x_sample, mem_prompt, state_gla, state_ret, state_s5_re, state_s5_im, cache_swa_k, cache_swa_v, cache_mem_k, cache_mem_v, ffn1_norm, ffn1_w_gate, ffn1_w_up, ffn1_w_down, ffn2_norm, ffn2_w_gate, ffn2_w_up, ffn2_w_down, mix_norm, even_w_in, gla_w_gate, gla_b_gate, gla_out_norm, even_w_out, odd_w_in, s5_a_re, s5_a_im, s5_log_step, s5_b_re, s5_b_im, s5_c_re, s5_c_im, s5_d, s5_w_glu, s5_b_glu, swa_q_norm, swa_k_norm, swa_sinks, odd_w_out, mem_x_norm, mem_m_norm, mem_w_q, mem_w_k, mem_w_v, mem_w_o, mem_q_norm, mem_k_norm):
    depth = ffn1_norm.shape[0]
    n_even, n_odd = even_w_in.shape[0], odd_w_in.shape[0]
    batch, seq, d = x_prompt.shape
    dec_batch = x_sample.shape[0]
    n_mem = mem_prompt.shape[1]

    ev = even_w_in
    ev_cols = jnp.concatenate(
        [ev[..., 0:1536], ev[..., 1552:3088], ev[..., 1536:1552],
         jnp.zeros(ev.shape[:2] + (EV_COLS - 3088,), ev.dtype)], axis=-1)
    wgate_pad = jnp.concatenate(
        [gla_w_gate, jnp.zeros((n_even, EV_COLS - EV_GA - GLA_RANK, gla_w_gate.shape[-1]), gla_w_gate.dtype)], axis=1)
    w = dict(
        ffn1_norm=ffn1_norm, ffn2_norm=ffn2_norm, mix_norm=mix_norm, mem_x_norm=mem_x_norm,
        ffn1_w_gate=_bf(ffn1_w_gate), ffn1_w_up=_bf(ffn1_w_up), ffn1_w_down=_bf(ffn1_w_down),
        ffn2_w_gate=_bf(ffn2_w_gate), ffn2_w_up=_bf(ffn2_w_up), ffn2_w_down=_bf(ffn2_w_down),
        even_w_in=_bf(ev_cols), gla_w_gate=_bf(wgate_pad), gla_b_gate=gla_b_gate[:, None, :],
        gla_out_norm=gla_out_norm[:, None, :], even_w_out=_bf(even_w_out),
        odd_w_in=_bf(odd_w_in), odd_w_out=_bf(odd_w_out), s5_d=s5_d, s5_w_glu=_bf(s5_w_glu), s5_b_glu=s5_b_glu,
        swa_q_norm=swa_q_norm, swa_k_norm=swa_k_norm, swa_sinks=swa_sinks,
        mem_w_q=_bf(mem_w_q), mem_w_o=_bf(mem_w_o), mem_q_norm=mem_q_norm,
    )
    ab_re, ab_im, bb_re, bb_im = [], [], [], []
    for i in range(n_odd):
        a_r, a_i, b_r, b_i = _s5_prep(s5_a_re[i], s5_a_im[i], s5_log_step[i], s5_b_re[i], s5_b_im[i])
        ab_re.append(a_r.reshape(-1))
        ab_im.append(a_i.reshape(-1))
        bb_re.append(_half_block_diag(b_r))
        bb_im.append(_half_block_diag(b_i))
    w.update(s5_ab_re=ab_re, s5_ab_im=ab_im, s5_bb_re=bb_re, s5_bb_im=bb_im,
             s5_cc_re=[_half_block_diag(jnp.swapaxes(s5_c_re[i], 1, 2)) for i in range(n_odd)],
             s5_cc_im=[_half_block_diag(jnp.swapaxes(s5_c_im[i], 1, 2)) for i in range(n_odd)])

    p_mem_k, p_mem_v = _memkv(mem_prompt, mem_m_norm[:, None, :], _bf(mem_w_k), _bf(mem_w_v), mem_k_norm[:, None, :])

    zeros = lambda *s: jnp.zeros(s, F32)
    p_states = (zeros(n_even, batch, GLA_HEADS, GLA_DK, GLA_DV), zeros(n_even, batch, RET_HEADS, RET_DK, GLA_DV),
                zeros(n_odd, batch, S5_GROUPS, S5_STATE), zeros(n_odd, batch, S5_GROUPS, S5_STATE), None, None)
    y_prompt, ps = _trunk(x_prompt, 0.0, p_states, p_mem_k, p_mem_v, w, sample=False)

    s_states = (state_gla, state_ret, state_s5_re, state_s5_im, cache_swa_k, cache_swa_v)
    y_sample, ss = _trunk(x_sample, float(PAST_LEN), s_states, cache_mem_k, cache_mem_v, w, sample=True)

    return (y_prompt, y_sample, ps["gla"], ps["ret"], ps["s5_re"], ps["s5_im"], ps["swa_k"], ps["swa_v"],
            p_mem_k, p_mem_v, ss["gla"], ss["ret"], ss["s5_re"], ss["s5_im"], ss["swa_k"], ss["swa_v"])
```

```python
import functools
import math

import jax
import jax.numpy as jnp
from jax import lax
from jax.experimental import pallas as pl
from jax.experimental.pallas import tpu as pltpu

F32 = jnp.float32
BF16 = jnp.bfloat16
NORM_EPS = 1e-6

D_MODEL = 1024
GLA_HEADS = 4
GLA_DK = 64
GLA_DV = 128
GLA_RANK = 16
GLA_TAU = 16.0
RET_HEADS = 4
RET_DK = 64
RET_THETA = 10000.0
LA_CHUNK = 64
S5_WIDTH = 512
S5_GROUP = 16
S5_GROUPS = 32
S5_STATE = 64
SWA_HD = 64
SWA_HEADS = 8
SWA_KV_HEADS = 2
SWA_WINDOW = 128
ROPE_THETA = 500000.0
MEM_HEADS = 4
MEM_HD = 256
PAST_LEN = 8192

VMEM_LIMIT_BYTES = 52 * 1024 * 1024
LANES = 128
SUBLANES = 8


def _params(*sem):
    return pltpu.CompilerParams(dimension_semantics=sem, vmem_limit_bytes=VMEM_LIMIT_BYTES)


def _rms(x, gain=None):
    y = x * lax.rsqrt(jnp.mean(x * x, axis=-1, keepdims=True) + NORM_EPS)
    return y if gain is None else y * gain


def _dot(a, b):
    return jnp.dot(a, b, preferred_element_type=F32)


def _dot_nt(a, b):
    return lax.dot_general(a, b, (((1,), (1,)), ((), ())), preferred_element_type=F32)


def _dot_tn(a, b):
    return lax.dot_general(a, b, (((0,), (0,)), ((), ())), preferred_element_type=F32)


def _split_bf16(x, terms):
    pieces = []
    for _ in range(terms):
        piece = _bf(x)
        pieces.append(piece)
        x = x - piece.astype(F32)
    return pieces


def _dot_exact_lhs(a, x, terms=3):
    return sum(_dot(a, piece) for piece in _split_bf16(x, terms))


def _dot_exact_rhs(x, b, terms=3):
    return sum(_dot(piece, b) for piece in _split_bf16(x, terms))


def _dot_nt_exact_lhs(a, x, terms=3):
    return sum(_dot_nt(a, piece) for piece in _split_bf16(x, terms))


def _bf(x):
    return x.astype(BF16)


def _log_sigmoid(x):
    return jnp.minimum(x, 0.0) - jnp.log(1.0 + jnp.exp(-jnp.abs(x)))


def _iota(shape, dim):
    return lax.broadcasted_iota(jnp.int32, shape, dim)


def _lane_tile(x, n):
    return jnp.concatenate([x] * n, axis=1)


def _swap_pairs(x):
    n = x.shape[-1]
    even = (_iota(x.shape, 1) & 1) == 0
    return jnp.where(even, pltpu.roll(x, n - 1, 1), pltpu.roll(x, 1, 1))


def _rope_partner(x, head_dim, half):
    n = x.shape[-1]
    first = (_iota(x.shape, 1) & (head_dim - 1)) < half
    return jnp.where(first, pltpu.roll(x, n - half, 1), pltpu.roll(x, half, 1))


FFN_ROW_TILE = 512


def _ffn_kernel(x_ref, g_ref, wg_ref, wu_ref, wd_ref, o_ref):
    x = x_ref[...]
    xn = _bf(_rms(x, g_ref[...]))
    gate = _dot(xn, wg_ref[...])
    up = _dot(xn, wu_ref[...])
    o_ref[...] = x + _dot(_bf(jax.nn.silu(gate) * up * 0.5), wd_ref[...])


def _ffn(x, gain, wg, wu, wd, *, layer):
    t, d = x.shape
    h = wg.shape[2]
    tm = FFN_ROW_TILE
    resident = dict(pipeline_mode=pl.Buffered(1))
    return pl.pallas_call(
        _ffn_kernel,
        grid=(t // tm,),
        in_specs=[
            pl.BlockSpec((tm, d), lambda i: (i, 0)),
            pl.BlockSpec((1, d), lambda i: (0, 0)),
            pl.BlockSpec((None, d, h), lambda i: (layer, 0, 0), **resident),
            pl.BlockSpec((None, d, h), lambda i: (layer, 0, 0), **resident),
            pl.BlockSpec((None, h, d), lambda i: (layer, 0, 0), **resident),
        ],
        out_specs=pl.BlockSpec((tm, d), lambda i: (i, 0)),
        out_shape=jax.ShapeDtypeStruct((t, d), F32),
        compiler_params=_params("parallel"),
        name="ffn",
    )(x, gain, wg, wu, wd)


def _nmm_kernel(x_ref, g_ref, w_ref, hg_ref, o_ref, xn_ref, *, n_norm_tiles):
    j = pl.program_id(1)

    @pl.when(j == 0)
    def _():
        xn_ref[...] = _bf(_rms(x_ref[...], g_ref[...]))

    y = _dot(xn_ref[...], w_ref[...])
    if n_norm_tiles == 0:
        o_ref[...] = y
    else:
        @pl.when(j < n_norm_tiles)
        def _():
            o_ref[...] = _rms(y, hg_ref[...])

        @pl.when(j >= n_norm_tiles)
        def _():
            o_ref[...] = y


def _norm_matmul(x, gain, w, *, tm, tn, head_gain=None, n_norm_tiles=0):
    t, d = x.shape
    n = w.shape[1]
    if head_gain is None:
        head_gain = jnp.ones((1, tn), F32)
    return pl.pallas_call(
        functools.partial(_nmm_kernel, n_norm_tiles=n_norm_tiles),
        grid=(t // tm, n // tn),
        in_specs=[
            pl.BlockSpec((tm, d), lambda i, j: (i, 0)),
            pl.BlockSpec((1, d), lambda i, j: (0, 0)),
            pl.BlockSpec((d, tn), lambda i, j: (0, j)),
            pl.BlockSpec((1, tn), lambda i, j: (0, 0)),
        ],
        out_specs=pl.BlockSpec((tm, tn), lambda i, j: (i, j)),
        out_shape=jax.ShapeDtypeStruct((t, n), F32),
        scratch_shapes=[pltpu.VMEM((tm, d), BF16)],
        compiler_params=_params("parallel", "arbitrary"),
        name="norm_matmul",
    )(x, gain, w, head_gain)


def _mmr_kernel(*refs, n_terms):
    x_ref = refs[0]
    a_refs = refs[1:1 + n_terms]
    w_refs = refs[1 + n_terms:1 + 2 * n_terms]
    o_ref = refs[1 + 2 * n_terms]
    acc = x_ref[...]
    for a_ref, w_ref in zip(a_refs, w_refs):
        acc = acc + _dot(_bf(a_ref[...]), w_ref[...])
    o_ref[...] = acc


def _matmul_residual(x, terms, *, tm):
    t, d = x.shape
    acts = [a for a, _ in terms]
    ws = [w for _, w in terms]
    in_specs = [pl.BlockSpec((tm, d), lambda i: (i, 0))]
    in_specs += [pl.BlockSpec((tm, a.shape[1]), lambda i: (i, 0)) for a in acts]
    in_specs += [pl.BlockSpec(w.shape, lambda i: (0, 0)) for w in ws]
    return pl.pallas_call(
        functools.partial(_mmr_kernel, n_terms=len(terms)),
        grid=(t // tm,),
        in_specs=in_specs,
        out_specs=pl.BlockSpec((tm, d), lambda i: (i, 0)),
        out_shape=jax.ShapeDtypeStruct((t, d), F32),
        compiler_params=_params("parallel"),
        name="matmul_residual",
    )(x, *acts, *ws)


EV_GQ, EV_GK, EV_GV, EV_GG = 0, 256, 512, 1024
EV_RQ, EV_RK, EV_RV, EV_RG = 1536, 1792, 2048, 2560
EV_GA = 3072
EV_COLS = 3200
EV_BLOCK = 256
PAIR = 2


def _gate_and_norm(o, gate, gain=None):
    return _rms(o, gain) * jax.nn.silu(gate)


def _even_prompt_kernel(x_ref, gain_ref, win_ref, wout_ref, cos_ref, sin_ref, ld_ref, wgate_ref, bgate_ref, gnorm_ref,
                        s0g_ref, s0r_ref, o_ref, sg_ref, sr_ref, proj_sc, mix_sc, o_sc, *, chunk, n_chunks):
    @pl.when(pl.program_id(1) == 0)
    def _():
        sg_ref[...] = s0g_ref[...]
        sr_ref[...] = s0r_ref[...]

    c = chunk
    lt = c * n_chunks
    blk = min(EV_BLOCK, lt)
    c_shift = c.bit_length() - 1
    x = x_ref[0]
    proj_sc[...] = _dot(_bf(_rms(x, gain_ref[...])), win_ref[...])

    def cols(a, b):
        return proj_sc[:, a:b]

    log_a = _log_sigmoid(_dot(_bf(cols(EV_GA, EV_COLS)), wgate_ref[...]) + bgate_ref[...]) * (1.0 / GLA_TAU)
    tril = _bf((_iota((c, c), 1) <= _iota((c, c), 0)).astype(F32))
    cum_parts = [_dot_exact_lhs(tril, log_a[i * c:(i + 1) * c]) for i in range(n_chunks)]
    tots = [p[c - 1:c] for p in cum_parts]
    cum = jnp.concatenate(cum_parts, axis=0)
    tot_b = jnp.concatenate([jnp.broadcast_to(t, (c, 256)) for t in tots], axis=0)
    k = cols(EV_GK, EV_GK + 256)
    gla = (cols(EV_GQ, EV_GQ + 256) * (GLA_DK ** -0.5) * jnp.exp(cum), k * jnp.exp(-cum), k * jnp.exp(tot_b - cum))

    ld = ld_ref[...]
    tpos = ((_iota((lt, 1), 0) & (c - 1)) + 1).astype(F32)
    cum_r = tpos * ld
    tot_r = float(c) * ld
    cos, sin = _lane_tile(cos_ref[...], PAIR), _lane_tile(sin_ref[...], PAIR)
    rq = cols(EV_RQ, EV_RQ + 256)
    rk = cols(EV_RK, EV_RK + 256)
    q_rot = rq * cos + _swap_pairs(rq) * sin
    k_rot = (rk * cos + _swap_pairs(rk) * sin) * (RET_DK ** -0.5)
    ret = (q_rot * jnp.exp(cum_r), k_rot * jnp.exp(-cum_r), k_rot * jnp.exp(tot_r - cum_r))

    tot_rows = jnp.concatenate(tots + [tot_r, jnp.zeros((LANES - n_chunks - 1, 256), F32)], axis=0)
    decay_cols = jnp.exp(jnp.transpose(tot_rows))

    row = _iota((blk, blk), 0)
    col = _iota((blk, blk), 1)
    blk_mask = ((row >> c_shift) == (col >> c_shift)) & (col <= row)
    lo = (_iota((lt, LANES), 1) < GLA_DK)
    gnorm = gnorm_ref[...]

    mixers = ((gla, EV_GV, EV_GG, sg_ref, 0, gnorm, lambda i: i),
              (ret, EV_RV, EV_RG, sr_ref, 512, None, lambda i: n_chunks))
    for m, ((q_dec, k_inv, k_dec), v_col, g_col, s_ref, out_col, gain, decay_col_of) in enumerate(mixers):
        for p in range(GLA_HEADS // PAIR):
            lanes = slice(p * LANES, (p + 1) * LANES)
            q_pair = q_dec[:, lanes]
            q_masked = [_bf(jnp.where(lo, q_pair, 0.0)), _bf(jnp.where(lo, 0.0, q_pair))]
            ki = _bf(k_inv[:, lanes])
            kd = _bf(k_dec[:, lanes])
            v_pair = _bf(cols(v_col + p * PAIR * GLA_DV, v_col + (p + 1) * PAIR * GLA_DV))
            for e in range(PAIR):
                slot = (m * (GLA_HEADS // PAIR) + p) * PAIR + e
                for r0 in range(0, lt, blk):
                    rs = slice(r0, r0 + blk)
                    scores = jnp.where(blk_mask, _dot_nt(q_masked[e][rs], ki[rs]), 0.0)
                    o_sc[slot, rs, :] = _dot(_bf(scores), v_pair[rs, e * GLA_DV:(e + 1) * GLA_DV])
            state = s_ref[0, p * PAIR:(p + 1) * PAIR].reshape(PAIR * GLA_DK, GLA_DV)
            for i in range(n_chunks):
                rs = slice(i * c, (i + 1) * c)
                q_stack = jnp.concatenate([q_masked[0][rs], q_masked[1][rs]], axis=0)
                o_inter = _dot(q_stack, _bf(state))
                kv = _dot_tn(kd[rs], v_pair[rs])
                kv = jnp.concatenate([kv[:GLA_DK, :GLA_DV], kv[GLA_DK:, GLA_DV:]], axis=0)
                ci = decay_col_of(i)
                state = state * decay_cols[p * LANES:(p + 1) * LANES, ci:ci + 1] + kv
                for e in range(PAIR):
                    slot = (m * (GLA_HEADS // PAIR) + p) * PAIR + e
                    o_sc[slot, rs, :] += o_inter[e * c:(e + 1) * c]
            s_ref[0, p * PAIR:(p + 1) * PAIR] = state.reshape(PAIR, GLA_DK, GLA_DV)
            for e in range(PAIR):
                h = p * PAIR + e
                slot = (m * (GLA_HEADS // PAIR) + p) * PAIR + e
                gate = cols(g_col + h * GLA_DV, g_col + (h + 1) * GLA_DV)
                mix_sc[:, out_col + h * GLA_DV:out_col + (h + 1) * GLA_DV] = _gate_and_norm(o_sc[slot], gate, gain)
    o_ref[0] = x + _dot(_bf(mix_sc[...]), wout_ref[...])


def _even_prompt(x, gain, w_in, w_out, cos, sin, ld_row, wgate, bgate, gnorm, s0g, s0r, *, lt):
    b, l, d = x.shape
    chunk = math.gcd(l, LA_CHUNK)
    st_spec = pl.BlockSpec((1, GLA_HEADS, GLA_DK, GLA_DV), lambda i, j: (i, 0, 0, 0))
    io_spec = pl.BlockSpec((1, lt, d), lambda i, j: (i, j, 0))
    full = lambda a: pl.BlockSpec(a.shape, lambda i, j: (0,) * a.ndim)
    return pl.pallas_call(
        functools.partial(_even_prompt_kernel, chunk=chunk, n_chunks=lt // chunk),
        grid=(b, l // lt),
        in_specs=[
            io_spec, full(gain), full(w_in), full(w_out),
            pl.BlockSpec((lt, LANES), lambda i, j: (j, 0)),
            pl.BlockSpec((lt, LANES), lambda i, j: (j, 0)),
            full(ld_row), full(wgate), full(bgate), full(gnorm), st_spec, st_spec,
        ],
        out_specs=[io_spec, st_spec, st_spec],
        out_shape=[jax.ShapeDtypeStruct(x.shape, F32),
                   jax.ShapeDtypeStruct(s0g.shape, F32), jax.ShapeDtypeStruct(s0r.shape, F32)],
        scratch_shapes=[pltpu.VMEM((lt, EV_COLS), F32), pltpu.VMEM((lt, d), F32),
                        pltpu.VMEM((GLA_HEADS + RET_HEADS, lt, GLA_DV), F32)],
        compiler_params=_params("parallel", "arbitrary"),
        name="even_prompt",
    )(x, gain, w_in, w_out, cos, sin, ld_row, wgate, bgate, gnorm, s0g, s0r)


def _even_sample_kernel(proj_ref, cos_ref, sin_ref, ld_ref, wgate_ref, bgate_ref, gnorm_ref, s0g_ref, s0r_ref,
                        mix_ref, sg_ref, sr_ref, *, bg, ls):
    r = bg * ls
    ls_shift = ls.bit_length() - 1
    dk_shift = GLA_DK.bit_length() - 1
    n_exp = bg * GLA_DK
    row_seq = _iota((r, r), 0) >> ls_shift
    col_seq = _iota((r, r), 1) >> ls_shift
    same = row_seq == col_seq
    seg = same & (_iota((r, r), 1) <= _iota((r, r), 0))
    seg_b = _bf(seg.astype(F32))
    same_b = _bf(same.astype(F32))
    tile_b = _bf(((_iota((GLA_DK, n_exp), 1) & (GLA_DK - 1)) == _iota((GLA_DK, n_exp), 0)).astype(F32))
    tile_t_b = _bf(((_iota((n_exp, GLA_DK), 0) & (GLA_DK - 1)) == _iota((n_exp, GLA_DK), 1)).astype(F32))
    q_mask = (_iota((r, n_exp), 0) >> ls_shift) == (_iota((r, n_exp), 1) >> dk_shift)
    k_mask = (_iota((n_exp, r), 0) >> dk_shift) == (_iota((n_exp, r), 1) >> ls_shift)
    tpos = ((_iota((r, 1), 0) & (ls - 1)) + 1).astype(F32)
    ld = ld_ref[...]
    cum_r = tpos * ld
    tot_r = float(ls) * ld
    r_dec = jnp.exp(tot_r)
    gnorm = gnorm_ref[...]

    def mixer(q_dec, k_inv, k_dec, v_col, g_col, s0_ref, s_ref, out_col, decay_of, gain):
        for h in range(GLA_HEADS):
            sl = slice(h * GLA_DK, (h + 1) * GLA_DK)
            v = _bf(proj_ref[:, v_col + h * GLA_DV:v_col + (h + 1) * GLA_DV])
            qd = _bf(q_dec[:, sl])
            scores = jnp.where(seg, _dot_nt(qd, _bf(k_inv[:, sl])), 0.0)
            state = s0_ref[:, h].reshape(n_exp, GLA_DV)
            q_exp = _bf(jnp.where(q_mask, _dot(qd, tile_b), 0.0))
            o = _dot(_bf(scores), v) + _dot(q_exp, _bf(state))
            k_exp = _bf(jnp.where(k_mask, _dot_nt(tile_t_b, _bf(k_dec[:, sl])), 0.0))
            new_state = state * decay_of(h, sl) + _dot(k_exp, v)
            s_ref[:, h] = new_state.reshape(bg, GLA_DK, GLA_DV)
            gate = proj_ref[:, g_col + h * GLA_DV:g_col + (h + 1) * GLA_DV]
            mix_ref[:, out_col + h * GLA_DV:out_col + (h + 1) * GLA_DV] = _gate_and_norm(o, gate, gain)

    log_a = _log_sigmoid(_dot(_bf(proj_ref[:, EV_GA:EV_COLS]), wgate_ref[...]) + bgate_ref[...]) * (1.0 / GLA_TAU)
    cum = _dot_exact_lhs(seg_b, log_a)
    tot = _dot_exact_lhs(same_b, log_a)
    k = proj_ref[:, EV_GK:EV_GK + 256]

    def gla_decay(h, sl):
        la_exp = jnp.where(k_mask, _dot_nt_exact_lhs(tile_t_b, log_a[:, sl]), 0.0)
        return jnp.exp(jnp.sum(la_exp, axis=-1, keepdims=True))

    mixer(proj_ref[:, EV_GQ:EV_GQ + 256] * (GLA_DK ** -0.5) * jnp.exp(cum), k * jnp.exp(-cum), k * jnp.exp(tot - cum),
          EV_GV, EV_GG, s0g_ref, sg_ref, 0, gla_decay, gnorm)

    cos = _lane_tile(cos_ref[...], PAIR)
    sin = _lane_tile(sin_ref[...], PAIR)
    rq = proj_ref[:, EV_RQ:EV_RQ + 256]
    rk = proj_ref[:, EV_RK:EV_RK + 256]
    q_rot = rq * cos + _swap_pairs(rq) * sin
    k_rot = (rk * cos + _swap_pairs(rk) * sin) * (RET_DK ** -0.5)

    def ret_decay(h, sl):
        return r_dec[:, h * RET_DK:h * RET_DK + 1]

    mixer(q_rot * jnp.exp(cum_r), k_rot * jnp.exp(-cum_r), k_rot * jnp.exp(tot_r - cum_r),
          EV_RV, EV_RG, s0r_ref, sr_ref, 512, ret_decay, None)


def _even_sample(proj, cos, sin, ld_row, wgate, bgate, gnorm, s0g, s0r, *, bg, ls):
    t = proj.shape[0]
    n_b = t // ls
    r = bg * ls
    st_spec = pl.BlockSpec((bg, GLA_HEADS, GLA_DK, GLA_DV), lambda i: (i, 0, 0, 0))
    full = lambda a: pl.BlockSpec(a.shape, lambda i: (0,) * a.ndim)
    return pl.pallas_call(
        functools.partial(_even_sample_kernel, bg=bg, ls=ls),
        grid=(n_b // bg,),
        in_specs=[
            pl.BlockSpec((r, EV_COLS), lambda i: (i, 0)),
            full(cos), full(sin), full(ld_row), full(wgate), full(bgate), full(gnorm), st_spec, st_spec,
        ],
        out_specs=[pl.BlockSpec((r, D_MODEL), lambda i: (i, 0)), st_spec, st_spec],
        out_shape=[jax.ShapeDtypeStruct((t, D_MODEL), F32),
                   jax.ShapeDtypeStruct(s0g.shape, F32), jax.ShapeDtypeStruct(s0r.shape, F32)],
        compiler_params=_params("parallel"),
        name="even_sample",
    )(proj, cos, sin, ld_row, wgate, bgate, gnorm, s0g, s0r)


def _s5_prep_kernel(are_ref, aim_ref, lstep_ref, bre_ref, bim_ref, abre_ref, abim_ref, bbre_ref, bbim_ref):
    a_re, a_im = are_ref[...], aim_ref[...]
    step = jnp.exp(lstep_ref[...])
    mag = jnp.exp(a_re * step)
    ab_re = mag * jnp.cos(a_im * step)
    ab_im = mag * jnp.sin(a_im * step)
    den = a_re * a_re + a_im * a_im
    coef_re = ((ab_re - 1.0) * a_re + ab_im * a_im) / den
    coef_im = (ab_im * a_re - (ab_re - 1.0) * a_im) / den
    b_re, b_im = bre_ref[...], bim_ref[...]
    abre_ref[...] = ab_re
    abim_ref[...] = ab_im
    bbre_ref[...] = coef_re * b_re - coef_im * b_im
    bbim_ref[...] = coef_re * b_im + coef_im * b_re


def _s5_prep(a_re, a_im, log_step, b_re, b_im):
    g, n = a_re.shape
    shp3 = jax.ShapeDtypeStruct((g, 1, n), F32)
    shpb = jax.ShapeDtypeStruct((g, S5_GROUP, n), F32)
    return pl.pallas_call(_s5_prep_kernel, out_shape=[shp3, shp3, shpb, shpb], name="s5_prep")(
        a_re.reshape(g, 1, n), a_im.reshape(g, 1, n), log_step.reshape(g, 1, 1),
        jnp.swapaxes(b_re, 1, 2), jnp.swapaxes(b_im, 1, 2))


S5_LANE_CHUNKS = S5_GROUPS * S5_STATE // LANES
S5_HALVES = 2


def _s5_layout(n_seq, lt):
    pack = max(1, SUBLANES // n_seq)
    pitch = lt + 4 if lt % SUBLANES == 0 else lt
    return pack, S5_LANE_CHUNKS // pack, pitch


def _s5_slot(c, n_groups):
    return c % n_groups, c // n_groups


def _s5_load_state(s0re_ref, s0im_ref, hre_ref, him_ref, n_seq, n_groups):
    for c in range(S5_LANE_CHUNKS):
        g, j = _s5_slot(c, n_groups)
        hre_ref[g, j * n_seq:(j + 1) * n_seq, :] = s0re_ref[:, c * LANES:(c + 1) * LANES]
        him_ref[g, j * n_seq:(j + 1) * n_seq, :] = s0im_ref[:, c * LANES:(c + 1) * LANES]


def _s5_store_state(sre_ref, sim_ref, hre_ref, him_ref, n_seq, n_groups):
    for c in range(S5_LANE_CHUNKS):
        g, j = _s5_slot(c, n_groups)
        sre_ref[:, c * LANES:(c + 1) * LANES] = hre_ref[g, j * n_seq:(j + 1) * n_seq, :]
        sim_ref[:, c * LANES:(c + 1) * LANES] = him_ref[g, j * n_seq:(j + 1) * n_seq, :]


def _s5_core(u, abre_ref, abim_ref, bbre_ref, bbim_ref, ccre_ref, ccim_ref, d_ref, wglu_ref, bglu_ref,
             xr_ref, xi_ref, hre_ref, him_ref, *, n_seq, lt, groups_per_pass, unroll):
    pack, n_groups, pitch = _s5_layout(n_seq, lt)
    rows = n_seq * lt
    per_half = S5_LANE_CHUNKS // S5_HALVES
    lanes_of = lambda c: slice(c * LANES, (c + 1) * LANES)
    slot_of = lambda c: _s5_slot(c, n_groups)

    def seq_rows(j, s):
        return slice((j * n_seq + s) * pitch, (j * n_seq + s) * pitch + lt)

    ub = _bf(u)
    u_cols = S5_WIDTH // S5_HALVES
    for half in range(S5_HALVES):
        uh = ub[:, half * u_cols:(half + 1) * u_cols]
        for x_ref, bb_ref in ((xr_ref, bbre_ref), (xi_ref, bbim_ref)):
            x = _dot(uh, bb_ref[half])
            for k in range(per_half):
                g, j = slot_of(half * per_half + k)
                if pitch == lt:
                    x_ref[g, j * rows:(j + 1) * rows, :] = x[:, lanes_of(k)]
                else:
                    for s in range(n_seq):
                        x_ref[g, seq_rows(j, s), :] = x[s * lt:(s + 1) * lt, lanes_of(k)]

    for g0 in range(0, n_groups, groups_per_pass):
        gs = list(range(g0, g0 + groups_per_pass))
        init = tuple(hre_ref[g] for g in gs) + tuple(him_ref[g] for g in gs)

        def step(t, carry, gs=gs):
            rws = pl.ds(t, pack * n_seq, stride=pitch)
            new_re, new_im = [], []
            for k, g in enumerate(gs):
                a_re, a_im = abre_ref[g], abim_ref[g]
                h_re, h_im = carry[k], carry[len(gs) + k]
                n_re = a_re * h_re - a_im * h_im + xr_ref[g, rws, :]
                n_im = a_re * h_im + a_im * h_re + xi_ref[g, rws, :]
                xr_ref[g, rws, :] = n_re
                xi_ref[g, rws, :] = n_im
                new_re.append(n_re)
                new_im.append(n_im)
            return tuple(new_re + new_im)

        fin = lax.fori_loop(0, lt, step, init, unroll=unroll)
        for k, g in enumerate(gs):
            hre_ref[g] = fin[k]
            him_ref[g] = fin[len(gs) + k]

    def gather(x_ref, half):
        cols = []
        for k in range(per_half):
            g, j = slot_of(half * per_half + k)
            if pitch == lt:
                cols.append(x_ref[g, j * rows:(j + 1) * rows, :])
            else:
                cols.append(jnp.concatenate([x_ref[g, seq_rows(j, s), :] for s in range(n_seq)], axis=0))
        return _bf(jnp.concatenate(cols, axis=1))

    y = jnp.concatenate([_dot(gather(xr_ref, half), ccre_ref[half]) - _dot(gather(xi_ref, half), ccim_ref[half])
                         for half in range(S5_HALVES)], axis=1) + d_ref[...] * u
    z = jax.nn.gelu(y, approximate=True)
    return z * jax.nn.sigmoid(_dot(_bf(z), wglu_ref[...]) + bglu_ref[...])


def _s5_kernel(u_ref, s0re_ref, s0im_ref, abre_ref, abim_ref, bbre_ref, bbim_ref, ccre_ref, ccim_ref,
               d_ref, wglu_ref, bglu_ref, out_ref, sre_ref, sim_ref, xr_ref, xi_ref, hre_ref, him_ref,
               *, n_seq, lt, groups_per_pass, unroll):
    n_groups = _s5_layout(n_seq, lt)[1]

    @pl.when(pl.program_id(0) == 0)
    def _():
        _s5_load_state(s0re_ref, s0im_ref, hre_ref, him_ref, n_seq, n_groups)

    out = _s5_core(u_ref[...].reshape(n_seq * lt, S5_WIDTH), abre_ref, abim_ref, bbre_ref, bbim_ref, ccre_ref, ccim_ref,
                   d_ref, wglu_ref, bglu_ref, xr_ref, xi_ref, hre_ref, him_ref,
                   n_seq=n_seq, lt=lt, groups_per_pass=groups_per_pass, unroll=unroll)
    out_ref[...] = out.reshape(out_ref.shape)

    @pl.when(pl.program_id(0) == pl.num_programs(0) - 1)
    def _():
        _s5_store_state(sre_ref, sim_ref, hre_ref, him_ref, n_seq, n_groups)


def _s5_tables(ab, n_seq, lt):
    pack, n_groups, _ = _s5_layout(n_seq, lt)
    tab = jnp.swapaxes(ab.reshape(pack, n_groups, 1, LANES), 0, 1)
    return jnp.broadcast_to(tab, (n_groups, pack, n_seq, LANES)).reshape(n_groups, pack * n_seq, LANES)


def _s5(proj3, s0_re, s0_im, ab_re, ab_im, bb_re, bb_im, cc_re, cc_im, d_row, wglu, bglu, *, n_seq, lt,
        groups_per_pass, unroll):
    nb, rows_b, _ = proj3.shape
    blk_rows = n_seq * lt // nb
    pack, n_groups, pitch = _s5_layout(n_seq, lt)
    ab_re, ab_im = _s5_tables(ab_re, n_seq, lt), _s5_tables(ab_im, n_seq, lt)
    full = lambda a: pl.BlockSpec(a.shape, lambda j: (0,) * a.ndim)
    io_spec = pl.BlockSpec((nb, blk_rows, S5_WIDTH), lambda j: (0, j, 0))
    x_scratch = pltpu.VMEM((n_groups, pack * n_seq * pitch, LANES), F32)
    h_scratch = pltpu.VMEM((n_groups, pack * n_seq, LANES), F32)
    return pl.pallas_call(
        functools.partial(_s5_kernel, n_seq=n_seq, lt=lt, groups_per_pass=groups_per_pass, unroll=unroll),
        grid=(rows_b // blk_rows,),
        in_specs=[io_spec, full(s0_re), full(s0_im), full(ab_re), full(ab_im), full(bb_re), full(bb_im),
                  full(cc_re), full(cc_im), full(d_row), full(wglu), full(bglu)],
        out_specs=[io_spec, full(s0_re), full(s0_im)],
        out_shape=[jax.ShapeDtypeStruct((nb, rows_b, S5_WIDTH), F32),
                   jax.ShapeDtypeStruct(s0_re.shape, F32), jax.ShapeDtypeStruct(s0_im.shape, F32)],
        scratch_shapes=[x_scratch, x_scratch, h_scratch, h_scratch],
        compiler_params=_params("arbitrary"),
        name="s5",
    )(proj3, s0_re, s0_im, ab_re, ab_im, bb_re, bb_im, cc_re, cc_im, d_row, wglu, bglu)


OD_Q_BLOCK = 1
OD_KV_BLOCK = 4
OD_COLS = 1280
ROPE_HALF = SWA_HD // 8
SWA_GROUP = SWA_HEADS // SWA_KV_HEADS


def _half_lanes(shape):
    return (_iota(shape, 1) & (LANES - 1)) < SWA_HD


def _pair_rms_scale(x):
    same_head = (_iota((LANES, LANES), 0) >= SWA_HD) == (_iota((LANES, LANES), 1) >= SWA_HD)
    sums = _dot_exact_rhs(x * x, _bf(same_head.astype(F32)), terms=2)
    return lax.rsqrt(sums * (1.0 / SWA_HD) + NORM_EPS)


def _swa_qk(xq, xk, qg, kg, cq, sq):
    xq_g = xq * qg
    n_pairs = SWA_HEADS // 2
    q_rot = xq_g * _lane_tile(cq, n_pairs) + _rope_partner(xq_g, SWA_HD, ROPE_HALF) * _lane_tile(sq, n_pairs)
    xk_g = xk * kg
    k_rot = xk_g * cq + _rope_partner(xk_g, SWA_HD, ROPE_HALF) * sq
    q_pairs = [q_rot[:, j * LANES:(j + 1) * LANES] * _pair_rms_scale(xq[:, j * LANES:(j + 1) * LANES])
               for j in range(SWA_HEADS // 2)]
    return q_pairs, k_rot * _pair_rms_scale(xk)


def _swa_query_stack(q_pairs, kh):
    lo = _half_lanes(q_pairs[0].shape)
    keep = lo if kh == 0 else jnp.logical_not(lo)
    parts = []
    for g in range(SWA_GROUP):
        hq = kh * SWA_GROUP + g
        pair = q_pairs[hq // 2]
        src = pair if hq % 2 == kh else pltpu.roll(pair, SWA_HD, 1)
        parts.append(jnp.where(keep, src, 0.0))
    return _bf(jnp.concatenate(parts, axis=0))


def _swa_merge_heads(o, kh, rows):
    lo = _half_lanes((rows, LANES))
    pairs = []
    for p in range(SWA_GROUP // 2):
        even, odd = o[2 * p * rows:(2 * p + 1) * rows], o[(2 * p + 1) * rows:(2 * p + 2) * rows]
        if kh == 0:
            pairs.append(jnp.where(lo, even, pltpu.roll(odd, SWA_HD, 1)))
        else:
            pairs.append(jnp.where(lo, pltpu.roll(even, SWA_HD, 1), odd))
    return pairs


def _swa_block(xq, xk, v_cur, k_prev, v_prev, mask, qg, kg, cq, sq, sink_ref):
    w = SWA_WINDOW
    ones_col = jnp.ones((2 * w, LANES), BF16)
    q_pairs, k_cur = _swa_qk(xq, xk, qg, kg, cq, sq)
    k_ext = _bf(jnp.concatenate([k_prev, k_cur], axis=0))
    v_ext = _bf(jnp.concatenate([v_prev, v_cur], axis=0))
    out_pairs = []
    for kh in range(SWA_KV_HEADS):
        s_all = _dot_nt(_swa_query_stack(q_pairs, kh), k_ext)
        weights, sink_terms = [], []
        for g in range(SWA_GROUP):
            s = jnp.where(mask, s_all[g * w:(g + 1) * w] * (SWA_HD ** -0.5), -jnp.inf)
            sink = sink_ref[kh * SWA_GROUP + g:kh * SWA_GROUP + g + 1, :]
            m = jnp.maximum(jnp.broadcast_to(jnp.max(s, axis=-1, keepdims=True), (w, LANES)), sink)
            weights.append(_bf(jnp.exp(s - jnp.concatenate([m, m], axis=1))))
            sink_terms.append(jnp.exp(sink - m))
        weights = jnp.concatenate(weights, axis=0)
        den = _dot(weights, ones_col) + jnp.concatenate(sink_terms, axis=0)
        out_pairs += _swa_merge_heads(_dot(weights, v_ext) / den, kh, w)
    return out_pairs, k_cur


def _odd_prompt_kernel(x_ref, gain_ref, win_ref, wout_ref, s0re_ref, s0im_ref, abre_ref, abim_ref, bbre_ref, bbim_ref,
                       ccre_ref, ccim_ref, d_ref, wglu_ref, bglu_ref, cq_ref, sq_ref, qg_ref, kg_ref, sink_ref,
                       o_ref, sre_ref, sim_ref, ck_ref, cv_ref,
                       proj_sc, mix_sc, xr_ref, xi_ref, hre_ref, him_ref, kprev_sc, vprev_sc,
                       *, n_seq, groups_per_pass, unroll):
    w = SWA_WINDOW
    rows = n_seq * w
    n_groups = _s5_layout(n_seq, w)[1]
    step = pl.program_id(0)

    @pl.when(step == 0)
    def _():
        _s5_load_state(s0re_ref, s0im_ref, hre_ref, him_ref, n_seq, n_groups)
        kprev_sc[...] = jnp.zeros_like(kprev_sc)
        vprev_sc[...] = jnp.zeros_like(vprev_sc)

    x = x_ref[...].reshape(rows, D_MODEL)
    proj_sc[...] = _dot(_bf(_rms(x, gain_ref[...])), win_ref[...])
    mix_sc[:, 0:S5_WIDTH] = _s5_core(proj_sc[:, 0:S5_WIDTH], abre_ref, abim_ref, bbre_ref, bbim_ref, ccre_ref, ccim_ref,
                                     d_ref, wglu_ref, bglu_ref, xr_ref, xi_ref, hre_ref, him_ref,
                                     n_seq=n_seq, lt=w, groups_per_pass=groups_per_pass, unroll=unroll)

    t_idx = _iota((w, 2 * w), 0)
    s_idx = _iota((w, 2 * w), 1)
    mask = (s_idx > t_idx) & (s_idx <= t_idx + w) & (s_idx >= jnp.where(step == 0, w, 0))
    q0, k0, v0 = OD_Q_BLOCK * 512, OD_KV_BLOCK * 256, OD_KV_BLOCK * 256 + LANES
    for s in range(n_seq):
        rs = slice(s * w, (s + 1) * w)
        v_cur = proj_sc[rs, v0:v0 + LANES]
        pairs, k_cur = _swa_block(proj_sc[rs, q0:q0 + 512], proj_sc[rs, k0:k0 + LANES], v_cur, kprev_sc[s], vprev_sc[s],
                                  mask, qg_ref[...], kg_ref[...], cq_ref[...], sq_ref[...], sink_ref)
        for i, pair in enumerate(pairs):
            mix_sc[rs, S5_WIDTH + i * LANES:S5_WIDTH + (i + 1) * LANES] = pair
        kprev_sc[s] = k_cur
        vprev_sc[s] = v_cur
    o_ref[...] = (x + _dot(_bf(mix_sc[...]), wout_ref[...])).reshape(o_ref.shape)

    @pl.when(step == pl.num_programs(0) - 1)
    def _():
        _s5_store_state(sre_ref, sim_ref, hre_ref, him_ref, n_seq, n_groups)
        ck_ref[...] = kprev_sc[...]
        cv_ref[...] = vprev_sc[...]


def _odd_prompt(x, gain, w_in, w_out, s0_re, s0_im, ab_re, ab_im, bb_re, bb_im, cc_re, cc_im, d_row, wglu, bglu,
                cq, sq, qg, kg, sink_rows, *, groups_per_pass, unroll):
    b, l, d = x.shape
    w = SWA_WINDOW
    pack, n_groups, pitch = _s5_layout(b, w)
    ab_re, ab_im = _s5_tables(ab_re, b, w), _s5_tables(ab_im, b, w)
    full = lambda a: pl.BlockSpec(a.shape, lambda j: (0,) * a.ndim)
    io_spec = pl.BlockSpec((b, w, d), lambda j: (0, j, 0))
    tab_spec = pl.BlockSpec((w, LANES), lambda j: (j, 0))
    cache_shape = jax.ShapeDtypeStruct((b, w, LANES), F32)
    x_scratch = pltpu.VMEM((n_groups, pack * b * pitch, LANES), F32)
    h_scratch = pltpu.VMEM((n_groups, pack * b, LANES), F32)
    kv_scratch = pltpu.VMEM((b, w, LANES), F32)
    return pl.pallas_call(
        functools.partial(_odd_prompt_kernel, n_seq=b, groups_per_pass=groups_per_pass, unroll=unroll),
        grid=(l // w,),
        in_specs=[io_spec, full(gain), full(w_in), full(w_out), full(s0_re), full(s0_im), full(ab_re), full(ab_im),
                  full(bb_re), full(bb_im), full(cc_re), full(cc_im), full(d_row), full(wglu), full(bglu),
                  tab_spec, tab_spec, full(qg), full(kg), full(sink_rows)],
        out_specs=[io_spec, full(s0_re), full(s0_im), pl.BlockSpec((b, w, LANES), lambda j: (0, 0, 0)),
                   pl.BlockSpec((b, w, LANES), lambda j: (0, 0, 0))],
        out_shape=[jax.ShapeDtypeStruct(x.shape, F32), jax.ShapeDtypeStruct(s0_re.shape, F32),
                   jax.ShapeDtypeStruct(s0_im.shape, F32), cache_shape, cache_shape],
        scratch_shapes=[pltpu.VMEM((b * w, OD_COLS), F32), pltpu.VMEM((b * w, d), F32),
                        x_scratch, x_scratch, h_scratch, h_scratch, kv_scratch, kv_scratch],
        compiler_params=_params("arbitrary"),
        name="odd_prompt",
    )(x, gain, w_in, w_out, s0_re, s0_im, ab_re, ab_im, bb_re, bb_im, cc_re, cc_im, d_row, wglu, bglu,
      cq, sq, qg, kg, sink_rows)


def _swa_sample_kernel(q_ref, kv_ref, ck_ref, cv_ref, cq_ref, sq_ref, qg_ref, kg_ref, sink_ref,
                       o_ref, nk_ref, nv_ref, *, bg, ls):
    w = SWA_WINDOW
    r = bg * ls
    ls_shift = ls.bit_length() - 1
    w_shift = w.bit_length() - 1
    rows_g = SWA_GROUP * r
    v_new = kv_ref[:, LANES:2 * LANES]
    q_pairs, k_new = _swa_qk(q_ref[...], kv_ref[:, 0:LANES], qg_ref[...], kg_ref[...], cq_ref[...], sq_ref[...])
    k_cache = _bf(ck_ref[...].reshape(bg * w, LANES))
    v_cache = _bf(cv_ref[...].reshape(bg * w, LANES))

    row = _iota((rows_g, bg * w), 0) & (r - 1)
    col = _iota((rows_g, bg * w), 1)
    mask_c = ((row >> ls_shift) == (col >> w_shift)) & ((col & (w - 1)) > (row & (ls - 1)))
    row_n = _iota((rows_g, r), 0) & (r - 1)
    col_n = _iota((rows_g, r), 1)
    mask_n = ((row_n >> ls_shift) == (col_n >> ls_shift)) & ((col_n & (ls - 1)) <= (row_n & (ls - 1)))
    for kh in range(SWA_KV_HEADS):
        q_stack = _swa_query_stack(q_pairs, kh)
        s_c = jnp.where(mask_c, _dot_nt(q_stack, k_cache) * (SWA_HD ** -0.5), -jnp.inf)
        s_n = jnp.where(mask_n, _dot_nt(q_stack, _bf(k_new)) * (SWA_HD ** -0.5), -jnp.inf)
        sink = jnp.concatenate([jnp.broadcast_to(sink_ref[kh * SWA_GROUP + g:kh * SWA_GROUP + g + 1, :], (r, LANES))
                                for g in range(SWA_GROUP)], axis=0)
        row_max = jnp.maximum(jnp.max(s_c, axis=-1, keepdims=True), jnp.max(s_n, axis=-1, keepdims=True))
        m = jnp.maximum(jnp.broadcast_to(row_max, (rows_g, LANES)), sink)
        e_c = _bf(jnp.exp(s_c - jnp.concatenate([m] * bg, axis=1)))
        e_n = _bf(jnp.exp(s_n - m[:, :r]))
        den = (_dot(e_c, jnp.ones((bg * w, LANES), BF16)) + _dot(e_n, jnp.ones((r, LANES), BF16))
               + jnp.exp(sink - m))
        o = (_dot(e_c, v_cache) + _dot(e_n, _bf(v_new))) / den
        for i, pair in enumerate(_swa_merge_heads(o, kh, r)):
            col_i = kh * (SWA_GROUP // 2) + i
            o_ref[:, col_i * LANES:(col_i + 1) * LANES] = pair

    nk_ref[:, 0:w - ls, :] = ck_ref[:, ls:w, :]
    nv_ref[:, 0:w - ls, :] = cv_ref[:, ls:w, :]
    for b in range(bg):
        nk_ref[b, w - ls:w, :] = k_new[b * ls:(b + 1) * ls, :]
        nv_ref[b, w - ls:w, :] = v_new[b * ls:(b + 1) * ls, :]


def _swa_sample(proj, cache_k, cache_v, cq, sq, qg, kg, sink_rows, *, bg, ls):
    t = proj.shape[0]
    w = SWA_WINDOW
    r = bg * ls
    full = lambda a: pl.BlockSpec(a.shape, lambda i: (0,) * a.ndim)
    cache_spec = pl.BlockSpec((bg, w, LANES), lambda i: (i, 0, 0))
    return pl.pallas_call(
        functools.partial(_swa_sample_kernel, bg=bg, ls=ls),
        grid=(t // r,),
        in_specs=[
            pl.BlockSpec((r, 512), lambda i: (i, OD_Q_BLOCK)),
            pl.BlockSpec((r, 256), lambda i: (i, OD_KV_BLOCK)),
            cache_spec, cache_spec, full(cq), full(sq), full(qg), full(kg), full(sink_rows),
        ],
        out_specs=[pl.BlockSpec((r, 512), lambda i: (i, 0)), cache_spec, cache_spec],
        out_shape=[jax.ShapeDtypeStruct((t, 512), F32),
                   jax.ShapeDtypeStruct(cache_k.shape, F32), jax.ShapeDtypeStruct(cache_v.shape, F32)],
        compiler_params=_params("parallel"),
        name="swa_sample",
    )(proj, proj, cache_k, cache_v, cq, sq, qg, kg, sink_rows)


def _mem_prompt_kernel(x_ref, g_ref, wq_ref, qg_ref, k_ref, v_ref, wo_ref, o_ref):
    x = x_ref[0]
    xn = _bf(_rms(x, g_ref[...]))
    k_all = _bf(_mem_rows(k_ref, 0))
    v_all = _bf(_mem_rows(v_ref, 0))
    acc = x
    for h in range(MEM_HEADS):
        sl = slice(h * MEM_HD, (h + 1) * MEM_HD)
        q = _bf(_rms(_dot(xn, wq_ref[:, sl]), qg_ref[...]))
        s = _dot_nt(q, k_all[:, sl]) * (MEM_HD ** -0.5)
        e = jnp.exp(s - jnp.max(s, axis=-1, keepdims=True))
        p = e / jnp.sum(e, axis=-1, keepdims=True)
        o = _dot(_bf(p), v_all[:, sl])
        acc = acc + _dot(_bf(o), wo_ref[sl, :])
    o_ref[0] = acc


def _mem_prompt(x, gain, wq, q_gain, k, v, wo, *, layer, lt):
    b, l, d = x.shape
    k, v = _mem_flat_view(k), _mem_flat_view(v)
    kv_spec = pl.BlockSpec((None, 1, k.shape[2], LANES), lambda i, j: (layer, i, 0, 0))
    io_spec = pl.BlockSpec((1, lt, d), lambda i, j: (i, j, 0))
    full = lambda a: pl.BlockSpec(a.shape, lambda i, j: (0,) * a.ndim)
    resident = lambda a: pl.BlockSpec(a.shape, lambda i, j: (0,) * a.ndim, pipeline_mode=pl.Buffered(1))
    return pl.pallas_call(
        _mem_prompt_kernel,
        grid=(b, l // lt),
        in_specs=[io_spec, full(gain), resident(wq), full(q_gain), kv_spec, kv_spec, resident(wo)],
        out_specs=io_spec,
        out_shape=jax.ShapeDtypeStruct(x.shape, F32),
        compiler_params=_params("parallel", "arbitrary"),
        name="mem_prompt",
    )(x, gain, wq, q_gain, k, v, wo)


def _memkv_kernel(mem_ref, g_ref, wk_ref, wv_ref, kg_ref, k_out, v_out):
    xn = _bf(_rms(mem_ref[...], g_ref[...]))
    nb, n_mem, heads, hd = k_out.shape
    for h in range(heads):
        sl = slice(h * hd, (h + 1) * hd)
        k_out[:, :, h, :] = _rms(_dot(xn, wk_ref[:, sl]), kg_ref[...]).reshape(nb, n_mem, hd)
        v_out[:, :, h, :] = _dot(xn, wv_ref[:, sl]).reshape(nb, n_mem, hd)


def _memkv(mem, m_gain, wk, wv, k_gain):
    nb, n_mem, d = mem.shape
    depth = wk.shape[0]
    out_shape = jax.ShapeDtypeStruct((depth, nb, n_mem, MEM_HEADS, MEM_HD), F32)
    per_layer = lambda s: pl.BlockSpec((None,) + s, lambda l: (l,) + (0,) * len(s))
    return pl.pallas_call(
        _memkv_kernel,
        grid=(depth,),
        in_specs=[pl.BlockSpec((nb * n_mem, d), lambda l: (0, 0)), per_layer((1, d)), per_layer((d, d)),
                  per_layer((d, d)), per_layer((1, MEM_HD))],
        out_specs=[per_layer((nb, n_mem, MEM_HEADS, MEM_HD)), per_layer((nb, n_mem, MEM_HEADS, MEM_HD))],
        out_shape=[out_shape, out_shape],
        compiler_params=_params("arbitrary"),
        name="memkv",
    )(mem.reshape(nb * n_mem, d), m_gain, wk, wv, k_gain)


def _mem_sample_kernel(q_ref, k_ref, v_ref, o_ref, *, bs, ls):
    r = bs * ls
    per_seq = MEM_HEADS * ls
    n_exp = bs * per_seq
    n_mem = k_ref.shape[1] // MEM_ROW_GROUP
    hd_shift = MEM_HD.bit_length() - 1
    ls_shift = ls.bit_length() - 1
    seq_shift = per_seq.bit_length() - 1
    mem_shift = n_mem.bit_length() - 1
    qb = _bf(q_ref[...])
    e_row = _iota((n_exp, r), 0)
    sel = _bf((((e_row >> seq_shift) << ls_shift) + (e_row & (ls - 1)) == _iota((n_exp, r), 1)).astype(F32))
    head_mask = (((_iota((n_exp, D_MODEL), 0) >> ls_shift) & (MEM_HEADS - 1))
                 == (_iota((n_exp, D_MODEL), 1) >> hd_shift))
    q_exp = _bf(jnp.where(head_mask, _dot(sel, qb), 0.0))
    k_all = _bf(jnp.concatenate([_mem_rows(k_ref, b) for b in range(bs)], axis=0))
    v_all = _bf(jnp.concatenate([_mem_rows(v_ref, b) for b in range(bs)], axis=0))
    own = ((_iota((bs * n_mem, n_exp), 0) >> mem_shift) == (_iota((bs * n_mem, n_exp), 1) >> seq_shift))
    own = own.reshape(bs, n_mem, n_exp)
    s = (_dot_nt(k_all, q_exp) * (MEM_HD ** -0.5)).reshape(bs, n_mem, n_exp)
    s = jnp.where(own, s, -1e30)
    e = jnp.where(own, jnp.exp(s - jnp.max(s, axis=1, keepdims=True)), 0.0)
    den = jnp.sum(e, axis=1, keepdims=True) + jnp.where(jnp.any(own, axis=1, keepdims=True), 0.0, 1.0)
    p = _bf((e / den).reshape(bs * n_mem, n_exp))
    o_all = jnp.where(head_mask, _dot_tn(p, v_all), 0.0)
    o_ref[...] = _dot_tn(sel, _bf(o_all))


MEM_LANE_TILES = MEM_HD // LANES
MEM_ROW_GROUP = MEM_HEADS * MEM_LANE_TILES


def _mem_rows(ref, b):
    n_mem = ref.shape[1] // MEM_ROW_GROUP
    return jnp.concatenate([ref[b, pl.ds(lt * MEM_HEADS + h, n_mem, stride=MEM_ROW_GROUP), :]
                            for h in range(MEM_HEADS) for lt in range(MEM_LANE_TILES)], axis=1)


def _mem_flat_view(a):
    depth, nb, n_mem, heads, hd = a.shape
    a = a.reshape(depth, nb, n_mem, heads, hd // LANES, LANES).transpose(0, 1, 2, 4, 3, 5)
    return a.reshape(depth, nb, n_mem * MEM_ROW_GROUP, LANES)


def _mem_sample(q, k, v, *, layer, bs, ls):
    t, d = q.shape
    k, v = _mem_flat_view(k), _mem_flat_view(v)
    r = bs * ls
    kv_spec = pl.BlockSpec((None, bs, k.shape[2], LANES), lambda i: (layer, i, 0, 0))
    io_spec = pl.BlockSpec((r, d), lambda i: (i, 0))
    return pl.pallas_call(
        functools.partial(_mem_sample_kernel, bs=bs, ls=ls),
        grid=(t // r,),
        in_specs=[io_spec, kv_spec, kv_spec],
        out_specs=io_spec,
        out_shape=jax.ShapeDtypeStruct(q.shape, F32),
        compiler_params=_params("parallel"),
        name="mem_sample",
    )(q, k, v)


def _retention_tables(pos):
    inv = 1.0 / (RET_THETA ** jnp.linspace(0.0, 1.0, RET_DK // 2, dtype=F32))
    ang = pos[:, None] * inv[None, :]
    cos = jnp.repeat(jnp.cos(ang), 2, axis=1)
    sin = jnp.stack([-jnp.sin(ang), jnp.sin(ang)], axis=-1).reshape(pos.shape[0], RET_DK)
    return jnp.tile(cos, (1, PAIR)), jnp.tile(sin, (1, PAIR))


def _rope_tables(pos):
    half = ROPE_HALF
    inv = 1.0 / (ROPE_THETA ** (jnp.arange(half, dtype=F32) * 2.0 / (2 * half)))
    ang = pos[:, None] * inv[None, :]
    n = pos.shape[0]
    rest = SWA_HD - 2 * half
    cos = jnp.concatenate([jnp.cos(ang), jnp.cos(ang), jnp.ones((n, rest), F32)], axis=1)
    sin = jnp.concatenate([-jnp.sin(ang), jnp.sin(ang), jnp.zeros((n, rest), F32)], axis=1)
    return jnp.tile(cos, (1, PAIR)), jnp.tile(sin, (1, PAIR))


def _block_diag(t):
    g, a, b = t.shape
    eye = jnp.eye(g, dtype=t.dtype)
    return (t[:, :, None, :] * eye[:, None, :, None]).reshape(g * a, g * b)


def _half_block_diag(t):
    per = t.shape[0] // S5_HALVES
    return _bf(jnp.stack([_block_diag(t[h * per:(h + 1) * per]) for h in range(S5_HALVES)]))


def _sink_rows(sinks):
    return jnp.broadcast_to(sinks.astype(F32)[:, None], (sinks.shape[0], LANES))


TILES = dict(
    even_prompt_rows=512,
    mem_prompt_rows=1024,
    sample_rows=512,
    proj_cols=640,
    even_sample_seqs=16,
    swa_sample_seqs=8,
    mem_sample_seqs=4,
    s5_groups_per_pass=8,
    s5_unroll=4,
)
def _trunk(x3, pos0, states, mem_k, mem_v, w, *, sample):
    b, l, d = x3.shape
    t = b * l
    x = x3.reshape(t, d)
    pos = pos0 + jnp.arange(l, dtype=F32)
    tm = TILES['sample_rows']
    gla_s, ret_s, s5_re, s5_im, swa_k, swa_v = states
    out_states = {k: [] for k in ("gla", "ret", "s5_re", "s5_im", "swa_k", "swa_v")}
    ld_row = jnp.repeat(jnp.log(1.0 - 2.0 ** (-5.0 - jnp.arange(RET_HEADS, dtype=F32))), RET_DK)[None, :]

    for layer in range(2):
        i = layer // 2
        x = _ffn(x, w['ffn1_norm'][layer][None], w['ffn1_w_gate'], w['ffn1_w_up'], w['ffn1_w_down'],
                 layer=layer)
        if layer % 2 == 0:
            cos, sin = _retention_tables(pos)
            args = (ld_row, w['gla_w_gate'][i], w['gla_b_gate'][i], w['gla_out_norm'][i])
            if sample:
                bg = TILES['even_sample_seqs']
                proj = _norm_matmul(x, w['mix_norm'][layer][None], w['even_w_in'][i], tm=tm, tn=TILES['proj_cols'])
                mixed, g_s, r_s = _even_sample(proj, jnp.tile(cos, (bg, 1)), jnp.tile(sin, (bg, 1)), *args,
                                               gla_s[i], ret_s[i], bg=bg, ls=l)
                x = _matmul_residual(x, [(mixed, w['even_w_out'][i])], tm=tm)
            else:
                x, g_s, r_s = _even_prompt(x.reshape(b, l, d), w['mix_norm'][layer][None], w['even_w_in'][i],
                                           w['even_w_out'][i], cos, sin, *args, gla_s[i], ret_s[i], lt=TILES['even_prompt_rows'])
                x = x.reshape(t, d)
            out_states["gla"].append(g_s)
            out_states["ret"].append(r_s)
        else:
            cq, sq = _rope_tables(pos)
            qg = jnp.tile(w['swa_q_norm'][i], SWA_HEADS)[None, :]
            kg = jnp.tile(w['swa_k_norm'][i], SWA_KV_HEADS)[None, :]
            s5_args = (w['s5_ab_re'][i], w['s5_ab_im'][i], w['s5_bb_re'][i], w['s5_bb_im'][i],
                       w['s5_cc_re'][i], w['s5_cc_im'][i], w['s5_d'][i][None], w['s5_w_glu'][i], w['s5_b_glu'][i][None])
            n_state = S5_GROUPS * S5_STATE
            s0_re, s0_im = s5_re[i].reshape(b, n_state), s5_im[i].reshape(b, n_state)
            sinks = _sink_rows(w['swa_sinks'][i])
            if sample:
                proj = _norm_matmul(x, w['mix_norm'][layer][None], w['odd_w_in'][i], tm=tm, tn=TILES['proj_cols'])
                c_out, sr, si = _s5(proj.reshape(1, t, OD_COLS), s0_re, s0_im, *s5_args, n_seq=b, lt=l,
                                    groups_per_pass=1, unroll=True)
                bg = TILES['swa_sample_seqs']
                d_out, kb, vb = _swa_sample(proj, swa_k[i].reshape(b, SWA_WINDOW, LANES),
                                            swa_v[i].reshape(b, SWA_WINDOW, LANES),
                                            jnp.tile(cq, (bg, 1)), jnp.tile(sq, (bg, 1)), qg, kg, sinks, bg=bg, ls=l)
                w_out = w['odd_w_out'][i]
                x = _matmul_residual(x, [(c_out.reshape(t, S5_WIDTH), w_out[:S5_WIDTH]), (d_out, w_out[S5_WIDTH:])],
                                     tm=tm)
            else:
                x, sr, si, kb, vb = _odd_prompt(x.reshape(b, l, d), w['mix_norm'][layer][None], w['odd_w_in'][i],
                                                w['odd_w_out'][i], s0_re, s0_im, *s5_args, cq, sq, qg, kg, sinks,
                                                groups_per_pass=TILES['s5_groups_per_pass'], unroll=TILES['s5_unroll'])
                x = x.reshape(t, d)
            out_states["s5_re"].append(sr.reshape(b, S5_GROUPS, S5_STATE))
            out_states["s5_im"].append(si.reshape(b, S5_GROUPS, S5_STATE))
            out_states["swa_k"].append(kb.reshape(b, -1, SWA_KV_HEADS, SWA_HD))
            out_states["swa_v"].append(vb.reshape(b, -1, SWA_KV_HEADS, SWA_HD))
        if sample:
            q = _norm_matmul(x, w['mem_x_norm'][layer][None], w['mem_w_q'][layer], tm=tm, tn=MEM_HD,
                             head_gain=w['mem_q_norm'][layer][None], n_norm_tiles=MEM_HEADS)
            o = _mem_sample(q, mem_k, mem_v, layer=layer, bs=TILES['mem_sample_seqs'], ls=l)
            x = _matmul_residual(x, [(o, w['mem_w_o'][layer])], tm=tm)
        else:
            x = _mem_prompt(x.reshape(b, l, d), w['mem_x_norm'][layer][None], w['mem_w_q'][layer],
                            w['mem_q_norm'][layer][None], mem_k, mem_v, w['mem_w_o'][layer],
                            layer=layer, lt=TILES['mem_prompt_rows']).reshape(t, d)
        x = _ffn(x, w['ffn2_norm'][layer][None], w['ffn2_w_gate'], w['ffn2_w_up'], w['ffn2_w_down'],
                 layer=layer)
    return x.reshape(b, l, d), {k: jnp.stack(v) for k, v in out_states.items()}


def kernel(x_prompt, x_sample, mem_prompt, state_gla, state_ret, state_s5_re, state_s5_im, cache_swa_k, cache_swa_v, cache_mem_k, cache_mem_v, ffn1_norm, ffn1_w_gate, ffn1_w_up, ffn1_w_down, ffn2_norm, ffn2_w_gate, ffn2_w_up, ffn2_w_down, mix_norm, even_w_in, gla_w_gate, gla_b_gate, gla_out_norm, even_w_out, odd_w_in, s5_a_re, s5_a_im, s5_log_step, s5_b_re, s5_b_im, s5_c_re, s5_c_im, s5_d, s5_w_glu, s5_b_glu, swa_q_norm, swa_k_norm, swa_sinks, odd_w_out, mem_x_norm, mem_m_norm, mem_w_q, mem_w_k, mem_w_v, mem_w_o, mem_q_norm, mem_k_norm):
    depth = ffn1_norm.shape[0]
    n_even, n_odd = even_w_in.shape[0], odd_w_in.shape[0]
    batch, seq, d = x_prompt.shape
    dec_batch = x_sample.shape[0]
    n_mem = mem_prompt.shape[1]

    ev = even_w_in
    ev_cols = jnp.concatenate(
        [ev[..., 0:1536], ev[..., 1552:3088], ev[..., 1536:1552],
         jnp.zeros(ev.shape[:2] + (EV_COLS - 3088,), ev.dtype)], axis=-1)
    wgate_pad = jnp.concatenate(
        [gla_w_gate, jnp.zeros((n_even, EV_COLS - EV_GA - GLA_RANK, gla_w_gate.shape[-1]), gla_w_gate.dtype)], axis=1)
    w = dict(
        ffn1_norm=ffn1_norm, ffn2_norm=ffn2_norm, mix_norm=mix_norm, mem_x_norm=mem_x_norm,
        ffn1_w_gate=_bf(ffn1_w_gate), ffn1_w_up=_bf(ffn1_w_up), ffn1_w_down=_bf(ffn1_w_down),
        ffn2_w_gate=_bf(ffn2_w_gate), ffn2_w_up=_bf(ffn2_w_up), ffn2_w_down=_bf(ffn2_w_down),
        even_w_in=_bf(ev_cols), gla_w_gate=_bf(wgate_pad), gla_b_gate=gla_b_gate[:, None, :],
        gla_out_norm=gla_out_norm[:, None, :], even_w_out=_bf(even_w_out),
        odd_w_in=_bf(odd_w_in), odd_w_out=_bf(odd_w_out), s5_d=s5_d, s5_w_glu=_bf(s5_w_glu), s5_b_glu=s5_b_glu,
        swa_q_norm=swa_q_norm, swa_k_norm=swa_k_norm, swa_sinks=swa_sinks,
        mem_w_q=_bf(mem_w_q), mem_w_o=_bf(mem_w_o), mem_q_norm=mem_q_norm,
    )
    ab_re, ab_im, bb_re, bb_im = [], [], [], []
    for i in range(n_odd):
        a_r, a_i, b_r, b_i = _s5_prep(s5_a_re[i], s5_a_im[i], s5_log_step[i], s5_b_re[i], s5_b_im[i])
        ab_re.append(a_r.reshape(-1))
        ab_im.append(a_i.reshape(-1))
        bb_re.append(_half_block_diag(b_r))
        bb_im.append(_half_block_diag(b_i))
    w.update(s5_ab_re=ab_re, s5_ab_im=ab_im, s5_bb_re=bb_re, s5_bb_im=bb_im,
             s5_cc_re=[_half_block_diag(jnp.swapaxes(s5_c_re[i], 1, 2)) for i in range(n_odd)],
             s5_cc_im=[_half_block_diag(jnp.swapaxes(s5_c_im[i], 1, 2)) for i in range(n_odd)])

    p_mem_k, p_mem_v = _memkv(mem_prompt, mem_m_norm[:, None, :], _bf(mem_w_k), _bf(mem_w_v), mem_k_norm[:, None, :])

    zeros = lambda *s: jnp.zeros(s, F32)
    p_states = (zeros(n_even, batch, GLA_HEADS, GLA_DK, GLA_DV), zeros(n_even, batch, RET_HEADS, RET_DK, GLA_DV),
                zeros(n_odd, batch, S5_GROUPS, S5_STATE), zeros(n_odd, batch, S5_GROUPS, S5_STATE), None, None)
    y_prompt, ps = _trunk(x_prompt, 0.0, p_states, p_mem_k, p_mem_v, w, sample=False)

    s_states = (state_gla, state_ret, state_s5_re, state_s5_im, cache_swa_k, cache_swa_v)
    y_sample, ss = _trunk(x_sample, float(PAST_LEN), s_states, cache_mem_k, cache_mem_v, w, sample=True)

    return (y_prompt, y_sample, ps["gla"], ps["ret"], ps["s5_re"], ps["s5_im"], ps["swa_k"], ps["swa_v"],
            p_mem_k, p_mem_v, ss["gla"], ss["ret"], ss["s5_re"], ss["s5_im"], ss["swa_k"], ss["swa_v"])
```

```python
import functools
import math

import jax
import jax.numpy as jnp
from jax import lax
from jax.experimental import pallas as pl
from jax.experimental.pallas import tpu as pltpu

F32 = jnp.float32
BF16 = jnp.bfloat16
NORM_EPS = 1e-6

D_MODEL = 1024
GLA_HEADS = 4
GLA_DK = 64
GLA_DV = 128
GLA_RANK = 16
GLA_TAU = 16.0
RET_HEADS = 4
RET_DK = 64
RET_THETA = 10000.0
LA_CHUNK = 64
S5_WIDTH = 512
S5_GROUP = 16
S5_GROUPS = 32
S5_STATE = 64
SWA_HD = 64
SWA_HEADS = 8
SWA_KV_HEADS = 2
SWA_WINDOW = 128
ROPE_THETA = 500000.0
MEM_HEADS = 4
MEM_HD = 256
PAST_LEN = 8192

VMEM_LIMIT_BYTES = 52 * 1024 * 1024
LANES = 128
SUBLANES = 8


def _params(*sem):
    return pltpu.CompilerParams(dimension_semantics=sem, vmem_limit_bytes=VMEM_LIMIT_BYTES)


def _rms(x, gain=None):
    y = x * lax.rsqrt(jnp.mean(x * x, axis=-1, keepdims=True) + NORM_EPS)
    return y if gain is None else y * gain


def _dot(a, b):
    return jnp.dot(a, b, preferred_element_type=F32)


def _dot_nt(a, b):
    return lax.dot_general(a, b, (((1,), (1,)), ((), ())), preferred_element_type=F32)


def _dot_tn(a, b):
    return lax.dot_general(a, b, (((0,), (0,)), ((), ())), preferred_element_type=F32)


def _split_bf16(x, terms):
    pieces = []
    for _ in range(terms):
        piece = _bf(x)
        pieces.append(piece)
        x = x - piece.astype(F32)
    return pieces


def _dot_exact_lhs(a, x, terms=3):
    return sum(_dot(a, piece) for piece in _split_bf16(x, terms))


def _dot_exact_rhs(x, b, terms=3):
    return sum(_dot(piece, b) for piece in _split_bf16(x, terms))


def _dot_nt_exact_lhs(a, x, terms=3):
    return sum(_dot_nt(a, piece) for piece in _split_bf16(x, terms))


def _bf(x):
    return x.astype(BF16)


def _log_sigmoid(x):
    return jnp.minimum(x, 0.0) - jnp.log(1.0 + jnp.exp(-jnp.abs(x)))


def _iota(shape, dim):
    return lax.broadcasted_iota(jnp.int32, shape, dim)


def _lane_tile(x, n):
    return jnp.concatenate([x] * n, axis=1)


def _swap_pairs(x):
    n = x.shape[-1]
    even = (_iota(x.shape, 1) & 1) == 0
    return jnp.where(even, pltpu.roll(x, n - 1, 1), pltpu.roll(x, 1, 1))


def _rope_partner(x, head_dim, half):
    n = x.shape[-1]
    first = (_iota(x.shape, 1) & (head_dim - 1)) < half
    return jnp.where(first, pltpu.roll(x, n - half, 1), pltpu.roll(x, half, 1))


FFN_ROW_TILE = 512


def _ffn_kernel(x_ref, g_ref, wg_ref, wu_ref, wd_ref, o_ref):
    x = x_ref[...]
    xn = _bf(_rms(x, g_ref[...]))
    gate = _dot(xn, wg_ref[...])
    up = _dot(xn, wu_ref[...])
    o_ref[...] = x + _dot(_bf(jax.nn.silu(gate) * up * 0.5), wd_ref[...])


def _ffn(x, gain, wg, wu, wd, *, layer):
    t, d = x.shape
    h = wg.shape[2]
    tm = FFN_ROW_TILE
    resident = dict(pipeline_mode=pl.Buffered(1))
    return pl.pallas_call(
        _ffn_kernel,
        grid=(t // tm,),
        in_specs=[
            pl.BlockSpec((tm, d), lambda i: (i, 0)),
            pl.BlockSpec((1, d), lambda i: (0, 0)),
            pl.BlockSpec((None, d, h), lambda i: (layer, 0, 0), **resident),
            pl.BlockSpec((None, d, h), lambda i: (layer, 0, 0), **resident),
            pl.BlockSpec((None, h, d), lambda i: (layer, 0, 0), **resident),
        ],
        out_specs=pl.BlockSpec((tm, d), lambda i: (i, 0)),
        out_shape=jax.ShapeDtypeStruct((t, d), F32),
        compiler_params=_params("parallel"),
        name="ffn",
    )(x, gain, wg, wu, wd)


def _nmm_kernel(x_ref, g_ref, w_ref, hg_ref, o_ref, xn_ref, *, n_norm_tiles):
    j = pl.program_id(1)

    @pl.when(j == 0)
    def _():
        xn_ref[...] = _bf(_rms(x_ref[...], g_ref[...]))

    y = _dot(xn_ref[...], w_ref[...])
    if n_norm_tiles == 0:
        o_ref[...] = y
    else:
        @pl.when(j < n_norm_tiles)
        def _():
            o_ref[...] = _rms(y, hg_ref[...])

        @pl.when(j >= n_norm_tiles)
        def _():
            o_ref[...] = y


def _norm_matmul(x, gain, w, *, tm, tn, head_gain=None, n_norm_tiles=0):
    t, d = x.shape
    n = w.shape[1]
    if head_gain is None:
        head_gain = jnp.ones((1, tn), F32)
    return pl.pallas_call(
        functools.partial(_nmm_kernel, n_norm_tiles=n_norm_tiles),
        grid=(t // tm, n // tn),
        in_specs=[
            pl.BlockSpec((tm, d), lambda i, j: (i, 0)),
            pl.BlockSpec((1, d), lambda i, j: (0, 0)),
            pl.BlockSpec((d, tn), lambda i, j: (0, j)),
            pl.BlockSpec((1, tn), lambda i, j: (0, 0)),
        ],
        out_specs=pl.BlockSpec((tm, tn), lambda i, j: (i, j)),
        out_shape=jax.ShapeDtypeStruct((t, n), F32),
        scratch_shapes=[pltpu.VMEM((tm, d), BF16)],
        compiler_params=_params("parallel", "arbitrary"),
        name="norm_matmul",
    )(x, gain, w, head_gain)


def _mmr_kernel(*refs, n_terms):
    x_ref = refs[0]
    a_refs = refs[1:1 + n_terms]
    w_refs = refs[1 + n_terms:1 + 2 * n_terms]
    o_ref = refs[1 + 2 * n_terms]
    acc = x_ref[...]
    for a_ref, w_ref in zip(a_refs, w_refs):
        acc = acc + _dot(_bf(a_ref[...]), w_ref[...])
    o_ref[...] = acc


def _matmul_residual(x, terms, *, tm):
    t, d = x.shape
    acts = [a for a, _ in terms]
    ws = [w for _, w in terms]
    in_specs = [pl.BlockSpec((tm, d), lambda i: (i, 0))]
    in_specs += [pl.BlockSpec((tm, a.shape[1]), lambda i: (i, 0)) for a in acts]
    in_specs += [pl.BlockSpec(w.shape, lambda i: (0, 0)) for w in ws]
    return pl.pallas_call(
        functools.partial(_mmr_kernel, n_terms=len(terms)),
        grid=(t // tm,),
        in_specs=in_specs,
        out_specs=pl.BlockSpec((tm, d), lambda i: (i, 0)),
        out_shape=jax.ShapeDtypeStruct((t, d), F32),
        compiler_params=_params("parallel"),
        name="matmul_residual",
    )(x, *acts, *ws)


EV_GQ, EV_GK, EV_GV, EV_GG = 0, 256, 512, 1024
EV_RQ, EV_RK, EV_RV, EV_RG = 1536, 1792, 2048, 2560
EV_GA = 3072
EV_COLS = 3200
EV_BLOCK = 256
PAIR = 2


def _gate_and_norm(o, gate, gain=None):
    return _rms(o, gain) * jax.nn.silu(gate)


def _even_prompt_kernel(x_ref, xnext_ref, gain_ref, win_ref, wout_ref, cos_ref, sin_ref, ld_ref, wgate_ref, bgate_ref,
                        gnorm_ref, s0g_ref, s0r_ref, o_ref, sg_ref, sr_ref, proj_a, proj_b, mix_sc, o_sc,
                        *, chunk, n_chunks, tiles_per_seq):
    n = pl.program_id(0)

    @pl.when(n % tiles_per_seq == 0)
    def _():
        sg_ref[...] = s0g_ref[...]
        sr_ref[...] = s0r_ref[...]

    @pl.when(n == 0)
    def _():
        proj_a[...] = _dot(_bf(_rms(x_ref[...], gain_ref[...])), win_ref[...])

    args = (x_ref, xnext_ref, gain_ref, win_ref, wout_ref, cos_ref, sin_ref, ld_ref, wgate_ref, bgate_ref, gnorm_ref,
            o_ref, sg_ref, sr_ref)

    @pl.when(n % 2 == 0)
    def _():
        _even_prompt_tile(*args, proj_a, proj_b, mix_sc, o_sc, chunk=chunk, n_chunks=n_chunks)

    @pl.when(n % 2 == 1)
    def _():
        _even_prompt_tile(*args, proj_b, proj_a, mix_sc, o_sc, chunk=chunk, n_chunks=n_chunks)


def _even_prompt_tile(x_ref, xnext_ref, gain_ref, win_ref, wout_ref, cos_ref, sin_ref, ld_ref, wgate_ref, bgate_ref,
                      gnorm_ref, o_ref, sg_ref, sr_ref, proj_sc, proj_next, mix_sc, o_sc, *, chunk, n_chunks):
    c = chunk
    lt = c * n_chunks
    blk = min(EV_BLOCK, lt)
    c_shift = c.bit_length() - 1
    x = x_ref[...]
    xn_next = _bf(_rms(xnext_ref[...], gain_ref[...]))
    piece_cols = [(a, min(a + 256, EV_COLS)) for a in range(0, EV_COLS, 256)]
    n_serial = (GLA_HEADS + RET_HEADS) // PAIR * n_chunks

    def issue_next_projection(step_index):
        for i, (a, b) in enumerate(piece_cols):
            if (i * n_serial) // len(piece_cols) == step_index:
                proj_next[:, a:b] = _dot(xn_next, win_ref[:, a:b])

    def cols(a, b):
        return proj_sc[:, a:b]

    log_a = _log_sigmoid(_dot(_bf(cols(EV_GA, EV_COLS)), wgate_ref[...]) + bgate_ref[...]) * (1.0 / GLA_TAU)
    tril = _bf((_iota((c, c), 1) <= _iota((c, c), 0)).astype(F32))
    cum_parts = [_dot_exact_lhs(tril, log_a[i * c:(i + 1) * c]) for i in range(n_chunks)]
    tots = [p[c - 1:c] for p in cum_parts]
    cum = jnp.concatenate(cum_parts, axis=0)
    tot_b = jnp.concatenate([jnp.broadcast_to(t, (c, 256)) for t in tots], axis=0)
    k = cols(EV_GK, EV_GK + 256)
    gla = (cols(EV_GQ, EV_GQ + 256) * (GLA_DK ** -0.5) * jnp.exp(cum), k * jnp.exp(-cum), k * jnp.exp(tot_b - cum))

    ld = ld_ref[...]
    tpos = ((_iota((lt, 1), 0) & (c - 1)) + 1).astype(F32)
    cum_r = tpos * ld
    tot_r = float(c) * ld
    cos, sin = _lane_tile(cos_ref[...], PAIR), _lane_tile(sin_ref[...], PAIR)
    rq = cols(EV_RQ, EV_RQ + 256)
    rk = cols(EV_RK, EV_RK + 256)
    q_rot = rq * cos + _swap_pairs(rq) * sin
    k_rot = (rk * cos + _swap_pairs(rk) * sin) * (RET_DK ** -0.5)
    ret = (q_rot * jnp.exp(cum_r), k_rot * jnp.exp(-cum_r), k_rot * jnp.exp(tot_r - cum_r))

    tot_rows = jnp.concatenate(tots + [tot_r, jnp.zeros((LANES - n_chunks - 1, 256), F32)], axis=0)
    decay_cols = jnp.exp(jnp.transpose(tot_rows))

    row = _iota((blk, blk), 0)
    col = _iota((blk, blk), 1)
    blk_mask = ((row >> c_shift) == (col >> c_shift)) & (col <= row)
    lo = (_iota((lt, LANES), 1) < GLA_DK)
    gnorm = gnorm_ref[...]

    mixers = ((gla, EV_GV, EV_GG, sg_ref, 0, gnorm, lambda i: i),
              (ret, EV_RV, EV_RG, sr_ref, 512, None, lambda i: n_chunks))
    for m, ((q_dec, k_inv, k_dec), v_col, g_col, s_ref, out_col, gain, decay_col_of) in enumerate(mixers):
        for p in range(GLA_HEADS // PAIR):
            lanes = slice(p * LANES, (p + 1) * LANES)
            q_pair = q_dec[:, lanes]
            q_masked = [_bf(jnp.where(lo, q_pair, 0.0)), _bf(jnp.where(lo, 0.0, q_pair))]
            ki = _bf(k_inv[:, lanes])
            kd = _bf(k_dec[:, lanes])
            v_pair = _bf(cols(v_col + p * PAIR * GLA_DV, v_col + (p + 1) * PAIR * GLA_DV))
            for e in range(PAIR):
                slot = (m * (GLA_HEADS // PAIR) + p) * PAIR + e
                for r0 in range(0, lt, blk):
                    rs = slice(r0, r0 + blk)
                    scores = jnp.where(blk_mask, _dot_nt(q_masked[e][rs], ki[rs]), 0.0)
                    o_sc[slot, rs, :] = _dot(_bf(scores), v_pair[rs, e * GLA_DV:(e + 1) * GLA_DV])
            state = s_ref[0, p * PAIR:(p + 1) * PAIR].reshape(PAIR * GLA_DK, GLA_DV)
            for i in range(n_chunks):
                issue_next_projection((m * (GLA_HEADS // PAIR) + p) * n_chunks + i)
                rs = slice(i * c, (i + 1) * c)
                q_stack = jnp.concatenate([q_masked[0][rs], q_masked[1][rs]], axis=0)
                o_inter = _dot(q_stack, _bf(state))
                kv = _dot_tn(kd[rs], v_pair[rs])
                kv = jnp.concatenate([kv[:GLA_DK, :GLA_DV], kv[GLA_DK:, GLA_DV:]], axis=0)
                ci = decay_col_of(i)
                state = state * decay_cols[p * LANES:(p + 1) * LANES, ci:ci + 1] + kv
                for e in range(PAIR):
                    slot = (m * (GLA_HEADS // PAIR) + p) * PAIR + e
                    o_sc[slot, rs, :] += o_inter[e * c:(e + 1) * c]
            s_ref[0, p * PAIR:(p + 1) * PAIR] = state.reshape(PAIR, GLA_DK, GLA_DV)
            for e in range(PAIR):
                h = p * PAIR + e
                slot = (m * (GLA_HEADS // PAIR) + p) * PAIR + e
                gate = cols(g_col + h * GLA_DV, g_col + (h + 1) * GLA_DV)
                mix_sc[:, out_col + h * GLA_DV:out_col + (h + 1) * GLA_DV] = _gate_and_norm(o_sc[slot], gate, gain)
    o_ref[...] = x + _dot(_bf(mix_sc[...]), wout_ref[...])


def _even_prompt(x, gain, w_in, w_out, cos, sin, ld_row, wgate, bgate, gnorm, s0g, s0r, *, lt):
    b, l, d = x.shape
    chunk = math.gcd(l, LA_CHUNK)
    nj = l // lt
    n_tiles = b * nj
    x2 = x.reshape(b * l, d)
    st_spec = pl.BlockSpec((1, GLA_HEADS, GLA_DK, GLA_DV), lambda n: (n // nj, 0, 0, 0))
    io_spec = pl.BlockSpec((lt, d), lambda n: (n, 0))
    next_spec = pl.BlockSpec((lt, d), lambda n: (jnp.minimum(n + 1, n_tiles - 1), 0))
    tab_spec = pl.BlockSpec((lt, LANES), lambda n: (n % nj, 0))
    full = lambda a: pl.BlockSpec(a.shape, lambda n: (0,) * a.ndim)
    out, sg, sr = pl.pallas_call(
        functools.partial(_even_prompt_kernel, chunk=chunk, n_chunks=lt // chunk, tiles_per_seq=nj),
        grid=(n_tiles,),
        in_specs=[
            io_spec, next_spec, full(gain), full(w_in), full(w_out), tab_spec, tab_spec,
            full(ld_row), full(wgate), full(bgate), full(gnorm), st_spec, st_spec,
        ],
        out_specs=[io_spec, st_spec, st_spec],
        out_shape=[jax.ShapeDtypeStruct(x2.shape, F32),
                   jax.ShapeDtypeStruct(s0g.shape, F32), jax.ShapeDtypeStruct(s0r.shape, F32)],
        scratch_shapes=[pltpu.VMEM((lt, EV_COLS), F32), pltpu.VMEM((lt, EV_COLS), F32), pltpu.VMEM((lt, d), F32),
                        pltpu.VMEM((GLA_HEADS + RET_HEADS, lt, GLA_DV), F32)],
        compiler_params=_params("arbitrary"),
        name="even_prompt",
    )(x2, x2, gain, w_in, w_out, cos, sin, ld_row, wgate, bgate, gnorm, s0g, s0r)
    return out.reshape(b, l, d), sg, sr


def _even_sample_kernel(proj_ref, cos_ref, sin_ref, ld_ref, wgate_ref, bgate_ref, gnorm_ref, s0g_ref, s0r_ref,
                        mix_ref, sg_ref, sr_ref, *, bg, ls):
    r = bg * ls
    ls_shift = ls.bit_length() - 1
    dk_shift = GLA_DK.bit_length() - 1
    n_exp = bg * GLA_DK
    row_seq = _iota((r, r), 0) >> ls_shift
    col_seq = _iota((r, r), 1) >> ls_shift
    same = row_seq == col_seq
    seg = same & (_iota((r, r), 1) <= _iota((r, r), 0))
    seg_b = _bf(seg.astype(F32))
    same_b = _bf(same.astype(F32))
    tile_b = _bf(((_iota((GLA_DK, n_exp), 1) & (GLA_DK - 1)) == _iota((GLA_DK, n_exp), 0)).astype(F32))
    tile_t_b = _bf(((_iota((n_exp, GLA_DK), 0) & (GLA_DK - 1)) == _iota((n_exp, GLA_DK), 1)).astype(F32))
    q_mask = (_iota((r, n_exp), 0) >> ls_shift) == (_iota((r, n_exp), 1) >> dk_shift)
    k_mask = (_iota((n_exp, r), 0) >> dk_shift) == (_iota((n_exp, r), 1) >> ls_shift)
    tpos = ((_iota((r, 1), 0) & (ls - 1)) + 1).astype(F32)
    ld = ld_ref[...]
    cum_r = tpos * ld
    tot_r = float(ls) * ld
    r_dec = jnp.exp(tot_r)
    gnorm = gnorm_ref[...]

    def mixer(q_dec, k_inv, k_dec, v_col, g_col, s0_ref, s_ref, out_col, decay_of, gain):
        for h in range(GLA_HEADS):
            sl = slice(h * GLA_DK, (h + 1) * GLA_DK)
            v = _bf(proj_ref[:, v_col + h * GLA_DV:v_col + (h + 1) * GLA_DV])
            qd = _bf(q_dec[:, sl])
            scores = jnp.where(seg, _dot_nt(qd, _bf(k_inv[:, sl])), 0.0)
            state = s0_ref[:, h].reshape(n_exp, GLA_DV)
            q_exp = _bf(jnp.where(q_mask, _dot(qd, tile_b), 0.0))
            o = _dot(_bf(scores), v) + _dot(q_exp, _bf(state))
            k_exp = _bf(jnp.where(k_mask, _dot_nt(tile_t_b, _bf(k_dec[:, sl])), 0.0))
            new_state = state * decay_of(h, sl) + _dot(k_exp, v)
            s_ref[:, h] = new_state.reshape(bg, GLA_DK, GLA_DV)
            gate = proj_ref[:, g_col + h * GLA_DV:g_col + (h + 1) * GLA_DV]
            mix_ref[:, out_col + h * GLA_DV:out_col + (h + 1) * GLA_DV] = _gate_and_norm(o, gate, gain)

    log_a = _log_sigmoid(_dot(_bf(proj_ref[:, EV_GA:EV_COLS]), wgate_ref[...]) + bgate_ref[...]) * (1.0 / GLA_TAU)
    cum = _dot_exact_lhs(seg_b, log_a)
    tot = _dot_exact_lhs(same_b, log_a)
    k = proj_ref[:, EV_GK:EV_GK + 256]

    def gla_decay(h, sl):
        la_exp = jnp.where(k_mask, _dot_nt_exact_lhs(tile_t_b, log_a[:, sl]), 0.0)
        return jnp.exp(jnp.sum(la_exp, axis=-1, keepdims=True))

    mixer(proj_ref[:, EV_GQ:EV_GQ + 256] * (GLA_DK ** -0.5) * jnp.exp(cum), k * jnp.exp(-cum), k * jnp.exp(tot - cum),
          EV_GV, EV_GG, s0g_ref, sg_ref, 0, gla_decay, gnorm)

    cos = _lane_tile(cos_ref[...], PAIR)
    sin = _lane_tile(sin_ref[...], PAIR)
    rq = proj_ref[:, EV_RQ:EV_RQ + 256]
    rk = proj_ref[:, EV_RK:EV_RK + 256]
    q_rot = rq * cos + _swap_pairs(rq) * sin
    k_rot = (rk * cos + _swap_pairs(rk) * sin) * (RET_DK ** -0.5)

    def ret_decay(h, sl):
        return r_dec[:, h * RET_DK:h * RET_DK + 1]

    mixer(q_rot * jnp.exp(cum_r), k_rot * jnp.exp(-cum_r), k_rot * jnp.exp(tot_r - cum_r),
          EV_RV, EV_RG, s0r_ref, sr_ref, 512, ret_decay, None)


def _even_sample(proj, cos, sin, ld_row, wgate, bgate, gnorm, s0g, s0r, *, bg, ls):
    t = proj.shape[0]
    n_b = t // ls
    r = bg * ls
    st_spec = pl.BlockSpec((bg, GLA_HEADS, GLA_DK, GLA_DV), lambda i: (i, 0, 0, 0))
    full = lambda a: pl.BlockSpec(a.shape, lambda i: (0,) * a.ndim)
    return pl.pallas_call(
        functools.partial(_even_sample_kernel, bg=bg, ls=ls),
        grid=(n_b // bg,),
        in_specs=[
            pl.BlockSpec((r, EV_COLS), lambda i: (i, 0)),
            full(cos), full(sin), full(ld_row), full(wgate), full(bgate), full(gnorm), st_spec, st_spec,
        ],
        out_specs=[pl.BlockSpec((r, D_MODEL), lambda i: (i, 0)), st_spec, st_spec],
        out_shape=[jax.ShapeDtypeStruct((t, D_MODEL), F32),
                   jax.ShapeDtypeStruct(s0g.shape, F32), jax.ShapeDtypeStruct(s0r.shape, F32)],
        compiler_params=_params("parallel"),
        name="even_sample",
    )(proj, cos, sin, ld_row, wgate, bgate, gnorm, s0g, s0r)


def _s5_prep_kernel(are_ref, aim_ref, lstep_ref, bre_ref, bim_ref, abre_ref, abim_ref, bbre_ref, bbim_ref):
    a_re, a_im = are_ref[...], aim_ref[...]
    step = jnp.exp(lstep_ref[...])
    mag = jnp.exp(a_re * step)
    ab_re = mag * jnp.cos(a_im * step)
    ab_im = mag * jnp.sin(a_im * step)
    den = a_re * a_re + a_im * a_im
    coef_re = ((ab_re - 1.0) * a_re + ab_im * a_im) / den
    coef_im = (ab_im * a_re - (ab_re - 1.0) * a_im) / den
    b_re, b_im = bre_ref[...], bim_ref[...]
    abre_ref[...] = ab_re
    abim_ref[...] = ab_im
    bbre_ref[...] = coef_re * b_re - coef_im * b_im
    bbim_ref[...] = coef_re * b_im + coef_im * b_re


def _s5_prep(a_re, a_im, log_step, b_re, b_im):
    g, n = a_re.shape
    shp3 = jax.ShapeDtypeStruct((g, 1, n), F32)
    shpb = jax.ShapeDtypeStruct((g, S5_GROUP, n), F32)
    return pl.pallas_call(_s5_prep_kernel, out_shape=[shp3, shp3, shpb, shpb], name="s5_prep")(
        a_re.reshape(g, 1, n), a_im.reshape(g, 1, n), log_step.reshape(g, 1, 1),
        jnp.swapaxes(b_re, 1, 2), jnp.swapaxes(b_im, 1, 2))


S5_LANE_CHUNKS = S5_GROUPS * S5_STATE // LANES
S5_HALVES = 2


def _s5_layout(n_seq, lt):
    pack = max(1, SUBLANES // n_seq)
    pitch = lt + 4 if lt % SUBLANES == 0 else lt
    return pack, S5_LANE_CHUNKS // pack, pitch


def _s5_slot(c, n_groups):
    return c % n_groups, c // n_groups


def _s5_load_state(s0re_ref, s0im_ref, hre_ref, him_ref, n_seq, n_groups):
    for c in range(S5_LANE_CHUNKS):
        g, j = _s5_slot(c, n_groups)
        hre_ref[g, j * n_seq:(j + 1) * n_seq, :] = s0re_ref[:, c * LANES:(c + 1) * LANES]
        him_ref[g, j * n_seq:(j + 1) * n_seq, :] = s0im_ref[:, c * LANES:(c + 1) * LANES]


def _s5_store_state(sre_ref, sim_ref, hre_ref, him_ref, n_seq, n_groups):
    for c in range(S5_LANE_CHUNKS):
        g, j = _s5_slot(c, n_groups)
        sre_ref[:, c * LANES:(c + 1) * LANES] = hre_ref[g, j * n_seq:(j + 1) * n_seq, :]
        sim_ref[:, c * LANES:(c + 1) * LANES] = him_ref[g, j * n_seq:(j + 1) * n_seq, :]


def _s5_core(u, abre_ref, abim_ref, bbre_ref, bbim_ref, ccre_ref, ccim_ref, d_ref, wglu_ref, bglu_ref,
             xr_ref, xi_ref, hre_ref, him_ref, *, n_seq, lt, groups_per_pass, unroll):
    pack, n_groups, pitch = _s5_layout(n_seq, lt)
    rows = n_seq * lt
    per_half = S5_LANE_CHUNKS // S5_HALVES
    lanes_of = lambda c: slice(c * LANES, (c + 1) * LANES)
    slot_of = lambda c: _s5_slot(c, n_groups)

    def seq_rows(j, s):
        return slice((j * n_seq + s) * pitch, (j * n_seq + s) * pitch + lt)

    ub = _bf(u)
    u_cols = S5_WIDTH // S5_HALVES
    for half in range(S5_HALVES):
        uh = ub[:, half * u_cols:(half + 1) * u_cols]
        for x_ref, bb_ref in ((xr_ref, bbre_ref), (xi_ref, bbim_ref)):
            x = _dot(uh, bb_ref[half])
            for k in range(per_half):
                g, j = slot_of(half * per_half + k)
                if pitch == lt:
                    x_ref[g, j * rows:(j + 1) * rows, :] = x[:, lanes_of(k)]
                else:
                    for s in range(n_seq):
                        x_ref[g, seq_rows(j, s), :] = x[s * lt:(s + 1) * lt, lanes_of(k)]

    for g0 in range(0, n_groups, groups_per_pass):
        gs = list(range(g0, g0 + groups_per_pass))
        init = tuple(hre_ref[g] for g in gs) + tuple(him_ref[g] for g in gs)

        def step(t, carry, gs=gs):
            rws = pl.ds(t, pack * n_seq, stride=pitch)
            new_re, new_im = [], []
            for k, g in enumerate(gs):
                a_re, a_im = abre_ref[g], abim_ref[g]
                h_re, h_im = carry[k], carry[len(gs) + k]
                n_re = a_re * h_re - a_im * h_im + xr_ref[g, rws, :]
                n_im = a_re * h_im + a_im * h_re + xi_ref[g, rws, :]
                xr_ref[g, rws, :] = n_re
                xi_ref[g, rws, :] = n_im
                new_re.append(n_re)
                new_im.append(n_im)
            return tuple(new_re + new_im)

        fin = lax.fori_loop(0, lt, step, init, unroll=unroll)
        for k, g in enumerate(gs):
            hre_ref[g] = fin[k]
            him_ref[g] = fin[len(gs) + k]

    def gather(x_ref, half):
        cols = []
        for k in range(per_half):
            g, j = slot_of(half * per_half + k)
            if pitch == lt:
                cols.append(x_ref[g, j * rows:(j + 1) * rows, :])
            else:
                cols.append(jnp.concatenate([x_ref[g, seq_rows(j, s), :] for s in range(n_seq)], axis=0))
        return _bf(jnp.concatenate(cols, axis=1))

    y = jnp.concatenate([_dot(gather(xr_ref, half), ccre_ref[half]) - _dot(gather(xi_ref, half), ccim_ref[half])
                         for half in range(S5_HALVES)], axis=1) + d_ref[...] * u
    z = jax.nn.gelu(y, approximate=True)
    return z * jax.nn.sigmoid(_dot(_bf(z), wglu_ref[...]) + bglu_ref[...])


def _s5_kernel(u_ref, s0re_ref, s0im_ref, abre_ref, abim_ref, bbre_ref, bbim_ref, ccre_ref, ccim_ref,
               d_ref, wglu_ref, bglu_ref, out_ref, sre_ref, sim_ref, xr_ref, xi_ref, hre_ref, him_ref,
               *, n_seq, lt, groups_per_pass, unroll):
    n_groups = _s5_layout(n_seq, lt)[1]

    @pl.when(pl.program_id(0) == 0)
    def _():
        _s5_load_state(s0re_ref, s0im_ref, hre_ref, him_ref, n_seq, n_groups)

    out = _s5_core(u_ref[...].reshape(n_seq * lt, S5_WIDTH), abre_ref, abim_ref, bbre_ref, bbim_ref, ccre_ref, ccim_ref,
                   d_ref, wglu_ref, bglu_ref, xr_ref, xi_ref, hre_ref, him_ref,
                   n_seq=n_seq, lt=lt, groups_per_pass=groups_per_pass, unroll=unroll)
    out_ref[...] = out.reshape(out_ref.shape)

    @pl.when(pl.program_id(0) == pl.num_programs(0) - 1)
    def _():
        _s5_store_state(sre_ref, sim_ref, hre_ref, him_ref, n_seq, n_groups)


def _s5_tables(ab, n_seq, lt):
    pack, n_groups, _ = _s5_layout(n_seq, lt)
    tab = jnp.swapaxes(ab.reshape(pack, n_groups, 1, LANES), 0, 1)
    return jnp.broadcast_to(tab, (n_groups, pack, n_seq, LANES)).reshape(n_groups, pack * n_seq, LANES)


def _s5(proj3, s0_re, s0_im, ab_re, ab_im, bb_re, bb_im, cc_re, cc_im, d_row, wglu, bglu, *, n_seq, lt,
        groups_per_pass, unroll):
    nb, rows_b, _ = proj3.shape
    blk_rows = n_seq * lt // nb
    pack, n_groups, pitch = _s5_layout(n_seq, lt)
    ab_re, ab_im = _s5_tables(ab_re, n_seq, lt), _s5_tables(ab_im, n_seq, lt)
    full = lambda a: pl.BlockSpec(a.shape, lambda j: (0,) * a.ndim)
    io_spec = pl.BlockSpec((nb, blk_rows, S5_WIDTH), lambda j: (0, j, 0))
    x_scratch = pltpu.VMEM((n_groups, pack * n_seq * pitch, LANES), F32)
    h_scratch = pltpu.VMEM((n_groups, pack * n_seq, LANES), F32)
    return pl.pallas_call(
        functools.partial(_s5_kernel, n_seq=n_seq, lt=lt, groups_per_pass=groups_per_pass, unroll=unroll),
        grid=(rows_b // blk_rows,),
        in_specs=[io_spec, full(s0_re), full(s0_im), full(ab_re), full(ab_im), full(bb_re), full(bb_im),
                  full(cc_re), full(cc_im), full(d_row), full(wglu), full(bglu)],
        out_specs=[io_spec, full(s0_re), full(s0_im)],
        out_shape=[jax.ShapeDtypeStruct((nb, rows_b, S5_WIDTH), F32),
                   jax.ShapeDtypeStruct(s0_re.shape, F32), jax.ShapeDtypeStruct(s0_im.shape, F32)],
        scratch_shapes=[x_scratch, x_scratch, h_scratch, h_scratch],
        compiler_params=_params("arbitrary"),
        name="s5",
    )(proj3, s0_re, s0_im, ab_re, ab_im, bb_re, bb_im, cc_re, cc_im, d_row, wglu, bglu)


OD_Q_BLOCK = 1
OD_KV_BLOCK = 4
OD_COLS = 1280
ROPE_HALF = SWA_HD // 8
SWA_GROUP = SWA_HEADS // SWA_KV_HEADS


def _half_lanes(shape):
    return (_iota(shape, 1) & (LANES - 1)) < SWA_HD


def _pair_rms_scale(x):
    same_head = (_iota((LANES, LANES), 0) >= SWA_HD) == (_iota((LANES, LANES), 1) >= SWA_HD)
    sums = _dot_exact_rhs(x * x, _bf(same_head.astype(F32)), terms=2)
    return lax.rsqrt(sums * (1.0 / SWA_HD) + NORM_EPS)


def _swa_qk(xq, xk, qg, kg, cq, sq):
    xq_g = xq * qg
    n_pairs = SWA_HEADS // 2
    q_rot = xq_g * _lane_tile(cq, n_pairs) + _rope_partner(xq_g, SWA_HD, ROPE_HALF) * _lane_tile(sq, n_pairs)
    xk_g = xk * kg
    k_rot = xk_g * cq + _rope_partner(xk_g, SWA_HD, ROPE_HALF) * sq
    q_pairs = [q_rot[:, j * LANES:(j + 1) * LANES] * _pair_rms_scale(xq[:, j * LANES:(j + 1) * LANES])
               for j in range(SWA_HEADS // 2)]
    return q_pairs, k_rot * _pair_rms_scale(xk)


def _swa_query_stack(q_pairs, kh):
    lo = _half_lanes(q_pairs[0].shape)
    keep = lo if kh == 0 else jnp.logical_not(lo)
    parts = []
    for g in range(SWA_GROUP):
        hq = kh * SWA_GROUP + g
        pair = q_pairs[hq // 2]
        src = pair if hq % 2 == kh else pltpu.roll(pair, SWA_HD, 1)
        parts.append(jnp.where(keep, src, 0.0))
    return _bf(jnp.concatenate(parts, axis=0))


def _swa_merge_heads(o, kh, rows):
    lo = _half_lanes((rows, LANES))
    pairs = []
    for p in range(SWA_GROUP // 2):
        even, odd = o[2 * p * rows:(2 * p + 1) * rows], o[(2 * p + 1) * rows:(2 * p + 2) * rows]
        if kh == 0:
            pairs.append(jnp.where(lo, even, pltpu.roll(odd, SWA_HD, 1)))
        else:
            pairs.append(jnp.where(lo, pltpu.roll(even, SWA_HD, 1), odd))
    return pairs


def _swa_block(xq, xk, v_cur, k_prev, v_prev, mask, qg, kg, cq, sq, sink_ref, after_head=None):
    w = SWA_WINDOW
    ones_col = jnp.ones((2 * w, LANES), BF16)
    q_pairs, k_cur = _swa_qk(xq, xk, qg, kg, cq, sq)
    k_ext = _bf(jnp.concatenate([k_prev, k_cur], axis=0))
    v_ext = _bf(jnp.concatenate([v_prev, v_cur], axis=0))
    out_pairs = []
    for kh in range(SWA_KV_HEADS):
        s_all = _dot_nt(_swa_query_stack(q_pairs, kh), k_ext)
        weights, sink_terms = [], []
        for g in range(SWA_GROUP):
            s = jnp.where(mask, s_all[g * w:(g + 1) * w] * (SWA_HD ** -0.5), -jnp.inf)
            sink = sink_ref[kh * SWA_GROUP + g:kh * SWA_GROUP + g + 1, :]
            m = jnp.maximum(jnp.broadcast_to(jnp.max(s, axis=-1, keepdims=True), (w, LANES)), sink)
            weights.append(_bf(jnp.exp(s - jnp.concatenate([m, m], axis=1))))
            sink_terms.append(jnp.exp(sink - m))
        weights = jnp.concatenate(weights, axis=0)
        den = _dot(weights, ones_col) + jnp.concatenate(sink_terms, axis=0)
        out_pairs += _swa_merge_heads(_dot(weights, v_ext) / den, kh, w)
        if after_head is not None:
            after_head(kh)
    return out_pairs, k_cur


def _odd_prompt_kernel(x_ref, xnext_ref, gain_ref, win_ref, wout_ref, s0re_ref, s0im_ref, abre_ref, abim_ref, bbre_ref,
                       bbim_ref, ccre_ref, ccim_ref, d_ref, wglu_ref, bglu_ref, cq_ref, sq_ref, qg_ref, kg_ref, sink_ref,
                       o_ref, sre_ref, sim_ref, ck_ref, cv_ref,
                       proj_a, proj_b, mix_sc, xr_ref, xi_ref, hre_ref, him_ref, kprev_sc, vprev_sc,
                       *, n_seq, groups_per_pass, unroll):
    n_groups = _s5_layout(n_seq, SWA_WINDOW)[1]
    step = pl.program_id(0)

    @pl.when(step == 0)
    def _():
        _s5_load_state(s0re_ref, s0im_ref, hre_ref, him_ref, n_seq, n_groups)
        kprev_sc[...] = jnp.zeros_like(kprev_sc)
        vprev_sc[...] = jnp.zeros_like(vprev_sc)
        x0 = x_ref[...].reshape(n_seq * SWA_WINDOW, D_MODEL)
        proj_a[...] = _dot(_bf(_rms(x0, gain_ref[...])), win_ref[...])

    args = (x_ref, xnext_ref, gain_ref, win_ref, wout_ref, abre_ref, abim_ref, bbre_ref, bbim_ref, ccre_ref, ccim_ref,
            d_ref, wglu_ref, bglu_ref, cq_ref, sq_ref, qg_ref, kg_ref, sink_ref, o_ref)
    scratch = (mix_sc, xr_ref, xi_ref, hre_ref, him_ref, kprev_sc, vprev_sc)
    kw = dict(n_seq=n_seq, groups_per_pass=groups_per_pass, unroll=unroll)

    @pl.when(step % 2 == 0)
    def _():
        _odd_prompt_step(*args, proj_a, proj_b, *scratch, **kw)

    @pl.when(step % 2 == 1)
    def _():
        _odd_prompt_step(*args, proj_b, proj_a, *scratch, **kw)

    @pl.when(step == pl.num_programs(0) - 1)
    def _():
        _s5_store_state(sre_ref, sim_ref, hre_ref, him_ref, n_seq, n_groups)
        ck_ref[...] = kprev_sc[...]
        cv_ref[...] = vprev_sc[...]


def _odd_prompt_step(x_ref, xnext_ref, gain_ref, win_ref, wout_ref, abre_ref, abim_ref, bbre_ref, bbim_ref, ccre_ref,
                     ccim_ref, d_ref, wglu_ref, bglu_ref, cq_ref, sq_ref, qg_ref, kg_ref, sink_ref, o_ref,
                     proj_sc, proj_next, mix_sc, xr_ref, xi_ref, hre_ref, him_ref, kprev_sc, vprev_sc,
                     *, n_seq, groups_per_pass, unroll):
    w = SWA_WINDOW
    rows = n_seq * w
    x = x_ref[...].reshape(rows, D_MODEL)
    xn_next = _bf(_rms(xnext_ref[...].reshape(rows, D_MODEL), gain_ref[...]))
    piece_cols = [(a, min(a + 256, OD_COLS)) for a in range(0, OD_COLS, 256)]
    n_slots = n_seq * SWA_KV_HEADS

    def issue_next_projection(slot_index):
        for i, (a, b) in enumerate(piece_cols):
            if (i * n_slots) // len(piece_cols) == slot_index:
                proj_next[:, a:b] = _dot(xn_next, win_ref[:, a:b])

    mix_sc[:, 0:S5_WIDTH] = _s5_core(proj_sc[:, 0:S5_WIDTH], abre_ref, abim_ref, bbre_ref, bbim_ref, ccre_ref, ccim_ref,
                                     d_ref, wglu_ref, bglu_ref, xr_ref, xi_ref, hre_ref, him_ref,
                                     n_seq=n_seq, lt=w, groups_per_pass=groups_per_pass, unroll=unroll)

    t_idx = _iota((w, 2 * w), 0)
    s_idx = _iota((w, 2 * w), 1)
    mask = (s_idx > t_idx) & (s_idx <= t_idx + w) & (s_idx >= jnp.where(pl.program_id(0) == 0, w, 0))
    q0, k0, v0 = OD_Q_BLOCK * 512, OD_KV_BLOCK * 256, OD_KV_BLOCK * 256 + LANES
    for s in range(n_seq):
        rs = slice(s * w, (s + 1) * w)
        v_cur = proj_sc[rs, v0:v0 + LANES]
        pairs, k_cur = _swa_block(proj_sc[rs, q0:q0 + 512], proj_sc[rs, k0:k0 + LANES], v_cur, kprev_sc[s], vprev_sc[s],
                                  mask, qg_ref[...], kg_ref[...], cq_ref[...], sq_ref[...], sink_ref,
                                  after_head=lambda kh, s=s: issue_next_projection(s * SWA_KV_HEADS + kh))
        for i, pair in enumerate(pairs):
            mix_sc[rs, S5_WIDTH + i * LANES:S5_WIDTH + (i + 1) * LANES] = pair
        kprev_sc[s] = k_cur
        vprev_sc[s] = v_cur
    o_ref[...] = (x + _dot(_bf(mix_sc[...]), wout_ref[...])).reshape(o_ref.shape)


def _odd_prompt(x, gain, w_in, w_out, s0_re, s0_im, ab_re, ab_im, bb_re, bb_im, cc_re, cc_im, d_row, wglu, bglu,
                cq, sq, qg, kg, sink_rows, *, groups_per_pass, unroll):
    b, l, d = x.shape
    w = SWA_WINDOW
    pack, n_groups, pitch = _s5_layout(b, w)
    ab_re, ab_im = _s5_tables(ab_re, b, w), _s5_tables(ab_im, b, w)
    full = lambda a: pl.BlockSpec(a.shape, lambda j: (0,) * a.ndim)
    io_spec = pl.BlockSpec((b, w, d), lambda j: (0, j, 0))
    n_steps = l // w
    next_spec = pl.BlockSpec((b, w, d), lambda j: (0, jnp.minimum(j + 1, n_steps - 1), 0))
    tab_spec = pl.BlockSpec((w, LANES), lambda j: (j, 0))
    cache_shape = jax.ShapeDtypeStruct((b, w, LANES), F32)
    x_scratch = pltpu.VMEM((n_groups, pack * b * pitch, LANES), F32)
    h_scratch = pltpu.VMEM((n_groups, pack * b, LANES), F32)
    kv_scratch = pltpu.VMEM((b, w, LANES), F32)
    return pl.pallas_call(
        functools.partial(_odd_prompt_kernel, n_seq=b, groups_per_pass=groups_per_pass, unroll=unroll),
        grid=(l // w,),
        in_specs=[io_spec, next_spec, full(gain), full(w_in), full(w_out), full(s0_re), full(s0_im), full(ab_re), full(ab_im),
                  full(bb_re), full(bb_im), full(cc_re), full(cc_im), full(d_row), full(wglu), full(bglu),
                  tab_spec, tab_spec, full(qg), full(kg), full(sink_rows)],
        out_specs=[io_spec, full(s0_re), full(s0_im), pl.BlockSpec((b, w, LANES), lambda j: (0, 0, 0)),
                   pl.BlockSpec((b, w, LANES), lambda j: (0, 0, 0))],
        out_shape=[jax.ShapeDtypeStruct(x.shape, F32), jax.ShapeDtypeStruct(s0_re.shape, F32),
                   jax.ShapeDtypeStruct(s0_im.shape, F32), cache_shape, cache_shape],
        scratch_shapes=[pltpu.VMEM((b * w, OD_COLS), F32), pltpu.VMEM((b * w, OD_COLS), F32), pltpu.VMEM((b * w, d), F32),
                        x_scratch, x_scratch, h_scratch, h_scratch, kv_scratch, kv_scratch],
        compiler_params=_params("arbitrary"),
        name="odd_prompt",
    )(x, x, gain, w_in, w_out, s0_re, s0_im, ab_re, ab_im, bb_re, bb_im, cc_re, cc_im, d_row, wglu, bglu,
      cq, sq, qg, kg, sink_rows)


def _swa_sample_kernel(q_ref, kv_ref, ck_ref, cv_ref, cq_ref, sq_ref, qg_ref, kg_ref, sink_ref,
                       o_ref, nk_ref, nv_ref, *, bg, ls):
    w = SWA_WINDOW
    r = bg * ls
    ls_shift = ls.bit_length() - 1
    w_shift = w.bit_length() - 1
    rows_g = SWA_GROUP * r
    v_new = kv_ref[:, LANES:2 * LANES]
    q_pairs, k_new = _swa_qk(q_ref[...], kv_ref[:, 0:LANES], qg_ref[...], kg_ref[...], cq_ref[...], sq_ref[...])
    k_cache = _bf(ck_ref[...].reshape(bg * w, LANES))
    v_cache = _bf(cv_ref[...].reshape(bg * w, LANES))

    row = _iota((rows_g, bg * w), 0) & (r - 1)
    col = _iota((rows_g, bg * w), 1)
    mask_c = ((row >> ls_shift) == (col >> w_shift)) & ((col & (w - 1)) > (row & (ls - 1)))
    row_n = _iota((rows_g, r), 0) & (r - 1)
    col_n = _iota((rows_g, r), 1)
    mask_n = ((row_n >> ls_shift) == (col_n >> ls_shift)) & ((col_n & (ls - 1)) <= (row_n & (ls - 1)))
    for kh in range(SWA_KV_HEADS):
        q_stack = _swa_query_stack(q_pairs, kh)
        s_c = jnp.where(mask_c, _dot_nt(q_stack, k_cache) * (SWA_HD ** -0.5), -jnp.inf)
        s_n = jnp.where(mask_n, _dot_nt(q_stack, _bf(k_new)) * (SWA_HD ** -0.5), -jnp.inf)
        sink = jnp.concatenate([jnp.broadcast_to(sink_ref[kh * SWA_GROUP + g:kh * SWA_GROUP + g + 1, :], (r, LANES))
                                for g in range(SWA_GROUP)], axis=0)
        row_max = jnp.maximum(jnp.max(s_c, axis=-1, keepdims=True), jnp.max(s_n, axis=-1, keepdims=True))
        m = jnp.maximum(jnp.broadcast_to(row_max, (rows_g, LANES)), sink)
        e_c = _bf(jnp.exp(s_c - jnp.concatenate([m] * bg, axis=1)))
        e_n = _bf(jnp.exp(s_n - m[:, :r]))
        den = (_dot(e_c, jnp.ones((bg * w, LANES), BF16)) + _dot(e_n, jnp.ones((r, LANES), BF16))
               + jnp.exp(sink - m))
        o = (_dot(e_c, v_cache) + _dot(e_n, _bf(v_new))) / den
        for i, pair in enumerate(_swa_merge_heads(o, kh, r)):
            col_i = kh * (SWA_GROUP // 2) + i
            o_ref[:, col_i * LANES:(col_i + 1) * LANES] = pair

    nk_ref[:, 0:w - ls, :] = ck_ref[:, ls:w, :]
    nv_ref[:, 0:w - ls, :] = cv_ref[:, ls:w, :]
    for b in range(bg):
        nk_ref[b, w - ls:w, :] = k_new[b * ls:(b + 1) * ls, :]
        nv_ref[b, w - ls:w, :] = v_new[b * ls:(b + 1) * ls, :]


def _swa_sample(proj, cache_k, cache_v, cq, sq, qg, kg, sink_rows, *, bg, ls):
    t = proj.shape[0]
    w = SWA_WINDOW
    r = bg * ls
    full = lambda a: pl.BlockSpec(a.shape, lambda i: (0,) * a.ndim)
    cache_spec = pl.BlockSpec((bg, w, LANES), lambda i: (i, 0, 0))
    return pl.pallas_call(
        functools.partial(_swa_sample_kernel, bg=bg, ls=ls),
        grid=(t // r,),
        in_specs=[
            pl.BlockSpec((r, 512), lambda i: (i, OD_Q_BLOCK)),
            pl.BlockSpec((r, 256), lambda i: (i, OD_KV_BLOCK)),
            cache_spec, cache_spec, full(cq), full(sq), full(qg), full(kg), full(sink_rows),
        ],
        out_specs=[pl.BlockSpec((r, 512), lambda i: (i, 0)), cache_spec, cache_spec],
        out_shape=[jax.ShapeDtypeStruct((t, 512), F32),
                   jax.ShapeDtypeStruct(cache_k.shape, F32), jax.ShapeDtypeStruct(cache_v.shape, F32)],
        compiler_params=_params("parallel"),
        name="swa_sample",
    )(proj, proj, cache_k, cache_v, cq, sq, qg, kg, sink_rows)


def _mem_prompt_kernel(x_ref, g_ref, wq_ref, qg_ref, k_ref, v_ref, wo_ref, o_ref):
    x = x_ref[0]
    xn = _bf(_rms(x, g_ref[...]))
    k_all = _bf(_mem_rows(k_ref, 0))
    v_all = _bf(_mem_rows(v_ref, 0))
    acc = x
    for h in range(MEM_HEADS):
        sl = slice(h * MEM_HD, (h + 1) * MEM_HD)
        q = _bf(_rms(_dot(xn, wq_ref[:, sl]), qg_ref[...]))
        s = _dot_nt(q, k_all[:, sl]) * (MEM_HD ** -0.5)
        e = jnp.exp(s - jnp.max(s, axis=-1, keepdims=True))
        p = e / jnp.sum(e, axis=-1, keepdims=True)
        o = _dot(_bf(p), v_all[:, sl])
        acc = acc + _dot(_bf(o), wo_ref[sl, :])
    o_ref[0] = acc


def _mem_prompt(x, gain, wq, q_gain, k, v, wo, *, layer, lt):
    b, l, d = x.shape
    k, v = _mem_flat_view(k), _mem_flat_view(v)
    kv_spec = pl.BlockSpec((None, 1, k.shape[2], LANES), lambda i, j: (layer, i, 0, 0))
    io_spec = pl.BlockSpec((1, lt, d), lambda i, j: (i, j, 0))
    full = lambda a: pl.BlockSpec(a.shape, lambda i, j: (0,) * a.ndim)
    resident = lambda a: pl.BlockSpec(a.shape, lambda i, j: (0,) * a.ndim, pipeline_mode=pl.Buffered(1))
    return pl.pallas_call(
        _mem_prompt_kernel,
        grid=(b, l // lt),
        in_specs=[io_spec, full(gain), resident(wq), full(q_gain), kv_spec, kv_spec, resident(wo)],
        out_specs=io_spec,
        out_shape=jax.ShapeDtypeStruct(x.shape, F32),
        compiler_params=_params("parallel", "arbitrary"),
        name="mem_prompt",
    )(x, gain, wq, q_gain, k, v, wo)


def _memkv_kernel(mem_ref, g_ref, wk_ref, wv_ref, kg_ref, k_out, v_out):
    xn = _bf(_rms(mem_ref[...], g_ref[...]))
    nb, n_mem, heads, hd = k_out.shape
    for h in range(heads):
        sl = slice(h * hd, (h + 1) * hd)
        k_out[:, :, h, :] = _rms(_dot(xn, wk_ref[:, sl]), kg_ref[...]).reshape(nb, n_mem, hd)
        v_out[:, :, h, :] = _dot(xn, wv_ref[:, sl]).reshape(nb, n_mem, hd)


def _memkv(mem, m_gain, wk, wv, k_gain):
    nb, n_mem, d = mem.shape
    depth = wk.shape[0]
    out_shape = jax.ShapeDtypeStruct((depth, nb, n_mem, MEM_HEADS, MEM_HD), F32)
    per_layer = lambda s: pl.BlockSpec((None,) + s, lambda l: (l,) + (0,) * len(s))
    return pl.pallas_call(
        _memkv_kernel,
        grid=(depth,),
        in_specs=[pl.BlockSpec((nb * n_mem, d), lambda l: (0, 0)), per_layer((1, d)), per_layer((d, d)),
                  per_layer((d, d)), per_layer((1, MEM_HD))],
        out_specs=[per_layer((nb, n_mem, MEM_HEADS, MEM_HD)), per_layer((nb, n_mem, MEM_HEADS, MEM_HD))],
        out_shape=[out_shape, out_shape],
        compiler_params=_params("arbitrary"),
        name="memkv",
    )(mem.reshape(nb * n_mem, d), m_gain, wk, wv, k_gain)


def _mem_sample_kernel(q_ref, k_ref, v_ref, o_ref, *, bs, ls):
    r = bs * ls
    per_seq = MEM_HEADS * ls
    n_exp = bs * per_seq
    n_mem = k_ref.shape[1] // MEM_ROW_GROUP
    hd_shift = MEM_HD.bit_length() - 1
    ls_shift = ls.bit_length() - 1
    seq_shift = per_seq.bit_length() - 1
    mem_shift = n_mem.bit_length() - 1
    qb = _bf(q_ref[...])
    e_row = _iota((n_exp, r), 0)
    sel = _bf((((e_row >> seq_shift) << ls_shift) + (e_row & (ls - 1)) == _iota((n_exp, r), 1)).astype(F32))
    head_mask = (((_iota((n_exp, D_MODEL), 0) >> ls_shift) & (MEM_HEADS - 1))
                 == (_iota((n_exp, D_MODEL), 1) >> hd_shift))
    q_exp = _bf(jnp.where(head_mask, _dot(sel, qb), 0.0))
    k_all = _bf(jnp.concatenate([_mem_rows(k_ref, b) for b in range(bs)], axis=0))
    v_all = _bf(jnp.concatenate([_mem_rows(v_ref, b) for b in range(bs)], axis=0))
    own = ((_iota((bs * n_mem, n_exp), 0) >> mem_shift) == (_iota((bs * n_mem, n_exp), 1) >> seq_shift))
    own = own.reshape(bs, n_mem, n_exp)
    s = (_dot_nt(k_all, q_exp) * (MEM_HD ** -0.5)).reshape(bs, n_mem, n_exp)
    s = jnp.where(own, s, -1e30)
    e = jnp.where(own, jnp.exp(s - jnp.max(s, axis=1, keepdims=True)), 0.0)
    den = jnp.sum(e, axis=1, keepdims=True) + jnp.where(jnp.any(own, axis=1, keepdims=True), 0.0, 1.0)
    p = _bf((e / den).reshape(bs * n_mem, n_exp))
    o_all = jnp.where(head_mask, _dot_tn(p, v_all), 0.0)
    o_ref[...] = _dot_tn(sel, _bf(o_all))


MEM_LANE_TILES = MEM_HD // LANES
MEM_ROW_GROUP = MEM_HEADS * MEM_LANE_TILES


def _mem_rows(ref, b):
    n_mem = ref.shape[1] // MEM_ROW_GROUP
    return jnp.concatenate([ref[b, pl.ds(lt * MEM_HEADS + h, n_mem, stride=MEM_ROW_GROUP), :]
                            for h in range(MEM_HEADS) for lt in range(MEM_LANE_TILES)], axis=1)


def _mem_flat_view(a):
    depth, nb, n_mem, heads, hd = a.shape
    a = a.reshape(depth, nb, n_mem, heads, hd // LANES, LANES).transpose(0, 1, 2, 4, 3, 5)
    return a.reshape(depth, nb, n_mem * MEM_ROW_GROUP, LANES)


def _mem_sample(q, k, v, *, layer, bs, ls):
    t, d = q.shape
    k, v = _mem_flat_view(k), _mem_flat_view(v)
    r = bs * ls
    kv_spec = pl.BlockSpec((None, bs, k.shape[2], LANES), lambda i: (layer, i, 0, 0))
    io_spec = pl.BlockSpec((r, d), lambda i: (i, 0))
    return pl.pallas_call(
        functools.partial(_mem_sample_kernel, bs=bs, ls=ls),
        grid=(t // r,),
        in_specs=[io_spec, kv_spec, kv_spec],
        out_specs=io_spec,
        out_shape=jax.ShapeDtypeStruct(q.shape, F32),
        compiler_params=_params("parallel"),
        name="mem_sample",
    )(q, k, v)


def _retention_tables(pos):
    inv = 1.0 / (RET_THETA ** jnp.linspace(0.0, 1.0, RET_DK // 2, dtype=F32))
    ang = pos[:, None] * inv[None, :]
    cos = jnp.repeat(jnp.cos(ang), 2, axis=1)
    sin = jnp.stack([-jnp.sin(ang), jnp.sin(ang)], axis=-1).reshape(pos.shape[0], RET_DK)
    return jnp.tile(cos, (1, PAIR)), jnp.tile(sin, (1, PAIR))


def _rope_tables(pos):
    half = ROPE_HALF
    inv = 1.0 / (ROPE_THETA ** (jnp.arange(half, dtype=F32) * 2.0 / (2 * half)))
    ang = pos[:, None] * inv[None, :]
    n = pos.shape[0]
    rest = SWA_HD - 2 * half
    cos = jnp.concatenate([jnp.cos(ang), jnp.cos(ang), jnp.ones((n, rest), F32)], axis=1)
    sin = jnp.concatenate([-jnp.sin(ang), jnp.sin(ang), jnp.zeros((n, rest), F32)], axis=1)
    return jnp.tile(cos, (1, PAIR)), jnp.tile(sin, (1, PAIR))


def _block_diag(t):
    g, a, b = t.shape
    eye = jnp.eye(g, dtype=t.dtype)
    return (t[:, :, None, :] * eye[:, None, :, None]).reshape(g * a, g * b)


def _half_block_diag(t):
    per = t.shape[0] // S5_HALVES
    return _bf(jnp.stack([_block_diag(t[h * per:(h + 1) * per]) for h in range(S5_HALVES)]))


def _sink_rows(sinks):
    return jnp.broadcast_to(sinks.astype(F32)[:, None], (sinks.shape[0], LANES))


TILES = dict(
    even_prompt_rows=512,
    mem_prompt_rows=1024,
    sample_rows=512,
    proj_cols=640,
    even_sample_seqs=16,
    swa_sample_seqs=8,
    mem_sample_seqs=4,
    s5_groups_per_pass=8,
    s5_unroll=4,
)
def _trunk(x3, pos0, states, mem_k, mem_v, w, *, sample):
    b, l, d = x3.shape
    t = b * l
    x = x3.reshape(t, d)
    pos = pos0 + jnp.arange(l, dtype=F32)
    tm = TILES['sample_rows']
    gla_s, ret_s, s5_re, s5_im, swa_k, swa_v = states
    out_states = {k: [] for k in ("gla", "ret", "s5_re", "s5_im", "swa_k", "swa_v")}
    ld_row = jnp.repeat(jnp.log(1.0 - 2.0 ** (-5.0 - jnp.arange(RET_HEADS, dtype=F32))), RET_DK)[None, :]

    for layer in range(2):
        i = layer // 2
        x = _ffn(x, w['ffn1_norm'][layer][None], w['ffn1_w_gate'], w['ffn1_w_up'], w['ffn1_w_down'],
                 layer=layer)
        if layer % 2 == 0:
            cos, sin = _retention_tables(pos)
            args = (ld_row, w['gla_w_gate'][i], w['gla_b_gate'][i], w['gla_out_norm'][i])
            if sample:
                bg = TILES['even_sample_seqs']
                proj = _norm_matmul(x, w['mix_norm'][layer][None], w['even_w_in'][i], tm=tm, tn=TILES['proj_cols'])
                mixed, g_s, r_s = _even_sample(proj, jnp.tile(cos, (bg, 1)), jnp.tile(sin, (bg, 1)), *args,
                                               gla_s[i], ret_s[i], bg=bg, ls=l)
                x = _matmul_residual(x, [(mixed, w['even_w_out'][i])], tm=tm)
            else:
                x, g_s, r_s = _even_prompt(x.reshape(b, l, d), w['mix_norm'][layer][None], w['even_w_in'][i],
                                           w['even_w_out'][i], cos, sin, *args, gla_s[i], ret_s[i], lt=TILES['even_prompt_rows'])
                x = x.reshape(t, d)
            out_states["gla"].append(g_s)
            out_states["ret"].append(r_s)
        else:
            cq, sq = _rope_tables(pos)
            qg = jnp.tile(w['swa_q_norm'][i], SWA_HEADS)[None, :]
            kg = jnp.tile(w['swa_k_norm'][i], SWA_KV_HEADS)[None, :]
            s5_args = (w['s5_ab_re'][i], w['s5_ab_im'][i], w['s5_bb_re'][i], w['s5_bb_im'][i],
                       w['s5_cc_re'][i], w['s5_cc_im'][i], w['s5_d'][i][None], w['s5_w_glu'][i], w['s5_b_glu'][i][None])
            n_state = S5_GROUPS * S5_STATE
            s0_re, s0_im = s5_re[i].reshape(b, n_state), s5_im[i].reshape(b, n_state)
            sinks = _sink_rows(w['swa_sinks'][i])
            if sample:
                proj = _norm_matmul(x, w['mix_norm'][layer][None], w['odd_w_in'][i], tm=tm, tn=TILES['proj_cols'])
                c_out, sr, si = _s5(proj.reshape(1, t, OD_COLS), s0_re, s0_im, *s5_args, n_seq=b, lt=l,
                                    groups_per_pass=1, unroll=True)
                bg = TILES['swa_sample_seqs']
                d_out, kb, vb = _swa_sample(proj, swa_k[i].reshape(b, SWA_WINDOW, LANES),
                                            swa_v[i].reshape(b, SWA_WINDOW, LANES),
                                            jnp.tile(cq, (bg, 1)), jnp.tile(sq, (bg, 1)), qg, kg, sinks, bg=bg, ls=l)
                w_out = w['odd_w_out'][i]
                x = _matmul_residual(x, [(c_out.reshape(t, S5_WIDTH), w_out[:S5_WIDTH]), (d_out, w_out[S5_WIDTH:])],
                                     tm=tm)
            else:
                x, sr, si, kb, vb = _odd_prompt(x.reshape(b, l, d), w['mix_norm'][layer][None], w['odd_w_in'][i],
                                                w['odd_w_out'][i], s0_re, s0_im, *s5_args, cq, sq, qg, kg, sinks,
                                                groups_per_pass=TILES['s5_groups_per_pass'], unroll=TILES['s5_unroll'])
                x = x.reshape(t, d)
            out_states["s5_re"].append(sr.reshape(b, S5_GROUPS, S5_STATE))
            out_states["s5_im"].append(si.reshape(b, S5_GROUPS, S5_STATE))
            out_states["swa_k"].append(kb.reshape(b, -1, SWA_KV_HEADS, SWA_HD))
            out_states["swa_v"].append(vb.reshape(b, -1, SWA_KV_HEADS, SWA_HD))
        if sample:
            q = _norm_matmul(x, w['mem_x_norm'][layer][None], w['mem_w_q'][layer], tm=tm, tn=MEM_HD,
                             head_gain=w['mem_q_norm'][layer][None], n_norm_tiles=MEM_HEADS)
            o = _mem_sample(q, mem_k, mem_v, layer=layer, bs=TILES['mem_sample_seqs'], ls=l)
            x = _matmul_residual(x, [(o, w['mem_w_o'][layer])], tm=tm)
        else:
            x = _mem_prompt(x.reshape(b, l, d), w['mem_x_norm'][layer][None], w['mem_w_q'][layer],
                            w['mem_q_norm'][layer][None], mem_k, mem_v, w['mem_w_o'][layer],
                            layer=layer, lt=TILES['mem_prompt_rows']).reshape(t, d)
        x = _ffn(x, w['ffn2_norm'][layer][None], w['ffn2_w_gate'], w['ffn2_w_up'], w['ffn2_w_down'],
                 layer=layer)
    return x.reshape(b, l, d), {k: jnp.stack(v) for k, v in out_states.items()}


def kernel(x_prompt, x_sample, mem_prompt, state_gla, state_ret, state_s5_re, state_s5_im, cache_swa_k, cache_swa_v, cache_mem_k, cache_mem_v, ffn1_norm, ffn1_w_gate, ffn1_w_up, ffn1_w_down, ffn2_norm, ffn2_w_gate, ffn2_w_up, ffn2_w_down, mix_norm, even_w_in, gla_w_gate, gla_b_gate, gla_out_norm, even_w_out, odd_w_in, s5_a_re, s5_a_im, s5_log_step, s5_b_re, s5_b_im, s5_c_re, s5_c_im, s5_d, s5_w_glu, s5_b_glu, swa_q_norm, swa_k_norm, swa_sinks, odd_w_out, mem_x_norm, mem_m_norm, mem_w_q, mem_w_k, mem_w_v, mem_w_o, mem_q_norm, mem_k_norm):
    depth = ffn1_norm.shape[0]
    n_even, n_odd = even_w_in.shape[0], odd_w_in.shape[0]
    batch, seq, d = x_prompt.shape
    dec_batch = x_sample.shape[0]
    n_mem = mem_prompt.shape[1]

    ev = even_w_in
    ev_cols = jnp.concatenate(
        [ev[..., 0:1536], ev[..., 1552:3088], ev[..., 1536:1552],
         jnp.zeros(ev.shape[:2] + (EV_COLS - 3088,), ev.dtype)], axis=-1)
    wgate_pad = jnp.concatenate(
        [gla_w_gate, jnp.zeros((n_even, EV_COLS - EV_GA - GLA_RANK, gla_w_gate.shape[-1]), gla_w_gate.dtype)], axis=1)
    w = dict(
        ffn1_norm=ffn1_norm, ffn2_norm=ffn2_norm, mix_norm=mix_norm, mem_x_norm=mem_x_norm,
        ffn1_w_gate=_bf(ffn1_w_gate), ffn1_w_up=_bf(ffn1_w_up), ffn1_w_down=_bf(ffn1_w_down),
        ffn2_w_gate=_bf(ffn2_w_gate), ffn2_w_up=_bf(ffn2_w_up), ffn2_w_down=_bf(ffn2_w_down),
        even_w_in=_bf(ev_cols), gla_w_gate=_bf(wgate_pad), gla_b_gate=gla_b_gate[:, None, :],
        gla_out_norm=gla_out_norm[:, None, :], even_w_out=_bf(even_w_out),
        odd_w_in=_bf(odd_w_in), odd_w_out=_bf(odd_w_out), s5_d=s5_d, s5_w_glu=_bf(s5_w_glu), s5_b_glu=s5_b_glu,
        swa_q_norm=swa_q_norm, swa_k_norm=swa_k_norm, swa_sinks=swa_sinks,
        mem_w_q=_bf(mem_w_q), mem_w_o=_bf(mem_w_o), mem_q_norm=mem_q_norm,
    )
    ab_re, ab_im, bb_re, bb_im = [], [], [], []
    for i in range(n_odd):
        a_r, a_i, b_r, b_i = _s5_prep(s5_a_re[i], s5_a_im[i], s5_log_step[i], s5_b_re[i], s5_b_im[i])
        ab_re.append(a_r.reshape(-1))
        ab_im.append(a_i.reshape(-1))
        bb_re.append(_half_block_diag(b_r))
        bb_im.append(_half_block_diag(b_i))
    w.update(s5_ab_re=ab_re, s5_ab_im=ab_im, s5_bb_re=bb_re, s5_bb_im=bb_im,
             s5_cc_re=[_half_block_diag(jnp.swapaxes(s5_c_re[i], 1, 2)) for i in range(n_odd)],
             s5_cc_im=[_half_block_diag(jnp.swapaxes(s5_c_im[i], 1, 2)) for i in range(n_odd)])

    p_mem_k, p_mem_v = _memkv(mem_prompt, mem_m_norm[:, None, :], _bf(mem_w_k), _bf(mem_w_v), mem_k_norm[:, None, :])

    zeros = lambda *s: jnp.zeros(s, F32)
    p_states = (zeros(n_even, batch, GLA_HEADS, GLA_DK, GLA_DV), zeros(n_even, batch, RET_HEADS, RET_DK, GLA_DV),
                zeros(n_odd, batch, S5_GROUPS, S5_STATE), zeros(n_odd, batch, S5_GROUPS, S5_STATE), None, None)
    y_prompt, ps = _trunk(x_prompt, 0.0, p_states, p_mem_k, p_mem_v, w, sample=False)

    s_states = (state_gla, state_ret, state_s5_re, state_s5_im, cache_swa_k, cache_swa_v)
    y_sample, ss = _trunk(x_sample, float(PAST_LEN), s_states, cache_mem_k, cache_mem_v, w, sample=True)

    return (y_prompt, y_sample, ps["gla"], ps["ret"], ps["s5_re"], ps["s5_im"], ps["swa_k"], ps["swa_v"],
            p_mem_k, p_mem_v, ss["gla"], ss["ret"], ss["s5_re"], ss["s5_im"], ss["swa_k"], ss["swa_v"])
```

```python
import functools
import math

import jax
import jax.numpy as jnp
from jax import lax
from jax.experimental import pallas as pl
from jax.experimental.pallas import tpu as pltpu

F32 = jnp.float32
BF16 = jnp.bfloat16
NORM_EPS = 1e-6

D_MODEL = 1024
GLA_HEADS = 4
GLA_DK = 64
GLA_DV = 128
GLA_RANK = 16
GLA_TAU = 16.0
RET_HEADS = 4
RET_DK = 64
RET_THETA = 10000.0
LA_CHUNK = 64
S5_WIDTH = 512
S5_GROUP = 16
S5_GROUPS = 32
S5_STATE = 64
SWA_HD = 64
SWA_HEADS = 8
SWA_KV_HEADS = 2
SWA_WINDOW = 128
ROPE_THETA = 500000.0
MEM_HEADS = 4
MEM_HD = 256
PAST_LEN = 8192

VMEM_LIMIT_BYTES = 52 * 1024 * 1024
LANES = 128
SUBLANES = 8


def _params(*sem):
    return pltpu.CompilerParams(dimension_semantics=sem, vmem_limit_bytes=VMEM_LIMIT_BYTES)


def _rms(x, gain=None):
    y = x * lax.rsqrt(jnp.mean(x * x, axis=-1, keepdims=True) + NORM_EPS)
    return y if gain is None else y * gain


def _dot(a, b):
    return jnp.dot(a, b, preferred_element_type=F32)


def _dot_nt(a, b):
    return lax.dot_general(a, b, (((1,), (1,)), ((), ())), preferred_element_type=F32)


def _dot_tn(a, b):
    return lax.dot_general(a, b, (((0,), (0,)), ((), ())), preferred_element_type=F32)


def _split_bf16(x, terms):
    pieces = []
    for _ in range(terms):
        piece = _bf(x)
        pieces.append(piece)
        x = x - piece.astype(F32)
    return pieces


def _dot_exact_lhs(a, x, terms=3):
    return sum(_dot(a, piece) for piece in _split_bf16(x, terms))


def _dot_exact_rhs(x, b, terms=3):
    return sum(_dot(piece, b) for piece in _split_bf16(x, terms))


def _dot_nt_exact_lhs(a, x, terms=3):
    return sum(_dot_nt(a, piece) for piece in _split_bf16(x, terms))


def _bf(x):
    return x.astype(BF16)


def _log_sigmoid(x):
    return jnp.minimum(x, 0.0) - jnp.log(1.0 + jnp.exp(-jnp.abs(x)))


def _iota(shape, dim):
    return lax.broadcasted_iota(jnp.int32, shape, dim)


def _lane_tile(x, n):
    return jnp.concatenate([x] * n, axis=1)


def _swap_pairs(x):
    n = x.shape[-1]
    even = (_iota(x.shape, 1) & 1) == 0
    return jnp.where(even, pltpu.roll(x, n - 1, 1), pltpu.roll(x, 1, 1))


def _rope_partner(x, head_dim, half):
    n = x.shape[-1]
    first = (_iota(x.shape, 1) & (head_dim - 1)) < half
    return jnp.where(first, pltpu.roll(x, n - half, 1), pltpu.roll(x, half, 1))


FFN_ROW_TILE = 512


def _ffn_kernel(x_ref, g_ref, wg_ref, wu_ref, wd_ref, o_ref):
    x = x_ref[...]
    xn = _bf(_rms(x, g_ref[...]))
    gate = _dot(xn, wg_ref[...])
    up = _dot(xn, wu_ref[...])
    o_ref[...] = x + _dot(_bf(jax.nn.silu(gate) * up * 0.5), wd_ref[...])


def _ffn(x, gain, wg, wu, wd, *, layer):
    t, d = x.shape
    h = wg.shape[2]
    tm = FFN_ROW_TILE
    resident = dict(pipeline_mode=pl.Buffered(1))
    return pl.pallas_call(
        _ffn_kernel,
        grid=(t // tm,),
        in_specs=[
            pl.BlockSpec((tm, d), lambda i: (i, 0)),
            pl.BlockSpec((1, d), lambda i: (0, 0)),
            pl.BlockSpec((None, d, h), lambda i: (layer, 0, 0), **resident),
            pl.BlockSpec((None, d, h), lambda i: (layer, 0, 0), **resident),
            pl.BlockSpec((None, h, d), lambda i: (layer, 0, 0), **resident),
        ],
        out_specs=pl.BlockSpec((tm, d), lambda i: (i, 0)),
        out_shape=jax.ShapeDtypeStruct((t, d), F32),
        compiler_params=_params("parallel"),
        name="ffn",
    )(x, gain, wg, wu, wd)


def _nmm_kernel(x_ref, g_ref, w_ref, hg_ref, o_ref, xn_ref, *, n_norm_tiles):
    j = pl.program_id(1)

    @pl.when(j == 0)
    def _():
        xn_ref[...] = _bf(_rms(x_ref[...], g_ref[...]))

    y = _dot(xn_ref[...], w_ref[...])
    if n_norm_tiles == 0:
        o_ref[...] = y
    else:
        @pl.when(j < n_norm_tiles)
        def _():
            o_ref[...] = _rms(y, hg_ref[...])

        @pl.when(j >= n_norm_tiles)
        def _():
            o_ref[...] = y


def _norm_matmul(x, gain, w, *, tm, tn, head_gain=None, n_norm_tiles=0):
    t, d = x.shape
    n = w.shape[1]
    if head_gain is None:
        head_gain = jnp.ones((1, tn), F32)
    return pl.pallas_call(
        functools.partial(_nmm_kernel, n_norm_tiles=n_norm_tiles),
        grid=(t // tm, n // tn),
        in_specs=[
            pl.BlockSpec((tm, d), lambda i, j: (i, 0)),
            pl.BlockSpec((1, d), lambda i, j: (0, 0)),
            pl.BlockSpec((d, tn), lambda i, j: (0, j)),
            pl.BlockSpec((1, tn), lambda i, j: (0, 0)),
        ],
        out_specs=pl.BlockSpec((tm, tn), lambda i, j: (i, j)),
        out_shape=jax.ShapeDtypeStruct((t, n), F32),
        scratch_shapes=[pltpu.VMEM((tm, d), BF16)],
        compiler_params=_params("parallel", "arbitrary"),
        name="norm_matmul",
    )(x, gain, w, head_gain)


def _mmr_kernel(*refs, n_terms):
    x_ref = refs[0]
    a_refs = refs[1:1 + n_terms]
    w_refs = refs[1 + n_terms:1 + 2 * n_terms]
    o_ref = refs[1 + 2 * n_terms]
    acc = x_ref[...]
    for a_ref, w_ref in zip(a_refs, w_refs):
        acc = acc + _dot(_bf(a_ref[...]), w_ref[...])
    o_ref[...] = acc


def _matmul_residual(x, terms, *, tm):
    t, d = x.shape
    acts = [a for a, _ in terms]
    ws = [w for _, w in terms]
    in_specs = [pl.BlockSpec((tm, d), lambda i: (i, 0))]
    in_specs += [pl.BlockSpec((tm, a.shape[1]), lambda i: (i, 0)) for a in acts]
    in_specs += [pl.BlockSpec(w.shape, lambda i: (0, 0)) for w in ws]
    return pl.pallas_call(
        functools.partial(_mmr_kernel, n_terms=len(terms)),
        grid=(t // tm,),
        in_specs=in_specs,
        out_specs=pl.BlockSpec((tm, d), lambda i: (i, 0)),
        out_shape=jax.ShapeDtypeStruct((t, d), F32),
        compiler_params=_params("parallel"),
        name="matmul_residual",
    )(x, *acts, *ws)


EV_GQ, EV_GK, EV_GV, EV_GG = 0, 256, 512, 1024
EV_RQ, EV_RK, EV_RV, EV_RG = 1536, 1792, 2048, 2560
EV_GA = 3072
EV_COLS = 3200
EV_BLOCK = 256
PAIR = 2


def _gate_and_norm(o, gate, gain=None):
    return _rms(o, gain) * jax.nn.silu(gate)


def _even_prompt_kernel(x_ref, xnext_ref, gain_ref, win_ref, wout_ref, cos_ref, sin_ref, ld_ref, wgate_ref, bgate_ref,
                        gnorm_ref, s0g_ref, s0r_ref, o_ref, sg_ref, sr_ref, proj_a, proj_b, mix_sc, o_sc,
                        *, chunk, n_chunks, tiles_per_seq):
    n = pl.program_id(0)

    @pl.when(n % tiles_per_seq == 0)
    def _():
        sg_ref[...] = s0g_ref[...]
        sr_ref[...] = s0r_ref[...]

    @pl.when(n == 0)
    def _():
        proj_a[...] = _dot(_bf(_rms(x_ref[...], gain_ref[...])), win_ref[...])

    args = (x_ref, xnext_ref, gain_ref, win_ref, wout_ref, cos_ref, sin_ref, ld_ref, wgate_ref, bgate_ref, gnorm_ref,
            o_ref, sg_ref, sr_ref)

    @pl.when(n % 2 == 0)
    def _():
        _even_prompt_tile(*args, proj_a, proj_b, mix_sc, o_sc, chunk=chunk, n_chunks=n_chunks)

    @pl.when(n % 2 == 1)
    def _():
        _even_prompt_tile(*args, proj_b, proj_a, mix_sc, o_sc, chunk=chunk, n_chunks=n_chunks)


def _even_prompt_tile(x_ref, xnext_ref, gain_ref, win_ref, wout_ref, cos_ref, sin_ref, ld_ref, wgate_ref, bgate_ref,
                      gnorm_ref, o_ref, sg_ref, sr_ref, proj_sc, proj_next, mix_sc, o_sc, *, chunk, n_chunks):
    c = chunk
    lt = c * n_chunks
    blk = min(EV_BLOCK, lt)
    c_shift = c.bit_length() - 1
    x = x_ref[...]
    xn_next = _bf(_rms(xnext_ref[...], gain_ref[...]))
    piece_cols = [(a, min(a + 256, EV_COLS)) for a in range(0, EV_COLS, 256)]
    n_serial = (GLA_HEADS + RET_HEADS) // PAIR * n_chunks

    def issue_next_projection(step_index):
        for i, (a, b) in enumerate(piece_cols):
            if (i * n_serial) // len(piece_cols) == step_index:
                proj_next[:, a:b] = _dot(xn_next, win_ref[:, a:b])

    def cols(a, b):
        return proj_sc[:, a:b]

    log_a = _log_sigmoid(_dot(_bf(cols(EV_GA, EV_COLS)), wgate_ref[...]) + bgate_ref[...]) * (1.0 / GLA_TAU)
    tril = _bf((_iota((c, c), 1) <= _iota((c, c), 0)).astype(F32))
    cum_parts = [_dot_exact_lhs(tril, log_a[i * c:(i + 1) * c]) for i in range(n_chunks)]
    tots = [p[c - 1:c] for p in cum_parts]
    cum = jnp.concatenate(cum_parts, axis=0)
    tot_b = jnp.concatenate([jnp.broadcast_to(t, (c, 256)) for t in tots], axis=0)
    k = cols(EV_GK, EV_GK + 256)
    gla = (cols(EV_GQ, EV_GQ + 256) * (GLA_DK ** -0.5) * jnp.exp(cum), k * jnp.exp(-cum), k * jnp.exp(tot_b - cum))

    ld = ld_ref[...]
    tpos = ((_iota((lt, 1), 0) & (c - 1)) + 1).astype(F32)
    cum_r = tpos * ld
    tot_r = float(c) * ld
    cos, sin = _lane_tile(cos_ref[...], PAIR), _lane_tile(sin_ref[...], PAIR)
    rq = cols(EV_RQ, EV_RQ + 256)
    rk = cols(EV_RK, EV_RK + 256)
    q_rot = rq * cos + _swap_pairs(rq) * sin
    k_rot = (rk * cos + _swap_pairs(rk) * sin) * (RET_DK ** -0.5)
    ret = (q_rot * jnp.exp(cum_r), k_rot * jnp.exp(-cum_r), k_rot * jnp.exp(tot_r - cum_r))

    tot_rows = jnp.concatenate(tots + [tot_r, jnp.zeros((LANES - n_chunks - 1, 256), F32)], axis=0)
    decay_cols = jnp.exp(jnp.transpose(tot_rows))

    row = _iota((blk, blk), 0)
    col = _iota((blk, blk), 1)
    blk_mask = ((row >> c_shift) == (col >> c_shift)) & (col <= row)
    lo = (_iota((lt, LANES), 1) < GLA_DK)
    gnorm = gnorm_ref[...]

    mixers = ((gla, EV_GV, EV_GG, sg_ref, 0, gnorm, lambda i: i),
              (ret, EV_RV, EV_RG, sr_ref, 512, None, lambda i: n_chunks))
    for m, ((q_dec, k_inv, k_dec), v_col, g_col, s_ref, out_col, gain, decay_col_of) in enumerate(mixers):
        for p in range(GLA_HEADS // PAIR):
            lanes = slice(p * LANES, (p + 1) * LANES)
            q_pair = q_dec[:, lanes]
            q_masked = [_bf(jnp.where(lo, q_pair, 0.0)), _bf(jnp.where(lo, 0.0, q_pair))]
            ki = _bf(k_inv[:, lanes])
            kd = _bf(k_dec[:, lanes])
            v_pair = _bf(cols(v_col + p * PAIR * GLA_DV, v_col + (p + 1) * PAIR * GLA_DV))
            for e in range(PAIR):
                slot = (m * (GLA_HEADS // PAIR) + p) * PAIR + e
                for r0 in range(0, lt, blk):
                    rs = slice(r0, r0 + blk)
                    scores = jnp.where(blk_mask, _dot_nt(q_masked[e][rs], ki[rs]), 0.0)
                    o_sc[slot, rs, :] = _dot(_bf(scores), v_pair[rs, e * GLA_DV:(e + 1) * GLA_DV])
            state = s_ref[0, p * PAIR:(p + 1) * PAIR].reshape(PAIR * GLA_DK, GLA_DV)
            for i in range(n_chunks):
                issue_next_projection((m * (GLA_HEADS // PAIR) + p) * n_chunks + i)
                rs = slice(i * c, (i + 1) * c)
                q_stack = jnp.concatenate([q_masked[0][rs], q_masked[1][rs]], axis=0)
                o_inter = _dot(q_stack, _bf(state))
                kv = _dot_tn(kd[rs], v_pair[rs])
                kv = jnp.concatenate([kv[:GLA_DK, :GLA_DV], kv[GLA_DK:, GLA_DV:]], axis=0)
                ci = decay_col_of(i)
                state = state * decay_cols[p * LANES:(p + 1) * LANES, ci:ci + 1] + kv
                for e in range(PAIR):
                    slot = (m * (GLA_HEADS // PAIR) + p) * PAIR + e
                    o_sc[slot, rs, :] += o_inter[e * c:(e + 1) * c]
            s_ref[0, p * PAIR:(p + 1) * PAIR] = state.reshape(PAIR, GLA_DK, GLA_DV)
            for e in range(PAIR):
                h = p * PAIR + e
                slot = (m * (GLA_HEADS // PAIR) + p) * PAIR + e
                gate = cols(g_col + h * GLA_DV, g_col + (h + 1) * GLA_DV)
                mix_sc[:, out_col + h * GLA_DV:out_col + (h + 1) * GLA_DV] = _gate_and_norm(o_sc[slot], gate, gain)
    o_ref[...] = x + _dot(_bf(mix_sc[...]), wout_ref[...])


def _even_prompt(x, gain, w_in, w_out, cos, sin, ld_row, wgate, bgate, gnorm, s0g, s0r, *, lt):
    b, l, d = x.shape
    chunk = math.gcd(l, LA_CHUNK)
    nj = l // lt
    n_tiles = b * nj
    x2 = x.reshape(b * l, d)
    st_spec = pl.BlockSpec((1, GLA_HEADS, GLA_DK, GLA_DV), lambda n: (n // nj, 0, 0, 0))
    io_spec = pl.BlockSpec((lt, d), lambda n: (n, 0))
    next_spec = pl.BlockSpec((lt, d), lambda n: (jnp.minimum(n + 1, n_tiles - 1), 0))
    tab_spec = pl.BlockSpec((lt, LANES), lambda n: (n % nj, 0))
    full = lambda a: pl.BlockSpec(a.shape, lambda n: (0,) * a.ndim)
    out, sg, sr = pl.pallas_call(
        functools.partial(_even_prompt_kernel, chunk=chunk, n_chunks=lt // chunk, tiles_per_seq=nj),
        grid=(n_tiles,),
        in_specs=[
            io_spec, next_spec, full(gain), full(w_in), full(w_out), tab_spec, tab_spec,
            full(ld_row), full(wgate), full(bgate), full(gnorm), st_spec, st_spec,
        ],
        out_specs=[io_spec, st_spec, st_spec],
        out_shape=[jax.ShapeDtypeStruct(x2.shape, F32),
                   jax.ShapeDtypeStruct(s0g.shape, F32), jax.ShapeDtypeStruct(s0r.shape, F32)],
        scratch_shapes=[pltpu.VMEM((lt, EV_COLS), F32), pltpu.VMEM((lt, EV_COLS), F32), pltpu.VMEM((lt, d), F32),
                        pltpu.VMEM((GLA_HEADS + RET_HEADS, lt, GLA_DV), F32)],
        compiler_params=_params("arbitrary"),
        name="even_prompt",
    )(x2, x2, gain, w_in, w_out, cos, sin, ld_row, wgate, bgate, gnorm, s0g, s0r)
    return out.reshape(b, l, d), sg, sr


def _even_sample_kernel(proj_ref, cos_ref, sin_ref, ld_ref, wgate_ref, bgate_ref, gnorm_ref, s0g_ref, s0r_ref,
                        mix_ref, sg_ref, sr_ref, *, bg, ls):
    r = bg * ls
    ls_shift = ls.bit_length() - 1
    dk_shift = GLA_DK.bit_length() - 1
    n_exp = bg * GLA_DK
    row_seq = _iota((r, r), 0) >> ls_shift
    col_seq = _iota((r, r), 1) >> ls_shift
    same = row_seq == col_seq
    seg = same & (_iota((r, r), 1) <= _iota((r, r), 0))
    seg_b = _bf(seg.astype(F32))
    same_b = _bf(same.astype(F32))
    tile_b = _bf(((_iota((GLA_DK, n_exp), 1) & (GLA_DK - 1)) == _iota((GLA_DK, n_exp), 0)).astype(F32))
    tile_t_b = _bf(((_iota((n_exp, GLA_DK), 0) & (GLA_DK - 1)) == _iota((n_exp, GLA_DK), 1)).astype(F32))
    q_mask = (_iota((r, n_exp), 0) >> ls_shift) == (_iota((r, n_exp), 1) >> dk_shift)
    k_mask = (_iota((n_exp, r), 0) >> dk_shift) == (_iota((n_exp, r), 1) >> ls_shift)
    tpos = ((_iota((r, 1), 0) & (ls - 1)) + 1).astype(F32)
    ld = ld_ref[...]
    cum_r = tpos * ld
    tot_r = float(ls) * ld
    r_dec = jnp.exp(tot_r)
    gnorm = gnorm_ref[...]

    def mixer(q_dec, k_inv, k_dec, v_col, g_col, s0_ref, s_ref, out_col, decay_of, gain):
        for h in range(GLA_HEADS):
            sl = slice(h * GLA_DK, (h + 1) * GLA_DK)
            v = _bf(proj_ref[:, v_col + h * GLA_DV:v_col + (h + 1) * GLA_DV])
            qd = _bf(q_dec[:, sl])
            scores = jnp.where(seg, _dot_nt(qd, _bf(k_inv[:, sl])), 0.0)
            state = s0_ref[:, h].reshape(n_exp, GLA_DV)
            q_exp = _bf(jnp.where(q_mask, _dot(qd, tile_b), 0.0))
            o = _dot(_bf(scores), v) + _dot(q_exp, _bf(state))
            k_exp = _bf(jnp.where(k_mask, _dot_nt(tile_t_b, _bf(k_dec[:, sl])), 0.0))
            new_state = state * decay_of(h, sl) + _dot(k_exp, v)
            s_ref[:, h] = new_state.reshape(bg, GLA_DK, GLA_DV)
            gate = proj_ref[:, g_col + h * GLA_DV:g_col + (h + 1) * GLA_DV]
            mix_ref[:, out_col + h * GLA_DV:out_col + (h + 1) * GLA_DV] = _gate_and_norm(o, gate, gain)

    log_a = _log_sigmoid(_dot(_bf(proj_ref[:, EV_GA:EV_COLS]), wgate_ref[...]) + bgate_ref[...]) * (1.0 / GLA_TAU)
    cum = _dot_exact_lhs(seg_b, log_a)
    tot = _dot_exact_lhs(same_b, log_a)
    k = proj_ref[:, EV_GK:EV_GK + 256]

    def gla_decay(h, sl):
        la_exp = jnp.where(k_mask, _dot_nt_exact_lhs(tile_t_b, log_a[:, sl]), 0.0)
        return jnp.exp(jnp.sum(la_exp, axis=-1, keepdims=True))

    mixer(proj_ref[:, EV_GQ:EV_GQ + 256] * (GLA_DK ** -0.5) * jnp.exp(cum), k * jnp.exp(-cum), k * jnp.exp(tot - cum),
          EV_GV, EV_GG, s0g_ref, sg_ref, 0, gla_decay, gnorm)

    cos = _lane_tile(cos_ref[...], PAIR)
    sin = _lane_tile(sin_ref[...], PAIR)
    rq = proj_ref[:, EV_RQ:EV_RQ + 256]
    rk = proj_ref[:, EV_RK:EV_RK + 256]
    q_rot = rq * cos + _swap_pairs(rq) * sin
    k_rot = (rk * cos + _swap_pairs(rk) * sin) * (RET_DK ** -0.5)

    def ret_decay(h, sl):
        return r_dec[:, h * RET_DK:h * RET_DK + 1]

    mixer(q_rot * jnp.exp(cum_r), k_rot * jnp.exp(-cum_r), k_rot * jnp.exp(tot_r - cum_r),
          EV_RV, EV_RG, s0r_ref, sr_ref, 512, ret_decay, None)


def _even_sample(proj, cos, sin, ld_row, wgate, bgate, gnorm, s0g, s0r, *, bg, ls):
    t = proj.shape[0]
    n_b = t // ls
    r = bg * ls
    st_spec = pl.BlockSpec((bg, GLA_HEADS, GLA_DK, GLA_DV), lambda i: (i, 0, 0, 0))
    full = lambda a: pl.BlockSpec(a.shape, lambda i: (0,) * a.ndim)
    return pl.pallas_call(
        functools.partial(_even_sample_kernel, bg=bg, ls=ls),
        grid=(n_b // bg,),
        in_specs=[
            pl.BlockSpec((r, EV_COLS), lambda i: (i, 0)),
            full(cos), full(sin), full(ld_row), full(wgate), full(bgate), full(gnorm), st_spec, st_spec,
        ],
        out_specs=[pl.BlockSpec((r, D_MODEL), lambda i: (i, 0)), st_spec, st_spec],
        out_shape=[jax.ShapeDtypeStruct((t, D_MODEL), F32),
                   jax.ShapeDtypeStruct(s0g.shape, F32), jax.ShapeDtypeStruct(s0r.shape, F32)],
        compiler_params=_params("parallel"),
        name="even_sample",
    )(proj, cos, sin, ld_row, wgate, bgate, gnorm, s0g, s0r)


def _s5_prep_kernel(are_ref, aim_ref, lstep_ref, bre_ref, bim_ref, abre_ref, abim_ref, bbre_ref, bbim_ref):
    a_re, a_im = are_ref[...], aim_ref[...]
    step = jnp.exp(lstep_ref[...])
    mag = jnp.exp(a_re * step)
    ab_re = mag * jnp.cos(a_im * step)
    ab_im = mag * jnp.sin(a_im * step)
    den = a_re * a_re + a_im * a_im
    coef_re = ((ab_re - 1.0) * a_re + ab_im * a_im) / den
    coef_im = (ab_im * a_re - (ab_re - 1.0) * a_im) / den
    b_re, b_im = bre_ref[...], bim_ref[...]
    abre_ref[...] = ab_re
    abim_ref[...] = ab_im
    bbre_ref[...] = coef_re * b_re - coef_im * b_im
    bbim_ref[...] = coef_re * b_im + coef_im * b_re


def _s5_prep(a_re, a_im, log_step, b_re, b_im):
    g, n = a_re.shape
    shp3 = jax.ShapeDtypeStruct((g, 1, n), F32)
    shpb = jax.ShapeDtypeStruct((g, S5_GROUP, n), F32)
    return pl.pallas_call(_s5_prep_kernel, out_shape=[shp3, shp3, shpb, shpb], name="s5_prep")(
        a_re.reshape(g, 1, n), a_im.reshape(g, 1, n), log_step.reshape(g, 1, 1),
        jnp.swapaxes(b_re, 1, 2), jnp.swapaxes(b_im, 1, 2))


S5_LANE_CHUNKS = S5_GROUPS * S5_STATE // LANES
S5_HALVES = 2


def _s5_layout(n_seq, lt):
    pack = max(1, SUBLANES // n_seq)
    pitch = lt + 4 if lt % SUBLANES == 0 else lt
    return pack, S5_LANE_CHUNKS // pack, pitch


def _s5_slot(c, n_groups):
    return c % n_groups, c // n_groups


def _s5_load_state(s0re_ref, s0im_ref, hre_ref, him_ref, n_seq, n_groups):
    for c in range(S5_LANE_CHUNKS):
        g, j = _s5_slot(c, n_groups)
        hre_ref[g, j * n_seq:(j + 1) * n_seq, :] = s0re_ref[:, c * LANES:(c + 1) * LANES]
        him_ref[g, j * n_seq:(j + 1) * n_seq, :] = s0im_ref[:, c * LANES:(c + 1) * LANES]


def _s5_store_state(sre_ref, sim_ref, hre_ref, him_ref, n_seq, n_groups):
    for c in range(S5_LANE_CHUNKS):
        g, j = _s5_slot(c, n_groups)
        sre_ref[:, c * LANES:(c + 1) * LANES] = hre_ref[g, j * n_seq:(j + 1) * n_seq, :]
        sim_ref[:, c * LANES:(c + 1) * LANES] = him_ref[g, j * n_seq:(j + 1) * n_seq, :]


def _s5_core(u, abre_ref, abim_ref, bbre_ref, bbim_ref, ccre_ref, ccim_ref, d_ref, wglu_ref, bglu_ref,
             xr_ref, xi_ref, hre_ref, him_ref, *, n_seq, lt, groups_per_pass, unroll):
    pack, n_groups, pitch = _s5_layout(n_seq, lt)
    rows = n_seq * lt
    per_half = S5_LANE_CHUNKS // S5_HALVES
    lanes_of = lambda c: slice(c * LANES, (c + 1) * LANES)
    slot_of = lambda c: _s5_slot(c, n_groups)

    def seq_rows(j, s):
        return slice((j * n_seq + s) * pitch, (j * n_seq + s) * pitch + lt)

    ub = _bf(u)
    u_cols = S5_WIDTH // S5_HALVES
    for half in range(S5_HALVES):
        uh = ub[:, half * u_cols:(half + 1) * u_cols]
        for x_ref, bb_ref in ((xr_ref, bbre_ref), (xi_ref, bbim_ref)):
            x = _dot(uh, bb_ref[half])
            for k in range(per_half):
                g, j = slot_of(half * per_half + k)
                if pitch == lt:
                    x_ref[g, j * rows:(j + 1) * rows, :] = x[:, lanes_of(k)]
                else:
                    for s in range(n_seq):
                        x_ref[g, seq_rows(j, s), :] = x[s * lt:(s + 1) * lt, lanes_of(k)]

    for g0 in range(0, n_groups, groups_per_pass):
        gs = list(range(g0, g0 + groups_per_pass))
        init = tuple(hre_ref[g] for g in gs) + tuple(him_ref[g] for g in gs)

        def step(t, carry, gs=gs):
            rws = pl.ds(t, pack * n_seq, stride=pitch)
            new_re, new_im = [], []
            for k, g in enumerate(gs):
                a_re, a_im = abre_ref[g], abim_ref[g]
                h_re, h_im = carry[k], carry[len(gs) + k]
                n_re = a_re * h_re - a_im * h_im + xr_ref[g, rws, :]
                n_im = a_re * h_im + a_im * h_re + xi_ref[g, rws, :]
                xr_ref[g, rws, :] = n_re
                xi_ref[g, rws, :] = n_im
                new_re.append(n_re)
                new_im.append(n_im)
            return tuple(new_re + new_im)

        fin = lax.fori_loop(0, lt, step, init, unroll=unroll)
        for k, g in enumerate(gs):
            hre_ref[g] = fin[k]
            him_ref[g] = fin[len(gs) + k]

    def gather(x_ref, half):
        cols = []
        for k in range(per_half):
            g, j = slot_of(half * per_half + k)
            if pitch == lt:
                cols.append(x_ref[g, j * rows:(j + 1) * rows, :])
            else:
                cols.append(jnp.concatenate([x_ref[g, seq_rows(j, s), :] for s in range(n_seq)], axis=0))
        return _bf(jnp.concatenate(cols, axis=1))

    y = jnp.concatenate([_dot(gather(xr_ref, half), ccre_ref[half]) - _dot(gather(xi_ref, half), ccim_ref[half])
                         for half in range(S5_HALVES)], axis=1) + d_ref[...] * u
    z = jax.nn.gelu(y, approximate=True)
    return z * jax.nn.sigmoid(_dot(_bf(z), wglu_ref[...]) + bglu_ref[...])


def _s5_kernel(u_ref, s0re_ref, s0im_ref, abre_ref, abim_ref, bbre_ref, bbim_ref, ccre_ref, ccim_ref,
               d_ref, wglu_ref, bglu_ref, out_ref, sre_ref, sim_ref, xr_ref, xi_ref, hre_ref, him_ref,
               *, n_seq, lt, groups_per_pass, unroll):
    n_groups = _s5_layout(n_seq, lt)[1]

    @pl.when(pl.program_id(0) == 0)
    def _():
        _s5_load_state(s0re_ref, s0im_ref, hre_ref, him_ref, n_seq, n_groups)

    out = _s5_core(u_ref[...].reshape(n_seq * lt, S5_WIDTH), abre_ref, abim_ref, bbre_ref, bbim_ref, ccre_ref, ccim_ref,
                   d_ref, wglu_ref, bglu_ref, xr_ref, xi_ref, hre_ref, him_ref,
                   n_seq=n_seq, lt=lt, groups_per_pass=groups_per_pass, unroll=unroll)
    out_ref[...] = out.reshape(out_ref.shape)

    @pl.when(pl.program_id(0) == pl.num_programs(0) - 1)
    def _():
        _s5_store_state(sre_ref, sim_ref, hre_ref, him_ref, n_seq, n_groups)


def _s5_tables(ab, n_seq, lt):
    pack, n_groups, _ = _s5_layout(n_seq, lt)
    tab = jnp.swapaxes(ab.reshape(pack, n_groups, 1, LANES), 0, 1)
    return jnp.broadcast_to(tab, (n_groups, pack, n_seq, LANES)).reshape(n_groups, pack * n_seq, LANES)


def _s5(proj3, s0_re, s0_im, ab_re, ab_im, bb_re, bb_im, cc_re, cc_im, d_row, wglu, bglu, *, n_seq, lt,
        groups_per_pass, unroll):
    nb, rows_b, _ = proj3.shape
    blk_rows = n_seq * lt // nb
    pack, n_groups, pitch = _s5_layout(n_seq, lt)
    ab_re, ab_im = _s5_tables(ab_re, n_seq, lt), _s5_tables(ab_im, n_seq, lt)
    full = lambda a: pl.BlockSpec(a.shape, lambda j: (0,) * a.ndim)
    io_spec = pl.BlockSpec((nb, blk_rows, S5_WIDTH), lambda j: (0, j, 0))
    x_scratch = pltpu.VMEM((n_groups, pack * n_seq * pitch, LANES), F32)
    h_scratch = pltpu.VMEM((n_groups, pack * n_seq, LANES), F32)
    return pl.pallas_call(
        functools.partial(_s5_kernel, n_seq=n_seq, lt=lt, groups_per_pass=groups_per_pass, unroll=unroll),
        grid=(rows_b // blk_rows,),
        in_specs=[io_spec, full(s0_re), full(s0_im), full(ab_re), full(ab_im), full(bb_re), full(bb_im),
                  full(cc_re), full(cc_im), full(d_row), full(wglu), full(bglu)],
        out_specs=[io_spec, full(s0_re), full(s0_im)],
        out_shape=[jax.ShapeDtypeStruct((nb, rows_b, S5_WIDTH), F32),
                   jax.ShapeDtypeStruct(s0_re.shape, F32), jax.ShapeDtypeStruct(s0_im.shape, F32)],
        scratch_shapes=[x_scratch, x_scratch, h_scratch, h_scratch],
        compiler_params=_params("arbitrary"),
        name="s5",
    )(proj3, s0_re, s0_im, ab_re, ab_im, bb_re, bb_im, cc_re, cc_im, d_row, wglu, bglu)


OD_Q_BLOCK = 1
OD_KV_BLOCK = 4
OD_COLS = 1280
ROPE_HALF = SWA_HD // 8
SWA_GROUP = SWA_HEADS // SWA_KV_HEADS


def _half_lanes(shape):
    return (_iota(shape, 1) & (LANES - 1)) < SWA_HD


def _pair_rms_scale(x):
    same_head = (_iota((LANES, LANES), 0) >= SWA_HD) == (_iota((LANES, LANES), 1) >= SWA_HD)
    sums = _dot_exact_rhs(x * x, _bf(same_head.astype(F32)), terms=2)
    return lax.rsqrt(sums * (1.0 / SWA_HD) + NORM_EPS)


def _swa_qk(xq, xk, qg, kg, cq, sq):
    xq_g = xq * qg
    n_pairs = SWA_HEADS // 2
    q_rot = xq_g * _lane_tile(cq, n_pairs) + _rope_partner(xq_g, SWA_HD, ROPE_HALF) * _lane_tile(sq, n_pairs)
    xk_g = xk * kg
    k_rot = xk_g * cq + _rope_partner(xk_g, SWA_HD, ROPE_HALF) * sq
    q_pairs = [q_rot[:, j * LANES:(j + 1) * LANES] * _pair_rms_scale(xq[:, j * LANES:(j + 1) * LANES])
               for j in range(SWA_HEADS // 2)]
    return q_pairs, k_rot * _pair_rms_scale(xk)


def _swa_query_stack(q_pairs, kh):
    lo = _half_lanes(q_pairs[0].shape)
    keep = lo if kh == 0 else jnp.logical_not(lo)
    parts = []
    for g in range(SWA_GROUP):
        hq = kh * SWA_GROUP + g
        pair = q_pairs[hq // 2]
        src = pair if hq % 2 == kh else pltpu.roll(pair, SWA_HD, 1)
        parts.append(jnp.where(keep, src, 0.0))
    return _bf(jnp.concatenate(parts, axis=0))


def _swa_merge_heads(o, kh, rows):
    lo = _half_lanes((rows, LANES))
    pairs = []
    for p in range(SWA_GROUP // 2):
        even, odd = o[2 * p * rows:(2 * p + 1) * rows], o[(2 * p + 1) * rows:(2 * p + 2) * rows]
        if kh == 0:
            pairs.append(jnp.where(lo, even, pltpu.roll(odd, SWA_HD, 1)))
        else:
            pairs.append(jnp.where(lo, pltpu.roll(even, SWA_HD, 1), odd))
    return pairs


def _swa_block(xq, xk, v_cur, k_prev, v_prev, mask, qg, kg, cq, sq, sink_ref, after_head=None):
    w = SWA_WINDOW
    ones_col = jnp.ones((2 * w, LANES), BF16)
    q_pairs, k_cur = _swa_qk(xq, xk, qg, kg, cq, sq)
    k_ext = _bf(jnp.concatenate([k_prev, k_cur], axis=0))
    v_ext = _bf(jnp.concatenate([v_prev, v_cur], axis=0))
    out_pairs = []
    for kh in range(SWA_KV_HEADS):
        s_all = _dot_nt(_swa_query_stack(q_pairs, kh), k_ext)
        weights, sink_terms = [], []
        for g in range(SWA_GROUP):
            s = jnp.where(mask, s_all[g * w:(g + 1) * w] * (SWA_HD ** -0.5), -jnp.inf)
            sink = sink_ref[kh * SWA_GROUP + g:kh * SWA_GROUP + g + 1, :]
            m = jnp.maximum(jnp.broadcast_to(jnp.max(s, axis=-1, keepdims=True), (w, LANES)), sink)
            weights.append(_bf(jnp.exp(s - jnp.concatenate([m, m], axis=1))))
            sink_terms.append(jnp.exp(sink - m))
        weights = jnp.concatenate(weights, axis=0)
        den = _dot(weights, ones_col) + jnp.concatenate(sink_terms, axis=0)
        out_pairs += _swa_merge_heads(_dot(weights, v_ext) / den, kh, w)
        if after_head is not None:
            after_head(kh)
    return out_pairs, k_cur


def _odd_prompt_kernel(x_ref, xnext_ref, gain_ref, win_ref, wout_ref, s0re_ref, s0im_ref, abre_ref, abim_ref, bbre_ref,
                       bbim_ref, ccre_ref, ccim_ref, d_ref, wglu_ref, bglu_ref, cq_ref, sq_ref, qg_ref, kg_ref, sink_ref,
                       o_ref, sre_ref, sim_ref, ck_ref, cv_ref,
                       proj_a, proj_b, mix_sc, xr_ref, xi_ref, hre_ref, him_ref, kprev_sc, vprev_sc,
                       *, n_seq, groups_per_pass, unroll):
    n_groups = _s5_layout(n_seq, SWA_WINDOW)[1]
    step = pl.program_id(0)

    @pl.when(step == 0)
    def _():
        _s5_load_state(s0re_ref, s0im_ref, hre_ref, him_ref, n_seq, n_groups)
        kprev_sc[...] = jnp.zeros_like(kprev_sc)
        vprev_sc[...] = jnp.zeros_like(vprev_sc)
        x0 = x_ref[...].reshape(n_seq * SWA_WINDOW, D_MODEL)
        proj_a[...] = _dot(_bf(_rms(x0, gain_ref[...])), win_ref[...])

    args = (x_ref, xnext_ref, gain_ref, win_ref, wout_ref, abre_ref, abim_ref, bbre_ref, bbim_ref, ccre_ref, ccim_ref,
            d_ref, wglu_ref, bglu_ref, cq_ref, sq_ref, qg_ref, kg_ref, sink_ref, o_ref)
    scratch = (mix_sc, xr_ref, xi_ref, hre_ref, him_ref, kprev_sc, vprev_sc)
    kw = dict(n_seq=n_seq, groups_per_pass=groups_per_pass, unroll=unroll)

    @pl.when(step % 2 == 0)
    def _():
        _odd_prompt_step(*args, proj_a, proj_b, *scratch, **kw)

    @pl.when(step % 2 == 1)
    def _():
        _odd_prompt_step(*args, proj_b, proj_a, *scratch, **kw)

    @pl.when(step == pl.num_programs(0) - 1)
    def _():
        _s5_store_state(sre_ref, sim_ref, hre_ref, him_ref, n_seq, n_groups)
        ck_ref[...] = kprev_sc[...]
        cv_ref[...] = vprev_sc[...]


def _odd_prompt_step(x_ref, xnext_ref, gain_ref, win_ref, wout_ref, abre_ref, abim_ref, bbre_ref, bbim_ref, ccre_ref,
                     ccim_ref, d_ref, wglu_ref, bglu_ref, cq_ref, sq_ref, qg_ref, kg_ref, sink_ref, o_ref,
                     proj_sc, proj_next, mix_sc, xr_ref, xi_ref, hre_ref, him_ref, kprev_sc, vprev_sc,
                     *, n_seq, groups_per_pass, unroll):
    w = SWA_WINDOW
    rows = n_seq * w
    x = x_ref[...].reshape(rows, D_MODEL)
    xn_next = _bf(_rms(xnext_ref[...].reshape(rows, D_MODEL), gain_ref[...]))
    piece_cols = [(a, min(a + 256, OD_COLS)) for a in range(0, OD_COLS, 256)]
    n_slots = n_seq * SWA_KV_HEADS

    def issue_next_projection(slot_index):
        for i, (a, b) in enumerate(piece_cols):
            if (i * n_slots) // len(piece_cols) == slot_index:
                proj_next[:, a:b] = _dot(xn_next, win_ref[:, a:b])

    mix_sc[:, 0:S5_WIDTH] = _s5_core(proj_sc[:, 0:S5_WIDTH], abre_ref, abim_ref, bbre_ref, bbim_ref, ccre_ref, ccim_ref,
                                     d_ref, wglu_ref, bglu_ref, xr_ref, xi_ref, hre_ref, him_ref,
                                     n_seq=n_seq, lt=w, groups_per_pass=groups_per_pass, unroll=unroll)

    t_idx = _iota((w, 2 * w), 0)
    s_idx = _iota((w, 2 * w), 1)
    mask = (s_idx > t_idx) & (s_idx <= t_idx + w) & (s_idx >= jnp.where(pl.program_id(0) == 0, w, 0))
    q0, k0, v0 = OD_Q_BLOCK * 512, OD_KV_BLOCK * 256, OD_KV_BLOCK * 256 + LANES
    for s in range(n_seq):
        rs = slice(s * w, (s + 1) * w)
        v_cur = proj_sc[rs, v0:v0 + LANES]
        pairs, k_cur = _swa_block(proj_sc[rs, q0:q0 + 512], proj_sc[rs, k0:k0 + LANES], v_cur, kprev_sc[s], vprev_sc[s],
                                  mask, qg_ref[...], kg_ref[...], cq_ref[...], sq_ref[...], sink_ref,
                                  after_head=lambda kh, s=s: issue_next_projection(s * SWA_KV_HEADS + kh))
        for i, pair in enumerate(pairs):
            mix_sc[rs, S5_WIDTH + i * LANES:S5_WIDTH + (i + 1) * LANES] = pair
        kprev_sc[s] = k_cur
        vprev_sc[s] = v_cur
    o_ref[...] = (x + _dot(_bf(mix_sc[...]), wout_ref[...])).reshape(o_ref.shape)


def _odd_prompt(x, gain, w_in, w_out, s0_re, s0_im, ab_re, ab_im, bb_re, bb_im, cc_re, cc_im, d_row, wglu, bglu,
                cq, sq, qg, kg, sink_rows, *, groups_per_pass, unroll):
    b, l, d = x.shape
    w = SWA_WINDOW
    pack, n_groups, pitch = _s5_layout(b, w)
    ab_re, ab_im = _s5_tables(ab_re, b, w), _s5_tables(ab_im, b, w)
    full = lambda a: pl.BlockSpec(a.shape, lambda j: (0,) * a.ndim)
    io_spec = pl.BlockSpec((b, w, d), lambda j: (0, j, 0))
    n_steps = l // w
    next_spec = pl.BlockSpec((b, w, d), lambda j: (0, jnp.minimum(j + 1, n_steps - 1), 0))
    tab_spec = pl.BlockSpec((w, LANES), lambda j: (j, 0))
    cache_shape = jax.ShapeDtypeStruct((b, w, LANES), F32)
    x_scratch = pltpu.VMEM((n_groups, pack * b * pitch, LANES), F32)
    h_scratch = pltpu.VMEM((n_groups, pack * b, LANES), F32)
    kv_scratch = pltpu.VMEM((b, w, LANES), F32)
    return pl.pallas_call(
        functools.partial(_odd_prompt_kernel, n_seq=b, groups_per_pass=groups_per_pass, unroll=unroll),
        grid=(l // w,),
        in_specs=[io_spec, next_spec, full(gain), full(w_in), full(w_out), full(s0_re), full(s0_im), full(ab_re), full(ab_im),
                  full(bb_re), full(bb_im), full(cc_re), full(cc_im), full(d_row), full(wglu), full(bglu),
                  tab_spec, tab_spec, full(qg), full(kg), full(sink_rows)],
        out_specs=[io_spec, full(s0_re), full(s0_im), pl.BlockSpec((b, w, LANES), lambda j: (0, 0, 0)),
                   pl.BlockSpec((b, w, LANES), lambda j: (0, 0, 0))],
        out_shape=[jax.ShapeDtypeStruct(x.shape, F32), jax.ShapeDtypeStruct(s0_re.shape, F32),
                   jax.ShapeDtypeStruct(s0_im.shape, F32), cache_shape, cache_shape],
        scratch_shapes=[pltpu.VMEM((b * w, OD_COLS), F32), pltpu.VMEM((b * w, OD_COLS), F32), pltpu.VMEM((b * w, d), F32),
                        x_scratch, x_scratch, h_scratch, h_scratch, kv_scratch, kv_scratch],
        compiler_params=_params("arbitrary"),
        name="odd_prompt",
    )(x, x, gain, w_in, w_out, s0_re, s0_im, ab_re, ab_im, bb_re, bb_im, cc_re, cc_im, d_row, wglu, bglu,
      cq, sq, qg, kg, sink_rows)


def _swa_sample_kernel(q_ref, kv_ref, ck_ref, cv_ref, cq_ref, sq_ref, qg_ref, kg_ref, sink_ref,
                       o_ref, nk_ref, nv_ref, *, bg, ls):
    w = SWA_WINDOW
    r = bg * ls
    ls_shift = ls.bit_length() - 1
    w_shift = w.bit_length() - 1
    rows_g = SWA_GROUP * r
    v_new = kv_ref[:, LANES:2 * LANES]
    q_pairs, k_new = _swa_qk(q_ref[...], kv_ref[:, 0:LANES], qg_ref[...], kg_ref[...], cq_ref[...], sq_ref[...])
    k_cache = _bf(ck_ref[...].reshape(bg * w, LANES))
    v_cache = _bf(cv_ref[...].reshape(bg * w, LANES))

    row = _iota((rows_g, bg * w), 0) & (r - 1)
    col = _iota((rows_g, bg * w), 1)
    mask_c = ((row >> ls_shift) == (col >> w_shift)) & ((col & (w - 1)) > (row & (ls - 1)))
    row_n = _iota((rows_g, r), 0) & (r - 1)
    col_n = _iota((rows_g, r), 1)
    mask_n = ((row_n >> ls_shift) == (col_n >> ls_shift)) & ((col_n & (ls - 1)) <= (row_n & (ls - 1)))
    for kh in range(SWA_KV_HEADS):
        q_stack = _swa_query_stack(q_pairs, kh)
        s_c = jnp.where(mask_c, _dot_nt(q_stack, k_cache) * (SWA_HD ** -0.5), -jnp.inf)
        s_n = jnp.where(mask_n, _dot_nt(q_stack, _bf(k_new)) * (SWA_HD ** -0.5), -jnp.inf)
        sink = jnp.concatenate([jnp.broadcast_to(sink_ref[kh * SWA_GROUP + g:kh * SWA_GROUP + g + 1, :], (r, LANES))
                                for g in range(SWA_GROUP)], axis=0)
        row_max = jnp.maximum(jnp.max(s_c, axis=-1, keepdims=True), jnp.max(s_n, axis=-1, keepdims=True))
        m = jnp.maximum(jnp.broadcast_to(row_max, (rows_g, LANES)), sink)
        e_c = _bf(jnp.exp(s_c - jnp.concatenate([m] * bg, axis=1)))
        e_n = _bf(jnp.exp(s_n - m[:, :r]))
        den = (_dot(e_c, jnp.ones((bg * w, LANES), BF16)) + _dot(e_n, jnp.ones((r, LANES), BF16))
               + jnp.exp(sink - m))
        o = (_dot(e_c, v_cache) + _dot(e_n, _bf(v_new))) / den
        for i, pair in enumerate(_swa_merge_heads(o, kh, r)):
            col_i = kh * (SWA_GROUP // 2) + i
            o_ref[:, col_i * LANES:(col_i + 1) * LANES] = pair

    nk_ref[:, 0:w - ls, :] = ck_ref[:, ls:w, :]
    nv_ref[:, 0:w - ls, :] = cv_ref[:, ls:w, :]
    for b in range(bg):
        nk_ref[b, w - ls:w, :] = k_new[b * ls:(b + 1) * ls, :]
        nv_ref[b, w - ls:w, :] = v_new[b * ls:(b + 1) * ls, :]


def _swa_sample(proj, cache_k, cache_v, cq, sq, qg, kg, sink_rows, *, bg, ls):
    t = proj.shape[0]
    w = SWA_WINDOW
    r = bg * ls
    full = lambda a: pl.BlockSpec(a.shape, lambda i: (0,) * a.ndim)
    cache_spec = pl.BlockSpec((bg, w, LANES), lambda i: (i, 0, 0))
    return pl.pallas_call(
        functools.partial(_swa_sample_kernel, bg=bg, ls=ls),
        grid=(t // r,),
        in_specs=[
            pl.BlockSpec((r, 512), lambda i: (i, OD_Q_BLOCK)),
            pl.BlockSpec((r, 256), lambda i: (i, OD_KV_BLOCK)),
            cache_spec, cache_spec, full(cq), full(sq), full(qg), full(kg), full(sink_rows),
        ],
        out_specs=[pl.BlockSpec((r, 512), lambda i: (i, 0)), cache_spec, cache_spec],
        out_shape=[jax.ShapeDtypeStruct((t, 512), F32),
                   jax.ShapeDtypeStruct(cache_k.shape, F32), jax.ShapeDtypeStruct(cache_v.shape, F32)],
        compiler_params=_params("parallel"),
        name="swa_sample",
    )(proj, proj, cache_k, cache_v, cq, sq, qg, kg, sink_rows)


def _mem_prompt_kernel(x_ref, g_ref, wq_ref, qg_ref, k_ref, v_ref, wo_ref, o_ref):
    x = x_ref[0]
    xn = _bf(_rms(x, g_ref[...]))
    k_all = _bf(_mem_rows(k_ref, 0))
    v_all = _bf(_mem_rows(v_ref, 0))
    heads = [slice(h * MEM_HD, (h + 1) * MEM_HD) for h in range(MEM_HEADS)]

    def q_head(h):
        return _bf(_rms(_dot(xn, wq_ref[:, heads[h]]), qg_ref[...]))

    def scores(q, h):
        return _dot_nt(q, k_all[:, heads[h]]) * (MEM_HD ** -0.5)

    acc = x
    q_next = q_head(1)
    s_next = scores(q_head(0), 0)
    o_prev = None
    for h in range(MEM_HEADS):
        s = s_next
        if h + 1 < MEM_HEADS:
            s_next = scores(q_next, h + 1)
        if h + 2 < MEM_HEADS:
            q_next = q_head(h + 2)
        e = jnp.exp(s - jnp.max(s, axis=-1, keepdims=True))
        p = e / jnp.sum(e, axis=-1, keepdims=True)
        if o_prev is not None:
            acc = acc + _dot(_bf(o_prev), wo_ref[heads[h - 1], :])
        o_prev = _dot(_bf(p), v_all[:, heads[h]])
    o_ref[0] = acc + _dot(_bf(o_prev), wo_ref[heads[MEM_HEADS - 1], :])


def _mem_prompt(x, gain, wq, q_gain, k, v, wo, *, layer, lt):
    b, l, d = x.shape
    k, v = _mem_flat_view(k), _mem_flat_view(v)
    kv_spec = pl.BlockSpec((None, 1, k.shape[2], LANES), lambda i, j: (layer, i, 0, 0))
    io_spec = pl.BlockSpec((1, lt, d), lambda i, j: (i, j, 0))
    full = lambda a: pl.BlockSpec(a.shape, lambda i, j: (0,) * a.ndim)
    resident = lambda a: pl.BlockSpec(a.shape, lambda i, j: (0,) * a.ndim, pipeline_mode=pl.Buffered(1))
    return pl.pallas_call(
        _mem_prompt_kernel,
        grid=(b, l // lt),
        in_specs=[io_spec, full(gain), resident(wq), full(q_gain), kv_spec, kv_spec, resident(wo)],
        out_specs=io_spec,
        out_shape=jax.ShapeDtypeStruct(x.shape, F32),
        compiler_params=_params("parallel", "arbitrary"),
        name="mem_prompt",
    )(x, gain, wq, q_gain, k, v, wo)


def _memkv_kernel(mem_ref, g_ref, wk_ref, wv_ref, kg_ref, k_out, v_out):
    xn = _bf(_rms(mem_ref[...], g_ref[...]))
    nb, n_mem, heads, hd = k_out.shape
    for h in range(heads):
        sl = slice(h * hd, (h + 1) * hd)
        k_out[:, :, h, :] = _rms(_dot(xn, wk_ref[:, sl]), kg_ref[...]).reshape(nb, n_mem, hd)
        v_out[:, :, h, :] = _dot(xn, wv_ref[:, sl]).reshape(nb, n_mem, hd)


def _memkv(mem, m_gain, wk, wv, k_gain):
    nb, n_mem, d = mem.shape
    depth = wk.shape[0]
    out_shape = jax.ShapeDtypeStruct((depth, nb, n_mem, MEM_HEADS, MEM_HD), F32)
    per_layer = lambda s: pl.BlockSpec((None,) + s, lambda l: (l,) + (0,) * len(s))
    return pl.pallas_call(
        _memkv_kernel,
        grid=(depth,),
        in_specs=[pl.BlockSpec((nb * n_mem, d), lambda l: (0, 0)), per_layer((1, d)), per_layer((d, d)),
                  per_layer((d, d)), per_layer((1, MEM_HD))],
        out_specs=[per_layer((nb, n_mem, MEM_HEADS, MEM_HD)), per_layer((nb, n_mem, MEM_HEADS, MEM_HD))],
        out_shape=[out_shape, out_shape],
        compiler_params=_params("arbitrary"),
        name="memkv",
    )(mem.reshape(nb * n_mem, d), m_gain, wk, wv, k_gain)


def _mem_sample_kernel(q_ref, k_ref, v_ref, o_ref, *, bs, ls):
    r = bs * ls
    per_seq = MEM_HEADS * ls
    n_exp = bs * per_seq
    n_mem = k_ref.shape[1] // MEM_ROW_GROUP
    hd_shift = MEM_HD.bit_length() - 1
    ls_shift = ls.bit_length() - 1
    seq_shift = per_seq.bit_length() - 1
    mem_shift = n_mem.bit_length() - 1
    qb = _bf(q_ref[...])
    e_row = _iota((n_exp, r), 0)
    sel = _bf((((e_row >> seq_shift) << ls_shift) + (e_row & (ls - 1)) == _iota((n_exp, r), 1)).astype(F32))
    head_mask = (((_iota((n_exp, D_MODEL), 0) >> ls_shift) & (MEM_HEADS - 1))
                 == (_iota((n_exp, D_MODEL), 1) >> hd_shift))
    q_exp = _bf(jnp.where(head_mask, _dot(sel, qb), 0.0))
    k_all = _bf(jnp.concatenate([_mem_rows(k_ref, b) for b in range(bs)], axis=0))
    v_all = _bf(jnp.concatenate([_mem_rows(v_ref, b) for b in range(bs)], axis=0))
    own = ((_iota((bs * n_mem, n_exp), 0) >> mem_shift) == (_iota((bs * n_mem, n_exp), 1) >> seq_shift))
    own = own.reshape(bs, n_mem, n_exp)
    s = (_dot_nt(k_all, q_exp) * (MEM_HD ** -0.5)).reshape(bs, n_mem, n_exp)
    s = jnp.where(own, s, -1e30)
    e = jnp.where(own, jnp.exp(s - jnp.max(s, axis=1, keepdims=True)), 0.0)
    den = jnp.sum(e, axis=1, keepdims=True) + jnp.where(jnp.any(own, axis=1, keepdims=True), 0.0, 1.0)
    p = _bf((e / den).reshape(bs * n_mem, n_exp))
    o_all = jnp.where(head_mask, _dot_tn(p, v_all), 0.0)
    o_ref[...] = _dot_tn(sel, _bf(o_all))


MEM_LANE_TILES = MEM_HD // LANES
MEM_ROW_GROUP = MEM_HEADS * MEM_LANE_TILES


def _mem_rows(ref, b):
    n_mem = ref.shape[1] // MEM_ROW_GROUP
    return jnp.concatenate([ref[b, pl.ds(lt * MEM_HEADS + h, n_mem, stride=MEM_ROW_GROUP), :]
                            for h in range(MEM_HEADS) for lt in range(MEM_LANE_TILES)], axis=1)


def _mem_flat_view(a):
    depth, nb, n_mem, heads, hd = a.shape
    a = a.reshape(depth, nb, n_mem, heads, hd // LANES, LANES).transpose(0, 1, 2, 4, 3, 5)
    return a.reshape(depth, nb, n_mem * MEM_ROW_GROUP, LANES)


def _mem_sample(q, k, v, *, layer, bs, ls):
    t, d = q.shape
    k, v = _mem_flat_view(k), _mem_flat_view(v)
    r = bs * ls
    kv_spec = pl.BlockSpec((None, bs, k.shape[2], LANES), lambda i: (layer, i, 0, 0))
    io_spec = pl.BlockSpec((r, d), lambda i: (i, 0))
    return pl.pallas_call(
        functools.partial(_mem_sample_kernel, bs=bs, ls=ls),
        grid=(t // r,),
        in_specs=[io_spec, kv_spec, kv_spec],
        out_specs=io_spec,
        out_shape=jax.ShapeDtypeStruct(q.shape, F32),
        compiler_params=_params("parallel"),
        name="mem_sample",
    )(q, k, v)


def _retention_tables(pos):
    inv = 1.0 / (RET_THETA ** jnp.linspace(0.0, 1.0, RET_DK // 2, dtype=F32))
    ang = pos[:, None] * inv[None, :]
    cos = jnp.repeat(jnp.cos(ang), 2, axis=1)
    sin = jnp.stack([-jnp.sin(ang), jnp.sin(ang)], axis=-1).reshape(pos.shape[0], RET_DK)
    return jnp.tile(cos, (1, PAIR)), jnp.tile(sin, (1, PAIR))


def _rope_tables(pos):
    half = ROPE_HALF
    inv = 1.0 / (ROPE_THETA ** (jnp.arange(half, dtype=F32) * 2.0 / (2 * half)))
    ang = pos[:, None] * inv[None, :]
    n = pos.shape[0]
    rest = SWA_HD - 2 * half
    cos = jnp.concatenate([jnp.cos(ang), jnp.cos(ang), jnp.ones((n, rest), F32)], axis=1)
    sin = jnp.concatenate([-jnp.sin(ang), jnp.sin(ang), jnp.zeros((n, rest), F32)], axis=1)
    return jnp.tile(cos, (1, PAIR)), jnp.tile(sin, (1, PAIR))


def _block_diag(t):
    g, a, b = t.shape
    eye = jnp.eye(g, dtype=t.dtype)
    return (t[:, :, None, :] * eye[:, None, :, None]).reshape(g * a, g * b)


def _half_block_diag(t):
    per = t.shape[0] // S5_HALVES
    return _bf(jnp.stack([_block_diag(t[h * per:(h + 1) * per]) for h in range(S5_HALVES)]))


def _sink_rows(sinks):
    return jnp.broadcast_to(sinks.astype(F32)[:, None], (sinks.shape[0], LANES))


TILES = dict(
    even_prompt_rows=512,
    mem_prompt_rows=1024,
    sample_rows=512,
    proj_cols=640,
    even_sample_seqs=16,
    swa_sample_seqs=8,
    mem_sample_seqs=4,
    s5_groups_per_pass=8,
    s5_unroll=4,
)
def _trunk(x3, pos0, states, mem_k, mem_v, w, *, sample):
    b, l, d = x3.shape
    t = b * l
    x = x3.reshape(t, d)
    pos = pos0 + jnp.arange(l, dtype=F32)
    tm = TILES['sample_rows']
    gla_s, ret_s, s5_re, s5_im, swa_k, swa_v = states
    out_states = {k: [] for k in ("gla", "ret", "s5_re", "s5_im", "swa_k", "swa_v")}
    ld_row = jnp.repeat(jnp.log(1.0 - 2.0 ** (-5.0 - jnp.arange(RET_HEADS, dtype=F32))), RET_DK)[None, :]

    for layer in range(2):
        i = layer // 2
        x = _ffn(x, w['ffn1_norm'][layer][None], w['ffn1_w_gate'], w['ffn1_w_up'], w['ffn1_w_down'],
                 layer=layer)
        if layer % 2 == 0:
            cos, sin = _retention_tables(pos)
            args = (ld_row, w['gla_w_gate'][i], w['gla_b_gate'][i], w['gla_out_norm'][i])
            if sample:
                bg = TILES['even_sample_seqs']
                proj = _norm_matmul(x, w['mix_norm'][layer][None], w['even_w_in'][i], tm=tm, tn=TILES['proj_cols'])
                mixed, g_s, r_s = _even_sample(proj, jnp.tile(cos, (bg, 1)), jnp.tile(sin, (bg, 1)), *args,
                                               gla_s[i], ret_s[i], bg=bg, ls=l)
                x = _matmul_residual(x, [(mixed, w['even_w_out'][i])], tm=tm)
            else:
                x, g_s, r_s = _even_prompt(x.reshape(b, l, d), w['mix_norm'][layer][None], w['even_w_in'][i],
                                           w['even_w_out'][i], cos, sin, *args, gla_s[i], ret_s[i], lt=TILES['even_prompt_rows'])
                x = x.reshape(t, d)
            out_states["gla"].append(g_s)
            out_states["ret"].append(r_s)
        else:
            cq, sq = _rope_tables(pos)
            qg = jnp.tile(w['swa_q_norm'][i], SWA_HEADS)[None, :]
            kg = jnp.tile(w['swa_k_norm'][i], SWA_KV_HEADS)[None, :]
            s5_args = (w['s5_ab_re'][i], w['s5_ab_im'][i], w['s5_bb_re'][i], w['s5_bb_im'][i],
                       w['s5_cc_re'][i], w['s5_cc_im'][i], w['s5_d'][i][None], w['s5_w_glu'][i], w['s5_b_glu'][i][None])
            n_state = S5_GROUPS * S5_STATE
            s0_re, s0_im = s5_re[i].reshape(b, n_state), s5_im[i].reshape(b, n_state)
            sinks = _sink_rows(w['swa_sinks'][i])
            if sample:
                proj = _norm_matmul(x, w['mix_norm'][layer][None], w['odd_w_in'][i], tm=tm, tn=TILES['proj_cols'])
                c_out, sr, si = _s5(proj.reshape(1, t, OD_COLS), s0_re, s0_im, *s5_args, n_seq=b, lt=l,
                                    groups_per_pass=1, unroll=True)
                bg = TILES['swa_sample_seqs']
                d_out, kb, vb = _swa_sample(proj, swa_k[i].reshape(b, SWA_WINDOW, LANES),
                                            swa_v[i].reshape(b, SWA_WINDOW, LANES),
                                            jnp.tile(cq, (bg, 1)), jnp.tile(sq, (bg, 1)), qg, kg, sinks, bg=bg, ls=l)
                w_out = w['odd_w_out'][i]
                x = _matmul_residual(x, [(c_out.reshape(t, S5_WIDTH), w_out[:S5_WIDTH]), (d_out, w_out[S5_WIDTH:])],
                                     tm=tm)
            else:
                x, sr, si, kb, vb = _odd_prompt(x.reshape(b, l, d), w['mix_norm'][layer][None], w['odd_w_in'][i],
                                                w['odd_w_out'][i], s0_re, s0_im, *s5_args, cq, sq, qg, kg, sinks,
                                                groups_per_pass=TILES['s5_groups_per_pass'], unroll=TILES['s5_unroll'])
                x = x.reshape(t, d)
            out_states["s5_re"].append(sr.reshape(b, S5_GROUPS, S5_STATE))
            out_states["s5_im"].append(si.reshape(b, S5_GROUPS, S5_STATE))
            out_states["swa_k"].append(kb.reshape(b, -1, SWA_KV_HEADS, SWA_HD))
            out_states["swa_v"].append(vb.reshape(b, -1, SWA_KV_HEADS, SWA_HD))
        if sample:
            q = _norm_matmul(x, w['mem_x_norm'][layer][None], w['mem_w_q'][layer], tm=tm, tn=MEM_HD,
                             head_gain=w['mem_q_norm'][layer][None], n_norm_tiles=MEM_HEADS)
            o = _mem_sample(q, mem_k, mem_v, layer=layer, bs=TILES['mem_sample_seqs'], ls=l)
            x = _matmul_residual(x, [(o, w['mem_w_o'][layer])], tm=tm)
        else:
            x = _mem_prompt(x.reshape(b, l, d), w['mem_x_norm'][layer][None], w['mem_w_q'][layer],
                            w['mem_q_norm'][layer][None], mem_k, mem_v, w['mem_w_o'][layer],
                            layer=layer, lt=TILES['mem_prompt_rows']).reshape(t, d)
        x = _ffn(x, w['ffn2_norm'][layer][None], w['ffn2_w_gate'], w['ffn2_w_up'], w['ffn2_w_down'],
                 layer=layer)
    return x.reshape(b, l, d), {k: jnp.stack(v) for k, v in out_states.items()}


def kernel(x_prompt, x_sample, mem_prompt, state_gla, state_ret, state_s5_re, state_s5_im, cache_swa_k, cache_swa_v, cache_mem_k, cache_mem_v, ffn1_norm, ffn1_w_gate, ffn1_w_up, ffn1_w_down, ffn2_norm, ffn2_w_gate, ffn2_w_up, ffn2_w_down, mix_norm, even_w_in, gla_w_gate, gla_b_gate, gla_out_norm, even_w_out, odd_w_in, s5_a_re, s5_a_im, s5_log_step, s5_b_re, s5_b_im, s5_c_re, s5_c_im, s5_d, s5_w_glu, s5_b_glu, swa_q_norm, swa_k_norm, swa_sinks, odd_w_out, mem_x_norm, mem_m_norm, mem_w_q, mem_w_k, mem_w_v, mem_w_o, mem_q_norm, mem_k_norm):
    depth = ffn1_norm.shape[0]
    n_even, n_odd = even_w_in.shape[0], odd_w_in.shape[0]
    batch, seq, d = x_prompt.shape
    dec_batch = x_sample.shape[0]
    n_mem = mem_prompt.shape[1]

    ev = even_w_in
    ev_cols = jnp.concatenate(
        [ev[..., 0:1536], ev[..., 1552:3088], ev[..., 1536:1552],
         jnp.zeros(ev.shape[:2] + (EV_COLS - 3088,), ev.dtype)], axis=-1)
    wgate_pad = jnp.concatenate(
        [gla_w_gate, jnp.zeros((n_even, EV_COLS - EV_GA - GLA_RANK, gla_w_gate.shape[-1]), gla_w_gate.dtype)], axis=1)
    w = dict(
        ffn1_norm=ffn1_norm, ffn2_norm=ffn2_norm, mix_norm=mix_norm, mem_x_norm=mem_x_norm,
        ffn1_w_gate=_bf(ffn1_w_gate), ffn1_w_up=_bf(ffn1_w_up), ffn1_w_down=_bf(ffn1_w_down),
        ffn2_w_gate=_bf(ffn2_w_gate), ffn2_w_up=_bf(ffn2_w_up), ffn2_w_down=_bf(ffn2_w_down),
        even_w_in=_bf(ev_cols), gla_w_gate=_bf(wgate_pad), gla_b_gate=gla_b_gate[:, None, :],
        gla_out_norm=gla_out_norm[:, None, :], even_w_out=_bf(even_w_out),
        odd_w_in=_bf(odd_w_in), odd_w_out=_bf(odd_w_out), s5_d=s5_d, s5_w_glu=_bf(s5_w_glu), s5_b_glu=s5_b_glu,
        swa_q_norm=swa_q_norm, swa_k_norm=swa_k_norm, swa_sinks=swa_sinks,
        mem_w_q=_bf(mem_w_q), mem_w_o=_bf(mem_w_o), mem_q_norm=mem_q_norm,
    )
    ab_re, ab_im, bb_re, bb_im = [], [], [], []
    for i in range(n_odd):
        a_r, a_i, b_r, b_i = _s5_prep(s5_a_re[i], s5_a_im[i], s5_log_step[i], s5_b_re[i], s5_b_im[i])
        ab_re.append(a_r.reshape(-1))
        ab_im.append(a_i.reshape(-1))
        bb_re.append(_half_block_diag(b_r))
        bb_im.append(_half_block_diag(b_i))
    w.update(s5_ab_re=ab_re, s5_ab_im=ab_im, s5_bb_re=bb_re, s5_bb_im=bb_im,
             s5_cc_re=[_half_block_diag(jnp.swapaxes(s5_c_re[i], 1, 2)) for i in range(n_odd)],
             s5_cc_im=[_half_block_diag(jnp.swapaxes(s5_c_im[i], 1, 2)) for i in range(n_odd)])

    p_mem_k, p_mem_v = _memkv(mem_prompt, mem_m_norm[:, None, :], _bf(mem_w_k), _bf(mem_w_v), mem_k_norm[:, None, :])

    zeros = lambda *s: jnp.zeros(s, F32)
    p_states = (zeros(n_even, batch, GLA_HEADS, GLA_DK, GLA_DV), zeros(n_even, batch, RET_HEADS, RET_DK, GLA_DV),
                zeros(n_odd, batch, S5_GROUPS, S5_STATE), zeros(n_odd, batch, S5_GROUPS, S5_STATE), None, None)
    y_prompt, ps = _trunk(x_prompt, 0.0, p_states, p_mem_k, p_mem_v, w, sample=False)

    s_states = (state_gla, state_ret, state_s5_re, state_s5_im, cache_swa_k, cache_swa_v)
    y_sample, ss = _trunk(x_sample, float(PAST_LEN), s_states, cache_mem_k, cache_mem_v, w, sample=True)

    return (y_prompt, y_sample, ps["gla"], ps["ret"], ps["s5_re"], ps["s5_im"], ps["swa_k"], ps["swa_v"],
            p_mem_k, p_mem_v, ss["gla"], ss["ret"], ss["s5_re"], ss["s5_im"], ss["swa_k"], ss["swa_v"])
```

```python
import functools
import math

import jax
import jax.numpy as jnp
from jax import lax
from jax.experimental import pallas as pl
from jax.experimental.pallas import tpu as pltpu

F32 = jnp.float32
BF16 = jnp.bfloat16
NORM_EPS = 1e-6

D_MODEL = 1024
GLA_HEADS = 4
GLA_DK = 64
GLA_DV = 128
GLA_RANK = 16
GLA_TAU = 16.0
RET_HEADS = 4
RET_DK = 64
RET_THETA = 10000.0
LA_CHUNK = 64
S5_WIDTH = 512
S5_GROUP = 16
S5_GROUPS = 32
S5_STATE = 64
SWA_HD = 64
SWA_HEADS = 8
SWA_KV_HEADS = 2
SWA_WINDOW = 128
ROPE_THETA = 500000.0
MEM_HEADS = 4
MEM_HD = 256
PAST_LEN = 8192

VMEM_LIMIT_BYTES = 52 * 1024 * 1024
LANES = 128
SUBLANES = 8


def _params(*sem):
    return pltpu.CompilerParams(dimension_semantics=sem, vmem_limit_bytes=VMEM_LIMIT_BYTES)


def _rms(x, gain=None):
    y = x * lax.rsqrt(jnp.mean(x * x, axis=-1, keepdims=True) + NORM_EPS)
    return y if gain is None else y * gain


def _dot(a, b):
    return jnp.dot(a, b, preferred_element_type=F32)


def _dot_nt(a, b):
    return lax.dot_general(a, b, (((1,), (1,)), ((), ())), preferred_element_type=F32)


def _dot_tn(a, b):
    return lax.dot_general(a, b, (((0,), (0,)), ((), ())), preferred_element_type=F32)


def _split_bf16(x, terms):
    pieces = []
    for _ in range(terms):
        piece = _bf(x)
        pieces.append(piece)
        x = x - piece.astype(F32)
    return pieces


def _dot_exact_lhs(a, x, terms=3):
    return sum(_dot(a, piece) for piece in _split_bf16(x, terms))


def _dot_exact_rhs(x, b, terms=3):
    return sum(_dot(piece, b) for piece in _split_bf16(x, terms))


def _dot_nt_exact_lhs(a, x, terms=3):
    return sum(_dot_nt(a, piece) for piece in _split_bf16(x, terms))


def _bf(x):
    return x.astype(BF16)


def _log_sigmoid(x):
    return jnp.minimum(x, 0.0) - jnp.log(1.0 + jnp.exp(-jnp.abs(x)))


def _iota(shape, dim):
    return lax.broadcasted_iota(jnp.int32, shape, dim)


def _lane_tile(x, n):
    return jnp.concatenate([x] * n, axis=1)


def _swap_pairs(x):
    n = x.shape[-1]
    even = (_iota(x.shape, 1) & 1) == 0
    return jnp.where(even, pltpu.roll(x, n - 1, 1), pltpu.roll(x, 1, 1))


def _rope_partner(x, head_dim, half):
    n = x.shape[-1]
    first = (_iota(x.shape, 1) & (head_dim - 1)) < half
    return jnp.where(first, pltpu.roll(x, n - half, 1), pltpu.roll(x, half, 1))


FFN_ROW_TILE = 512


def _ffn_kernel(x_ref, g_ref, wg_ref, wu_ref, wd_ref, o_ref):
    x = x_ref[...]
    xn = _bf(_rms(x, g_ref[...]))
    gate = _dot(xn, wg_ref[...])
    up = _dot(xn, wu_ref[...])
    o_ref[...] = x + _dot(_bf(jax.nn.silu(gate) * up * 0.5), wd_ref[...])


def _ffn(x, gain, wg, wu, wd, *, layer):
    t, d = x.shape
    h = wg.shape[2]
    tm = FFN_ROW_TILE
    resident = dict(pipeline_mode=pl.Buffered(1))
    return pl.pallas_call(
        _ffn_kernel,
        grid=(t // tm,),
        in_specs=[
            pl.BlockSpec((tm, d), lambda i: (i, 0)),
            pl.BlockSpec((1, d), lambda i: (0, 0)),
            pl.BlockSpec((None, d, h), lambda i: (layer, 0, 0), **resident),
            pl.BlockSpec((None, d, h), lambda i: (layer, 0, 0), **resident),
            pl.BlockSpec((None, h, d), lambda i: (layer, 0, 0), **resident),
        ],
        out_specs=pl.BlockSpec((tm, d), lambda i: (i, 0)),
        out_shape=jax.ShapeDtypeStruct((t, d), F32),
        compiler_params=_params("parallel"),
        name="ffn",
    )(x, gain, wg, wu, wd)


def _nmm_kernel(x_ref, g_ref, w_ref, hg_ref, o_ref, xn_ref, *, n_norm_tiles):
    j = pl.program_id(1)

    @pl.when(j == 0)
    def _():
        xn_ref[...] = _bf(_rms(x_ref[...], g_ref[...]))

    y = _dot(xn_ref[...], w_ref[...])
    if n_norm_tiles == 0:
        o_ref[...] = y
    else:
        @pl.when(j < n_norm_tiles)
        def _():
            o_ref[...] = _rms(y, hg_ref[...])

        @pl.when(j >= n_norm_tiles)
        def _():
            o_ref[...] = y


def _norm_matmul(x, gain, w, *, tm, tn, head_gain=None, n_norm_tiles=0):
    t, d = x.shape
    n = w.shape[1]
    if head_gain is None:
        head_gain = jnp.ones((1, tn), F32)
    return pl.pallas_call(
        functools.partial(_nmm_kernel, n_norm_tiles=n_norm_tiles),
        grid=(t // tm, n // tn),
        in_specs=[
            pl.BlockSpec((tm, d), lambda i, j: (i, 0)),
            pl.BlockSpec((1, d), lambda i, j: (0, 0)),
            pl.BlockSpec((d, tn), lambda i, j: (0, j)),
            pl.BlockSpec((1, tn), lambda i, j: (0, 0)),
        ],
        out_specs=pl.BlockSpec((tm, tn), lambda i, j: (i, j)),
        out_shape=jax.ShapeDtypeStruct((t, n), F32),
        scratch_shapes=[pltpu.VMEM((tm, d), BF16)],
        compiler_params=_params("parallel", "arbitrary"),
        name="norm_matmul",
    )(x, gain, w, head_gain)


def _mmr_kernel(*refs, n_terms):
    x_ref = refs[0]
    a_refs = refs[1:1 + n_terms]
    w_refs = refs[1 + n_terms:1 + 2 * n_terms]
    o_ref = refs[1 + 2 * n_terms]
    acc = x_ref[...]
    for a_ref, w_ref in zip(a_refs, w_refs):
        acc = acc + _dot(_bf(a_ref[...]), w_ref[...])
    o_ref[...] = acc


def _matmul_residual(x, terms, *, tm):
    t, d = x.shape
    acts = [a for a, _ in terms]
    ws = [w for _, w in terms]
    in_specs = [pl.BlockSpec((tm, d), lambda i: (i, 0))]
    in_specs += [pl.BlockSpec((tm, a.shape[1]), lambda i: (i, 0)) for a in acts]
    in_specs += [pl.BlockSpec(w.shape, lambda i: (0, 0)) for w in ws]
    return pl.pallas_call(
        functools.partial(_mmr_kernel, n_terms=len(terms)),
        grid=(t // tm,),
        in_specs=in_specs,
        out_specs=pl.BlockSpec((tm, d), lambda i: (i, 0)),
        out_shape=jax.ShapeDtypeStruct((t, d), F32),
        compiler_params=_params("parallel"),
        name="matmul_residual",
    )(x, *acts, *ws)


EV_GQ, EV_GK, EV_GV, EV_GG = 0, 256, 512, 1024
EV_RQ, EV_RK, EV_RV, EV_RG = 1536, 1792, 2048, 2560
EV_GA = 3072
EV_COLS = 3200
EV_BLOCK = 256
PAIR = 2


def _gate_and_norm(o, gate, gain=None):
    return _rms(o, gain) * jax.nn.silu(gate)


def _even_prompt_kernel(x_ref, xnext_ref, gain_ref, win_ref, wout_ref, cos_ref, sin_ref, ld_ref, wgate_ref, bgate_ref,
                        gnorm_ref, s0g_ref, s0r_ref, o_ref, sg_ref, sr_ref, proj_a, proj_b, mix_sc, o_sc,
                        *, chunk, n_chunks, tiles_per_seq):
    n = pl.program_id(0)

    @pl.when(n % tiles_per_seq == 0)
    def _():
        sg_ref[...] = s0g_ref[...]
        sr_ref[...] = s0r_ref[...]

    @pl.when(n == 0)
    def _():
        proj_a[...] = _dot(_bf(_rms(x_ref[...], gain_ref[...])), win_ref[...])

    args = (x_ref, xnext_ref, gain_ref, win_ref, wout_ref, cos_ref, sin_ref, ld_ref, wgate_ref, bgate_ref, gnorm_ref,
            o_ref, sg_ref, sr_ref)

    @pl.when(n % 2 == 0)
    def _():
        _even_prompt_tile(*args, proj_a, proj_b, mix_sc, o_sc, chunk=chunk, n_chunks=n_chunks)

    @pl.when(n % 2 == 1)
    def _():
        _even_prompt_tile(*args, proj_b, proj_a, mix_sc, o_sc, chunk=chunk, n_chunks=n_chunks)


def _even_prompt_tile(x_ref, xnext_ref, gain_ref, win_ref, wout_ref, cos_ref, sin_ref, ld_ref, wgate_ref, bgate_ref,
                      gnorm_ref, o_ref, sg_ref, sr_ref, proj_sc, proj_next, mix_sc, o_sc, *, chunk, n_chunks):
    c = chunk
    lt = c * n_chunks
    blk = min(EV_BLOCK, lt)
    c_shift = c.bit_length() - 1
    x = x_ref[...]
    xn_next = _bf(_rms(xnext_ref[...], gain_ref[...]))
    piece_cols = [(a, min(a + 256, EV_COLS)) for a in range(0, EV_COLS, 256)]
    n_serial = (GLA_HEADS + RET_HEADS) // PAIR * n_chunks

    def issue_next_projection(step_index):
        for i, (a, b) in enumerate(piece_cols):
            if (i * n_serial) // len(piece_cols) == step_index:
                proj_next[:, a:b] = _dot(xn_next, win_ref[:, a:b])

    def cols(a, b):
        return proj_sc[:, a:b]

    log_a = _log_sigmoid(_dot(_bf(cols(EV_GA, EV_COLS)), wgate_ref[...]) + bgate_ref[...]) * (1.0 / GLA_TAU)
    tril = _bf((_iota((c, c), 1) <= _iota((c, c), 0)).astype(F32))
    cum_parts = [_dot_exact_lhs(tril, log_a[i * c:(i + 1) * c]) for i in range(n_chunks)]
    tots = [p[c - 1:c] for p in cum_parts]
    cum = jnp.concatenate(cum_parts, axis=0)
    tot_b = jnp.concatenate([jnp.broadcast_to(t, (c, 256)) for t in tots], axis=0)
    k = cols(EV_GK, EV_GK + 256)
    gla = (cols(EV_GQ, EV_GQ + 256) * (GLA_DK ** -0.5) * jnp.exp(cum), k * jnp.exp(-cum), k * jnp.exp(tot_b - cum))

    ld = ld_ref[...]
    tpos = ((_iota((lt, 1), 0) & (c - 1)) + 1).astype(F32)
    cum_r = tpos * ld
    tot_r = float(c) * ld
    cos, sin = _lane_tile(cos_ref[...], PAIR), _lane_tile(sin_ref[...], PAIR)
    rq = cols(EV_RQ, EV_RQ + 256)
    rk = cols(EV_RK, EV_RK + 256)
    q_rot = rq * cos + _swap_pairs(rq) * sin
    k_rot = (rk * cos + _swap_pairs(rk) * sin) * (RET_DK ** -0.5)
    ret = (q_rot * jnp.exp(cum_r), k_rot * jnp.exp(-cum_r), k_rot * jnp.exp(tot_r - cum_r))

    tot_rows = jnp.concatenate(tots + [tot_r, jnp.zeros((LANES - n_chunks - 1, 256), F32)], axis=0)
    decay_cols = jnp.exp(jnp.transpose(tot_rows))

    row = _iota((blk, blk), 0)
    col = _iota((blk, blk), 1)
    blk_mask = ((row >> c_shift) == (col >> c_shift)) & (col <= row)
    lo = (_iota((lt, LANES), 1) < GLA_DK)
    gnorm = gnorm_ref[...]

    mixers = ((gla, EV_GV, EV_GG, sg_ref, 0, gnorm, lambda i: i),
              (ret, EV_RV, EV_RG, sr_ref, 512, None, lambda i: n_chunks))
    n_pairs = GLA_HEADS // PAIR
    groups = [(m, p) for m in range(len(mixers)) for p in range(n_pairs)]

    def prepare(m, p):
        (q_dec, k_inv, k_dec), v_col = mixers[m][0], mixers[m][1]
        lanes = slice(p * LANES, (p + 1) * LANES)
        q_pair = q_dec[:, lanes]
        q_masked = [_bf(jnp.where(lo, q_pair, 0.0)), _bf(jnp.where(lo, 0.0, q_pair))]
        v_pair = _bf(cols(v_col + p * PAIR * GLA_DV, v_col + (p + 1) * PAIR * GLA_DV))
        return q_masked, _bf(k_inv[:, lanes]), _bf(k_dec[:, lanes]), v_pair

    def intra_units(m, p, prepared):
        q_masked, ki, _, v_pair = prepared
        units = []
        for e in range(PAIR):
            slot = (m * n_pairs + p) * PAIR + e
            for r0 in range(0, lt, blk):
                def unit(e=e, slot=slot, rs=slice(r0, r0 + blk)):
                    scores = jnp.where(blk_mask, _dot_nt(q_masked[e][rs], ki[rs]), 0.0)
                    o_sc[slot, rs, :] = _dot(_bf(scores), v_pair[rs, e * GLA_DV:(e + 1) * GLA_DV])
                units.append(unit)
        return units

    prepared = prepare(*groups[0])
    for unit in intra_units(*groups[0], prepared):
        unit()
    for gi, (m, p) in enumerate(groups):
        _, v_col, g_col, s_ref, out_col, gain, decay_col_of = mixers[m]
        q_masked, _, kd, v_pair = prepared
        if gi + 1 < len(groups):
            next_prepared = prepare(*groups[gi + 1])
            pending = intra_units(*groups[gi + 1], next_prepared)
        else:
            next_prepared, pending = None, []
        stride = max(1, n_chunks // max(1, len(pending)))
        state = s_ref[0, p * PAIR:(p + 1) * PAIR].reshape(PAIR * GLA_DK, GLA_DV)
        for i in range(n_chunks):
            issue_next_projection(gi * n_chunks + i)
            if i % stride == 0 and i // stride < len(pending):
                pending[i // stride]()
            rs = slice(i * c, (i + 1) * c)
            q_stack = jnp.concatenate([q_masked[0][rs], q_masked[1][rs]], axis=0)
            o_inter = _dot(q_stack, _bf(state))
            kv = _dot_tn(kd[rs], v_pair[rs])
            kv = jnp.concatenate([kv[:GLA_DK, :GLA_DV], kv[GLA_DK:, GLA_DV:]], axis=0)
            ci = decay_col_of(i)
            state = state * decay_cols[p * LANES:(p + 1) * LANES, ci:ci + 1] + kv
            for e in range(PAIR):
                slot = (m * n_pairs + p) * PAIR + e
                o_sc[slot, rs, :] += o_inter[e * c:(e + 1) * c]
        for k in range(n_chunks // stride, len(pending)):
            pending[k]()
        s_ref[0, p * PAIR:(p + 1) * PAIR] = state.reshape(PAIR, GLA_DK, GLA_DV)
        for e in range(PAIR):
            h = p * PAIR + e
            slot = (m * n_pairs + p) * PAIR + e
            gate = cols(g_col + h * GLA_DV, g_col + (h + 1) * GLA_DV)
            mix_sc[:, out_col + h * GLA_DV:out_col + (h + 1) * GLA_DV] = _gate_and_norm(o_sc[slot], gate, gain)
        prepared = next_prepared
    o_ref[...] = x + _dot(_bf(mix_sc[...]), wout_ref[...])


def _even_prompt(x, gain, w_in, w_out, cos, sin, ld_row, wgate, bgate, gnorm, s0g, s0r, *, lt):
    b, l, d = x.shape
    chunk = math.gcd(l, LA_CHUNK)
    nj = l // lt
    n_tiles = b * nj
    x2 = x.reshape(b * l, d)
    st_spec = pl.BlockSpec((1, GLA_HEADS, GLA_DK, GLA_DV), lambda n: (n // nj, 0, 0, 0))
    io_spec = pl.BlockSpec((lt, d), lambda n: (n, 0))
    next_spec = pl.BlockSpec((lt, d), lambda n: (jnp.minimum(n + 1, n_tiles - 1), 0))
    tab_spec = pl.BlockSpec((lt, LANES), lambda n: (n % nj, 0))
    full = lambda a: pl.BlockSpec(a.shape, lambda n: (0,) * a.ndim)
    out, sg, sr = pl.pallas_call(
        functools.partial(_even_prompt_kernel, chunk=chunk, n_chunks=lt // chunk, tiles_per_seq=nj),
        grid=(n_tiles,),
        in_specs=[
            io_spec, next_spec, full(gain), full(w_in), full(w_out), tab_spec, tab_spec,
            full(ld_row), full(wgate), full(bgate), full(gnorm), st_spec, st_spec,
        ],
        out_specs=[io_spec, st_spec, st_spec],
        out_shape=[jax.ShapeDtypeStruct(x2.shape, F32),
                   jax.ShapeDtypeStruct(s0g.shape, F32), jax.ShapeDtypeStruct(s0r.shape, F32)],
        scratch_shapes=[pltpu.VMEM((lt, EV_COLS), F32), pltpu.VMEM((lt, EV_COLS), F32), pltpu.VMEM((lt, d), F32),
                        pltpu.VMEM((GLA_HEADS + RET_HEADS, lt, GLA_DV), F32)],
        compiler_params=_params("arbitrary"),
        name="even_prompt",
    )(x2, x2, gain, w_in, w_out, cos, sin, ld_row, wgate, bgate, gnorm, s0g, s0r)
    return out.reshape(b, l, d), sg, sr


def _even_sample_kernel(proj_ref, cos_ref, sin_ref, ld_ref, wgate_ref, bgate_ref, gnorm_ref, s0g_ref, s0r_ref,
                        mix_ref, sg_ref, sr_ref, *, bg, ls):
    r = bg * ls
    ls_shift = ls.bit_length() - 1
    dk_shift = GLA_DK.bit_length() - 1
    n_exp = bg * GLA_DK
    row_seq = _iota((r, r), 0) >> ls_shift
    col_seq = _iota((r, r), 1) >> ls_shift
    same = row_seq == col_seq
    seg = same & (_iota((r, r), 1) <= _iota((r, r), 0))
    seg_b = _bf(seg.astype(F32))
    same_b = _bf(same.astype(F32))
    tile_b = _bf(((_iota((GLA_DK, n_exp), 1) & (GLA_DK - 1)) == _iota((GLA_DK, n_exp), 0)).astype(F32))
    tile_t_b = _bf(((_iota((n_exp, GLA_DK), 0) & (GLA_DK - 1)) == _iota((n_exp, GLA_DK), 1)).astype(F32))
    q_mask = (_iota((r, n_exp), 0) >> ls_shift) == (_iota((r, n_exp), 1) >> dk_shift)
    k_mask = (_iota((n_exp, r), 0) >> dk_shift) == (_iota((n_exp, r), 1) >> ls_shift)
    tpos = ((_iota((r, 1), 0) & (ls - 1)) + 1).astype(F32)
    ld = ld_ref[...]
    cum_r = tpos * ld
    tot_r = float(ls) * ld
    r_dec = jnp.exp(tot_r)
    gnorm = gnorm_ref[...]

    def mixer(q_dec, k_inv, k_dec, v_col, g_col, s0_ref, s_ref, out_col, decay_of, gain):
        for h in range(GLA_HEADS):
            sl = slice(h * GLA_DK, (h + 1) * GLA_DK)
            v = _bf(proj_ref[:, v_col + h * GLA_DV:v_col + (h + 1) * GLA_DV])
            qd = _bf(q_dec[:, sl])
            scores = jnp.where(seg, _dot_nt(qd, _bf(k_inv[:, sl])), 0.0)
            state = s0_ref[:, h].reshape(n_exp, GLA_DV)
            q_exp = _bf(jnp.where(q_mask, _dot(qd, tile_b), 0.0))
            o = _dot(_bf(scores), v) + _dot(q_exp, _bf(state))
            k_exp = _bf(jnp.where(k_mask, _dot_nt(tile_t_b, _bf(k_dec[:, sl])), 0.0))
            new_state = state * decay_of(h, sl) + _dot(k_exp, v)
            s_ref[:, h] = new_state.reshape(bg, GLA_DK, GLA_DV)
            gate = proj_ref[:, g_col + h * GLA_DV:g_col + (h + 1) * GLA_DV]
            mix_ref[:, out_col + h * GLA_DV:out_col + (h + 1) * GLA_DV] = _gate_and_norm(o, gate, gain)

    log_a = _log_sigmoid(_dot(_bf(proj_ref[:, EV_GA:EV_COLS]), wgate_ref[...]) + bgate_ref[...]) * (1.0 / GLA_TAU)
    cum = _dot_exact_lhs(seg_b, log_a)
    tot = _dot_exact_lhs(same_b, log_a)
    k = proj_ref[:, EV_GK:EV_GK + 256]

    def gla_decay(h, sl):
        la_exp = jnp.where(k_mask, _dot_nt_exact_lhs(tile_t_b, log_a[:, sl]), 0.0)
        return jnp.exp(jnp.sum(la_exp, axis=-1, keepdims=True))

    mixer(proj_ref[:, EV_GQ:EV_GQ + 256] * (GLA_DK ** -0.5) * jnp.exp(cum), k * jnp.exp(-cum), k * jnp.exp(tot - cum),
          EV_GV, EV_GG, s0g_ref, sg_ref, 0, gla_decay, gnorm)

    cos = _lane_tile(cos_ref[...], PAIR)
    sin = _lane_tile(sin_ref[...], PAIR)
    rq = proj_ref[:, EV_RQ:EV_RQ + 256]
    rk = proj_ref[:, EV_RK:EV_RK + 256]
    q_rot = rq * cos + _swap_pairs(rq) * sin
    k_rot = (rk * cos + _swap_pairs(rk) * sin) * (RET_DK ** -0.5)

    def ret_decay(h, sl):
        return r_dec[:, h * RET_DK:h * RET_DK + 1]

    mixer(q_rot * jnp.exp(cum_r), k_rot * jnp.exp(-cum_r), k_rot * jnp.exp(tot_r - cum_r),
          EV_RV, EV_RG, s0r_ref, sr_ref, 512, ret_decay, None)


def _even_sample(proj, cos, sin, ld_row, wgate, bgate, gnorm, s0g, s0r, *, bg, ls):
    t = proj.shape[0]
    n_b = t // ls
    r = bg * ls
    st_spec = pl.BlockSpec((bg, GLA_HEADS, GLA_DK, GLA_DV), lambda i: (i, 0, 0, 0))
    full = lambda a: pl.BlockSpec(a.shape, lambda i: (0,) * a.ndim)
    return pl.pallas_call(
        functools.partial(_even_sample_kernel, bg=bg, ls=ls),
        grid=(n_b // bg,),
        in_specs=[
            pl.BlockSpec((r, EV_COLS), lambda i: (i, 0)),
            full(cos), full(sin), full(ld_row), full(wgate), full(bgate), full(gnorm), st_spec, st_spec,
        ],
        out_specs=[pl.BlockSpec((r, D_MODEL), lambda i: (i, 0)), st_spec, st_spec],
        out_shape=[jax.ShapeDtypeStruct((t, D_MODEL), F32),
                   jax.ShapeDtypeStruct(s0g.shape, F32), jax.ShapeDtypeStruct(s0r.shape, F32)],
        compiler_params=_params("parallel"),
        name="even_sample",
    )(proj, cos, sin, ld_row, wgate, bgate, gnorm, s0g, s0r)


def _s5_prep_kernel(are_ref, aim_ref, lstep_ref, bre_ref, bim_ref, abre_ref, abim_ref, bbre_ref, bbim_ref):
    a_re, a_im = are_ref[...], aim_ref[...]
    step = jnp.exp(lstep_ref[...])
    mag = jnp.exp(a_re * step)
    ab_re = mag * jnp.cos(a_im * step)
    ab_im = mag * jnp.sin(a_im * step)
    den = a_re * a_re + a_im * a_im
    coef_re = ((ab_re - 1.0) * a_re + ab_im * a_im) / den
    coef_im = (ab_im * a_re - (ab_re - 1.0) * a_im) / den
    b_re, b_im = bre_ref[...], bim_ref[...]
    abre_ref[...] = ab_re
    abim_ref[...] = ab_im
    bbre_ref[...] = coef_re * b_re - coef_im * b_im
    bbim_ref[...] = coef_re * b_im + coef_im * b_re


def _s5_prep(a_re, a_im, log_step, b_re, b_im):
    g, n = a_re.shape
    shp3 = jax.ShapeDtypeStruct((g, 1, n), F32)
    shpb = jax.ShapeDtypeStruct((g, S5_GROUP, n), F32)
    return pl.pallas_call(_s5_prep_kernel, out_shape=[shp3, shp3, shpb, shpb], name="s5_prep")(
        a_re.reshape(g, 1, n), a_im.reshape(g, 1, n), log_step.reshape(g, 1, 1),
        jnp.swapaxes(b_re, 1, 2), jnp.swapaxes(b_im, 1, 2))


S5_LANE_CHUNKS = S5_GROUPS * S5_STATE // LANES
S5_HALVES = 2


def _s5_layout(n_seq, lt):
    pack = max(1, SUBLANES // n_seq)
    pitch = lt + 4 if lt % SUBLANES == 0 else lt
    return pack, S5_LANE_CHUNKS // pack, pitch


def _s5_slot(c, n_groups):
    return c % n_groups, c // n_groups


def _s5_load_state(s0re_ref, s0im_ref, hre_ref, him_ref, n_seq, n_groups):
    for c in range(S5_LANE_CHUNKS):
        g, j = _s5_slot(c, n_groups)
        hre_ref[g, j * n_seq:(j + 1) * n_seq, :] = s0re_ref[:, c * LANES:(c + 1) * LANES]
        him_ref[g, j * n_seq:(j + 1) * n_seq, :] = s0im_ref[:, c * LANES:(c + 1) * LANES]


def _s5_store_state(sre_ref, sim_ref, hre_ref, him_ref, n_seq, n_groups):
    for c in range(S5_LANE_CHUNKS):
        g, j = _s5_slot(c, n_groups)
        sre_ref[:, c * LANES:(c + 1) * LANES] = hre_ref[g, j * n_seq:(j + 1) * n_seq, :]
        sim_ref[:, c * LANES:(c + 1) * LANES] = him_ref[g, j * n_seq:(j + 1) * n_seq, :]


def _s5_core(u, abre_ref, abim_ref, bbre_ref, bbim_ref, ccre_ref, ccim_ref, d_ref, wglu_ref, bglu_ref,
             xr_ref, xi_ref, hre_ref, him_ref, *, n_seq, lt, groups_per_pass, unroll):
    pack, n_groups, pitch = _s5_layout(n_seq, lt)
    rows = n_seq * lt
    per_half = S5_LANE_CHUNKS // S5_HALVES
    lanes_of = lambda c: slice(c * LANES, (c + 1) * LANES)
    slot_of = lambda c: _s5_slot(c, n_groups)

    def seq_rows(j, s):
        return slice((j * n_seq + s) * pitch, (j * n_seq + s) * pitch + lt)

    ub = _bf(u)
    u_cols = S5_WIDTH // S5_HALVES
    for half in range(S5_HALVES):
        uh = ub[:, half * u_cols:(half + 1) * u_cols]
        for x_ref, bb_ref in ((xr_ref, bbre_ref), (xi_ref, bbim_ref)):
            x = _dot(uh, bb_ref[half])
            for k in range(per_half):
                g, j = slot_of(half * per_half + k)
                if pitch == lt:
                    x_ref[g, j * rows:(j + 1) * rows, :] = x[:, lanes_of(k)]
                else:
                    for s in range(n_seq):
                        x_ref[g, seq_rows(j, s), :] = x[s * lt:(s + 1) * lt, lanes_of(k)]

    for g0 in range(0, n_groups, groups_per_pass):
        gs = list(range(g0, g0 + groups_per_pass))
        init = tuple(hre_ref[g] for g in gs) + tuple(him_ref[g] for g in gs)

        def step(t, carry, gs=gs):
            rws = pl.ds(t, pack * n_seq, stride=pitch)
            new_re, new_im = [], []
            for k, g in enumerate(gs):
                a_re, a_im = abre_ref[g], abim_ref[g]
                h_re, h_im = carry[k], carry[len(gs) + k]
                n_re = a_re * h_re - a_im * h_im + xr_ref[g, rws, :]
                n_im = a_re * h_im + a_im * h_re + xi_ref[g, rws, :]
                xr_ref[g, rws, :] = n_re
                xi_ref[g, rws, :] = n_im
                new_re.append(n_re)
                new_im.append(n_im)
            return tuple(new_re + new_im)

        fin = lax.fori_loop(0, lt, step, init, unroll=unroll)
        for k, g in enumerate(gs):
            hre_ref[g] = fin[k]
            him_ref[g] = fin[len(gs) + k]

    def gather(x_ref, half):
        cols = []
        for k in range(per_half):
            g, j = slot_of(half * per_half + k)
            if pitch == lt:
                cols.append(x_ref[g, j * rows:(j + 1) * rows, :])
            else:
                cols.append(jnp.concatenate([x_ref[g, seq_rows(j, s), :] for s in range(n_seq)], axis=0))
        return _bf(jnp.concatenate(cols, axis=1))

    y = jnp.concatenate([_dot(gather(xr_ref, half), ccre_ref[half]) - _dot(gather(xi_ref, half), ccim_ref[half])
                         for half in range(S5_HALVES)], axis=1) + d_ref[...] * u
    z = jax.nn.gelu(y, approximate=True)
    return z * jax.nn.sigmoid(_dot(_bf(z), wglu_ref[...]) + bglu_ref[...])


def _s5_kernel(u_ref, s0re_ref, s0im_ref, abre_ref, abim_ref, bbre_ref, bbim_ref, ccre_ref, ccim_ref,
               d_ref, wglu_ref, bglu_ref, out_ref, sre_ref, sim_ref, xr_ref, xi_ref, hre_ref, him_ref,
               *, n_seq, lt, groups_per_pass, unroll):
    n_groups = _s5_layout(n_seq, lt)[1]

    @pl.when(pl.program_id(0) == 0)
    def _():
        _s5_load_state(s0re_ref, s0im_ref, hre_ref, him_ref, n_seq, n_groups)

    out = _s5_core(u_ref[...].reshape(n_seq * lt, S5_WIDTH), abre_ref, abim_ref, bbre_ref, bbim_ref, ccre_ref, ccim_ref,
                   d_ref, wglu_ref, bglu_ref, xr_ref, xi_ref, hre_ref, him_ref,
                   n_seq=n_seq, lt=lt, groups_per_pass=groups_per_pass, unroll=unroll)
    out_ref[...] = out.reshape(out_ref.shape)

    @pl.when(pl.program_id(0) == pl.num_programs(0) - 1)
    def _():
        _s5_store_state(sre_ref, sim_ref, hre_ref, him_ref, n_seq, n_groups)


def _s5_tables(ab, n_seq, lt):
    pack, n_groups, _ = _s5_layout(n_seq, lt)
    tab = jnp.swapaxes(ab.reshape(pack, n_groups, 1, LANES), 0, 1)
    return jnp.broadcast_to(tab, (n_groups, pack, n_seq, LANES)).reshape(n_groups, pack * n_seq, LANES)


def _s5(proj3, s0_re, s0_im, ab_re, ab_im, bb_re, bb_im, cc_re, cc_im, d_row, wglu, bglu, *, n_seq, lt,
        groups_per_pass, unroll):
    nb, rows_b, _ = proj3.shape
    blk_rows = n_seq * lt // nb
    pack, n_groups, pitch = _s5_layout(n_seq, lt)
    ab_re, ab_im = _s5_tables(ab_re, n_seq, lt), _s5_tables(ab_im, n_seq, lt)
    full = lambda a: pl.BlockSpec(a.shape, lambda j: (0,) * a.ndim)
    io_spec = pl.BlockSpec((nb, blk_rows, S5_WIDTH), lambda j: (0, j, 0))
    x_scratch = pltpu.VMEM((n_groups, pack * n_seq * pitch, LANES), F32)
    h_scratch = pltpu.VMEM((n_groups, pack * n_seq, LANES), F32)
    return pl.pallas_call(
        functools.partial(_s5_kernel, n_seq=n_seq, lt=lt, groups_per_pass=groups_per_pass, unroll=unroll),
        grid=(rows_b // blk_rows,),
        in_specs=[io_spec, full(s0_re), full(s0_im), full(ab_re), full(ab_im), full(bb_re), full(bb_im),
                  full(cc_re), full(cc_im), full(d_row), full(wglu), full(bglu)],
        out_specs=[io_spec, full(s0_re), full(s0_im)],
        out_shape=[jax.ShapeDtypeStruct((nb, rows_b, S5_WIDTH), F32),
                   jax.ShapeDtypeStruct(s0_re.shape, F32), jax.ShapeDtypeStruct(s0_im.shape, F32)],
        scratch_shapes=[x_scratch, x_scratch, h_scratch, h_scratch],
        compiler_params=_params("arbitrary"),
        name="s5",
    )(proj3, s0_re, s0_im, ab_re, ab_im, bb_re, bb_im, cc_re, cc_im, d_row, wglu, bglu)


OD_Q_BLOCK = 1
OD_KV_BLOCK = 4
OD_COLS = 1280
ROPE_HALF = SWA_HD // 8
SWA_GROUP = SWA_HEADS // SWA_KV_HEADS


def _half_lanes(shape):
    return (_iota(shape, 1) & (LANES - 1)) < SWA_HD


def _pair_rms_scale(x):
    same_head = (_iota((LANES, LANES), 0) >= SWA_HD) == (_iota((LANES, LANES), 1) >= SWA_HD)
    sums = _dot_exact_rhs(x * x, _bf(same_head.astype(F32)), terms=2)
    return lax.rsqrt(sums * (1.0 / SWA_HD) + NORM_EPS)


def _swa_qk(xq, xk, qg, kg, cq, sq):
    xq_g = xq * qg
    n_pairs = SWA_HEADS // 2
    q_rot = xq_g * _lane_tile(cq, n_pairs) + _rope_partner(xq_g, SWA_HD, ROPE_HALF) * _lane_tile(sq, n_pairs)
    xk_g = xk * kg
    k_rot = xk_g * cq + _rope_partner(xk_g, SWA_HD, ROPE_HALF) * sq
    q_pairs = [q_rot[:, j * LANES:(j + 1) * LANES] * _pair_rms_scale(xq[:, j * LANES:(j + 1) * LANES])
               for j in range(SWA_HEADS // 2)]
    return q_pairs, k_rot * _pair_rms_scale(xk)


def _swa_query_stack(q_pairs, kh):
    lo = _half_lanes(q_pairs[0].shape)
    keep = lo if kh == 0 else jnp.logical_not(lo)
    parts = []
    for g in range(SWA_GROUP):
        hq = kh * SWA_GROUP + g
        pair = q_pairs[hq // 2]
        src = pair if hq % 2 == kh else pltpu.roll(pair, SWA_HD, 1)
        parts.append(jnp.where(keep, src, 0.0))
    return _bf(jnp.concatenate(parts, axis=0))


def _swa_merge_heads(o, kh, rows):
    lo = _half_lanes((rows, LANES))
    pairs = []
    for p in range(SWA_GROUP // 2):
        even, odd = o[2 * p * rows:(2 * p + 1) * rows], o[(2 * p + 1) * rows:(2 * p + 2) * rows]
        if kh == 0:
            pairs.append(jnp.where(lo, even, pltpu.roll(odd, SWA_HD, 1)))
        else:
            pairs.append(jnp.where(lo, pltpu.roll(even, SWA_HD, 1), odd))
    return pairs


def _swa_prepare(xq, xk, v_cur, k_prev, v_prev, qg, kg, cq, sq):
    q_pairs, k_cur = _swa_qk(xq, xk, qg, kg, cq, sq)
    k_ext = _bf(jnp.concatenate([k_prev, k_cur], axis=0))
    v_ext = _bf(jnp.concatenate([v_prev, v_cur], axis=0))
    return q_pairs, k_cur, k_ext, v_ext


def _swa_attend(prepared, mask, sink_ref, after_head=None):
    w = SWA_WINDOW
    q_pairs, _, k_ext, v_ext = prepared
    ones_col = jnp.ones((2 * w, LANES), BF16)
    out_pairs = []
    for kh in range(SWA_KV_HEADS):
        s_all = _dot_nt(_swa_query_stack(q_pairs, kh), k_ext)
        weights, sink_terms = [], []
        for g in range(SWA_GROUP):
            s = jnp.where(mask, s_all[g * w:(g + 1) * w] * (SWA_HD ** -0.5), -jnp.inf)
            sink = sink_ref[kh * SWA_GROUP + g:kh * SWA_GROUP + g + 1, :]
            m = jnp.maximum(jnp.broadcast_to(jnp.max(s, axis=-1, keepdims=True), (w, LANES)), sink)
            weights.append(_bf(jnp.exp(s - jnp.concatenate([m, m], axis=1))))
            sink_terms.append(jnp.exp(sink - m))
        weights = jnp.concatenate(weights, axis=0)
        den = _dot(weights, ones_col) + jnp.concatenate(sink_terms, axis=0)
        out_pairs += _swa_merge_heads(_dot(weights, v_ext) / den, kh, w)
        if after_head is not None:
            after_head(kh)
    return out_pairs


def _odd_prompt_kernel(x_ref, xnext_ref, gain_ref, win_ref, wout_ref, s0re_ref, s0im_ref, abre_ref, abim_ref, bbre_ref,
                       bbim_ref, ccre_ref, ccim_ref, d_ref, wglu_ref, bglu_ref, cq_ref, sq_ref, qg_ref, kg_ref, sink_ref,
                       o_ref, sre_ref, sim_ref, ck_ref, cv_ref,
                       proj_a, proj_b, mix_sc, xr_ref, xi_ref, hre_ref, him_ref, kprev_sc, vprev_sc,
                       *, n_seq, groups_per_pass, unroll):
    n_groups = _s5_layout(n_seq, SWA_WINDOW)[1]
    step = pl.program_id(0)

    @pl.when(step == 0)
    def _():
        _s5_load_state(s0re_ref, s0im_ref, hre_ref, him_ref, n_seq, n_groups)
        kprev_sc[...] = jnp.zeros_like(kprev_sc)
        vprev_sc[...] = jnp.zeros_like(vprev_sc)
        x0 = x_ref[...].reshape(n_seq * SWA_WINDOW, D_MODEL)
        proj_a[...] = _dot(_bf(_rms(x0, gain_ref[...])), win_ref[...])

    args = (x_ref, xnext_ref, gain_ref, win_ref, wout_ref, abre_ref, abim_ref, bbre_ref, bbim_ref, ccre_ref, ccim_ref,
            d_ref, wglu_ref, bglu_ref, cq_ref, sq_ref, qg_ref, kg_ref, sink_ref, o_ref)
    scratch = (mix_sc, xr_ref, xi_ref, hre_ref, him_ref, kprev_sc, vprev_sc)
    kw = dict(n_seq=n_seq, groups_per_pass=groups_per_pass, unroll=unroll)

    @pl.when(step % 2 == 0)
    def _():
        _odd_prompt_step(*args, proj_a, proj_b, *scratch, **kw)

    @pl.when(step % 2 == 1)
    def _():
        _odd_prompt_step(*args, proj_b, proj_a, *scratch, **kw)

    @pl.when(step == pl.num_programs(0) - 1)
    def _():
        _s5_store_state(sre_ref, sim_ref, hre_ref, him_ref, n_seq, n_groups)
        ck_ref[...] = kprev_sc[...]
        cv_ref[...] = vprev_sc[...]


def _odd_prompt_step(x_ref, xnext_ref, gain_ref, win_ref, wout_ref, abre_ref, abim_ref, bbre_ref, bbim_ref, ccre_ref,
                     ccim_ref, d_ref, wglu_ref, bglu_ref, cq_ref, sq_ref, qg_ref, kg_ref, sink_ref, o_ref,
                     proj_sc, proj_next, mix_sc, xr_ref, xi_ref, hre_ref, him_ref, kprev_sc, vprev_sc,
                     *, n_seq, groups_per_pass, unroll):
    w = SWA_WINDOW
    rows = n_seq * w
    x = x_ref[...].reshape(rows, D_MODEL)
    xn_next = _bf(_rms(xnext_ref[...].reshape(rows, D_MODEL), gain_ref[...]))
    piece_cols = [(a, min(a + 256, OD_COLS)) for a in range(0, OD_COLS, 256)]
    n_slots = n_seq * SWA_KV_HEADS

    def issue_next_projection(slot_index):
        for i, (a, b) in enumerate(piece_cols):
            if (i * n_slots) // len(piece_cols) == slot_index:
                proj_next[:, a:b] = _dot(xn_next, win_ref[:, a:b])

    mix_sc[:, 0:S5_WIDTH] = _s5_core(proj_sc[:, 0:S5_WIDTH], abre_ref, abim_ref, bbre_ref, bbim_ref, ccre_ref, ccim_ref,
                                     d_ref, wglu_ref, bglu_ref, xr_ref, xi_ref, hre_ref, him_ref,
                                     n_seq=n_seq, lt=w, groups_per_pass=groups_per_pass, unroll=unroll)

    t_idx = _iota((w, 2 * w), 0)
    s_idx = _iota((w, 2 * w), 1)
    mask = (s_idx > t_idx) & (s_idx <= t_idx + w) & (s_idx >= jnp.where(pl.program_id(0) == 0, w, 0))
    q0, k0, v0 = OD_Q_BLOCK * 512, OD_KV_BLOCK * 256, OD_KV_BLOCK * 256 + LANES
    def prepare(s):
        rs = slice(s * w, (s + 1) * w)
        v_cur = proj_sc[rs, v0:v0 + LANES]
        return (v_cur,) + _swa_prepare(proj_sc[rs, q0:q0 + 512], proj_sc[rs, k0:k0 + LANES], v_cur, kprev_sc[s],
                                       vprev_sc[s], qg_ref[...], kg_ref[...], cq_ref[...], sq_ref[...])

    state = {"next": prepare(0)}
    for s in range(n_seq):
        rs = slice(s * w, (s + 1) * w)
        v_cur, *prepared = state["next"]

        def after_head(kh, s=s):
            issue_next_projection(s * SWA_KV_HEADS + kh)
            if kh == 0 and s + 1 < n_seq:
                state["next"] = prepare(s + 1)

        pairs = _swa_attend(prepared, mask, sink_ref, after_head=after_head)
        for i, pair in enumerate(pairs):
            mix_sc[rs, S5_WIDTH + i * LANES:S5_WIDTH + (i + 1) * LANES] = pair
        kprev_sc[s] = prepared[1]
        vprev_sc[s] = v_cur
    o_ref[...] = (x + _dot(_bf(mix_sc[...]), wout_ref[...])).reshape(o_ref.shape)


def _odd_prompt(x, gain, w_in, w_out, s0_re, s0_im, ab_re, ab_im, bb_re, bb_im, cc_re, cc_im, d_row, wglu, bglu,
                cq, sq, qg, kg, sink_rows, *, groups_per_pass, unroll):
    b, l, d = x.shape
    w = SWA_WINDOW
    pack, n_groups, pitch = _s5_layout(b, w)
    ab_re, ab_im = _s5_tables(ab_re, b, w), _s5_tables(ab_im, b, w)
    full = lambda a: pl.BlockSpec(a.shape, lambda j: (0,) * a.ndim)
    io_spec = pl.BlockSpec((b, w, d), lambda j: (0, j, 0))
    n_steps = l // w
    next_spec = pl.BlockSpec((b, w, d), lambda j: (0, jnp.minimum(j + 1, n_steps - 1), 0))
    tab_spec = pl.BlockSpec((w, LANES), lambda j: (j, 0))
    cache_shape = jax.ShapeDtypeStruct((b, w, LANES), F32)
    x_scratch = pltpu.VMEM((n_groups, pack * b * pitch, LANES), F32)
    h_scratch = pltpu.VMEM((n_groups, pack * b, LANES), F32)
    kv_scratch = pltpu.VMEM((b, w, LANES), F32)
    return pl.pallas_call(
        functools.partial(_odd_prompt_kernel, n_seq=b, groups_per_pass=groups_per_pass, unroll=unroll),
        grid=(l // w,),
        in_specs=[io_spec, next_spec, full(gain), full(w_in), full(w_out), full(s0_re), full(s0_im), full(ab_re), full(ab_im),
                  full(bb_re), full(bb_im), full(cc_re), full(cc_im), full(d_row), full(wglu), full(bglu),
                  tab_spec, tab_spec, full(qg), full(kg), full(sink_rows)],
        out_specs=[io_spec, full(s0_re), full(s0_im), pl.BlockSpec((b, w, LANES), lambda j: (0, 0, 0)),
                   pl.BlockSpec((b, w, LANES), lambda j: (0, 0, 0))],
        out_shape=[jax.ShapeDtypeStruct(x.shape, F32), jax.ShapeDtypeStruct(s0_re.shape, F32),
                   jax.ShapeDtypeStruct(s0_im.shape, F32), cache_shape, cache_shape],
        scratch_shapes=[pltpu.VMEM((b * w, OD_COLS), F32), pltpu.VMEM((b * w, OD_COLS), F32), pltpu.VMEM((b * w, d), F32),
                        x_scratch, x_scratch, h_scratch, h_scratch, kv_scratch, kv_scratch],
        compiler_params=_params("arbitrary"),
        name="odd_prompt",
    )(x, x, gain, w_in, w_out, s0_re, s0_im, ab_re, ab_im, bb_re, bb_im, cc_re, cc_im, d_row, wglu, bglu,
      cq, sq, qg, kg, sink_rows)


def _swa_sample_kernel(q_ref, kv_ref, ck_ref, cv_ref, cq_ref, sq_ref, qg_ref, kg_ref, sink_ref,
                       o_ref, nk_ref, nv_ref, *, bg, ls):
    w = SWA_WINDOW
    r = bg * ls
    ls_shift = ls.bit_length() - 1
    w_shift = w.bit_length() - 1
    rows_g = SWA_GROUP * r
    v_new = kv_ref[:, LANES:2 * LANES]
    q_pairs, k_new = _swa_qk(q_ref[...], kv_ref[:, 0:LANES], qg_ref[...], kg_ref[...], cq_ref[...], sq_ref[...])
    k_cache = _bf(ck_ref[...].reshape(bg * w, LANES))
    v_cache = _bf(cv_ref[...].reshape(bg * w, LANES))

    row = _iota((rows_g, bg * w), 0) & (r - 1)
    col = _iota((rows_g, bg * w), 1)
    mask_c = ((row >> ls_shift) == (col >> w_shift)) & ((col & (w - 1)) > (row & (ls - 1)))
    row_n = _iota((rows_g, r), 0) & (r - 1)
    col_n = _iota((rows_g, r), 1)
    mask_n = ((row_n >> ls_shift) == (col_n >> ls_shift)) & ((col_n & (ls - 1)) <= (row_n & (ls - 1)))
    for kh in range(SWA_KV_HEADS):
        q_stack = _swa_query_stack(q_pairs, kh)
        s_c = jnp.where(mask_c, _dot_nt(q_stack, k_cache) * (SWA_HD ** -0.5), -jnp.inf)
        s_n = jnp.where(mask_n, _dot_nt(q_stack, _bf(k_new)) * (SWA_HD ** -0.5), -jnp.inf)
        sink = jnp.concatenate([jnp.broadcast_to(sink_ref[kh * SWA_GROUP + g:kh * SWA_GROUP + g + 1, :], (r, LANES))
                                for g in range(SWA_GROUP)], axis=0)
        row_max = jnp.maximum(jnp.max(s_c, axis=-1, keepdims=True), jnp.max(s_n, axis=-1, keepdims=True))
        m = jnp.maximum(jnp.broadcast_to(row_max, (rows_g, LANES)), sink)
        e_c = _bf(jnp.exp(s_c - jnp.concatenate([m] * bg, axis=1)))
        e_n = _bf(jnp.exp(s_n - m[:, :r]))
        den = (_dot(e_c, jnp.ones((bg * w, LANES), BF16)) + _dot(e_n, jnp.ones((r, LANES), BF16))
               + jnp.exp(sink - m))
        o = (_dot(e_c, v_cache) + _dot(e_n, _bf(v_new))) / den
        for i, pair in enumerate(_swa_merge_heads(o, kh, r)):
            col_i = kh * (SWA_GROUP // 2) + i
            o_ref[:, col_i * LANES:(col_i + 1) * LANES] = pair

    nk_ref[:, 0:w - ls, :] = ck_ref[:, ls:w, :]
    nv_ref[:, 0:w - ls, :] = cv_ref[:, ls:w, :]
    for b in range(bg):
        nk_ref[b, w - ls:w, :] = k_new[b * ls:(b + 1) * ls, :]
        nv_ref[b, w - ls:w, :] = v_new[b * ls:(b + 1) * ls, :]


def _swa_sample(proj, cache_k, cache_v, cq, sq, qg, kg, sink_rows, *, bg, ls):
    t = proj.shape[0]
    w = SWA_WINDOW
    r = bg * ls
    full = lambda a: pl.BlockSpec(a.shape, lambda i: (0,) * a.ndim)
    cache_spec = pl.BlockSpec((bg, w, LANES), lambda i: (i, 0, 0))
    return pl.pallas_call(
        functools.partial(_swa_sample_kernel, bg=bg, ls=ls),
        grid=(t // r,),
        in_specs=[
            pl.BlockSpec((r, 512), lambda i: (i, OD_Q_BLOCK)),
            pl.BlockSpec((r, 256), lambda i: (i, OD_KV_BLOCK)),
            cache_spec, cache_spec, full(cq), full(sq), full(qg), full(kg), full(sink_rows),
        ],
        out_specs=[pl.BlockSpec((r, 512), lambda i: (i, 0)), cache_spec, cache_spec],
        out_shape=[jax.ShapeDtypeStruct((t, 512), F32),
                   jax.ShapeDtypeStruct(cache_k.shape, F32), jax.ShapeDtypeStruct(cache_v.shape, F32)],
        compiler_params=_params("parallel"),
        name="swa_sample",
    )(proj, proj, cache_k, cache_v, cq, sq, qg, kg, sink_rows)


def _mem_prompt_kernel(x_ref, g_ref, wq_ref, qg_ref, k_ref, v_ref, wo_ref, o_ref):
    x = x_ref[0]
    xn = _bf(_rms(x, g_ref[...]))
    k_all = _bf(_mem_rows(k_ref, 0))
    v_all = _bf(_mem_rows(v_ref, 0))
    heads = [slice(h * MEM_HD, (h + 1) * MEM_HD) for h in range(MEM_HEADS)]

    def q_head(h):
        return _bf(_rms(_dot(xn, wq_ref[:, heads[h]]), qg_ref[...]))

    def scores(q, h):
        return _dot_nt(q, k_all[:, heads[h]]) * (MEM_HD ** -0.5)

    acc = x
    q_next = q_head(1)
    s_next = scores(q_head(0), 0)
    o_prev = None
    for h in range(MEM_HEADS):
        s = s_next
        if h + 1 < MEM_HEADS:
            s_next = scores(q_next, h + 1)
        if h + 2 < MEM_HEADS:
            q_next = q_head(h + 2)
        e = jnp.exp(s - jnp.max(s, axis=-1, keepdims=True))
        p = e / jnp.sum(e, axis=-1, keepdims=True)
        if o_prev is not None:
            acc = acc + _dot(_bf(o_prev), wo_ref[heads[h - 1], :])
        o_prev = _dot(_bf(p), v_all[:, heads[h]])
    o_ref[0] = acc + _dot(_bf(o_prev), wo_ref[heads[MEM_HEADS - 1], :])


def _mem_prompt(x, gain, wq, q_gain, k, v, wo, *, layer, lt):
    b, l, d = x.shape
    k, v = _mem_flat_view(k), _mem_flat_view(v)
    kv_spec = pl.BlockSpec((None, 1, k.shape[2], LANES), lambda i, j: (layer, i, 0, 0))
    io_spec = pl.BlockSpec((1, lt, d), lambda i, j: (i, j, 0))
    full = lambda a: pl.BlockSpec(a.shape, lambda i, j: (0,) * a.ndim)
    resident = lambda a: pl.BlockSpec(a.shape, lambda i, j: (0,) * a.ndim, pipeline_mode=pl.Buffered(1))
    return pl.pallas_call(
        _mem_prompt_kernel,
        grid=(b, l // lt),
        in_specs=[io_spec, full(gain), resident(wq), full(q_gain), kv_spec, kv_spec, resident(wo)],
        out_specs=io_spec,
        out_shape=jax.ShapeDtypeStruct(x.shape, F32),
        compiler_params=_params("parallel", "arbitrary"),
        name="mem_prompt",
    )(x, gain, wq, q_gain, k, v, wo)


def _memkv_kernel(mem_ref, g_ref, wk_ref, wv_ref, kg_ref, k_out, v_out):
    xn = _bf(_rms(mem_ref[...], g_ref[...]))
    nb, n_mem, heads, hd = k_out.shape
    for h in range(heads):
        sl = slice(h * hd, (h + 1) * hd)
        k_out[:, :, h, :] = _rms(_dot(xn, wk_ref[:, sl]), kg_ref[...]).reshape(nb, n_mem, hd)
        v_out[:, :, h, :] = _dot(xn, wv_ref[:, sl]).reshape(nb, n_mem, hd)


def _memkv(mem, m_gain, wk, wv, k_gain):
    nb, n_mem, d = mem.shape
    depth = wk.shape[0]
    out_shape = jax.ShapeDtypeStruct((depth, nb, n_mem, MEM_HEADS, MEM_HD), F32)
    per_layer = lambda s: pl.BlockSpec((None,) + s, lambda l: (l,) + (0,) * len(s))
    return pl.pallas_call(
        _memkv_kernel,
        grid=(depth,),
        in_specs=[pl.BlockSpec((nb * n_mem, d), lambda l: (0, 0)), per_layer((1, d)), per_layer((d, d)),
                  per_layer((d, d)), per_layer((1, MEM_HD))],
        out_specs=[per_layer((nb, n_mem, MEM_HEADS, MEM_HD)), per_layer((nb, n_mem, MEM_HEADS, MEM_HD))],
        out_shape=[out_shape, out_shape],
        compiler_params=_params("arbitrary"),
        name="memkv",
    )(mem.reshape(nb * n_mem, d), m_gain, wk, wv, k_gain)


def _mem_sample_kernel(q_ref, k_ref, v_ref, o_ref, *, bs, ls):
    r = bs * ls
    per_seq = MEM_HEADS * ls
    n_exp = bs * per_seq
    n_mem = k_ref.shape[1] // MEM_ROW_GROUP
    hd_shift = MEM_HD.bit_length() - 1
    ls_shift = ls.bit_length() - 1
    seq_shift = per_seq.bit_length() - 1
    mem_shift = n_mem.bit_length() - 1
    qb = _bf(q_ref[...])
    e_row = _iota((n_exp, r), 0)
    sel = _bf((((e_row >> seq_shift) << ls_shift) + (e_row & (ls - 1)) == _iota((n_exp, r), 1)).astype(F32))
    head_mask = (((_iota((n_exp, D_MODEL), 0) >> ls_shift) & (MEM_HEADS - 1))
                 == (_iota((n_exp, D_MODEL), 1) >> hd_shift))
    q_exp = _bf(jnp.where(head_mask, _dot(sel, qb), 0.0))
    k_all = _bf(jnp.concatenate([_mem_rows(k_ref, b) for b in range(bs)], axis=0))
    v_all = _bf(jnp.concatenate([_mem_rows(v_ref, b) for b in range(bs)], axis=0))
    own = ((_iota((bs * n_mem, n_exp), 0) >> mem_shift) == (_iota((bs * n_mem, n_exp), 1) >> seq_shift))
    own = own.reshape(bs, n_mem, n_exp)
    s = (_dot_nt(k_all, q_exp) * (MEM_HD ** -0.5)).reshape(bs, n_mem, n_exp)
    s = jnp.where(own, s, -1e30)
    e = jnp.where(own, jnp.exp(s - jnp.max(s, axis=1, keepdims=True)), 0.0)
    den = jnp.sum(e, axis=1, keepdims=True) + jnp.where(jnp.any(own, axis=1, keepdims=True), 0.0, 1.0)
    p = _bf((e / den).reshape(bs * n_mem, n_exp))
    o_all = jnp.where(head_mask, _dot_tn(p, v_all), 0.0)
    o_ref[...] = _dot_tn(sel, _bf(o_all))


MEM_LANE_TILES = MEM_HD // LANES
MEM_ROW_GROUP = MEM_HEADS * MEM_LANE_TILES


def _mem_rows(ref, b):
    n_mem = ref.shape[1] // MEM_ROW_GROUP
    return jnp.concatenate([ref[b, pl.ds(lt * MEM_HEADS + h, n_mem, stride=MEM_ROW_GROUP), :]
                            for h in range(MEM_HEADS) for lt in range(MEM_LANE_TILES)], axis=1)


def _mem_flat_view(a):
    depth, nb, n_mem, heads, hd = a.shape
    a = a.reshape(depth, nb, n_mem, heads, hd // LANES, LANES).transpose(0, 1, 2, 4, 3, 5)
    return a.reshape(depth, nb, n_mem * MEM_ROW_GROUP, LANES)


def _mem_sample(q, k, v, *, layer, bs, ls):
    t, d = q.shape
    k, v = _mem_flat_view(k), _mem_flat_view(v)
    r = bs * ls
    kv_spec = pl.BlockSpec((None, bs, k.shape[2], LANES), lambda i: (layer, i, 0, 0))
    io_spec = pl.BlockSpec((r, d), lambda i: (i, 0))
    return pl.pallas_call(
        functools.partial(_mem_sample_kernel, bs=bs, ls=ls),
        grid=(t // r,),
        in_specs=[io_spec, kv_spec, kv_spec],
        out_specs=io_spec,
        out_shape=jax.ShapeDtypeStruct(q.shape, F32),
        compiler_params=_params("parallel"),
        name="mem_sample",
    )(q, k, v)


def _retention_tables(pos):
    inv = 1.0 / (RET_THETA ** jnp.linspace(0.0, 1.0, RET_DK // 2, dtype=F32))
    ang = pos[:, None] * inv[None, :]
    cos = jnp.repeat(jnp.cos(ang), 2, axis=1)
    sin = jnp.stack([-jnp.sin(ang), jnp.sin(ang)], axis=-1).reshape(pos.shape[0], RET_DK)
    return jnp.tile(cos, (1, PAIR)), jnp.tile(sin, (1, PAIR))


def _rope_tables(pos):
    half = ROPE_HALF
    inv = 1.0 / (ROPE_THETA ** (jnp.arange(half, dtype=F32) * 2.0 / (2 * half)))
    ang = pos[:, None] * inv[None, :]
    n = pos.shape[0]
    rest = SWA_HD - 2 * half
    cos = jnp.concatenate([jnp.cos(ang), jnp.cos(ang), jnp.ones((n, rest), F32)], axis=1)
    sin = jnp.concatenate([-jnp.sin(ang), jnp.sin(ang), jnp.zeros((n, rest), F32)], axis=1)
    return jnp.tile(cos, (1, PAIR)), jnp.tile(sin, (1, PAIR))


def _block_diag(t):
    g, a, b = t.shape
    eye = jnp.eye(g, dtype=t.dtype)
    return (t[:, :, None, :] * eye[:, None, :, None]).reshape(g * a, g * b)


def _half_block_diag(t):
    per = t.shape[0] // S5_HALVES
    return _bf(jnp.stack([_block_diag(t[h * per:(h + 1) * per]) for h in range(S5_HALVES)]))


def _sink_rows(sinks):
    return jnp.broadcast_to(sinks.astype(F32)[:, None], (sinks.shape[0], LANES))


TILES = dict(
    even_prompt_rows=512,
    mem_prompt_rows=1024,
    sample_rows=512,
    proj_cols=640,
    even_sample_seqs=16,
    swa_sample_seqs=8,
    mem_sample_seqs=4,
    s5_groups_per_pass=8,
    s5_unroll=4,
)
def _trunk(x3, pos0, states, mem_k, mem_v, w, *, sample):
    b, l, d = x3.shape
    t = b * l
    x = x3.reshape(t, d)
    pos = pos0 + jnp.arange(l, dtype=F32)
    tm = TILES['sample_rows']
    gla_s, ret_s, s5_re, s5_im, swa_k, swa_v = states
    out_states = {k: [] for k in ("gla", "ret", "s5_re", "s5_im", "swa_k", "swa_v")}
    ld_row = jnp.repeat(jnp.log(1.0 - 2.0 ** (-5.0 - jnp.arange(RET_HEADS, dtype=F32))), RET_DK)[None, :]

    for layer in range(2):
        i = layer // 2
        x = _ffn(x, w['ffn1_norm'][layer][None], w['ffn1_w_gate'], w['ffn1_w_up'], w['ffn1_w_down'],
                 layer=layer)
        if layer % 2 == 0:
            cos, sin = _retention_tables(pos)
            args = (ld_row, w['gla_w_gate'][i], w['gla_b_gate'][i], w['gla_out_norm'][i])
            if sample:
                bg = TILES['even_sample_seqs']
                proj = _norm_matmul(x, w['mix_norm'][layer][None], w['even_w_in'][i], tm=tm, tn=TILES['proj_cols'])
                mixed, g_s, r_s = _even_sample(proj, jnp.tile(cos, (bg, 1)), jnp.tile(sin, (bg, 1)), *args,
                                               gla_s[i], ret_s[i], bg=bg, ls=l)
                x = _matmul_residual(x, [(mixed, w['even_w_out'][i])], tm=tm)
            else:
                x, g_s, r_s = _even_prompt(x.reshape(b, l, d), w['mix_norm'][layer][None], w['even_w_in'][i],
                                           w['even_w_out'][i], cos, sin, *args, gla_s[i], ret_s[i], lt=TILES['even_prompt_rows'])
                x = x.reshape(t, d)
            out_states["gla"].append(g_s)
            out_states["ret"].append(r_s)
        else:
            cq, sq = _rope_tables(pos)
            qg = jnp.tile(w['swa_q_norm'][i], SWA_HEADS)[None, :]
            kg = jnp.tile(w['swa_k_norm'][i], SWA_KV_HEADS)[None, :]
            s5_args = (w['s5_ab_re'][i], w['s5_ab_im'][i], w['s5_bb_re'][i], w['s5_bb_im'][i],
                       w['s5_cc_re'][i], w['s5_cc_im'][i], w['s5_d'][i][None], w['s5_w_glu'][i], w['s5_b_glu'][i][None])
            n_state = S5_GROUPS * S5_STATE
            s0_re, s0_im = s5_re[i].reshape(b, n_state), s5_im[i].reshape(b, n_state)
            sinks = _sink_rows(w['swa_sinks'][i])
            if sample:
                proj = _norm_matmul(x, w['mix_norm'][layer][None], w['odd_w_in'][i], tm=tm, tn=TILES['proj_cols'])
                c_out, sr, si = _s5(proj.reshape(1, t, OD_COLS), s0_re, s0_im, *s5_args, n_seq=b, lt=l,
                                    groups_per_pass=1, unroll=True)
                bg = TILES['swa_sample_seqs']
                d_out, kb, vb = _swa_sample(proj, swa_k[i].reshape(b, SWA_WINDOW, LANES),
                                            swa_v[i].reshape(b, SWA_WINDOW, LANES),
                                            jnp.tile(cq, (bg, 1)), jnp.tile(sq, (bg, 1)), qg, kg, sinks, bg=bg, ls=l)
                w_out = w['odd_w_out'][i]
                x = _matmul_residual(x, [(c_out.reshape(t, S5_WIDTH), w_out[:S5_WIDTH]), (d_out, w_out[S5_WIDTH:])],
                                     tm=tm)
            else:
                x, sr, si, kb, vb = _odd_prompt(x.reshape(b, l, d), w['mix_norm'][layer][None], w['odd_w_in'][i],
                                                w['odd_w_out'][i], s0_re, s0_im, *s5_args, cq, sq, qg, kg, sinks,
                                                groups_per_pass=TILES['s5_groups_per_pass'], unroll=TILES['s5_unroll'])
                x = x.reshape(t, d)
            out_states["s5_re"].append(sr.reshape(b, S5_GROUPS, S5_STATE))
            out_states["s5_im"].append(si.reshape(b, S5_GROUPS, S5_STATE))
            out_states["swa_k"].append(kb.reshape(b, -1, SWA_KV_HEADS, SWA_HD))
            out_states["swa_v"].append(vb.reshape(b, -1, SWA_KV_HEADS, SWA_HD))
        if sample:
            q = _norm_matmul(x, w['mem_x_norm'][layer][None], w['mem_w_q'][layer], tm=tm, tn=MEM_HD,
                             head_gain=w['mem_q_norm'][layer][None], n_norm_tiles=MEM_HEADS)
            o = _mem_sample(q, mem_k, mem_v, layer=layer, bs=TILES['mem_sample_seqs'], ls=l)
            x = _matmul_residual(x, [(o, w['mem_w_o'][layer])], tm=tm)
        else:
            x = _mem_prompt(x.reshape(b, l, d), w['mem_x_norm'][layer][None], w['mem_w_q'][layer],
                            w['mem_q_norm'][layer][None], mem_k, mem_v, w['mem_w_o'][layer],
                            layer=layer, lt=TILES['mem_prompt_rows']).reshape(t, d)
        x = _ffn(x, w['ffn2_norm'][layer][None], w['ffn2_w_gate'], w['ffn2_w_up'], w['ffn2_w_down'],
                 layer=layer)
    return x.reshape(b, l, d), {k: jnp.stack(v) for k, v in out_states.items()}


def kernel(x_prompt, x_sample, mem_prompt, state_gla, state_ret, state_s5_re, state_s5_im, cache_swa_k, cache_swa_v, cache_mem_k, cache_mem_v, ffn1_norm, ffn1_w_gate, ffn1_w_up, ffn1_w_down, ffn2_norm, ffn2_w_gate, ffn2_w_up, ffn2_w_down, mix_norm, even_w_in, gla_w_gate, gla_b_gate, gla_out_norm, even_w_out, odd_w_in, s5_a_re, s5_a_im, s5_log_step, s5_b_re, s5_b_im, s5_c_re, s5_c_im, s5_d, s5_w_glu, s5_b_glu, swa_q_norm, swa_k_norm, swa_sinks, odd_w_out, mem_x_norm, mem_m_norm, mem_w_q, mem_w_k, mem_w_v, mem_w_o, mem_q_norm, mem_k_norm):
    depth = ffn1_norm.shape[0]
    n_even, n_odd = even_w_in.shape[0], odd_w_in.shape[0]
    batch, seq, d = x_prompt.shape
    dec_batch = x_sample.shape[0]
    n_mem = mem_prompt.shape[1]

    ev = even_w_in
    ev_cols = jnp.concatenate(
        [ev[..., 0:1536], ev[..., 1552:3088], ev[..., 1536:1552],
         jnp.zeros(ev.shape[:2] + (EV_COLS - 3088,), ev.dtype)], axis=-1)
    wgate_pad = jnp.concatenate(
        [gla_w_gate, jnp.zeros((n_even, EV_COLS - EV_GA - GLA_RANK, gla_w_gate.shape[-1]), gla_w_gate.dtype)], axis=1)
    w = dict(
        ffn1_norm=ffn1_norm, ffn2_norm=ffn2_norm, mix_norm=mix_norm, mem_x_norm=mem_x_norm,
        ffn1_w_gate=_bf(ffn1_w_gate), ffn1_w_up=_bf(ffn1_w_up), ffn1_w_down=_bf(ffn1_w_down),
        ffn2_w_gate=_bf(ffn2_w_gate), ffn2_w_up=_bf(ffn2_w_up), ffn2_w_down=_bf(ffn2_w_down),
        even_w_in=_bf(ev_cols), gla_w_gate=_bf(wgate_pad), gla_b_gate=gla_b_gate[:, None, :],
        gla_out_norm=gla_out_norm[:, None, :], even_w_out=_bf(even_w_out),
        odd_w_in=_bf(odd_w_in), odd_w_out=_bf(odd_w_out), s5_d=s5_d, s5_w_glu=_bf(s5_w_glu), s5_b_glu=s5_b_glu,
        swa_q_norm=swa_q_norm, swa_k_norm=swa_k_norm, swa_sinks=swa_sinks,
        mem_w_q=_bf(mem_w_q), mem_w_o=_bf(mem_w_o), mem_q_norm=mem_q_norm,
    )
    ab_re, ab_im, bb_re, bb_im = [], [], [], []
    for i in range(n_odd):
        a_r, a_i, b_r, b_i = _s5_prep(s5_a_re[i], s5_a_im[i], s5_log_step[i], s5_b_re[i], s5_b_im[i])
        ab_re.append(a_r.reshape(-1))
        ab_im.append(a_i.reshape(-1))
        bb_re.append(_half_block_diag(b_r))
        bb_im.append(_half_block_diag(b_i))
    w.update(s5_ab_re=ab_re, s5_ab_im=ab_im, s5_bb_re=bb_re, s5_bb_im=bb_im,
             s5_cc_re=[_half_block_diag(jnp.swapaxes(s5_c_re[i], 1, 2)) for i in range(n_odd)],
             s5_cc_im=[_half_block_diag(jnp.swapaxes(s5_c_im[i], 1, 2)) for i in range(n_odd)])

    p_mem_k, p_mem_v = _memkv(mem_prompt, mem_m_norm[:, None, :], _bf(mem_w_k), _bf(mem_w_v), mem_k_norm[:, None, :])

    zeros = lambda *s: jnp.zeros(s, F32)
    p_states = (zeros(n_even, batch, GLA_HEADS, GLA_DK, GLA_DV), zeros(n_even, batch, RET_HEADS, RET_DK, GLA_DV),
                zeros(n_odd, batch, S5_GROUPS, S5_STATE), zeros(n_odd, batch, S5_GROUPS, S5_STATE), None, None)
    y_prompt, ps = _trunk(x_prompt, 0.0, p_states, p_mem_k, p_mem_v, w, sample=False)

    s_states = (state_gla, state_ret, state_s5_re, state_s5_im, cache_swa_k, cache_swa_v)
    y_sample, ss = _trunk(x_sample, float(PAST_LEN), s_states, cache_mem_k, cache_mem_v, w, sample=True)

    return (y_prompt, y_sample, ps["gla"], ps["ret"], ps["s5_re"], ps["s5_im"], ps["swa_k"], ps["swa_v"],
            p_mem_k, p_mem_v, ss["gla"], ss["ret"], ss["s5_re"], ss["s5_im"], ss["swa_k"], ss["swa_v"])
```

```python
import functools
import math

import jax
import jax.numpy as jnp
from jax import lax
from jax.experimental import pallas as pl
from jax.experimental.pallas import tpu as pltpu

F32 = jnp.float32
BF16 = jnp.bfloat16
NORM_EPS = 1e-6

D_MODEL = 1024
GLA_HEADS = 4
GLA_DK = 64
GLA_DV = 128
GLA_RANK = 16
GLA_TAU = 16.0
RET_HEADS = 4
RET_DK = 64
RET_THETA = 10000.0
LA_CHUNK = 64
S5_WIDTH = 512
S5_GROUP = 16
S5_GROUPS = 32
S5_STATE = 64
SWA_HD = 64
SWA_HEADS = 8
SWA_KV_HEADS = 2
SWA_WINDOW = 128
ROPE_THETA = 500000.0
MEM_HEADS = 4
MEM_HD = 256
PAST_LEN = 8192

VMEM_LIMIT_BYTES = 52 * 1024 * 1024
LANES = 128
SUBLANES = 8


def _params(*sem):
    return pltpu.CompilerParams(dimension_semantics=sem, vmem_limit_bytes=VMEM_LIMIT_BYTES)


def _rms(x, gain=None):
    y = x * lax.rsqrt(jnp.mean(x * x, axis=-1, keepdims=True) + NORM_EPS)
    return y if gain is None else y * gain


def _dot(a, b):
    return jnp.dot(a, b, preferred_element_type=F32)


def _dot_nt(a, b):
    return lax.dot_general(a, b, (((1,), (1,)), ((), ())), preferred_element_type=F32)


def _dot_tn(a, b):
    return lax.dot_general(a, b, (((0,), (0,)), ((), ())), preferred_element_type=F32)


def _split_bf16(x, terms):
    pieces = []
    for _ in range(terms):
        piece = _bf(x)
        pieces.append(piece)
        x = x - piece.astype(F32)
    return pieces


def _dot_exact_lhs(a, x, terms=3):
    return sum(_dot(a, piece) for piece in _split_bf16(x, terms))


def _dot_exact_rhs(x, b, terms=3):
    return sum(_dot(piece, b) for piece in _split_bf16(x, terms))


def _dot_nt_exact_lhs(a, x, terms=3):
    return sum(_dot_nt(a, piece) for piece in _split_bf16(x, terms))


def _bf(x):
    return x.astype(BF16)


def _log_sigmoid(x):
    return jnp.minimum(x, 0.0) - jnp.log(1.0 + jnp.exp(-jnp.abs(x)))


def _iota(shape, dim):
    return lax.broadcasted_iota(jnp.int32, shape, dim)


def _lane_tile(x, n):
    return jnp.concatenate([x] * n, axis=1)


def _swap_pairs(x):
    n = x.shape[-1]
    even = (_iota(x.shape, 1) & 1) == 0
    return jnp.where(even, pltpu.roll(x, n - 1, 1), pltpu.roll(x, 1, 1))


def _rope_partner(x, head_dim, half):
    n = x.shape[-1]
    first = (_iota(x.shape, 1) & (head_dim - 1)) < half
    return jnp.where(first, pltpu.roll(x, n - half, 1), pltpu.roll(x, half, 1))


FFN_ROW_TILE = 512


def _ffn_kernel(x_ref, g_ref, wg_ref, wu_ref, wd_ref, o_ref):
    x = x_ref[...]
    xn = _bf(_rms(x, g_ref[...]))
    gate = _dot(xn, wg_ref[...])
    up = _dot(xn, wu_ref[...])
    o_ref[...] = x + _dot(_bf(jax.nn.silu(gate) * up * 0.5), wd_ref[...])


def _ffn(x, gain, wg, wu, wd, *, layer):
    t, d = x.shape
    h = wg.shape[2]
    tm = FFN_ROW_TILE
    resident = dict(pipeline_mode=pl.Buffered(1))
    return pl.pallas_call(
        _ffn_kernel,
        grid=(t // tm,),
        in_specs=[
            pl.BlockSpec((tm, d), lambda i: (i, 0)),
            pl.BlockSpec((1, d), lambda i: (0, 0)),
            pl.BlockSpec((None, d, h), lambda i: (layer, 0, 0), **resident),
            pl.BlockSpec((None, d, h), lambda i: (layer, 0, 0), **resident),
            pl.BlockSpec((None, h, d), lambda i: (layer, 0, 0), **resident),
        ],
        out_specs=pl.BlockSpec((tm, d), lambda i: (i, 0)),
        out_shape=jax.ShapeDtypeStruct((t, d), F32),
        compiler_params=_params("parallel"),
        name="ffn",
    )(x, gain, wg, wu, wd)


def _nmm_kernel(x_ref, g_ref, w_ref, hg_ref, o_ref, xn_ref, *, n_norm_tiles):
    j = pl.program_id(1)

    @pl.when(j == 0)
    def _():
        xn_ref[...] = _bf(_rms(x_ref[...], g_ref[...]))

    y = _dot(xn_ref[...], w_ref[...])
    if n_norm_tiles == 0:
        o_ref[...] = y
    else:
        @pl.when(j < n_norm_tiles)
        def _():
            o_ref[...] = _rms(y, hg_ref[...])

        @pl.when(j >= n_norm_tiles)
        def _():
            o_ref[...] = y


def _norm_matmul(x, gain, w, *, tm, tn, head_gain=None, n_norm_tiles=0):
    t, d = x.shape
    n = w.shape[1]
    if head_gain is None:
        head_gain = jnp.ones((1, tn), F32)
    return pl.pallas_call(
        functools.partial(_nmm_kernel, n_norm_tiles=n_norm_tiles),
        grid=(t // tm, n // tn),
        in_specs=[
            pl.BlockSpec((tm, d), lambda i, j: (i, 0)),
            pl.BlockSpec((1, d), lambda i, j: (0, 0)),
            pl.BlockSpec((d, tn), lambda i, j: (0, j)),
            pl.BlockSpec((1, tn), lambda i, j: (0, 0)),
        ],
        out_specs=pl.BlockSpec((tm, tn), lambda i, j: (i, j)),
        out_shape=jax.ShapeDtypeStruct((t, n), F32),
        scratch_shapes=[pltpu.VMEM((tm, d), BF16)],
        compiler_params=_params("parallel", "arbitrary"),
        name="norm_matmul",
    )(x, gain, w, head_gain)


def _mmr_kernel(*refs, n_terms):
    x_ref = refs[0]
    a_refs = refs[1:1 + n_terms]
    w_refs = refs[1 + n_terms:1 + 2 * n_terms]
    o_ref = refs[1 + 2 * n_terms]
    acc = x_ref[...]
    for a_ref, w_ref in zip(a_refs, w_refs):
        acc = acc + _dot(_bf(a_ref[...]), w_ref[...])
    o_ref[...] = acc


def _matmul_residual(x, terms, *, tm):
    t, d = x.shape
    acts = [a for a, _ in terms]
    ws = [w for _, w in terms]
    in_specs = [pl.BlockSpec((tm, d), lambda i: (i, 0))]
    in_specs += [pl.BlockSpec((tm, a.shape[1]), lambda i: (i, 0)) for a in acts]
    in_specs += [pl.BlockSpec(w.shape, lambda i: (0, 0)) for w in ws]
    return pl.pallas_call(
        functools.partial(_mmr_kernel, n_terms=len(terms)),
        grid=(t // tm,),
        in_specs=in_specs,
        out_specs=pl.BlockSpec((tm, d), lambda i: (i, 0)),
        out_shape=jax.ShapeDtypeStruct((t, d), F32),
        compiler_params=_params("parallel"),
        name="matmul_residual",
    )(x, *acts, *ws)


EV_GQ, EV_GK, EV_GV, EV_GG = 0, 256, 512, 1024
EV_RQ, EV_RK, EV_RV, EV_RG = 1536, 1792, 2048, 2560
EV_GA = 3072
EV_COLS = 3200
EV_BLOCK = 256
PAIR = 2


def _gate_and_norm(o, gate, gain=None):
    return _rms(o, gain) * jax.nn.silu(gate)


def _even_prompt_kernel(x_ref, xnext_ref, gain_ref, win_ref, wout_ref, cos_ref, sin_ref, ld_ref, wgate_ref, bgate_ref,
                        gnorm_ref, s0g_ref, s0r_ref, o_ref, sg_ref, sr_ref, proj_a, proj_b, mix_sc, o_sc,
                        *, chunk, n_chunks, tiles_per_seq):
    n = pl.program_id(0)

    @pl.when(n % tiles_per_seq == 0)
    def _():
        sg_ref[...] = s0g_ref[...]
        sr_ref[...] = s0r_ref[...]

    @pl.when(n == 0)
    def _():
        proj_a[...] = _dot(_bf(_rms(x_ref[...], gain_ref[...])), win_ref[...])

    args = (x_ref, xnext_ref, gain_ref, win_ref, wout_ref, cos_ref, sin_ref, ld_ref, wgate_ref, bgate_ref, gnorm_ref,
            o_ref, sg_ref, sr_ref)

    @pl.when(n % 2 == 0)
    def _():
        _even_prompt_tile(*args, proj_a, proj_b, mix_sc, o_sc, chunk=chunk, n_chunks=n_chunks)

    @pl.when(n % 2 == 1)
    def _():
        _even_prompt_tile(*args, proj_b, proj_a, mix_sc, o_sc, chunk=chunk, n_chunks=n_chunks)


def _even_prompt_tile(x_ref, xnext_ref, gain_ref, win_ref, wout_ref, cos_ref, sin_ref, ld_ref, wgate_ref, bgate_ref,
                      gnorm_ref, o_ref, sg_ref, sr_ref, proj_sc, proj_next, mix_sc, o_sc, *, chunk, n_chunks):
    c = chunk
    lt = c * n_chunks
    blk = min(EV_BLOCK, lt)
    c_shift = c.bit_length() - 1
    x = x_ref[...]
    xn_next = _bf(_rms(xnext_ref[...], gain_ref[...]))
    piece_cols = [(a, min(a + 256, EV_COLS)) for a in range(0, EV_COLS, 256)]
    n_serial = (GLA_HEADS + RET_HEADS) // PAIR * n_chunks

    def issue_next_projection(step_index):
        for i, (a, b) in enumerate(piece_cols):
            if (i * n_serial) // len(piece_cols) == step_index:
                proj_next[:, a:b] = _dot(xn_next, win_ref[:, a:b])

    def cols(a, b):
        return proj_sc[:, a:b]

    log_a = _log_sigmoid(_dot(_bf(cols(EV_GA, EV_COLS)), wgate_ref[...]) + bgate_ref[...]) * (1.0 / GLA_TAU)
    tril = _bf((_iota((c, c), 1) <= _iota((c, c), 0)).astype(F32))
    cum_parts = [_dot_exact_lhs(tril, log_a[i * c:(i + 1) * c]) for i in range(n_chunks)]
    tots = [p[c - 1:c] for p in cum_parts]
    cum = jnp.concatenate(cum_parts, axis=0)
    tot_b = jnp.concatenate([jnp.broadcast_to(t, (c, 256)) for t in tots], axis=0)
    k = cols(EV_GK, EV_GK + 256)
    gla = (cols(EV_GQ, EV_GQ + 256) * (GLA_DK ** -0.5) * jnp.exp(cum), k * jnp.exp(-cum), k * jnp.exp(tot_b - cum))

    ld = ld_ref[...]
    tpos = ((_iota((lt, 1), 0) & (c - 1)) + 1).astype(F32)
    cum_r = tpos * ld
    tot_r = float(c) * ld
    cos, sin = _lane_tile(cos_ref[...], PAIR), _lane_tile(sin_ref[...], PAIR)
    rq = cols(EV_RQ, EV_RQ + 256)
    rk = cols(EV_RK, EV_RK + 256)
    q_rot = rq * cos + _swap_pairs(rq) * sin
    k_rot = (rk * cos + _swap_pairs(rk) * sin) * (RET_DK ** -0.5)
    ret = (q_rot * jnp.exp(cum_r), k_rot * jnp.exp(-cum_r), k_rot * jnp.exp(tot_r - cum_r))

    tot_rows = jnp.concatenate(tots + [tot_r, jnp.zeros((LANES - n_chunks - 1, 256), F32)], axis=0)
    decay_cols = jnp.exp(jnp.transpose(tot_rows))

    row = _iota((blk, blk), 0)
    col = _iota((blk, blk), 1)
    blk_mask = ((row >> c_shift) == (col >> c_shift)) & (col <= row)
    lo = (_iota((lt, LANES), 1) < GLA_DK)
    gnorm = gnorm_ref[...]

    mixers = ((gla, EV_GV, EV_GG, sg_ref, 0, gnorm, lambda i: i),
              (ret, EV_RV, EV_RG, sr_ref, 512, None, lambda i: n_chunks))
    n_pairs = GLA_HEADS // PAIR
    groups = [(m, p) for m in range(len(mixers)) for p in range(n_pairs)]

    def prepare(m, p):
        (q_dec, k_inv, k_dec), v_col = mixers[m][0], mixers[m][1]
        lanes = slice(p * LANES, (p + 1) * LANES)
        q_pair = q_dec[:, lanes]
        q_masked = [_bf(jnp.where(lo, q_pair, 0.0)), _bf(jnp.where(lo, 0.0, q_pair))]
        v_pair = _bf(cols(v_col + p * PAIR * GLA_DV, v_col + (p + 1) * PAIR * GLA_DV))
        return q_masked, _bf(k_inv[:, lanes]), _bf(k_dec[:, lanes]), v_pair

    def intra_units(m, p, prepared):
        q_masked, ki, _, v_pair = prepared
        units = []
        for e in range(PAIR):
            slot = (m * n_pairs + p) * PAIR + e
            for r0 in range(0, lt, blk):
                def unit(e=e, slot=slot, rs=slice(r0, r0 + blk)):
                    scores = jnp.where(blk_mask, _dot_nt(q_masked[e][rs], ki[rs]), 0.0)
                    o_sc[slot, rs, :] = _dot(_bf(scores), v_pair[rs, e * GLA_DV:(e + 1) * GLA_DV])
                units.append(unit)
        return units

    prepared = prepare(*groups[0])
    for unit in intra_units(*groups[0], prepared):
        unit()
    for gi, (m, p) in enumerate(groups):
        _, v_col, g_col, s_ref, out_col, gain, decay_col_of = mixers[m]
        q_masked, _, kd, v_pair = prepared
        if gi + 1 < len(groups):
            next_prepared = prepare(*groups[gi + 1])
            pending = intra_units(*groups[gi + 1], next_prepared)
        else:
            next_prepared, pending = None, []
        stride = max(1, n_chunks // max(1, len(pending)))
        state = s_ref[0, p * PAIR:(p + 1) * PAIR].reshape(PAIR * GLA_DK, GLA_DV)
        for i in range(n_chunks):
            issue_next_projection(gi * n_chunks + i)
            if i % stride == 0 and i // stride < len(pending):
                pending[i // stride]()
            rs = slice(i * c, (i + 1) * c)
            q_stack = jnp.concatenate([q_masked[0][rs], q_masked[1][rs]], axis=0)
            o_inter = _dot(q_stack, _bf(state))
            kv = _dot_tn(kd[rs], v_pair[rs])
            kv = jnp.concatenate([kv[:GLA_DK, :GLA_DV], kv[GLA_DK:, GLA_DV:]], axis=0)
            ci = decay_col_of(i)
            state = state * decay_cols[p * LANES:(p + 1) * LANES, ci:ci + 1] + kv
            for e in range(PAIR):
                slot = (m * n_pairs + p) * PAIR + e
                o_sc[slot, rs, :] += o_inter[e * c:(e + 1) * c]
        for k in range(n_chunks // stride, len(pending)):
            pending[k]()
        s_ref[0, p * PAIR:(p + 1) * PAIR] = state.reshape(PAIR, GLA_DK, GLA_DV)
        for e in range(PAIR):
            h = p * PAIR + e
            slot = (m * n_pairs + p) * PAIR + e
            gate = cols(g_col + h * GLA_DV, g_col + (h + 1) * GLA_DV)
            mix_sc[:, out_col + h * GLA_DV:out_col + (h + 1) * GLA_DV] = _gate_and_norm(o_sc[slot], gate, gain)
        prepared = next_prepared
    o_ref[...] = x + _dot(_bf(mix_sc[...]), wout_ref[...])


def _even_prompt(x, gain, w_in, w_out, cos, sin, ld_row, wgate, bgate, gnorm, s0g, s0r, *, lt):
    b, l, d = x.shape
    chunk = math.gcd(l, LA_CHUNK)
    nj = l // lt
    n_tiles = b * nj
    x2 = x.reshape(b * l, d)
    st_spec = pl.BlockSpec((1, GLA_HEADS, GLA_DK, GLA_DV), lambda n: (n // nj, 0, 0, 0))
    io_spec = pl.BlockSpec((lt, d), lambda n: (n, 0))
    next_spec = pl.BlockSpec((lt, d), lambda n: (jnp.minimum(n + 1, n_tiles - 1), 0))
    tab_spec = pl.BlockSpec((lt, LANES), lambda n: (n % nj, 0))
    full = lambda a: pl.BlockSpec(a.shape, lambda n: (0,) * a.ndim)
    out, sg, sr = pl.pallas_call(
        functools.partial(_even_prompt_kernel, chunk=chunk, n_chunks=lt // chunk, tiles_per_seq=nj),
        grid=(n_tiles,),
        in_specs=[
            io_spec, next_spec, full(gain), full(w_in), full(w_out), tab_spec, tab_spec,
            full(ld_row), full(wgate), full(bgate), full(gnorm), st_spec, st_spec,
        ],
        out_specs=[io_spec, st_spec, st_spec],
        out_shape=[jax.ShapeDtypeStruct(x2.shape, F32),
                   jax.ShapeDtypeStruct(s0g.shape, F32), jax.ShapeDtypeStruct(s0r.shape, F32)],
        scratch_shapes=[pltpu.VMEM((lt, EV_COLS), F32), pltpu.VMEM((lt, EV_COLS), F32), pltpu.VMEM((lt, d), F32),
                        pltpu.VMEM((GLA_HEADS + RET_HEADS, lt, GLA_DV), F32)],
        compiler_params=_params("arbitrary"),
        name="even_prompt",
    )(x2, x2, gain, w_in, w_out, cos, sin, ld_row, wgate, bgate, gnorm, s0g, s0r)
    return out.reshape(b, l, d), sg, sr


def _even_sample_kernel(proj_ref, cos_ref, sin_ref, ld_ref, wgate_ref, bgate_ref, gnorm_ref, s0g_ref, s0r_ref,
                        mix_ref, sg_ref, sr_ref, *, bg, ls):
    r = bg * ls
    ls_shift = ls.bit_length() - 1
    dk_shift = GLA_DK.bit_length() - 1
    n_exp = bg * GLA_DK
    row_seq = _iota((r, r), 0) >> ls_shift
    col_seq = _iota((r, r), 1) >> ls_shift
    same = row_seq == col_seq
    seg = same & (_iota((r, r), 1) <= _iota((r, r), 0))
    seg_b = _bf(seg.astype(F32))
    same_b = _bf(same.astype(F32))
    tile_b = _bf(((_iota((GLA_DK, n_exp), 1) & (GLA_DK - 1)) == _iota((GLA_DK, n_exp), 0)).astype(F32))
    tile_t_b = _bf(((_iota((n_exp, GLA_DK), 0) & (GLA_DK - 1)) == _iota((n_exp, GLA_DK), 1)).astype(F32))
    q_mask = (_iota((r, n_exp), 0) >> ls_shift) == (_iota((r, n_exp), 1) >> dk_shift)
    k_mask = (_iota((n_exp, r), 0) >> dk_shift) == (_iota((n_exp, r), 1) >> ls_shift)
    tpos = ((_iota((r, 1), 0) & (ls - 1)) + 1).astype(F32)
    ld = ld_ref[...]
    cum_r = tpos * ld
    tot_r = float(ls) * ld
    r_dec = jnp.exp(tot_r)
    gnorm = gnorm_ref[...]

    def mixer(q_dec, k_inv, k_dec, v_col, g_col, s0_ref, s_ref, out_col, decay_of, gain):
        for h in range(GLA_HEADS):
            sl = slice(h * GLA_DK, (h + 1) * GLA_DK)
            v = _bf(proj_ref[:, v_col + h * GLA_DV:v_col + (h + 1) * GLA_DV])
            qd = _bf(q_dec[:, sl])
            scores = jnp.where(seg, _dot_nt(qd, _bf(k_inv[:, sl])), 0.0)
            state = s0_ref[:, h].reshape(n_exp, GLA_DV)
            q_exp = _bf(jnp.where(q_mask, _dot(qd, tile_b), 0.0))
            o = _dot(_bf(scores), v) + _dot(q_exp, _bf(state))
            k_exp = _bf(jnp.where(k_mask, _dot_nt(tile_t_b, _bf(k_dec[:, sl])), 0.0))
            new_state = state * decay_of(h, sl) + _dot(k_exp, v)
            s_ref[:, h] = new_state.reshape(bg, GLA_DK, GLA_DV)
            gate = proj_ref[:, g_col + h * GLA_DV:g_col + (h + 1) * GLA_DV]
            mix_ref[:, out_col + h * GLA_DV:out_col + (h + 1) * GLA_DV] = _gate_and_norm(o, gate, gain)

    log_a = _log_sigmoid(_dot(_bf(proj_ref[:, EV_GA:EV_COLS]), wgate_ref[...]) + bgate_ref[...]) * (1.0 / GLA_TAU)
    cum = _dot_exact_lhs(seg_b, log_a)
    tot = _dot_exact_lhs(same_b, log_a)
    k = proj_ref[:, EV_GK:EV_GK + 256]

    def gla_decay(h, sl):
        la_exp = jnp.where(k_mask, _dot_nt_exact_lhs(tile_t_b, log_a[:, sl]), 0.0)
        return jnp.exp(jnp.sum(la_exp, axis=-1, keepdims=True))

    mixer(proj_ref[:, EV_GQ:EV_GQ + 256] * (GLA_DK ** -0.5) * jnp.exp(cum), k * jnp.exp(-cum), k * jnp.exp(tot - cum),
          EV_GV, EV_GG, s0g_ref, sg_ref, 0, gla_decay, gnorm)

    cos = _lane_tile(cos_ref[...], PAIR)
    sin = _lane_tile(sin_ref[...], PAIR)
    rq = proj_ref[:, EV_RQ:EV_RQ + 256]
    rk = proj_ref[:, EV_RK:EV_RK + 256]
    q_rot = rq * cos + _swap_pairs(rq) * sin
    k_rot = (rk * cos + _swap_pairs(rk) * sin) * (RET_DK ** -0.5)

    def ret_decay(h, sl):
        return r_dec[:, h * RET_DK:h * RET_DK + 1]

    mixer(q_rot * jnp.exp(cum_r), k_rot * jnp.exp(-cum_r), k_rot * jnp.exp(tot_r - cum_r),
          EV_RV, EV_RG, s0r_ref, sr_ref, 512, ret_decay, None)


def _even_sample(proj, cos, sin, ld_row, wgate, bgate, gnorm, s0g, s0r, *, bg, ls):
    t = proj.shape[0]
    n_b = t // ls
    r = bg * ls
    st_spec = pl.BlockSpec((bg, GLA_HEADS, GLA_DK, GLA_DV), lambda i: (i, 0, 0, 0))
    full = lambda a: pl.BlockSpec(a.shape, lambda i: (0,) * a.ndim)
    return pl.pallas_call(
        functools.partial(_even_sample_kernel, bg=bg, ls=ls),
        grid=(n_b // bg,),
        in_specs=[
            pl.BlockSpec((r, EV_COLS), lambda i: (i, 0)),
            full(cos), full(sin), full(ld_row), full(wgate), full(bgate), full(gnorm), st_spec, st_spec,
        ],
        out_specs=[pl.BlockSpec((r, D_MODEL), lambda i: (i, 0)), st_spec, st_spec],
        out_shape=[jax.ShapeDtypeStruct((t, D_MODEL), F32),
                   jax.ShapeDtypeStruct(s0g.shape, F32), jax.ShapeDtypeStruct(s0r.shape, F32)],
        compiler_params=_params("parallel"),
        name="even_sample",
    )(proj, cos, sin, ld_row, wgate, bgate, gnorm, s0g, s0r)


def _s5_prep_kernel(are_ref, aim_ref, lstep_ref, bre_ref, bim_ref, abre_ref, abim_ref, bbre_ref, bbim_ref):
    a_re, a_im = are_ref[...], aim_ref[...]
    step = jnp.exp(lstep_ref[...])
    mag = jnp.exp(a_re * step)
    ab_re = mag * jnp.cos(a_im * step)
    ab_im = mag * jnp.sin(a_im * step)
    den = a_re * a_re + a_im * a_im
    coef_re = ((ab_re - 1.0) * a_re + ab_im * a_im) / den
    coef_im = (ab_im * a_re - (ab_re - 1.0) * a_im) / den
    b_re, b_im = bre_ref[...], bim_ref[...]
    abre_ref[...] = ab_re
    abim_ref[...] = ab_im
    bbre_ref[...] = coef_re * b_re - coef_im * b_im
    bbim_ref[...] = coef_re * b_im + coef_im * b_re


def _s5_prep(a_re, a_im, log_step, b_re, b_im):
    g, n = a_re.shape
    shp3 = jax.ShapeDtypeStruct((g, 1, n), F32)
    shpb = jax.ShapeDtypeStruct((g, S5_GROUP, n), F32)
    return pl.pallas_call(_s5_prep_kernel, out_shape=[shp3, shp3, shpb, shpb], name="s5_prep")(
        a_re.reshape(g, 1, n), a_im.reshape(g, 1, n), log_step.reshape(g, 1, 1),
        jnp.swapaxes(b_re, 1, 2), jnp.swapaxes(b_im, 1, 2))


S5_LANE_CHUNKS = S5_GROUPS * S5_STATE // LANES
S5_HALVES = 2


def _s5_layout(n_seq, lt):
    pack = max(1, SUBLANES // n_seq)
    pitch = lt + 4 if lt % SUBLANES == 0 else lt
    return pack, S5_LANE_CHUNKS // pack, pitch


def _s5_slot(c, n_groups):
    return c % n_groups, c // n_groups


def _s5_load_state(s0re_ref, s0im_ref, hre_ref, him_ref, n_seq, n_groups):
    for c in range(S5_LANE_CHUNKS):
        g, j = _s5_slot(c, n_groups)
        hre_ref[g, j * n_seq:(j + 1) * n_seq, :] = s0re_ref[:, c * LANES:(c + 1) * LANES]
        him_ref[g, j * n_seq:(j + 1) * n_seq, :] = s0im_ref[:, c * LANES:(c + 1) * LANES]


def _s5_store_state(sre_ref, sim_ref, hre_ref, him_ref, n_seq, n_groups):
    for c in range(S5_LANE_CHUNKS):
        g, j = _s5_slot(c, n_groups)
        sre_ref[:, c * LANES:(c + 1) * LANES] = hre_ref[g, j * n_seq:(j + 1) * n_seq, :]
        sim_ref[:, c * LANES:(c + 1) * LANES] = him_ref[g, j * n_seq:(j + 1) * n_seq, :]


def _s5_core(u, abre_ref, abim_ref, bbre_ref, bbim_ref, ccre_ref, ccim_ref, d_ref, wglu_ref, bglu_ref,
             xr_ref, xi_ref, hre_ref, him_ref, *, n_seq, lt, groups_per_pass, unroll):
    pack, n_groups, pitch = _s5_layout(n_seq, lt)
    rows = n_seq * lt
    per_half = S5_LANE_CHUNKS // S5_HALVES
    lanes_of = lambda c: slice(c * LANES, (c + 1) * LANES)
    slot_of = lambda c: _s5_slot(c, n_groups)

    def seq_rows(j, s):
        return slice((j * n_seq + s) * pitch, (j * n_seq + s) * pitch + lt)

    ub = _bf(u)
    u_cols = S5_WIDTH // S5_HALVES
    for half in range(S5_HALVES):
        uh = ub[:, half * u_cols:(half + 1) * u_cols]
        for x_ref, bb_ref in ((xr_ref, bbre_ref), (xi_ref, bbim_ref)):
            x = _dot(uh, bb_ref[half])
            for k in range(per_half):
                g, j = slot_of(half * per_half + k)
                if pitch == lt:
                    x_ref[g, j * rows:(j + 1) * rows, :] = x[:, lanes_of(k)]
                else:
                    for s in range(n_seq):
                        x_ref[g, seq_rows(j, s), :] = x[s * lt:(s + 1) * lt, lanes_of(k)]

    for g0 in range(0, n_groups, groups_per_pass):
        gs = list(range(g0, g0 + groups_per_pass))
        init = tuple(hre_ref[g] for g in gs) + tuple(him_ref[g] for g in gs)

        def step(t, carry, gs=gs):
            rws = pl.ds(t, pack * n_seq, stride=pitch)
            new_re, new_im = [], []
            for k, g in enumerate(gs):
                a_re, a_im = abre_ref[g], abim_ref[g]
                h_re, h_im = carry[k], carry[len(gs) + k]
                n_re = a_re * h_re - a_im * h_im + xr_ref[g, rws, :]
                n_im = a_re * h_im + a_im * h_re + xi_ref[g, rws, :]
                xr_ref[g, rws, :] = n_re
                xi_ref[g, rws, :] = n_im
                new_re.append(n_re)
                new_im.append(n_im)
            return tuple(new_re + new_im)

        fin = lax.fori_loop(0, lt, step, init, unroll=unroll)
        for k, g in enumerate(gs):
            hre_ref[g] = fin[k]
            him_ref[g] = fin[len(gs) + k]

    def gather(x_ref, half):
        cols = []
        for k in range(per_half):
            g, j = slot_of(half * per_half + k)
            if pitch == lt:
                cols.append(x_ref[g, j * rows:(j + 1) * rows, :])
            else:
                cols.append(jnp.concatenate([x_ref[g, seq_rows(j, s), :] for s in range(n_seq)], axis=0))
        return _bf(jnp.concatenate(cols, axis=1))

    y = jnp.concatenate([_dot(gather(xr_ref, half), ccre_ref[half]) - _dot(gather(xi_ref, half), ccim_ref[half])
                         for half in range(S5_HALVES)], axis=1) + d_ref[...] * u
    z = jax.nn.gelu(y, approximate=True)
    return z * jax.nn.sigmoid(_dot(_bf(z), wglu_ref[...]) + bglu_ref[...])


def _s5_kernel(u_ref, s0re_ref, s0im_ref, abre_ref, abim_ref, bbre_ref, bbim_ref, ccre_ref, ccim_ref,
               d_ref, wglu_ref, bglu_ref, out_ref, sre_ref, sim_ref, xr_ref, xi_ref, hre_ref, him_ref,
               *, n_seq, lt, groups_per_pass, unroll):
    n_groups = _s5_layout(n_seq, lt)[1]

    @pl.when(pl.program_id(0) == 0)
    def _():
        _s5_load_state(s0re_ref, s0im_ref, hre_ref, him_ref, n_seq, n_groups)

    out = _s5_core(u_ref[...].reshape(n_seq * lt, S5_WIDTH), abre_ref, abim_ref, bbre_ref, bbim_ref, ccre_ref, ccim_ref,
                   d_ref, wglu_ref, bglu_ref, xr_ref, xi_ref, hre_ref, him_ref,
                   n_seq=n_seq, lt=lt, groups_per_pass=groups_per_pass, unroll=unroll)
    out_ref[...] = out.reshape(out_ref.shape)

    @pl.when(pl.program_id(0) == pl.num_programs(0) - 1)
    def _():
        _s5_store_state(sre_ref, sim_ref, hre_ref, him_ref, n_seq, n_groups)


def _s5_tables(ab, n_seq, lt):
    pack, n_groups, _ = _s5_layout(n_seq, lt)
    tab = jnp.swapaxes(ab.reshape(pack, n_groups, 1, LANES), 0, 1)
    return jnp.broadcast_to(tab, (n_groups, pack, n_seq, LANES)).reshape(n_groups, pack * n_seq, LANES)


def _s5(proj3, s0_re, s0_im, ab_re, ab_im, bb_re, bb_im, cc_re, cc_im, d_row, wglu, bglu, *, n_seq, lt,
        groups_per_pass, unroll):
    nb, rows_b, _ = proj3.shape
    blk_rows = n_seq * lt // nb
    pack, n_groups, pitch = _s5_layout(n_seq, lt)
    ab_re, ab_im = _s5_tables(ab_re, n_seq, lt), _s5_tables(ab_im, n_seq, lt)
    full = lambda a: pl.BlockSpec(a.shape, lambda j: (0,) * a.ndim)
    io_spec = pl.BlockSpec((nb, blk_rows, S5_WIDTH), lambda j: (0, j, 0))
    x_scratch = pltpu.VMEM((n_groups, pack * n_seq * pitch, LANES), F32)
    h_scratch = pltpu.VMEM((n_groups, pack * n_seq, LANES), F32)
    return pl.pallas_call(
        functools.partial(_s5_kernel, n_seq=n_seq, lt=lt, groups_per_pass=groups_per_pass, unroll=unroll),
        grid=(rows_b // blk_rows,),
        in_specs=[io_spec, full(s0_re), full(s0_im), full(ab_re), full(ab_im), full(bb_re), full(bb_im),
                  full(cc_re), full(cc_im), full(d_row), full(wglu), full(bglu)],
        out_specs=[io_spec, full(s0_re), full(s0_im)],
        out_shape=[jax.ShapeDtypeStruct((nb, rows_b, S5_WIDTH), F32),
                   jax.ShapeDtypeStruct(s0_re.shape, F32), jax.ShapeDtypeStruct(s0_im.shape, F32)],
        scratch_shapes=[x_scratch, x_scratch, h_scratch, h_scratch],
        compiler_params=_params("arbitrary"),
        name="s5",
    )(proj3, s0_re, s0_im, ab_re, ab_im, bb_re, bb_im, cc_re, cc_im, d_row, wglu, bglu)


OD_Q_BLOCK = 1
OD_KV_BLOCK = 4
OD_COLS = 1280
ROPE_HALF = SWA_HD // 8
SWA_GROUP = SWA_HEADS // SWA_KV_HEADS


def _half_lanes(shape):
    return (_iota(shape, 1) & (LANES - 1)) < SWA_HD


def _pair_rms_scale(x):
    same_head = (_iota((LANES, LANES), 0) >= SWA_HD) == (_iota((LANES, LANES), 1) >= SWA_HD)
    sums = _dot_exact_rhs(x * x, _bf(same_head.astype(F32)), terms=2)
    return lax.rsqrt(sums * (1.0 / SWA_HD) + NORM_EPS)


def _swa_qk(xq, xk, qg, kg, cq, sq):
    xq_g = xq * qg
    n_pairs = SWA_HEADS // 2
    q_rot = xq_g * _lane_tile(cq, n_pairs) + _rope_partner(xq_g, SWA_HD, ROPE_HALF) * _lane_tile(sq, n_pairs)
    xk_g = xk * kg
    k_rot = xk_g * cq + _rope_partner(xk_g, SWA_HD, ROPE_HALF) * sq
    q_pairs = [q_rot[:, j * LANES:(j + 1) * LANES] * _pair_rms_scale(xq[:, j * LANES:(j + 1) * LANES])
               for j in range(SWA_HEADS // 2)]
    return q_pairs, k_rot * _pair_rms_scale(xk)


def _swa_query_stack(q_pairs, kh):
    lo = _half_lanes(q_pairs[0].shape)
    keep = lo if kh == 0 else jnp.logical_not(lo)
    parts = []
    for g in range(SWA_GROUP):
        hq = kh * SWA_GROUP + g
        pair = q_pairs[hq // 2]
        src = pair if hq % 2 == kh else pltpu.roll(pair, SWA_HD, 1)
        parts.append(jnp.where(keep, src, 0.0))
    return _bf(jnp.concatenate(parts, axis=0))


def _swa_merge_heads(o, kh, rows):
    lo = _half_lanes((rows, LANES))
    pairs = []
    for p in range(SWA_GROUP // 2):
        even, odd = o[2 * p * rows:(2 * p + 1) * rows], o[(2 * p + 1) * rows:(2 * p + 2) * rows]
        if kh == 0:
            pairs.append(jnp.where(lo, even, pltpu.roll(odd, SWA_HD, 1)))
        else:
            pairs.append(jnp.where(lo, pltpu.roll(even, SWA_HD, 1), odd))
    return pairs


def _swa_prepare(xq, xk, v_cur, k_prev, v_prev, qg, kg, cq, sq):
    q_pairs, k_cur = _swa_qk(xq, xk, qg, kg, cq, sq)
    k_ext = _bf(jnp.concatenate([k_prev, k_cur], axis=0))
    v_ext = _bf(jnp.concatenate([v_prev, v_cur], axis=0))
    return q_pairs, k_cur, k_ext, v_ext


def _swa_attend(prepared, mask, sink_ref, after_head=None):
    w = SWA_WINDOW
    q_pairs, _, k_ext, v_ext = prepared
    ones_col = jnp.ones((2 * w, LANES), BF16)
    out_pairs = []
    for kh in range(SWA_KV_HEADS):
        s_all = _dot_nt(_swa_query_stack(q_pairs, kh), k_ext)
        weights, sink_terms = [], []
        for g in range(SWA_GROUP):
            s = jnp.where(mask, s_all[g * w:(g + 1) * w] * (SWA_HD ** -0.5), -jnp.inf)
            sink = sink_ref[kh * SWA_GROUP + g:kh * SWA_GROUP + g + 1, :]
            m = jnp.maximum(jnp.broadcast_to(jnp.max(s, axis=-1, keepdims=True), (w, LANES)), sink)
            weights.append(_bf(jnp.exp(s - jnp.concatenate([m, m], axis=1))))
            sink_terms.append(jnp.exp(sink - m))
        weights = jnp.concatenate(weights, axis=0)
        den = _dot(weights, ones_col) + jnp.concatenate(sink_terms, axis=0)
        out_pairs += _swa_merge_heads(_dot(weights, v_ext) / den, kh, w)
        if after_head is not None:
            after_head(kh)
    return out_pairs


def _odd_prompt_kernel(x_ref, xnext_ref, gain_ref, win_ref, wout_ref, s0re_ref, s0im_ref, abre_ref, abim_ref, bbre_ref,
                       bbim_ref, ccre_ref, ccim_ref, d_ref, wglu_ref, bglu_ref, cq_ref, sq_ref, qg_ref, kg_ref, sink_ref,
                       o_ref, sre_ref, sim_ref, ck_ref, cv_ref,
                       proj_a, proj_b, mix_sc, xr_ref, xi_ref, hre_ref, him_ref, kprev_sc, vprev_sc,
                       *, n_seq, groups_per_pass, unroll):
    n_groups = _s5_layout(n_seq, SWA_WINDOW)[1]
    step = pl.program_id(0)

    @pl.when(step == 0)
    def _():
        _s5_load_state(s0re_ref, s0im_ref, hre_ref, him_ref, n_seq, n_groups)
        kprev_sc[...] = jnp.zeros_like(kprev_sc)
        vprev_sc[...] = jnp.zeros_like(vprev_sc)
        x0 = x_ref[...].reshape(n_seq * SWA_WINDOW, D_MODEL)
        proj_a[...] = _dot(_bf(_rms(x0, gain_ref[...])), win_ref[...])

    args = (x_ref, xnext_ref, gain_ref, win_ref, wout_ref, abre_ref, abim_ref, bbre_ref, bbim_ref, ccre_ref, ccim_ref,
            d_ref, wglu_ref, bglu_ref, cq_ref, sq_ref, qg_ref, kg_ref, sink_ref, o_ref)
    scratch = (mix_sc, xr_ref, xi_ref, hre_ref, him_ref, kprev_sc, vprev_sc)
    kw = dict(n_seq=n_seq, groups_per_pass=groups_per_pass, unroll=unroll)

    @pl.when(step % 2 == 0)
    def _():
        _odd_prompt_step(*args, proj_a, proj_b, *scratch, **kw)

    @pl.when(step % 2 == 1)
    def _():
        _odd_prompt_step(*args, proj_b, proj_a, *scratch, **kw)

    @pl.when(step == pl.num_programs(0) - 1)
    def _():
        _s5_store_state(sre_ref, sim_ref, hre_ref, him_ref, n_seq, n_groups)
        ck_ref[...] = kprev_sc[...]
        cv_ref[...] = vprev_sc[...]


def _odd_prompt_step(x_ref, xnext_ref, gain_ref, win_ref, wout_ref, abre_ref, abim_ref, bbre_ref, bbim_ref, ccre_ref,
                     ccim_ref, d_ref, wglu_ref, bglu_ref, cq_ref, sq_ref, qg_ref, kg_ref, sink_ref, o_ref,
                     proj_sc, proj_next, mix_sc, xr_ref, xi_ref, hre_ref, him_ref, kprev_sc, vprev_sc,
                     *, n_seq, groups_per_pass, unroll):
    w = SWA_WINDOW
    rows = n_seq * w
    x = x_ref[...].reshape(rows, D_MODEL)
    xn_next = _bf(_rms(xnext_ref[...].reshape(rows, D_MODEL), gain_ref[...]))
    piece_cols = [(a, min(a + 256, OD_COLS)) for a in range(0, OD_COLS, 256)]
    n_slots = n_seq * SWA_KV_HEADS

    def issue_next_projection(slot_index):
        for i, (a, b) in enumerate(piece_cols):
            if (i * n_slots) // len(piece_cols) == slot_index:
                proj_next[:, a:b] = _dot(xn_next, win_ref[:, a:b])

    mix_sc[:, 0:S5_WIDTH] = _s5_core(proj_sc[:, 0:S5_WIDTH], abre_ref, abim_ref, bbre_ref, bbim_ref, ccre_ref, ccim_ref,
                                     d_ref, wglu_ref, bglu_ref, xr_ref, xi_ref, hre_ref, him_ref,
                                     n_seq=n_seq, lt=w, groups_per_pass=groups_per_pass, unroll=unroll)

    t_idx = _iota((w, 2 * w), 0)
    s_idx = _iota((w, 2 * w), 1)
    mask = (s_idx > t_idx) & (s_idx <= t_idx + w) & (s_idx >= jnp.where(pl.program_id(0) == 0, w, 0))
    q0, k0, v0 = OD_Q_BLOCK * 512, OD_KV_BLOCK * 256, OD_KV_BLOCK * 256 + LANES
    def prepare(s):
        rs = slice(s * w, (s + 1) * w)
        v_cur = proj_sc[rs, v0:v0 + LANES]
        return (v_cur,) + _swa_prepare(proj_sc[rs, q0:q0 + 512], proj_sc[rs, k0:k0 + LANES], v_cur, kprev_sc[s],
                                       vprev_sc[s], qg_ref[...], kg_ref[...], cq_ref[...], sq_ref[...])

    state = {"next": prepare(0)}
    for s in range(n_seq):
        rs = slice(s * w, (s + 1) * w)
        v_cur, *prepared = state["next"]

        def after_head(kh, s=s):
            issue_next_projection(s * SWA_KV_HEADS + kh)
            if kh == 0 and s + 1 < n_seq:
                state["next"] = prepare(s + 1)

        pairs = _swa_attend(prepared, mask, sink_ref, after_head=after_head)
        for i, pair in enumerate(pairs):
            mix_sc[rs, S5_WIDTH + i * LANES:S5_WIDTH + (i + 1) * LANES] = pair
        kprev_sc[s] = prepared[1]
        vprev_sc[s] = v_cur
    o_ref[...] = (x + _dot(_bf(mix_sc[...]), wout_ref[...])).reshape(o_ref.shape)


def _odd_prompt(x, gain, w_in, w_out, s0_re, s0_im, ab_re, ab_im, bb_re, bb_im, cc_re, cc_im, d_row, wglu, bglu,
                cq, sq, qg, kg, sink_rows, *, groups_per_pass, unroll):
    b, l, d = x.shape
    w = SWA_WINDOW
    pack, n_groups, pitch = _s5_layout(b, w)
    ab_re, ab_im = _s5_tables(ab_re, b, w), _s5_tables(ab_im, b, w)
    full = lambda a: pl.BlockSpec(a.shape, lambda j: (0,) * a.ndim)
    io_spec = pl.BlockSpec((b, w, d), lambda j: (0, j, 0))
    n_steps = l // w
    next_spec = pl.BlockSpec((b, w, d), lambda j: (0, jnp.minimum(j + 1, n_steps - 1), 0))
    tab_spec = pl.BlockSpec((w, LANES), lambda j: (j, 0))
    cache_shape = jax.ShapeDtypeStruct((b, w, LANES), F32)
    x_scratch = pltpu.VMEM((n_groups, pack * b * pitch, LANES), F32)
    h_scratch = pltpu.VMEM((n_groups, pack * b, LANES), F32)
    kv_scratch = pltpu.VMEM((b, w, LANES), F32)
    return pl.pallas_call(
        functools.partial(_odd_prompt_kernel, n_seq=b, groups_per_pass=groups_per_pass, unroll=unroll),
        grid=(l // w,),
        in_specs=[io_spec, next_spec, full(gain), full(w_in), full(w_out), full(s0_re), full(s0_im), full(ab_re), full(ab_im),
                  full(bb_re), full(bb_im), full(cc_re), full(cc_im), full(d_row), full(wglu), full(bglu),
                  tab_spec, tab_spec, full(qg), full(kg), full(sink_rows)],
        out_specs=[io_spec, full(s0_re), full(s0_im), pl.BlockSpec((b, w, LANES), lambda j: (0, 0, 0)),
                   pl.BlockSpec((b, w, LANES), lambda j: (0, 0, 0))],
        out_shape=[jax.ShapeDtypeStruct(x.shape, F32), jax.ShapeDtypeStruct(s0_re.shape, F32),
                   jax.ShapeDtypeStruct(s0_im.shape, F32), cache_shape, cache_shape],
        scratch_shapes=[pltpu.VMEM((b * w, OD_COLS), F32), pltpu.VMEM((b * w, OD_COLS), F32), pltpu.VMEM((b * w, d), F32),
                        x_scratch, x_scratch, h_scratch, h_scratch, kv_scratch, kv_scratch],
        compiler_params=_params("arbitrary"),
        name="odd_prompt",
    )(x, x, gain, w_in, w_out, s0_re, s0_im, ab_re, ab_im, bb_re, bb_im, cc_re, cc_im, d_row, wglu, bglu,
      cq, sq, qg, kg, sink_rows)


def _swa_sample_kernel(q_ref, kv_ref, ck_ref, cv_ref, cq_ref, sq_ref, qg_ref, kg_ref, sink_ref,
                       o_ref, nk_ref, nv_ref, *, bg, ls):
    w = SWA_WINDOW
    r = bg * ls
    ls_shift = ls.bit_length() - 1
    w_shift = w.bit_length() - 1
    rows_g = SWA_GROUP * r
    v_new = kv_ref[:, LANES:2 * LANES]
    q_pairs, k_new = _swa_qk(q_ref[...], kv_ref[:, 0:LANES], qg_ref[...], kg_ref[...], cq_ref[...], sq_ref[...])
    k_cache = _bf(ck_ref[...].reshape(bg * w, LANES))
    v_cache = _bf(cv_ref[...].reshape(bg * w, LANES))

    row = _iota((rows_g, bg * w), 0) & (r - 1)
    col = _iota((rows_g, bg * w), 1)
    mask_c = ((row >> ls_shift) == (col >> w_shift)) & ((col & (w - 1)) > (row & (ls - 1)))
    row_n = _iota((rows_g, r), 0) & (r - 1)
    col_n = _iota((rows_g, r), 1)
    mask_n = ((row_n >> ls_shift) == (col_n >> ls_shift)) & ((col_n & (ls - 1)) <= (row_n & (ls - 1)))
    for kh in range(SWA_KV_HEADS):
        q_stack = _swa_query_stack(q_pairs, kh)
        s_c = jnp.where(mask_c, _dot_nt(q_stack, k_cache) * (SWA_HD ** -0.5), -jnp.inf)
        s_n = jnp.where(mask_n, _dot_nt(q_stack, _bf(k_new)) * (SWA_HD ** -0.5), -jnp.inf)
        sink = jnp.concatenate([jnp.broadcast_to(sink_ref[kh * SWA_GROUP + g:kh * SWA_GROUP + g + 1, :], (r, LANES))
                                for g in range(SWA_GROUP)], axis=0)
        row_max = jnp.maximum(jnp.max(s_c, axis=-1, keepdims=True), jnp.max(s_n, axis=-1, keepdims=True))
        m = jnp.maximum(jnp.broadcast_to(row_max, (rows_g, LANES)), sink)
        e_c = _bf(jnp.exp(s_c - jnp.concatenate([m] * bg, axis=1)))
        e_n = _bf(jnp.exp(s_n - m[:, :r]))
        den = (_dot(e_c, jnp.ones((bg * w, LANES), BF16)) + _dot(e_n, jnp.ones((r, LANES), BF16))
               + jnp.exp(sink - m))
        o = (_dot(e_c, v_cache) + _dot(e_n, _bf(v_new))) / den
        for i, pair in enumerate(_swa_merge_heads(o, kh, r)):
            col_i = kh * (SWA_GROUP // 2) + i
            o_ref[:, col_i * LANES:(col_i + 1) * LANES] = pair

    nk_ref[:, 0:w - ls, :] = ck_ref[:, ls:w, :]
    nv_ref[:, 0:w - ls, :] = cv_ref[:, ls:w, :]
    for b in range(bg):
        nk_ref[b, w - ls:w, :] = k_new[b * ls:(b + 1) * ls, :]
        nv_ref[b, w - ls:w, :] = v_new[b * ls:(b + 1) * ls, :]


def _swa_sample(proj, cache_k, cache_v, cq, sq, qg, kg, sink_rows, *, bg, ls):
    t = proj.shape[0]
    w = SWA_WINDOW
    r = bg * ls
    full = lambda a: pl.BlockSpec(a.shape, lambda i: (0,) * a.ndim)
    cache_spec = pl.BlockSpec((bg, w, LANES), lambda i: (i, 0, 0))
    return pl.pallas_call(
        functools.partial(_swa_sample_kernel, bg=bg, ls=ls),
        grid=(t // r,),
        in_specs=[
            pl.BlockSpec((r, 512), lambda i: (i, OD_Q_BLOCK)),
            pl.BlockSpec((r, 256), lambda i: (i, OD_KV_BLOCK)),
            cache_spec, cache_spec, full(cq), full(sq), full(qg), full(kg), full(sink_rows),
        ],
        out_specs=[pl.BlockSpec((r, 512), lambda i: (i, 0)), cache_spec, cache_spec],
        out_shape=[jax.ShapeDtypeStruct((t, 512), F32),
                   jax.ShapeDtypeStruct(cache_k.shape, F32), jax.ShapeDtypeStruct(cache_v.shape, F32)],
        compiler_params=_params("parallel"),
        name="swa_sample",
    )(proj, proj, cache_k, cache_v, cq, sq, qg, kg, sink_rows)


def _mem_prompt_kernel(x_ref, g_ref, wq_ref, qg_ref, k_ref, v_ref, wo_ref, o_ref):
    x = x_ref[0]
    xn = _bf(_rms(x, g_ref[...]))
    k_all = _bf(_mem_rows(k_ref, 0))
    v_all = _bf(_mem_rows(v_ref, 0))
    heads = [slice(h * MEM_HD, (h + 1) * MEM_HD) for h in range(MEM_HEADS)]

    def q_head(h):
        return _bf(_rms(_dot(xn, wq_ref[:, heads[h]]), qg_ref[...]))

    def scores(q, h):
        return _dot_nt(q, k_all[:, heads[h]]) * (MEM_HD ** -0.5)

    acc = x
    q_next = q_head(1)
    s_next = scores(q_head(0), 0)
    o_prev = None
    for h in range(MEM_HEADS):
        s = s_next
        if h + 1 < MEM_HEADS:
            s_next = scores(q_next, h + 1)
        if h + 2 < MEM_HEADS:
            q_next = q_head(h + 2)
        e = jnp.exp(s - jnp.max(s, axis=-1, keepdims=True))
        p = e / jnp.sum(e, axis=-1, keepdims=True)
        if o_prev is not None:
            acc = acc + _dot(_bf(o_prev), wo_ref[heads[h - 1], :])
        o_prev = _dot(_bf(p), v_all[:, heads[h]])
    o_ref[0] = acc + _dot(_bf(o_prev), wo_ref[heads[MEM_HEADS - 1], :])


def _mem_prompt(x, gain, wq, q_gain, k, v, wo, *, layer, lt):
    b, l, d = x.shape
    k, v = _mem_flat_view(k), _mem_flat_view(v)
    kv_spec = pl.BlockSpec((None, 1, k.shape[2], LANES), lambda i, j: (layer, i, 0, 0))
    io_spec = pl.BlockSpec((1, lt, d), lambda i, j: (i, j, 0))
    full = lambda a: pl.BlockSpec(a.shape, lambda i, j: (0,) * a.ndim)
    resident = lambda a: pl.BlockSpec(a.shape, lambda i, j: (0,) * a.ndim, pipeline_mode=pl.Buffered(1))
    return pl.pallas_call(
        _mem_prompt_kernel,
        grid=(b, l // lt),
        in_specs=[io_spec, full(gain), resident(wq), full(q_gain), kv_spec, kv_spec, resident(wo)],
        out_specs=io_spec,
        out_shape=jax.ShapeDtypeStruct(x.shape, F32),
        compiler_params=_params("parallel", "arbitrary"),
        name="mem_prompt",
    )(x, gain, wq, q_gain, k, v, wo)


def _memkv_kernel(mem_ref, g_ref, wk_ref, wv_ref, kg_ref, k_out, v_out):
    xn = _bf(_rms(mem_ref[...], g_ref[...]))
    nb, n_mem, heads, hd = k_out.shape
    for h in range(heads):
        sl = slice(h * hd, (h + 1) * hd)
        k_out[:, :, h, :] = _rms(_dot(xn, wk_ref[:, sl]), kg_ref[...]).reshape(nb, n_mem, hd)
        v_out[:, :, h, :] = _dot(xn, wv_ref[:, sl]).reshape(nb, n_mem, hd)


def _memkv(mem, m_gain, wk, wv, k_gain):
    nb, n_mem, d = mem.shape
    depth = wk.shape[0]
    out_shape = jax.ShapeDtypeStruct((depth, nb, n_mem, MEM_HEADS, MEM_HD), F32)
    per_layer = lambda s: pl.BlockSpec((None,) + s, lambda l: (l,) + (0,) * len(s))
    return pl.pallas_call(
        _memkv_kernel,
        grid=(depth,),
        in_specs=[pl.BlockSpec((nb * n_mem, d), lambda l: (0, 0)), per_layer((1, d)), per_layer((d, d)),
                  per_layer((d, d)), per_layer((1, MEM_HD))],
        out_specs=[per_layer((nb, n_mem, MEM_HEADS, MEM_HD)), per_layer((nb, n_mem, MEM_HEADS, MEM_HD))],
        out_shape=[out_shape, out_shape],
        compiler_params=_params("arbitrary"),
        name="memkv",
    )(mem.reshape(nb * n_mem, d), m_gain, wk, wv, k_gain)


MEM_RING = 3


def _mem_sample_ring_kernel(q_ref, k_hbm, v_hbm, o_ref, k_ring, v_ring, sem, *, bs, ls, layer, n_steps):
    i = pl.program_id(0)
    ahead = MEM_RING - 1

    def copies(step, slot):
        rows = pl.ds(step * bs, bs)
        return (pltpu.make_async_copy(k_hbm.at[layer, rows], k_ring.at[slot], sem.at[0, slot]),
                pltpu.make_async_copy(v_hbm.at[layer, rows], v_ring.at[slot], sem.at[1, slot]))

    @pl.when(i == 0)
    def _():
        for step in range(min(ahead, n_steps)):
            for cp in copies(step, step):
                cp.start()

    @pl.when(i + ahead < n_steps)
    def _():
        for cp in copies(i + ahead, (i + ahead) % MEM_RING):
            cp.start()

    slot = i % MEM_RING
    for cp in copies(i, slot):
        cp.wait()
    _mem_sample_kernel(q_ref, k_ring.at[slot], v_ring.at[slot], o_ref, bs=bs, ls=ls)


def _mem_sample_kernel(q_ref, k_ref, v_ref, o_ref, *, bs, ls):
    r = bs * ls
    per_seq = MEM_HEADS * ls
    n_exp = bs * per_seq
    n_mem = k_ref.shape[1] // MEM_ROW_GROUP
    hd_shift = MEM_HD.bit_length() - 1
    ls_shift = ls.bit_length() - 1
    seq_shift = per_seq.bit_length() - 1
    mem_shift = n_mem.bit_length() - 1
    qb = _bf(q_ref[...])
    e_row = _iota((n_exp, r), 0)
    sel = _bf((((e_row >> seq_shift) << ls_shift) + (e_row & (ls - 1)) == _iota((n_exp, r), 1)).astype(F32))
    head_mask = (((_iota((n_exp, D_MODEL), 0) >> ls_shift) & (MEM_HEADS - 1))
                 == (_iota((n_exp, D_MODEL), 1) >> hd_shift))
    q_exp = _bf(jnp.where(head_mask, _dot(sel, qb), 0.0))
    k_all = _bf(jnp.concatenate([_mem_rows(k_ref, b) for b in range(bs)], axis=0))
    v_all = _bf(jnp.concatenate([_mem_rows(v_ref, b) for b in range(bs)], axis=0))
    own = ((_iota((bs * n_mem, n_exp), 0) >> mem_shift) == (_iota((bs * n_mem, n_exp), 1) >> seq_shift))
    own = own.reshape(bs, n_mem, n_exp)
    s = (_dot_nt(k_all, q_exp) * (MEM_HD ** -0.5)).reshape(bs, n_mem, n_exp)
    s = jnp.where(own, s, -1e30)
    e = jnp.where(own, jnp.exp(s - jnp.max(s, axis=1, keepdims=True)), 0.0)
    den = jnp.sum(e, axis=1, keepdims=True) + jnp.where(jnp.any(own, axis=1, keepdims=True), 0.0, 1.0)
    p = _bf((e / den).reshape(bs * n_mem, n_exp))
    o_all = jnp.where(head_mask, _dot_tn(p, v_all), 0.0)
    o_ref[...] = _dot_tn(sel, _bf(o_all))


MEM_LANE_TILES = MEM_HD // LANES
MEM_ROW_GROUP = MEM_HEADS * MEM_LANE_TILES


def _mem_rows(ref, b):
    n_mem = ref.shape[1] // MEM_ROW_GROUP
    return jnp.concatenate([ref[b, pl.ds(lt * MEM_HEADS + h, n_mem, stride=MEM_ROW_GROUP), :]
                            for h in range(MEM_HEADS) for lt in range(MEM_LANE_TILES)], axis=1)


def _mem_flat_view(a):
    depth, nb, n_mem, heads, hd = a.shape
    a = a.reshape(depth, nb, n_mem, heads, hd // LANES, LANES).transpose(0, 1, 2, 4, 3, 5)
    return a.reshape(depth, nb, n_mem * MEM_ROW_GROUP, LANES)


def _mem_sample(q, k, v, *, layer, bs, ls):
    t, d = q.shape
    k, v = _mem_flat_view(k), _mem_flat_view(v)
    r = bs * ls
    n_steps = t // r
    hbm_spec = pl.BlockSpec(memory_space=pl.ANY)
    io_spec = pl.BlockSpec((r, d), lambda i: (i, 0))
    ring = pltpu.VMEM((MEM_RING, bs, k.shape[2], LANES), F32)
    return pl.pallas_call(
        functools.partial(_mem_sample_ring_kernel, bs=bs, ls=ls, layer=layer, n_steps=n_steps),
        grid=(n_steps,),
        in_specs=[io_spec, hbm_spec, hbm_spec],
        out_specs=io_spec,
        out_shape=jax.ShapeDtypeStruct(q.shape, F32),
        scratch_shapes=[ring, ring, pltpu.SemaphoreType.DMA((2, MEM_RING))],
        compiler_params=_params("arbitrary"),
        name="mem_sample",
    )(q, k, v)


def _retention_tables(pos):
    inv = 1.0 / (RET_THETA ** jnp.linspace(0.0, 1.0, RET_DK // 2, dtype=F32))
    ang = pos[:, None] * inv[None, :]
    cos = jnp.repeat(jnp.cos(ang), 2, axis=1)
    sin = jnp.stack([-jnp.sin(ang), jnp.sin(ang)], axis=-1).reshape(pos.shape[0], RET_DK)
    return jnp.tile(cos, (1, PAIR)), jnp.tile(sin, (1, PAIR))


def _rope_tables(pos):
    half = ROPE_HALF
    inv = 1.0 / (ROPE_THETA ** (jnp.arange(half, dtype=F32) * 2.0 / (2 * half)))
    ang = pos[:, None] * inv[None, :]
    n = pos.shape[0]
    rest = SWA_HD - 2 * half
    cos = jnp.concatenate([jnp.cos(ang), jnp.cos(ang), jnp.ones((n, rest), F32)], axis=1)
    sin = jnp.concatenate([-jnp.sin(ang), jnp.sin(ang), jnp.zeros((n, rest), F32)], axis=1)
    return jnp.tile(cos, (1, PAIR)), jnp.tile(sin, (1, PAIR))


def _block_diag(t):
    g, a, b = t.shape
    eye = jnp.eye(g, dtype=t.dtype)
    return (t[:, :, None, :] * eye[:, None, :, None]).reshape(g * a, g * b)


def _half_block_diag(t):
    per = t.shape[0] // S5_HALVES
    return _bf(jnp.stack([_block_diag(t[h * per:(h + 1) * per]) for h in range(S5_HALVES)]))


def _sink_rows(sinks):
    return jnp.broadcast_to(sinks.astype(F32)[:, None], (sinks.shape[0], LANES))


TILES = dict(
    even_prompt_rows=512,
    mem_prompt_rows=1024,
    sample_rows=512,
    proj_cols=640,
    even_sample_seqs=16,
    swa_sample_seqs=8,
    mem_sample_seqs=4,
    s5_groups_per_pass=8,
    s5_unroll=4,
)
def _trunk(x3, pos0, states, mem_k, mem_v, w, *, sample):
    b, l, d = x3.shape
    t = b * l
    x = x3.reshape(t, d)
    pos = pos0 + jnp.arange(l, dtype=F32)
    tm = TILES['sample_rows']
    gla_s, ret_s, s5_re, s5_im, swa_k, swa_v = states
    out_states = {k: [] for k in ("gla", "ret", "s5_re", "s5_im", "swa_k", "swa_v")}
    ld_row = jnp.repeat(jnp.log(1.0 - 2.0 ** (-5.0 - jnp.arange(RET_HEADS, dtype=F32))), RET_DK)[None, :]

    for layer in range(2):
        i = layer // 2
        x = _ffn(x, w['ffn1_norm'][layer][None], w['ffn1_w_gate'], w['ffn1_w_up'], w['ffn1_w_down'],
                 layer=layer)
        if layer % 2 == 0:
            cos, sin = _retention_tables(pos)
            args = (ld_row, w['gla_w_gate'][i], w['gla_b_gate'][i], w['gla_out_norm'][i])
            if sample:
                bg = TILES['even_sample_seqs']
                proj = _norm_matmul(x, w['mix_norm'][layer][None], w['even_w_in'][i], tm=tm, tn=TILES['proj_cols'])
                mixed, g_s, r_s = _even_sample(proj, jnp.tile(cos, (bg, 1)), jnp.tile(sin, (bg, 1)), *args,
                                               gla_s[i], ret_s[i], bg=bg, ls=l)
                x = _matmul_residual(x, [(mixed, w['even_w_out'][i])], tm=tm)
            else:
                x, g_s, r_s = _even_prompt(x.reshape(b, l, d), w['mix_norm'][layer][None], w['even_w_in'][i],
                                           w['even_w_out'][i], cos, sin, *args, gla_s[i], ret_s[i], lt=TILES['even_prompt_rows'])
                x = x.reshape(t, d)
            out_states["gla"].append(g_s)
            out_states["ret"].append(r_s)
        else:
            cq, sq = _rope_tables(pos)
            qg = jnp.tile(w['swa_q_norm'][i], SWA_HEADS)[None, :]
            kg = jnp.tile(w['swa_k_norm'][i], SWA_KV_HEADS)[None, :]
            s5_args = (w['s5_ab_re'][i], w['s5_ab_im'][i], w['s5_bb_re'][i], w['s5_bb_im'][i],
                       w['s5_cc_re'][i], w['s5_cc_im'][i], w['s5_d'][i][None], w['s5_w_glu'][i], w['s5_b_glu'][i][None])
            n_state = S5_GROUPS * S5_STATE
            s0_re, s0_im = s5_re[i].reshape(b, n_state), s5_im[i].reshape(b, n_state)
            sinks = _sink_rows(w['swa_sinks'][i])
            if sample:
                proj = _norm_matmul(x, w['mix_norm'][layer][None], w['odd_w_in'][i], tm=tm, tn=TILES['proj_cols'])
                c_out, sr, si = _s5(proj.reshape(1, t, OD_COLS), s0_re, s0_im, *s5_args, n_seq=b, lt=l,
                                    groups_per_pass=1, unroll=True)
                bg = TILES['swa_sample_seqs']
                d_out, kb, vb = _swa_sample(proj, swa_k[i].reshape(b, SWA_WINDOW, LANES),
                                            swa_v[i].reshape(b, SWA_WINDOW, LANES),
                                            jnp.tile(cq, (bg, 1)), jnp.tile(sq, (bg, 1)), qg, kg, sinks, bg=bg, ls=l)
                w_out = w['odd_w_out'][i]
                x = _matmul_residual(x, [(c_out.reshape(t, S5_WIDTH), w_out[:S5_WIDTH]), (d_out, w_out[S5_WIDTH:])],
                                     tm=tm)
            else:
                x, sr, si, kb, vb = _odd_prompt(x.reshape(b, l, d), w['mix_norm'][layer][None], w['odd_w_in'][i],
                                                w['odd_w_out'][i], s0_re, s0_im, *s5_args, cq, sq, qg, kg, sinks,
                                                groups_per_pass=TILES['s5_groups_per_pass'], unroll=TILES['s5_unroll'])
                x = x.reshape(t, d)
            out_states["s5_re"].append(sr.reshape(b, S5_GROUPS, S5_STATE))
            out_states["s5_im"].append(si.reshape(b, S5_GROUPS, S5_STATE))
            out_states["swa_k"].append(kb.reshape(b, -1, SWA_KV_HEADS, SWA_HD))
            out_states["swa_v"].append(vb.reshape(b, -1, SWA_KV_HEADS, SWA_HD))
        if sample:
            q = _norm_matmul(x, w['mem_x_norm'][layer][None], w['mem_w_q'][layer], tm=tm, tn=MEM_HD,
                             head_gain=w['mem_q_norm'][layer][None], n_norm_tiles=MEM_HEADS)
            o = _mem_sample(q, mem_k, mem_v, layer=layer, bs=TILES['mem_sample_seqs'], ls=l)
            x = _matmul_residual(x, [(o, w['mem_w_o'][layer])], tm=tm)
        else:
            x = _mem_prompt(x.reshape(b, l, d), w['mem_x_norm'][layer][None], w['mem_w_q'][layer],
                            w['mem_q_norm'][layer][None], mem_k, mem_v, w['mem_w_o'][layer],
                            layer=layer, lt=TILES['mem_prompt_rows']).reshape(t, d)
        x = _ffn(x, w['ffn2_norm'][layer][None], w['ffn2_w_gate'], w['ffn2_w_up'], w['ffn2_w_down'],
                 layer=layer)
    return x.reshape(b, l, d), {k: jnp.stack(v) for k, v in out_states.items()}


def kernel(x_prompt, x_sample, mem_prompt, state_gla, state_ret, state_s5_re, state_s5_im, cache_swa_k, cache_swa_v, cache_mem_k, cache_mem_v, ffn1_norm, ffn1_w_gate, ffn1_w_up, ffn1_w_down, ffn2_norm, ffn2_w_gate, ffn2_w_up, ffn2_w_down, mix_norm, even_w_in, gla_w_gate, gla_b_gate, gla_out_norm, even_w_out, odd_w_in, s5_a_re, s5_a_im, s5_log_step, s5_b_re, s5_b_im, s5_c_re, s5_c_im, s5_d, s5_w_glu, s5_b_glu, swa_q_norm, swa_k_norm, swa_sinks, odd_w_out, mem_x_norm, mem_m_norm, mem_w_q, mem_w_k, mem_w_v, mem_w_o, mem_q_norm, mem_k_norm):
    depth = ffn1_norm.shape[0]
    n_even, n_odd = even_w_in.shape[0], odd_w_in.shape[0]
    batch, seq, d = x_prompt.shape
    dec_batch = x_sample.shape[0]
    n_mem = mem_prompt.shape[1]

    ev = even_w_in
    ev_cols = jnp.concatenate(
        [ev[..., 0:1536], ev[..., 1552:3088], ev[..., 1536:1552],
         jnp.zeros(ev.shape[:2] + (EV_COLS - 3088,), ev.dtype)], axis=-1)
    wgate_pad = jnp.concatenate(
        [gla_w_gate, jnp.zeros((n_even, EV_COLS - EV_GA - GLA_RANK, gla_w_gate.shape[-1]), gla_w_gate.dtype)], axis=1)
    w = dict(
        ffn1_norm=ffn1_norm, ffn2_norm=ffn2_norm, mix_norm=mix_norm, mem_x_norm=mem_x_norm,
        ffn1_w_gate=_bf(ffn1_w_gate), ffn1_w_up=_bf(ffn1_w_up), ffn1_w_down=_bf(ffn1_w_down),
        ffn2_w_gate=_bf(ffn2_w_gate), ffn2_w_up=_bf(ffn2_w_up), ffn2_w_down=_bf(ffn2_w_down),
        even_w_in=_bf(ev_cols), gla_w_gate=_bf(wgate_pad), gla_b_gate=gla_b_gate[:, None, :],
        gla_out_norm=gla_out_norm[:, None, :], even_w_out=_bf(even_w_out),
        odd_w_in=_bf(odd_w_in), odd_w_out=_bf(odd_w_out), s5_d=s5_d, s5_w_glu=_bf(s5_w_glu), s5_b_glu=s5_b_glu,
        swa_q_norm=swa_q_norm, swa_k_norm=swa_k_norm, swa_sinks=swa_sinks,
        mem_w_q=_bf(mem_w_q), mem_w_o=_bf(mem_w_o), mem_q_norm=mem_q_norm,
    )
    ab_re, ab_im, bb_re, bb_im = [], [], [], []
    for i in range(n_odd):
        a_r, a_i, b_r, b_i = _s5_prep(s5_a_re[i], s5_a_im[i], s5_log_step[i], s5_b_re[i], s5_b_im[i])
        ab_re.append(a_r.reshape(-1))
        ab_im.append(a_i.reshape(-1))
        bb_re.append(_half_block_diag(b_r))
        bb_im.append(_half_block_diag(b_i))
    w.update(s5_ab_re=ab_re, s5_ab_im=ab_im, s5_bb_re=bb_re, s5_bb_im=bb_im,
             s5_cc_re=[_half_block_diag(jnp.swapaxes(s5_c_re[i], 1, 2)) for i in range(n_odd)],
             s5_cc_im=[_half_block_diag(jnp.swapaxes(s5_c_im[i], 1, 2)) for i in range(n_odd)])

    p_mem_k, p_mem_v = _memkv(mem_prompt, mem_m_norm[:, None, :], _bf(mem_w_k), _bf(mem_w_v), mem_k_norm[:, None, :])

    zeros = lambda *s: jnp.zeros(s, F32)
    p_states = (zeros(n_even, batch, GLA_HEADS, GLA_DK, GLA_DV), zeros(n_even, batch, RET_HEADS, RET_DK, GLA_DV),
                zeros(n_odd, batch, S5_GROUPS, S5_STATE), zeros(n_odd, batch, S5_GROUPS, S5_STATE), None, None)
    y_prompt, ps = _trunk(x_prompt, 0.0, p_states, p_mem_k, p_mem_v, w, sample=False)

    s_states = (state_gla, state_ret, state_s5_re, state_s5_im, cache_swa_k, cache_swa_v)
    y_sample, ss = _trunk(x_sample, float(PAST_LEN), s_states, cache_mem_k, cache_mem_v, w, sample=True)

    return (y_prompt, y_sample, ps["gla"], ps["ret"], ps["s5_re"], ps["s5_im"], ps["swa_k"], ps["swa_v"],
            p_mem_k, p_mem_v, ss["gla"], ss["ret"], ss["s5_re"], ss["s5_im"], ss["swa_k"], ss["swa_v"])
```
